```python
import jax, jax.numpy as jnp
from jax import lax
import numpy as np

D_MODEL = 2048
BATCH = 8
SEQ = 8192
DEPTH = 1

MEM_LEN = 256
HEAD_DIM = 128
H_HGRN = 6
H_FOX = 6
H_MEM = 4
W_HGRN = H_HGRN * HEAD_DIM
W_FOX = H_FOX * HEAD_DIM
W_MEM = H_MEM * HEAD_DIM
D_FF = 128 * ((8 * D_MODEL // 3 + 127) // 128)
CHUNK = 64
Q_BLOCK = 128
N_BRANCH = 3
EPS = 1e-6
IN_SIZES = (W_HGRN, W_HGRN, W_HGRN, W_HGRN,
            W_FOX, W_FOX, W_FOX, H_FOX,
            W_MEM)
IN_COLS = sum(IN_SIZES)
IN_SPLITS = tuple(int(s) for s in np.cumsum(IN_SIZES)[:-1])

kernel_name = "hybrid_hgrn2_fox_mem_macaron"


def rms_norm(x, g):
    xf = x.astype(jnp.float32)
    y = xf * lax.rsqrt(jnp.mean(xf * xf, axis=-1, keepdims=True) + EPS)
    return (y * g.astype(jnp.float32)).astype(x.dtype)


def swiglu(x, w_gate, w_up, w_down):
    return (jax.nn.silu(x @ w_gate) * (x @ w_up)) @ w_down


def split_heads(a, n):
    b, t, _ = a.shape
    return a.reshape(b, t, n, -1).transpose(0, 2, 1, 3)


def merge_heads(a):
    b, h, t, d = a.shape
    return a.transpose(0, 2, 1, 3).reshape(b, t, h * d)


def hgrn2_chunk_scan(q, k, v, log_f):
    bsz, h, t, dk = q.shape
    dv = v.shape[-1]
    n = t // CHUNK

    def to_chunks(a):
        return jnp.moveaxis(a.reshape(bsz, h, n, CHUNK, a.shape[-1]), 2, 0)

    qc, kc, vc, gc = to_chunks(q), to_chunks(k), to_chunks(v), to_chunks(log_f)
    causal = jnp.tril(jnp.ones((CHUNK, CHUNK), dtype=bool))[:, :, None]

    def step(state, inp):
        qb, kb, vb, gb = inp
        bcum = jnp.cumsum(gb, axis=2)
        o_inter = jnp.einsum('bhtk,bhkv->bhtv', qb * jnp.exp(bcum), state)
        diff = bcum[:, :, :, None, :] - bcum[:, :, None, :, :]
        decay = jnp.exp(jnp.where(causal, diff, -jnp.inf))
        scores = jnp.einsum('bhtk,bhtsk->bhts', qb, decay * kb[:, :, None, :, :])
        o_intra = jnp.einsum('bhts,bhsv->bhtv', scores, vb)
        last = bcum[:, :, -1:, :]
        k_to_end = kb * jnp.exp(last - bcum)
        new_state = jnp.exp(last[:, :, 0, :])[..., None] * state + \
            jnp.einsum('bhsk,bhsv->bhkv', k_to_end, vb)
        return new_state, o_inter + o_intra

    s0 = jnp.zeros((bsz, h, dk, dv), jnp.float32)
    _, o = lax.scan(step, s0, (qc, kc, vc, gc))
    return jnp.moveaxis(o, 0, 2).reshape(bsz, h, t, dv)


def forgetting_attention(q, k, v, log_f):
    bsz, h, t, d = q.shape
    c = jnp.cumsum(log_f, axis=-1)
    nb = t // Q_BLOCK
    q_blocks = jnp.moveaxis(q.reshape(bsz, h, nb, Q_BLOCK, d), 2, 0)
    c_blocks = jnp.moveaxis(c.reshape(bsz, h, nb, Q_BLOCK), 2, 0)
    starts = jnp.arange(nb, dtype=jnp.int32) * Q_BLOCK
    key_pos = jnp.arange(t, dtype=jnp.int32)
    scale = d ** -0.5

    def block(args):
        q_blk, c_blk, start = args
        s = jnp.einsum('bhqd,bhkd->bhqk', q_blk, k).astype(jnp.float32) * scale
        s = s + c_blk[..., None] - c[:, :, None, :]
        q_pos = start + jnp.arange(Q_BLOCK, dtype=jnp.int32)
        s = jnp.where(key_pos[None, :] <= q_pos[:, None], s, -jnp.inf)
        p = jax.nn.softmax(s, axis=-1).astype(v.dtype)
        return jnp.einsum('bhqk,bhkd->bhqd', p, v)

    o = lax.map(block, (q_blocks, c_blocks, starts))
    return jnp.moveaxis(o, 0, 2).reshape(bsz, h, t, d)


def memory_attention(q, mem_k, mem_v):
    s = jnp.einsum('bhtd,bhmd->bhtm', q, mem_k).astype(jnp.float32) * (HEAD_DIM ** -0.5)
    p = jax.nn.softmax(s, axis=-1).astype(mem_v.dtype)
    return jnp.einsum('bhtm,bhmd->bhtd', p, mem_v)


def hybrid_mixer(u, mem, mem_g, w_in, lb, hgrn_g, fox_b, w_mem_kv,
                 w_hgrn_out, w_fox_out, w_mem_out, w_gate, w_o):
    bsz, t, _ = u.shape
    proj = u @ w_in
    (hq, hf, hi, hog, fq, fk, fv, ff, mq) = jnp.split(proj, IN_SPLITS, axis=-1)

    lbf = lb.astype(jnp.float32)
    g = lbf + (1.0 - lbf) * jax.nn.sigmoid(hf.astype(jnp.float32))
    log_g = jnp.log(g)
    o_h = hgrn2_chunk_scan(split_heads(jax.nn.silu(hq).astype(jnp.float32), H_HGRN),
                           split_heads(1.0 - g, H_HGRN),
                           split_heads(hi.astype(jnp.float32), H_HGRN),
                           split_heads(log_g, H_HGRN))
    o_h = o_h * lax.rsqrt(jnp.mean(o_h * o_h, axis=-1, keepdims=True) + EPS)
    o_h = o_h * hgrn_g.astype(jnp.float32).reshape(H_HGRN, HEAD_DIM)[None, :, None, :]
    o_h = merge_heads(o_h).astype(u.dtype) * jax.nn.silu(hog)

    fox_logf = jax.nn.log_sigmoid(ff.astype(jnp.float32) + fox_b.astype(jnp.float32))
    o_f = forgetting_attention(split_heads(fq, H_FOX), split_heads(fk, H_FOX),
                               split_heads(fv, H_FOX), fox_logf.transpose(0, 2, 1))
    o_f = merge_heads(o_f)

    mem_kv = rms_norm(mem, mem_g) @ w_mem_kv
    mk, mv = jnp.split(mem_kv, 2, axis=-1)
    o_m = merge_heads(memory_attention(split_heads(mq, H_MEM),
                                       split_heads(mk, H_MEM), split_heads(mv, H_MEM)))

    gates = jax.nn.sigmoid(u @ w_gate).reshape(bsz, t, N_BRANCH, D_MODEL)
    merged = gates[:, :, 0] * (o_h @ w_hgrn_out) + \
        gates[:, :, 1] * (o_f @ w_fox_out) + \
        gates[:, :, 2] * (o_m @ w_mem_out)
    return merged @ w_o


def _fwd_setup_inputs(seed: int = 0) -> dict:
    key = jax.random.key(seed)
    ks = jax.random.split(key, 32)
    L = DEPTH

    def dense(k, fan_in, fan_out):
        return jax.random.normal(k, (L, fan_in, fan_out), jnp.float32) * fan_in ** -0.5

    def gain(k, n):
        return 1.0 + 0.05 * jax.random.normal(k, (L, n), jnp.float32)

    return {
        "x": jax.random.normal(ks[0], (BATCH, SEQ, D_MODEL), jnp.float32),
        "mem": jax.random.normal(ks[1], (BATCH, MEM_LEN, D_MODEL), jnp.float32),
        "ffn1_pre": gain(ks[2], D_MODEL),
        "ffn1_post": gain(ks[3], D_MODEL),
        "ffn1_wg": dense(ks[4], D_MODEL, D_FF),
        "ffn1_wu": dense(ks[5], D_MODEL, D_FF),
        "ffn1_wd": dense(ks[6], D_FF, D_MODEL),
        "mix_pre": gain(ks[7], D_MODEL),
        "mix_post": gain(ks[8], D_MODEL),
        "mem_norm": gain(ks[9], D_MODEL),
        "w_in": dense(ks[10], D_MODEL, IN_COLS),
        "hgrn_lb": 0.1 * jax.random.normal(ks[11], (DEPTH + 1, W_HGRN), jnp.float32),
        "hgrn_gnorm": gain(ks[12], W_HGRN),
        "fox_fb": 1.0 + 0.1 * jax.random.normal(ks[13], (L, H_FOX), jnp.float32),
        "w_mem_kv": dense(ks[14], D_MODEL, 2 * W_MEM),
        "w_hgrn_out": dense(ks[15], W_HGRN, D_MODEL),
        "w_fox_out": dense(ks[16], W_FOX, D_MODEL),
        "w_mem_out": dense(ks[17], W_MEM, D_MODEL),
        "w_gate": dense(ks[18], D_MODEL, N_BRANCH * D_MODEL),
        "w_o": dense(ks[19], D_MODEL, D_MODEL),
        "ffn2_pre": gain(ks[20], D_MODEL),
        "ffn2_post": gain(ks[21], D_MODEL),
        "ffn2_wg": dense(ks[22], D_MODEL, D_FF),
        "ffn2_wu": dense(ks[23], D_MODEL, D_FF),
        "ffn2_wd": dense(ks[24], D_FF, D_MODEL),
    }


def _fwd_reference(x, mem, ffn1_pre, ffn1_post, ffn1_wg, ffn1_wu, ffn1_wd,
              mix_pre, mix_post, mem_norm, w_in, hgrn_lb, hgrn_gnorm, fox_fb,
              w_mem_kv, w_hgrn_out, w_fox_out, w_mem_out, w_gate, w_o,
              ffn2_pre, ffn2_post, ffn2_wg, ffn2_wu, ffn2_wd):
    lb_all = jnp.cumsum(jax.nn.softmax(hgrn_lb.astype(jnp.float32), axis=0), axis=0)
    for l in range(DEPTH):
        h = swiglu(rms_norm(x, ffn1_pre[l]), ffn1_wg[l], ffn1_wu[l], ffn1_wd[l])
        x = x + 0.5 * rms_norm(h, ffn1_post[l])
        m = hybrid_mixer(rms_norm(x, mix_pre[l]), mem, mem_norm[l], w_in[l], lb_all[l],
                         hgrn_gnorm[l], fox_fb[l], w_mem_kv[l], w_hgrn_out[l],
                         w_fox_out[l], w_mem_out[l], w_gate[l], w_o[l])
        x = x + rms_norm(m, mix_post[l])
        h = swiglu(rms_norm(x, ffn2_pre[l]), ffn2_wg[l], ffn2_wu[l], ffn2_wd[l])
        x = x + 0.5 * rms_norm(h, ffn2_post[l])
    return x


import jax as _jax
import jax.numpy as _jnp

TWIN_FORMAT = 'train_step'
FWD_PARAMS = ['x', 'mem', 'ffn1_pre', 'ffn1_post', 'ffn1_wg', 'ffn1_wu', 'ffn1_wd', 'mix_pre', 'mix_post', 'mem_norm', 'w_in', 'hgrn_lb', 'hgrn_gnorm', 'fox_fb', 'w_mem_kv', 'w_hgrn_out', 'w_fox_out', 'w_mem_out', 'w_gate', 'w_o', 'ffn2_pre', 'ffn2_post', 'ffn2_wg', 'ffn2_wu', 'ffn2_wd']
TWIN_WEIGHTS = ['ffn1_pre', 'ffn1_post', 'ffn1_wg', 'ffn1_wu', 'ffn1_wd', 'mix_pre', 'mix_post', 'mem_norm', 'w_in', 'hgrn_lb', 'hgrn_gnorm', 'fox_fb', 'w_mem_kv', 'w_hgrn_out', 'w_fox_out', 'w_mem_out', 'w_gate', 'w_o', 'ffn2_pre', 'ffn2_post', 'ffn2_wg', 'ffn2_wu', 'ffn2_wd']
TWIN_DIFF_INPUT = 'x'
TWIN_INPUTS = ['x', 'mem', 'ffn1_pre', 'ffn1_post', 'ffn1_wg', 'ffn1_wu', 'ffn1_wd', 'mix_pre', 'mix_post', 'mem_norm', 'w_in', 'hgrn_lb', 'hgrn_gnorm', 'fox_fb', 'w_mem_kv', 'w_hgrn_out', 'w_fox_out', 'w_mem_out', 'w_gate', 'w_o', 'ffn2_pre', 'ffn2_post', 'ffn2_wg', 'ffn2_wu', 'ffn2_wd', 'loss_target', 'm_ffn1_pre', 'm_ffn1_post', 'm_ffn1_wg', 'm_ffn1_wu', 'm_ffn1_wd', 'm_mix_pre', 'm_mix_post', 'm_mem_norm', 'm_w_in', 'm_hgrn_lb', 'm_hgrn_gnorm', 'm_fox_fb', 'm_w_mem_kv', 'm_w_hgrn_out', 'm_w_fox_out', 'm_w_mem_out', 'm_w_gate', 'm_w_o', 'm_ffn2_pre', 'm_ffn2_post', 'm_ffn2_wg', 'm_ffn2_wu', 'm_ffn2_wd', 'v_ffn1_pre', 'v_ffn1_post', 'v_ffn1_wg', 'v_ffn1_wu', 'v_ffn1_wd', 'v_mix_pre', 'v_mix_post', 'v_mem_norm', 'v_w_in', 'v_hgrn_lb', 'v_hgrn_gnorm', 'v_fox_fb', 'v_w_mem_kv', 'v_w_hgrn_out', 'v_w_fox_out', 'v_w_mem_out', 'v_w_gate', 'v_w_o', 'v_ffn2_pre', 'v_ffn2_post', 'v_ffn2_wg', 'v_ffn2_wu', 'v_ffn2_wd']
TWIN_OUTPUTS = ['loss', 'grad_x', 'grad_ffn1_pre', 'grad_ffn1_post', 'grad_ffn1_wg', 'grad_ffn1_wu', 'grad_ffn1_wd', 'grad_mix_pre', 'grad_mix_post', 'grad_mem_norm', 'grad_w_in', 'grad_hgrn_lb', 'grad_hgrn_gnorm', 'grad_fox_fb', 'grad_w_mem_kv', 'grad_w_hgrn_out', 'grad_w_fox_out', 'grad_w_mem_out', 'grad_w_gate', 'grad_w_o', 'grad_ffn2_pre', 'grad_ffn2_post', 'grad_ffn2_wg', 'grad_ffn2_wu', 'grad_ffn2_wd', 'delta_ffn1_pre', 'delta_ffn1_post', 'delta_ffn1_wg', 'delta_ffn1_wu', 'delta_ffn1_wd', 'delta_mix_pre', 'delta_mix_post', 'delta_mem_norm', 'delta_w_in', 'delta_hgrn_lb', 'delta_hgrn_gnorm', 'delta_fox_fb', 'delta_w_mem_kv', 'delta_w_hgrn_out', 'delta_w_fox_out', 'delta_w_mem_out', 'delta_w_gate', 'delta_w_o', 'delta_ffn2_pre', 'delta_ffn2_post', 'delta_ffn2_wg', 'delta_ffn2_wu', 'delta_ffn2_wd', 'new_m_ffn1_pre', 'new_m_ffn1_post', 'new_m_ffn1_wg', 'new_m_ffn1_wu', 'new_m_ffn1_wd', 'new_m_mix_pre', 'new_m_mix_post', 'new_m_mem_norm', 'new_m_w_in', 'new_m_hgrn_lb', 'new_m_hgrn_gnorm', 'new_m_fox_fb', 'new_m_w_mem_kv', 'new_m_w_hgrn_out', 'new_m_w_fox_out', 'new_m_w_mem_out', 'new_m_w_gate', 'new_m_w_o', 'new_m_ffn2_pre', 'new_m_ffn2_post', 'new_m_ffn2_wg', 'new_m_ffn2_wu', 'new_m_ffn2_wd', 'new_v_ffn1_pre', 'new_v_ffn1_post', 'new_v_ffn1_wg', 'new_v_ffn1_wu', 'new_v_ffn1_wd', 'new_v_mix_pre', 'new_v_mix_post', 'new_v_mem_norm', 'new_v_w_in', 'new_v_hgrn_lb', 'new_v_hgrn_gnorm', 'new_v_fox_fb', 'new_v_w_mem_kv', 'new_v_w_hgrn_out', 'new_v_w_fox_out', 'new_v_w_mem_out', 'new_v_w_gate', 'new_v_w_o', 'new_v_ffn2_pre', 'new_v_ffn2_post', 'new_v_ffn2_wg', 'new_v_ffn2_wu', 'new_v_ffn2_wd']
TWIN_LEAF_KINDS = {'loss': 'loss', 'grad_x': 'grad_x', 'grad_ffn1_pre': 'grad_w', 'grad_ffn1_post': 'grad_w', 'grad_ffn1_wg': 'grad_w', 'grad_ffn1_wu': 'grad_w', 'grad_ffn1_wd': 'grad_w', 'grad_mix_pre': 'grad_w', 'grad_mix_post': 'grad_w', 'grad_mem_norm': 'grad_w', 'grad_w_in': 'grad_w', 'grad_hgrn_lb': 'grad_w', 'grad_hgrn_gnorm': 'grad_w', 'grad_fox_fb': 'grad_w', 'grad_w_mem_kv': 'grad_w', 'grad_w_hgrn_out': 'grad_w', 'grad_w_fox_out': 'grad_w', 'grad_w_mem_out': 'grad_w', 'grad_w_gate': 'grad_w', 'grad_w_o': 'grad_w', 'grad_ffn2_pre': 'grad_w', 'grad_ffn2_post': 'grad_w', 'grad_ffn2_wg': 'grad_w', 'grad_ffn2_wu': 'grad_w', 'grad_ffn2_wd': 'grad_w', 'delta_ffn1_pre': 'delta_w', 'delta_ffn1_post': 'delta_w', 'delta_ffn1_wg': 'delta_w', 'delta_ffn1_wu': 'delta_w', 'delta_ffn1_wd': 'delta_w', 'delta_mix_pre': 'delta_w', 'delta_mix_post': 'delta_w', 'delta_mem_norm': 'delta_w', 'delta_w_in': 'delta_w', 'delta_hgrn_lb': 'delta_w', 'delta_hgrn_gnorm': 'delta_w', 'delta_fox_fb': 'delta_w', 'delta_w_mem_kv': 'delta_w', 'delta_w_hgrn_out': 'delta_w', 'delta_w_fox_out': 'delta_w', 'delta_w_mem_out': 'delta_w', 'delta_w_gate': 'delta_w', 'delta_w_o': 'delta_w', 'delta_ffn2_pre': 'delta_w', 'delta_ffn2_post': 'delta_w', 'delta_ffn2_wg': 'delta_w', 'delta_ffn2_wu': 'delta_w', 'delta_ffn2_wd': 'delta_w', 'new_m_ffn1_pre': 'new_m', 'new_m_ffn1_post': 'new_m', 'new_m_ffn1_wg': 'new_m', 'new_m_ffn1_wu': 'new_m', 'new_m_ffn1_wd': 'new_m', 'new_m_mix_pre': 'new_m', 'new_m_mix_post': 'new_m', 'new_m_mem_norm': 'new_m', 'new_m_w_in': 'new_m', 'new_m_hgrn_lb': 'new_m', 'new_m_hgrn_gnorm': 'new_m', 'new_m_fox_fb': 'new_m', 'new_m_w_mem_kv': 'new_m', 'new_m_w_hgrn_out': 'new_m', 'new_m_w_fox_out': 'new_m', 'new_m_w_mem_out': 'new_m', 'new_m_w_gate': 'new_m', 'new_m_w_o': 'new_m', 'new_m_ffn2_pre': 'new_m', 'new_m_ffn2_post': 'new_m', 'new_m_ffn2_wg': 'new_m', 'new_m_ffn2_wu': 'new_m', 'new_m_ffn2_wd': 'new_m', 'new_v_ffn1_pre': 'new_v', 'new_v_ffn1_post': 'new_v', 'new_v_ffn1_wg': 'new_v', 'new_v_ffn1_wu': 'new_v', 'new_v_ffn1_wd': 'new_v', 'new_v_mix_pre': 'new_v', 'new_v_mix_post': 'new_v', 'new_v_mem_norm': 'new_v', 'new_v_w_in': 'new_v', 'new_v_hgrn_lb': 'new_v', 'new_v_hgrn_gnorm': 'new_v', 'new_v_fox_fb': 'new_v', 'new_v_w_mem_kv': 'new_v', 'new_v_w_hgrn_out': 'new_v', 'new_v_w_fox_out': 'new_v', 'new_v_w_mem_out': 'new_v', 'new_v_w_gate': 'new_v', 'new_v_w_o': 'new_v', 'new_v_ffn2_pre': 'new_v', 'new_v_ffn2_post': 'new_v', 'new_v_ffn2_wg': 'new_v', 'new_v_ffn2_wu': 'new_v', 'new_v_ffn2_wd': 'new_v'}


def _forward(args):
    return _fwd_reference(*[args[k] for k in FWD_PARAMS])


def _output_shape():
    def fwd():
        inp = _fwd_setup_inputs(0)
        return _fwd_reference(*[inp[k] for k in FWD_PARAMS])
    out = _jax.eval_shape(fwd)
    return out.shape, out.dtype

N_MICROBATCH = 1
ADAM_LR = 0.001
ADAM_B1 = 0.9
ADAM_B2 = 0.999
ADAM_EPS = 1e-08
ADAM_WD = 0.01
ADAM_STEP = 10
PER_EXAMPLE_BATCH_AXIS = {'x': 0, 'mem': 0, 'loss_target': 0}
SHARED_INPUTS = []
_WEIGHT_DTYPES = {'ffn1_pre': _jnp.float32, 'ffn1_post': _jnp.float32, 'ffn1_wg': _jnp.float32, 'ffn1_wu': _jnp.float32, 'ffn1_wd': _jnp.float32, 'mix_pre': _jnp.float32, 'mix_post': _jnp.float32, 'mem_norm': _jnp.float32, 'w_in': _jnp.float32, 'hgrn_lb': _jnp.float32, 'hgrn_gnorm': _jnp.float32, 'fox_fb': _jnp.float32, 'w_mem_kv': _jnp.float32, 'w_hgrn_out': _jnp.float32, 'w_fox_out': _jnp.float32, 'w_mem_out': _jnp.float32, 'w_gate': _jnp.float32, 'w_o': _jnp.float32, 'ffn2_pre': _jnp.float32, 'ffn2_post': _jnp.float32, 'ffn2_wg': _jnp.float32, 'ffn2_wu': _jnp.float32, 'ffn2_wd': _jnp.float32}
MOMENT_SCALE = {'ffn1_pre': 2.846897e-01, 'ffn1_post': 7.954471e+00, 'ffn1_wg': 1.233378e-01, 'ffn1_wu': 1.272628e-01, 'ffn1_wd': 2.109690e-01, 'mix_pre': 3.735370e-01, 'mix_post': 3.210137e+01, 'mem_norm': 5.432855e-02, 'w_in': 2.138786e-01, 'hgrn_lb': 2.619324e-02, 'hgrn_gnorm': 3.256737e-01, 'fox_fb': 5.491770e+00, 'w_mem_kv': 6.904399e-02, 'w_hgrn_out': 1.897528e-01, 'w_fox_out': 2.214247e-01, 'w_mem_out': 3.582578e-02, 'w_gate': 5.124308e-02, 'w_o': 3.020883e-01, 'ffn2_pre': 2.144504e-01, 'ffn2_post': 7.982543e+00, 'ffn2_wg': 7.193784e-02, 'ffn2_wu': 1.051780e-01, 'ffn2_wd': 1.728229e-01}


def _to_microbatches(a, axis):
    t = _jnp.moveaxis(a, axis, 0)
    t = t.reshape((N_MICROBATCH, t.shape[0] // N_MICROBATCH) + t.shape[1:])
    return _jnp.moveaxis(t, 1, axis + 1)


def setup_inputs(seed: int = 0) -> dict:
    inp = _fwd_setup_inputs(seed)
    key = _jax.random.fold_in(_jax.random.key(seed), 7919)
    shape, _ = _output_shape()
    out = dict(inp)
    out["loss_target"] = _jax.random.normal(_jax.random.fold_in(key, 0), shape, _jnp.float32)
    for i, name in enumerate(TWIN_WEIGHTS):
        w = inp[name].astype(_jnp.float32)
        if MOMENT_SCALE is None:
            s = _jnp.sqrt(_jnp.mean(_jnp.square(w)) + 1e-30)
        else:
            s = MOMENT_SCALE[name]
        km, kv = _jax.random.split(_jax.random.fold_in(key, i + 1))
        out[name] = w
        out["m_" + name] = s * _jax.random.normal(km, w.shape, _jnp.float32)
        out["v_" + name] = (s * s) * _jax.random.uniform(kv, w.shape, _jnp.float32, 0.5, 1.5)
    if N_MICROBATCH > 1:
        for name, axis in PER_EXAMPLE_BATCH_AXIS.items():
            out[name] = _to_microbatches(out[name], axis)
    return {'x': out['x'], 'mem': out['mem'], 'ffn1_pre': out['ffn1_pre'], 'ffn1_post': out['ffn1_post'], 'ffn1_wg': out['ffn1_wg'], 'ffn1_wu': out['ffn1_wu'], 'ffn1_wd': out['ffn1_wd'], 'mix_pre': out['mix_pre'], 'mix_post': out['mix_post'], 'mem_norm': out['mem_norm'], 'w_in': out['w_in'], 'hgrn_lb': out['hgrn_lb'], 'hgrn_gnorm': out['hgrn_gnorm'], 'fox_fb': out['fox_fb'], 'w_mem_kv': out['w_mem_kv'], 'w_hgrn_out': out['w_hgrn_out'], 'w_fox_out': out['w_fox_out'], 'w_mem_out': out['w_mem_out'], 'w_gate': out['w_gate'], 'w_o': out['w_o'], 'ffn2_pre': out['ffn2_pre'], 'ffn2_post': out['ffn2_post'], 'ffn2_wg': out['ffn2_wg'], 'ffn2_wu': out['ffn2_wu'], 'ffn2_wd': out['ffn2_wd'], 'loss_target': out['loss_target'], 'm_ffn1_pre': out['m_ffn1_pre'], 'm_ffn1_post': out['m_ffn1_post'], 'm_ffn1_wg': out['m_ffn1_wg'], 'm_ffn1_wu': out['m_ffn1_wu'], 'm_ffn1_wd': out['m_ffn1_wd'], 'm_mix_pre': out['m_mix_pre'], 'm_mix_post': out['m_mix_post'], 'm_mem_norm': out['m_mem_norm'], 'm_w_in': out['m_w_in'], 'm_hgrn_lb': out['m_hgrn_lb'], 'm_hgrn_gnorm': out['m_hgrn_gnorm'], 'm_fox_fb': out['m_fox_fb'], 'm_w_mem_kv': out['m_w_mem_kv'], 'm_w_hgrn_out': out['m_w_hgrn_out'], 'm_w_fox_out': out['m_w_fox_out'], 'm_w_mem_out': out['m_w_mem_out'], 'm_w_gate': out['m_w_gate'], 'm_w_o': out['m_w_o'], 'm_ffn2_pre': out['m_ffn2_pre'], 'm_ffn2_post': out['m_ffn2_post'], 'm_ffn2_wg': out['m_ffn2_wg'], 'm_ffn2_wu': out['m_ffn2_wu'], 'm_ffn2_wd': out['m_ffn2_wd'], 'v_ffn1_pre': out['v_ffn1_pre'], 'v_ffn1_post': out['v_ffn1_post'], 'v_ffn1_wg': out['v_ffn1_wg'], 'v_ffn1_wu': out['v_ffn1_wu'], 'v_ffn1_wd': out['v_ffn1_wd'], 'v_mix_pre': out['v_mix_pre'], 'v_mix_post': out['v_mix_post'], 'v_mem_norm': out['v_mem_norm'], 'v_w_in': out['v_w_in'], 'v_hgrn_lb': out['v_hgrn_lb'], 'v_hgrn_gnorm': out['v_hgrn_gnorm'], 'v_fox_fb': out['v_fox_fb'], 'v_w_mem_kv': out['v_w_mem_kv'], 'v_w_hgrn_out': out['v_w_hgrn_out'], 'v_w_fox_out': out['v_w_fox_out'], 'v_w_mem_out': out['v_w_mem_out'], 'v_w_gate': out['v_w_gate'], 'v_w_o': out['v_w_o'], 'v_ffn2_pre': out['v_ffn2_pre'], 'v_ffn2_post': out['v_ffn2_post'], 'v_ffn2_wg': out['v_ffn2_wg'], 'v_ffn2_wu': out['v_ffn2_wu'], 'v_ffn2_wd': out['v_ffn2_wd']}


def _loss(weights, diff, rest, loss_target):
    with _jax.named_scope("forward"):
        args = {**rest, TWIN_DIFF_INPUT: diff, **{k: w.astype(_WEIGHT_DTYPES[k]) for k, w in weights.items()}}
        y = _forward(args)
    with _jax.named_scope("loss_head"):
        err = _jnp.square(y.astype(_jnp.float32) - loss_target)
        return 0.5 * _jnp.sum(_jnp.mean(err, axis=-1)) if err.ndim else 0.5 * err


def _adamw(w, g, m, v):
    m = ADAM_B1 * m + (1.0 - ADAM_B1) * g
    v = ADAM_B2 * v + (1.0 - ADAM_B2) * _jnp.square(g)
    m_hat = m / (1.0 - ADAM_B1 ** ADAM_STEP)
    v_hat = v / (1.0 - ADAM_B2 ** ADAM_STEP)
    delta = -ADAM_LR * (m_hat / (_jnp.sqrt(v_hat) + ADAM_EPS) + ADAM_WD * w)
    return delta, m, v


def reference(x, mem, ffn1_pre, ffn1_post, ffn1_wg, ffn1_wu, ffn1_wd, mix_pre, mix_post, mem_norm, w_in, hgrn_lb, hgrn_gnorm, fox_fb, w_mem_kv, w_hgrn_out, w_fox_out, w_mem_out, w_gate, w_o, ffn2_pre, ffn2_post, ffn2_wg, ffn2_wu, ffn2_wd, loss_target, m_ffn1_pre, m_ffn1_post, m_ffn1_wg, m_ffn1_wu, m_ffn1_wd, m_mix_pre, m_mix_post, m_mem_norm, m_w_in, m_hgrn_lb, m_hgrn_gnorm, m_fox_fb, m_w_mem_kv, m_w_hgrn_out, m_w_fox_out, m_w_mem_out, m_w_gate, m_w_o, m_ffn2_pre, m_ffn2_post, m_ffn2_wg, m_ffn2_wu, m_ffn2_wd, v_ffn1_pre, v_ffn1_post, v_ffn1_wg, v_ffn1_wu, v_ffn1_wd, v_mix_pre, v_mix_post, v_mem_norm, v_w_in, v_hgrn_lb, v_hgrn_gnorm, v_fox_fb, v_w_mem_kv, v_w_hgrn_out, v_w_fox_out, v_w_mem_out, v_w_gate, v_w_o, v_ffn2_pre, v_ffn2_post, v_ffn2_wg, v_ffn2_wu, v_ffn2_wd):
    given = dict(x=x, mem=mem, ffn1_pre=ffn1_pre, ffn1_post=ffn1_post, ffn1_wg=ffn1_wg, ffn1_wu=ffn1_wu, ffn1_wd=ffn1_wd, mix_pre=mix_pre, mix_post=mix_post, mem_norm=mem_norm, w_in=w_in, hgrn_lb=hgrn_lb, hgrn_gnorm=hgrn_gnorm, fox_fb=fox_fb, w_mem_kv=w_mem_kv, w_hgrn_out=w_hgrn_out, w_fox_out=w_fox_out, w_mem_out=w_mem_out, w_gate=w_gate, w_o=w_o, ffn2_pre=ffn2_pre, ffn2_post=ffn2_post, ffn2_wg=ffn2_wg, ffn2_wu=ffn2_wu, ffn2_wd=ffn2_wd, loss_target=loss_target, m_ffn1_pre=m_ffn1_pre, m_ffn1_post=m_ffn1_post, m_ffn1_wg=m_ffn1_wg, m_ffn1_wu=m_ffn1_wu, m_ffn1_wd=m_ffn1_wd, m_mix_pre=m_mix_pre, m_mix_post=m_mix_post, m_mem_norm=m_mem_norm, m_w_in=m_w_in, m_hgrn_lb=m_hgrn_lb, m_hgrn_gnorm=m_hgrn_gnorm, m_fox_fb=m_fox_fb, m_w_mem_kv=m_w_mem_kv, m_w_hgrn_out=m_w_hgrn_out, m_w_fox_out=m_w_fox_out, m_w_mem_out=m_w_mem_out, m_w_gate=m_w_gate, m_w_o=m_w_o, m_ffn2_pre=m_ffn2_pre, m_ffn2_post=m_ffn2_post, m_ffn2_wg=m_ffn2_wg, m_ffn2_wu=m_ffn2_wu, m_ffn2_wd=m_ffn2_wd, v_ffn1_pre=v_ffn1_pre, v_ffn1_post=v_ffn1_post, v_ffn1_wg=v_ffn1_wg, v_ffn1_wu=v_ffn1_wu, v_ffn1_wd=v_ffn1_wd, v_mix_pre=v_mix_pre, v_mix_post=v_mix_post, v_mem_norm=v_mem_norm, v_w_in=v_w_in, v_hgrn_lb=v_hgrn_lb, v_hgrn_gnorm=v_hgrn_gnorm, v_fox_fb=v_fox_fb, v_w_mem_kv=v_w_mem_kv, v_w_hgrn_out=v_w_hgrn_out, v_w_fox_out=v_w_fox_out, v_w_mem_out=v_w_mem_out, v_w_gate=v_w_gate, v_w_o=v_w_o, v_ffn2_pre=v_ffn2_pre, v_ffn2_post=v_ffn2_post, v_ffn2_wg=v_ffn2_wg, v_ffn2_wu=v_ffn2_wu, v_ffn2_wd=v_ffn2_wd)
    weights = {n: given[n] for n in TWIN_WEIGHTS}
    shared = {n: given[n] for n in SHARED_INPUTS}
    per_example = {n: given[n] for n in ['x', 'mem']}
    grad_fn = _jax.value_and_grad(_loss, argnums=(0, 1))

    def one_microbatch(ex, loss_target):
        ex = dict(ex)
        diff = ex.pop(TWIN_DIFF_INPUT)
        return grad_fn(weights, diff, {**shared, **ex}, loss_target)

    if N_MICROBATCH == 1:
        loss, (grad_w, grad_x) = one_microbatch(per_example, given["loss_target"])
    else:
        def body(carry, xs):
            loss_sum, grad_sum = carry
            l_k, (gw_k, gx_k) = one_microbatch(xs[0], xs[1])
            with _jax.named_scope("update"):
                return (loss_sum + l_k, _jax.tree.map(_jnp.add, grad_sum, gw_k)), gx_k

        init = (_jnp.zeros((), _jnp.float32), _jax.tree.map(_jnp.zeros_like, weights))
        (loss, grad_w), grad_x = _jax.lax.scan(body, init, (per_example, given["loss_target"]))
    with _jax.named_scope("update"):
        delta_w, new_m, new_v = {}, {}, {}
        for n in TWIN_WEIGHTS:
            delta_w[n], new_m[n], new_v[n] = _adamw(weights[n], grad_w[n], given["m_" + n], given["v_" + n])
    return (loss, grad_x, *[grad_w[n] for n in TWIN_WEIGHTS], *[delta_w[n] for n in TWIN_WEIGHTS],
            *[new_m[n] for n in TWIN_WEIGHTS], *[new_v[n] for n in TWIN_WEIGHTS])
```

```python
import functools

import jax
import jax.numpy as jnp
from jax import lax
from jax.experimental import pallas as pl
from jax.experimental.pallas import tpu as pltpu

F32 = jnp.float32
BF16 = jnp.bfloat16
HIGHEST = lax.Precision.HIGHEST

NDEV = 8
D = 2048
F = 5504
FP = 5632
HD = 128
NH = 6
NM = 4
WH = NH * HD
WM = NM * HD
P = 6144
FF_COL = 5376
MQ_COL = 5382
CHUNK = 64
EPS = 1e-6
SCALE = HD ** -0.5
NEG = -1e30
VMEM_LIMIT = 48 * 1024 * 1024

CB_HQ, CB_HF, CB_HI, CB_HOG, CB_FQ, CB_FK, CB_FV, CB_FF, CB_MQ = 0, 6, 12, 18, 24, 30, 36, 42, 43

ADAM_LR, ADAM_B1, ADAM_B2, ADAM_EPS, ADAM_WD, ADAM_STEP = 0.001, 0.9, 0.999, 1e-08, 0.01, 10

NT = (((1,), (1,)), ((), ()))
NN = (((1,), (0,)), ((), ()))
TN = (((0,), (0,)), ((), ()))
MESH = pl.DeviceIdType.MESH


def _params(sem=None, **kw):
    return pltpu.CompilerParams(dimension_semantics=sem, vmem_limit_bytes=VMEM_LIMIT, **kw)


def _tile(n, prefs):
    for p in prefs:
        if p <= n and n % p == 0:
            return p
    return n


def _dot(a, b, dims):
    return lax.dot_general(a.astype(BF16), b.astype(BF16), dims, preferred_element_type=F32)


def _mm(a, b, mode, out_dtype, name, add=None):
    if mode == "nn":
        (M, K), (K2, N) = a.shape, b.shape
    elif mode == "nt":
        (M, K), (N, K2) = a.shape, b.shape
    else:
        (K, M), (K2, N) = a.shape, b.shape
    assert K == K2, (a.shape, b.shape, mode)
    tm = _tile(M, (1024, 512, 256, 128))
    tn = _tile(N, (512, 768, 256, 128))
    tk = _tile(K, (2048, 1408, 1024, 768, 512, 256, 128))
    if mode == "tn":
        tm = _tile(M, (512, 256, 128))
        tk = _tile(K, (1024, 512, 256, 128))
    nk = K // tk
    dims = {"nn": NN, "nt": NT, "tn": TN}[mode]
    has_add = add is not None

    def body(*refs):
        a_ref, b_ref = refs[0], refs[1]
        c_ref = refs[2] if has_add else None
        o_ref = refs[3] if has_add else refs[2]
        acc_ref = refs[-1]
        k = pl.program_id(2)
        part = _dot(a_ref[...], b_ref[...], dims)

        def finish(r):
            if has_add:
                r = r + c_ref[...].astype(F32)
            o_ref[...] = r.astype(o_ref.dtype)

        if nk == 1:
            finish(part)
        else:
            @pl.when(k == 0)
            def _():
                acc_ref[...] = part

            @pl.when(k > 0)
            def _():
                acc_ref[...] += part

            @pl.when(k == nk - 1)
            def _():
                finish(acc_ref[...])

    if mode == "nn":
        a_spec = pl.BlockSpec((tm, tk), lambda i, j, k: (i, k))
        b_spec = pl.BlockSpec((tk, tn), lambda i, j, k: (k, j))
    elif mode == "nt":
        a_spec = pl.BlockSpec((tm, tk), lambda i, j, k: (i, k))
        b_spec = pl.BlockSpec((tn, tk), lambda i, j, k: (j, k))
    else:
        a_spec = pl.BlockSpec((tk, tm), lambda i, j, k: (k, i))
        b_spec = pl.BlockSpec((tk, tn), lambda i, j, k: (k, j))
    o_spec = pl.BlockSpec((tm, tn), lambda i, j, k: (i, j))
    in_specs = [a_spec, b_spec] + ([o_spec] if has_add else [])
    args = (a, b) + ((add,) if has_add else ())
    return pl.pallas_call(
        body, name=name, grid=(M // tm, N // tn, nk), in_specs=in_specs, out_specs=o_spec,
        out_shape=jax.ShapeDtypeStruct((M, N), out_dtype),
        scratch_shapes=[pltpu.VMEM((tm, tn) if nk > 1 else (8, 128), F32)],
        compiler_params=_params(("parallel", "parallel", "arbitrary")),
    )(*args)


def _ffn_up(n, wg_t, wu_t, name):
    T = n.shape[0]
    tm = _tile(T, (1024, 512, 256, 128))
    tn = 512

    def body(n_ref, wg_ref, wu_ref, g_ref, u_ref, a_ref):
        x = n_ref[...]
        g = _dot(x, wg_ref[...], NT)
        u = _dot(x, wu_ref[...], NT)
        g_ref[...] = g
        u_ref[...] = u
        a_ref[...] = (g * jax.nn.sigmoid(g) * u).astype(BF16)

    w_spec = pl.BlockSpec((tn, D), lambda i, j: (j, 0))
    o_spec = pl.BlockSpec((tm, tn), lambda i, j: (i, j))
    return pl.pallas_call(
        body, name=name, grid=(T // tm, FP // tn),
        in_specs=[pl.BlockSpec((tm, D), lambda i, j: (i, 0)), w_spec, w_spec],
        out_specs=[o_spec, o_spec, o_spec],
        out_shape=[jax.ShapeDtypeStruct((T, FP), F32), jax.ShapeDtypeStruct((T, FP), F32),
                   jax.ShapeDtypeStruct((T, FP), BF16)],
        compiler_params=_params(("parallel", "parallel")),
    )(n, wg_t, wu_t)


def _row_specs(rows, tr, cw):
    return [pl.BlockSpec((tr, cw), lambda j, i, o=off: (i, o + j)) for _, off in rows]


def _const_specs(consts, cw):
    specs = []
    for arr, off in consts:
        if off is None:
            specs.append(pl.BlockSpec(arr.shape, lambda j, i: (0, 0)))
        else:
            specs.append(pl.BlockSpec((arr.shape[0], cw), lambda j, i, o=off: (0, o + j)))
    return specs


def _rowwise(fn, rows, consts, out_dtypes, name, tr, cw, ncol):
    T = rows[0][0].shape[0]
    nr, nc = len(rows), len(consts)

    def body(*refs):
        r = [x[...].astype(F32) for x in refs[:nr]]
        c = [x[...] for x in refs[nr:nr + nc]]
        res = fn(*r, *c)
        for o_ref, v in zip(refs[nr + nc:], res):
            o_ref[...] = v.astype(o_ref.dtype)

    o_spec = pl.BlockSpec((tr, cw), lambda j, i: (i, j))
    return pl.pallas_call(
        body, name=name, grid=(ncol, T // tr),
        in_specs=_row_specs(rows, tr, cw) + _const_specs(consts, cw),
        out_specs=[o_spec] * len(out_dtypes),
        out_shape=[jax.ShapeDtypeStruct((T, ncol * cw), dt) for dt in out_dtypes],
        compiler_params=_params(("parallel", "parallel")),
    )(*[a for a, _ in rows], *[a for a, _ in consts])


def _rowwise_bwd(fn, rows, consts, cots, diff, ddtypes, name, tr, cw, ncol):
    T = rows[0][0].shape[0]
    nr, nc, nt, nd = len(rows), len(consts), len(cots), len(diff)

    def body(*refs):
        r = [x[...].astype(F32) for x in refs[:nr]]
        c = [x[...] for x in refs[nr:nr + nc]]
        ct = [x[...].astype(F32) for x in refs[nr + nc:nr + nc + nt]]
        drow_refs = refs[nr + nc + nt:nr + nc + nt + nd]
        dconst_refs = refs[nr + nc + nt + nd:]
        i = pl.program_id(1)

        def f(*args):
            full = list(r)
            for idx, a in zip(diff, args[:nd]):
                full[idx] = a
            return tuple(fn(*full, *args[nd:]))

        _, vjp = jax.vjp(f, *[r[d] for d in diff], *c)
        g = vjp(tuple(ct))
        for o_ref, v in zip(drow_refs, g[:nd]):
            o_ref[...] = v.astype(o_ref.dtype)

        @pl.when(i == 0)
        def _():
            for o_ref in dconst_refs:
                o_ref[...] = jnp.zeros_like(o_ref)

        for o_ref, v in zip(dconst_refs, g[nd:]):
            o_ref[...] += v

    o_spec = pl.BlockSpec((tr, cw), lambda j, i: (i, j))
    out_shape = [jax.ShapeDtypeStruct((T, ncol * cw), dt) for dt in ddtypes]
    out_shape += [jax.ShapeDtypeStruct(a.shape, F32) for a, _ in consts]
    return pl.pallas_call(
        body, name=name, grid=(ncol, T // tr),
        in_specs=_row_specs(rows, tr, cw) + _const_specs(consts, cw) + _row_specs(cots, tr, cw),
        out_specs=[o_spec] * nd + _const_specs(consts, cw),
        out_shape=out_shape,
        compiler_params=_params(("parallel", "arbitrary")),
    )(*[a for a, _ in rows], *[a for a, _ in consts], *[a for a, _ in cots])


def _rms(x, g):
    return x * lax.rsqrt(jnp.mean(x * x, axis=-1, keepdims=True) + EPS) * g


def _silu(x):
    return x * jax.nn.sigmoid(x)


def _norm_fn(x, g):
    return (_rms(x, g),)


def _norm_res_fn(x, g):
    return (x, _rms(x, g))


def _resid_fn(scale, x, h, g):
    return (x + scale * _rms(h, g),)


def _resid_h_fn(scale, h, g):
    return (scale * _rms(h, g),)


def _hpost_fn(o, hog, gn):
    return (_rms(o, gn) * _silu(hog),)


def _merge_fn(z0, z1, z2, yh, yf, ym):
    return (jax.nn.sigmoid(z0) * yh + jax.nn.sigmoid(z1) * yf + jax.nn.sigmoid(z2) * ym,)


def _swiglu_fn(g, u):
    return (_silu(g) * u,)


def _loss(x3, tgt, name):
    T = x3.shape[0]
    tr = _tile(T, (256, 128))

    def body(x_ref, t_ref, dy_ref, s_ref):
        i = pl.program_id(0)
        e = x_ref[...] - t_ref[...]
        dy_ref[...] = e * (1.0 / D)
        col = jnp.sum(e * e, axis=0, keepdims=True)
        tot = col[:, 0:HD]
        for k in range(1, D // HD):
            tot = tot + col[:, k * HD:(k + 1) * HD]

        @pl.when(i == 0)
        def _():
            s_ref[...] = jnp.zeros_like(s_ref)

        s_ref[...] += tot

    spec = pl.BlockSpec((tr, D), lambda i: (i, 0))
    return pl.pallas_call(
        body, name=name, grid=(T // tr,), in_specs=[spec, spec],
        out_specs=[spec, pl.BlockSpec((1, HD), lambda i: (0, 0))],
        out_shape=[jax.ShapeDtypeStruct((T, D), F32), jax.ShapeDtypeStruct((1, HD), F32)],
        compiler_params=_params(("arbitrary",)),
    )(x3, tgt)


def _lower_bound(lb_ref):
    a0 = lb_ref[0:1, :]
    a1 = lb_ref[1:2, :]
    mx = jnp.maximum(a0, a1)
    e0 = jnp.exp(a0 - mx)
    return e0 / (e0 + jnp.exp(a1 - mx))


def _hgrn_prep(hq, hf, lb):
    g = lb + (1.0 - lb) * jax.nn.sigmoid(hf)
    return _silu(hq), 1.0 - g, jnp.log(g)


def _tri(n, upper):
    r = lax.broadcasted_iota(jnp.int32, (n, n), 0)
    c = lax.broadcasted_iota(jnp.int32, (n, n), 1)
    return (c >= r) if upper else (c <= r)


def _hgrn_factors(q, k, gl):
    low = _tri(CHUNK, False)
    b = lax.dot_general(low.astype(F32), gl, NN, precision=HIGHEST, preferred_element_type=F32)
    bl = b[CHUNK - 1:CHUNK, :]
    ref = b[CHUNK // 2 - 1:CHUNK // 2, :]
    eb = jnp.exp(b)
    ea = jnp.exp(b - ref)
    ebn = jnp.exp(ref - b)
    ek = jnp.exp(bl - b)
    ebl = jnp.exp(bl)
    return low, eb, ea, ebn, ek, ebl


def _hgrn_fwd(proj, hgrn_lb):
    T = proj.shape[0]
    cb = _tile(T, (512, 256, 128, 64))
    nchunk = cb // CHUNK

    def body(hq_ref, hf_ref, hi_ref, lb_ref, o_ref, st_ref, state):
        @pl.when(pl.program_id(0) == 0)
        def _():
            state[...] = jnp.zeros_like(state)

        lb = _lower_bound(lb_ref)

        def chunk(c, carry):
            r0 = pl.multiple_of(c * CHUNK, CHUNK)
            for h in range(NH):
                cols = slice(h * HD, (h + 1) * HD)
                q, k, gl = _hgrn_prep(hq_ref[pl.ds(r0, CHUNK), cols], hf_ref[pl.ds(r0, CHUNK), cols], lb[:, cols])
                v = hi_ref[pl.ds(r0, CHUNK), cols]
                low, eb, ea, ebn, ek, ebl = _hgrn_factors(q, k, gl)
                s_t = state[h]
                st_ref[c, h] = s_t
                pm = jnp.where(low, _dot(q * ea, k * ebn, NT), 0.0)
                o_ref[pl.ds(r0, CHUNK), cols] = _dot(q * eb, s_t, NT) + _dot(pm, v, NN)
                state[h] = s_t * ebl + _dot(v, k * ek, TN)
            return carry

        lax.fori_loop(0, nchunk, chunk, 0)

    def col(off):
        return pl.BlockSpec((cb, WH), lambda i, o=off: (i, o))

    return pl.pallas_call(
        body, name="hgrn_fwd", grid=(T // cb,),
        in_specs=[col(0), col(1), col(2), pl.BlockSpec((2, WH), lambda i: (0, 0))],
        out_specs=[pl.BlockSpec((cb, WH), lambda i: (i, 0)),
                   pl.BlockSpec((nchunk, NH, HD, HD), lambda i: (i, 0, 0, 0))],
        out_shape=[jax.ShapeDtypeStruct((T, WH), F32), jax.ShapeDtypeStruct((T // CHUNK, NH, HD, HD), F32)],
        scratch_shapes=[pltpu.VMEM((NH, HD, HD), F32)],
        compiler_params=_params(("arbitrary",)),
    )(proj, proj, proj, hgrn_lb)


def _hgrn_bwd(proj, hgrn_lb, states, do):
    T = proj.shape[0]
    cb = _tile(T, (512, 256, 128, 64))
    nchunk = cb // CHUNK
    nb = T // cb

    def body(hq_ref, hf_ref, hi_ref, lb_ref, st_ref, do_ref, dhq_ref, dhf_ref, dhi_ref, dlb_ref, dstate):
        @pl.when(pl.program_id(0) == 0)
        def _():
            dstate[...] = jnp.zeros_like(dstate)
            dlb_ref[...] = jnp.zeros_like(dlb_ref)

        lb = _lower_bound(lb_ref)
        up = _tri(CHUNK, True)
        last = lax.broadcasted_iota(jnp.int32, (CHUNK, HD), 0) == CHUNK - 1

        def chunk(cc, carry):
            c = nchunk - 1 - cc
            r0 = pl.multiple_of(c * CHUNK, CHUNK)
            for h in range(NH):
                cols = slice(h * HD, (h + 1) * HD)
                hq = hq_ref[pl.ds(r0, CHUNK), cols]
                hf = hf_ref[pl.ds(r0, CHUNK), cols]
                (q, k, gl), prep_vjp = jax.vjp(_hgrn_prep, hq, hf, lb[:, cols])
                v = hi_ref[pl.ds(r0, CHUNK), cols]
                d_o = do_ref[pl.ds(r0, CHUNK), cols]
                low, eb, ea, ebn, ek, ebl = _hgrn_factors(q, k, gl)
                s_t = st_ref[c, h]
                ds_new = dstate[h]
                qe, am, bm, kb = q * eb, q * ea, k * ebn, k * ek
                pm_t = jnp.where(up, _dot(bm, am, NT), 0.0)
                dp = jnp.where(low, _dot(d_o, v, NT), 0.0)
                dp_t = jnp.where(up, _dot(v, d_o, NT), 0.0)
                dqe = _dot(d_o, s_t, NN)
                da = _dot(dp, bm, NN)
                db_m = _dot(dp_t, am, NN)
                dkb = _dot(v, ds_new, NN)
                dv = _dot(pm_t, d_o, NN) + _dot(kb, ds_new, NT)
                dq = dqe * eb + da * ea
                dk = db_m * ebn + dkb * ek
                dbl = jnp.sum(dkb * kb, axis=0, keepdims=True) + jnp.sum(ds_new * s_t, axis=0, keepdims=True) * ebl
                db = (dqe * qe + da * am.astype(BF16).astype(F32) - db_m * bm.astype(BF16).astype(F32) - dkb * kb
                      + jnp.where(last, dbl, 0.0))
                dgl = lax.dot_general(up.astype(F32), db, NN, precision=HIGHEST, preferred_element_type=F32)
                dhq, dhf, dlb = prep_vjp((dq, dk, dgl))
                dhq_ref[pl.ds(r0, CHUNK), cols] = dhq.astype(dhq_ref.dtype)
                dhf_ref[pl.ds(r0, CHUNK), cols] = dhf.astype(dhf_ref.dtype)
                dhi_ref[pl.ds(r0, CHUNK), cols] = dv.astype(dhi_ref.dtype)
                dlb_ref[:, cols] += dlb
                dstate[h] = _dot(d_o, qe, TN) + ds_new * ebl
            return carry

        lax.fori_loop(0, nchunk, chunk, 0)

    def col(off):
        return pl.BlockSpec((cb, WH), lambda i, o=off: (nb - 1 - i, o))

    row = pl.BlockSpec((cb, WH), lambda i: (nb - 1 - i, 0))
    return pl.pallas_call(
        body, name="hgrn_bwd", grid=(nb,),
        in_specs=[col(0), col(1), col(2), pl.BlockSpec((2, WH), lambda i: (0, 0)),
                  pl.BlockSpec((nchunk, NH, HD, HD), lambda i: (nb - 1 - i, 0, 0, 0)), row],
        out_specs=[row, row, row, pl.BlockSpec((1, WH), lambda i: (0, 0))],
        out_shape=[jax.ShapeDtypeStruct((T, WH), BF16)] * 3 + [jax.ShapeDtypeStruct((1, WH), F32)],
        scratch_shapes=[pltpu.VMEM((NH, HD, HD), F32)],
        compiler_params=_params(("arbitrary",)),
    )(proj, proj, proj, hgrn_lb, states, do)


def _log_sigmoid(z):
    return jnp.minimum(z, 0.0) - jnp.log(1.0 + jnp.exp(-jnp.abs(z)))


def _fox_cum(proj, fb_pad):
    T = proj.shape[0]
    tb = _tile(T, (256, 128))

    def body(ff_ref, fb_ref, ct_ref, cq_ref, carry):
        @pl.when(pl.program_id(0) == 0)
        def _():
            carry[...] = jnp.zeros_like(carry)

        lf = _log_sigmoid(ff_ref[...] + fb_ref[...])
        cs = lax.dot_general(_tri(tb, False).astype(F32), lf, NN, precision=HIGHEST,
                             preferred_element_type=F32) + carry[0:1, :]
        carry[0:1, :] = cs[tb - 1:tb, :]
        ct_ref[...] = cs.T[0:8, :]
        for h in range(NH):
            cq_ref[h] = jnp.broadcast_to(cs[:, h:h + 1], (tb, HD))

    return pl.pallas_call(
        body, name="fox_cum", grid=(T // tb,),
        in_specs=[pl.BlockSpec((tb, HD), lambda i: (i, CB_FF)), pl.BlockSpec((1, HD), lambda i: (0, 0))],
        out_specs=[pl.BlockSpec((8, tb), lambda i: (0, i)), pl.BlockSpec((NH, tb, HD), lambda i: (0, i, 0))],
        out_shape=[jax.ShapeDtypeStruct((8, T), F32), jax.ShapeDtypeStruct((NH, T, HD), F32)],
        scratch_shapes=[pltpu.VMEM((8, HD), F32)],
        compiler_params=_params(("arbitrary",)),
    )(proj, fb_pad)


def _fox_cum_bwd(dc, proj, fb_pad):
    T = proj.shape[0]
    tb = _tile(T, (256, 128))
    nb = T // tb

    def body(dc_ref, ff_ref, fb_ref, dff_ref, dfb_ref, carry):
        @pl.when(pl.program_id(0) == 0)
        def _():
            carry[...] = jnp.zeros_like(carry)
            dfb_ref[...] = jnp.zeros_like(dfb_ref)

        rid = lax.broadcasted_iota(jnp.int32, (8, tb), 0)
        m8 = jnp.zeros((8, tb), F32)
        for h in range(NH):
            m8 = m8 + jnp.where(rid == h, dc_ref[h], 0.0)
        dcb = jnp.concatenate([m8, jnp.zeros((HD - 8, tb), F32)], axis=0).T
        rev = lax.dot_general(_tri(tb, True).astype(F32), dcb, NN, precision=HIGHEST,
                              preferred_element_type=F32) + carry[0:1, :]
        carry[0:1, :] = rev[0:1, :]
        dff = rev * jax.nn.sigmoid(-(ff_ref[...] + fb_ref[...]))
        dff_ref[...] = dff.astype(dff_ref.dtype)
        dfb_ref[...] += jnp.sum(dff, axis=0, keepdims=True)

    return pl.pallas_call(
        body, name="fox_cum_bwd", grid=(nb,),
        in_specs=[pl.BlockSpec((NH, 8, tb), lambda i: (0, 0, nb - 1 - i)),
                  pl.BlockSpec((tb, HD), lambda i: (nb - 1 - i, CB_FF)), pl.BlockSpec((1, HD), lambda i: (0, 0))],
        out_specs=[pl.BlockSpec((tb, HD), lambda i: (nb - 1 - i, 0)), pl.BlockSpec((1, HD), lambda i: (0, 0))],
        out_shape=[jax.ShapeDtypeStruct((T, HD), BF16), jax.ShapeDtypeStruct((1, HD), F32)],
        scratch_shapes=[pltpu.VMEM((8, HD), F32)],
        compiler_params=_params(("arbitrary",)),
    )(dc, proj, fb_pad)


def _fox_scores(q, k, cq, ck, i, j, bq, bk):
    s = _dot(q, k, NT) * SCALE + (cq - ck)
    rows = lax.broadcasted_iota(jnp.int32, (bq, bk), 0) + i * bq
    cols = lax.broadcasted_iota(jnp.int32, (bq, bk), 1) + j * bk
    return jnp.where(cols <= rows, s, NEG)


def _fox_fwd(proj, ct, cq):
    T = proj.shape[0]
    bq = bk = _tile(T, (512, 256, 128))
    nq = nk = T // bq

    def body(q_ref, k_ref, v_ref, ct_ref, cq_ref, o_ref, lse_ref, m_s, l_s, acc_s):
        h, i, j = pl.program_id(0), pl.program_id(1), pl.program_id(2)

        @pl.when(j == 0)
        def _():
            m_s[...] = jnp.full_like(m_s, NEG)
            l_s[...] = jnp.zeros_like(l_s)
            acc_s[...] = jnp.zeros_like(acc_s)

        @pl.when(j <= i)
        def _():
            s = _fox_scores(q_ref[...], k_ref[...], cq_ref[:, 0:1], ct_ref[pl.ds(h, 1), :], i, j, bq, bk)
            m_prev = m_s[...]
            m_new = jnp.maximum(m_prev, jnp.max(s, axis=1, keepdims=True))
            alpha = jnp.exp(m_prev - m_new)
            p = jnp.exp(s - m_new)
            l_s[...] = alpha * l_s[...] + jnp.sum(p, axis=1, keepdims=True)
            acc_s[...] = alpha * acc_s[...] + _dot(p, v_ref[...], NN)
            m_s[...] = m_new

        @pl.when(j == nk - 1)
        def _():
            o_ref[...] = acc_s[...] / l_s[...]
            lse_ref[...] = jnp.broadcast_to(m_s[...] + jnp.log(l_s[...]), (bq, HD))

    def kv(off):
        return pl.BlockSpec((bk, HD), lambda h, i, j, o=off: (jnp.minimum(j, i), o + h))

    return pl.pallas_call(
        body, name="fox_fwd", grid=(NH, nq, nk),
        in_specs=[pl.BlockSpec((bq, HD), lambda h, i, j: (i, CB_FQ + h)), kv(CB_FK), kv(CB_FV),
                  pl.BlockSpec((8, bk), lambda h, i, j: (0, jnp.minimum(j, i))),
                  pl.BlockSpec((None, bq, HD), lambda h, i, j: (h, i, 0))],
        out_specs=[pl.BlockSpec((bq, HD), lambda h, i, j: (i, h)),
                   pl.BlockSpec((None, bq, HD), lambda h, i, j: (h, i, 0))],
        out_shape=[jax.ShapeDtypeStruct((T, WH), F32), jax.ShapeDtypeStruct((NH, T, HD), F32)],
        scratch_shapes=[pltpu.VMEM((bq, 1), F32), pltpu.VMEM((bq, 1), F32), pltpu.VMEM((bq, HD), F32)],
        compiler_params=_params(("parallel", "parallel", "arbitrary")),
    )(proj, proj, proj, ct, cq)


def _fox_bwd_dq(proj, ct, cq, lse, do):
    T = proj.shape[0]
    bq = bk = _tile(T, (512, 256, 128))
    nq = nk = T // bq

    def body(q_ref, k_ref, v_ref, ct_ref, cq_ref, lse_ref, do_ref, dq_ref, delta_ref, acc_s, delta_s, psum_s):
        h, i, jj = pl.program_id(0), pl.program_id(1), pl.program_id(2)
        j = jj % nk

        @pl.when(jj == 0)
        def _():
            acc_s[...] = jnp.zeros_like(acc_s)
            delta_s[...] = jnp.zeros_like(delta_s)
            psum_s[...] = jnp.zeros_like(psum_s)

        @pl.when(j <= i)
        def _():
            k = k_ref[...]
            s = _fox_scores(q_ref[...], k, cq_ref[:, 0:1], ct_ref[pl.ds(h, 1), :], i, j, bq, bk)
            p = jnp.exp(s - lse_ref[:, 0:1])
            dp = _dot(do_ref[...], v_ref[...], NT)

            @pl.when(jj < nk)
            def _():
                delta_s[...] += jnp.sum(p * dp, axis=1, keepdims=True)
                psum_s[...] += jnp.sum(p, axis=1, keepdims=True)

            @pl.when(jj >= nk)
            def _():
                ds = p * (dp - delta_s[...] / psum_s[...])
                acc_s[...] += _dot(ds, k, NN) * SCALE

        @pl.when(jj == 2 * nk - 1)
        def _():
            dq_ref[...] = acc_s[...].astype(dq_ref.dtype)
            delta_ref[...] = jnp.broadcast_to(delta_s[...] / psum_s[...], (bq, HD))

    def kv(off):
        return pl.BlockSpec((bk, HD), lambda h, i, jj, o=off: (jnp.minimum(jj % nk, i), o + h))

    qrow = pl.BlockSpec((bq, HD), lambda h, i, jj: (i, h))
    stat = pl.BlockSpec((None, bq, HD), lambda h, i, jj: (h, i, 0))
    return pl.pallas_call(
        body, name="fox_bwd_dq", grid=(NH, nq, 2 * nk),
        in_specs=[pl.BlockSpec((bq, HD), lambda h, i, jj: (i, CB_FQ + h)), kv(CB_FK), kv(CB_FV),
                  pl.BlockSpec((8, bk), lambda h, i, jj: (0, jnp.minimum(jj % nk, i))), stat, stat, qrow],
        out_specs=[qrow, stat],
        out_shape=[jax.ShapeDtypeStruct((T, WH), BF16), jax.ShapeDtypeStruct((NH, T, HD), F32)],
        scratch_shapes=[pltpu.VMEM((bq, HD), F32), pltpu.VMEM((bq, 1), F32), pltpu.VMEM((bq, 1), F32)],
        compiler_params=_params(("parallel", "parallel", "arbitrary")),
    )(proj, proj, proj, ct, cq, lse, do)


def _fox_bwd_dkv(proj, ct, cq, lse, delta, do):
    T = proj.shape[0]
    bq = bk = _tile(T, (512, 256, 128))
    nq = nk = T // bq

    def body(q_ref, k_ref, v_ref, ct_ref, cq_ref, lse_ref, delta_ref, do_ref, dk_ref, dv_ref, dc_ref, dk_s, dv_s, dc_s):
        h, j, i = pl.program_id(0), pl.program_id(1), pl.program_id(2)

        @pl.when(i == 0)
        def _():
            dk_s[...] = jnp.zeros_like(dk_s)
            dv_s[...] = jnp.zeros_like(dv_s)
            dc_s[...] = jnp.zeros_like(dc_s)

        @pl.when(i >= j)
        def _():
            q = q_ref[...]
            d_o = do_ref[...]
            s = _fox_scores(q, k_ref[...], cq_ref[:, 0:1], ct_ref[pl.ds(h, 1), :], i, j, bq, bk)
            p = jnp.exp(s - lse_ref[:, 0:1])
            dv_s[...] += _dot(p, d_o, TN)
            dp = _dot(d_o, v_ref[...], NT)
            ds = p * (dp - delta_ref[:, 0:1])
            dk_s[...] += _dot(ds, q, TN) * SCALE
            dc_s[...] -= jnp.sum(ds, axis=0, keepdims=True)

        @pl.when(i == nq - 1)
        def _():
            dk_ref[...] = dk_s[...].astype(dk_ref.dtype)
            dv_ref[...] = dv_s[...].astype(dv_ref.dtype)
            dc_ref[...] = jnp.broadcast_to(dc_s[...], (8, bk))

    def kv(off):
        return pl.BlockSpec((bk, HD), lambda h, j, i, o=off: (j, o + h))

    qrow = pl.BlockSpec((bq, HD), lambda h, j, i: (jnp.maximum(i, j), h))
    stat = pl.BlockSpec((None, bq, HD), lambda h, j, i: (h, jnp.maximum(i, j), 0))
    krow = pl.BlockSpec((bk, HD), lambda h, j, i: (j, h))
    return pl.pallas_call(
        body, name="fox_bwd_dkv", grid=(NH, nk, nq),
        in_specs=[pl.BlockSpec((bq, HD), lambda h, j, i: (jnp.maximum(i, j), CB_FQ + h)), kv(CB_FK), kv(CB_FV),
                  pl.BlockSpec((8, bk), lambda h, j, i: (0, j)), stat, stat, stat, qrow],
        out_specs=[krow, krow, pl.BlockSpec((None, 8, bk), lambda h, j, i: (h, 0, j))],
        out_shape=[jax.ShapeDtypeStruct((T, WH), BF16), jax.ShapeDtypeStruct((T, WH), BF16),
                   jax.ShapeDtypeStruct((NH, 8, T), F32)],
        scratch_shapes=[pltpu.VMEM((bk, HD), F32), pltpu.VMEM((bk, HD), F32), pltpu.VMEM((1, bk), F32)],
        compiler_params=_params(("parallel", "parallel", "arbitrary")),
    )(proj, proj, proj, ct, cq, lse, delta, do)


def _mem_probs(q, mk):
    s = _dot(q, mk, NT) * SCALE
    e = jnp.exp(s - jnp.max(s, axis=1, keepdims=True))
    return e / jnp.sum(e, axis=1, keepdims=True)


def _mem_fwd(proj, mem_kv):
    T = proj.shape[0]
    tr = _tile(T, (512, 256, 128))
    M = mem_kv.shape[0]

    def body(q_ref, mk_ref, mv_ref, o_ref):
        o_ref[...] = _dot(_mem_probs(q_ref[...], mk_ref[...]), mv_ref[...], NN)

    return pl.pallas_call(
        body, name="mem_fwd", grid=(NM, T // tr),
        in_specs=[pl.BlockSpec((tr, HD), lambda h, i: (i, CB_MQ + h)),
                  pl.BlockSpec((M, HD), lambda h, i: (0, h)), pl.BlockSpec((M, HD), lambda h, i: (0, NM + h))],
        out_specs=pl.BlockSpec((tr, HD), lambda h, i: (i, h)),
        out_shape=jax.ShapeDtypeStruct((T, WM), F32),
        compiler_params=_params(("parallel", "parallel")),
    )(proj, mem_kv, mem_kv)


def _mem_bwd(proj, mem_kv, do):
    T = proj.shape[0]
    tr = _tile(T, (512, 256, 128))
    M = mem_kv.shape[0]

    def body(q_ref, mk_ref, mv_ref, do_ref, dq_ref, dmk_ref, dmv_ref):
        @pl.when(pl.program_id(1) == 0)
        def _():
            dmk_ref[...] = jnp.zeros_like(dmk_ref)
            dmv_ref[...] = jnp.zeros_like(dmv_ref)

        q, mk, d_o = q_ref[...], mk_ref[...], do_ref[...]
        p = _mem_probs(q, mk)
        dmv_ref[...] += _dot(p, d_o, TN)
        dp = _dot(d_o, mv_ref[...], NT)
        ds = p * (dp - jnp.sum(p * dp, axis=1, keepdims=True))
        dq_ref[...] = (_dot(ds, mk, NN) * SCALE).astype(dq_ref.dtype)
        dmk_ref[...] += _dot(ds, q, TN) * SCALE

    acc = pl.BlockSpec((M, HD), lambda h, i: (0, h))
    row = pl.BlockSpec((tr, HD), lambda h, i: (i, h))
    return pl.pallas_call(
        body, name="mem_bwd", grid=(NM, T // tr),
        in_specs=[pl.BlockSpec((tr, HD), lambda h, i: (i, CB_MQ + h)),
                  pl.BlockSpec((M, HD), lambda h, i: (0, h)), pl.BlockSpec((M, HD), lambda h, i: (0, NM + h)), row],
        out_specs=[row, acc, acc],
        out_shape=[jax.ShapeDtypeStruct((T, WM), BF16), jax.ShapeDtypeStruct((M, WM), F32),
                   jax.ShapeDtypeStruct((M, WM), F32)],
        compiler_params=_params(("parallel", "arbitrary")),
    )(proj, mem_kv, mem_kv, do)


def _mesh_place():
    x, y, c = lax.axis_index("x"), lax.axis_index("y"), lax.axis_index("c")
    return x, y, c


def _peer(x, y, c, k):
    px = 1 - x if k & 4 else x
    py = 1 - y if k & 2 else y
    pc = 1 - c if k & 1 else c
    return (px, py, pc), 4 * px + 2 * py + pc


def _all_gather(shards, pad_rows):
    n = len(shards)
    npad = sum(1 for p in pad_rows if p)
    zeros = jnp.zeros((max(pad_rows) or 16, shards[0].shape[1]), shards[0].dtype)

    def body(*refs):
        ins, z_ref, outs = refs[:n], refs[n], refs[n + 1:2 * n + 1]
        send_sems, recv_sems, loc_sems = refs[2 * n + 1:]
        x, y, c = _mesh_place()
        me = 4 * x + 2 * y + c
        copies = []
        ip = 0
        for w in range(n):
            r = ins[w].shape[0]
            dst = outs[w].at[pl.ds(pl.multiple_of(me * r, 16), r), :]
            cp = pltpu.make_async_copy(ins[w], dst, loc_sems.at[w])
            cp.start()
            copies.append(cp)
            if pad_rows[w]:
                cp = pltpu.make_async_copy(z_ref.at[pl.ds(0, pad_rows[w]), :],
                                           outs[w].at[pl.ds(NDEV * r, pad_rows[w]), :], loc_sems.at[n + ip])
                cp.start()
                copies.append(cp)
                ip += 1
            for k in range(1, NDEV):
                peer, _ = _peer(x, y, c, k)
                cp = pltpu.make_async_remote_copy(src_ref=ins[w], dst_ref=dst, send_sem=send_sems.at[w, k - 1],
                                                  recv_sem=recv_sems.at[w, k - 1], device_id=peer, device_id_type=MESH)
                cp.start()
                copies.append(cp)
        for cp in copies:
            cp.wait()

    any_spec = pl.BlockSpec(memory_space=pl.ANY)
    return pl.pallas_call(
        body, name="all_gather_weights",
        in_specs=[any_spec] * (n + 1), out_specs=[any_spec] * n,
        out_shape=[jax.ShapeDtypeStruct((NDEV * s.shape[0] + p, s.shape[1]), s.dtype) for s, p in zip(shards, pad_rows)],
        scratch_shapes=[pltpu.SemaphoreType.DMA((n, NDEV - 1)), pltpu.SemaphoreType.DMA((n, NDEV - 1)),
                        pltpu.SemaphoreType.DMA((n + npad,))],
        compiler_params=pltpu.CompilerParams(has_side_effects=True),
    )(*shards, zeros)


def _reduce_scatter_exchange(grads, shard_rows):
    n = len(grads)

    def body(*refs):
        ins, outs = refs[:n], refs[n:2 * n]
        send_sems, recv_sems, loc_sems = refs[2 * n:]
        x, y, c = _mesh_place()
        me = 4 * x + 2 * y + c
        copies = []
        for w in range(n):
            r = shard_rows[w]
            cp = pltpu.make_async_copy(ins[w].at[pl.ds(pl.multiple_of(me * r, 16), r), :], outs[w].at[me], loc_sems.at[w])
            cp.start()
            copies.append(cp)
            for k in range(1, NDEV):
                peer, pidx = _peer(x, y, c, k)
                cp = pltpu.make_async_remote_copy(
                    src_ref=ins[w].at[pl.ds(pl.multiple_of(pidx * r, 16), r), :], dst_ref=outs[w].at[me],
                    send_sem=send_sems.at[w, k - 1], recv_sem=recv_sems.at[w, k - 1], device_id=peer, device_id_type=MESH)
                cp.start()
                copies.append(cp)
        for cp in copies:
            cp.wait()

    any_spec = pl.BlockSpec(memory_space=pl.ANY)
    return pl.pallas_call(
        body, name="reduce_scatter_grads",
        in_specs=[any_spec] * n, out_specs=[any_spec] * n,
        out_shape=[jax.ShapeDtypeStruct((NDEV, r, g.shape[1]), g.dtype) for g, r in zip(grads, shard_rows)],
        scratch_shapes=[pltpu.SemaphoreType.DMA((n, NDEV - 1)), pltpu.SemaphoreType.DMA((n, NDEV - 1)),
                        pltpu.SemaphoreType.DMA((n,))],
        compiler_params=pltpu.CompilerParams(has_side_effects=True),
    )(*grads)


def _sum_slots(recv, name):
    _, r, c = recv.shape
    tr = _tile(r, (128, 64, 32, 16))

    def body(x_ref, o_ref):
        acc = x_ref[0].astype(F32)
        for d in range(1, NDEV):
            acc = acc + x_ref[d].astype(F32)
        o_ref[...] = acc

    return pl.pallas_call(
        body, name=name, grid=(r // tr,),
        in_specs=[pl.BlockSpec((NDEV, tr, c), lambda i: (0, i, 0))],
        out_specs=pl.BlockSpec((tr, c), lambda i: (i, 0)),
        out_shape=jax.ShapeDtypeStruct((r, c), F32),
        compiler_params=_params(("parallel",)),
    )(recv)


def _all_reduce_small(part):
    R, W = part.shape

    def body(x_ref, o_ref, buf, send_sems, recv_sems):
        x, y, c = _mesh_place()
        me = 4 * x + 2 * y + c
        buf[me] = x_ref[...]
        copies = []
        for k in range(1, NDEV):
            peer, _ = _peer(x, y, c, k)
            cp = pltpu.make_async_remote_copy(src_ref=x_ref, dst_ref=buf.at[me], send_sem=send_sems.at[k - 1],
                                              recv_sem=recv_sems.at[k - 1], device_id=peer, device_id_type=MESH)
            cp.start()
            copies.append(cp)
        for cp in copies:
            cp.wait()
        acc = buf[0]
        for d in range(1, NDEV):
            acc = acc + buf[d]
        o_ref[...] = acc

    vm = pl.BlockSpec(memory_space=pltpu.VMEM)
    return pl.pallas_call(
        body, name="all_reduce_small", in_specs=[vm], out_specs=vm,
        out_shape=jax.ShapeDtypeStruct((R, W), F32),
        scratch_shapes=[pltpu.VMEM((NDEV, R, W), F32), pltpu.SemaphoreType.DMA((NDEV - 1,)),
                        pltpu.SemaphoreType.DMA((NDEV - 1,))],
        compiler_params=pltpu.CompilerParams(has_side_effects=True),
    )(part)


def _adam_math(w, g, m, v):
    m2 = ADAM_B1 * m + (1.0 - ADAM_B1) * g
    v2 = ADAM_B2 * v + (1.0 - ADAM_B2) * (g * g)
    m_hat = m2 / (1.0 - ADAM_B1 ** ADAM_STEP)
    v_hat = v2 / (1.0 - ADAM_B2 ** ADAM_STEP)
    delta = -ADAM_LR * (m_hat / (jnp.sqrt(v_hat) + ADAM_EPS) + ADAM_WD * w)
    return delta, m2, v2


def _adamw(w, g, m, v, name):
    r, c = w.shape
    tr = r
    for cand in (1024, 512, 256, 128, 64, 32, 16, 8):
        if r % cand == 0 and cand * c <= 256 * 1024:
            tr = cand
            break

    def body(w_ref, g_ref, m_ref, v_ref, d_ref, m2_ref, v2_ref):
        d_ref[...], m2_ref[...], v2_ref[...] = _adam_math(w_ref[...], g_ref[...], m_ref[...], v_ref[...])

    spec = pl.BlockSpec((tr, c), lambda i: (i, 0))
    return pl.pallas_call(
        body, name=name, grid=(r // tr,), in_specs=[spec] * 4, out_specs=[spec] * 3,
        out_shape=[jax.ShapeDtypeStruct((r, c), F32)] * 3,
        compiler_params=_params(("parallel",)),
    )(w, g, m, v)


GAINS = ("ffn1_pre", "ffn1_post", "mix_pre", "mix_post", "mem_norm", "ffn2_pre", "ffn2_post")
GAIN_ROWS = D // HD
ROW_LB = len(GAINS) * GAIN_ROWS
ROWS_GRAD_IN = ROW_LB + 24
ROWS_PACKED = ROW_LB + 32


def _small_update(gsum, w_p, m_p, v_p):
    def body(g_ref, w_ref, m_ref, v_ref, go_ref, d_ref, m2_ref, v2_ref):
        a0 = w_ref[ROW_LB:ROW_LB + 8, :]
        a1 = w_ref[ROW_LB + 8:ROW_LB + 16, :]
        mx = jnp.maximum(a0, a1)
        e0, e1 = jnp.exp(a0 - mx), jnp.exp(a1 - mx)
        lb = e0 / (e0 + e1)
        da0 = g_ref[ROW_LB:ROW_LB + 8, :] * lb * (1.0 - lb)
        g = jnp.concatenate([g_ref[0:ROW_LB, :], da0, -da0, g_ref[ROW_LB + 8:ROWS_GRAD_IN, :]], axis=0)
        go_ref[...] = g
        d_ref[...], m2_ref[...], v2_ref[...] = _adam_math(w_ref[...], g, m_ref[...], v_ref[...])

    vm = pl.BlockSpec(memory_space=pltpu.VMEM)
    return pl.pallas_call(
        body, name="small_update", in_specs=[vm] * 4, out_specs=[vm] * 4,
        out_shape=[jax.ShapeDtypeStruct((ROWS_PACKED, HD), F32)] * 4,
    )(gsum, w_p, m_p, v_p)


def _rows8(a):
    a = a.reshape(-1)
    rows = -(-a.shape[0] // HD)
    rows8 = -(-rows // 8) * 8
    return jnp.pad(a, (0, rows8 * HD - a.shape[0])).reshape(rows8, HD)


def _pack_small(gains, lb0, lb1, gnorm, fb):
    return jnp.concatenate([_rows8(g) for g in gains] + [_rows8(lb0), _rows8(lb1), _rows8(gnorm), _rows8(fb)], axis=0)


def _unpack_small(p):
    out = {}
    for i, name in enumerate(GAINS):
        out[name] = p[i * GAIN_ROWS:(i + 1) * GAIN_ROWS].reshape(1, D)
    lb0 = p[ROW_LB:ROW_LB + NH].reshape(1, WH)
    lb1 = p[ROW_LB + 8:ROW_LB + 8 + NH].reshape(1, WH)
    out["hgrn_lb"] = jnp.concatenate([lb0, lb1], axis=0)
    out["hgrn_gnorm"] = p[ROW_LB + 16:ROW_LB + 16 + NH].reshape(1, WH)
    out["fox_fb"] = p[ROW_LB + 24:ROW_LB + 25, 0:NH]
    return out


def _ffn_forward(xin, pre, post, wg_t, wu_t, wd, tag):
    T = xin.shape[0]
    tr = _tile(T, (256, 128))
    (n,) = _rowwise(_norm_fn, [(xin, 0)], [(pre, None)], [BF16], f"{tag}_pre", tr, D, 1)
    g, u, a = _ffn_up(n, wg_t, wu_t, f"{tag}_up")
    h = _mm(a, wd, "nn", F32, f"{tag}_down")
    (xout,) = _rowwise(functools.partial(_resid_fn, 0.5), [(xin, 0), (h, 0)], [(post, None)], [F32], f"{tag}_post", tr, D, 1)
    return xout, (xin, n, g, u, a, h)


def _ffn_backward(dxout, saved, pre, post, wg_t, wu_t, wd, tag):
    xin, n, g, u, a, h = saved
    T = xin.shape[0]
    tr = _tile(T, (256, 128))
    dh, dpost = _rowwise_bwd(functools.partial(_resid_h_fn, 0.5), [(h, 0)], [(post, None)], [(dxout, 0)], [0], [BF16],
                             f"{tag}_post_bwd", tr, D, 1)
    da = _mm(dh, wd, "nt", BF16, f"{tag}_da")
    dwd = _mm(a, dh, "tn", BF16, f"{tag}_dwd")
    dg, du = _rowwise_bwd(_swiglu_fn, [(g, 0), (u, 0)], [], [(da, 0)], [0, 1], [BF16, BF16], f"{tag}_act_bwd",
                          tr, 512, FP // 512)
    dwg = _mm(dg, n, "tn", BF16, f"{tag}_dwg")
    dwu = _mm(du, n, "tn", BF16, f"{tag}_dwu")
    dn = _mm(dg, wg_t, "nn", F32, f"{tag}_dn_g")
    dn = _mm(du, wu_t, "nn", F32, f"{tag}_dn_u", add=dn)
    dxin, dpre = _rowwise_bwd(_norm_res_fn, [(xin, 0)], [(pre, None)], [(dxout, 0), (dn, 0)], [0], [F32],
                              f"{tag}_pre_bwd", tr, D, 1)
    return dxin, (dwg, dwu, dwd), dpre, dpost


def _local_step(x, mem, tgt, small, wts):
    T = x.shape[0]
    tr = _tile(T, (256, 128))
    fb_pad = jnp.pad(small["fox_fb"], ((0, 0), (0, HD - NH)))

    x1, ffn1_saved = _ffn_forward(x, small["ffn1_pre"], small["ffn1_post"], wts["ffn1_wg"], wts["ffn1_wu"],
                                  wts["ffn1_wd"], "ffn1")
    (un,) = _rowwise(_norm_fn, [(x1, 0)], [(small["mix_pre"], None)], [BF16], "mix_pre", tr, D, 1)
    proj = _mm(un, wts["w_in"], "nn", F32, "proj")
    z = _mm(un, wts["w_gate"], "nt", F32, "gate_logits")
    (memn,) = _rowwise(_norm_fn, [(mem, 0)], [(small["mem_norm"], None)], [BF16], "mem_norm", mem.shape[0], D, 1)
    mem_kv = _mm(memn, wts["w_mem_kv"], "nn", F32, "mem_kv")

    o_raw, states = _hgrn_fwd(proj, small["hgrn_lb"])
    (o_h,) = _rowwise(_hpost_fn, [(o_raw, 0), (proj, CB_HOG)], [(small["hgrn_gnorm"], 0)], [BF16], "hgrn_post",
                      tr, HD, NH)
    ct, cq = _fox_cum(proj, fb_pad)
    o_f, lse = _fox_fwd(proj, ct, cq)
    o_m = _mem_fwd(proj, mem_kv)

    yh = _mm(o_h, wts["w_hgrn_out"], "nt", F32, "hgrn_out")
    yf = _mm(o_f, wts["w_fox_out"], "nt", F32, "fox_out")
    ym = _mm(o_m, wts["w_mem_out"], "nt", F32, "mem_out")
    zc = D // 512
    merge_rows = [(z, 0), (z, zc), (z, 2 * zc), (yh, 0), (yf, 0), (ym, 0)]
    (merged,) = _rowwise(_merge_fn, merge_rows, [], [BF16], "merge", tr, 512, zc)
    m = _mm(merged, wts["w_o"], "nn", F32, "mix_out")
    (x2,) = _rowwise(functools.partial(_resid_fn, 1.0), [(x1, 0), (m, 0)], [(small["mix_post"], None)], [F32], "mix_post",
                     tr, D, 1)
    x3, ffn2_saved = _ffn_forward(x2, small["ffn2_pre"], small["ffn2_post"], wts["ffn2_wg"], wts["ffn2_wu"],
                                  wts["ffn2_wd"], "ffn2")
    dy, loss_part = _loss(x3, tgt, "loss")

    gw, gs = {}, {}
    dx2, (gw["ffn2_wg"], gw["ffn2_wu"], gw["ffn2_wd"]), gs["ffn2_pre"], gs["ffn2_post"] = _ffn_backward(
        dy, ffn2_saved, small["ffn2_pre"], small["ffn2_post"], wts["ffn2_wg"], wts["ffn2_wu"], wts["ffn2_wd"], "ffn2")

    dm, gs["mix_post"] = _rowwise_bwd(functools.partial(_resid_h_fn, 1.0), [(m, 0)], [(small["mix_post"], None)],
                                      [(dx2, 0)], [0], [BF16], "mix_post_bwd", tr, D, 1)
    dmerged = _mm(dm, wts["w_o"], "nt", F32, "d_merged")
    gw["w_o"] = _mm(merged, dm, "tn", BF16, "d_w_o")
    dz0, dz1, dz2, dyh, dyf, dym = _rowwise_bwd(_merge_fn, merge_rows, [], [(dmerged, 0)], [0, 1, 2, 3, 4, 5], [BF16] * 6,
                                                "merge_bwd", tr, 512, zc)
    dz = jnp.concatenate([dz0, dz1, dz2], axis=1)
    gw["w_gate"] = _mm(dz, un, "tn", BF16, "d_w_gate")
    dun = _mm(dz, wts["w_gate"], "nn", F32, "d_un_gate")

    do_h = _mm(dyh, wts["w_hgrn_out"], "nn", F32, "d_o_h")
    gw["w_hgrn_out"] = _mm(dyh, o_h, "tn", BF16, "d_w_hgrn_out")
    do_f = _mm(dyf, wts["w_fox_out"], "nn", F32, "d_o_f")
    gw["w_fox_out"] = _mm(dyf, o_f, "tn", BF16, "d_w_fox_out")
    do_m = _mm(dym, wts["w_mem_out"], "nn", F32, "d_o_m")
    gw["w_mem_out"] = _mm(dym, o_m, "tn", BF16, "d_w_mem_out")

    do_raw, dhog, gs["hgrn_gnorm"] = _rowwise_bwd(_hpost_fn, [(o_raw, 0), (proj, CB_HOG)], [(small["hgrn_gnorm"], 0)],
                                                  [(do_h, 0)], [0, 1], [F32, BF16], "hgrn_post_bwd", tr, HD, NH)
    dhq, dhf, dhi, gs["hgrn_lb"] = _hgrn_bwd(proj, small["hgrn_lb"], states, do_raw)
    dfq, delta = _fox_bwd_dq(proj, ct, cq, lse, do_f)
    dfk, dfv, dc = _fox_bwd_dkv(proj, ct, cq, lse, delta, do_f)
    dff, dfb = _fox_cum_bwd(dc, proj, fb_pad)
    gs["fox_fb"] = dfb
    dmq, dmk, dmv = _mem_bwd(proj, mem_kv, do_m)

    dproj = jnp.concatenate([dhq, dhf, dhi, dhog, dfq, dfk, dfv, dff, dmq, jnp.zeros((T, HD), BF16)], axis=1)
    gw["w_in"] = _mm(un, dproj, "tn", BF16, "d_w_in")
    dun = _mm(dproj, wts["w_in"], "nt", F32, "d_un_proj", add=dun)
    dx1, gs["mix_pre"] = _rowwise_bwd(_norm_res_fn, [(x1, 0)], [(small["mix_pre"], None)], [(dx2, 0), (dun, 0)], [0], [F32],
                                      "mix_pre_bwd", tr, D, 1)

    dmem_kv = jnp.concatenate([dmk, dmv], axis=1)
    gw["w_mem_kv"] = _mm(memn, dmem_kv, "tn", BF16, "d_w_mem_kv")
    dmemn = _mm(dmem_kv, wts["w_mem_kv"], "nt", F32, "d_memn")
    _, gs["mem_norm"] = _rowwise_bwd(_norm_fn, [(mem, 0)], [(small["mem_norm"], None)], [(dmemn, 0)], [0], [BF16],
                                     "mem_norm_bwd", mem.shape[0], D, 1)

    dx, (gw["ffn1_wg"], gw["ffn1_wu"], gw["ffn1_wd"]), gs["ffn1_pre"], gs["ffn1_post"] = _ffn_backward(
        dx1, ffn1_saved, small["ffn1_pre"], small["ffn1_post"], wts["ffn1_wg"], wts["ffn1_wu"], wts["ffn1_wd"], "ffn1")
    return loss_part, dx, gw, gs


BIG = ("ffn1_wg", "ffn1_wu", "ffn1_wd", "w_in", "w_mem_kv", "w_hgrn_out", "w_fox_out", "w_mem_out", "w_gate", "w_o",
       "ffn2_wg", "ffn2_wu", "ffn2_wd")
TRANSPOSED = ("ffn1_wg", "ffn1_wu", "ffn2_wg", "ffn2_wu", "w_hgrn_out", "w_fox_out", "w_mem_out", "w_gate")
FFN_PAD = {"ffn1_wg": FP - F, "ffn1_wu": FP - F, "ffn1_wd": FP - F, "ffn2_wg": FP - F, "ffn2_wu": FP - F,
           "ffn2_wd": FP - F}
SMALL = GAINS + ("hgrn_lb", "hgrn_gnorm", "fox_fb")
WEIGHTS = ("ffn1_pre", "ffn1_post", "ffn1_wg", "ffn1_wu", "ffn1_wd", "mix_pre", "mix_post", "mem_norm", "w_in", "hgrn_lb",
           "hgrn_gnorm", "fox_fb", "w_mem_kv", "w_hgrn_out", "w_fox_out", "w_mem_out", "w_gate", "w_o", "ffn2_pre",
           "ffn2_post", "ffn2_wg", "ffn2_wu", "ffn2_wd")


def _to_gather_layout(name, w):
    if name in TRANSPOSED:
        w = w.T
    if name == "w_in":
        r = w.shape[0]
        w = jnp.concatenate([w[:, :MQ_COL], jnp.zeros((r, FF_COL + HD - MQ_COL), w.dtype), w[:, MQ_COL:],
                             jnp.zeros((r, P - FF_COL - HD - WM), w.dtype)], axis=1)
    return w.astype(BF16)


def _from_gather_layout(name, g):
    if name == "w_in":
        g = jnp.concatenate([g[:, :MQ_COL], g[:, FF_COL + HD:FF_COL + HD + WM]], axis=1)
    if name in TRANSPOSED:
        g = g.T
    return g


def kernel(x, mem, ffn1_pre, ffn1_post, ffn1_wg, ffn1_wu, ffn1_wd, mix_pre, mix_post, mem_norm, w_in, hgrn_lb, hgrn_gnorm, fox_fb, w_mem_kv, w_hgrn_out, w_fox_out, w_mem_out, w_gate, w_o, ffn2_pre, ffn2_post, ffn2_wg, ffn2_wu, ffn2_wd, loss_target, m_ffn1_pre, m_ffn1_post, m_ffn1_wg, m_ffn1_wu, m_ffn1_wd, m_mix_pre, m_mix_post, m_mem_norm, m_w_in, m_hgrn_lb, m_hgrn_gnorm, m_fox_fb, m_w_mem_kv, m_w_hgrn_out, m_w_fox_out, m_w_mem_out, m_w_gate, m_w_o, m_ffn2_pre, m_ffn2_post, m_ffn2_wg, m_ffn2_wu, m_ffn2_wd, v_ffn1_pre, v_ffn1_post, v_ffn1_wg, v_ffn1_wu, v_ffn1_wd, v_mix_pre, v_mix_post, v_mem_norm, v_w_in, v_hgrn_lb, v_hgrn_gnorm, v_fox_fb, v_w_mem_kv, v_w_hgrn_out, v_w_fox_out, v_w_mem_out, v_w_gate, v_w_o, v_ffn2_pre, v_ffn2_post, v_ffn2_wg, v_ffn2_wu, v_ffn2_wd):
    a = dict(locals())
    small = {n: a[n] for n in SMALL}
    shard = {n: a[n][0] if a[n].ndim == 3 else a[n] for n in BIG}

    blocks = [_to_gather_layout(n, shard[n]) for n in BIG]
    gathered = _all_gather(blocks, [FFN_PAD.get(n, 0) for n in BIG])
    wts = dict(zip(BIG, gathered))

    loss_part, dx, gw, gs = _local_step(x[0], mem[0], loss_target[0], small, wts)
    loss = lax.psum(0.5 / D * jnp.sum(loss_part), ("x", "y", "c"))

    recv = _reduce_scatter_exchange([gw[n] for n in BIG], [b.shape[0] for b in blocks])
    grads, deltas, new_m, new_v = {}, {}, {}, {}
    for n, rb in zip(BIG, recv):
        g = _from_gather_layout(n, _sum_slots(rb, f"sum_{n}"))
        d, m2, v2 = _adamw(shard[n], g, a["m_" + n].reshape(g.shape), a["v_" + n].reshape(g.shape), f"adamw_{n}")
        full = a[n].shape
        grads[n], deltas[n], new_m[n], new_v[n] = g.reshape(full), d.reshape(full), m2.reshape(full), v2.reshape(full)

    part = jnp.concatenate([_rows8(gs[n]) for n in GAINS] + [_rows8(gs["hgrn_lb"]), _rows8(gs["hgrn_gnorm"]),
                                                             _rows8(gs["fox_fb"][:, :NH])], axis=0)
    gsum = _all_reduce_small(part)

    def packed(prefix):
        lb = a[prefix + "hgrn_lb"]
        return _pack_small([a[prefix + n] for n in GAINS], lb[0], lb[1], a[prefix + "hgrn_gnorm"], a[prefix + "fox_fb"])

    g_p, d_p, m_p, v_p = _small_update(gsum, packed(""), packed("m_"), packed("v_"))
    for dst, p in ((grads, g_p), (deltas, d_p), (new_m, m_p), (new_v, v_p)):
        dst.update(_unpack_small(p))

    return (loss, dx[None], *[grads[n] for n in WEIGHTS], *[deltas[n] for n in WEIGHTS],
            *[new_m[n] for n in WEIGHTS], *[new_v[n] for n in WEIGHTS])
```

```python
import functools

import jax
import jax.numpy as jnp
from jax import lax
from jax.experimental import pallas as pl
from jax.experimental.pallas import tpu as pltpu

F32 = jnp.float32
BF16 = jnp.bfloat16
HIGHEST = lax.Precision.HIGHEST

NDEV = 8
D = 2048
F = 5504
FP = 5632
HD = 128
NH = 6
NM = 4
WH = NH * HD
WM = NM * HD
P = 6144
FF_COL = 5376
MQ_COL = 5382
CHUNK = 64
EPS = 1e-6
SCALE = HD ** -0.5
NEG = -1e30
VMEM_LIMIT = 48 * 1024 * 1024

CB_HQ, CB_HF, CB_HI, CB_HOG, CB_FQ, CB_FK, CB_FV, CB_FF, CB_MQ = 0, 6, 12, 18, 24, 30, 36, 42, 43

ADAM_LR, ADAM_B1, ADAM_B2, ADAM_EPS, ADAM_WD, ADAM_STEP = 0.001, 0.9, 0.999, 1e-08, 0.01, 10

NT = (((1,), (1,)), ((), ()))
NN = (((1,), (0,)), ((), ()))
TN = (((0,), (0,)), ((), ()))
MESH = pl.DeviceIdType.MESH


def _params(sem=None, **kw):
    return pltpu.CompilerParams(dimension_semantics=sem, vmem_limit_bytes=VMEM_LIMIT, **kw)


def _tile(n, prefs):
    for p in prefs:
        if p <= n and n % p == 0:
            return p
    return n


def _dot(a, b, dims):
    return lax.dot_general(a.astype(BF16), b.astype(BF16), dims, preferred_element_type=F32)


def _mm(a, b, mode, out_dtype, name, add=None):
    if mode == "nn":
        (M, K), (K2, N) = a.shape, b.shape
    elif mode == "nt":
        (M, K), (N, K2) = a.shape, b.shape
    else:
        (K, M), (K2, N) = a.shape, b.shape
    assert K == K2, (a.shape, b.shape, mode)
    if mode == "tn":
        tm = _tile(M, (512, 256, 128))
        tn = _tile(N, (1024, 768, 512, 256, 128))
        tk = _tile(K, (4096, 2048, 1024, 512, 256, 128))
    else:
        tm = _tile(M, (1024, 512, 256, 128)) if K <= 2048 else _tile(M, (512, 256, 128))
        tn = _tile(N, (512, 768, 256, 128))
        tk = K if K <= 6144 else _tile(K, (2048, 1024, 512, 256, 128))
    nk = K // tk
    dims = {"nn": NN, "nt": NT, "tn": TN}[mode]
    has_add = add is not None

    def body(*refs):
        a_ref, b_ref = refs[0], refs[1]
        c_ref = refs[2] if has_add else None
        o_ref = refs[3] if has_add else refs[2]
        acc_ref = refs[-1]
        k = pl.program_id(2)
        part = _dot(a_ref[...], b_ref[...], dims)

        def finish(r):
            if has_add:
                r = r + c_ref[...].astype(F32)
            o_ref[...] = r.astype(o_ref.dtype)

        if nk == 1:
            finish(part)
        else:
            @pl.when(k == 0)
            def _():
                acc_ref[...] = part

            @pl.when(k > 0)
            def _():
                acc_ref[...] += part

            @pl.when(k == nk - 1)
            def _():
                finish(acc_ref[...])

    if mode == "nn":
        a_spec = pl.BlockSpec((tm, tk), lambda i, j, k: (i, k))
        b_spec = pl.BlockSpec((tk, tn), lambda i, j, k: (k, j))
    elif mode == "nt":
        a_spec = pl.BlockSpec((tm, tk), lambda i, j, k: (i, k))
        b_spec = pl.BlockSpec((tn, tk), lambda i, j, k: (j, k))
    else:
        a_spec = pl.BlockSpec((tk, tm), lambda i, j, k: (k, i))
        b_spec = pl.BlockSpec((tk, tn), lambda i, j, k: (k, j))
    o_spec = pl.BlockSpec((tm, tn), lambda i, j, k: (i, j))
    in_specs = [a_spec, b_spec] + ([o_spec] if has_add else [])
    args = (a, b) + ((add,) if has_add else ())
    return pl.pallas_call(
        body, name=name, grid=(M // tm, N // tn, nk), in_specs=in_specs, out_specs=o_spec,
        out_shape=jax.ShapeDtypeStruct((M, N), out_dtype),
        scratch_shapes=[pltpu.VMEM((tm, tn) if nk > 1 else (8, 128), F32)],
        compiler_params=_params(("parallel", "parallel", "arbitrary")),
    )(*args)


def _ffn_up(n, wg_t, wu_t, name):
    T = n.shape[0]
    tm = _tile(T, (1024, 512, 256, 128))
    tn = 512

    def body(n_ref, wg_ref, wu_ref, g_ref, u_ref, a_ref):
        x = n_ref[...]
        g = _dot(x, wg_ref[...], NT)
        u = _dot(x, wu_ref[...], NT)
        g_ref[...] = g
        u_ref[...] = u
        a_ref[...] = (g * jax.nn.sigmoid(g) * u).astype(BF16)

    w_spec = pl.BlockSpec((tn, D), lambda i, j: (j, 0))
    o_spec = pl.BlockSpec((tm, tn), lambda i, j: (i, j))
    return pl.pallas_call(
        body, name=name, grid=(T // tm, FP // tn),
        in_specs=[pl.BlockSpec((tm, D), lambda i, j: (i, 0)), w_spec, w_spec],
        out_specs=[o_spec, o_spec, o_spec],
        out_shape=[jax.ShapeDtypeStruct((T, FP), F32), jax.ShapeDtypeStruct((T, FP), F32),
                   jax.ShapeDtypeStruct((T, FP), BF16)],
        compiler_params=_params(("parallel", "parallel")),
    )(n, wg_t, wu_t)


def _row_specs(rows, tr, cw):
    return [pl.BlockSpec((tr, cw), lambda j, i, o=off: (i, o + j)) for _, off in rows]


def _const_specs(consts, cw):
    specs = []
    for arr, off in consts:
        if off is None:
            specs.append(pl.BlockSpec(arr.shape, lambda j, i: (0, 0)))
        else:
            specs.append(pl.BlockSpec((arr.shape[0], cw), lambda j, i, o=off: (0, o + j)))
    return specs


def _rowwise(fn, rows, consts, out_dtypes, name, tr, cw, ncol):
    T = rows[0][0].shape[0]
    nr, nc = len(rows), len(consts)

    def body(*refs):
        r = [x[...].astype(F32) for x in refs[:nr]]
        c = [x[...] for x in refs[nr:nr + nc]]
        res = fn(*r, *c)
        for o_ref, v in zip(refs[nr + nc:], res):
            o_ref[...] = v.astype(o_ref.dtype)

    o_spec = pl.BlockSpec((tr, cw), lambda j, i: (i, j))
    return pl.pallas_call(
        body, name=name, grid=(ncol, T // tr),
        in_specs=_row_specs(rows, tr, cw) + _const_specs(consts, cw),
        out_specs=[o_spec] * len(out_dtypes),
        out_shape=[jax.ShapeDtypeStruct((T, ncol * cw), dt) for dt in out_dtypes],
        compiler_params=_params(("parallel", "parallel")),
    )(*[a for a, _ in rows], *[a for a, _ in consts])


def _rowwise_bwd(fn, rows, consts, cots, diff, ddtypes, name, tr, cw, ncol):
    T = rows[0][0].shape[0]
    nr, nc, nt, nd = len(rows), len(consts), len(cots), len(diff)

    def body(*refs):
        r = [x[...].astype(F32) for x in refs[:nr]]
        c = [x[...] for x in refs[nr:nr + nc]]
        ct = [x[...].astype(F32) for x in refs[nr + nc:nr + nc + nt]]
        drow_refs = refs[nr + nc + nt:nr + nc + nt + nd]
        dconst_refs = refs[nr + nc + nt + nd:]
        i = pl.program_id(1)

        def f(*args):
            full = list(r)
            for idx, a in zip(diff, args[:nd]):
                full[idx] = a
            return tuple(fn(*full, *args[nd:]))

        _, vjp = jax.vjp(f, *[r[d] for d in diff], *c)
        g = vjp(tuple(ct))
        for o_ref, v in zip(drow_refs, g[:nd]):
            o_ref[...] = v.astype(o_ref.dtype)

        @pl.when(i == 0)
        def _():
            for o_ref in dconst_refs:
                o_ref[...] = jnp.zeros_like(o_ref)

        for o_ref, v in zip(dconst_refs, g[nd:]):
            o_ref[...] += v

    o_spec = pl.BlockSpec((tr, cw), lambda j, i: (i, j))
    out_shape = [jax.ShapeDtypeStruct((T, ncol * cw), dt) for dt in ddtypes]
    out_shape += [jax.ShapeDtypeStruct(a.shape, F32) for a, _ in consts]
    return pl.pallas_call(
        body, name=name, grid=(ncol, T // tr),
        in_specs=_row_specs(rows, tr, cw) + _const_specs(consts, cw) + _row_specs(cots, tr, cw),
        out_specs=[o_spec] * nd + _const_specs(consts, cw),
        out_shape=out_shape,
        compiler_params=_params(("parallel", "arbitrary")),
    )(*[a for a, _ in rows], *[a for a, _ in consts], *[a for a, _ in cots])


def _rms(x, g):
    return x * lax.rsqrt(jnp.mean(x * x, axis=-1, keepdims=True) + EPS) * g


def _silu(x):
    return x * jax.nn.sigmoid(x)


def _norm_fn(x, g):
    return (_rms(x, g),)


def _norm_res_fn(x, g):
    return (x, _rms(x, g))


def _resid_fn(scale, x, h, g):
    return (x + scale * _rms(h, g),)


def _resid_h_fn(scale, h, g):
    return (scale * _rms(h, g),)


def _hpost_fn(o, hog, gn):
    return (_rms(o, gn) * _silu(hog),)


def _merge_fn(z0, z1, z2, yh, yf, ym):
    return (jax.nn.sigmoid(z0) * yh + jax.nn.sigmoid(z1) * yf + jax.nn.sigmoid(z2) * ym,)


def _swiglu_fn(g, u):
    return (_silu(g) * u,)


def _loss(x3, tgt, name):
    T = x3.shape[0]
    tr = _tile(T, (256, 128))

    def body(x_ref, t_ref, dy_ref, s_ref):
        i = pl.program_id(0)
        e = x_ref[...] - t_ref[...]
        dy_ref[...] = e * (1.0 / D)
        col = jnp.sum(e * e, axis=0, keepdims=True)
        tot = col[:, 0:HD]
        for k in range(1, D // HD):
            tot = tot + col[:, k * HD:(k + 1) * HD]

        @pl.when(i == 0)
        def _():
            s_ref[...] = jnp.zeros_like(s_ref)

        s_ref[...] += tot

    spec = pl.BlockSpec((tr, D), lambda i: (i, 0))
    return pl.pallas_call(
        body, name=name, grid=(T // tr,), in_specs=[spec, spec],
        out_specs=[spec, pl.BlockSpec((1, HD), lambda i: (0, 0))],
        out_shape=[jax.ShapeDtypeStruct((T, D), F32), jax.ShapeDtypeStruct((1, HD), F32)],
        compiler_params=_params(("arbitrary",)),
    )(x3, tgt)


def _lower_bound(lb_ref):
    a0 = lb_ref[0:1, :]
    a1 = lb_ref[1:2, :]
    mx = jnp.maximum(a0, a1)
    e0 = jnp.exp(a0 - mx)
    return e0 / (e0 + jnp.exp(a1 - mx))


def _hgrn_prep(hq, hf, lb):
    g = lb + (1.0 - lb) * jax.nn.sigmoid(hf)
    return _silu(hq), 1.0 - g, jnp.log(g)


def _tri(n, upper):
    r = lax.broadcasted_iota(jnp.int32, (n, n), 0)
    c = lax.broadcasted_iota(jnp.int32, (n, n), 1)
    return (c >= r) if upper else (c <= r)


def _hgrn_factors(q, k, gl):
    low = _tri(CHUNK, False)
    b = lax.dot_general(low.astype(F32), gl, NN, precision=HIGHEST, preferred_element_type=F32)
    bl = b[CHUNK - 1:CHUNK, :]
    ref = b[CHUNK // 2 - 1:CHUNK // 2, :]
    eb = jnp.exp(b)
    ea = jnp.exp(b - ref)
    ebn = jnp.exp(ref - b)
    ek = jnp.exp(bl - b)
    ebl = jnp.exp(bl)
    return low, eb, ea, ebn, ek, ebl


def _hgrn_fwd(proj, hgrn_lb):
    T = proj.shape[0]
    cb = _tile(T, (512, 256, 128, 64))
    nchunk = cb // CHUNK

    def body(hq_ref, hf_ref, hi_ref, lb_ref, o_ref, st_ref, state):
        @pl.when(pl.program_id(0) == 0)
        def _():
            state[...] = jnp.zeros_like(state)

        lb = _lower_bound(lb_ref)

        def chunk(c, carry):
            r0 = pl.multiple_of(c * CHUNK, CHUNK)
            for h in range(NH):
                cols = slice(h * HD, (h + 1) * HD)
                q, k, gl = _hgrn_prep(hq_ref[pl.ds(r0, CHUNK), cols], hf_ref[pl.ds(r0, CHUNK), cols], lb[:, cols])
                v = hi_ref[pl.ds(r0, CHUNK), cols]
                low, eb, ea, ebn, ek, ebl = _hgrn_factors(q, k, gl)
                s_t = state[h]
                st_ref[c, h] = s_t
                pm = jnp.where(low, _dot(q * ea, k * ebn, NT), 0.0)
                o_ref[pl.ds(r0, CHUNK), cols] = _dot(q * eb, s_t, NT) + _dot(pm, v, NN)
                state[h] = s_t * ebl + _dot(v, k * ek, TN)
            return carry

        lax.fori_loop(0, nchunk, chunk, 0)

    def col(off):
        return pl.BlockSpec((cb, WH), lambda i, o=off: (i, o))

    return pl.pallas_call(
        body, name="hgrn_fwd", grid=(T // cb,),
        in_specs=[col(0), col(1), col(2), pl.BlockSpec((2, WH), lambda i: (0, 0))],
        out_specs=[pl.BlockSpec((cb, WH), lambda i: (i, 0)),
                   pl.BlockSpec((nchunk, NH, HD, HD), lambda i: (i, 0, 0, 0))],
        out_shape=[jax.ShapeDtypeStruct((T, WH), F32), jax.ShapeDtypeStruct((T // CHUNK, NH, HD, HD), F32)],
        scratch_shapes=[pltpu.VMEM((NH, HD, HD), F32)],
        compiler_params=_params(("arbitrary",)),
    )(proj, proj, proj, hgrn_lb)


def _hgrn_bwd(proj, hgrn_lb, states, do):
    T = proj.shape[0]
    cb = _tile(T, (512, 256, 128, 64))
    nchunk = cb // CHUNK
    nb = T // cb

    def body(hq_ref, hf_ref, hi_ref, lb_ref, st_ref, do_ref, dhq_ref, dhf_ref, dhi_ref, dlb_ref, dstate):
        @pl.when(pl.program_id(0) == 0)
        def _():
            dstate[...] = jnp.zeros_like(dstate)
            dlb_ref[...] = jnp.zeros_like(dlb_ref)

        lb = _lower_bound(lb_ref)
        up = _tri(CHUNK, True)
        last = lax.broadcasted_iota(jnp.int32, (CHUNK, HD), 0) == CHUNK - 1

        def chunk(cc, carry):
            c = nchunk - 1 - cc
            r0 = pl.multiple_of(c * CHUNK, CHUNK)
            for h in range(NH):
                cols = slice(h * HD, (h + 1) * HD)
                hq = hq_ref[pl.ds(r0, CHUNK), cols]
                hf = hf_ref[pl.ds(r0, CHUNK), cols]
                (q, k, gl), prep_vjp = jax.vjp(_hgrn_prep, hq, hf, lb[:, cols])
                v = hi_ref[pl.ds(r0, CHUNK), cols]
                d_o = do_ref[pl.ds(r0, CHUNK), cols]
                low, eb, ea, ebn, ek, ebl = _hgrn_factors(q, k, gl)
                s_t = st_ref[c, h]
                ds_new = dstate[h]
                qe, am, bm, kb = q * eb, q * ea, k * ebn, k * ek
                pm_t = jnp.where(up, _dot(bm, am, NT), 0.0)
                dp = jnp.where(low, _dot(d_o, v, NT), 0.0)
                dp_t = jnp.where(up, _dot(v, d_o, NT), 0.0)
                dqe = _dot(d_o, s_t, NN)
                da = _dot(dp, bm, NN)
                db_m = _dot(dp_t, am, NN)
                dkb = _dot(v, ds_new, NN)
                dv = _dot(pm_t, d_o, NN) + _dot(kb, ds_new, NT)
                dq = dqe * eb + da * ea
                dk = db_m * ebn + dkb * ek
                dbl = jnp.sum(dkb * kb, axis=0, keepdims=True) + jnp.sum(ds_new * s_t, axis=0, keepdims=True) * ebl
                db = (dqe * qe + da * am.astype(BF16).astype(F32) - db_m * bm.astype(BF16).astype(F32) - dkb * kb
                      + jnp.where(last, dbl, 0.0))
                dgl = lax.dot_general(up.astype(F32), db, NN, precision=HIGHEST, preferred_element_type=F32)
                dhq, dhf, dlb = prep_vjp((dq, dk, dgl))
                dhq_ref[pl.ds(r0, CHUNK), cols] = dhq.astype(dhq_ref.dtype)
                dhf_ref[pl.ds(r0, CHUNK), cols] = dhf.astype(dhf_ref.dtype)
                dhi_ref[pl.ds(r0, CHUNK), cols] = dv.astype(dhi_ref.dtype)
                dlb_ref[:, cols] += dlb
                dstate[h] = _dot(d_o, qe, TN) + ds_new * ebl
            return carry

        lax.fori_loop(0, nchunk, chunk, 0)

    def col(off):
        return pl.BlockSpec((cb, WH), lambda i, o=off: (nb - 1 - i, o))

    row = pl.BlockSpec((cb, WH), lambda i: (nb - 1 - i, 0))
    return pl.pallas_call(
        body, name="hgrn_bwd", grid=(nb,),
        in_specs=[col(0), col(1), col(2), pl.BlockSpec((2, WH), lambda i: (0, 0)),
                  pl.BlockSpec((nchunk, NH, HD, HD), lambda i: (nb - 1 - i, 0, 0, 0)), row],
        out_specs=[row, row, row, pl.BlockSpec((1, WH), lambda i: (0, 0))],
        out_shape=[jax.ShapeDtypeStruct((T, WH), BF16)] * 3 + [jax.ShapeDtypeStruct((1, WH), F32)],
        scratch_shapes=[pltpu.VMEM((NH, HD, HD), F32)],
        compiler_params=_params(("arbitrary",)),
    )(proj, proj, proj, hgrn_lb, states, do)


def _log_sigmoid(z):
    return jnp.minimum(z, 0.0) - jnp.log(1.0 + jnp.exp(-jnp.abs(z)))


def _fox_cum(proj, fb_pad):
    T = proj.shape[0]
    tb = _tile(T, (256, 128))

    def body(ff_ref, fb_ref, ct_ref, cq_ref, carry):
        @pl.when(pl.program_id(0) == 0)
        def _():
            carry[...] = jnp.zeros_like(carry)

        lf = _log_sigmoid(ff_ref[...] + fb_ref[...])
        cs = lax.dot_general(_tri(tb, False).astype(F32), lf, NN, precision=HIGHEST,
                             preferred_element_type=F32) + carry[0:1, :]
        carry[0:1, :] = cs[tb - 1:tb, :]
        ct_ref[...] = cs.T[0:8, :]
        for h in range(NH):
            cq_ref[h] = jnp.broadcast_to(cs[:, h:h + 1], (tb, HD))

    return pl.pallas_call(
        body, name="fox_cum", grid=(T // tb,),
        in_specs=[pl.BlockSpec((tb, HD), lambda i: (i, CB_FF)), pl.BlockSpec((1, HD), lambda i: (0, 0))],
        out_specs=[pl.BlockSpec((8, tb), lambda i: (0, i)), pl.BlockSpec((NH, tb, HD), lambda i: (0, i, 0))],
        out_shape=[jax.ShapeDtypeStruct((8, T), F32), jax.ShapeDtypeStruct((NH, T, HD), F32)],
        scratch_shapes=[pltpu.VMEM((8, HD), F32)],
        compiler_params=_params(("arbitrary",)),
    )(proj, fb_pad)


def _fox_cum_bwd(dc, proj, fb_pad):
    T = proj.shape[0]
    tb = _tile(T, (256, 128))
    nb = T // tb

    def body(dc_ref, ff_ref, fb_ref, dff_ref, dfb_ref, carry):
        @pl.when(pl.program_id(0) == 0)
        def _():
            carry[...] = jnp.zeros_like(carry)
            dfb_ref[...] = jnp.zeros_like(dfb_ref)

        rid = lax.broadcasted_iota(jnp.int32, (8, tb), 0)
        m8 = jnp.zeros((8, tb), F32)
        for h in range(NH):
            m8 = m8 + jnp.where(rid == h, dc_ref[h], 0.0)
        dcb = jnp.concatenate([m8, jnp.zeros((HD - 8, tb), F32)], axis=0).T
        rev = lax.dot_general(_tri(tb, True).astype(F32), dcb, NN, precision=HIGHEST,
                              preferred_element_type=F32) + carry[0:1, :]
        carry[0:1, :] = rev[0:1, :]
        dff = rev * jax.nn.sigmoid(-(ff_ref[...] + fb_ref[...]))
        dff_ref[...] = dff.astype(dff_ref.dtype)
        dfb_ref[...] += jnp.sum(dff, axis=0, keepdims=True)

    return pl.pallas_call(
        body, name="fox_cum_bwd", grid=(nb,),
        in_specs=[pl.BlockSpec((NH, 8, tb), lambda i: (0, 0, nb - 1 - i)),
                  pl.BlockSpec((tb, HD), lambda i: (nb - 1 - i, CB_FF)), pl.BlockSpec((1, HD), lambda i: (0, 0))],
        out_specs=[pl.BlockSpec((tb, HD), lambda i: (nb - 1 - i, 0)), pl.BlockSpec((1, HD), lambda i: (0, 0))],
        out_shape=[jax.ShapeDtypeStruct((T, HD), BF16), jax.ShapeDtypeStruct((1, HD), F32)],
        scratch_shapes=[pltpu.VMEM((8, HD), F32)],
        compiler_params=_params(("arbitrary",)),
    )(dc, proj, fb_pad)


def _fox_scores(q, k, cq, ck, i, j, bq, bk):
    s = _dot(q, k, NT) * SCALE + (cq - ck)
    diff = lax.broadcasted_iota(jnp.int32, (bq, bk), 1) - lax.broadcasted_iota(jnp.int32, (bq, bk), 0)
    return jnp.where(diff <= i * bq - j * bk, s, NEG)


def _heads(h):
    return slice(h * HD, (h + 1) * HD)


def _fox_fwd(proj, ct, cq):
    T = proj.shape[0]
    bq = bk = _tile(T, (512, 256, 128))
    nq = nk = T // bq

    def body(q_ref, k_ref, v_ref, ct_ref, cq_ref, o_ref, lse_ref, m_s, l_s, acc_s):
        i, j = pl.program_id(0), pl.program_id(1)

        @pl.when(j == 0)
        def _():
            m_s[...] = jnp.full_like(m_s, NEG)
            l_s[...] = jnp.zeros_like(l_s)
            acc_s[...] = jnp.zeros_like(acc_s)

        @pl.when(j <= i)
        def _():
            for h in range(NH):
                hs = _heads(h)
                s = _fox_scores(q_ref[:, hs], k_ref[:, hs], cq_ref[h, :, 0:1], ct_ref[h:h + 1, :], i, j, bq, bk)
                m_prev = m_s[h]
                m_new = jnp.maximum(m_prev, jnp.max(s, axis=1, keepdims=True))
                alpha = jnp.exp(m_prev - m_new)
                p = jnp.exp(s - m_new)
                l_s[h] = alpha * l_s[h] + jnp.sum(p, axis=1, keepdims=True)
                acc_s[:, hs] = alpha * acc_s[:, hs] + _dot(p, v_ref[:, hs], NN)
                m_s[h] = m_new

        @pl.when(j == nk - 1)
        def _():
            for h in range(NH):
                o_ref[:, _heads(h)] = acc_s[:, _heads(h)] / l_s[h]
                lse_ref[h] = jnp.broadcast_to(m_s[h] + jnp.log(l_s[h]), (bq, HD))

    def kv(off):
        return pl.BlockSpec((bk, WH), lambda i, j, o=off // NH: (jnp.minimum(j, i), o))

    stat = pl.BlockSpec((NH, bq, HD), lambda i, j: (0, i, 0))
    return pl.pallas_call(
        body, name="fox_fwd", grid=(nq, nk),
        in_specs=[pl.BlockSpec((bq, WH), lambda i, j: (i, CB_FQ // NH)), kv(CB_FK), kv(CB_FV),
                  pl.BlockSpec((8, bk), lambda i, j: (0, jnp.minimum(j, i))), stat],
        out_specs=[pl.BlockSpec((bq, WH), lambda i, j: (i, 0)), stat],
        out_shape=[jax.ShapeDtypeStruct((T, WH), F32), jax.ShapeDtypeStruct((NH, T, HD), F32)],
        scratch_shapes=[pltpu.VMEM((NH, bq, 1), F32), pltpu.VMEM((NH, bq, 1), F32), pltpu.VMEM((bq, WH), F32)],
        compiler_params=_params(("parallel", "arbitrary")),
    )(proj, proj, proj, ct, cq)


def _fox_bwd_dq(proj, ct, cq, lse, do):
    T = proj.shape[0]
    bq = bk = _tile(T, (512, 256, 128))
    nq = nk = T // bq

    def body(q_ref, k_ref, v_ref, ct_ref, cq_ref, lse_ref, do_ref, dq_ref, delta_ref, acc_s, delta_s, psum_s):
        i, jj = pl.program_id(0), pl.program_id(1)
        j = jj % nk

        @pl.when(jj == 0)
        def _():
            acc_s[...] = jnp.zeros_like(acc_s)
            delta_s[...] = jnp.zeros_like(delta_s)
            psum_s[...] = jnp.zeros_like(psum_s)

        def probs(h):
            hs = _heads(h)
            k = k_ref[:, hs]
            s = _fox_scores(q_ref[:, hs], k, cq_ref[h, :, 0:1], ct_ref[h:h + 1, :], i, j, bq, bk)
            return k, jnp.exp(s - lse_ref[h, :, 0:1]), _dot(do_ref[:, hs], v_ref[:, hs], NT)

        @pl.when((j <= i) & (jj < nk))
        def _():
            for h in range(NH):
                _, p, dp = probs(h)
                delta_s[h] += jnp.sum(p * dp, axis=1, keepdims=True)
                psum_s[h] += jnp.sum(p, axis=1, keepdims=True)

        @pl.when((j <= i) & (jj >= nk))
        def _():
            for h in range(NH):
                k, p, dp = probs(h)
                ds = p * (dp - delta_s[h] / psum_s[h])
                acc_s[:, _heads(h)] += _dot(ds, k, NN) * SCALE

        @pl.when(jj == 2 * nk - 1)
        def _():
            dq_ref[...] = acc_s[...].astype(dq_ref.dtype)
            for h in range(NH):
                delta_ref[h] = jnp.broadcast_to(delta_s[h] / psum_s[h], (bq, HD))

    def kv(off):
        return pl.BlockSpec((bk, WH), lambda i, jj, o=off // NH: (jnp.minimum(jj % nk, i), o))

    qrow = pl.BlockSpec((bq, WH), lambda i, jj: (i, 0))
    stat = pl.BlockSpec((NH, bq, HD), lambda i, jj: (0, i, 0))
    return pl.pallas_call(
        body, name="fox_bwd_dq", grid=(nq, 2 * nk),
        in_specs=[pl.BlockSpec((bq, WH), lambda i, jj: (i, CB_FQ // NH)), kv(CB_FK), kv(CB_FV),
                  pl.BlockSpec((8, bk), lambda i, jj: (0, jnp.minimum(jj % nk, i))), stat, stat, qrow],
        out_specs=[qrow, stat],
        out_shape=[jax.ShapeDtypeStruct((T, WH), BF16), jax.ShapeDtypeStruct((NH, T, HD), F32)],
        scratch_shapes=[pltpu.VMEM((bq, WH), F32), pltpu.VMEM((NH, bq, 1), F32), pltpu.VMEM((NH, bq, 1), F32)],
        compiler_params=_params(("parallel", "arbitrary")),
    )(proj, proj, proj, ct, cq, lse, do)


def _fox_bwd_dkv(proj, ct, cq, lse, delta, do):
    T = proj.shape[0]
    bq = bk = _tile(T, (512, 256, 128))
    nq = nk = T // bq

    def body(q_ref, k_ref, v_ref, ct_ref, cq_ref, lse_ref, delta_ref, do_ref, dk_ref, dv_ref, dc_ref, dk_s, dv_s, dc_s):
        j, i = pl.program_id(0), pl.program_id(1)

        @pl.when(i == 0)
        def _():
            dk_s[...] = jnp.zeros_like(dk_s)
            dv_s[...] = jnp.zeros_like(dv_s)
            dc_s[...] = jnp.zeros_like(dc_s)

        @pl.when(i >= j)
        def _():
            for h in range(NH):
                hs = _heads(h)
                q = q_ref[:, hs]
                d_o = do_ref[:, hs]
                s = _fox_scores(q, k_ref[:, hs], cq_ref[h, :, 0:1], ct_ref[h:h + 1, :], i, j, bq, bk)
                p = jnp.exp(s - lse_ref[h, :, 0:1])
                dv_s[:, hs] += _dot(p, d_o, TN)
                dp = _dot(d_o, v_ref[:, hs], NT)
                ds = p * (dp - delta_ref[h, :, 0:1])
                dk_s[:, hs] += _dot(ds, q, TN) * SCALE
                dc_s[h:h + 1, :] -= jnp.sum(ds, axis=0, keepdims=True)

        @pl.when(i == nq - 1)
        def _():
            dk_ref[...] = dk_s[...].astype(dk_ref.dtype)
            dv_ref[...] = dv_s[...].astype(dv_ref.dtype)
            for h in range(NH):
                dc_ref[h] = jnp.broadcast_to(dc_s[h:h + 1, :], (8, bk))

    def kv(off):
        return pl.BlockSpec((bk, WH), lambda j, i, o=off // NH: (j, o))

    qrow = pl.BlockSpec((bq, WH), lambda j, i: (jnp.maximum(i, j), 0))
    stat = pl.BlockSpec((NH, bq, HD), lambda j, i: (0, jnp.maximum(i, j), 0))
    krow = pl.BlockSpec((bk, WH), lambda j, i: (j, 0))
    return pl.pallas_call(
        body, name="fox_bwd_dkv", grid=(nk, nq),
        in_specs=[pl.BlockSpec((bq, WH), lambda j, i: (jnp.maximum(i, j), CB_FQ // NH)), kv(CB_FK), kv(CB_FV),
                  pl.BlockSpec((8, bk), lambda j, i: (0, j)), stat, stat, stat, qrow],
        out_specs=[krow, krow, pl.BlockSpec((NH, 8, bk), lambda j, i: (0, 0, j))],
        out_shape=[jax.ShapeDtypeStruct((T, WH), BF16), jax.ShapeDtypeStruct((T, WH), BF16),
                   jax.ShapeDtypeStruct((NH, 8, T), F32)],
        scratch_shapes=[pltpu.VMEM((bk, WH), F32), pltpu.VMEM((bk, WH), F32), pltpu.VMEM((8, bk), F32)],
        compiler_params=_params(("parallel", "arbitrary")),
    )(proj, proj, proj, ct, cq, lse, delta, do)


def _mem_probs(q, mk):
    s = _dot(q, mk, NT) * SCALE
    e = jnp.exp(s - jnp.max(s, axis=1, keepdims=True))
    return e / jnp.sum(e, axis=1, keepdims=True)


def _mem_fwd(proj, mem_kv):
    T = proj.shape[0]
    tr = _tile(T, (512, 256, 128))
    M = mem_kv.shape[0]

    def body(q_ref, mk_ref, mv_ref, o_ref):
        o_ref[...] = _dot(_mem_probs(q_ref[...], mk_ref[...]), mv_ref[...], NN)

    return pl.pallas_call(
        body, name="mem_fwd", grid=(NM, T // tr),
        in_specs=[pl.BlockSpec((tr, HD), lambda h, i: (i, CB_MQ + h)),
                  pl.BlockSpec((M, HD), lambda h, i: (0, h)), pl.BlockSpec((M, HD), lambda h, i: (0, NM + h))],
        out_specs=pl.BlockSpec((tr, HD), lambda h, i: (i, h)),
        out_shape=jax.ShapeDtypeStruct((T, WM), F32),
        compiler_params=_params(("parallel", "parallel")),
    )(proj, mem_kv, mem_kv)


def _mem_bwd(proj, mem_kv, do):
    T = proj.shape[0]
    tr = _tile(T, (512, 256, 128))
    M = mem_kv.shape[0]

    def body(q_ref, mk_ref, mv_ref, do_ref, dq_ref, dmk_ref, dmv_ref):
        @pl.when(pl.program_id(1) == 0)
        def _():
            dmk_ref[...] = jnp.zeros_like(dmk_ref)
            dmv_ref[...] = jnp.zeros_like(dmv_ref)

        q, mk, d_o = q_ref[...], mk_ref[...], do_ref[...]
        p = _mem_probs(q, mk)
        dmv_ref[...] += _dot(p, d_o, TN)
        dp = _dot(d_o, mv_ref[...], NT)
        ds = p * (dp - jnp.sum(p * dp, axis=1, keepdims=True))
        dq_ref[...] = (_dot(ds, mk, NN) * SCALE).astype(dq_ref.dtype)
        dmk_ref[...] += _dot(ds, q, TN) * SCALE

    acc = pl.BlockSpec((M, HD), lambda h, i: (0, h))
    row = pl.BlockSpec((tr, HD), lambda h, i: (i, h))
    return pl.pallas_call(
        body, name="mem_bwd", grid=(NM, T // tr),
        in_specs=[pl.BlockSpec((tr, HD), lambda h, i: (i, CB_MQ + h)),
                  pl.BlockSpec((M, HD), lambda h, i: (0, h)), pl.BlockSpec((M, HD), lambda h, i: (0, NM + h)), row],
        out_specs=[row, acc, acc],
        out_shape=[jax.ShapeDtypeStruct((T, WM), BF16), jax.ShapeDtypeStruct((M, WM), F32),
                   jax.ShapeDtypeStruct((M, WM), F32)],
        compiler_params=_params(("parallel", "arbitrary")),
    )(proj, mem_kv, mem_kv, do)


def _mesh_place():
    x, y, c = lax.axis_index("x"), lax.axis_index("y"), lax.axis_index("c")
    return x, y, c


CHIP_FLIPS = (4, 2, 6)
CHIP_OF_SLOT = (0,) + CHIP_FLIPS


def _peer(x, y, c, k):
    px = 1 - x if k & 4 else x
    py = 1 - y if k & 2 else y
    pc = 1 - c if k & 1 else c
    return (px, py, pc), 4 * px + 2 * py + pc


def _all_gather(shards, pad_rows):
    n = len(shards)
    npad = sum(1 for p in pad_rows if p)
    zeros = jnp.zeros((max(pad_rows) or 16, shards[0].shape[1]), shards[0].dtype)

    def body(*refs):
        ins, z_ref, outs = refs[:n], refs[n], refs[n + 1:2 * n + 1]
        send_sems, recv_sems, loc_sems = refs[2 * n + 1:]
        x, y, c = _mesh_place()
        me = 4 * x + 2 * y + c
        copies, sends = [], []
        ip = 0
        for w in range(n):
            r = ins[w].shape[0]
            dst = outs[w].at[pl.ds(pl.multiple_of(me * r, 16), r), :]
            cp = pltpu.make_async_copy(ins[w], dst, loc_sems.at[w])
            cp.start()
            copies.append(cp)
            if pad_rows[w]:
                cp = pltpu.make_async_copy(z_ref.at[pl.ds(0, pad_rows[w]), :],
                                           outs[w].at[pl.ds(NDEV * r, pad_rows[w]), :], loc_sems.at[n + ip])
                cp.start()
                copies.append(cp)
                ip += 1
            for s, k in enumerate((1,) + CHIP_FLIPS):
                peer, _ = _peer(x, y, c, k)
                cp = pltpu.make_async_remote_copy(src_ref=ins[w], dst_ref=dst, send_sem=send_sems.at[w, s],
                                                  recv_sem=recv_sems.at[w, s], device_id=peer, device_id_type=MESH)
                cp.start()
                (copies if s == 0 else sends).append(cp)
        sibling, _ = _peer(x, y, c, 1)
        for w in range(n):
            r = ins[w].shape[0]
            for s, k in enumerate(CHIP_FLIPS):
                _, pidx = _peer(x, y, c, k)
                rows = outs[w].at[pl.ds(pl.multiple_of(pidx * r, 16), r), :]
                pltpu.make_async_remote_copy(src_ref=rows, dst_ref=rows, send_sem=send_sems.at[w, 1 + s],
                                             recv_sem=recv_sems.at[w, 1 + s], device_id=sibling,
                                             device_id_type=MESH).wait_recv()
                cp = pltpu.make_async_remote_copy(src_ref=rows, dst_ref=rows, send_sem=send_sems.at[w, 4 + s],
                                                  recv_sem=recv_sems.at[w, 4 + s], device_id=sibling, device_id_type=MESH)
                cp.start()
                copies.append(cp)
        for cp in copies:
            cp.wait()
        for cp in sends:
            cp.wait_send()

    any_spec = pl.BlockSpec(memory_space=pl.ANY)
    return pl.pallas_call(
        body, name="all_gather_weights",
        in_specs=[any_spec] * (n + 1), out_specs=[any_spec] * n,
        out_shape=[jax.ShapeDtypeStruct((NDEV * s.shape[0] + p, s.shape[1]), s.dtype) for s, p in zip(shards, pad_rows)],
        scratch_shapes=[pltpu.SemaphoreType.DMA((n, NDEV - 1)), pltpu.SemaphoreType.DMA((n, NDEV - 1)),
                        pltpu.SemaphoreType.DMA((n + npad,))],
        compiler_params=pltpu.CompilerParams(has_side_effects=True),
    )(*shards, zeros)


def _exchange_in_chip(grads, shard_rows):
    n = len(grads)
    ns = len(CHIP_OF_SLOT)

    def body(*refs):
        ins, mine, theirs = refs[:n], refs[n:2 * n], refs[2 * n:3 * n]
        send_sems, recv_sems, loc_sems = refs[3 * n:]
        x, y, c = _mesh_place()
        sibling, _ = _peer(x, y, c, 1)
        copies = []
        for w in range(n):
            r = shard_rows[w]
            for s, k in enumerate(CHIP_OF_SLOT):
                _, own = _peer(x, y, c, k)
                cp = pltpu.make_async_copy(ins[w].at[pl.ds(pl.multiple_of(own * r, 16), r), :], mine[w].at[s],
                                           loc_sems.at[w, s])
                cp.start()
                copies.append(cp)
                _, other = _peer(x, y, c, k | 1)
                cp = pltpu.make_async_remote_copy(
                    src_ref=ins[w].at[pl.ds(pl.multiple_of(other * r, 16), r), :], dst_ref=theirs[w].at[s],
                    send_sem=send_sems.at[w, s], recv_sem=recv_sems.at[w, s], device_id=sibling, device_id_type=MESH)
                cp.start()
                copies.append(cp)
        for cp in copies:
            cp.wait()

    any_spec = pl.BlockSpec(memory_space=pl.ANY)
    shapes = [jax.ShapeDtypeStruct((ns, r, g.shape[1]), g.dtype) for g, r in zip(grads, shard_rows)]
    out = pl.pallas_call(
        body, name="reduce_scatter_in_chip",
        in_specs=[any_spec] * n, out_specs=[any_spec] * (2 * n), out_shape=shapes + shapes,
        scratch_shapes=[pltpu.SemaphoreType.DMA((n, ns)), pltpu.SemaphoreType.DMA((n, ns)),
                        pltpu.SemaphoreType.DMA((n, ns))],
        compiler_params=pltpu.CompilerParams(has_side_effects=True),
    )(*grads)
    return out[:n], out[n:]


def _pair_sum(mine, theirs, name):
    ns, r, c = mine.shape
    tr = _tile(r, (128, 64, 32, 16))

    def body(a_ref, b_ref, o_ref):
        o_ref[...] = (a_ref[...].astype(F32) + b_ref[...].astype(F32)).astype(o_ref.dtype)

    spec = pl.BlockSpec((ns, tr, c), lambda i: (0, i, 0))
    return pl.pallas_call(
        body, name=name, grid=(r // tr,), in_specs=[spec, spec], out_specs=spec,
        out_shape=jax.ShapeDtypeStruct((ns, r, c), mine.dtype),
        compiler_params=_params(("parallel",)),
    )(mine, theirs)


def _exchange_between_chips(pairs):
    n = len(pairs)
    ns = len(CHIP_OF_SLOT) - 1

    def body(*refs):
        ins, outs = refs[:n], refs[n:2 * n]
        send_sems, recv_sems = refs[2 * n:]
        x, y, c = _mesh_place()
        copies = []
        for w in range(n):
            for s, k in enumerate(CHIP_OF_SLOT[1:]):
                peer, _ = _peer(x, y, c, k)
                cp = pltpu.make_async_remote_copy(src_ref=ins[w].at[s + 1], dst_ref=outs[w].at[s], send_sem=send_sems.at[w, s],
                                                  recv_sem=recv_sems.at[w, s], device_id=peer, device_id_type=MESH)
                cp.start()
                copies.append(cp)
        for cp in copies:
            cp.wait()

    any_spec = pl.BlockSpec(memory_space=pl.ANY)
    return pl.pallas_call(
        body, name="reduce_scatter_between_chips",
        in_specs=[any_spec] * n, out_specs=[any_spec] * n,
        out_shape=[jax.ShapeDtypeStruct((ns,) + p.shape[1:], p.dtype) for p in pairs],
        scratch_shapes=[pltpu.SemaphoreType.DMA((n, ns)), pltpu.SemaphoreType.DMA((n, ns))],
        compiler_params=pltpu.CompilerParams(has_side_effects=True),
    )(*pairs)


def _sum_chips(pair, recv, name):
    _, r, c = recv.shape
    tr = _tile(r, (128, 64, 32, 16))

    def body(p_ref, x_ref, o_ref):
        acc = p_ref[...].astype(F32)
        for s in range(x_ref.shape[0]):
            acc = acc + x_ref[s].astype(F32)
        o_ref[...] = acc

    return pl.pallas_call(
        body, name=name, grid=(r // tr,),
        in_specs=[pl.BlockSpec((None, tr, c), lambda i: (0, i, 0)), pl.BlockSpec((recv.shape[0], tr, c), lambda i: (0, i, 0))],
        out_specs=pl.BlockSpec((tr, c), lambda i: (i, 0)),
        out_shape=jax.ShapeDtypeStruct((r, c), F32),
        compiler_params=_params(("parallel",)),
    )(pair, recv)


def _all_reduce_small(part):
    R, W = part.shape

    def body(x_ref, o_ref, buf, send_sems, recv_sems):
        x, y, c = _mesh_place()
        me = 4 * x + 2 * y + c
        buf[me] = x_ref[...]
        copies = []
        for k in range(1, NDEV):
            peer, _ = _peer(x, y, c, k)
            cp = pltpu.make_async_remote_copy(src_ref=x_ref, dst_ref=buf.at[me], send_sem=send_sems.at[k - 1],
                                              recv_sem=recv_sems.at[k - 1], device_id=peer, device_id_type=MESH)
            cp.start()
            copies.append(cp)
        for cp in copies:
            cp.wait()
        acc = buf[0]
        for d in range(1, NDEV):
            acc = acc + buf[d]
        o_ref[...] = acc

    vm = pl.BlockSpec(memory_space=pltpu.VMEM)
    return pl.pallas_call(
        body, name="all_reduce_small", in_specs=[vm], out_specs=vm,
        out_shape=jax.ShapeDtypeStruct((R, W), F32),
        scratch_shapes=[pltpu.VMEM((NDEV, R, W), F32), pltpu.SemaphoreType.DMA((NDEV - 1,)),
                        pltpu.SemaphoreType.DMA((NDEV - 1,))],
        compiler_params=pltpu.CompilerParams(has_side_effects=True),
    )(part)


def _adam_math(w, g, m, v):
    m2 = ADAM_B1 * m + (1.0 - ADAM_B1) * g
    v2 = ADAM_B2 * v + (1.0 - ADAM_B2) * (g * g)
    m_hat = m2 / (1.0 - ADAM_B1 ** ADAM_STEP)
    v_hat = v2 / (1.0 - ADAM_B2 ** ADAM_STEP)
    delta = -ADAM_LR * (m_hat / (jnp.sqrt(v_hat) + ADAM_EPS) + ADAM_WD * w)
    return delta, m2, v2


def _adamw(w, g, m, v, name):
    r, c = w.shape
    tr = r
    for cand in (1024, 512, 256, 128, 64, 32, 16, 8):
        if r % cand == 0 and cand * c <= 256 * 1024:
            tr = cand
            break

    def body(w_ref, g_ref, m_ref, v_ref, d_ref, m2_ref, v2_ref):
        d_ref[...], m2_ref[...], v2_ref[...] = _adam_math(w_ref[...], g_ref[...], m_ref[...], v_ref[...])

    spec = pl.BlockSpec((tr, c), lambda i: (i, 0))
    return pl.pallas_call(
        body, name=name, grid=(r // tr,), in_specs=[spec] * 4, out_specs=[spec] * 3,
        out_shape=[jax.ShapeDtypeStruct((r, c), F32)] * 3,
        compiler_params=_params(("parallel",)),
    )(w, g, m, v)


GAINS = ("ffn1_pre", "ffn1_post", "mix_pre", "mix_post", "mem_norm", "ffn2_pre", "ffn2_post")
GAIN_ROWS = D // HD
ROW_LB = len(GAINS) * GAIN_ROWS
ROWS_GRAD_IN = ROW_LB + 24
ROWS_PACKED = ROW_LB + 32


def _small_update(gsum, w_p, m_p, v_p):
    def body(g_ref, w_ref, m_ref, v_ref, go_ref, d_ref, m2_ref, v2_ref):
        a0 = w_ref[ROW_LB:ROW_LB + 8, :]
        a1 = w_ref[ROW_LB + 8:ROW_LB + 16, :]
        mx = jnp.maximum(a0, a1)
        e0, e1 = jnp.exp(a0 - mx), jnp.exp(a1 - mx)
        lb = e0 / (e0 + e1)
        da0 = g_ref[ROW_LB:ROW_LB + 8, :] * lb * (1.0 - lb)
        g = jnp.concatenate([g_ref[0:ROW_LB, :], da0, -da0, g_ref[ROW_LB + 8:ROWS_GRAD_IN, :]], axis=0)
        go_ref[...] = g
        d_ref[...], m2_ref[...], v2_ref[...] = _adam_math(w_ref[...], g, m_ref[...], v_ref[...])

    vm = pl.BlockSpec(memory_space=pltpu.VMEM)
    return pl.pallas_call(
        body, name="small_update", in_specs=[vm] * 4, out_specs=[vm] * 4,
        out_shape=[jax.ShapeDtypeStruct((ROWS_PACKED, HD), F32)] * 4,
    )(gsum, w_p, m_p, v_p)


def _rows8(a):
    a = a.reshape(-1)
    rows = -(-a.shape[0] // HD)
    rows8 = -(-rows // 8) * 8
    return jnp.pad(a, (0, rows8 * HD - a.shape[0])).reshape(rows8, HD)


def _pack_small(gains, lb0, lb1, gnorm, fb):
    return jnp.concatenate([_rows8(g) for g in gains] + [_rows8(lb0), _rows8(lb1), _rows8(gnorm), _rows8(fb)], axis=0)


def _unpack_small(p):
    out = {}
    for i, name in enumerate(GAINS):
        out[name] = p[i * GAIN_ROWS:(i + 1) * GAIN_ROWS].reshape(1, D)
    lb0 = p[ROW_LB:ROW_LB + NH].reshape(1, WH)
    lb1 = p[ROW_LB + 8:ROW_LB + 8 + NH].reshape(1, WH)
    out["hgrn_lb"] = jnp.concatenate([lb0, lb1], axis=0)
    out["hgrn_gnorm"] = p[ROW_LB + 16:ROW_LB + 16 + NH].reshape(1, WH)
    out["fox_fb"] = p[ROW_LB + 24:ROW_LB + 25, 0:NH]
    return out


def _ffn_forward(xin, pre, post, wg_t, wu_t, wd, tag):
    T = xin.shape[0]
    tr = _tile(T, (256, 128))
    (n,) = _rowwise(_norm_fn, [(xin, 0)], [(pre, None)], [BF16], f"{tag}_pre", tr, D, 1)
    g, u, a = _ffn_up(n, wg_t, wu_t, f"{tag}_up")
    h = _mm(a, wd, "nn", F32, f"{tag}_down")
    (xout,) = _rowwise(functools.partial(_resid_fn, 0.5), [(xin, 0), (h, 0)], [(post, None)], [F32], f"{tag}_post", tr, D, 1)
    return xout, (xin, n, g, u, a, h)


def _ffn_backward(dxout, saved, pre, post, wg_t, wu_t, wd, tag):
    xin, n, g, u, a, h = saved
    T = xin.shape[0]
    tr = _tile(T, (256, 128))
    dh, dpost = _rowwise_bwd(functools.partial(_resid_h_fn, 0.5), [(h, 0)], [(post, None)], [(dxout, 0)], [0], [BF16],
                             f"{tag}_post_bwd", tr, D, 1)
    da = _mm(dh, wd, "nt", BF16, f"{tag}_da")
    dwd = _mm(a, dh, "tn", BF16, f"{tag}_dwd")
    dg, du = _rowwise_bwd(_swiglu_fn, [(g, 0), (u, 0)], [], [(da, 0)], [0, 1], [BF16, BF16], f"{tag}_act_bwd",
                          tr, 512, FP // 512)
    dwg = _mm(dg, n, "tn", BF16, f"{tag}_dwg")
    dwu = _mm(du, n, "tn", BF16, f"{tag}_dwu")
    dn = _mm(dg, wg_t, "nn", F32, f"{tag}_dn_g")
    dn = _mm(du, wu_t, "nn", F32, f"{tag}_dn_u", add=dn)
    dxin, dpre = _rowwise_bwd(_norm_res_fn, [(xin, 0)], [(pre, None)], [(dxout, 0), (dn, 0)], [0], [F32],
                              f"{tag}_pre_bwd", tr, D, 1)
    return dxin, (dwg, dwu, dwd), dpre, dpost


def _local_step(x, mem, tgt, small, wts):
    T = x.shape[0]
    tr = _tile(T, (256, 128))
    fb_pad = jnp.pad(small["fox_fb"], ((0, 0), (0, HD - NH)))

    x1, ffn1_saved = _ffn_forward(x, small["ffn1_pre"], small["ffn1_post"], wts["ffn1_wg"], wts["ffn1_wu"],
                                  wts["ffn1_wd"], "ffn1")
    (un,) = _rowwise(_norm_fn, [(x1, 0)], [(small["mix_pre"], None)], [BF16], "mix_pre", tr, D, 1)
    proj = _mm(un, wts["w_in"], "nn", F32, "proj")
    z = _mm(un, wts["w_gate"], "nt", F32, "gate_logits")
    (memn,) = _rowwise(_norm_fn, [(mem, 0)], [(small["mem_norm"], None)], [BF16], "mem_norm", mem.shape[0], D, 1)
    mem_kv = _mm(memn, wts["w_mem_kv"], "nn", F32, "mem_kv")

    o_raw, states = _hgrn_fwd(proj, small["hgrn_lb"])
    (o_h,) = _rowwise(_hpost_fn, [(o_raw, 0), (proj, CB_HOG)], [(small["hgrn_gnorm"], 0)], [BF16], "hgrn_post",
                      tr, HD, NH)
    ct, cq = _fox_cum(proj, fb_pad)
    o_f, lse = _fox_fwd(proj, ct, cq)
    o_m = _mem_fwd(proj, mem_kv)

    yh = _mm(o_h, wts["w_hgrn_out"], "nt", F32, "hgrn_out")
    yf = _mm(o_f, wts["w_fox_out"], "nt", F32, "fox_out")
    ym = _mm(o_m, wts["w_mem_out"], "nt", F32, "mem_out")
    zc = D // 512
    merge_rows = [(z, 0), (z, zc), (z, 2 * zc), (yh, 0), (yf, 0), (ym, 0)]
    (merged,) = _rowwise(_merge_fn, merge_rows, [], [BF16], "merge", tr, 512, zc)
    m = _mm(merged, wts["w_o"], "nn", F32, "mix_out")
    (x2,) = _rowwise(functools.partial(_resid_fn, 1.0), [(x1, 0), (m, 0)], [(small["mix_post"], None)], [F32], "mix_post",
                     tr, D, 1)
    x3, ffn2_saved = _ffn_forward(x2, small["ffn2_pre"], small["ffn2_post"], wts["ffn2_wg"], wts["ffn2_wu"],
                                  wts["ffn2_wd"], "ffn2")
    dy, loss_part = _loss(x3, tgt, "loss")

    gw, gs = {}, {}
    dx2, (gw["ffn2_wg"], gw["ffn2_wu"], gw["ffn2_wd"]), gs["ffn2_pre"], gs["ffn2_post"] = _ffn_backward(
        dy, ffn2_saved, small["ffn2_pre"], small["ffn2_post"], wts["ffn2_wg"], wts["ffn2_wu"], wts["ffn2_wd"], "ffn2")

    dm, gs["mix_post"] = _rowwise_bwd(functools.partial(_resid_h_fn, 1.0), [(m, 0)], [(small["mix_post"], None)],
                                      [(dx2, 0)], [0], [BF16], "mix_post_bwd", tr, D, 1)
    dmerged = _mm(dm, wts["w_o"], "nt", F32, "d_merged")
    gw["w_o"] = _mm(merged, dm, "tn", BF16, "d_w_o")
    dz0, dz1, dz2, dyh, dyf, dym = _rowwise_bwd(_merge_fn, merge_rows, [], [(dmerged, 0)], [0, 1, 2, 3, 4, 5], [BF16] * 6,
                                                "merge_bwd", tr, 512, zc)
    dz = jnp.concatenate([dz0, dz1, dz2], axis=1)
    gw["w_gate"] = _mm(dz, un, "tn", BF16, "d_w_gate")
    dun = _mm(dz, wts["w_gate"], "nn", F32, "d_un_gate")

    do_h = _mm(dyh, wts["w_hgrn_out"], "nn", F32, "d_o_h")
    gw["w_hgrn_out"] = _mm(dyh, o_h, "tn", BF16, "d_w_hgrn_out")
    do_f = _mm(dyf, wts["w_fox_out"], "nn", F32, "d_o_f")
    gw["w_fox_out"] = _mm(dyf, o_f, "tn", BF16, "d_w_fox_out")
    do_m = _mm(dym, wts["w_mem_out"], "nn", F32, "d_o_m")
    gw["w_mem_out"] = _mm(dym, o_m, "tn", BF16, "d_w_mem_out")

    do_raw, dhog, gs["hgrn_gnorm"] = _rowwise_bwd(_hpost_fn, [(o_raw, 0), (proj, CB_HOG)], [(small["hgrn_gnorm"], 0)],
                                                  [(do_h, 0)], [0, 1], [F32, BF16], "hgrn_post_bwd", tr, HD, NH)
    dhq, dhf, dhi, gs["hgrn_lb"] = _hgrn_bwd(proj, small["hgrn_lb"], states, do_raw)
    dfq, delta = _fox_bwd_dq(proj, ct, cq, lse, do_f)
    dfk, dfv, dc = _fox_bwd_dkv(proj, ct, cq, lse, delta, do_f)
    dff, dfb = _fox_cum_bwd(dc, proj, fb_pad)
    gs["fox_fb"] = dfb
    dmq, dmk, dmv = _mem_bwd(proj, mem_kv, do_m)

    dproj = jnp.concatenate([dhq, dhf, dhi, dhog, dfq, dfk, dfv, dff, dmq, jnp.zeros((T, HD), BF16)], axis=1)
    gw["w_in"] = _mm(un, dproj, "tn", BF16, "d_w_in")
    dun = _mm(dproj, wts["w_in"], "nt", F32, "d_un_proj", add=dun)
    dx1, gs["mix_pre"] = _rowwise_bwd(_norm_res_fn, [(x1, 0)], [(small["mix_pre"], None)], [(dx2, 0), (dun, 0)], [0], [F32],
                                      "mix_pre_bwd", tr, D, 1)

    dmem_kv = jnp.concatenate([dmk, dmv], axis=1)
    gw["w_mem_kv"] = _mm(memn, dmem_kv, "tn", BF16, "d_w_mem_kv")
    dmemn = _mm(dmem_kv, wts["w_mem_kv"], "nt", F32, "d_memn")
    _, gs["mem_norm"] = _rowwise_bwd(_norm_fn, [(mem, 0)], [(small["mem_norm"], None)], [(dmemn, 0)], [0], [BF16],
                                     "mem_norm_bwd", mem.shape[0], D, 1)

    dx, (gw["ffn1_wg"], gw["ffn1_wu"], gw["ffn1_wd"]), gs["ffn1_pre"], gs["ffn1_post"] = _ffn_backward(
        dx1, ffn1_saved, small["ffn1_pre"], small["ffn1_post"], wts["ffn1_wg"], wts["ffn1_wu"], wts["ffn1_wd"], "ffn1")
    return loss_part, dx, gw, gs


BIG = ("ffn1_wg", "ffn1_wu", "ffn1_wd", "w_in", "w_mem_kv", "w_hgrn_out", "w_fox_out", "w_mem_out", "w_gate", "w_o",
       "ffn2_wg", "ffn2_wu", "ffn2_wd")
TRANSPOSED = ("ffn1_wg", "ffn1_wu", "ffn2_wg", "ffn2_wu", "w_hgrn_out", "w_fox_out", "w_mem_out", "w_gate")
FFN_PAD = {"ffn1_wg": FP - F, "ffn1_wu": FP - F, "ffn1_wd": FP - F, "ffn2_wg": FP - F, "ffn2_wu": FP - F,
           "ffn2_wd": FP - F}
SMALL = GAINS + ("hgrn_lb", "hgrn_gnorm", "fox_fb")
WEIGHTS = ("ffn1_pre", "ffn1_post", "ffn1_wg", "ffn1_wu", "ffn1_wd", "mix_pre", "mix_post", "mem_norm", "w_in", "hgrn_lb",
           "hgrn_gnorm", "fox_fb", "w_mem_kv", "w_hgrn_out", "w_fox_out", "w_mem_out", "w_gate", "w_o", "ffn2_pre",
           "ffn2_post", "ffn2_wg", "ffn2_wu", "ffn2_wd")


def _to_gather_layout(name, w):
    if name in TRANSPOSED:
        w = w.T
    if name == "w_in":
        r = w.shape[0]
        w = jnp.concatenate([w[:, :MQ_COL], jnp.zeros((r, FF_COL + HD - MQ_COL), w.dtype), w[:, MQ_COL:],
                             jnp.zeros((r, P - FF_COL - HD - WM), w.dtype)], axis=1)
    return w.astype(BF16)


def _from_gather_layout(name, g):
    if name == "w_in":
        g = jnp.concatenate([g[:, :MQ_COL], g[:, FF_COL + HD:FF_COL + HD + WM]], axis=1)
    if name in TRANSPOSED:
        g = g.T
    return g


def kernel(x, mem, ffn1_pre, ffn1_post, ffn1_wg, ffn1_wu, ffn1_wd, mix_pre, mix_post, mem_norm, w_in, hgrn_lb, hgrn_gnorm, fox_fb, w_mem_kv, w_hgrn_out, w_fox_out, w_mem_out, w_gate, w_o, ffn2_pre, ffn2_post, ffn2_wg, ffn2_wu, ffn2_wd, loss_target, m_ffn1_pre, m_ffn1_post, m_ffn1_wg, m_ffn1_wu, m_ffn1_wd, m_mix_pre, m_mix_post, m_mem_norm, m_w_in, m_hgrn_lb, m_hgrn_gnorm, m_fox_fb, m_w_mem_kv, m_w_hgrn_out, m_w_fox_out, m_w_mem_out, m_w_gate, m_w_o, m_ffn2_pre, m_ffn2_post, m_ffn2_wg, m_ffn2_wu, m_ffn2_wd, v_ffn1_pre, v_ffn1_post, v_ffn1_wg, v_ffn1_wu, v_ffn1_wd, v_mix_pre, v_mix_post, v_mem_norm, v_w_in, v_hgrn_lb, v_hgrn_gnorm, v_fox_fb, v_w_mem_kv, v_w_hgrn_out, v_w_fox_out, v_w_mem_out, v_w_gate, v_w_o, v_ffn2_pre, v_ffn2_post, v_ffn2_wg, v_ffn2_wu, v_ffn2_wd):
    a = dict(locals())
    small = {n: a[n] for n in SMALL}
    shard = {n: a[n][0] if a[n].ndim == 3 else a[n] for n in BIG}

    blocks = [_to_gather_layout(n, shard[n]) for n in BIG]
    gathered = _all_gather(blocks, [FFN_PAD.get(n, 0) for n in BIG])
    wts = dict(zip(BIG, gathered))

    loss_part, dx, gw, gs = _local_step(x[0], mem[0], loss_target[0], small, wts)
    loss = lax.psum(0.5 / D * jnp.sum(loss_part), ("x", "y", "c"))

    mine, theirs = _exchange_in_chip([gw[n] for n in BIG], [b.shape[0] for b in blocks])
    pairs = [_pair_sum(a_, b_, f"pair_{n}") for n, a_, b_ in zip(BIG, mine, theirs)]
    recv = _exchange_between_chips(pairs)
    grads, deltas, new_m, new_v = {}, {}, {}, {}
    for n, pr, rb in zip(BIG, pairs, recv):
        g = _from_gather_layout(n, _sum_chips(pr, rb, f"sum_{n}"))
        d, m2, v2 = _adamw(shard[n], g, a["m_" + n].reshape(g.shape), a["v_" + n].reshape(g.shape), f"adamw_{n}")
        full = a[n].shape
        grads[n], deltas[n], new_m[n], new_v[n] = g.reshape(full), d.reshape(full), m2.reshape(full), v2.reshape(full)

    part = jnp.concatenate([_rows8(gs[n]) for n in GAINS] + [_rows8(gs["hgrn_lb"]), _rows8(gs["hgrn_gnorm"]),
                                                             _rows8(gs["fox_fb"][:, :NH])], axis=0)
    gsum = _all_reduce_small(part)

    def packed(prefix):
        lb = a[prefix + "hgrn_lb"]
        return _pack_small([a[prefix + n] for n in GAINS], lb[0], lb[1], a[prefix + "hgrn_gnorm"], a[prefix + "fox_fb"])

    g_p, d_p, m_p, v_p = _small_update(gsum, packed(""), packed("m_"), packed("v_"))
    for dst, p in ((grads, g_p), (deltas, d_p), (new_m, m_p), (new_v, v_p)):
        dst.update(_unpack_small(p))

    return (loss, dx[None], *[grads[n] for n in WEIGHTS], *[deltas[n] for n in WEIGHTS],
            *[new_m[n] for n in WEIGHTS], *[new_v[n] for n in WEIGHTS])
```

```python
import functools

import jax
import jax.numpy as jnp
from jax import lax
from jax.experimental import pallas as pl
from jax.experimental.pallas import tpu as pltpu

F32 = jnp.float32
BF16 = jnp.bfloat16
HIGHEST = lax.Precision.HIGHEST

NDEV = 8
D = 2048
F = 5504
FP = 5632
HD = 128
NH = 6
NM = 4
WH = NH * HD
WM = NM * HD
P = 6144
FF_COL = 5376
MQ_COL = 5382
CHUNK = 64
EPS = 1e-6
SCALE = HD ** -0.5
NEG = -1e30
VMEM_LIMIT = 48 * 1024 * 1024

CB_HQ, CB_HF, CB_HI, CB_HOG, CB_FQ, CB_FK, CB_FV, CB_FF, CB_MQ = 0, 6, 12, 18, 24, 30, 36, 42, 43

ADAM_LR, ADAM_B1, ADAM_B2, ADAM_EPS, ADAM_WD, ADAM_STEP = 0.001, 0.9, 0.999, 1e-08, 0.01, 10

NT = (((1,), (1,)), ((), ()))
NN = (((1,), (0,)), ((), ()))
TN = (((0,), (0,)), ((), ()))
MESH = pl.DeviceIdType.MESH


def _params(sem=None, **kw):
    return pltpu.CompilerParams(dimension_semantics=sem, vmem_limit_bytes=VMEM_LIMIT, **kw)


def _tile(n, prefs):
    for p in prefs:
        if p <= n and n % p == 0:
            return p
    return n


def _dot(a, b, dims):
    return lax.dot_general(a.astype(BF16), b.astype(BF16), dims, preferred_element_type=F32)


def _mm(a, b, mode, out_dtype, name, add=None):
    if mode == "nn":
        (M, K), (K2, N) = a.shape, b.shape
    elif mode == "nt":
        (M, K), (N, K2) = a.shape, b.shape
    else:
        (K, M), (K2, N) = a.shape, b.shape
    assert K == K2, (a.shape, b.shape, mode)
    if mode == "tn":
        tm = _tile(M, (512, 256, 128))
        tn = _tile(N, (1024, 768, 512, 256, 128))
        tk = _tile(K, (4096, 2048, 1024, 512, 256, 128))
    else:
        tm = _tile(M, (1024, 512, 256, 128)) if K <= 2048 else _tile(M, (512, 256, 128))
        tn = _tile(N, (512, 768, 256, 128))
        tk = K if K <= 6144 else _tile(K, (2048, 1024, 512, 256, 128))
    nk = K // tk
    dims = {"nn": NN, "nt": NT, "tn": TN}[mode]
    has_add = add is not None

    def body(*refs):
        a_ref, b_ref = refs[0], refs[1]
        c_ref = refs[2] if has_add else None
        o_ref = refs[3] if has_add else refs[2]
        acc_ref = refs[-1]
        k = pl.program_id(2)
        part = _dot(a_ref[...], b_ref[...], dims)

        def finish(r):
            if has_add:
                r = r + c_ref[...].astype(F32)
            o_ref[...] = r.astype(o_ref.dtype)

        if nk == 1:
            finish(part)
        else:
            @pl.when(k == 0)
            def _():
                acc_ref[...] = part

            @pl.when(k > 0)
            def _():
                acc_ref[...] += part

            @pl.when(k == nk - 1)
            def _():
                finish(acc_ref[...])

    if mode == "nn":
        a_spec = pl.BlockSpec((tm, tk), lambda i, j, k: (i, k))
        b_spec = pl.BlockSpec((tk, tn), lambda i, j, k: (k, j))
    elif mode == "nt":
        a_spec = pl.BlockSpec((tm, tk), lambda i, j, k: (i, k))
        b_spec = pl.BlockSpec((tn, tk), lambda i, j, k: (j, k))
    else:
        a_spec = pl.BlockSpec((tk, tm), lambda i, j, k: (k, i))
        b_spec = pl.BlockSpec((tk, tn), lambda i, j, k: (k, j))
    o_spec = pl.BlockSpec((tm, tn), lambda i, j, k: (i, j))
    in_specs = [a_spec, b_spec] + ([o_spec] if has_add else [])
    args = (a, b) + ((add,) if has_add else ())
    return pl.pallas_call(
        body, name=name, grid=(M // tm, N // tn, nk), in_specs=in_specs, out_specs=o_spec,
        out_shape=jax.ShapeDtypeStruct((M, N), out_dtype),
        scratch_shapes=[pltpu.VMEM((tm, tn) if nk > 1 else (8, 128), F32)],
        compiler_params=_params(("parallel", "parallel", "arbitrary")),
    )(*args)


def _ffn_up(n, wg_t, wu_t, name):
    T = n.shape[0]
    tm = _tile(T, (1024, 512, 256, 128))
    tn = 512

    def body(n_ref, wg_ref, wu_ref, g_ref, u_ref, a_ref):
        x = n_ref[...]
        g = _dot(x, wg_ref[...], NT)
        u = _dot(x, wu_ref[...], NT)
        g_ref[...] = g
        u_ref[...] = u
        a_ref[...] = (g * jax.nn.sigmoid(g) * u).astype(BF16)

    w_spec = pl.BlockSpec((tn, D), lambda i, j: (j, 0))
    o_spec = pl.BlockSpec((tm, tn), lambda i, j: (i, j))
    return pl.pallas_call(
        body, name=name, grid=(T // tm, FP // tn),
        in_specs=[pl.BlockSpec((tm, D), lambda i, j: (i, 0)), w_spec, w_spec],
        out_specs=[o_spec, o_spec, o_spec],
        out_shape=[jax.ShapeDtypeStruct((T, FP), F32), jax.ShapeDtypeStruct((T, FP), F32),
                   jax.ShapeDtypeStruct((T, FP), BF16)],
        compiler_params=_params(("parallel", "parallel")),
    )(n, wg_t, wu_t)


def _ffn_act_bwd(dh, wd, g, u, name):
    T = dh.shape[0]
    tm = _tile(T, (1024, 512, 256, 128))
    tn = 512

    def body(dh_ref, wd_ref, g_ref, u_ref, dg_ref, du_ref):
        da = _dot(dh_ref[...], wd_ref[...], NT)
        g = g_ref[...]
        sg = jax.nn.sigmoid(g)
        dg_ref[...] = (da * u_ref[...] * (sg * (1.0 + g * (1.0 - sg)))).astype(dg_ref.dtype)
        du_ref[...] = (da * (g * sg)).astype(du_ref.dtype)

    tile = pl.BlockSpec((tm, tn), lambda i, j: (i, j))
    return pl.pallas_call(
        body, name=name, grid=(T // tm, FP // tn),
        in_specs=[pl.BlockSpec((tm, D), lambda i, j: (i, 0)), pl.BlockSpec((tn, D), lambda i, j: (j, 0)), tile, tile],
        out_specs=[tile, tile],
        out_shape=[jax.ShapeDtypeStruct((T, FP), BF16), jax.ShapeDtypeStruct((T, FP), BF16)],
        compiler_params=_params(("parallel", "parallel")),
    )(dh, wd, g, u)


def _row_specs(rows, tr, cw):
    return [pl.BlockSpec((tr, cw), lambda j, i, o=off: (i, o + j)) for _, off in rows]


def _const_specs(consts, cw):
    specs = []
    for arr, off in consts:
        if off is None:
            specs.append(pl.BlockSpec(arr.shape, lambda j, i: (0, 0)))
        else:
            specs.append(pl.BlockSpec((arr.shape[0], cw), lambda j, i, o=off: (0, o + j)))
    return specs


def _rowwise(fn, rows, consts, out_dtypes, name, tr, cw, ncol):
    T = rows[0][0].shape[0]
    nr, nc = len(rows), len(consts)

    def body(*refs):
        r = [x[...].astype(F32) for x in refs[:nr]]
        c = [x[...] for x in refs[nr:nr + nc]]
        res = fn(*r, *c)
        for o_ref, v in zip(refs[nr + nc:], res):
            o_ref[...] = v.astype(o_ref.dtype)

    o_spec = pl.BlockSpec((tr, cw), lambda j, i: (i, j))
    return pl.pallas_call(
        body, name=name, grid=(ncol, T // tr),
        in_specs=_row_specs(rows, tr, cw) + _const_specs(consts, cw),
        out_specs=[o_spec] * len(out_dtypes),
        out_shape=[jax.ShapeDtypeStruct((T, ncol * cw), dt) for dt in out_dtypes],
        compiler_params=_params(("parallel", "parallel")),
    )(*[a for a, _ in rows], *[a for a, _ in consts])


def _rowwise_bwd(fn, rows, consts, cots, diff, ddtypes, name, tr, cw, ncol):
    T = rows[0][0].shape[0]
    nr, nc, nt, nd = len(rows), len(consts), len(cots), len(diff)

    def body(*refs):
        r = [x[...].astype(F32) for x in refs[:nr]]
        c = [x[...] for x in refs[nr:nr + nc]]
        ct = [x[...].astype(F32) for x in refs[nr + nc:nr + nc + nt]]
        drow_refs = refs[nr + nc + nt:nr + nc + nt + nd]
        dconst_refs = refs[nr + nc + nt + nd:]
        i = pl.program_id(1)

        def f(*args):
            full = list(r)
            for idx, a in zip(diff, args[:nd]):
                full[idx] = a
            return tuple(fn(*full, *args[nd:]))

        _, vjp = jax.vjp(f, *[r[d] for d in diff], *c)
        g = vjp(tuple(ct))
        for o_ref, v in zip(drow_refs, g[:nd]):
            o_ref[...] = v.astype(o_ref.dtype)

        @pl.when(i == 0)
        def _():
            for o_ref in dconst_refs:
                o_ref[...] = jnp.zeros_like(o_ref)

        for o_ref, v in zip(dconst_refs, g[nd:]):
            o_ref[...] += v

    o_spec = pl.BlockSpec((tr, cw), lambda j, i: (i, j))
    out_shape = [jax.ShapeDtypeStruct((T, ncol * cw), dt) for dt in ddtypes]
    out_shape += [jax.ShapeDtypeStruct(a.shape, F32) for a, _ in consts]
    return pl.pallas_call(
        body, name=name, grid=(ncol, T // tr),
        in_specs=_row_specs(rows, tr, cw) + _const_specs(consts, cw) + _row_specs(cots, tr, cw),
        out_specs=[o_spec] * nd + _const_specs(consts, cw),
        out_shape=out_shape,
        compiler_params=_params(("parallel", "arbitrary")),
    )(*[a for a, _ in rows], *[a for a, _ in consts], *[a for a, _ in cots])


def _rms(x, g):
    return x * lax.rsqrt(jnp.mean(x * x, axis=-1, keepdims=True) + EPS) * g


def _silu(x):
    return x * jax.nn.sigmoid(x)


def _norm_fn(x, g):
    return (_rms(x, g),)


def _norm_res_fn(x, g):
    return (x, _rms(x, g))


def _resid_fn(scale, x, h, g):
    return (x + scale * _rms(h, g),)


def _resid_h_fn(scale, h, g):
    return (scale * _rms(h, g),)


def _hpost_fn(o, hog, gn):
    return (_rms(o, gn) * _silu(hog),)


def _merge_fn(z0, z1, z2, yh, yf, ym):
    return (jax.nn.sigmoid(z0) * yh + jax.nn.sigmoid(z1) * yf + jax.nn.sigmoid(z2) * ym,)


def _loss(x3, tgt, name):
    T = x3.shape[0]
    tr = _tile(T, (256, 128))

    def body(x_ref, t_ref, dy_ref, s_ref):
        i = pl.program_id(0)
        e = x_ref[...] - t_ref[...]
        dy_ref[...] = e * (1.0 / D)
        col = jnp.sum(e * e, axis=0, keepdims=True)
        tot = col[:, 0:HD]
        for k in range(1, D // HD):
            tot = tot + col[:, k * HD:(k + 1) * HD]

        @pl.when(i == 0)
        def _():
            s_ref[...] = jnp.zeros_like(s_ref)

        s_ref[...] += tot

    spec = pl.BlockSpec((tr, D), lambda i: (i, 0))
    return pl.pallas_call(
        body, name=name, grid=(T // tr,), in_specs=[spec, spec],
        out_specs=[spec, pl.BlockSpec((1, HD), lambda i: (0, 0))],
        out_shape=[jax.ShapeDtypeStruct((T, D), F32), jax.ShapeDtypeStruct((1, HD), F32)],
        compiler_params=_params(("arbitrary",)),
    )(x3, tgt)


def _lower_bound(lb_ref):
    a0 = lb_ref[0:1, :]
    a1 = lb_ref[1:2, :]
    mx = jnp.maximum(a0, a1)
    e0 = jnp.exp(a0 - mx)
    return e0 / (e0 + jnp.exp(a1 - mx))


def _hgrn_prep(hq, hf, lb):
    g = lb + (1.0 - lb) * jax.nn.sigmoid(hf)
    return _silu(hq), 1.0 - g, jnp.log(g)


def _tri(n, upper):
    r = lax.broadcasted_iota(jnp.int32, (n, n), 0)
    c = lax.broadcasted_iota(jnp.int32, (n, n), 1)
    return (c >= r) if upper else (c <= r)


def _hgrn_factors(q, k, gl):
    low = _tri(CHUNK, False)
    b = lax.dot_general(low.astype(F32), gl, NN, precision=HIGHEST, preferred_element_type=F32)
    bl = b[CHUNK - 1:CHUNK, :]
    ref = b[CHUNK // 2 - 1:CHUNK // 2, :]
    eb = jnp.exp(b)
    ea = jnp.exp(b - ref)
    ebn = jnp.exp(ref - b)
    ek = jnp.exp(bl - b)
    ebl = jnp.exp(bl)
    return low, eb, ea, ebn, ek, ebl


def _hgrn_fwd(proj, hgrn_lb):
    T = proj.shape[0]
    cb = _tile(T, (512, 256, 128, 64))
    nchunk = cb // CHUNK

    def body(hq_ref, hf_ref, hi_ref, lb_ref, o_ref, st_ref, state):
        @pl.when(pl.program_id(0) == 0)
        def _():
            state[...] = jnp.zeros_like(state)

        lb = _lower_bound(lb_ref)

        def chunk(c, carry):
            r0 = pl.multiple_of(c * CHUNK, CHUNK)
            for h in range(NH):
                cols = slice(h * HD, (h + 1) * HD)
                q, k, gl = _hgrn_prep(hq_ref[pl.ds(r0, CHUNK), cols], hf_ref[pl.ds(r0, CHUNK), cols], lb[:, cols])
                v = hi_ref[pl.ds(r0, CHUNK), cols]
                low, eb, ea, ebn, ek, ebl = _hgrn_factors(q, k, gl)
                s_t = state[h]
                st_ref[c, h] = s_t
                pm = jnp.where(low, _dot(q * ea, k * ebn, NT), 0.0)
                o_ref[pl.ds(r0, CHUNK), cols] = _dot(q * eb, s_t, NT) + _dot(pm, v, NN)
                state[h] = s_t * ebl + _dot(v, k * ek, TN)
            return carry

        lax.fori_loop(0, nchunk, chunk, 0)

    def col(off):
        return pl.BlockSpec((cb, WH), lambda i, o=off: (i, o))

    return pl.pallas_call(
        body, name="hgrn_fwd", grid=(T // cb,),
        in_specs=[col(0), col(1), col(2), pl.BlockSpec((2, WH), lambda i: (0, 0))],
        out_specs=[pl.BlockSpec((cb, WH), lambda i: (i, 0)),
                   pl.BlockSpec((nchunk, NH, HD, HD), lambda i: (i, 0, 0, 0))],
        out_shape=[jax.ShapeDtypeStruct((T, WH), F32), jax.ShapeDtypeStruct((T // CHUNK, NH, HD, HD), F32)],
        scratch_shapes=[pltpu.VMEM((NH, HD, HD), F32)],
        compiler_params=_params(("arbitrary",)),
    )(proj, proj, proj, hgrn_lb)


def _hgrn_bwd(proj, hgrn_lb, states, do):
    T = proj.shape[0]
    cb = _tile(T, (512, 256, 128, 64))
    nchunk = cb // CHUNK
    nb = T // cb

    def body(hq_ref, hf_ref, hi_ref, lb_ref, st_ref, do_ref, dhq_ref, dhf_ref, dhi_ref, dlb_ref, dstate):
        @pl.when(pl.program_id(0) == 0)
        def _():
            dstate[...] = jnp.zeros_like(dstate)
            dlb_ref[...] = jnp.zeros_like(dlb_ref)

        lb = _lower_bound(lb_ref)
        up = _tri(CHUNK, True)
        last = lax.broadcasted_iota(jnp.int32, (CHUNK, HD), 0) == CHUNK - 1

        def chunk(cc, carry):
            c = nchunk - 1 - cc
            r0 = pl.multiple_of(c * CHUNK, CHUNK)
            for h in range(NH):
                cols = slice(h * HD, (h + 1) * HD)
                hq = hq_ref[pl.ds(r0, CHUNK), cols]
                hf = hf_ref[pl.ds(r0, CHUNK), cols]
                (q, k, gl), prep_vjp = jax.vjp(_hgrn_prep, hq, hf, lb[:, cols])
                v = hi_ref[pl.ds(r0, CHUNK), cols]
                d_o = do_ref[pl.ds(r0, CHUNK), cols]
                low, eb, ea, ebn, ek, ebl = _hgrn_factors(q, k, gl)
                s_t = st_ref[c, h]
                ds_new = dstate[h]
                qe, am, bm, kb = q * eb, q * ea, k * ebn, k * ek
                pm_t = jnp.where(up, _dot(bm, am, NT), 0.0)
                dp = jnp.where(low, _dot(d_o, v, NT), 0.0)
                dp_t = jnp.where(up, _dot(v, d_o, NT), 0.0)
                dqe = _dot(d_o, s_t, NN)
                da = _dot(dp, bm, NN)
                db_m = _dot(dp_t, am, NN)
                dkb = _dot(v, ds_new, NN)
                dv = _dot(pm_t, d_o, NN) + _dot(kb, ds_new, NT)
                dq = dqe * eb + da * ea
                dk = db_m * ebn + dkb * ek
                dbl = jnp.sum(dkb * kb, axis=0, keepdims=True) + jnp.sum(ds_new * s_t, axis=0, keepdims=True) * ebl
                db = (dqe * qe + da * am.astype(BF16).astype(F32) - db_m * bm.astype(BF16).astype(F32) - dkb * kb
                      + jnp.where(last, dbl, 0.0))
                dgl = lax.dot_general(up.astype(F32), db, NN, precision=HIGHEST, preferred_element_type=F32)
                dhq, dhf, dlb = prep_vjp((dq, dk, dgl))
                dhq_ref[pl.ds(r0, CHUNK), cols] = dhq.astype(dhq_ref.dtype)
                dhf_ref[pl.ds(r0, CHUNK), cols] = dhf.astype(dhf_ref.dtype)
                dhi_ref[pl.ds(r0, CHUNK), cols] = dv.astype(dhi_ref.dtype)
                dlb_ref[:, cols] += dlb
                dstate[h] = _dot(d_o, qe, TN) + ds_new * ebl
            return carry

        lax.fori_loop(0, nchunk, chunk, 0)

    def col(off):
        return pl.BlockSpec((cb, WH), lambda i, o=off: (nb - 1 - i, o))

    row = pl.BlockSpec((cb, WH), lambda i: (nb - 1 - i, 0))
    return pl.pallas_call(
        body, name="hgrn_bwd", grid=(nb,),
        in_specs=[col(0), col(1), col(2), pl.BlockSpec((2, WH), lambda i: (0, 0)),
                  pl.BlockSpec((nchunk, NH, HD, HD), lambda i: (nb - 1 - i, 0, 0, 0)), row],
        out_specs=[row, row, row, pl.BlockSpec((1, WH), lambda i: (0, 0))],
        out_shape=[jax.ShapeDtypeStruct((T, WH), BF16)] * 3 + [jax.ShapeDtypeStruct((1, WH), F32)],
        scratch_shapes=[pltpu.VMEM((NH, HD, HD), F32)],
        compiler_params=_params(("arbitrary",)),
    )(proj, proj, proj, hgrn_lb, states, do)


def _log_sigmoid(z):
    return jnp.minimum(z, 0.0) - jnp.log(1.0 + jnp.exp(-jnp.abs(z)))


def _fox_cum(proj, fb_pad):
    T = proj.shape[0]
    tb = _tile(T, (256, 128))

    def body(ff_ref, fb_ref, ct_ref, cq_ref, carry):
        @pl.when(pl.program_id(0) == 0)
        def _():
            carry[...] = jnp.zeros_like(carry)

        lf = _log_sigmoid(ff_ref[...] + fb_ref[...])
        cs = lax.dot_general(_tri(tb, False).astype(F32), lf, NN, precision=HIGHEST,
                             preferred_element_type=F32) + carry[0:1, :]
        carry[0:1, :] = cs[tb - 1:tb, :]
        ct_ref[...] = cs.T[0:8, :]
        for h in range(NH):
            cq_ref[h] = jnp.broadcast_to(cs[:, h:h + 1], (tb, HD))

    return pl.pallas_call(
        body, name="fox_cum", grid=(T // tb,),
        in_specs=[pl.BlockSpec((tb, HD), lambda i: (i, CB_FF)), pl.BlockSpec((1, HD), lambda i: (0, 0))],
        out_specs=[pl.BlockSpec((8, tb), lambda i: (0, i)), pl.BlockSpec((NH, tb, HD), lambda i: (0, i, 0))],
        out_shape=[jax.ShapeDtypeStruct((8, T), F32), jax.ShapeDtypeStruct((NH, T, HD), F32)],
        scratch_shapes=[pltpu.VMEM((8, HD), F32)],
        compiler_params=_params(("arbitrary",)),
    )(proj, fb_pad)


def _fox_cum_bwd(dc, proj, fb_pad):
    T = proj.shape[0]
    tb = _tile(T, (256, 128))
    nb = T // tb

    def body(dc_ref, ff_ref, fb_ref, dff_ref, dfb_ref, carry):
        @pl.when(pl.program_id(0) == 0)
        def _():
            carry[...] = jnp.zeros_like(carry)
            dfb_ref[...] = jnp.zeros_like(dfb_ref)

        rid = lax.broadcasted_iota(jnp.int32, (8, tb), 0)
        m8 = jnp.zeros((8, tb), F32)
        for h in range(NH):
            m8 = m8 + jnp.where(rid == h, dc_ref[h], 0.0)
        dcb = jnp.concatenate([m8, jnp.zeros((HD - 8, tb), F32)], axis=0).T
        rev = lax.dot_general(_tri(tb, True).astype(F32), dcb, NN, precision=HIGHEST,
                              preferred_element_type=F32) + carry[0:1, :]
        carry[0:1, :] = rev[0:1, :]
        dff = rev * jax.nn.sigmoid(-(ff_ref[...] + fb_ref[...]))
        dff_ref[...] = dff.astype(dff_ref.dtype)
        dfb_ref[...] += jnp.sum(dff, axis=0, keepdims=True)

    return pl.pallas_call(
        body, name="fox_cum_bwd", grid=(nb,),
        in_specs=[pl.BlockSpec((NH, 8, tb), lambda i: (0, 0, nb - 1 - i)),
                  pl.BlockSpec((tb, HD), lambda i: (nb - 1 - i, CB_FF)), pl.BlockSpec((1, HD), lambda i: (0, 0))],
        out_specs=[pl.BlockSpec((tb, HD), lambda i: (nb - 1 - i, 0)), pl.BlockSpec((1, HD), lambda i: (0, 0))],
        out_shape=[jax.ShapeDtypeStruct((T, HD), BF16), jax.ShapeDtypeStruct((1, HD), F32)],
        scratch_shapes=[pltpu.VMEM((8, HD), F32)],
        compiler_params=_params(("arbitrary",)),
    )(dc, proj, fb_pad)


STRIP = 128


def _fox_scores(q, k, cq, ck, i, j, bq, bk, r0=0):
    rows = q.shape[0]
    s = _dot(q, k, NT) * SCALE + (cq - ck)
    diff = lax.broadcasted_iota(jnp.int32, (rows, bk), 1) - lax.broadcasted_iota(jnp.int32, (rows, bk), 0)
    return jnp.where(diff <= i * bq + r0 - j * bk, s, NEG)


def _heads(h):
    return slice(h * HD, (h + 1) * HD)


def _fox_fwd(proj, ct, cq):
    T = proj.shape[0]
    bq = bk = _tile(T, (512, 256, 128))
    nq = nk = T // bq

    def body(q_ref, k_ref, v_ref, ct_ref, cq_ref, o_ref, lse_ref, m_s, l_s, acc_s):
        i, j = pl.program_id(0), pl.program_id(1)

        @pl.when(j == 0)
        def _():
            m_s[...] = jnp.full_like(m_s, NEG)
            l_s[...] = jnp.zeros_like(l_s)
            acc_s[...] = jnp.zeros_like(acc_s)

        @pl.when(j <= i)
        def _():
            for h in range(NH):
                hs = _heads(h)
                k, v, ck = k_ref[:, hs], v_ref[:, hs], ct_ref[h:h + 1, :]
                for r0 in range(0, bq, STRIP):
                    rs = slice(r0, r0 + STRIP)
                    s = _fox_scores(q_ref[rs, hs], k, cq_ref[h, rs, 0:1], ck, i, j, bq, bk, r0)
                    m_prev = m_s[h, rs]
                    m_new = jnp.maximum(m_prev, jnp.max(s, axis=1, keepdims=True))
                    alpha = jnp.exp(m_prev - m_new)
                    p = jnp.exp(s - m_new)
                    l_s[h, rs] = alpha * l_s[h, rs] + jnp.sum(p, axis=1, keepdims=True)
                    acc_s[rs, hs] = alpha * acc_s[rs, hs] + _dot(p, v, NN)
                    m_s[h, rs] = m_new

        @pl.when(j == nk - 1)
        def _():
            for h in range(NH):
                o_ref[:, _heads(h)] = acc_s[:, _heads(h)] / l_s[h]
                lse_ref[h] = jnp.broadcast_to(m_s[h] + jnp.log(l_s[h]), (bq, HD))

    def kv(off):
        return pl.BlockSpec((bk, WH), lambda i, j, o=off // NH: (jnp.minimum(j, i), o))

    stat = pl.BlockSpec((NH, bq, HD), lambda i, j: (0, i, 0))
    return pl.pallas_call(
        body, name="fox_fwd", grid=(nq, nk),
        in_specs=[pl.BlockSpec((bq, WH), lambda i, j: (i, CB_FQ // NH)), kv(CB_FK), kv(CB_FV),
                  pl.BlockSpec((8, bk), lambda i, j: (0, jnp.minimum(j, i))), stat],
        out_specs=[pl.BlockSpec((bq, WH), lambda i, j: (i, 0)), stat],
        out_shape=[jax.ShapeDtypeStruct((T, WH), F32), jax.ShapeDtypeStruct((NH, T, HD), F32)],
        scratch_shapes=[pltpu.VMEM((NH, bq, 1), F32), pltpu.VMEM((NH, bq, 1), F32), pltpu.VMEM((bq, WH), F32)],
        compiler_params=_params(("parallel", "arbitrary")),
    )(proj, proj, proj, ct, cq)


def _fox_bwd_dq(proj, ct, cq, lse, do):
    T = proj.shape[0]
    bq = bk = _tile(T, (512, 256, 128))
    nq = nk = T // bq

    def body(q_ref, k_ref, v_ref, ct_ref, cq_ref, lse_ref, do_ref, dq_ref, delta_ref, acc_s, delta_s, psum_s):
        i, jj = pl.program_id(0), pl.program_id(1)
        j = jj % nk

        @pl.when(jj == 0)
        def _():
            acc_s[...] = jnp.zeros_like(acc_s)
            delta_s[...] = jnp.zeros_like(delta_s)
            psum_s[...] = jnp.zeros_like(psum_s)

        def probs(h):
            hs = _heads(h)
            k = k_ref[:, hs]
            s = _fox_scores(q_ref[:, hs], k, cq_ref[h, :, 0:1], ct_ref[h:h + 1, :], i, j, bq, bk)
            return k, jnp.exp(s - lse_ref[h, :, 0:1]), _dot(do_ref[:, hs], v_ref[:, hs], NT)

        @pl.when((j <= i) & (jj < nk))
        def _():
            for h in range(NH):
                _, p, dp = probs(h)
                delta_s[h] += jnp.sum(p * dp, axis=1, keepdims=True)
                psum_s[h] += jnp.sum(p, axis=1, keepdims=True)

        @pl.when((j <= i) & (jj >= nk))
        def _():
            for h in range(NH):
                k, p, dp = probs(h)
                ds = p * (dp - delta_s[h] / psum_s[h])
                acc_s[:, _heads(h)] += _dot(ds, k, NN) * SCALE

        @pl.when(jj == 2 * nk - 1)
        def _():
            dq_ref[...] = acc_s[...].astype(dq_ref.dtype)
            for h in range(NH):
                delta_ref[h] = jnp.broadcast_to(delta_s[h] / psum_s[h], (bq, HD))

    def kv(off):
        return pl.BlockSpec((bk, WH), lambda i, jj, o=off // NH: (jnp.minimum(jj % nk, i), o))

    qrow = pl.BlockSpec((bq, WH), lambda i, jj: (i, 0))
    stat = pl.BlockSpec((NH, bq, HD), lambda i, jj: (0, i, 0))
    return pl.pallas_call(
        body, name="fox_bwd_dq", grid=(nq, 2 * nk),
        in_specs=[pl.BlockSpec((bq, WH), lambda i, jj: (i, CB_FQ // NH)), kv(CB_FK), kv(CB_FV),
                  pl.BlockSpec((8, bk), lambda i, jj: (0, jnp.minimum(jj % nk, i))), stat, stat, qrow],
        out_specs=[qrow, stat],
        out_shape=[jax.ShapeDtypeStruct((T, WH), BF16), jax.ShapeDtypeStruct((NH, T, HD), F32)],
        scratch_shapes=[pltpu.VMEM((bq, WH), F32), pltpu.VMEM((NH, bq, 1), F32), pltpu.VMEM((NH, bq, 1), F32)],
        compiler_params=_params(("parallel", "arbitrary")),
    )(proj, proj, proj, ct, cq, lse, do)


def _fox_bwd_dkv(proj, ct, cq, lse, delta, do):
    T = proj.shape[0]
    bq = bk = _tile(T, (512, 256, 128))
    nq = nk = T // bq

    def body(q_ref, k_ref, v_ref, ct_ref, cq_ref, lse_ref, delta_ref, do_ref, dk_ref, dv_ref, dc_ref, dk_s, dv_s, dc_s):
        j, i = pl.program_id(0), pl.program_id(1)

        @pl.when(i == 0)
        def _():
            dk_s[...] = jnp.zeros_like(dk_s)
            dv_s[...] = jnp.zeros_like(dv_s)
            dc_s[...] = jnp.zeros_like(dc_s)

        @pl.when(i >= j)
        def _():
            for h in range(NH):
                hs = _heads(h)
                q = q_ref[:, hs]
                d_o = do_ref[:, hs]
                s = _fox_scores(q, k_ref[:, hs], cq_ref[h, :, 0:1], ct_ref[h:h + 1, :], i, j, bq, bk)
                p = jnp.exp(s - lse_ref[h, :, 0:1])
                dv_s[:, hs] += _dot(p, d_o, TN)
                dp = _dot(d_o, v_ref[:, hs], NT)
                ds = p * (dp - delta_ref[h, :, 0:1])
                dk_s[:, hs] += _dot(ds, q, TN) * SCALE
                dc_s[h:h + 1, :] -= jnp.sum(ds, axis=0, keepdims=True)

        @pl.when(i == nq - 1)
        def _():
            dk_ref[...] = dk_s[...].astype(dk_ref.dtype)
            dv_ref[...] = dv_s[...].astype(dv_ref.dtype)
            for h in range(NH):
                dc_ref[h] = jnp.broadcast_to(dc_s[h:h + 1, :], (8, bk))

    def kv(off):
        return pl.BlockSpec((bk, WH), lambda j, i, o=off // NH: (j, o))

    qrow = pl.BlockSpec((bq, WH), lambda j, i: (jnp.maximum(i, j), 0))
    stat = pl.BlockSpec((NH, bq, HD), lambda j, i: (0, jnp.maximum(i, j), 0))
    krow = pl.BlockSpec((bk, WH), lambda j, i: (j, 0))
    return pl.pallas_call(
        body, name="fox_bwd_dkv", grid=(nk, nq),
        in_specs=[pl.BlockSpec((bq, WH), lambda j, i: (jnp.maximum(i, j), CB_FQ // NH)), kv(CB_FK), kv(CB_FV),
                  pl.BlockSpec((8, bk), lambda j, i: (0, j)), stat, stat, stat, qrow],
        out_specs=[krow, krow, pl.BlockSpec((NH, 8, bk), lambda j, i: (0, 0, j))],
        out_shape=[jax.ShapeDtypeStruct((T, WH), BF16), jax.ShapeDtypeStruct((T, WH), BF16),
                   jax.ShapeDtypeStruct((NH, 8, T), F32)],
        scratch_shapes=[pltpu.VMEM((bk, WH), F32), pltpu.VMEM((bk, WH), F32), pltpu.VMEM((8, bk), F32)],
        compiler_params=_params(("parallel", "arbitrary")),
    )(proj, proj, proj, ct, cq, lse, delta, do)


def _mem_probs(q, mk):
    s = _dot(q, mk, NT) * SCALE
    e = jnp.exp(s - jnp.max(s, axis=1, keepdims=True))
    return e / jnp.sum(e, axis=1, keepdims=True)


def _mem_fwd(proj, mem_kv):
    T = proj.shape[0]
    tr = _tile(T, (512, 256, 128))
    M = mem_kv.shape[0]

    def body(q_ref, mk_ref, mv_ref, o_ref):
        o_ref[...] = _dot(_mem_probs(q_ref[...], mk_ref[...]), mv_ref[...], NN)

    return pl.pallas_call(
        body, name="mem_fwd", grid=(NM, T // tr),
        in_specs=[pl.BlockSpec((tr, HD), lambda h, i: (i, CB_MQ + h)),
                  pl.BlockSpec((M, HD), lambda h, i: (0, h)), pl.BlockSpec((M, HD), lambda h, i: (0, NM + h))],
        out_specs=pl.BlockSpec((tr, HD), lambda h, i: (i, h)),
        out_shape=jax.ShapeDtypeStruct((T, WM), F32),
        compiler_params=_params(("parallel", "parallel")),
    )(proj, mem_kv, mem_kv)


def _mem_bwd(proj, mem_kv, do):
    T = proj.shape[0]
    tr = _tile(T, (512, 256, 128))
    M = mem_kv.shape[0]

    def body(q_ref, mk_ref, mv_ref, do_ref, dq_ref, dmk_ref, dmv_ref):
        @pl.when(pl.program_id(1) == 0)
        def _():
            dmk_ref[...] = jnp.zeros_like(dmk_ref)
            dmv_ref[...] = jnp.zeros_like(dmv_ref)

        q, mk, d_o = q_ref[...], mk_ref[...], do_ref[...]
        p = _mem_probs(q, mk)
        dmv_ref[...] += _dot(p, d_o, TN)
        dp = _dot(d_o, mv_ref[...], NT)
        ds = p * (dp - jnp.sum(p * dp, axis=1, keepdims=True))
        dq_ref[...] = (_dot(ds, mk, NN) * SCALE).astype(dq_ref.dtype)
        dmk_ref[...] += _dot(ds, q, TN) * SCALE

    acc = pl.BlockSpec((M, HD), lambda h, i: (0, h))
    row = pl.BlockSpec((tr, HD), lambda h, i: (i, h))
    return pl.pallas_call(
        body, name="mem_bwd", grid=(NM, T // tr),
        in_specs=[pl.BlockSpec((tr, HD), lambda h, i: (i, CB_MQ + h)),
                  pl.BlockSpec((M, HD), lambda h, i: (0, h)), pl.BlockSpec((M, HD), lambda h, i: (0, NM + h)), row],
        out_specs=[row, acc, acc],
        out_shape=[jax.ShapeDtypeStruct((T, WM), BF16), jax.ShapeDtypeStruct((M, WM), F32),
                   jax.ShapeDtypeStruct((M, WM), F32)],
        compiler_params=_params(("parallel", "arbitrary")),
    )(proj, mem_kv, mem_kv, do)


def _mesh_place():
    x, y, c = lax.axis_index("x"), lax.axis_index("y"), lax.axis_index("c")
    return x, y, c


CHIP_FLIPS = (4, 2, 6)
CHIP_OF_SLOT = (0,) + CHIP_FLIPS


def _peer(x, y, c, k):
    px = 1 - x if k & 4 else x
    py = 1 - y if k & 2 else y
    pc = 1 - c if k & 1 else c
    return (px, py, pc), 4 * px + 2 * py + pc


def _all_gather(shards, pad_rows):
    n = len(shards)
    npad = sum(1 for p in pad_rows if p)
    zeros = jnp.zeros((max(pad_rows) or 16, shards[0].shape[1]), shards[0].dtype)

    def body(*refs):
        ins, z_ref, outs = refs[:n], refs[n], refs[n + 1:2 * n + 1]
        send_sems, recv_sems, loc_sems = refs[2 * n + 1:]
        x, y, c = _mesh_place()
        me = 4 * x + 2 * y + c
        copies, sends = [], []
        ip = 0
        for w in range(n):
            r = ins[w].shape[0]
            dst = outs[w].at[pl.ds(pl.multiple_of(me * r, 16), r), :]
            cp = pltpu.make_async_copy(ins[w], dst, loc_sems.at[w])
            cp.start()
            copies.append(cp)
            if pad_rows[w]:
                cp = pltpu.make_async_copy(z_ref.at[pl.ds(0, pad_rows[w]), :],
                                           outs[w].at[pl.ds(NDEV * r, pad_rows[w]), :], loc_sems.at[n + ip])
                cp.start()
                copies.append(cp)
                ip += 1
            for s, k in enumerate((1,) + CHIP_FLIPS):
                peer, _ = _peer(x, y, c, k)
                cp = pltpu.make_async_remote_copy(src_ref=ins[w], dst_ref=dst, send_sem=send_sems.at[w, s],
                                                  recv_sem=recv_sems.at[w, s], device_id=peer, device_id_type=MESH)
                cp.start()
                (copies if s == 0 else sends).append(cp)
        sibling, _ = _peer(x, y, c, 1)
        for w in range(n):
            r = ins[w].shape[0]
            for s, k in enumerate(CHIP_FLIPS):
                _, pidx = _peer(x, y, c, k)
                rows = outs[w].at[pl.ds(pl.multiple_of(pidx * r, 16), r), :]
                pltpu.make_async_remote_copy(src_ref=rows, dst_ref=rows, send_sem=send_sems.at[w, 1 + s],
                                             recv_sem=recv_sems.at[w, 1 + s], device_id=sibling,
                                             device_id_type=MESH).wait_recv()
                cp = pltpu.make_async_remote_copy(src_ref=rows, dst_ref=rows, send_sem=send_sems.at[w, 4 + s],
                                                  recv_sem=recv_sems.at[w, 4 + s], device_id=sibling, device_id_type=MESH)
                cp.start()
                copies.append(cp)
        for cp in copies:
            cp.wait()
        for cp in sends:
            cp.wait_send()

    any_spec = pl.BlockSpec(memory_space=pl.ANY)
    return pl.pallas_call(
        body, name="all_gather_weights",
        in_specs=[any_spec] * (n + 1), out_specs=[any_spec] * n,
        out_shape=[jax.ShapeDtypeStruct((NDEV * s.shape[0] + p, s.shape[1]), s.dtype) for s, p in zip(shards, pad_rows)],
        scratch_shapes=[pltpu.SemaphoreType.DMA((n, NDEV - 1)), pltpu.SemaphoreType.DMA((n, NDEV - 1)),
                        pltpu.SemaphoreType.DMA((n + npad,))],
        compiler_params=pltpu.CompilerParams(has_side_effects=True),
    )(*shards, zeros)


def _exchange_in_chip(grads, shard_rows):
    n = len(grads)
    ns = len(CHIP_OF_SLOT)

    def body(*refs):
        ins, theirs = refs[:n], refs[n:2 * n]
        send_sems, recv_sems = refs[2 * n:]
        x, y, c = _mesh_place()
        sibling, _ = _peer(x, y, c, 1)
        copies = []
        for w in range(n):
            r = shard_rows[w]
            for s, k in enumerate(CHIP_OF_SLOT):
                _, other = _peer(x, y, c, k | 1)
                cp = pltpu.make_async_remote_copy(
                    src_ref=ins[w].at[pl.ds(pl.multiple_of(other * r, 16), r), :], dst_ref=theirs[w].at[s],
                    send_sem=send_sems.at[w, s], recv_sem=recv_sems.at[w, s], device_id=sibling, device_id_type=MESH)
                cp.start()
                copies.append(cp)
        for cp in copies:
            cp.wait()

    any_spec = pl.BlockSpec(memory_space=pl.ANY)
    return pl.pallas_call(
        body, name="reduce_scatter_in_chip",
        in_specs=[any_spec] * n, out_specs=[any_spec] * n,
        out_shape=[jax.ShapeDtypeStruct((ns, r, g.shape[1]), g.dtype) for g, r in zip(grads, shard_rows)],
        scratch_shapes=[pltpu.SemaphoreType.DMA((n, ns)), pltpu.SemaphoreType.DMA((n, ns))],
        compiler_params=pltpu.CompilerParams(has_side_effects=True),
    )(*grads)


def _pair_sum(grad, theirs, name):
    ns, r, c = theirs.shape
    tr = _tile(r, (128, 64, 32, 16))
    per_block = r // tr

    def body(a_ref, b_ref, o_ref):
        o_ref[...] = (a_ref[...].astype(F32) + b_ref[...].astype(F32)).astype(o_ref.dtype)

    def owner_rows(s, i):
        x, y, c_ = _mesh_place()
        fx, fy = s % 2, s // 2
        px, py = x + fx - 2 * x * fx, y + fy - 2 * y * fy
        return ((4 * px + 2 * py + c_) * per_block + i, 0)

    slot = pl.BlockSpec((None, tr, c), lambda s, i: (s, i, 0))
    return pl.pallas_call(
        body, name=name, grid=(ns, per_block),
        in_specs=[pl.BlockSpec((tr, c), owner_rows), slot], out_specs=slot,
        out_shape=jax.ShapeDtypeStruct((ns, r, c), theirs.dtype),
        compiler_params=_params(("parallel", "parallel")),
    )(grad, theirs)


def _exchange_between_chips(pairs):
    n = len(pairs)
    ns = len(CHIP_OF_SLOT) - 1

    def body(*refs):
        ins, outs = refs[:n], refs[n:2 * n]
        send_sems, recv_sems = refs[2 * n:]
        x, y, c = _mesh_place()
        copies = []
        for w in range(n):
            for s, k in enumerate(CHIP_OF_SLOT[1:]):
                peer, _ = _peer(x, y, c, k)
                cp = pltpu.make_async_remote_copy(src_ref=ins[w].at[s + 1], dst_ref=outs[w].at[s], send_sem=send_sems.at[w, s],
                                                  recv_sem=recv_sems.at[w, s], device_id=peer, device_id_type=MESH)
                cp.start()
                copies.append(cp)
        for cp in copies:
            cp.wait()

    any_spec = pl.BlockSpec(memory_space=pl.ANY)
    return pl.pallas_call(
        body, name="reduce_scatter_between_chips",
        in_specs=[any_spec] * n, out_specs=[any_spec] * n,
        out_shape=[jax.ShapeDtypeStruct((ns,) + p.shape[1:], p.dtype) for p in pairs],
        scratch_shapes=[pltpu.SemaphoreType.DMA((n, ns)), pltpu.SemaphoreType.DMA((n, ns))],
        compiler_params=pltpu.CompilerParams(has_side_effects=True),
    )(*pairs)


def _sum_chips(pair, recv, name):
    _, r, c = recv.shape
    tr = _tile(r, (128, 64, 32, 16))

    def body(p_ref, x_ref, o_ref):
        acc = p_ref[...].astype(F32)
        for s in range(x_ref.shape[0]):
            acc = acc + x_ref[s].astype(F32)
        o_ref[...] = acc

    return pl.pallas_call(
        body, name=name, grid=(r // tr,),
        in_specs=[pl.BlockSpec((None, tr, c), lambda i: (0, i, 0)), pl.BlockSpec((recv.shape[0], tr, c), lambda i: (0, i, 0))],
        out_specs=pl.BlockSpec((tr, c), lambda i: (i, 0)),
        out_shape=jax.ShapeDtypeStruct((r, c), F32),
        compiler_params=_params(("parallel",)),
    )(pair, recv)


def _all_reduce_small(part):
    R, W = part.shape

    def body(x_ref, o_ref, buf, send_sems, recv_sems):
        x, y, c = _mesh_place()
        me = 4 * x + 2 * y + c
        buf[me] = x_ref[...]
        copies = []
        for k in range(1, NDEV):
            peer, _ = _peer(x, y, c, k)
            cp = pltpu.make_async_remote_copy(src_ref=x_ref, dst_ref=buf.at[me], send_sem=send_sems.at[k - 1],
                                              recv_sem=recv_sems.at[k - 1], device_id=peer, device_id_type=MESH)
            cp.start()
            copies.append(cp)
        for cp in copies:
            cp.wait()
        acc = buf[0]
        for d in range(1, NDEV):
            acc = acc + buf[d]
        o_ref[...] = acc

    vm = pl.BlockSpec(memory_space=pltpu.VMEM)
    return pl.pallas_call(
        body, name="all_reduce_small", in_specs=[vm], out_specs=vm,
        out_shape=jax.ShapeDtypeStruct((R, W), F32),
        scratch_shapes=[pltpu.VMEM((NDEV, R, W), F32), pltpu.SemaphoreType.DMA((NDEV - 1,)),
                        pltpu.SemaphoreType.DMA((NDEV - 1,))],
        compiler_params=pltpu.CompilerParams(has_side_effects=True),
    )(part)


def _adam_math(w, g, m, v):
    m2 = ADAM_B1 * m + (1.0 - ADAM_B1) * g
    v2 = ADAM_B2 * v + (1.0 - ADAM_B2) * (g * g)
    m_hat = m2 / (1.0 - ADAM_B1 ** ADAM_STEP)
    v_hat = v2 / (1.0 - ADAM_B2 ** ADAM_STEP)
    delta = -ADAM_LR * (m_hat / (jnp.sqrt(v_hat) + ADAM_EPS) + ADAM_WD * w)
    return delta, m2, v2


def _adamw(w, g, m, v, name):
    r, c = w.shape
    tr = r
    for cand in (1024, 512, 256, 128, 64, 32, 16, 8):
        if r % cand == 0 and cand * c <= 256 * 1024:
            tr = cand
            break

    def body(w_ref, g_ref, m_ref, v_ref, d_ref, m2_ref, v2_ref):
        d_ref[...], m2_ref[...], v2_ref[...] = _adam_math(w_ref[...], g_ref[...], m_ref[...], v_ref[...])

    spec = pl.BlockSpec((tr, c), lambda i: (i, 0))
    return pl.pallas_call(
        body, name=name, grid=(r // tr,), in_specs=[spec] * 4, out_specs=[spec] * 3,
        out_shape=[jax.ShapeDtypeStruct((r, c), F32)] * 3,
        compiler_params=_params(("parallel",)),
    )(w, g, m, v)


GAINS = ("ffn1_pre", "ffn1_post", "mix_pre", "mix_post", "mem_norm", "ffn2_pre", "ffn2_post")
GAIN_ROWS = D // HD
ROW_LB = len(GAINS) * GAIN_ROWS
ROWS_GRAD_IN = ROW_LB + 24
ROWS_PACKED = ROW_LB + 32


def _small_update(gsum, w_p, m_p, v_p):
    def body(g_ref, w_ref, m_ref, v_ref, go_ref, d_ref, m2_ref, v2_ref):
        a0 = w_ref[ROW_LB:ROW_LB + 8, :]
        a1 = w_ref[ROW_LB + 8:ROW_LB + 16, :]
        mx = jnp.maximum(a0, a1)
        e0, e1 = jnp.exp(a0 - mx), jnp.exp(a1 - mx)
        lb = e0 / (e0 + e1)
        da0 = g_ref[ROW_LB:ROW_LB + 8, :] * lb * (1.0 - lb)
        g = jnp.concatenate([g_ref[0:ROW_LB, :], da0, -da0, g_ref[ROW_LB + 8:ROWS_GRAD_IN, :]], axis=0)
        go_ref[...] = g
        d_ref[...], m2_ref[...], v2_ref[...] = _adam_math(w_ref[...], g, m_ref[...], v_ref[...])

    vm = pl.BlockSpec(memory_space=pltpu.VMEM)
    return pl.pallas_call(
        body, name="small_update", in_specs=[vm] * 4, out_specs=[vm] * 4,
        out_shape=[jax.ShapeDtypeStruct((ROWS_PACKED, HD), F32)] * 4,
    )(gsum, w_p, m_p, v_p)


def _rows8(a):
    a = a.reshape(-1)
    rows = -(-a.shape[0] // HD)
    rows8 = -(-rows // 8) * 8
    return jnp.pad(a, (0, rows8 * HD - a.shape[0])).reshape(rows8, HD)


def _pack_small(gains, lb0, lb1, gnorm, fb):
    return jnp.concatenate([_rows8(g) for g in gains] + [_rows8(lb0), _rows8(lb1), _rows8(gnorm), _rows8(fb)], axis=0)


def _unpack_small(p):
    out = {}
    for i, name in enumerate(GAINS):
        out[name] = p[i * GAIN_ROWS:(i + 1) * GAIN_ROWS].reshape(1, D)
    lb0 = p[ROW_LB:ROW_LB + NH].reshape(1, WH)
    lb1 = p[ROW_LB + 8:ROW_LB + 8 + NH].reshape(1, WH)
    out["hgrn_lb"] = jnp.concatenate([lb0, lb1], axis=0)
    out["hgrn_gnorm"] = p[ROW_LB + 16:ROW_LB + 16 + NH].reshape(1, WH)
    out["fox_fb"] = p[ROW_LB + 24:ROW_LB + 25, 0:NH]
    return out


def _ffn_forward(xin, pre, post, wg_t, wu_t, wd, tag):
    T = xin.shape[0]
    tr = _tile(T, (256, 128))
    (n,) = _rowwise(_norm_fn, [(xin, 0)], [(pre, None)], [BF16], f"{tag}_pre", tr, D, 1)
    g, u, a = _ffn_up(n, wg_t, wu_t, f"{tag}_up")
    h = _mm(a, wd, "nn", F32, f"{tag}_down")
    (xout,) = _rowwise(functools.partial(_resid_fn, 0.5), [(xin, 0), (h, 0)], [(post, None)], [F32], f"{tag}_post", tr, D, 1)
    return xout, (xin, n, g, u, a, h)


def _ffn_backward(dxout, saved, pre, post, wg_t, wu_t, wd, tag):
    xin, n, g, u, a, h = saved
    T = xin.shape[0]
    tr = _tile(T, (256, 128))
    dh, dpost = _rowwise_bwd(functools.partial(_resid_h_fn, 0.5), [(h, 0)], [(post, None)], [(dxout, 0)], [0], [BF16],
                             f"{tag}_post_bwd", tr, D, 1)
    dwd = _mm(a, dh, "tn", BF16, f"{tag}_dwd")
    dg, du = _ffn_act_bwd(dh, wd, g, u, f"{tag}_act_bwd")
    dwg = _mm(dg, n, "tn", BF16, f"{tag}_dwg")
    dwu = _mm(du, n, "tn", BF16, f"{tag}_dwu")
    dn = _mm(dg, wg_t, "nn", F32, f"{tag}_dn_g")
    dn = _mm(du, wu_t, "nn", F32, f"{tag}_dn_u", add=dn)
    dxin, dpre = _rowwise_bwd(_norm_res_fn, [(xin, 0)], [(pre, None)], [(dxout, 0), (dn, 0)], [0], [F32],
                              f"{tag}_pre_bwd", tr, D, 1)
    return dxin, (dwg, dwu, dwd), dpre, dpost


def _local_step(x, mem, tgt, small, wts):
    T = x.shape[0]
    tr = _tile(T, (256, 128))
    fb_pad = jnp.pad(small["fox_fb"], ((0, 0), (0, HD - NH)))

    x1, ffn1_saved = _ffn_forward(x, small["ffn1_pre"], small["ffn1_post"], wts["ffn1_wg"], wts["ffn1_wu"],
                                  wts["ffn1_wd"], "ffn1")
    (un,) = _rowwise(_norm_fn, [(x1, 0)], [(small["mix_pre"], None)], [BF16], "mix_pre", tr, D, 1)
    proj = _mm(un, wts["w_in"], "nn", F32, "proj")
    z = _mm(un, wts["w_gate"], "nt", F32, "gate_logits")
    (memn,) = _rowwise(_norm_fn, [(mem, 0)], [(small["mem_norm"], None)], [BF16], "mem_norm", mem.shape[0], D, 1)
    mem_kv = _mm(memn, wts["w_mem_kv"], "nn", F32, "mem_kv")

    o_raw, states = _hgrn_fwd(proj, small["hgrn_lb"])
    (o_h,) = _rowwise(_hpost_fn, [(o_raw, 0), (proj, CB_HOG)], [(small["hgrn_gnorm"], 0)], [BF16], "hgrn_post",
                      tr, HD, NH)
    ct, cq = _fox_cum(proj, fb_pad)
    o_f, lse = _fox_fwd(proj, ct, cq)
    o_m = _mem_fwd(proj, mem_kv)

    yh = _mm(o_h, wts["w_hgrn_out"], "nt", F32, "hgrn_out")
    yf = _mm(o_f, wts["w_fox_out"], "nt", F32, "fox_out")
    ym = _mm(o_m, wts["w_mem_out"], "nt", F32, "mem_out")
    zc = D // 512
    merge_rows = [(z, 0), (z, zc), (z, 2 * zc), (yh, 0), (yf, 0), (ym, 0)]
    (merged,) = _rowwise(_merge_fn, merge_rows, [], [BF16], "merge", tr, 512, zc)
    m = _mm(merged, wts["w_o"], "nn", F32, "mix_out")
    (x2,) = _rowwise(functools.partial(_resid_fn, 1.0), [(x1, 0), (m, 0)], [(small["mix_post"], None)], [F32], "mix_post",
                     tr, D, 1)
    x3, ffn2_saved = _ffn_forward(x2, small["ffn2_pre"], small["ffn2_post"], wts["ffn2_wg"], wts["ffn2_wu"],
                                  wts["ffn2_wd"], "ffn2")
    dy, loss_part = _loss(x3, tgt, "loss")

    gw, gs = {}, {}
    dx2, (gw["ffn2_wg"], gw["ffn2_wu"], gw["ffn2_wd"]), gs["ffn2_pre"], gs["ffn2_post"] = _ffn_backward(
        dy, ffn2_saved, small["ffn2_pre"], small["ffn2_post"], wts["ffn2_wg"], wts["ffn2_wu"], wts["ffn2_wd"], "ffn2")

    dm, gs["mix_post"] = _rowwise_bwd(functools.partial(_resid_h_fn, 1.0), [(m, 0)], [(small["mix_post"], None)],
                                      [(dx2, 0)], [0], [BF16], "mix_post_bwd", tr, D, 1)
    dmerged = _mm(dm, wts["w_o"], "nt", F32, "d_merged")
    gw["w_o"] = _mm(merged, dm, "tn", BF16, "d_w_o")
    dz0, dz1, dz2, dyh, dyf, dym = _rowwise_bwd(_merge_fn, merge_rows, [], [(dmerged, 0)], [0, 1, 2, 3, 4, 5], [BF16] * 6,
                                                "merge_bwd", tr, 512, zc)
    dz = jnp.concatenate([dz0, dz1, dz2], axis=1)
    gw["w_gate"] = _mm(dz, un, "tn", BF16, "d_w_gate")
    dun = _mm(dz, wts["w_gate"], "nn", F32, "d_un_gate")

    do_h = _mm(dyh, wts["w_hgrn_out"], "nn", F32, "d_o_h")
    gw["w_hgrn_out"] = _mm(dyh, o_h, "tn", BF16, "d_w_hgrn_out")
    do_f = _mm(dyf, wts["w_fox_out"], "nn", F32, "d_o_f")
    gw["w_fox_out"] = _mm(dyf, o_f, "tn", BF16, "d_w_fox_out")
    do_m = _mm(dym, wts["w_mem_out"], "nn", F32, "d_o_m")
    gw["w_mem_out"] = _mm(dym, o_m, "tn", BF16, "d_w_mem_out")

    do_raw, dhog, gs["hgrn_gnorm"] = _rowwise_bwd(_hpost_fn, [(o_raw, 0), (proj, CB_HOG)], [(small["hgrn_gnorm"], 0)],
                                                  [(do_h, 0)], [0, 1], [F32, BF16], "hgrn_post_bwd", tr, HD, NH)
    dhq, dhf, dhi, gs["hgrn_lb"] = _hgrn_bwd(proj, small["hgrn_lb"], states, do_raw)
    dfq, delta = _fox_bwd_dq(proj, ct, cq, lse, do_f)
    dfk, dfv, dc = _fox_bwd_dkv(proj, ct, cq, lse, delta, do_f)
    dff, dfb = _fox_cum_bwd(dc, proj, fb_pad)
    gs["fox_fb"] = dfb
    dmq, dmk, dmv = _mem_bwd(proj, mem_kv, do_m)

    dproj = jnp.concatenate([dhq, dhf, dhi, dhog, dfq, dfk, dfv, dff, dmq, jnp.zeros((T, HD), BF16)], axis=1)
    gw["w_in"] = _mm(un, dproj, "tn", BF16, "d_w_in")
    dun = _mm(dproj, wts["w_in"], "nt", F32, "d_un_proj", add=dun)
    dx1, gs["mix_pre"] = _rowwise_bwd(_norm_res_fn, [(x1, 0)], [(small["mix_pre"], None)], [(dx2, 0), (dun, 0)], [0], [F32],
                                      "mix_pre_bwd", tr, D, 1)

    dmem_kv = jnp.concatenate([dmk, dmv], axis=1)
    gw["w_mem_kv"] = _mm(memn, dmem_kv, "tn", BF16, "d_w_mem_kv")
    dmemn = _mm(dmem_kv, wts["w_mem_kv"], "nt", F32, "d_memn")
    _, gs["mem_norm"] = _rowwise_bwd(_norm_fn, [(mem, 0)], [(small["mem_norm"], None)], [(dmemn, 0)], [0], [BF16],
                                     "mem_norm_bwd", mem.shape[0], D, 1)

    dx, (gw["ffn1_wg"], gw["ffn1_wu"], gw["ffn1_wd"]), gs["ffn1_pre"], gs["ffn1_post"] = _ffn_backward(
        dx1, ffn1_saved, small["ffn1_pre"], small["ffn1_post"], wts["ffn1_wg"], wts["ffn1_wu"], wts["ffn1_wd"], "ffn1")
    return loss_part, dx, gw, gs


BIG = ("ffn1_wg", "ffn1_wu", "ffn1_wd", "w_in", "w_mem_kv", "w_hgrn_out", "w_fox_out", "w_mem_out", "w_gate", "w_o",
       "ffn2_wg", "ffn2_wu", "ffn2_wd")
TRANSPOSED = ("ffn1_wg", "ffn1_wu", "ffn2_wg", "ffn2_wu", "w_hgrn_out", "w_fox_out", "w_mem_out", "w_gate")
FFN_PAD = {"ffn1_wg": FP - F, "ffn1_wu": FP - F, "ffn1_wd": FP - F, "ffn2_wg": FP - F, "ffn2_wu": FP - F,
           "ffn2_wd": FP - F}
SMALL = GAINS + ("hgrn_lb", "hgrn_gnorm", "fox_fb")
WEIGHTS = ("ffn1_pre", "ffn1_post", "ffn1_wg", "ffn1_wu", "ffn1_wd", "mix_pre", "mix_post", "mem_norm", "w_in", "hgrn_lb",
           "hgrn_gnorm", "fox_fb", "w_mem_kv", "w_hgrn_out", "w_fox_out", "w_mem_out", "w_gate", "w_o", "ffn2_pre",
           "ffn2_post", "ffn2_wg", "ffn2_wu", "ffn2_wd")


def _to_gather_layout(name, w):
    if name in TRANSPOSED:
        w = w.T
    if name == "w_in":
        r = w.shape[0]
        w = jnp.concatenate([w[:, :MQ_COL], jnp.zeros((r, FF_COL + HD - MQ_COL), w.dtype), w[:, MQ_COL:],
                             jnp.zeros((r, P - FF_COL - HD - WM), w.dtype)], axis=1)
    return w.astype(BF16)


def _from_gather_layout(name, g):
    if name == "w_in":
        g = jnp.concatenate([g[:, :MQ_COL], g[:, FF_COL + HD:FF_COL + HD + WM]], axis=1)
    if name in TRANSPOSED:
        g = g.T
    return g


def kernel(x, mem, ffn1_pre, ffn1_post, ffn1_wg, ffn1_wu, ffn1_wd, mix_pre, mix_post, mem_norm, w_in, hgrn_lb, hgrn_gnorm, fox_fb, w_mem_kv, w_hgrn_out, w_fox_out, w_mem_out, w_gate, w_o, ffn2_pre, ffn2_post, ffn2_wg, ffn2_wu, ffn2_wd, loss_target, m_ffn1_pre, m_ffn1_post, m_ffn1_wg, m_ffn1_wu, m_ffn1_wd, m_mix_pre, m_mix_post, m_mem_norm, m_w_in, m_hgrn_lb, m_hgrn_gnorm, m_fox_fb, m_w_mem_kv, m_w_hgrn_out, m_w_fox_out, m_w_mem_out, m_w_gate, m_w_o, m_ffn2_pre, m_ffn2_post, m_ffn2_wg, m_ffn2_wu, m_ffn2_wd, v_ffn1_pre, v_ffn1_post, v_ffn1_wg, v_ffn1_wu, v_ffn1_wd, v_mix_pre, v_mix_post, v_mem_norm, v_w_in, v_hgrn_lb, v_hgrn_gnorm, v_fox_fb, v_w_mem_kv, v_w_hgrn_out, v_w_fox_out, v_w_mem_out, v_w_gate, v_w_o, v_ffn2_pre, v_ffn2_post, v_ffn2_wg, v_ffn2_wu, v_ffn2_wd):
    a = dict(locals())
    small = {n: a[n] for n in SMALL}
    shard = {n: a[n][0] if a[n].ndim == 3 else a[n] for n in BIG}

    blocks = [_to_gather_layout(n, shard[n]) for n in BIG]
    gathered = _all_gather(blocks, [FFN_PAD.get(n, 0) for n in BIG])
    wts = dict(zip(BIG, gathered))

    loss_part, dx, gw, gs = _local_step(x[0], mem[0], loss_target[0], small, wts)
    loss = lax.psum(0.5 / D * jnp.sum(loss_part), ("x", "y", "c"))

    theirs = _exchange_in_chip([gw[n] for n in BIG], [b.shape[0] for b in blocks])
    pairs = [_pair_sum(gw[n], t_, f"pair_{n}") for n, t_ in zip(BIG, theirs)]
    recv = _exchange_between_chips(pairs)
    grads, deltas, new_m, new_v = {}, {}, {}, {}
    for n, pr, rb in zip(BIG, pairs, recv):
        g = _from_gather_layout(n, _sum_chips(pr, rb, f"sum_{n}"))
        d, m2, v2 = _adamw(shard[n], g, a["m_" + n].reshape(g.shape), a["v_" + n].reshape(g.shape), f"adamw_{n}")
        full = a[n].shape
        grads[n], deltas[n], new_m[n], new_v[n] = g.reshape(full), d.reshape(full), m2.reshape(full), v2.reshape(full)

    part = jnp.concatenate([_rows8(gs[n]) for n in GAINS] + [_rows8(gs["hgrn_lb"]), _rows8(gs["hgrn_gnorm"]),
                                                             _rows8(gs["fox_fb"][:, :NH])], axis=0)
    gsum = _all_reduce_small(part)

    def packed(prefix):
        lb = a[prefix + "hgrn_lb"]
        return _pack_small([a[prefix + n] for n in GAINS], lb[0], lb[1], a[prefix + "hgrn_gnorm"], a[prefix + "fox_fb"])

    g_p, d_p, m_p, v_p = _small_update(gsum, packed(""), packed("m_"), packed("v_"))
    for dst, p in ((grads, g_p), (deltas, d_p), (new_m, m_p), (new_v, v_p)):
        dst.update(_unpack_small(p))

    return (loss, dx[None], *[grads[n] for n in WEIGHTS], *[deltas[n] for n in WEIGHTS],
            *[new_m[n] for n in WEIGHTS], *[new_v[n] for n in WEIGHTS])
```

```python
import functools

import jax
import jax.numpy as jnp
from jax import lax
from jax.experimental import pallas as pl
from jax.experimental.pallas import tpu as pltpu

F32 = jnp.float32
BF16 = jnp.bfloat16
HIGHEST = lax.Precision.HIGHEST

NDEV = 8
D = 2048
F = 5504
FP = 5632
HD = 128
NH = 6
NM = 4
WH = NH * HD
WM = NM * HD
P = 6144
FF_COL = 5376
MQ_COL = 5382
CHUNK = 64
EPS = 1e-6
SCALE = HD ** -0.5
NEG = -1e30
VMEM_LIMIT = 48 * 1024 * 1024

CB_HQ, CB_HF, CB_HI, CB_HOG, CB_FQ, CB_FK, CB_FV, CB_FF, CB_MQ = 0, 6, 12, 18, 24, 30, 36, 42, 43

ADAM_LR, ADAM_B1, ADAM_B2, ADAM_EPS, ADAM_WD, ADAM_STEP = 0.001, 0.9, 0.999, 1e-08, 0.01, 10

NT = (((1,), (1,)), ((), ()))
NN = (((1,), (0,)), ((), ()))
TN = (((0,), (0,)), ((), ()))
MESH = pl.DeviceIdType.MESH


def _params(sem=None, **kw):
    return pltpu.CompilerParams(dimension_semantics=sem, vmem_limit_bytes=VMEM_LIMIT, **kw)


def _tile(n, prefs):
    for p in prefs:
        if p <= n and n % p == 0:
            return p
    return n


def _dot(a, b, dims):
    return lax.dot_general(a.astype(BF16), b.astype(BF16), dims, preferred_element_type=F32)


def _mm(a, b, mode, out_dtype, name, add=None):
    if mode == "nn":
        (M, K), (K2, N) = a.shape, b.shape
    elif mode == "nt":
        (M, K), (N, K2) = a.shape, b.shape
    else:
        (K, M), (K2, N) = a.shape, b.shape
    assert K == K2, (a.shape, b.shape, mode)
    if mode == "tn":
        tm = _tile(M, (512, 256, 128))
        tn = _tile(N, (1024, 768, 512, 256, 128))
        tk = _tile(K, (4096, 2048, 1024, 512, 256, 128))
    else:
        tm = _tile(M, (1024, 512, 256, 128)) if K <= 2048 else _tile(M, (512, 256, 128))
        tn = _tile(N, (512, 768, 256, 128))
        tk = K if K <= 6144 else _tile(K, (2048, 1024, 512, 256, 128))
    nk = K // tk
    dims = {"nn": NN, "nt": NT, "tn": TN}[mode]
    has_add = add is not None

    def body(*refs):
        a_ref, b_ref = refs[0], refs[1]
        c_ref = refs[2] if has_add else None
        o_ref = refs[3] if has_add else refs[2]
        acc_ref = refs[-1]
        k = pl.program_id(2)
        part = _dot(a_ref[...], b_ref[...], dims)

        def finish(r):
            if has_add:
                r = r + c_ref[...].astype(F32)
            o_ref[...] = r.astype(o_ref.dtype)

        if nk == 1:
            finish(part)
        else:
            @pl.when(k == 0)
            def _():
                acc_ref[...] = part

            @pl.when(k > 0)
            def _():
                acc_ref[...] += part

            @pl.when(k == nk - 1)
            def _():
                finish(acc_ref[...])

    if mode == "nn":
        a_spec = pl.BlockSpec((tm, tk), lambda i, j, k: (i, k))
        b_spec = pl.BlockSpec((tk, tn), lambda i, j, k: (k, j))
    elif mode == "nt":
        a_spec = pl.BlockSpec((tm, tk), lambda i, j, k: (i, k))
        b_spec = pl.BlockSpec((tn, tk), lambda i, j, k: (j, k))
    else:
        a_spec = pl.BlockSpec((tk, tm), lambda i, j, k: (k, i))
        b_spec = pl.BlockSpec((tk, tn), lambda i, j, k: (k, j))
    o_spec = pl.BlockSpec((tm, tn), lambda i, j, k: (i, j))
    in_specs = [a_spec, b_spec] + ([o_spec] if has_add else [])
    args = (a, b) + ((add,) if has_add else ())
    return pl.pallas_call(
        body, name=name, grid=(M // tm, N // tn, nk), in_specs=in_specs, out_specs=o_spec,
        out_shape=jax.ShapeDtypeStruct((M, N), out_dtype),
        scratch_shapes=[pltpu.VMEM((tm, tn) if nk > 1 else (8, 128), F32)],
        compiler_params=_params(("parallel", "parallel", "arbitrary")),
    )(*args)


def _ffn_up(n, wg_t, wu_t, name):
    T = n.shape[0]
    tm = _tile(T, (1024, 512, 256, 128))
    tn = 512

    def body(n_ref, wg_ref, wu_ref, g_ref, u_ref, a_ref):
        x = n_ref[...]
        g = _dot(x, wg_ref[...], NT)
        u = _dot(x, wu_ref[...], NT)
        g_ref[...] = g
        u_ref[...] = u
        a_ref[...] = (g * jax.nn.sigmoid(g) * u).astype(BF16)

    w_spec = pl.BlockSpec((tn, D), lambda i, j: (j, 0))
    o_spec = pl.BlockSpec((tm, tn), lambda i, j: (i, j))
    return pl.pallas_call(
        body, name=name, grid=(T // tm, FP // tn),
        in_specs=[pl.BlockSpec((tm, D), lambda i, j: (i, 0)), w_spec, w_spec],
        out_specs=[o_spec, o_spec, o_spec],
        out_shape=[jax.ShapeDtypeStruct((T, FP), F32), jax.ShapeDtypeStruct((T, FP), F32),
                   jax.ShapeDtypeStruct((T, FP), BF16)],
        compiler_params=_params(("parallel", "parallel")),
    )(n, wg_t, wu_t)


def _ffn_act_bwd(dh, wd, g, u, name):
    T = dh.shape[0]
    tm = _tile(T, (1024, 512, 256, 128))
    tn = 512

    def body(dh_ref, wd_ref, g_ref, u_ref, dg_ref, du_ref):
        da = _dot(dh_ref[...], wd_ref[...], NT)
        g = g_ref[...]
        sg = jax.nn.sigmoid(g)
        dg_ref[...] = (da * u_ref[...] * (sg * (1.0 + g * (1.0 - sg)))).astype(dg_ref.dtype)
        du_ref[...] = (da * (g * sg)).astype(du_ref.dtype)

    tile = pl.BlockSpec((tm, tn), lambda i, j: (i, j))
    return pl.pallas_call(
        body, name=name, grid=(T // tm, FP // tn),
        in_specs=[pl.BlockSpec((tm, D), lambda i, j: (i, 0)), pl.BlockSpec((tn, D), lambda i, j: (j, 0)), tile, tile],
        out_specs=[tile, tile],
        out_shape=[jax.ShapeDtypeStruct((T, FP), BF16), jax.ShapeDtypeStruct((T, FP), BF16)],
        compiler_params=_params(("parallel", "parallel")),
    )(dh, wd, g, u)


def _row_specs(rows, tr, cw):
    return [pl.BlockSpec((tr, cw), lambda j, i, o=off: (i, o + j)) for _, off in rows]


def _const_specs(consts, cw):
    specs = []
    for arr, off in consts:
        if off is None:
            specs.append(pl.BlockSpec(arr.shape, lambda j, i: (0, 0)))
        else:
            specs.append(pl.BlockSpec((arr.shape[0], cw), lambda j, i, o=off: (0, o + j)))
    return specs


def _rowwise(fn, rows, consts, out_dtypes, name, tr, cw, ncol):
    T = rows[0][0].shape[0]
    nr, nc = len(rows), len(consts)

    def body(*refs):
        r = [x[...].astype(F32) for x in refs[:nr]]
        c = [x[...] for x in refs[nr:nr + nc]]
        res = fn(*r, *c)
        for o_ref, v in zip(refs[nr + nc:], res):
            o_ref[...] = v.astype(o_ref.dtype)

    o_spec = pl.BlockSpec((tr, cw), lambda j, i: (i, j))
    return pl.pallas_call(
        body, name=name, grid=(ncol, T // tr),
        in_specs=_row_specs(rows, tr, cw) + _const_specs(consts, cw),
        out_specs=[o_spec] * len(out_dtypes),
        out_shape=[jax.ShapeDtypeStruct((T, ncol * cw), dt) for dt in out_dtypes],
        compiler_params=_params(("parallel", "parallel")),
    )(*[a for a, _ in rows], *[a for a, _ in consts])


def _rowwise_bwd(fn, rows, consts, cots, diff, ddtypes, name, tr, cw, ncol):
    T = rows[0][0].shape[0]
    nr, nc, nt, nd = len(rows), len(consts), len(cots), len(diff)

    def body(*refs):
        r = [x[...].astype(F32) for x in refs[:nr]]
        c = [x[...] for x in refs[nr:nr + nc]]
        ct = [x[...].astype(F32) for x in refs[nr + nc:nr + nc + nt]]
        drow_refs = refs[nr + nc + nt:nr + nc + nt + nd]
        dconst_refs = refs[nr + nc + nt + nd:]
        i = pl.program_id(1)

        def f(*args):
            full = list(r)
            for idx, a in zip(diff, args[:nd]):
                full[idx] = a
            return tuple(fn(*full, *args[nd:]))

        _, vjp = jax.vjp(f, *[r[d] for d in diff], *c)
        g = vjp(tuple(ct))
        for o_ref, v in zip(drow_refs, g[:nd]):
            o_ref[...] = v.astype(o_ref.dtype)

        @pl.when(i == 0)
        def _():
            for o_ref in dconst_refs:
                o_ref[...] = jnp.zeros_like(o_ref)

        for o_ref, v in zip(dconst_refs, g[nd:]):
            o_ref[...] += v

    o_spec = pl.BlockSpec((tr, cw), lambda j, i: (i, j))
    out_shape = [jax.ShapeDtypeStruct((T, ncol * cw), dt) for dt in ddtypes]
    out_shape += [jax.ShapeDtypeStruct(a.shape, F32) for a, _ in consts]
    return pl.pallas_call(
        body, name=name, grid=(ncol, T // tr),
        in_specs=_row_specs(rows, tr, cw) + _const_specs(consts, cw) + _row_specs(cots, tr, cw),
        out_specs=[o_spec] * nd + _const_specs(consts, cw),
        out_shape=out_shape,
        compiler_params=_params(("parallel", "arbitrary")),
    )(*[a for a, _ in rows], *[a for a, _ in consts], *[a for a, _ in cots])


def _rms(x, g):
    return x * lax.rsqrt(jnp.mean(x * x, axis=-1, keepdims=True) + EPS) * g


def _silu(x):
    return x * jax.nn.sigmoid(x)


def _norm_fn(x, g):
    return (_rms(x, g),)


def _norm_res_fn(x, g):
    return (x, _rms(x, g))


def _resid_fn(scale, x, h, g):
    return (x + scale * _rms(h, g),)


def _resid_h_fn(scale, h, g):
    return (scale * _rms(h, g),)


def _hpost_fn(o, hog, gn):
    return (_rms(o, gn) * _silu(hog),)


def _merge_fn(z0, z1, z2, yh, yf, ym):
    return (jax.nn.sigmoid(z0) * yh + jax.nn.sigmoid(z1) * yf + jax.nn.sigmoid(z2) * ym,)


def _loss(x3, tgt, name):
    T = x3.shape[0]
    tr = _tile(T, (256, 128))

    def body(x_ref, t_ref, dy_ref, s_ref):
        i = pl.program_id(0)
        e = x_ref[...] - t_ref[...]
        dy_ref[...] = e * (1.0 / D)
        col = jnp.sum(e * e, axis=0, keepdims=True)
        tot = col[:, 0:HD]
        for k in range(1, D // HD):
            tot = tot + col[:, k * HD:(k + 1) * HD]

        @pl.when(i == 0)
        def _():
            s_ref[...] = jnp.zeros_like(s_ref)

        s_ref[...] += tot

    spec = pl.BlockSpec((tr, D), lambda i: (i, 0))
    return pl.pallas_call(
        body, name=name, grid=(T // tr,), in_specs=[spec, spec],
        out_specs=[spec, pl.BlockSpec((1, HD), lambda i: (0, 0))],
        out_shape=[jax.ShapeDtypeStruct((T, D), F32), jax.ShapeDtypeStruct((1, HD), F32)],
        compiler_params=_params(("arbitrary",)),
    )(x3, tgt)


def _lower_bound(lb_ref):
    a0 = lb_ref[0:1, :]
    a1 = lb_ref[1:2, :]
    mx = jnp.maximum(a0, a1)
    e0 = jnp.exp(a0 - mx)
    return e0 / (e0 + jnp.exp(a1 - mx))


def _hgrn_prep(hq, hf, lb):
    g = lb + (1.0 - lb) * jax.nn.sigmoid(hf)
    return _silu(hq), 1.0 - g, jnp.log(g)


def _tri(n, upper):
    r = lax.broadcasted_iota(jnp.int32, (n, n), 0)
    c = lax.broadcasted_iota(jnp.int32, (n, n), 1)
    return (c >= r) if upper else (c <= r)


def _hgrn_factors(q, k, gl):
    low = _tri(CHUNK, False)
    b = lax.dot_general(low.astype(F32), gl, NN, precision=HIGHEST, preferred_element_type=F32)
    bl = b[CHUNK - 1:CHUNK, :]
    ref = b[CHUNK // 2 - 1:CHUNK // 2, :]
    eb = jnp.exp(b)
    ea = jnp.exp(b - ref)
    ebn = jnp.exp(ref - b)
    ek = jnp.exp(bl - b)
    ebl = jnp.exp(bl)
    return low, eb, ea, ebn, ek, ebl


def _hgrn_fwd(proj, hgrn_lb):
    T = proj.shape[0]
    cb = _tile(T, (512, 256, 128, 64))
    nchunk = cb // CHUNK

    def body(hq_ref, hf_ref, hi_ref, lb_ref, o_ref, st_ref, state):
        @pl.when(pl.program_id(0) == 0)
        def _():
            state[...] = jnp.zeros_like(state)

        lb = _lower_bound(lb_ref)

        def chunk(c, carry):
            r0 = pl.multiple_of(c * CHUNK, CHUNK)
            for h in range(NH):
                cols = slice(h * HD, (h + 1) * HD)
                q, k, gl = _hgrn_prep(hq_ref[pl.ds(r0, CHUNK), cols], hf_ref[pl.ds(r0, CHUNK), cols], lb[:, cols])
                v = hi_ref[pl.ds(r0, CHUNK), cols]
                low, eb, ea, ebn, ek, ebl = _hgrn_factors(q, k, gl)
                s_t = state[h]
                st_ref[c, h] = s_t
                pm = jnp.where(low, _dot(q * ea, k * ebn, NT), 0.0)
                o_ref[pl.ds(r0, CHUNK), cols] = _dot(q * eb, s_t, NT) + _dot(pm, v, NN)
                state[h] = s_t * ebl + _dot(v, k * ek, TN)
            return carry

        lax.fori_loop(0, nchunk, chunk, 0)

    def col(off):
        return pl.BlockSpec((cb, WH), lambda i, o=off: (i, o))

    return pl.pallas_call(
        body, name="hgrn_fwd", grid=(T // cb,),
        in_specs=[col(0), col(1), col(2), pl.BlockSpec((2, WH), lambda i: (0, 0))],
        out_specs=[pl.BlockSpec((cb, WH), lambda i: (i, 0)),
                   pl.BlockSpec((nchunk, NH, HD, HD), lambda i: (i, 0, 0, 0))],
        out_shape=[jax.ShapeDtypeStruct((T, WH), F32), jax.ShapeDtypeStruct((T // CHUNK, NH, HD, HD), F32)],
        scratch_shapes=[pltpu.VMEM((NH, HD, HD), F32)],
        compiler_params=_params(("arbitrary",)),
    )(proj, proj, proj, hgrn_lb)


def _hgrn_bwd(proj, hgrn_lb, states, do):
    T = proj.shape[0]
    cb = _tile(T, (512, 256, 128, 64))
    nchunk = cb // CHUNK
    nb = T // cb

    def body(hq_ref, hf_ref, hi_ref, lb_ref, st_ref, do_ref, dhq_ref, dhf_ref, dhi_ref, dlb_ref, dstate):
        @pl.when(pl.program_id(0) == 0)
        def _():
            dstate[...] = jnp.zeros_like(dstate)
            dlb_ref[...] = jnp.zeros_like(dlb_ref)

        lb = _lower_bound(lb_ref)
        up = _tri(CHUNK, True)
        last = lax.broadcasted_iota(jnp.int32, (CHUNK, HD), 0) == CHUNK - 1

        def chunk(cc, carry):
            c = nchunk - 1 - cc
            r0 = pl.multiple_of(c * CHUNK, CHUNK)
            for h in range(NH):
                cols = slice(h * HD, (h + 1) * HD)
                hq = hq_ref[pl.ds(r0, CHUNK), cols]
                hf = hf_ref[pl.ds(r0, CHUNK), cols]
                (q, k, gl), prep_vjp = jax.vjp(_hgrn_prep, hq, hf, lb[:, cols])
                v = hi_ref[pl.ds(r0, CHUNK), cols]
                d_o = do_ref[pl.ds(r0, CHUNK), cols]
                low, eb, ea, ebn, ek, ebl = _hgrn_factors(q, k, gl)
                s_t = st_ref[c, h]
                ds_new = dstate[h]
                qe, am, bm, kb = q * eb, q * ea, k * ebn, k * ek
                pm_t = jnp.where(up, _dot(bm, am, NT), 0.0)
                dp = jnp.where(low, _dot(d_o, v, NT), 0.0)
                dp_t = jnp.where(up, _dot(v, d_o, NT), 0.0)
                dqe = _dot(d_o, s_t, NN)
                da = _dot(dp, bm, NN)
                db_m = _dot(dp_t, am, NN)
                dkb = _dot(v, ds_new, NN)
                dv = _dot(pm_t, d_o, NN) + _dot(kb, ds_new, NT)
                dq = dqe * eb + da * ea
                dk = db_m * ebn + dkb * ek
                dbl = jnp.sum(dkb * kb, axis=0, keepdims=True) + jnp.sum(ds_new * s_t, axis=0, keepdims=True) * ebl
                db = (dqe * qe + da * am.astype(BF16).astype(F32) - db_m * bm.astype(BF16).astype(F32) - dkb * kb
                      + jnp.where(last, dbl, 0.0))
                dgl = lax.dot_general(up.astype(F32), db, NN, precision=HIGHEST, preferred_element_type=F32)
                dhq, dhf, dlb = prep_vjp((dq, dk, dgl))
                dhq_ref[pl.ds(r0, CHUNK), cols] = dhq.astype(dhq_ref.dtype)
                dhf_ref[pl.ds(r0, CHUNK), cols] = dhf.astype(dhf_ref.dtype)
                dhi_ref[pl.ds(r0, CHUNK), cols] = dv.astype(dhi_ref.dtype)
                dlb_ref[:, cols] += dlb
                dstate[h] = _dot(d_o, qe, TN) + ds_new * ebl
            return carry

        lax.fori_loop(0, nchunk, chunk, 0)

    def col(off):
        return pl.BlockSpec((cb, WH), lambda i, o=off: (nb - 1 - i, o))

    row = pl.BlockSpec((cb, WH), lambda i: (nb - 1 - i, 0))
    return pl.pallas_call(
        body, name="hgrn_bwd", grid=(nb,),
        in_specs=[col(0), col(1), col(2), pl.BlockSpec((2, WH), lambda i: (0, 0)),
                  pl.BlockSpec((nchunk, NH, HD, HD), lambda i: (nb - 1 - i, 0, 0, 0)), row],
        out_specs=[row, row, row, pl.BlockSpec((1, WH), lambda i: (0, 0))],
        out_shape=[jax.ShapeDtypeStruct((T, WH), BF16)] * 3 + [jax.ShapeDtypeStruct((1, WH), F32)],
        scratch_shapes=[pltpu.VMEM((NH, HD, HD), F32)],
        compiler_params=_params(("arbitrary",)),
    )(proj, proj, proj, hgrn_lb, states, do)


def _log_sigmoid(z):
    return jnp.minimum(z, 0.0) - jnp.log(1.0 + jnp.exp(-jnp.abs(z)))


def _fox_cum(proj, fb_pad):
    T = proj.shape[0]
    tb = _tile(T, (256, 128))

    def body(ff_ref, fb_ref, ct_ref, cq_ref, carry):
        @pl.when(pl.program_id(0) == 0)
        def _():
            carry[...] = jnp.zeros_like(carry)

        lf = _log_sigmoid(ff_ref[...] + fb_ref[...])
        cs = lax.dot_general(_tri(tb, False).astype(F32), lf, NN, precision=HIGHEST,
                             preferred_element_type=F32) + carry[0:1, :]
        carry[0:1, :] = cs[tb - 1:tb, :]
        ct_ref[...] = cs.T[0:8, :]
        for h in range(NH):
            cq_ref[h] = jnp.broadcast_to(cs[:, h:h + 1], (tb, HD))

    return pl.pallas_call(
        body, name="fox_cum", grid=(T // tb,),
        in_specs=[pl.BlockSpec((tb, HD), lambda i: (i, CB_FF)), pl.BlockSpec((1, HD), lambda i: (0, 0))],
        out_specs=[pl.BlockSpec((8, tb), lambda i: (0, i)), pl.BlockSpec((NH, tb, HD), lambda i: (0, i, 0))],
        out_shape=[jax.ShapeDtypeStruct((8, T), F32), jax.ShapeDtypeStruct((NH, T, HD), F32)],
        scratch_shapes=[pltpu.VMEM((8, HD), F32)],
        compiler_params=_params(("arbitrary",)),
    )(proj, fb_pad)


def _fox_cum_bwd(dc, proj, fb_pad):
    T = proj.shape[0]
    tb = _tile(T, (256, 128))
    nb = T // tb

    def body(dc_ref, ff_ref, fb_ref, dff_ref, dfb_ref, carry):
        @pl.when(pl.program_id(0) == 0)
        def _():
            carry[...] = jnp.zeros_like(carry)
            dfb_ref[...] = jnp.zeros_like(dfb_ref)

        rid = lax.broadcasted_iota(jnp.int32, (8, tb), 0)
        m8 = jnp.zeros((8, tb), F32)
        for h in range(NH):
            m8 = m8 + jnp.where(rid == h, dc_ref[h], 0.0)
        dcb = jnp.concatenate([m8, jnp.zeros((HD - 8, tb), F32)], axis=0).T
        rev = lax.dot_general(_tri(tb, True).astype(F32), dcb, NN, precision=HIGHEST,
                              preferred_element_type=F32) + carry[0:1, :]
        carry[0:1, :] = rev[0:1, :]
        dff = rev * jax.nn.sigmoid(-(ff_ref[...] + fb_ref[...]))
        dff_ref[...] = dff.astype(dff_ref.dtype)
        dfb_ref[...] += jnp.sum(dff, axis=0, keepdims=True)

    return pl.pallas_call(
        body, name="fox_cum_bwd", grid=(nb,),
        in_specs=[pl.BlockSpec((NH, 8, tb), lambda i: (0, 0, nb - 1 - i)),
                  pl.BlockSpec((tb, HD), lambda i: (nb - 1 - i, CB_FF)), pl.BlockSpec((1, HD), lambda i: (0, 0))],
        out_specs=[pl.BlockSpec((tb, HD), lambda i: (nb - 1 - i, 0)), pl.BlockSpec((1, HD), lambda i: (0, 0))],
        out_shape=[jax.ShapeDtypeStruct((T, HD), BF16), jax.ShapeDtypeStruct((1, HD), F32)],
        scratch_shapes=[pltpu.VMEM((8, HD), F32)],
        compiler_params=_params(("arbitrary",)),
    )(dc, proj, fb_pad)


STRIP = 128


def _fox_scores(q, k, cq, ck, i, j, bq, bk, r0=0):
    rows = q.shape[0]
    s = _dot(q, k, NT) * SCALE + (cq - ck)
    diff = lax.broadcasted_iota(jnp.int32, (rows, bk), 1) - lax.broadcasted_iota(jnp.int32, (rows, bk), 0)
    return jnp.where(diff <= i * bq + r0 - j * bk, s, NEG)


def _heads(h):
    return slice(h * HD, (h + 1) * HD)


UNDERFLOW = -120.0


def _fox_windows(proj, cq):
    T = proj.shape[0]
    bq = _tile(T, (512, 256, 128))
    nq = T // bq
    assert nq <= HD

    def body(q_ref, k_ref, cq_ref, jlo_ref, ihi_ref, norm_s, cs_s, ce_s):
        i = pl.program_id(0)

        @pl.when(i == 0)
        def _():
            norm_s[...] = jnp.zeros_like(norm_s)
            cs_s[...] = jnp.zeros_like(cs_s)
            ce_s[...] = jnp.zeros_like(ce_s)

        lane = lax.broadcasted_iota(jnp.int32, (1, HD), 1)
        for h in range(NH):
            for row, ref in ((h, q_ref), (8 + h, k_ref)):
                x = ref[:, _heads(h)]
                biggest = jnp.max(jnp.sum(x * x, axis=1, keepdims=True), axis=0, keepdims=True)
                norm_s[row:row + 1, :] = jnp.maximum(norm_s[row:row + 1, :], jnp.broadcast_to(biggest, (1, HD)))
            cs_s[h, pl.ds(i, 1), :] = cq_ref[h, 0:1, :]
            ce_s[h:h + 1, :] = jnp.where(lane == i, cq_ref[h, bq - 1:bq, :], ce_s[h:h + 1, :])

        @pl.when(i == nq - 1)
        def _():
            rows = lax.broadcasted_iota(jnp.int32, (HD, HD), 0)
            cols = lax.broadcasted_iota(jnp.int32, (HD, HD), 1)
            need = cols == rows
            for h in range(NH):
                slack = 2.05 * SCALE * jnp.sqrt(norm_s[h:h + 1, :] * norm_s[8 + h:9 + h, :])
                bound = cs_s[h] - ce_s[h:h + 1, :] + slack
                need = need | ((bound >= UNDERFLOW) & (cols < rows))
            need = need & (rows < nq) & (cols < nq)
            jlo = jnp.min(jnp.where(need, cols, HD).astype(F32), axis=1, keepdims=True)
            ihi = jnp.max(jnp.where(need, rows, -1).astype(F32), axis=0, keepdims=True)
            jlo_ref[...] = jnp.broadcast_to(jlo, (HD, HD)).astype(jnp.int32)
            ihi_ref[...] = jnp.broadcast_to(ihi, (8, HD)).astype(jnp.int32)

    jlo, ihi = pl.pallas_call(
        body, name="fox_windows", grid=(nq,),
        in_specs=[pl.BlockSpec((bq, WH), lambda i: (i, CB_FQ // NH)), pl.BlockSpec((bq, WH), lambda i: (i, CB_FK // NH)),
                  pl.BlockSpec((NH, bq, HD), lambda i: (0, i, 0))],
        out_specs=[pl.BlockSpec((HD, HD), lambda i: (0, 0)), pl.BlockSpec((8, HD), lambda i: (0, 0))],
        out_shape=[jax.ShapeDtypeStruct((HD, HD), jnp.int32), jax.ShapeDtypeStruct((8, HD), jnp.int32)],
        scratch_shapes=[pltpu.VMEM((16, HD), F32), pltpu.VMEM((NH, HD, HD), F32), pltpu.VMEM((8, HD), F32)],
        compiler_params=_params(("arbitrary",)),
    )(proj, proj, cq)
    return jnp.concatenate([jlo[:nq, 0], ihi[0, :nq]])


def _fox_fwd(win, proj, ct, cq):
    T = proj.shape[0]
    bq = bk = _tile(T, (512, 256, 128))
    nq = nk = T // bq

    def body(win_ref, q_ref, k_ref, v_ref, ct_ref, cq_ref, o_ref, lse_ref, m_s, l_s, acc_s):
        i, jj = pl.program_id(0), pl.program_id(1)
        j = win_ref[i] + jj

        @pl.when(jj == 0)
        def _():
            m_s[...] = jnp.full_like(m_s, NEG)
            l_s[...] = jnp.zeros_like(l_s)
            acc_s[...] = jnp.zeros_like(acc_s)

        @pl.when(j <= i)
        def _():
            for h in range(NH):
                hs = _heads(h)
                k, v, ck = k_ref[:, hs], v_ref[:, hs], ct_ref[h:h + 1, :]
                for r0 in range(0, bq, STRIP):
                    rs = slice(r0, r0 + STRIP)
                    s = _fox_scores(q_ref[rs, hs], k, cq_ref[h, rs, 0:1], ck, i, j, bq, bk, r0)
                    m_prev = m_s[h, rs]
                    m_new = jnp.maximum(m_prev, jnp.max(s, axis=1, keepdims=True))
                    alpha = jnp.exp(m_prev - m_new)
                    p = jnp.exp(s - m_new)
                    l_s[h, rs] = alpha * l_s[h, rs] + jnp.sum(p, axis=1, keepdims=True)
                    acc_s[rs, hs] = alpha * acc_s[rs, hs] + _dot(p, v, NN)
                    m_s[h, rs] = m_new

        @pl.when(jj == nk - 1)
        def _():
            for h in range(NH):
                o_ref[:, _heads(h)] = acc_s[:, _heads(h)] / l_s[h]
                lse_ref[h] = jnp.broadcast_to(m_s[h] + jnp.log(l_s[h]), (bq, HD))

    def key_block(i, jj, win):
        return jnp.minimum(win[i] + jj, i)

    def kv(off):
        return pl.BlockSpec((bk, WH), lambda i, jj, win, o=off // NH: (key_block(i, jj, win), o))

    stat = pl.BlockSpec((NH, bq, HD), lambda i, jj, win: (0, i, 0))
    return pl.pallas_call(
        body, name="fox_fwd",
        grid_spec=pltpu.PrefetchScalarGridSpec(
            num_scalar_prefetch=1, grid=(nq, nk),
            in_specs=[pl.BlockSpec((bq, WH), lambda i, jj, win: (i, CB_FQ // NH)), kv(CB_FK), kv(CB_FV),
                      pl.BlockSpec((8, bk), lambda i, jj, win: (0, key_block(i, jj, win))), stat],
            out_specs=[pl.BlockSpec((bq, WH), lambda i, jj, win: (i, 0)), stat],
            scratch_shapes=[pltpu.VMEM((NH, bq, 1), F32), pltpu.VMEM((NH, bq, 1), F32), pltpu.VMEM((bq, WH), F32)]),
        out_shape=[jax.ShapeDtypeStruct((T, WH), F32), jax.ShapeDtypeStruct((NH, T, HD), F32)],
        compiler_params=_params(("parallel", "arbitrary")),
    )(win, proj, proj, proj, ct, cq)


def _fox_bwd_dq(win, proj, ct, cq, lse, do):
    T = proj.shape[0]
    bq = bk = _tile(T, (512, 256, 128))
    nq = nk = T // bq

    def body(win_ref, q_ref, k_ref, v_ref, ct_ref, cq_ref, lse_ref, do_ref, dq_ref, delta_ref, acc_s, delta_s, psum_s):
        i, jj = pl.program_id(0), pl.program_id(1)
        j = win_ref[i] + jj % nk

        @pl.when(jj == 0)
        def _():
            acc_s[...] = jnp.zeros_like(acc_s)
            delta_s[...] = jnp.zeros_like(delta_s)
            psum_s[...] = jnp.zeros_like(psum_s)

        def probs(h):
            hs = _heads(h)
            k = k_ref[:, hs]
            s = _fox_scores(q_ref[:, hs], k, cq_ref[h, :, 0:1], ct_ref[h:h + 1, :], i, j, bq, bk)
            return k, jnp.exp(s - lse_ref[h, :, 0:1]), _dot(do_ref[:, hs], v_ref[:, hs], NT)

        @pl.when((j <= i) & (jj < nk))
        def _():
            for h in range(NH):
                _, p, dp = probs(h)
                delta_s[h] += jnp.sum(p * dp, axis=1, keepdims=True)
                psum_s[h] += jnp.sum(p, axis=1, keepdims=True)

        @pl.when((j <= i) & (jj >= nk))
        def _():
            for h in range(NH):
                k, p, dp = probs(h)
                ds = p * (dp - delta_s[h] / psum_s[h])
                acc_s[:, _heads(h)] += _dot(ds, k, NN) * SCALE

        @pl.when(jj == 2 * nk - 1)
        def _():
            dq_ref[...] = acc_s[...].astype(dq_ref.dtype)
            for h in range(NH):
                delta_ref[h] = jnp.broadcast_to(delta_s[h] / psum_s[h], (bq, HD))

    def key_block(i, jj, win):
        return jnp.minimum(win[i] + jj % nk, i)

    def kv(off):
        return pl.BlockSpec((bk, WH), lambda i, jj, win, o=off // NH: (key_block(i, jj, win), o))

    qrow = pl.BlockSpec((bq, WH), lambda i, jj, win: (i, 0))
    stat = pl.BlockSpec((NH, bq, HD), lambda i, jj, win: (0, i, 0))
    return pl.pallas_call(
        body, name="fox_bwd_dq",
        grid_spec=pltpu.PrefetchScalarGridSpec(
            num_scalar_prefetch=1, grid=(nq, 2 * nk),
            in_specs=[pl.BlockSpec((bq, WH), lambda i, jj, win: (i, CB_FQ // NH)), kv(CB_FK), kv(CB_FV),
                      pl.BlockSpec((8, bk), lambda i, jj, win: (0, key_block(i, jj, win))), stat, stat, qrow],
            out_specs=[qrow, stat],
            scratch_shapes=[pltpu.VMEM((bq, WH), F32), pltpu.VMEM((NH, bq, 1), F32), pltpu.VMEM((NH, bq, 1), F32)]),
        out_shape=[jax.ShapeDtypeStruct((T, WH), BF16), jax.ShapeDtypeStruct((NH, T, HD), F32)],
        compiler_params=_params(("parallel", "arbitrary")),
    )(win, proj, proj, proj, ct, cq, lse, do)


def _fox_bwd_dkv(win, proj, ct, cq, lse, delta, do):
    T = proj.shape[0]
    bq = bk = _tile(T, (512, 256, 128))
    nq = nk = T // bq

    def body(win_ref, q_ref, k_ref, v_ref, ct_ref, cq_ref, lse_ref, delta_ref, do_ref, dk_ref, dv_ref, dc_ref,
             dk_s, dv_s, dc_s):
        j, ii = pl.program_id(0), pl.program_id(1)
        i = j + ii

        @pl.when(ii == 0)
        def _():
            dk_s[...] = jnp.zeros_like(dk_s)
            dv_s[...] = jnp.zeros_like(dv_s)
            dc_s[...] = jnp.zeros_like(dc_s)

        @pl.when(i <= win_ref[nq + j])
        def _():
            for h in range(NH):
                hs = _heads(h)
                q = q_ref[:, hs]
                d_o = do_ref[:, hs]
                s = _fox_scores(q, k_ref[:, hs], cq_ref[h, :, 0:1], ct_ref[h:h + 1, :], i, j, bq, bk)
                p = jnp.exp(s - lse_ref[h, :, 0:1])
                dv_s[:, hs] += _dot(p, d_o, TN)
                dp = _dot(d_o, v_ref[:, hs], NT)
                ds = p * (dp - delta_ref[h, :, 0:1])
                dk_s[:, hs] += _dot(ds, q, TN) * SCALE
                dc_s[h:h + 1, :] -= jnp.sum(ds, axis=0, keepdims=True)

        @pl.when(ii == nq - 1)
        def _():
            dk_ref[...] = dk_s[...].astype(dk_ref.dtype)
            dv_ref[...] = dv_s[...].astype(dv_ref.dtype)
            for h in range(NH):
                dc_ref[h] = jnp.broadcast_to(dc_s[h:h + 1, :], (8, bk))

    def query_block(j, ii, win):
        return jnp.minimum(j + ii, win[nq + j])

    def kv(off):
        return pl.BlockSpec((bk, WH), lambda j, ii, win, o=off // NH: (j, o))

    qrow = pl.BlockSpec((bq, WH), lambda j, ii, win: (query_block(j, ii, win), 0))
    stat = pl.BlockSpec((NH, bq, HD), lambda j, ii, win: (0, query_block(j, ii, win), 0))
    krow = pl.BlockSpec((bk, WH), lambda j, ii, win: (j, 0))
    return pl.pallas_call(
        body, name="fox_bwd_dkv",
        grid_spec=pltpu.PrefetchScalarGridSpec(
            num_scalar_prefetch=1, grid=(nk, nq),
            in_specs=[pl.BlockSpec((bq, WH), lambda j, ii, win: (query_block(j, ii, win), CB_FQ // NH)), kv(CB_FK),
                      kv(CB_FV), pl.BlockSpec((8, bk), lambda j, ii, win: (0, j)), stat, stat, stat, qrow],
            out_specs=[krow, krow, pl.BlockSpec((NH, 8, bk), lambda j, ii, win: (0, 0, j))],
            scratch_shapes=[pltpu.VMEM((bk, WH), F32), pltpu.VMEM((bk, WH), F32), pltpu.VMEM((8, bk), F32)]),
        out_shape=[jax.ShapeDtypeStruct((T, WH), BF16), jax.ShapeDtypeStruct((T, WH), BF16),
                   jax.ShapeDtypeStruct((NH, 8, T), F32)],
        compiler_params=_params(("parallel", "arbitrary")),
    )(win, proj, proj, proj, ct, cq, lse, delta, do)


def _mem_probs(q, mk):
    s = _dot(q, mk, NT) * SCALE
    e = jnp.exp(s - jnp.max(s, axis=1, keepdims=True))
    return e / jnp.sum(e, axis=1, keepdims=True)


def _mem_fwd(proj, mem_kv):
    T = proj.shape[0]
    tr = _tile(T, (512, 256, 128))
    M = mem_kv.shape[0]

    def body(q_ref, mk_ref, mv_ref, o_ref):
        o_ref[...] = _dot(_mem_probs(q_ref[...], mk_ref[...]), mv_ref[...], NN)

    return pl.pallas_call(
        body, name="mem_fwd", grid=(NM, T // tr),
        in_specs=[pl.BlockSpec((tr, HD), lambda h, i: (i, CB_MQ + h)),
                  pl.BlockSpec((M, HD), lambda h, i: (0, h)), pl.BlockSpec((M, HD), lambda h, i: (0, NM + h))],
        out_specs=pl.BlockSpec((tr, HD), lambda h, i: (i, h)),
        out_shape=jax.ShapeDtypeStruct((T, WM), F32),
        compiler_params=_params(("parallel", "parallel")),
    )(proj, mem_kv, mem_kv)


def _mem_bwd(proj, mem_kv, do):
    T = proj.shape[0]
    tr = _tile(T, (512, 256, 128))
    M = mem_kv.shape[0]

    def body(q_ref, mk_ref, mv_ref, do_ref, dq_ref, dmk_ref, dmv_ref):
        @pl.when(pl.program_id(1) == 0)
        def _():
            dmk_ref[...] = jnp.zeros_like(dmk_ref)
            dmv_ref[...] = jnp.zeros_like(dmv_ref)

        q, mk, d_o = q_ref[...], mk_ref[...], do_ref[...]
        p = _mem_probs(q, mk)
        dmv_ref[...] += _dot(p, d_o, TN)
        dp = _dot(d_o, mv_ref[...], NT)
        ds = p * (dp - jnp.sum(p * dp, axis=1, keepdims=True))
        dq_ref[...] = (_dot(ds, mk, NN) * SCALE).astype(dq_ref.dtype)
        dmk_ref[...] += _dot(ds, q, TN) * SCALE

    acc = pl.BlockSpec((M, HD), lambda h, i: (0, h))
    row = pl.BlockSpec((tr, HD), lambda h, i: (i, h))
    return pl.pallas_call(
        body, name="mem_bwd", grid=(NM, T // tr),
        in_specs=[pl.BlockSpec((tr, HD), lambda h, i: (i, CB_MQ + h)),
                  pl.BlockSpec((M, HD), lambda h, i: (0, h)), pl.BlockSpec((M, HD), lambda h, i: (0, NM + h)), row],
        out_specs=[row, acc, acc],
        out_shape=[jax.ShapeDtypeStruct((T, WM), BF16), jax.ShapeDtypeStruct((M, WM), F32),
                   jax.ShapeDtypeStruct((M, WM), F32)],
        compiler_params=_params(("parallel", "arbitrary")),
    )(proj, mem_kv, mem_kv, do)


def _mesh_place():
    x, y, c = lax.axis_index("x"), lax.axis_index("y"), lax.axis_index("c")
    return x, y, c


CHIP_FLIPS = (4, 2, 6)
CHIP_OF_SLOT = (0,) + CHIP_FLIPS


def _peer(x, y, c, k):
    px = 1 - x if k & 4 else x
    py = 1 - y if k & 2 else y
    pc = 1 - c if k & 1 else c
    return (px, py, pc), 4 * px + 2 * py + pc


def _all_gather(shards, pad_rows):
    n = len(shards)
    npad = sum(1 for p in pad_rows if p)
    zeros = jnp.zeros((max(pad_rows) or 16, shards[0].shape[1]), shards[0].dtype)

    def body(*refs):
        ins, z_ref, outs = refs[:n], refs[n], refs[n + 1:2 * n + 1]
        send_sems, recv_sems, loc_sems = refs[2 * n + 1:]
        x, y, c = _mesh_place()
        me = 4 * x + 2 * y + c
        copies, sends = [], []
        ip = 0
        for w in range(n):
            r = ins[w].shape[0]
            dst = outs[w].at[pl.ds(pl.multiple_of(me * r, 16), r), :]
            cp = pltpu.make_async_copy(ins[w], dst, loc_sems.at[w])
            cp.start()
            copies.append(cp)
            if pad_rows[w]:
                cp = pltpu.make_async_copy(z_ref.at[pl.ds(0, pad_rows[w]), :],
                                           outs[w].at[pl.ds(NDEV * r, pad_rows[w]), :], loc_sems.at[n + ip])
                cp.start()
                copies.append(cp)
                ip += 1
            for s, k in enumerate((1,) + CHIP_FLIPS):
                peer, _ = _peer(x, y, c, k)
                cp = pltpu.make_async_remote_copy(src_ref=ins[w], dst_ref=dst, send_sem=send_sems.at[w, s],
                                                  recv_sem=recv_sems.at[w, s], device_id=peer, device_id_type=MESH)
                cp.start()
                (copies if s == 0 else sends).append(cp)
        sibling, _ = _peer(x, y, c, 1)
        for w in range(n):
            r = ins[w].shape[0]
            for s, k in enumerate(CHIP_FLIPS):
                _, pidx = _peer(x, y, c, k)
                rows = outs[w].at[pl.ds(pl.multiple_of(pidx * r, 16), r), :]
                pltpu.make_async_remote_copy(src_ref=rows, dst_ref=rows, send_sem=send_sems.at[w, 1 + s],
                                             recv_sem=recv_sems.at[w, 1 + s], device_id=sibling,
                                             device_id_type=MESH).wait_recv()
                cp = pltpu.make_async_remote_copy(src_ref=rows, dst_ref=rows, send_sem=send_sems.at[w, 4 + s],
                                                  recv_sem=recv_sems.at[w, 4 + s], device_id=sibling, device_id_type=MESH)
                cp.start()
                copies.append(cp)
        for cp in copies:
            cp.wait()
        for cp in sends:
            cp.wait_send()

    any_spec = pl.BlockSpec(memory_space=pl.ANY)
    return pl.pallas_call(
        body, name="all_gather_weights",
        in_specs=[any_spec] * (n + 1), out_specs=[any_spec] * n,
        out_shape=[jax.ShapeDtypeStruct((NDEV * s.shape[0] + p, s.shape[1]), s.dtype) for s, p in zip(shards, pad_rows)],
        scratch_shapes=[pltpu.SemaphoreType.DMA((n, NDEV - 1)), pltpu.SemaphoreType.DMA((n, NDEV - 1)),
                        pltpu.SemaphoreType.DMA((n + npad,))],
        compiler_params=pltpu.CompilerParams(has_side_effects=True),
    )(*shards, zeros)


def _exchange_in_chip(grads, shard_rows):
    n = len(grads)
    ns = len(CHIP_OF_SLOT)

    def body(*refs):
        ins, theirs = refs[:n], refs[n:2 * n]
        send_sems, recv_sems = refs[2 * n:]
        x, y, c = _mesh_place()
        sibling, _ = _peer(x, y, c, 1)
        copies = []
        for w in range(n):
            r = shard_rows[w]
            for s, k in enumerate(CHIP_OF_SLOT):
                _, other = _peer(x, y, c, k | 1)
                cp = pltpu.make_async_remote_copy(
                    src_ref=ins[w].at[pl.ds(pl.multiple_of(other * r, 16), r), :], dst_ref=theirs[w].at[s],
                    send_sem=send_sems.at[w, s], recv_sem=recv_sems.at[w, s], device_id=sibling, device_id_type=MESH)
                cp.start()
                copies.append(cp)
        for cp in copies:
            cp.wait()

    any_spec = pl.BlockSpec(memory_space=pl.ANY)
    return pl.pallas_call(
        body, name="reduce_scatter_in_chip",
        in_specs=[any_spec] * n, out_specs=[any_spec] * n,
        out_shape=[jax.ShapeDtypeStruct((ns, r, g.shape[1]), g.dtype) for g, r in zip(grads, shard_rows)],
        scratch_shapes=[pltpu.SemaphoreType.DMA((n, ns)), pltpu.SemaphoreType.DMA((n, ns))],
        compiler_params=pltpu.CompilerParams(has_side_effects=True),
    )(*grads)


def _pair_sum(grad, theirs, name):
    ns, r, c = theirs.shape
    tr = _tile(r, (128, 64, 32, 16))
    per_block = r // tr

    def body(a_ref, b_ref, o_ref):
        o_ref[...] = (a_ref[...].astype(F32) + b_ref[...].astype(F32)).astype(o_ref.dtype)

    def owner_rows(s, i):
        x, y, c_ = _mesh_place()
        fx, fy = s % 2, s // 2
        px, py = x + fx - 2 * x * fx, y + fy - 2 * y * fy
        return ((4 * px + 2 * py + c_) * per_block + i, 0)

    slot = pl.BlockSpec((None, tr, c), lambda s, i: (s, i, 0))
    return pl.pallas_call(
        body, name=name, grid=(ns, per_block),
        in_specs=[pl.BlockSpec((tr, c), owner_rows), slot], out_specs=slot,
        out_shape=jax.ShapeDtypeStruct((ns, r, c), theirs.dtype),
        compiler_params=_params(("parallel", "parallel")),
    )(grad, theirs)


def _exchange_between_chips(pairs):
    n = len(pairs)
    ns = len(CHIP_OF_SLOT) - 1

    def body(*refs):
        ins, outs = refs[:n], refs[n:2 * n]
        send_sems, recv_sems = refs[2 * n:]
        x, y, c = _mesh_place()
        copies = []
        for w in range(n):
            for s, k in enumerate(CHIP_OF_SLOT[1:]):
                peer, _ = _peer(x, y, c, k)
                cp = pltpu.make_async_remote_copy(src_ref=ins[w].at[s + 1], dst_ref=outs[w].at[s], send_sem=send_sems.at[w, s],
                                                  recv_sem=recv_sems.at[w, s], device_id=peer, device_id_type=MESH)
                cp.start()
                copies.append(cp)
        for cp in copies:
            cp.wait()

    any_spec = pl.BlockSpec(memory_space=pl.ANY)
    return pl.pallas_call(
        body, name="reduce_scatter_between_chips",
        in_specs=[any_spec] * n, out_specs=[any_spec] * n,
        out_shape=[jax.ShapeDtypeStruct((ns,) + p.shape[1:], p.dtype) for p in pairs],
        scratch_shapes=[pltpu.SemaphoreType.DMA((n, ns)), pltpu.SemaphoreType.DMA((n, ns))],
        compiler_params=pltpu.CompilerParams(has_side_effects=True),
    )(*pairs)


def _sum_chips(pair, recv, name):
    _, r, c = recv.shape
    tr = _tile(r, (128, 64, 32, 16))

    def body(p_ref, x_ref, o_ref):
        acc = p_ref[...].astype(F32)
        for s in range(x_ref.shape[0]):
            acc = acc + x_ref[s].astype(F32)
        o_ref[...] = acc

    return pl.pallas_call(
        body, name=name, grid=(r // tr,),
        in_specs=[pl.BlockSpec((None, tr, c), lambda i: (0, i, 0)), pl.BlockSpec((recv.shape[0], tr, c), lambda i: (0, i, 0))],
        out_specs=pl.BlockSpec((tr, c), lambda i: (i, 0)),
        out_shape=jax.ShapeDtypeStruct((r, c), F32),
        compiler_params=_params(("parallel",)),
    )(pair, recv)


def _all_reduce_small(part):
    R, W = part.shape

    def body(x_ref, o_ref, buf, send_sems, recv_sems):
        x, y, c = _mesh_place()
        me = 4 * x + 2 * y + c
        buf[me] = x_ref[...]
        copies = []
        for k in range(1, NDEV):
            peer, _ = _peer(x, y, c, k)
            cp = pltpu.make_async_remote_copy(src_ref=x_ref, dst_ref=buf.at[me], send_sem=send_sems.at[k - 1],
                                              recv_sem=recv_sems.at[k - 1], device_id=peer, device_id_type=MESH)
            cp.start()
            copies.append(cp)
        for cp in copies:
            cp.wait()
        acc = buf[0]
        for d in range(1, NDEV):
            acc = acc + buf[d]
        o_ref[...] = acc

    vm = pl.BlockSpec(memory_space=pltpu.VMEM)
    return pl.pallas_call(
        body, name="all_reduce_small", in_specs=[vm], out_specs=vm,
        out_shape=jax.ShapeDtypeStruct((R, W), F32),
        scratch_shapes=[pltpu.VMEM((NDEV, R, W), F32), pltpu.SemaphoreType.DMA((NDEV - 1,)),
                        pltpu.SemaphoreType.DMA((NDEV - 1,))],
        compiler_params=pltpu.CompilerParams(has_side_effects=True),
    )(part)


def _adam_math(w, g, m, v):
    m2 = ADAM_B1 * m + (1.0 - ADAM_B1) * g
    v2 = ADAM_B2 * v + (1.0 - ADAM_B2) * (g * g)
    m_hat = m2 / (1.0 - ADAM_B1 ** ADAM_STEP)
    v_hat = v2 / (1.0 - ADAM_B2 ** ADAM_STEP)
    delta = -ADAM_LR * (m_hat / (jnp.sqrt(v_hat) + ADAM_EPS) + ADAM_WD * w)
    return delta, m2, v2


def _adamw(w, g, m, v, name):
    r, c = w.shape
    tr = r
    for cand in (1024, 512, 256, 128, 64, 32, 16, 8):
        if r % cand == 0 and cand * c <= 256 * 1024:
            tr = cand
            break

    def body(w_ref, g_ref, m_ref, v_ref, d_ref, m2_ref, v2_ref):
        d_ref[...], m2_ref[...], v2_ref[...] = _adam_math(w_ref[...], g_ref[...], m_ref[...], v_ref[...])

    spec = pl.BlockSpec((tr, c), lambda i: (i, 0))
    return pl.pallas_call(
        body, name=name, grid=(r // tr,), in_specs=[spec] * 4, out_specs=[spec] * 3,
        out_shape=[jax.ShapeDtypeStruct((r, c), F32)] * 3,
        compiler_params=_params(("parallel",)),
    )(w, g, m, v)


GAINS = ("ffn1_pre", "ffn1_post", "mix_pre", "mix_post", "mem_norm", "ffn2_pre", "ffn2_post")
GAIN_ROWS = D // HD
ROW_LB = len(GAINS) * GAIN_ROWS
ROWS_GRAD_IN = ROW_LB + 24
ROWS_PACKED = ROW_LB + 32


def _small_update(gsum, w_p, m_p, v_p):
    def body(g_ref, w_ref, m_ref, v_ref, go_ref, d_ref, m2_ref, v2_ref):
        a0 = w_ref[ROW_LB:ROW_LB + 8, :]
        a1 = w_ref[ROW_LB + 8:ROW_LB + 16, :]
        mx = jnp.maximum(a0, a1)
        e0, e1 = jnp.exp(a0 - mx), jnp.exp(a1 - mx)
        lb = e0 / (e0 + e1)
        da0 = g_ref[ROW_LB:ROW_LB + 8, :] * lb * (1.0 - lb)
        g = jnp.concatenate([g_ref[0:ROW_LB, :], da0, -da0, g_ref[ROW_LB + 8:ROWS_GRAD_IN, :]], axis=0)
        go_ref[...] = g
        d_ref[...], m2_ref[...], v2_ref[...] = _adam_math(w_ref[...], g, m_ref[...], v_ref[...])

    vm = pl.BlockSpec(memory_space=pltpu.VMEM)
    return pl.pallas_call(
        body, name="small_update", in_specs=[vm] * 4, out_specs=[vm] * 4,
        out_shape=[jax.ShapeDtypeStruct((ROWS_PACKED, HD), F32)] * 4,
    )(gsum, w_p, m_p, v_p)


def _rows8(a):
    a = a.reshape(-1)
    rows = -(-a.shape[0] // HD)
    rows8 = -(-rows // 8) * 8
    return jnp.pad(a, (0, rows8 * HD - a.shape[0])).reshape(rows8, HD)


def _pack_small(gains, lb0, lb1, gnorm, fb):
    return jnp.concatenate([_rows8(g) for g in gains] + [_rows8(lb0), _rows8(lb1), _rows8(gnorm), _rows8(fb)], axis=0)


def _unpack_small(p):
    out = {}
    for i, name in enumerate(GAINS):
        out[name] = p[i * GAIN_ROWS:(i + 1) * GAIN_ROWS].reshape(1, D)
    lb0 = p[ROW_LB:ROW_LB + NH].reshape(1, WH)
    lb1 = p[ROW_LB + 8:ROW_LB + 8 + NH].reshape(1, WH)
    out["hgrn_lb"] = jnp.concatenate([lb0, lb1], axis=0)
    out["hgrn_gnorm"] = p[ROW_LB + 16:ROW_LB + 16 + NH].reshape(1, WH)
    out["fox_fb"] = p[ROW_LB + 24:ROW_LB + 25, 0:NH]
    return out


def _ffn_forward(xin, pre, post, wg_t, wu_t, wd, tag):
    T = xin.shape[0]
    tr = _tile(T, (256, 128))
    (n,) = _rowwise(_norm_fn, [(xin, 0)], [(pre, None)], [BF16], f"{tag}_pre", tr, D, 1)
    g, u, a = _ffn_up(n, wg_t, wu_t, f"{tag}_up")
    h = _mm(a, wd, "nn", F32, f"{tag}_down")
    (xout,) = _rowwise(functools.partial(_resid_fn, 0.5), [(xin, 0), (h, 0)], [(post, None)], [F32], f"{tag}_post", tr, D, 1)
    return xout, (xin, n, g, u, a, h)


def _ffn_backward(dxout, saved, pre, post, wg_t, wu_t, wd, tag):
    xin, n, g, u, a, h = saved
    T = xin.shape[0]
    tr = _tile(T, (256, 128))
    dh, dpost = _rowwise_bwd(functools.partial(_resid_h_fn, 0.5), [(h, 0)], [(post, None)], [(dxout, 0)], [0], [BF16],
                             f"{tag}_post_bwd", tr, D, 1)
    dwd = _mm(a, dh, "tn", BF16, f"{tag}_dwd")
    dg, du = _ffn_act_bwd(dh, wd, g, u, f"{tag}_act_bwd")
    dwg = _mm(dg, n, "tn", BF16, f"{tag}_dwg")
    dwu = _mm(du, n, "tn", BF16, f"{tag}_dwu")
    dn = _mm(dg, wg_t, "nn", F32, f"{tag}_dn_g")
    dn = _mm(du, wu_t, "nn", F32, f"{tag}_dn_u", add=dn)
    dxin, dpre = _rowwise_bwd(_norm_res_fn, [(xin, 0)], [(pre, None)], [(dxout, 0), (dn, 0)], [0], [F32],
                              f"{tag}_pre_bwd", tr, D, 1)
    return dxin, (dwg, dwu, dwd), dpre, dpost


def _local_step(x, mem, tgt, small, wts):
    T = x.shape[0]
    tr = _tile(T, (256, 128))
    fb_pad = jnp.pad(small["fox_fb"], ((0, 0), (0, HD - NH)))

    x1, ffn1_saved = _ffn_forward(x, small["ffn1_pre"], small["ffn1_post"], wts["ffn1_wg"], wts["ffn1_wu"],
                                  wts["ffn1_wd"], "ffn1")
    (un,) = _rowwise(_norm_fn, [(x1, 0)], [(small["mix_pre"], None)], [BF16], "mix_pre", tr, D, 1)
    proj = _mm(un, wts["w_in"], "nn", F32, "proj")
    z = _mm(un, wts["w_gate"], "nt", F32, "gate_logits")
    (memn,) = _rowwise(_norm_fn, [(mem, 0)], [(small["mem_norm"], None)], [BF16], "mem_norm", mem.shape[0], D, 1)
    mem_kv = _mm(memn, wts["w_mem_kv"], "nn", F32, "mem_kv")

    o_raw, states = _hgrn_fwd(proj, small["hgrn_lb"])
    (o_h,) = _rowwise(_hpost_fn, [(o_raw, 0), (proj, CB_HOG)], [(small["hgrn_gnorm"], 0)], [BF16], "hgrn_post",
                      tr, HD, NH)
    ct, cq = _fox_cum(proj, fb_pad)
    win = _fox_windows(proj, cq)
    o_f, lse = _fox_fwd(win, proj, ct, cq)
    o_m = _mem_fwd(proj, mem_kv)

    yh = _mm(o_h, wts["w_hgrn_out"], "nt", F32, "hgrn_out")
    yf = _mm(o_f, wts["w_fox_out"], "nt", F32, "fox_out")
    ym = _mm(o_m, wts["w_mem_out"], "nt", F32, "mem_out")
    zc = D // 512
    merge_rows = [(z, 0), (z, zc), (z, 2 * zc), (yh, 0), (yf, 0), (ym, 0)]
    (merged,) = _rowwise(_merge_fn, merge_rows, [], [BF16], "merge", tr, 512, zc)
    m = _mm(merged, wts["w_o"], "nn", F32, "mix_out")
    (x2,) = _rowwise(functools.partial(_resid_fn, 1.0), [(x1, 0), (m, 0)], [(small["mix_post"], None)], [F32], "mix_post",
                     tr, D, 1)
    x3, ffn2_saved = _ffn_forward(x2, small["ffn2_pre"], small["ffn2_post"], wts["ffn2_wg"], wts["ffn2_wu"],
                                  wts["ffn2_wd"], "ffn2")
    dy, loss_part = _loss(x3, tgt, "loss")

    gw, gs = {}, {}
    dx2, (gw["ffn2_wg"], gw["ffn2_wu"], gw["ffn2_wd"]), gs["ffn2_pre"], gs["ffn2_post"] = _ffn_backward(
        dy, ffn2_saved, small["ffn2_pre"], small["ffn2_post"], wts["ffn2_wg"], wts["ffn2_wu"], wts["ffn2_wd"], "ffn2")

    dm, gs["mix_post"] = _rowwise_bwd(functools.partial(_resid_h_fn, 1.0), [(m, 0)], [(small["mix_post"], None)],
                                      [(dx2, 0)], [0], [BF16], "mix_post_bwd", tr, D, 1)
    dmerged = _mm(dm, wts["w_o"], "nt", F32, "d_merged")
    gw["w_o"] = _mm(merged, dm, "tn", BF16, "d_w_o")
    dz0, dz1, dz2, dyh, dyf, dym = _rowwise_bwd(_merge_fn, merge_rows, [], [(dmerged, 0)], [0, 1, 2, 3, 4, 5], [BF16] * 6,
                                                "merge_bwd", tr, 512, zc)
    dz = jnp.concatenate([dz0, dz1, dz2], axis=1)
    gw["w_gate"] = _mm(dz, un, "tn", BF16, "d_w_gate")
    dun = _mm(dz, wts["w_gate"], "nn", F32, "d_un_gate")

    do_h = _mm(dyh, wts["w_hgrn_out"], "nn", F32, "d_o_h")
    gw["w_hgrn_out"] = _mm(dyh, o_h, "tn", BF16, "d_w_hgrn_out")
    do_f = _mm(dyf, wts["w_fox_out"], "nn", F32, "d_o_f")
    gw["w_fox_out"] = _mm(dyf, o_f, "tn", BF16, "d_w_fox_out")
    do_m = _mm(dym, wts["w_mem_out"], "nn", F32, "d_o_m")
    gw["w_mem_out"] = _mm(dym, o_m, "tn", BF16, "d_w_mem_out")

    do_raw, dhog, gs["hgrn_gnorm"] = _rowwise_bwd(_hpost_fn, [(o_raw, 0), (proj, CB_HOG)], [(small["hgrn_gnorm"], 0)],
                                                  [(do_h, 0)], [0, 1], [F32, BF16], "hgrn_post_bwd", tr, HD, NH)
    dhq, dhf, dhi, gs["hgrn_lb"] = _hgrn_bwd(proj, small["hgrn_lb"], states, do_raw)
    dfq, delta = _fox_bwd_dq(win, proj, ct, cq, lse, do_f)
    dfk, dfv, dc = _fox_bwd_dkv(win, proj, ct, cq, lse, delta, do_f)
    dff, dfb = _fox_cum_bwd(dc, proj, fb_pad)
    gs["fox_fb"] = dfb
    dmq, dmk, dmv = _mem_bwd(proj, mem_kv, do_m)

    dproj = jnp.concatenate([dhq, dhf, dhi, dhog, dfq, dfk, dfv, dff, dmq, jnp.zeros((T, HD), BF16)], axis=1)
    gw["w_in"] = _mm(un, dproj, "tn", BF16, "d_w_in")
    dun = _mm(dproj, wts["w_in"], "nt", F32, "d_un_proj", add=dun)
    dx1, gs["mix_pre"] = _rowwise_bwd(_norm_res_fn, [(x1, 0)], [(small["mix_pre"], None)], [(dx2, 0), (dun, 0)], [0], [F32],
                                      "mix_pre_bwd", tr, D, 1)

    dmem_kv = jnp.concatenate([dmk, dmv], axis=1)
    gw["w_mem_kv"] = _mm(memn, dmem_kv, "tn", BF16, "d_w_mem_kv")
    dmemn = _mm(dmem_kv, wts["w_mem_kv"], "nt", F32, "d_memn")
    _, gs["mem_norm"] = _rowwise_bwd(_norm_fn, [(mem, 0)], [(small["mem_norm"], None)], [(dmemn, 0)], [0], [BF16],
                                     "mem_norm_bwd", mem.shape[0], D, 1)

    dx, (gw["ffn1_wg"], gw["ffn1_wu"], gw["ffn1_wd"]), gs["ffn1_pre"], gs["ffn1_post"] = _ffn_backward(
        dx1, ffn1_saved, small["ffn1_pre"], small["ffn1_post"], wts["ffn1_wg"], wts["ffn1_wu"], wts["ffn1_wd"], "ffn1")
    return loss_part, dx, gw, gs


BIG = ("ffn1_wg", "ffn1_wu", "ffn1_wd", "w_in", "w_mem_kv", "w_hgrn_out", "w_fox_out", "w_mem_out", "w_gate", "w_o",
       "ffn2_wg", "ffn2_wu", "ffn2_wd")
TRANSPOSED = ("ffn1_wg", "ffn1_wu", "ffn2_wg", "ffn2_wu", "w_hgrn_out", "w_fox_out", "w_mem_out", "w_gate")
FFN_PAD = {"ffn1_wg": FP - F, "ffn1_wu": FP - F, "ffn1_wd": FP - F, "ffn2_wg": FP - F, "ffn2_wu": FP - F,
           "ffn2_wd": FP - F}
SMALL = GAINS + ("hgrn_lb", "hgrn_gnorm", "fox_fb")
WEIGHTS = ("ffn1_pre", "ffn1_post", "ffn1_wg", "ffn1_wu", "ffn1_wd", "mix_pre", "mix_post", "mem_norm", "w_in", "hgrn_lb",
           "hgrn_gnorm", "fox_fb", "w_mem_kv", "w_hgrn_out", "w_fox_out", "w_mem_out", "w_gate", "w_o", "ffn2_pre",
           "ffn2_post", "ffn2_wg", "ffn2_wu", "ffn2_wd")


def _to_gather_layout(name, w):
    if name in TRANSPOSED:
        w = w.T
    if name == "w_in":
        r = w.shape[0]
        w = jnp.concatenate([w[:, :MQ_COL], jnp.zeros((r, FF_COL + HD - MQ_COL), w.dtype), w[:, MQ_COL:],
                             jnp.zeros((r, P - FF_COL - HD - WM), w.dtype)], axis=1)
    return w.astype(BF16)


def _from_gather_layout(name, g):
    if name == "w_in":
        g = jnp.concatenate([g[:, :MQ_COL], g[:, FF_COL + HD:FF_COL + HD + WM]], axis=1)
    if name in TRANSPOSED:
        g = g.T
    return g


def kernel(x, mem, ffn1_pre, ffn1_post, ffn1_wg, ffn1_wu, ffn1_wd, mix_pre, mix_post, mem_norm, w_in, hgrn_lb, hgrn_gnorm, fox_fb, w_mem_kv, w_hgrn_out, w_fox_out, w_mem_out, w_gate, w_o, ffn2_pre, ffn2_post, ffn2_wg, ffn2_wu, ffn2_wd, loss_target, m_ffn1_pre, m_ffn1_post, m_ffn1_wg, m_ffn1_wu, m_ffn1_wd, m_mix_pre, m_mix_post, m_mem_norm, m_w_in, m_hgrn_lb, m_hgrn_gnorm, m_fox_fb, m_w_mem_kv, m_w_hgrn_out, m_w_fox_out, m_w_mem_out, m_w_gate, m_w_o, m_ffn2_pre, m_ffn2_post, m_ffn2_wg, m_ffn2_wu, m_ffn2_wd, v_ffn1_pre, v_ffn1_post, v_ffn1_wg, v_ffn1_wu, v_ffn1_wd, v_mix_pre, v_mix_post, v_mem_norm, v_w_in, v_hgrn_lb, v_hgrn_gnorm, v_fox_fb, v_w_mem_kv, v_w_hgrn_out, v_w_fox_out, v_w_mem_out, v_w_gate, v_w_o, v_ffn2_pre, v_ffn2_post, v_ffn2_wg, v_ffn2_wu, v_ffn2_wd):
    a = dict(locals())
    small = {n: a[n] for n in SMALL}
    shard = {n: a[n][0] if a[n].ndim == 3 else a[n] for n in BIG}

    blocks = [_to_gather_layout(n, shard[n]) for n in BIG]
    gathered = _all_gather(blocks, [FFN_PAD.get(n, 0) for n in BIG])
    wts = dict(zip(BIG, gathered))

    loss_part, dx, gw, gs = _local_step(x[0], mem[0], loss_target[0], small, wts)
    loss = lax.psum(0.5 / D * jnp.sum(loss_part), ("x", "y", "c"))

    theirs = _exchange_in_chip([gw[n] for n in BIG], [b.shape[0] for b in blocks])
    pairs = [_pair_sum(gw[n], t_, f"pair_{n}") for n, t_ in zip(BIG, theirs)]
    recv = _exchange_between_chips(pairs)
    grads, deltas, new_m, new_v = {}, {}, {}, {}
    for n, pr, rb in zip(BIG, pairs, recv):
        g = _from_gather_layout(n, _sum_chips(pr, rb, f"sum_{n}"))
        d, m2, v2 = _adamw(shard[n], g, a["m_" + n].reshape(g.shape), a["v_" + n].reshape(g.shape), f"adamw_{n}")
        full = a[n].shape
        grads[n], deltas[n], new_m[n], new_v[n] = g.reshape(full), d.reshape(full), m2.reshape(full), v2.reshape(full)

    part = jnp.concatenate([_rows8(gs[n]) for n in GAINS] + [_rows8(gs["hgrn_lb"]), _rows8(gs["hgrn_gnorm"]),
                                                             _rows8(gs["fox_fb"][:, :NH])], axis=0)
    gsum = _all_reduce_small(part)

    def packed(prefix):
        lb = a[prefix + "hgrn_lb"]
        return _pack_small([a[prefix + n] for n in GAINS], lb[0], lb[1], a[prefix + "hgrn_gnorm"], a[prefix + "fox_fb"])

    g_p, d_p, m_p, v_p = _small_update(gsum, packed(""), packed("m_"), packed("v_"))
    for dst, p in ((grads, g_p), (deltas, d_p), (new_m, m_p), (new_v, v_p)):
        dst.update(_unpack_small(p))

    return (loss, dx[None], *[grads[n] for n in WEIGHTS], *[deltas[n] for n in WEIGHTS],
            *[new_m[n] for n in WEIGHTS], *[new_v[n] for n in WEIGHTS])
```

```python
import functools

import jax
import jax.numpy as jnp
from jax import lax
from jax.experimental import pallas as pl
from jax.experimental.pallas import tpu as pltpu

F32 = jnp.float32
BF16 = jnp.bfloat16
HIGHEST = lax.Precision.HIGHEST

NDEV = 8
D = 2048
F = 5504
FP = 5632
HD = 128
NH = 6
NM = 4
WH = NH * HD
WM = NM * HD
P = 6144
FF_COL = 5376
MQ_COL = 5382
CHUNK = 64
EPS = 1e-6
SCALE = HD ** -0.5
NEG = -1e30
VMEM_LIMIT = 48 * 1024 * 1024

CB_HQ, CB_HF, CB_HI, CB_HOG, CB_FQ, CB_FK, CB_FV, CB_FF, CB_MQ = 0, 6, 12, 18, 24, 30, 36, 42, 43

ADAM_LR, ADAM_B1, ADAM_B2, ADAM_EPS, ADAM_WD, ADAM_STEP = 0.001, 0.9, 0.999, 1e-08, 0.01, 10

NT = (((1,), (1,)), ((), ()))
NN = (((1,), (0,)), ((), ()))
TN = (((0,), (0,)), ((), ()))
MESH = pl.DeviceIdType.MESH


def _params(sem=None, **kw):
    return pltpu.CompilerParams(dimension_semantics=sem, vmem_limit_bytes=VMEM_LIMIT, **kw)


def _tile(n, prefs):
    for p in prefs:
        if p <= n and n % p == 0:
            return p
    return n


def _dot(a, b, dims):
    return lax.dot_general(a.astype(BF16), b.astype(BF16), dims, preferred_element_type=F32)


def _mm(a, b, mode, out_dtype, name, add=None):
    if mode == "nn":
        (M, K), (K2, N) = a.shape, b.shape
    elif mode == "nt":
        (M, K), (N, K2) = a.shape, b.shape
    else:
        (K, M), (K2, N) = a.shape, b.shape
    assert K == K2, (a.shape, b.shape, mode)
    if mode == "tn":
        tm = _tile(M, (512, 256, 128))
        tn = _tile(N, (1024, 768, 512, 256, 128))
        tk = _tile(K, (4096, 2048, 1024, 512, 256, 128))
    else:
        tm = _tile(M, (1024, 512, 256, 128)) if K <= 2048 else _tile(M, (512, 256, 128))
        tn = _tile(N, (512, 768, 256, 128))
        tk = K if K <= 6144 else _tile(K, (2048, 1024, 512, 256, 128))
    nk = K // tk
    dims = {"nn": NN, "nt": NT, "tn": TN}[mode]
    has_add = add is not None

    def body(*refs):
        a_ref, b_ref = refs[0], refs[1]
        c_ref = refs[2] if has_add else None
        o_ref = refs[3] if has_add else refs[2]
        acc_ref = refs[-1]
        k = pl.program_id(2)
        part = _dot(a_ref[...], b_ref[...], dims)

        def finish(r):
            if has_add:
                r = r + c_ref[...].astype(F32)
            o_ref[...] = r.astype(o_ref.dtype)

        if nk == 1:
            finish(part)
        else:
            @pl.when(k == 0)
            def _():
                acc_ref[...] = part

            @pl.when(k > 0)
            def _():
                acc_ref[...] += part

            @pl.when(k == nk - 1)
            def _():
                finish(acc_ref[...])

    if mode == "nn":
        a_spec = pl.BlockSpec((tm, tk), lambda i, j, k: (i, k))
        b_spec = pl.BlockSpec((tk, tn), lambda i, j, k: (k, j))
    elif mode == "nt":
        a_spec = pl.BlockSpec((tm, tk), lambda i, j, k: (i, k))
        b_spec = pl.BlockSpec((tn, tk), lambda i, j, k: (j, k))
    else:
        a_spec = pl.BlockSpec((tk, tm), lambda i, j, k: (k, i))
        b_spec = pl.BlockSpec((tk, tn), lambda i, j, k: (k, j))
    o_spec = pl.BlockSpec((tm, tn), lambda i, j, k: (i, j))
    in_specs = [a_spec, b_spec] + ([o_spec] if has_add else [])
    args = (a, b) + ((add,) if has_add else ())
    return pl.pallas_call(
        body, name=name, grid=(M // tm, N // tn, nk), in_specs=in_specs, out_specs=o_spec,
        out_shape=jax.ShapeDtypeStruct((M, N), out_dtype),
        scratch_shapes=[pltpu.VMEM((tm, tn) if nk > 1 else (8, 128), F32)],
        compiler_params=_params(("parallel", "parallel", "arbitrary")),
    )(*args)


class _Rider:
    def __init__(self, plan, inputs, steps):
        self.plan, self.inputs, self.steps = plan, list(inputs), steps
        self.n_out = len(plan.out_shape())
        self.n_sem = len(plan.sems())

    def run(self, step, total, in_refs, out_refs, sem_refs):
        n = self.plan.n
        if isinstance(self.plan, _Gather):
            args = (in_refs[:n], in_refs[n], out_refs) + tuple(sem_refs)
        else:
            args = (in_refs, out_refs) + tuple(sem_refs)
        for frac, method in self.steps:
            @pl.when(step == int(frac * (total - 1)))
            def _(method=method):
                getattr(self.plan, method)(*args)


GATHER_STEPS = ((0.0, "start"), (0.6, "forward"), (1.0, "finish"))
EXCHANGE_STEPS = ((0.0, "start"), (1.0, "finish"))


def _carry(rider, refs, n_in, n_out, n_scratch, step, total):
    if rider is None:
        return refs
    ri, ro, rs = len(rider.inputs), rider.n_out, rider.n_sem
    own_in, rid_in = refs[:n_in], refs[n_in:n_in + ri]
    own_out, rid_out = refs[n_in + ri:n_in + ri + n_out], refs[n_in + ri + n_out:n_in + ri + n_out + ro]
    own_scr, rid_sem = refs[n_in + ri + n_out + ro:n_in + ri + n_out + ro + n_scratch], refs[len(refs) - rs:]
    rider.run(step, total, rid_in, rid_out, rid_sem)
    return tuple(own_in) + tuple(own_out) + tuple(own_scr)


def _with_rider(rider, in_specs, out_specs, out_shape, scratch):
    if rider is None:
        return in_specs, out_specs, out_shape, scratch, ()
    any_spec = pl.BlockSpec(memory_space=pl.ANY)
    return (list(in_specs) + [any_spec] * len(rider.inputs), list(out_specs) + [any_spec] * rider.n_out,
            list(out_shape) + rider.plan.out_shape(), list(scratch) + rider.plan.sems(), tuple(rider.inputs))


def _ffn_up(n, wg_t, wu_t, name, rider=None):
    T = n.shape[0]
    tm = _tile(T, (1024, 512, 256, 128))
    tn = 512
    ni, nj = T // tm, FP // tn

    def body(*refs):
        step = pl.program_id(0) * nj + pl.program_id(1)
        n_ref, wg_ref, wu_ref, g_ref, u_ref, a_ref = _carry(rider, refs, 3, 3, 0, step, ni * nj)
        x = n_ref[...]
        g = _dot(x, wg_ref[...], NT)
        u = _dot(x, wu_ref[...], NT)
        g_ref[...] = g
        u_ref[...] = u
        a_ref[...] = (g * jax.nn.sigmoid(g) * u).astype(BF16)

    w_spec = pl.BlockSpec((tn, D), lambda i, j: (j, 0))
    o_spec = pl.BlockSpec((tm, tn), lambda i, j: (i, j))
    in_specs, out_specs, out_shape, scratch, extra = _with_rider(
        rider, [pl.BlockSpec((tm, D), lambda i, j: (i, 0)), w_spec, w_spec], [o_spec, o_spec, o_spec],
        [jax.ShapeDtypeStruct((T, FP), F32), jax.ShapeDtypeStruct((T, FP), F32), jax.ShapeDtypeStruct((T, FP), BF16)], [])
    return pl.pallas_call(
        body, name=name, grid=(ni, nj), in_specs=in_specs, out_specs=out_specs, out_shape=out_shape,
        scratch_shapes=scratch,
        compiler_params=_params(("arbitrary", "arbitrary") if rider else ("parallel", "parallel"),
                                has_side_effects=rider is not None),
    )(n, wg_t, wu_t, *extra)


def _ffn_act_bwd(dh, wd, g, u, name, rider=None):
    T = dh.shape[0]
    tm = _tile(T, (1024, 512, 256, 128))
    tn = 512
    ni, nj = T // tm, FP // tn

    def body(*refs):
        step = pl.program_id(0) * nj + pl.program_id(1)
        dh_ref, wd_ref, g_ref, u_ref, dg_ref, du_ref = _carry(rider, refs, 4, 2, 0, step, ni * nj)
        da = _dot(dh_ref[...], wd_ref[...], NT)
        g = g_ref[...]
        sg = jax.nn.sigmoid(g)
        dg_ref[...] = (da * u_ref[...] * (sg * (1.0 + g * (1.0 - sg)))).astype(dg_ref.dtype)
        du_ref[...] = (da * (g * sg)).astype(du_ref.dtype)

    tile = pl.BlockSpec((tm, tn), lambda i, j: (i, j))
    in_specs, out_specs, out_shape, scratch, extra = _with_rider(
        rider, [pl.BlockSpec((tm, D), lambda i, j: (i, 0)), pl.BlockSpec((tn, D), lambda i, j: (j, 0)), tile, tile],
        [tile, tile], [jax.ShapeDtypeStruct((T, FP), BF16), jax.ShapeDtypeStruct((T, FP), BF16)], [])
    return pl.pallas_call(
        body, name=name, grid=(ni, nj), in_specs=in_specs, out_specs=out_specs, out_shape=out_shape,
        scratch_shapes=scratch,
        compiler_params=_params(("arbitrary", "arbitrary") if rider else ("parallel", "parallel"),
                                has_side_effects=rider is not None),
    )(dh, wd, g, u, *extra)


def _row_specs(rows, tr, cw):
    return [pl.BlockSpec((tr, cw), lambda j, i, o=off: (i, o + j)) for _, off in rows]


def _const_specs(consts, cw):
    specs = []
    for arr, off in consts:
        if off is None:
            specs.append(pl.BlockSpec(arr.shape, lambda j, i: (0, 0)))
        else:
            specs.append(pl.BlockSpec((arr.shape[0], cw), lambda j, i, o=off: (0, o + j)))
    return specs


def _rowwise(fn, rows, consts, out_dtypes, name, tr, cw, ncol):
    T = rows[0][0].shape[0]
    nr, nc = len(rows), len(consts)

    def body(*refs):
        r = [x[...].astype(F32) for x in refs[:nr]]
        c = [x[...] for x in refs[nr:nr + nc]]
        res = fn(*r, *c)
        for o_ref, v in zip(refs[nr + nc:], res):
            o_ref[...] = v.astype(o_ref.dtype)

    o_spec = pl.BlockSpec((tr, cw), lambda j, i: (i, j))
    return pl.pallas_call(
        body, name=name, grid=(ncol, T // tr),
        in_specs=_row_specs(rows, tr, cw) + _const_specs(consts, cw),
        out_specs=[o_spec] * len(out_dtypes),
        out_shape=[jax.ShapeDtypeStruct((T, ncol * cw), dt) for dt in out_dtypes],
        compiler_params=_params(("parallel", "parallel")),
    )(*[a for a, _ in rows], *[a for a, _ in consts])


def _rowwise_bwd(fn, rows, consts, cots, diff, ddtypes, name, tr, cw, ncol):
    T = rows[0][0].shape[0]
    nr, nc, nt, nd = len(rows), len(consts), len(cots), len(diff)

    def body(*refs):
        r = [x[...].astype(F32) for x in refs[:nr]]
        c = [x[...] for x in refs[nr:nr + nc]]
        ct = [x[...].astype(F32) for x in refs[nr + nc:nr + nc + nt]]
        drow_refs = refs[nr + nc + nt:nr + nc + nt + nd]
        dconst_refs = refs[nr + nc + nt + nd:]
        i = pl.program_id(1)

        def f(*args):
            full = list(r)
            for idx, a in zip(diff, args[:nd]):
                full[idx] = a
            return tuple(fn(*full, *args[nd:]))

        _, vjp = jax.vjp(f, *[r[d] for d in diff], *c)
        g = vjp(tuple(ct))
        for o_ref, v in zip(drow_refs, g[:nd]):
            o_ref[...] = v.astype(o_ref.dtype)

        @pl.when(i == 0)
        def _():
            for o_ref in dconst_refs:
                o_ref[...] = jnp.zeros_like(o_ref)

        for o_ref, v in zip(dconst_refs, g[nd:]):
            o_ref[...] += v

    o_spec = pl.BlockSpec((tr, cw), lambda j, i: (i, j))
    out_shape = [jax.ShapeDtypeStruct((T, ncol * cw), dt) for dt in ddtypes]
    out_shape += [jax.ShapeDtypeStruct(a.shape, F32) for a, _ in consts]
    return pl.pallas_call(
        body, name=name, grid=(ncol, T // tr),
        in_specs=_row_specs(rows, tr, cw) + _const_specs(consts, cw) + _row_specs(cots, tr, cw),
        out_specs=[o_spec] * nd + _const_specs(consts, cw),
        out_shape=out_shape,
        compiler_params=_params(("parallel", "arbitrary")),
    )(*[a for a, _ in rows], *[a for a, _ in consts], *[a for a, _ in cots])


def _rms(x, g):
    return x * lax.rsqrt(jnp.mean(x * x, axis=-1, keepdims=True) + EPS) * g


def _silu(x):
    return x * jax.nn.sigmoid(x)


def _norm_fn(x, g):
    return (_rms(x, g),)


def _norm_res_fn(x, g):
    return (x, _rms(x, g))


def _resid_fn(scale, x, h, g):
    return (x + scale * _rms(h, g),)


def _resid_h_fn(scale, h, g):
    return (scale * _rms(h, g),)


def _hpost_fn(o, hog, gn):
    return (_rms(o, gn) * _silu(hog),)


def _merge_fn(z0, z1, z2, yh, yf, ym):
    return (jax.nn.sigmoid(z0) * yh + jax.nn.sigmoid(z1) * yf + jax.nn.sigmoid(z2) * ym,)


def _loss(x3, tgt, name):
    T = x3.shape[0]
    tr = _tile(T, (256, 128))

    def body(x_ref, t_ref, dy_ref, s_ref):
        i = pl.program_id(0)
        e = x_ref[...] - t_ref[...]
        dy_ref[...] = e * (1.0 / D)
        col = jnp.sum(e * e, axis=0, keepdims=True)
        tot = col[:, 0:HD]
        for k in range(1, D // HD):
            tot = tot + col[:, k * HD:(k + 1) * HD]

        @pl.when(i == 0)
        def _():
            s_ref[...] = jnp.zeros_like(s_ref)

        s_ref[...] += tot

    spec = pl.BlockSpec((tr, D), lambda i: (i, 0))
    return pl.pallas_call(
        body, name=name, grid=(T // tr,), in_specs=[spec, spec],
        out_specs=[spec, pl.BlockSpec((1, HD), lambda i: (0, 0))],
        out_shape=[jax.ShapeDtypeStruct((T, D), F32), jax.ShapeDtypeStruct((1, HD), F32)],
        compiler_params=_params(("arbitrary",)),
    )(x3, tgt)


def _lower_bound(lb_ref):
    a0 = lb_ref[0:1, :]
    a1 = lb_ref[1:2, :]
    mx = jnp.maximum(a0, a1)
    e0 = jnp.exp(a0 - mx)
    return e0 / (e0 + jnp.exp(a1 - mx))


def _hgrn_prep(hq, hf, lb):
    g = lb + (1.0 - lb) * jax.nn.sigmoid(hf)
    return _silu(hq), 1.0 - g, jnp.log(g)


def _tri(n, upper):
    r = lax.broadcasted_iota(jnp.int32, (n, n), 0)
    c = lax.broadcasted_iota(jnp.int32, (n, n), 1)
    return (c >= r) if upper else (c <= r)


def _hgrn_factors(q, k, gl):
    low = _tri(CHUNK, False)
    b = lax.dot_general(low.astype(F32), gl, NN, precision=HIGHEST, preferred_element_type=F32)
    bl = b[CHUNK - 1:CHUNK, :]
    ref = b[CHUNK // 2 - 1:CHUNK // 2, :]
    eb = jnp.exp(b)
    ea = jnp.exp(b - ref)
    ebn = jnp.exp(ref - b)
    ek = jnp.exp(bl - b)
    ebl = jnp.exp(bl)
    return low, eb, ea, ebn, ek, ebl


def _hgrn_fwd(proj, hgrn_lb, rider=None):
    T = proj.shape[0]
    cb = _tile(T, (512, 256, 128, 64))
    nchunk = cb // CHUNK

    def body(*refs):
        hq_ref, hf_ref, hi_ref, lb_ref, o_ref, st_ref, state = _carry(rider, refs, 4, 2, 1, pl.program_id(0), T // cb)

        @pl.when(pl.program_id(0) == 0)
        def _():
            state[...] = jnp.zeros_like(state)

        lb = _lower_bound(lb_ref)

        def chunk(c, carry):
            r0 = pl.multiple_of(c * CHUNK, CHUNK)
            for h in range(NH):
                cols = slice(h * HD, (h + 1) * HD)
                q, k, gl = _hgrn_prep(hq_ref[pl.ds(r0, CHUNK), cols], hf_ref[pl.ds(r0, CHUNK), cols], lb[:, cols])
                v = hi_ref[pl.ds(r0, CHUNK), cols]
                low, eb, ea, ebn, ek, ebl = _hgrn_factors(q, k, gl)
                s_t = state[h]
                st_ref[c, h] = s_t
                pm = jnp.where(low, _dot(q * ea, k * ebn, NT), 0.0)
                o_ref[pl.ds(r0, CHUNK), cols] = _dot(q * eb, s_t, NT) + _dot(pm, v, NN)
                state[h] = s_t * ebl + _dot(v, k * ek, TN)
            return carry

        lax.fori_loop(0, nchunk, chunk, 0)

    def col(off):
        return pl.BlockSpec((cb, WH), lambda i, o=off: (i, o))

    in_specs, out_specs, out_shape, scratch, extra = _with_rider(
        rider, [col(0), col(1), col(2), pl.BlockSpec((2, WH), lambda i: (0, 0))],
        [pl.BlockSpec((cb, WH), lambda i: (i, 0)), pl.BlockSpec((nchunk, NH, HD, HD), lambda i: (i, 0, 0, 0))],
        [jax.ShapeDtypeStruct((T, WH), F32), jax.ShapeDtypeStruct((T // CHUNK, NH, HD, HD), F32)],
        [pltpu.VMEM((NH, HD, HD), F32)])
    return pl.pallas_call(
        body, name="hgrn_fwd", grid=(T // cb,), in_specs=in_specs, out_specs=out_specs, out_shape=out_shape,
        scratch_shapes=scratch, compiler_params=_params(("arbitrary",), has_side_effects=rider is not None),
    )(proj, proj, proj, hgrn_lb, *extra)


def _hgrn_bwd(proj, hgrn_lb, states, do, rider=None):
    T = proj.shape[0]
    cb = _tile(T, (512, 256, 128, 64))
    nchunk = cb // CHUNK
    nb = T // cb

    def body(*refs):
        (hq_ref, hf_ref, hi_ref, lb_ref, st_ref, do_ref, dhq_ref, dhf_ref, dhi_ref, dlb_ref,
         dstate) = _carry(rider, refs, 6, 4, 1, pl.program_id(0), nb)

        @pl.when(pl.program_id(0) == 0)
        def _():
            dstate[...] = jnp.zeros_like(dstate)
            dlb_ref[...] = jnp.zeros_like(dlb_ref)

        lb = _lower_bound(lb_ref)
        up = _tri(CHUNK, True)
        last = lax.broadcasted_iota(jnp.int32, (CHUNK, HD), 0) == CHUNK - 1

        def chunk(cc, carry):
            c = nchunk - 1 - cc
            r0 = pl.multiple_of(c * CHUNK, CHUNK)
            for h in range(NH):
                cols = slice(h * HD, (h + 1) * HD)
                hq = hq_ref[pl.ds(r0, CHUNK), cols]
                hf = hf_ref[pl.ds(r0, CHUNK), cols]
                (q, k, gl), prep_vjp = jax.vjp(_hgrn_prep, hq, hf, lb[:, cols])
                v = hi_ref[pl.ds(r0, CHUNK), cols]
                d_o = do_ref[pl.ds(r0, CHUNK), cols]
                low, eb, ea, ebn, ek, ebl = _hgrn_factors(q, k, gl)
                s_t = st_ref[c, h]
                ds_new = dstate[h]
                qe, am, bm, kb = q * eb, q * ea, k * ebn, k * ek
                pm_t = jnp.where(up, _dot(bm, am, NT), 0.0)
                dp = jnp.where(low, _dot(d_o, v, NT), 0.0)
                dp_t = jnp.where(up, _dot(v, d_o, NT), 0.0)
                dqe = _dot(d_o, s_t, NN)
                da = _dot(dp, bm, NN)
                db_m = _dot(dp_t, am, NN)
                dkb = _dot(v, ds_new, NN)
                dv = _dot(pm_t, d_o, NN) + _dot(kb, ds_new, NT)
                dq = dqe * eb + da * ea
                dk = db_m * ebn + dkb * ek
                dbl = jnp.sum(dkb * kb, axis=0, keepdims=True) + jnp.sum(ds_new * s_t, axis=0, keepdims=True) * ebl
                db = (dqe * qe + da * am.astype(BF16).astype(F32) - db_m * bm.astype(BF16).astype(F32) - dkb * kb
                      + jnp.where(last, dbl, 0.0))
                dgl = lax.dot_general(up.astype(F32), db, NN, precision=HIGHEST, preferred_element_type=F32)
                dhq, dhf, dlb = prep_vjp((dq, dk, dgl))
                dhq_ref[pl.ds(r0, CHUNK), cols] = dhq.astype(dhq_ref.dtype)
                dhf_ref[pl.ds(r0, CHUNK), cols] = dhf.astype(dhf_ref.dtype)
                dhi_ref[pl.ds(r0, CHUNK), cols] = dv.astype(dhi_ref.dtype)
                dlb_ref[:, cols] += dlb
                dstate[h] = _dot(d_o, qe, TN) + ds_new * ebl
            return carry

        lax.fori_loop(0, nchunk, chunk, 0)

    def col(off):
        return pl.BlockSpec((cb, WH), lambda i, o=off: (nb - 1 - i, o))

    row = pl.BlockSpec((cb, WH), lambda i: (nb - 1 - i, 0))
    in_specs, out_specs, out_shape, scratch, extra = _with_rider(
        rider, [col(0), col(1), col(2), pl.BlockSpec((2, WH), lambda i: (0, 0)),
                pl.BlockSpec((nchunk, NH, HD, HD), lambda i: (nb - 1 - i, 0, 0, 0)), row],
        [row, row, row, pl.BlockSpec((1, WH), lambda i: (0, 0))],
        [jax.ShapeDtypeStruct((T, WH), BF16)] * 3 + [jax.ShapeDtypeStruct((1, WH), F32)], [pltpu.VMEM((NH, HD, HD), F32)])
    return pl.pallas_call(
        body, name="hgrn_bwd", grid=(nb,), in_specs=in_specs, out_specs=out_specs, out_shape=out_shape,
        scratch_shapes=scratch, compiler_params=_params(("arbitrary",), has_side_effects=rider is not None),
    )(proj, proj, proj, hgrn_lb, states, do, *extra)


def _log_sigmoid(z):
    return jnp.minimum(z, 0.0) - jnp.log(1.0 + jnp.exp(-jnp.abs(z)))


def _fox_cum(proj, fb_pad):
    T = proj.shape[0]
    tb = _tile(T, (256, 128))

    def body(ff_ref, fb_ref, ct_ref, cq_ref, carry):
        @pl.when(pl.program_id(0) == 0)
        def _():
            carry[...] = jnp.zeros_like(carry)

        lf = _log_sigmoid(ff_ref[...] + fb_ref[...])
        cs = lax.dot_general(_tri(tb, False).astype(F32), lf, NN, precision=HIGHEST,
                             preferred_element_type=F32) + carry[0:1, :]
        carry[0:1, :] = cs[tb - 1:tb, :]
        ct_ref[...] = cs.T[0:8, :]
        for h in range(NH):
            cq_ref[h] = jnp.broadcast_to(cs[:, h:h + 1], (tb, HD))

    return pl.pallas_call(
        body, name="fox_cum", grid=(T // tb,),
        in_specs=[pl.BlockSpec((tb, HD), lambda i: (i, CB_FF)), pl.BlockSpec((1, HD), lambda i: (0, 0))],
        out_specs=[pl.BlockSpec((8, tb), lambda i: (0, i)), pl.BlockSpec((NH, tb, HD), lambda i: (0, i, 0))],
        out_shape=[jax.ShapeDtypeStruct((8, T), F32), jax.ShapeDtypeStruct((NH, T, HD), F32)],
        scratch_shapes=[pltpu.VMEM((8, HD), F32)],
        compiler_params=_params(("arbitrary",)),
    )(proj, fb_pad)


def _fox_cum_bwd(dc, proj, fb_pad):
    T = proj.shape[0]
    tb = _tile(T, (256, 128))
    nb = T // tb

    def body(dc_ref, ff_ref, fb_ref, dff_ref, dfb_ref, carry):
        @pl.when(pl.program_id(0) == 0)
        def _():
            carry[...] = jnp.zeros_like(carry)
            dfb_ref[...] = jnp.zeros_like(dfb_ref)

        rid = lax.broadcasted_iota(jnp.int32, (8, tb), 0)
        m8 = jnp.zeros((8, tb), F32)
        for h in range(NH):
            m8 = m8 + jnp.where(rid == h, dc_ref[h], 0.0)
        dcb = jnp.concatenate([m8, jnp.zeros((HD - 8, tb), F32)], axis=0).T
        rev = lax.dot_general(_tri(tb, True).astype(F32), dcb, NN, precision=HIGHEST,
                              preferred_element_type=F32) + carry[0:1, :]
        carry[0:1, :] = rev[0:1, :]
        dff = rev * jax.nn.sigmoid(-(ff_ref[...] + fb_ref[...]))
        dff_ref[...] = dff.astype(dff_ref.dtype)
        dfb_ref[...] += jnp.sum(dff, axis=0, keepdims=True)

    return pl.pallas_call(
        body, name="fox_cum_bwd", grid=(nb,),
        in_specs=[pl.BlockSpec((NH, 8, tb), lambda i: (0, 0, nb - 1 - i)),
                  pl.BlockSpec((tb, HD), lambda i: (nb - 1 - i, CB_FF)), pl.BlockSpec((1, HD), lambda i: (0, 0))],
        out_specs=[pl.BlockSpec((tb, HD), lambda i: (nb - 1 - i, 0)), pl.BlockSpec((1, HD), lambda i: (0, 0))],
        out_shape=[jax.ShapeDtypeStruct((T, HD), BF16), jax.ShapeDtypeStruct((1, HD), F32)],
        scratch_shapes=[pltpu.VMEM((8, HD), F32)],
        compiler_params=_params(("arbitrary",)),
    )(dc, proj, fb_pad)


STRIP = 128


def _fox_scores(q, k, cq, ck, i, j, bq, bk, r0=0):
    rows = q.shape[0]
    s = _dot(q, k, NT) * SCALE + (cq - ck)
    diff = lax.broadcasted_iota(jnp.int32, (rows, bk), 1) - lax.broadcasted_iota(jnp.int32, (rows, bk), 0)
    return jnp.where(diff <= i * bq + r0 - j * bk, s, NEG)


def _heads(h):
    return slice(h * HD, (h + 1) * HD)


UNDERFLOW = -120.0


def _fox_windows(proj, cq):
    T = proj.shape[0]
    bq = _tile(T, (512, 256, 128))
    nq = T // bq
    assert nq <= HD

    def body(q_ref, k_ref, cq_ref, jlo_ref, ihi_ref, norm_s, cs_s, ce_s):
        i = pl.program_id(0)

        @pl.when(i == 0)
        def _():
            norm_s[...] = jnp.zeros_like(norm_s)
            cs_s[...] = jnp.zeros_like(cs_s)
            ce_s[...] = jnp.zeros_like(ce_s)

        lane = lax.broadcasted_iota(jnp.int32, (1, HD), 1)
        for h in range(NH):
            for row, ref in ((h, q_ref), (8 + h, k_ref)):
                x = ref[:, _heads(h)]
                biggest = jnp.max(jnp.sum(x * x, axis=1, keepdims=True), axis=0, keepdims=True)
                norm_s[row:row + 1, :] = jnp.maximum(norm_s[row:row + 1, :], jnp.broadcast_to(biggest, (1, HD)))
            cs_s[h, pl.ds(i, 1), :] = cq_ref[h, 0:1, :]
            ce_s[h:h + 1, :] = jnp.where(lane == i, cq_ref[h, bq - 1:bq, :], ce_s[h:h + 1, :])

        @pl.when(i == nq - 1)
        def _():
            rows = lax.broadcasted_iota(jnp.int32, (HD, HD), 0)
            cols = lax.broadcasted_iota(jnp.int32, (HD, HD), 1)
            need = cols == rows
            for h in range(NH):
                slack = 2.05 * SCALE * jnp.sqrt(norm_s[h:h + 1, :] * norm_s[8 + h:9 + h, :])
                bound = cs_s[h] - ce_s[h:h + 1, :] + slack
                need = need | ((bound >= UNDERFLOW) & (cols < rows))
            need = need & (rows < nq) & (cols < nq)
            jlo = jnp.min(jnp.where(need, cols, HD).astype(F32), axis=1, keepdims=True)
            ihi = jnp.max(jnp.where(need, rows, -1).astype(F32), axis=0, keepdims=True)
            jlo_ref[...] = jnp.broadcast_to(jlo, (HD, HD)).astype(jnp.int32)
            ihi_ref[...] = jnp.broadcast_to(ihi, (8, HD)).astype(jnp.int32)

    jlo, ihi = pl.pallas_call(
        body, name="fox_windows", grid=(nq,),
        in_specs=[pl.BlockSpec((bq, WH), lambda i: (i, CB_FQ // NH)), pl.BlockSpec((bq, WH), lambda i: (i, CB_FK // NH)),
                  pl.BlockSpec((NH, bq, HD), lambda i: (0, i, 0))],
        out_specs=[pl.BlockSpec((HD, HD), lambda i: (0, 0)), pl.BlockSpec((8, HD), lambda i: (0, 0))],
        out_shape=[jax.ShapeDtypeStruct((HD, HD), jnp.int32), jax.ShapeDtypeStruct((8, HD), jnp.int32)],
        scratch_shapes=[pltpu.VMEM((16, HD), F32), pltpu.VMEM((NH, HD, HD), F32), pltpu.VMEM((8, HD), F32)],
        compiler_params=_params(("arbitrary",)),
    )(proj, proj, cq)
    return jnp.concatenate([jlo[:nq, 0], ihi[0, :nq]])


def _fox_fwd(win, proj, ct, cq):
    T = proj.shape[0]
    bq = bk = _tile(T, (512, 256, 128))
    nq = nk = T // bq

    def body(win_ref, q_ref, k_ref, v_ref, ct_ref, cq_ref, o_ref, lse_ref, m_s, l_s, acc_s):
        i, jj = pl.program_id(0), pl.program_id(1)
        j = win_ref[i] + jj

        @pl.when(jj == 0)
        def _():
            m_s[...] = jnp.full_like(m_s, NEG)
            l_s[...] = jnp.zeros_like(l_s)
            acc_s[...] = jnp.zeros_like(acc_s)

        @pl.when(j <= i)
        def _():
            for h in range(NH):
                hs = _heads(h)
                k, v, ck = k_ref[:, hs], v_ref[:, hs], ct_ref[h:h + 1, :]
                for r0 in range(0, bq, STRIP):
                    rs = slice(r0, r0 + STRIP)
                    s = _fox_scores(q_ref[rs, hs], k, cq_ref[h, rs, 0:1], ck, i, j, bq, bk, r0)
                    m_prev = m_s[h, rs]
                    m_new = jnp.maximum(m_prev, jnp.max(s, axis=1, keepdims=True))
                    alpha = jnp.exp(m_prev - m_new)
                    p = jnp.exp(s - m_new)
                    l_s[h, rs] = alpha * l_s[h, rs] + jnp.sum(p, axis=1, keepdims=True)
                    acc_s[rs, hs] = alpha * acc_s[rs, hs] + _dot(p, v, NN)
                    m_s[h, rs] = m_new

        @pl.when(jj == nk - 1)
        def _():
            for h in range(NH):
                o_ref[:, _heads(h)] = acc_s[:, _heads(h)] / l_s[h]
                lse_ref[h] = jnp.broadcast_to(m_s[h] + jnp.log(l_s[h]), (bq, HD))

    def key_block(i, jj, win):
        return jnp.minimum(win[i] + jj, i)

    def kv(off):
        return pl.BlockSpec((bk, WH), lambda i, jj, win, o=off // NH: (key_block(i, jj, win), o))

    stat = pl.BlockSpec((NH, bq, HD), lambda i, jj, win: (0, i, 0))
    return pl.pallas_call(
        body, name="fox_fwd",
        grid_spec=pltpu.PrefetchScalarGridSpec(
            num_scalar_prefetch=1, grid=(nq, nk),
            in_specs=[pl.BlockSpec((bq, WH), lambda i, jj, win: (i, CB_FQ // NH)), kv(CB_FK), kv(CB_FV),
                      pl.BlockSpec((8, bk), lambda i, jj, win: (0, key_block(i, jj, win))), stat],
            out_specs=[pl.BlockSpec((bq, WH), lambda i, jj, win: (i, 0)), stat],
            scratch_shapes=[pltpu.VMEM((NH, bq, 1), F32), pltpu.VMEM((NH, bq, 1), F32), pltpu.VMEM((bq, WH), F32)]),
        out_shape=[jax.ShapeDtypeStruct((T, WH), F32), jax.ShapeDtypeStruct((NH, T, HD), F32)],
        compiler_params=_params(("parallel", "arbitrary")),
    )(win, proj, proj, proj, ct, cq)


def _fox_bwd_dq(win, proj, ct, cq, lse, do):
    T = proj.shape[0]
    bq = bk = _tile(T, (512, 256, 128))
    nq = nk = T // bq

    def body(win_ref, q_ref, k_ref, v_ref, ct_ref, cq_ref, lse_ref, do_ref, dq_ref, delta_ref, acc_s, delta_s, psum_s):
        i, jj = pl.program_id(0), pl.program_id(1)
        j = win_ref[i] + jj % nk

        @pl.when(jj == 0)
        def _():
            acc_s[...] = jnp.zeros_like(acc_s)
            delta_s[...] = jnp.zeros_like(delta_s)
            psum_s[...] = jnp.zeros_like(psum_s)

        def probs(h):
            hs = _heads(h)
            k = k_ref[:, hs]
            s = _fox_scores(q_ref[:, hs], k, cq_ref[h, :, 0:1], ct_ref[h:h + 1, :], i, j, bq, bk)
            return k, jnp.exp(s - lse_ref[h, :, 0:1]), _dot(do_ref[:, hs], v_ref[:, hs], NT)

        @pl.when((j <= i) & (jj < nk))
        def _():
            for h in range(NH):
                _, p, dp = probs(h)
                delta_s[h] += jnp.sum(p * dp, axis=1, keepdims=True)
                psum_s[h] += jnp.sum(p, axis=1, keepdims=True)

        @pl.when((j <= i) & (jj >= nk))
        def _():
            for h in range(NH):
                k, p, dp = probs(h)
                ds = p * (dp - delta_s[h] / psum_s[h])
                acc_s[:, _heads(h)] += _dot(ds, k, NN) * SCALE

        @pl.when(jj == 2 * nk - 1)
        def _():
            dq_ref[...] = acc_s[...].astype(dq_ref.dtype)
            for h in range(NH):
                delta_ref[h] = jnp.broadcast_to(delta_s[h] / psum_s[h], (bq, HD))

    def key_block(i, jj, win):
        return jnp.minimum(win[i] + jj % nk, i)

    def kv(off):
        return pl.BlockSpec((bk, WH), lambda i, jj, win, o=off // NH: (key_block(i, jj, win), o))

    qrow = pl.BlockSpec((bq, WH), lambda i, jj, win: (i, 0))
    stat = pl.BlockSpec((NH, bq, HD), lambda i, jj, win: (0, i, 0))
    return pl.pallas_call(
        body, name="fox_bwd_dq",
        grid_spec=pltpu.PrefetchScalarGridSpec(
            num_scalar_prefetch=1, grid=(nq, 2 * nk),
            in_specs=[pl.BlockSpec((bq, WH), lambda i, jj, win: (i, CB_FQ // NH)), kv(CB_FK), kv(CB_FV),
                      pl.BlockSpec((8, bk), lambda i, jj, win: (0, key_block(i, jj, win))), stat, stat, qrow],
            out_specs=[qrow, stat],
            scratch_shapes=[pltpu.VMEM((bq, WH), F32), pltpu.VMEM((NH, bq, 1), F32), pltpu.VMEM((NH, bq, 1), F32)]),
        out_shape=[jax.ShapeDtypeStruct((T, WH), BF16), jax.ShapeDtypeStruct((NH, T, HD), F32)],
        compiler_params=_params(("parallel", "arbitrary")),
    )(win, proj, proj, proj, ct, cq, lse, do)


def _fox_bwd_dkv(win, proj, ct, cq, lse, delta, do):
    T = proj.shape[0]
    bq = bk = _tile(T, (512, 256, 128))
    nq = nk = T // bq

    def body(win_ref, q_ref, k_ref, v_ref, ct_ref, cq_ref, lse_ref, delta_ref, do_ref, dk_ref, dv_ref, dc_ref,
             dk_s, dv_s, dc_s):
        j, ii = pl.program_id(0), pl.program_id(1)
        i = j + ii

        @pl.when(ii == 0)
        def _():
            dk_s[...] = jnp.zeros_like(dk_s)
            dv_s[...] = jnp.zeros_like(dv_s)
            dc_s[...] = jnp.zeros_like(dc_s)

        @pl.when(i <= win_ref[nq + j])
        def _():
            for h in range(NH):
                hs = _heads(h)
                q = q_ref[:, hs]
                d_o = do_ref[:, hs]
                s = _fox_scores(q, k_ref[:, hs], cq_ref[h, :, 0:1], ct_ref[h:h + 1, :], i, j, bq, bk)
                p = jnp.exp(s - lse_ref[h, :, 0:1])
                dv_s[:, hs] += _dot(p, d_o, TN)
                dp = _dot(d_o, v_ref[:, hs], NT)
                ds = p * (dp - delta_ref[h, :, 0:1])
                dk_s[:, hs] += _dot(ds, q, TN) * SCALE
                dc_s[h:h + 1, :] -= jnp.sum(ds, axis=0, keepdims=True)

        @pl.when(ii == nq - 1)
        def _():
            dk_ref[...] = dk_s[...].astype(dk_ref.dtype)
            dv_ref[...] = dv_s[...].astype(dv_ref.dtype)
            for h in range(NH):
                dc_ref[h] = jnp.broadcast_to(dc_s[h:h + 1, :], (8, bk))

    def query_block(j, ii, win):
        return jnp.minimum(j + ii, win[nq + j])

    def kv(off):
        return pl.BlockSpec((bk, WH), lambda j, ii, win, o=off // NH: (j, o))

    qrow = pl.BlockSpec((bq, WH), lambda j, ii, win: (query_block(j, ii, win), 0))
    stat = pl.BlockSpec((NH, bq, HD), lambda j, ii, win: (0, query_block(j, ii, win), 0))
    krow = pl.BlockSpec((bk, WH), lambda j, ii, win: (j, 0))
    return pl.pallas_call(
        body, name="fox_bwd_dkv",
        grid_spec=pltpu.PrefetchScalarGridSpec(
            num_scalar_prefetch=1, grid=(nk, nq),
            in_specs=[pl.BlockSpec((bq, WH), lambda j, ii, win: (query_block(j, ii, win), CB_FQ // NH)), kv(CB_FK),
                      kv(CB_FV), pl.BlockSpec((8, bk), lambda j, ii, win: (0, j)), stat, stat, stat, qrow],
            out_specs=[krow, krow, pl.BlockSpec((NH, 8, bk), lambda j, ii, win: (0, 0, j))],
            scratch_shapes=[pltpu.VMEM((bk, WH), F32), pltpu.VMEM((bk, WH), F32), pltpu.VMEM((8, bk), F32)]),
        out_shape=[jax.ShapeDtypeStruct((T, WH), BF16), jax.ShapeDtypeStruct((T, WH), BF16),
                   jax.ShapeDtypeStruct((NH, 8, T), F32)],
        compiler_params=_params(("parallel", "arbitrary")),
    )(win, proj, proj, proj, ct, cq, lse, delta, do)


def _mem_probs(q, mk):
    s = _dot(q, mk, NT) * SCALE
    e = jnp.exp(s - jnp.max(s, axis=1, keepdims=True))
    return e / jnp.sum(e, axis=1, keepdims=True)


def _mem_fwd(proj, mem_kv):
    T = proj.shape[0]
    tr = _tile(T, (512, 256, 128))
    M = mem_kv.shape[0]

    def body(q_ref, mk_ref, mv_ref, o_ref):
        o_ref[...] = _dot(_mem_probs(q_ref[...], mk_ref[...]), mv_ref[...], NN)

    return pl.pallas_call(
        body, name="mem_fwd", grid=(NM, T // tr),
        in_specs=[pl.BlockSpec((tr, HD), lambda h, i: (i, CB_MQ + h)),
                  pl.BlockSpec((M, HD), lambda h, i: (0, h)), pl.BlockSpec((M, HD), lambda h, i: (0, NM + h))],
        out_specs=pl.BlockSpec((tr, HD), lambda h, i: (i, h)),
        out_shape=jax.ShapeDtypeStruct((T, WM), F32),
        compiler_params=_params(("parallel", "parallel")),
    )(proj, mem_kv, mem_kv)


def _mem_bwd(proj, mem_kv, do):
    T = proj.shape[0]
    tr = _tile(T, (512, 256, 128))
    M = mem_kv.shape[0]

    def body(q_ref, mk_ref, mv_ref, do_ref, dq_ref, dmk_ref, dmv_ref):
        @pl.when(pl.program_id(1) == 0)
        def _():
            dmk_ref[...] = jnp.zeros_like(dmk_ref)
            dmv_ref[...] = jnp.zeros_like(dmv_ref)

        q, mk, d_o = q_ref[...], mk_ref[...], do_ref[...]
        p = _mem_probs(q, mk)
        dmv_ref[...] += _dot(p, d_o, TN)
        dp = _dot(d_o, mv_ref[...], NT)
        ds = p * (dp - jnp.sum(p * dp, axis=1, keepdims=True))
        dq_ref[...] = (_dot(ds, mk, NN) * SCALE).astype(dq_ref.dtype)
        dmk_ref[...] += _dot(ds, q, TN) * SCALE

    acc = pl.BlockSpec((M, HD), lambda h, i: (0, h))
    row = pl.BlockSpec((tr, HD), lambda h, i: (i, h))
    return pl.pallas_call(
        body, name="mem_bwd", grid=(NM, T // tr),
        in_specs=[pl.BlockSpec((tr, HD), lambda h, i: (i, CB_MQ + h)),
                  pl.BlockSpec((M, HD), lambda h, i: (0, h)), pl.BlockSpec((M, HD), lambda h, i: (0, NM + h)), row],
        out_specs=[row, acc, acc],
        out_shape=[jax.ShapeDtypeStruct((T, WM), BF16), jax.ShapeDtypeStruct((M, WM), F32),
                   jax.ShapeDtypeStruct((M, WM), F32)],
        compiler_params=_params(("parallel", "arbitrary")),
    )(proj, mem_kv, mem_kv, do)


def _mesh_place():
    x, y, c = lax.axis_index("x"), lax.axis_index("y"), lax.axis_index("c")
    return x, y, c


CHIP_FLIPS = (4, 2, 6)
CHIP_OF_SLOT = (0,) + CHIP_FLIPS


def _peer(x, y, c, k):
    px = 1 - x if k & 4 else x
    py = 1 - y if k & 2 else y
    pc = 1 - c if k & 1 else c
    return (px, py, pc), 4 * px + 2 * py + pc


class _Gather:
    def __init__(self, shapes, pad_rows):
        self.shapes, self.pad_rows, self.n = shapes, pad_rows, len(shapes)
        self.npad = sum(1 for p in pad_rows if p)

    def zeros(self):
        return jnp.zeros((max(self.pad_rows) or 16, self.shapes[0][1]), BF16)

    def out_shape(self):
        return [jax.ShapeDtypeStruct((NDEV * r + p, c), BF16) for (r, c), p in zip(self.shapes, self.pad_rows)]

    def sems(self):
        return [pltpu.SemaphoreType.DMA((self.n, NDEV - 1)), pltpu.SemaphoreType.DMA((self.n, NDEV - 1)),
                pltpu.SemaphoreType.DMA((self.n + self.npad,))]

    def _copies(self, ins, z_ref, outs, send_sems, recv_sems, loc_sems):
        x, y, c = _mesh_place()
        me = 4 * x + 2 * y + c
        sibling, _ = _peer(x, y, c, 1)
        local, first, arrive, forward = [], [], [], []
        ip = 0
        for w in range(self.n):
            r = ins[w].shape[0]
            dst = outs[w].at[pl.ds(pl.multiple_of(me * r, 16), r), :]
            local.append(functools.partial(pltpu.make_async_copy, ins[w], dst, loc_sems.at[w]))
            if self.pad_rows[w]:
                local.append(functools.partial(pltpu.make_async_copy, z_ref.at[pl.ds(0, self.pad_rows[w]), :],
                                               outs[w].at[pl.ds(NDEV * r, self.pad_rows[w]), :], loc_sems.at[self.n + ip]))
                ip += 1

            def remote(src, dst_, s, to):
                return functools.partial(pltpu.make_async_remote_copy, src_ref=src, dst_ref=dst_, send_sem=send_sems.at[w, s],
                                         recv_sem=recv_sems.at[w, s], device_id=to, device_id_type=MESH)

            for s, k in enumerate((1,) + CHIP_FLIPS):
                first.append(remote(ins[w], dst, s, _peer(x, y, c, k)[0]))
            for s, k in enumerate(CHIP_FLIPS):
                _, pidx = _peer(x, y, c, k)
                rows = outs[w].at[pl.ds(pl.multiple_of(pidx * r, 16), r), :]
                arrive.append(remote(rows, rows, 1 + s, sibling))
                forward.append(remote(rows, rows, 4 + s, sibling))
        return local, first, arrive, forward


    def start(self, *refs):
        local, first, _, _ = self._copies(*refs)
        for make in local + first:
            make().start()

    def forward(self, *refs):
        _, _, arrive, forward = self._copies(*refs)
        for a, f in zip(arrive, forward):
            a().wait_recv()
            f().start()

    def finish(self, *refs):
        local, first, _, forward = self._copies(*refs)
        for make in local + first[0::4] + forward:
            make().wait()
        for s in (1, 2, 3):
            for make in first[s::4]:
                make().wait_send()


def _all_gather(shards, pad_rows):
    n = len(shards)
    plan = _Gather([s.shape for s in shards], pad_rows)

    def body(*refs):
        args = (refs[:n], refs[n], refs[n + 1:2 * n + 1]) + tuple(refs[2 * n + 1:])
        plan.start(*args)
        plan.forward(*args)
        plan.finish(*args)

    any_spec = pl.BlockSpec(memory_space=pl.ANY)
    return pl.pallas_call(
        body, name="all_gather_weights",
        in_specs=[any_spec] * (n + 1), out_specs=[any_spec] * n,
        out_shape=plan.out_shape(),
        scratch_shapes=plan.sems(),
        compiler_params=pltpu.CompilerParams(has_side_effects=True),
    )(*shards, plan.zeros())


def _exchange_in_chip(grads, shard_rows, name):
    n = len(grads)
    ns = len(CHIP_OF_SLOT)

    def body(*refs):
        ins, theirs = refs[:n], refs[n:2 * n]
        send_sems, recv_sems = refs[2 * n:]
        x, y, c = _mesh_place()
        sibling, _ = _peer(x, y, c, 1)
        copies = []
        for w in range(n):
            r = shard_rows[w]
            for s, k in enumerate(CHIP_OF_SLOT):
                _, other = _peer(x, y, c, k | 1)
                cp = pltpu.make_async_remote_copy(
                    src_ref=ins[w].at[pl.ds(pl.multiple_of(other * r, 16), r), :], dst_ref=theirs[w].at[s],
                    send_sem=send_sems.at[w, s], recv_sem=recv_sems.at[w, s], device_id=sibling, device_id_type=MESH)
                cp.start()
                copies.append(cp)
        for cp in copies:
            cp.wait()

    any_spec = pl.BlockSpec(memory_space=pl.ANY)
    return pl.pallas_call(
        body, name=name,
        in_specs=[any_spec] * n, out_specs=[any_spec] * n,
        out_shape=[jax.ShapeDtypeStruct((ns, r, g.shape[1]), g.dtype) for g, r in zip(grads, shard_rows)],
        scratch_shapes=[pltpu.SemaphoreType.DMA((n, ns)), pltpu.SemaphoreType.DMA((n, ns))],
        compiler_params=pltpu.CompilerParams(has_side_effects=True),
    )(*grads)


def _pair_sum(grad, theirs, name):
    ns, r, c = theirs.shape
    tr = r if r * c <= 2 * 1024 * 1024 else _tile(r, (256, 128, 64, 32, 16))
    per_block = r // tr

    def body(a_ref, b_ref, o_ref):
        o_ref[...] = (a_ref[...].astype(F32) + b_ref[...].astype(F32)).astype(o_ref.dtype)

    def owner_rows(s, i):
        x, y, c_ = _mesh_place()
        fx, fy = s % 2, s // 2
        px, py = x + fx - 2 * x * fx, y + fy - 2 * y * fy
        return ((4 * px + 2 * py + c_) * per_block + i, 0)

    slot = pl.BlockSpec((None, tr, c), lambda s, i: (s, i, 0))
    return pl.pallas_call(
        body, name=name, grid=(ns, per_block),
        in_specs=[pl.BlockSpec((tr, c), owner_rows), slot], out_specs=slot,
        out_shape=jax.ShapeDtypeStruct((ns, r, c), theirs.dtype),
        compiler_params=_params(("parallel", "parallel")),
    )(grad, theirs)


def _exchange_between_chips(pairs, name):
    n = len(pairs)
    plan = _ChipExchange([p.shape for p in pairs])

    def body(*refs):
        args = (refs[:n], refs[n:2 * n]) + tuple(refs[2 * n:])
        plan.start(*args)
        plan.finish(*args)

    any_spec = pl.BlockSpec(memory_space=pl.ANY)
    return pl.pallas_call(
        body, name=name,
        in_specs=[any_spec] * n, out_specs=[any_spec] * n,
        out_shape=plan.out_shape(), scratch_shapes=plan.sems(),
        compiler_params=pltpu.CompilerParams(has_side_effects=True),
    )(*pairs)


class _ChipExchange:
    def __init__(self, shapes):
        self.shapes, self.n, self.ns = shapes, len(shapes), len(CHIP_OF_SLOT) - 1

    def out_shape(self):
        return [jax.ShapeDtypeStruct((self.ns,) + tuple(s[1:]), BF16) for s in self.shapes]

    def sems(self):
        return [pltpu.SemaphoreType.DMA((self.n, self.ns)), pltpu.SemaphoreType.DMA((self.n, self.ns))]

    def _copies(self, ins, outs, send_sems, recv_sems):
        x, y, c = _mesh_place()
        copies = []
        for w in range(self.n):
            for s, k in enumerate(CHIP_OF_SLOT[1:]):
                peer, _ = _peer(x, y, c, k)
                copies.append(pltpu.make_async_remote_copy(
                    src_ref=ins[w].at[s + 1], dst_ref=outs[w].at[s], send_sem=send_sems.at[w, s],
                    recv_sem=recv_sems.at[w, s], device_id=peer, device_id_type=MESH))
        return copies

    def start(self, *refs):
        for cp in self._copies(*refs):
            cp.start()

    def finish(self, *refs):
        for cp in self._copies(*refs):
            cp.wait()


def _sum_chips(pair, recv, name):
    _, r, c = recv.shape
    tr = _tile(r, (128, 64, 32, 16))

    def body(p_ref, x_ref, o_ref):
        acc = p_ref[...].astype(F32)
        for s in range(x_ref.shape[0]):
            acc = acc + x_ref[s].astype(F32)
        o_ref[...] = acc

    return pl.pallas_call(
        body, name=name, grid=(r // tr,),
        in_specs=[pl.BlockSpec((None, tr, c), lambda i: (0, i, 0)), pl.BlockSpec((recv.shape[0], tr, c), lambda i: (0, i, 0))],
        out_specs=pl.BlockSpec((tr, c), lambda i: (i, 0)),
        out_shape=jax.ShapeDtypeStruct((r, c), F32),
        compiler_params=_params(("parallel",)),
    )(pair, recv)


def _all_reduce_small(part):
    R, W = part.shape

    def body(x_ref, o_ref, buf, send_sems, recv_sems):
        x, y, c = _mesh_place()
        me = 4 * x + 2 * y + c
        buf[me] = x_ref[...]
        copies = []
        for k in range(1, NDEV):
            peer, _ = _peer(x, y, c, k)
            cp = pltpu.make_async_remote_copy(src_ref=x_ref, dst_ref=buf.at[me], send_sem=send_sems.at[k - 1],
                                              recv_sem=recv_sems.at[k - 1], device_id=peer, device_id_type=MESH)
            cp.start()
            copies.append(cp)
        for cp in copies:
            cp.wait()
        acc = buf[0]
        for d in range(1, NDEV):
            acc = acc + buf[d]
        o_ref[...] = acc

    vm = pl.BlockSpec(memory_space=pltpu.VMEM)
    return pl.pallas_call(
        body, name="all_reduce_small", in_specs=[vm], out_specs=vm,
        out_shape=jax.ShapeDtypeStruct((R, W), F32),
        scratch_shapes=[pltpu.VMEM((NDEV, R, W), F32), pltpu.SemaphoreType.DMA((NDEV - 1,)),
                        pltpu.SemaphoreType.DMA((NDEV - 1,))],
        compiler_params=pltpu.CompilerParams(has_side_effects=True),
    )(part)


def _adam_math(w, g, m, v):
    m2 = ADAM_B1 * m + (1.0 - ADAM_B1) * g
    v2 = ADAM_B2 * v + (1.0 - ADAM_B2) * (g * g)
    m_hat = m2 / (1.0 - ADAM_B1 ** ADAM_STEP)
    v_hat = v2 / (1.0 - ADAM_B2 ** ADAM_STEP)
    delta = -ADAM_LR * (m_hat / (jnp.sqrt(v_hat) + ADAM_EPS) + ADAM_WD * w)
    return delta, m2, v2


def _adamw(w, g, m, v, name):
    r, c = w.shape
    tr = r
    for cand in (1024, 512, 256, 128, 64, 32, 16, 8):
        if r % cand == 0 and cand * c <= 256 * 1024:
            tr = cand
            break

    def body(w_ref, g_ref, m_ref, v_ref, d_ref, m2_ref, v2_ref):
        d_ref[...], m2_ref[...], v2_ref[...] = _adam_math(w_ref[...], g_ref[...], m_ref[...], v_ref[...])

    spec = pl.BlockSpec((tr, c), lambda i: (i, 0))
    return pl.pallas_call(
        body, name=name, grid=(r // tr,), in_specs=[spec] * 4, out_specs=[spec] * 3,
        out_shape=[jax.ShapeDtypeStruct((r, c), F32)] * 3,
        compiler_params=_params(("parallel",)),
    )(w, g, m, v)


GAINS = ("ffn1_pre", "ffn1_post", "mix_pre", "mix_post", "mem_norm", "ffn2_pre", "ffn2_post")
GAIN_ROWS = D // HD
ROW_LB = len(GAINS) * GAIN_ROWS
ROWS_GRAD_IN = ROW_LB + 24
ROWS_PACKED = ROW_LB + 32


def _small_update(gsum, w_p, m_p, v_p):
    def body(g_ref, w_ref, m_ref, v_ref, go_ref, d_ref, m2_ref, v2_ref):
        a0 = w_ref[ROW_LB:ROW_LB + 8, :]
        a1 = w_ref[ROW_LB + 8:ROW_LB + 16, :]
        mx = jnp.maximum(a0, a1)
        e0, e1 = jnp.exp(a0 - mx), jnp.exp(a1 - mx)
        lb = e0 / (e0 + e1)
        da0 = g_ref[ROW_LB:ROW_LB + 8, :] * lb * (1.0 - lb)
        g = jnp.concatenate([g_ref[0:ROW_LB, :], da0, -da0, g_ref[ROW_LB + 8:ROWS_GRAD_IN, :]], axis=0)
        go_ref[...] = g
        d_ref[...], m2_ref[...], v2_ref[...] = _adam_math(w_ref[...], g, m_ref[...], v_ref[...])

    vm = pl.BlockSpec(memory_space=pltpu.VMEM)
    return pl.pallas_call(
        body, name="small_update", in_specs=[vm] * 4, out_specs=[vm] * 4,
        out_shape=[jax.ShapeDtypeStruct((ROWS_PACKED, HD), F32)] * 4,
    )(gsum, w_p, m_p, v_p)


def _rows8(a):
    a = a.reshape(-1)
    rows = -(-a.shape[0] // HD)
    rows8 = -(-rows // 8) * 8
    return jnp.pad(a, (0, rows8 * HD - a.shape[0])).reshape(rows8, HD)


def _pack_small(gains, lb0, lb1, gnorm, fb):
    return jnp.concatenate([_rows8(g) for g in gains] + [_rows8(lb0), _rows8(lb1), _rows8(gnorm), _rows8(fb)], axis=0)


def _unpack_small(p):
    out = {}
    for i, name in enumerate(GAINS):
        out[name] = p[i * GAIN_ROWS:(i + 1) * GAIN_ROWS].reshape(1, D)
    lb0 = p[ROW_LB:ROW_LB + NH].reshape(1, WH)
    lb1 = p[ROW_LB + 8:ROW_LB + 8 + NH].reshape(1, WH)
    out["hgrn_lb"] = jnp.concatenate([lb0, lb1], axis=0)
    out["hgrn_gnorm"] = p[ROW_LB + 16:ROW_LB + 16 + NH].reshape(1, WH)
    out["fox_fb"] = p[ROW_LB + 24:ROW_LB + 25, 0:NH]
    return out


def _ffn_forward(xin, pre, post, wg_t, wu_t, wd, tag, rider=None):
    T = xin.shape[0]
    tr = _tile(T, (256, 128))
    (n,) = _rowwise(_norm_fn, [(xin, 0)], [(pre, None)], [BF16], f"{tag}_pre", tr, D, 1)
    g, u, a, *carried = _ffn_up(n, wg_t, wu_t, f"{tag}_up", rider)
    h = _mm(a, wd, "nn", F32, f"{tag}_down")
    (xout,) = _rowwise(functools.partial(_resid_fn, 0.5), [(xin, 0), (h, 0)], [(post, None)], [F32], f"{tag}_post", tr, D, 1)
    return xout, (xin, n, g, u, a, h), carried


def _ffn_backward(dxout, saved, pre, post, wg_t, wu_t, wd, tag, rider=None):
    xin, n, g, u, a, h = saved
    T = xin.shape[0]
    tr = _tile(T, (256, 128))
    dh, dpost = _rowwise_bwd(functools.partial(_resid_h_fn, 0.5), [(h, 0)], [(post, None)], [(dxout, 0)], [0], [BF16],
                             f"{tag}_post_bwd", tr, D, 1)
    dwd = _mm(a, dh, "tn", BF16, f"{tag}_dwd")
    dg, du, *carried = _ffn_act_bwd(dh, wd, g, u, f"{tag}_act_bwd", rider)
    dwg = _mm(dg, n, "tn", BF16, f"{tag}_dwg")
    dwu = _mm(du, n, "tn", BF16, f"{tag}_dwu")
    dn = _mm(dg, wg_t, "nn", F32, f"{tag}_dn_g")
    dn = _mm(du, wu_t, "nn", F32, f"{tag}_dn_u", add=dn)
    dxin, dpre = _rowwise_bwd(_norm_res_fn, [(xin, 0)], [(pre, None)], [(dxout, 0), (dn, 0)], [0], [F32],
                              f"{tag}_pre_bwd", tr, D, 1)
    return dxin, (dwg, dwu, dwd), dpre, dpost, carried


GROUP_FFN1 = ("ffn1_wg", "ffn1_wu", "ffn1_wd")
GROUP_MIX = ("w_in", "w_mem_kv", "w_hgrn_out", "w_fox_out", "w_mem_out", "w_gate", "w_o")
GROUP_FFN2 = ("ffn2_wg", "ffn2_wu", "ffn2_wd")


def _gather_rider(blocks, names):
    plan = _Gather([blocks[n].shape for n in names], [FFN_PAD.get(n, 0) for n in names])
    return _Rider(plan, [blocks[n] for n in names] + [plan.zeros()], GATHER_STEPS)


def _local_step(x, mem, tgt, small, wts=None, blocks=None):
    T = x.shape[0]
    tr = _tile(T, (256, 128))
    fb_pad = jnp.pad(small["fox_fb"], ((0, 0), (0, HD - NH)))
    dist = blocks is not None
    ride_mix = ride_ffn2 = None
    if dist:
        wts = dict(zip(GROUP_FFN1, _all_gather([blocks[n] for n in GROUP_FFN1], [FFN_PAD[n] for n in GROUP_FFN1])))
        ride_mix, ride_ffn2 = _gather_rider(blocks, GROUP_MIX), _gather_rider(blocks, GROUP_FFN2)

    x1, ffn1_saved, carried = _ffn_forward(x, small["ffn1_pre"], small["ffn1_post"], wts["ffn1_wg"], wts["ffn1_wu"],
                                           wts["ffn1_wd"], "ffn1", ride_mix)
    wts.update(zip(GROUP_MIX, carried))
    (un,) = _rowwise(_norm_fn, [(x1, 0)], [(small["mix_pre"], None)], [BF16], "mix_pre", tr, D, 1)
    proj = _mm(un, wts["w_in"], "nn", F32, "proj")
    z = _mm(un, wts["w_gate"], "nt", F32, "gate_logits")
    (memn,) = _rowwise(_norm_fn, [(mem, 0)], [(small["mem_norm"], None)], [BF16], "mem_norm", mem.shape[0], D, 1)
    mem_kv = _mm(memn, wts["w_mem_kv"], "nn", F32, "mem_kv")

    o_raw, states, *carried = _hgrn_fwd(proj, small["hgrn_lb"], ride_ffn2)
    wts.update(zip(GROUP_FFN2, carried))
    (o_h,) = _rowwise(_hpost_fn, [(o_raw, 0), (proj, CB_HOG)], [(small["hgrn_gnorm"], 0)], [BF16], "hgrn_post",
                      tr, HD, NH)
    ct, cq = _fox_cum(proj, fb_pad)
    win = _fox_windows(proj, cq)
    o_f, lse = _fox_fwd(win, proj, ct, cq)
    o_m = _mem_fwd(proj, mem_kv)

    yh = _mm(o_h, wts["w_hgrn_out"], "nt", F32, "hgrn_out")
    yf = _mm(o_f, wts["w_fox_out"], "nt", F32, "fox_out")
    ym = _mm(o_m, wts["w_mem_out"], "nt", F32, "mem_out")
    zc = D // 512
    merge_rows = [(z, 0), (z, zc), (z, 2 * zc), (yh, 0), (yf, 0), (ym, 0)]
    (merged,) = _rowwise(_merge_fn, merge_rows, [], [BF16], "merge", tr, 512, zc)
    m = _mm(merged, wts["w_o"], "nn", F32, "mix_out")
    (x2,) = _rowwise(functools.partial(_resid_fn, 1.0), [(x1, 0), (m, 0)], [(small["mix_post"], None)], [F32], "mix_post",
                     tr, D, 1)
    x3, ffn2_saved, _ = _ffn_forward(x2, small["ffn2_pre"], small["ffn2_post"], wts["ffn2_wg"], wts["ffn2_wu"],
                                     wts["ffn2_wd"], "ffn2")
    dy, loss_part = _loss(x3, tgt, "loss")

    gw, gs, reduced = {}, {}, {}

    def pair_sums(names, tag):
        if not dist:
            return None, None
        theirs = _exchange_in_chip([gw[n] for n in names], [blocks[n].shape[0] for n in names], f"reduce_in_chip_{tag}")
        pairs = [_pair_sum(gw[n], t_, f"pair_{n}") for n, t_ in zip(names, theirs)]
        return pairs, _Rider(_ChipExchange([p.shape for p in pairs]), pairs, EXCHANGE_STEPS)

    def chip_sums(names, pairs, recv):
        for n, p_, r_ in zip(names, pairs or (), recv):
            reduced[n] = _sum_chips(p_, r_, f"sum_{n}")

    dx2, (gw["ffn2_wg"], gw["ffn2_wu"], gw["ffn2_wd"]), gs["ffn2_pre"], gs["ffn2_post"], _ = _ffn_backward(
        dy, ffn2_saved, small["ffn2_pre"], small["ffn2_post"], wts["ffn2_wg"], wts["ffn2_wu"], wts["ffn2_wd"], "ffn2")
    pairs_ffn2, ride_ffn2_grads = pair_sums(GROUP_FFN2, "ffn2")

    dm, gs["mix_post"] = _rowwise_bwd(functools.partial(_resid_h_fn, 1.0), [(m, 0)], [(small["mix_post"], None)],
                                      [(dx2, 0)], [0], [BF16], "mix_post_bwd", tr, D, 1)
    dmerged = _mm(dm, wts["w_o"], "nt", F32, "d_merged")
    gw["w_o"] = _mm(merged, dm, "tn", BF16, "d_w_o")
    dz0, dz1, dz2, dyh, dyf, dym = _rowwise_bwd(_merge_fn, merge_rows, [], [(dmerged, 0)], [0, 1, 2, 3, 4, 5], [BF16] * 6,
                                                "merge_bwd", tr, 512, zc)
    dz = jnp.concatenate([dz0, dz1, dz2], axis=1)
    gw["w_gate"] = _mm(dz, un, "tn", BF16, "d_w_gate")
    dun = _mm(dz, wts["w_gate"], "nn", F32, "d_un_gate")

    do_h = _mm(dyh, wts["w_hgrn_out"], "nn", F32, "d_o_h")
    gw["w_hgrn_out"] = _mm(dyh, o_h, "tn", BF16, "d_w_hgrn_out")
    do_f = _mm(dyf, wts["w_fox_out"], "nn", F32, "d_o_f")
    gw["w_fox_out"] = _mm(dyf, o_f, "tn", BF16, "d_w_fox_out")
    do_m = _mm(dym, wts["w_mem_out"], "nn", F32, "d_o_m")
    gw["w_mem_out"] = _mm(dym, o_m, "tn", BF16, "d_w_mem_out")

    do_raw, dhog, gs["hgrn_gnorm"] = _rowwise_bwd(_hpost_fn, [(o_raw, 0), (proj, CB_HOG)], [(small["hgrn_gnorm"], 0)],
                                                  [(do_h, 0)], [0, 1], [F32, BF16], "hgrn_post_bwd", tr, HD, NH)
    dhq, dhf, dhi, gs["hgrn_lb"], *carried = _hgrn_bwd(proj, small["hgrn_lb"], states, do_raw, ride_ffn2_grads)
    chip_sums(GROUP_FFN2, pairs_ffn2, carried)
    dfq, delta = _fox_bwd_dq(win, proj, ct, cq, lse, do_f)
    dfk, dfv, dc = _fox_bwd_dkv(win, proj, ct, cq, lse, delta, do_f)
    dff, dfb = _fox_cum_bwd(dc, proj, fb_pad)
    gs["fox_fb"] = dfb
    dmq, dmk, dmv = _mem_bwd(proj, mem_kv, do_m)

    dproj = jnp.concatenate([dhq, dhf, dhi, dhog, dfq, dfk, dfv, dff, dmq, jnp.zeros((T, HD), BF16)], axis=1)
    gw["w_in"] = _mm(un, dproj, "tn", BF16, "d_w_in")
    dun = _mm(dproj, wts["w_in"], "nt", F32, "d_un_proj", add=dun)
    dx1, gs["mix_pre"] = _rowwise_bwd(_norm_res_fn, [(x1, 0)], [(small["mix_pre"], None)], [(dx2, 0), (dun, 0)], [0], [F32],
                                      "mix_pre_bwd", tr, D, 1)

    dmem_kv = jnp.concatenate([dmk, dmv], axis=1)
    gw["w_mem_kv"] = _mm(memn, dmem_kv, "tn", BF16, "d_w_mem_kv")
    dmemn = _mm(dmem_kv, wts["w_mem_kv"], "nt", F32, "d_memn")
    _, gs["mem_norm"] = _rowwise_bwd(_norm_fn, [(mem, 0)], [(small["mem_norm"], None)], [(dmemn, 0)], [0], [BF16],
                                     "mem_norm_bwd", mem.shape[0], D, 1)

    pairs_mix, ride_mix_grads = pair_sums(GROUP_MIX, "mix")
    dx, (gw["ffn1_wg"], gw["ffn1_wu"], gw["ffn1_wd"]), gs["ffn1_pre"], gs["ffn1_post"], carried = _ffn_backward(
        dx1, ffn1_saved, small["ffn1_pre"], small["ffn1_post"], wts["ffn1_wg"], wts["ffn1_wu"], wts["ffn1_wd"], "ffn1",
        ride_mix_grads)
    chip_sums(GROUP_MIX, pairs_mix, carried)
    if not dist:
        return loss_part, dx, gw, gs
    pairs_ffn1, _ = pair_sums(GROUP_FFN1, "ffn1")
    chip_sums(GROUP_FFN1, pairs_ffn1, _exchange_between_chips(pairs_ffn1, "reduce_between_chips_ffn1"))
    return loss_part, dx, reduced, gs


BIG = ("ffn1_wg", "ffn1_wu", "ffn1_wd", "w_in", "w_mem_kv", "w_hgrn_out", "w_fox_out", "w_mem_out", "w_gate", "w_o",
       "ffn2_wg", "ffn2_wu", "ffn2_wd")
TRANSPOSED = ("ffn1_wg", "ffn1_wu", "ffn2_wg", "ffn2_wu", "w_hgrn_out", "w_fox_out", "w_mem_out", "w_gate")
FFN_PAD = {"ffn1_wg": FP - F, "ffn1_wu": FP - F, "ffn1_wd": FP - F, "ffn2_wg": FP - F, "ffn2_wu": FP - F,
           "ffn2_wd": FP - F}
SMALL = GAINS + ("hgrn_lb", "hgrn_gnorm", "fox_fb")
WEIGHTS = ("ffn1_pre", "ffn1_post", "ffn1_wg", "ffn1_wu", "ffn1_wd", "mix_pre", "mix_post", "mem_norm", "w_in", "hgrn_lb",
           "hgrn_gnorm", "fox_fb", "w_mem_kv", "w_hgrn_out", "w_fox_out", "w_mem_out", "w_gate", "w_o", "ffn2_pre",
           "ffn2_post", "ffn2_wg", "ffn2_wu", "ffn2_wd")


def _to_gather_layout(name, w):
    if name in TRANSPOSED:
        w = w.T
    if name == "w_in":
        r = w.shape[0]
        w = jnp.concatenate([w[:, :MQ_COL], jnp.zeros((r, FF_COL + HD - MQ_COL), w.dtype), w[:, MQ_COL:],
                             jnp.zeros((r, P - FF_COL - HD - WM), w.dtype)], axis=1)
    return w.astype(BF16)


def _from_gather_layout(name, g):
    if name == "w_in":
        g = jnp.concatenate([g[:, :MQ_COL], g[:, FF_COL + HD:FF_COL + HD + WM]], axis=1)
    if name in TRANSPOSED:
        g = g.T
    return g


def kernel(x, mem, ffn1_pre, ffn1_post, ffn1_wg, ffn1_wu, ffn1_wd, mix_pre, mix_post, mem_norm, w_in, hgrn_lb, hgrn_gnorm, fox_fb, w_mem_kv, w_hgrn_out, w_fox_out, w_mem_out, w_gate, w_o, ffn2_pre, ffn2_post, ffn2_wg, ffn2_wu, ffn2_wd, loss_target, m_ffn1_pre, m_ffn1_post, m_ffn1_wg, m_ffn1_wu, m_ffn1_wd, m_mix_pre, m_mix_post, m_mem_norm, m_w_in, m_hgrn_lb, m_hgrn_gnorm, m_fox_fb, m_w_mem_kv, m_w_hgrn_out, m_w_fox_out, m_w_mem_out, m_w_gate, m_w_o, m_ffn2_pre, m_ffn2_post, m_ffn2_wg, m_ffn2_wu, m_ffn2_wd, v_ffn1_pre, v_ffn1_post, v_ffn1_wg, v_ffn1_wu, v_ffn1_wd, v_mix_pre, v_mix_post, v_mem_norm, v_w_in, v_hgrn_lb, v_hgrn_gnorm, v_fox_fb, v_w_mem_kv, v_w_hgrn_out, v_w_fox_out, v_w_mem_out, v_w_gate, v_w_o, v_ffn2_pre, v_ffn2_post, v_ffn2_wg, v_ffn2_wu, v_ffn2_wd):
    a = dict(locals())
    small = {n: a[n] for n in SMALL}
    shard = {n: a[n][0] if a[n].ndim == 3 else a[n] for n in BIG}

    blocks = {n: _to_gather_layout(n, shard[n]) for n in BIG}
    loss_part, dx, reduced, gs = _local_step(x[0], mem[0], loss_target[0], small, blocks=blocks)
    loss = lax.psum(0.5 / D * jnp.sum(loss_part), ("x", "y", "c"))

    grads, deltas, new_m, new_v = {}, {}, {}, {}
    for n in BIG:
        g = _from_gather_layout(n, reduced[n])
        d, m2, v2 = _adamw(shard[n], g, a["m_" + n].reshape(g.shape), a["v_" + n].reshape(g.shape), f"adamw_{n}")
        full = a[n].shape
        grads[n], deltas[n], new_m[n], new_v[n] = g.reshape(full), d.reshape(full), m2.reshape(full), v2.reshape(full)

    part = jnp.concatenate([_rows8(gs[n]) for n in GAINS] + [_rows8(gs["hgrn_lb"]), _rows8(gs["hgrn_gnorm"]),
                                                             _rows8(gs["fox_fb"][:, :NH])], axis=0)
    gsum = _all_reduce_small(part)

    def packed(prefix):
        lb = a[prefix + "hgrn_lb"]
        return _pack_small([a[prefix + n] for n in GAINS], lb[0], lb[1], a[prefix + "hgrn_gnorm"], a[prefix + "fox_fb"])

    g_p, d_p, m_p, v_p = _small_update(gsum, packed(""), packed("m_"), packed("v_"))
    for dst, p in ((grads, g_p), (deltas, d_p), (new_m, m_p), (new_v, v_p)):
        dst.update(_unpack_small(p))

    return (loss, dx[None], *[grads[n] for n in WEIGHTS], *[deltas[n] for n in WEIGHTS],
            *[new_m[n] for n in WEIGHTS], *[new_v[n] for n in WEIGHTS])
```

```python
import functools

import jax
import jax.numpy as jnp
from jax import lax
from jax.experimental import pallas as pl
from jax.experimental.pallas import tpu as pltpu

F32 = jnp.float32
BF16 = jnp.bfloat16
HIGHEST = lax.Precision.HIGHEST

NDEV = 8
D = 2048
F = 5504
FP = 5632
HD = 128
NH = 6
NM = 4
WH = NH * HD
WM = NM * HD
P = 6144
FF_COL = 5376
MQ_COL = 5382
CHUNK = 64
EPS = 1e-6
SCALE = HD ** -0.5
NEG = -1e30
VMEM_LIMIT = 48 * 1024 * 1024

CB_HQ, CB_HF, CB_HI, CB_HOG, CB_FQ, CB_FK, CB_FV, CB_FF, CB_MQ = 0, 6, 12, 18, 24, 30, 36, 42, 43

ADAM_LR, ADAM_B1, ADAM_B2, ADAM_EPS, ADAM_WD, ADAM_STEP = 0.001, 0.9, 0.999, 1e-08, 0.01, 10

NT = (((1,), (1,)), ((), ()))
NN = (((1,), (0,)), ((), ()))
TN = (((0,), (0,)), ((), ()))
MESH = pl.DeviceIdType.MESH


def _params(sem=None, **kw):
    return pltpu.CompilerParams(dimension_semantics=sem, vmem_limit_bytes=VMEM_LIMIT, **kw)


def _tile(n, prefs):
    for p in prefs:
        if p <= n and n % p == 0:
            return p
    return n


def _dot(a, b, dims):
    return lax.dot_general(a.astype(BF16), b.astype(BF16), dims, preferred_element_type=F32)


def _mm(a, b, mode, out_dtype, name, add=None, rider=None):
    if mode == "nn":
        (M, K), (K2, N) = a.shape, b.shape
    elif mode == "nt":
        (M, K), (N, K2) = a.shape, b.shape
    else:
        (K, M), (K2, N) = a.shape, b.shape
    assert K == K2, (a.shape, b.shape, mode)
    if mode == "tn":
        tm = _tile(M, (512, 256, 128))
        tn = _tile(N, (1024, 768, 512, 256, 128))
        tk = _tile(K, (4096, 2048, 1024, 512, 256, 128))
    else:
        tm = _tile(M, (1024, 512, 256, 128)) if K <= 2048 else _tile(M, (512, 256, 128))
        tn = _tile(N, (512, 768, 256, 128))
        tk = K if K <= 6144 else _tile(K, (2048, 1024, 512, 256, 128))
    nk = K // tk
    dims = {"nn": NN, "nt": NT, "tn": TN}[mode]
    has_add = add is not None

    ni, nj = M // tm, N // tn
    n_in = 3 if has_add else 2

    def body(*refs):
        step = (pl.program_id(0) * nj + pl.program_id(1)) * nk + pl.program_id(2)
        refs = _carry(rider, refs, n_in, 1, 1, step, ni * nj * nk)
        a_ref, b_ref = refs[0], refs[1]
        c_ref = refs[2] if has_add else None
        o_ref = refs[3] if has_add else refs[2]
        acc_ref = refs[-1]
        k = pl.program_id(2)
        part = _dot(a_ref[...], b_ref[...], dims)

        def finish(r):
            if has_add:
                r = r + c_ref[...].astype(F32)
            o_ref[...] = r.astype(o_ref.dtype)

        if nk == 1:
            finish(part)
        else:
            @pl.when(k == 0)
            def _():
                acc_ref[...] = part

            @pl.when(k > 0)
            def _():
                acc_ref[...] += part

            @pl.when(k == nk - 1)
            def _():
                finish(acc_ref[...])

    if mode == "nn":
        a_spec = pl.BlockSpec((tm, tk), lambda i, j, k: (i, k))
        b_spec = pl.BlockSpec((tk, tn), lambda i, j, k: (k, j))
    elif mode == "nt":
        a_spec = pl.BlockSpec((tm, tk), lambda i, j, k: (i, k))
        b_spec = pl.BlockSpec((tn, tk), lambda i, j, k: (j, k))
    else:
        a_spec = pl.BlockSpec((tk, tm), lambda i, j, k: (k, i))
        b_spec = pl.BlockSpec((tk, tn), lambda i, j, k: (k, j))
    o_spec = pl.BlockSpec((tm, tn), lambda i, j, k: (i, j))
    args = (a, b) + ((add,) if has_add else ())
    in_specs, out_specs, out_shape, scratch, extra = _with_rider(
        rider, [a_spec, b_spec] + ([o_spec] if has_add else []), [o_spec], [jax.ShapeDtypeStruct((M, N), out_dtype)],
        [pltpu.VMEM((tm, tn) if nk > 1 else (8, 128), F32)])
    out = pl.pallas_call(
        body, name=name, grid=(ni, nj, nk), in_specs=in_specs, out_specs=out_specs, out_shape=out_shape,
        scratch_shapes=scratch,
        compiler_params=_params(("arbitrary",) * 3 if rider else ("parallel", "parallel", "arbitrary"),
                                has_side_effects=rider is not None),
    )(*args, *extra)
    return out if rider else out[0]


class _Rider:
    def __init__(self, plan, inputs, steps):
        self.plan, self.inputs, self.steps = plan, list(inputs), steps
        self.n_out = len(plan.out_shape())
        self.n_sem = len(plan.sems())

    def run(self, step, total, in_refs, out_refs, sem_refs):
        n = self.plan.n
        if isinstance(self.plan, _Gather):
            args = (in_refs[:n], in_refs[n], out_refs) + tuple(sem_refs)
        else:
            args = (in_refs, out_refs) + tuple(sem_refs)
        for frac, method in self.steps:
            @pl.when(step == int(frac * (total - 1)))
            def _(method=method):
                getattr(self.plan, method)(*args)


GATHER_STEPS = ((0.0, "start"), (0.6, "forward"), (1.0, "finish"))
EXCHANGE_STEPS = ((0.0, "start"), (1.0, "finish"))


def _carry(rider, refs, n_in, n_out, n_scratch, step, total):
    if rider is None:
        return refs
    ri, ro, rs = len(rider.inputs), rider.n_out, rider.n_sem
    own_in, rid_in = refs[:n_in], refs[n_in:n_in + ri]
    own_out, rid_out = refs[n_in + ri:n_in + ri + n_out], refs[n_in + ri + n_out:n_in + ri + n_out + ro]
    own_scr, rid_sem = refs[n_in + ri + n_out + ro:n_in + ri + n_out + ro + n_scratch], refs[len(refs) - rs:]
    rider.run(step, total, rid_in, rid_out, rid_sem)
    return tuple(own_in) + tuple(own_out) + tuple(own_scr)


def _with_rider(rider, in_specs, out_specs, out_shape, scratch):
    if rider is None:
        return in_specs, out_specs, out_shape, scratch, ()
    any_spec = pl.BlockSpec(memory_space=pl.ANY)
    return (list(in_specs) + [any_spec] * len(rider.inputs), list(out_specs) + [any_spec] * rider.n_out,
            list(out_shape) + rider.plan.out_shape(), list(scratch) + rider.plan.sems(), tuple(rider.inputs))


def _ffn_up(n, wg_t, wu_t, name, rider=None):
    T = n.shape[0]
    tm = _tile(T, (1024, 512, 256, 128))
    tn = 512
    ni, nj = T // tm, FP // tn

    def body(*refs):
        step = pl.program_id(0) * nj + pl.program_id(1)
        n_ref, wg_ref, wu_ref, g_ref, u_ref, a_ref = _carry(rider, refs, 3, 3, 0, step, ni * nj)
        x = n_ref[...]
        g = _dot(x, wg_ref[...], NT)
        u = _dot(x, wu_ref[...], NT)
        g_ref[...] = g
        u_ref[...] = u
        a_ref[...] = (g * jax.nn.sigmoid(g) * u).astype(BF16)

    w_spec = pl.BlockSpec((tn, D), lambda i, j: (j, 0))
    o_spec = pl.BlockSpec((tm, tn), lambda i, j: (i, j))
    in_specs, out_specs, out_shape, scratch, extra = _with_rider(
        rider, [pl.BlockSpec((tm, D), lambda i, j: (i, 0)), w_spec, w_spec], [o_spec, o_spec, o_spec],
        [jax.ShapeDtypeStruct((T, FP), F32), jax.ShapeDtypeStruct((T, FP), F32), jax.ShapeDtypeStruct((T, FP), BF16)], [])
    return pl.pallas_call(
        body, name=name, grid=(ni, nj), in_specs=in_specs, out_specs=out_specs, out_shape=out_shape,
        scratch_shapes=scratch,
        compiler_params=_params(("arbitrary", "arbitrary") if rider else ("parallel", "parallel"),
                                has_side_effects=rider is not None),
    )(n, wg_t, wu_t, *extra)


def _ffn_act_bwd(dh, wd, g, u, name, rider=None):
    T = dh.shape[0]
    tm = _tile(T, (1024, 512, 256, 128))
    tn = 512
    ni, nj = T // tm, FP // tn

    def body(*refs):
        step = pl.program_id(0) * nj + pl.program_id(1)
        dh_ref, wd_ref, g_ref, u_ref, dg_ref, du_ref = _carry(rider, refs, 4, 2, 0, step, ni * nj)
        da = _dot(dh_ref[...], wd_ref[...], NT)
        g = g_ref[...]
        sg = jax.nn.sigmoid(g)
        dg_ref[...] = (da * u_ref[...] * (sg * (1.0 + g * (1.0 - sg)))).astype(dg_ref.dtype)
        du_ref[...] = (da * (g * sg)).astype(du_ref.dtype)

    tile = pl.BlockSpec((tm, tn), lambda i, j: (i, j))
    in_specs, out_specs, out_shape, scratch, extra = _with_rider(
        rider, [pl.BlockSpec((tm, D), lambda i, j: (i, 0)), pl.BlockSpec((tn, D), lambda i, j: (j, 0)), tile, tile],
        [tile, tile], [jax.ShapeDtypeStruct((T, FP), BF16), jax.ShapeDtypeStruct((T, FP), BF16)], [])
    return pl.pallas_call(
        body, name=name, grid=(ni, nj), in_specs=in_specs, out_specs=out_specs, out_shape=out_shape,
        scratch_shapes=scratch,
        compiler_params=_params(("arbitrary", "arbitrary") if rider else ("parallel", "parallel"),
                                has_side_effects=rider is not None),
    )(dh, wd, g, u, *extra)


def _row_specs(rows, tr, cw):
    return [pl.BlockSpec((tr, cw), lambda j, i, o=off: (i, o + j)) for _, off in rows]


def _const_specs(consts, cw):
    specs = []
    for arr, off in consts:
        if off is None:
            specs.append(pl.BlockSpec(arr.shape, lambda j, i: (0, 0)))
        else:
            specs.append(pl.BlockSpec((arr.shape[0], cw), lambda j, i, o=off: (0, o + j)))
    return specs


def _rowwise(fn, rows, consts, out_dtypes, name, tr, cw, ncol):
    T = rows[0][0].shape[0]
    nr, nc = len(rows), len(consts)

    def body(*refs):
        r = [x[...].astype(F32) for x in refs[:nr]]
        c = [x[...] for x in refs[nr:nr + nc]]
        res = fn(*r, *c)
        for o_ref, v in zip(refs[nr + nc:], res):
            o_ref[...] = v.astype(o_ref.dtype)

    o_spec = pl.BlockSpec((tr, cw), lambda j, i: (i, j))
    return pl.pallas_call(
        body, name=name, grid=(ncol, T // tr),
        in_specs=_row_specs(rows, tr, cw) + _const_specs(consts, cw),
        out_specs=[o_spec] * len(out_dtypes),
        out_shape=[jax.ShapeDtypeStruct((T, ncol * cw), dt) for dt in out_dtypes],
        compiler_params=_params(("parallel", "parallel")),
    )(*[a for a, _ in rows], *[a for a, _ in consts])


def _rowwise_bwd(fn, rows, consts, cots, diff, ddtypes, name, tr, cw, ncol):
    T = rows[0][0].shape[0]
    nr, nc, nt, nd = len(rows), len(consts), len(cots), len(diff)

    def body(*refs):
        r = [x[...].astype(F32) for x in refs[:nr]]
        c = [x[...] for x in refs[nr:nr + nc]]
        ct = [x[...].astype(F32) for x in refs[nr + nc:nr + nc + nt]]
        drow_refs = refs[nr + nc + nt:nr + nc + nt + nd]
        dconst_refs = refs[nr + nc + nt + nd:]
        i = pl.program_id(1)

        def f(*args):
            full = list(r)
            for idx, a in zip(diff, args[:nd]):
                full[idx] = a
            return tuple(fn(*full, *args[nd:]))

        _, vjp = jax.vjp(f, *[r[d] for d in diff], *c)
        g = vjp(tuple(ct))
        for o_ref, v in zip(drow_refs, g[:nd]):
            o_ref[...] = v.astype(o_ref.dtype)

        @pl.when(i == 0)
        def _():
            for o_ref in dconst_refs:
                o_ref[...] = jnp.zeros_like(o_ref)

        for o_ref, v in zip(dconst_refs, g[nd:]):
            o_ref[...] += v

    o_spec = pl.BlockSpec((tr, cw), lambda j, i: (i, j))
    out_shape = [jax.ShapeDtypeStruct((T, ncol * cw), dt) for dt in ddtypes]
    out_shape += [jax.ShapeDtypeStruct(a.shape, F32) for a, _ in consts]
    return pl.pallas_call(
        body, name=name, grid=(ncol, T // tr),
        in_specs=_row_specs(rows, tr, cw) + _const_specs(consts, cw) + _row_specs(cots, tr, cw),
        out_specs=[o_spec] * nd + _const_specs(consts, cw),
        out_shape=out_shape,
        compiler_params=_params(("parallel", "arbitrary")),
    )(*[a for a, _ in rows], *[a for a, _ in consts], *[a for a, _ in cots])


def _rms(x, g):
    return x * lax.rsqrt(jnp.mean(x * x, axis=-1, keepdims=True) + EPS) * g


def _silu(x):
    return x * jax.nn.sigmoid(x)


def _norm_fn(x, g):
    return (_rms(x, g),)


def _norm_res_fn(x, g):
    return (x, _rms(x, g))


def _resid_fn(scale, x, h, g):
    return (x + scale * _rms(h, g),)


def _resid_h_fn(scale, h, g):
    return (scale * _rms(h, g),)


def _hpost_fn(o, hog, gn):
    return (_rms(o, gn) * _silu(hog),)


def _merge_fn(z0, z1, z2, yh, yf, ym):
    return (jax.nn.sigmoid(z0) * yh + jax.nn.sigmoid(z1) * yf + jax.nn.sigmoid(z2) * ym,)


def _loss(x3, tgt, name):
    T = x3.shape[0]
    tr = _tile(T, (256, 128))

    def body(x_ref, t_ref, dy_ref, s_ref):
        i = pl.program_id(0)
        e = x_ref[...] - t_ref[...]
        dy_ref[...] = e * (1.0 / D)
        col = jnp.sum(e * e, axis=0, keepdims=True)
        tot = col[:, 0:HD]
        for k in range(1, D // HD):
            tot = tot + col[:, k * HD:(k + 1) * HD]

        @pl.when(i == 0)
        def _():
            s_ref[...] = jnp.zeros_like(s_ref)

        s_ref[...] += tot

    spec = pl.BlockSpec((tr, D), lambda i: (i, 0))
    return pl.pallas_call(
        body, name=name, grid=(T // tr,), in_specs=[spec, spec],
        out_specs=[spec, pl.BlockSpec((1, HD), lambda i: (0, 0))],
        out_shape=[jax.ShapeDtypeStruct((T, D), F32), jax.ShapeDtypeStruct((1, HD), F32)],
        compiler_params=_params(("arbitrary",)),
    )(x3, tgt)


def _lower_bound(lb_ref):
    a0 = lb_ref[0:1, :]
    a1 = lb_ref[1:2, :]
    mx = jnp.maximum(a0, a1)
    e0 = jnp.exp(a0 - mx)
    return e0 / (e0 + jnp.exp(a1 - mx))


def _hgrn_prep(hq, hf, lb):
    g = lb + (1.0 - lb) * jax.nn.sigmoid(hf)
    return _silu(hq), 1.0 - g, jnp.log(g)


def _tri(n, upper):
    r = lax.broadcasted_iota(jnp.int32, (n, n), 0)
    c = lax.broadcasted_iota(jnp.int32, (n, n), 1)
    return (c >= r) if upper else (c <= r)


def _hgrn_factors(q, k, gl):
    low = _tri(CHUNK, False)
    b = lax.dot_general(low.astype(F32), gl, NN, precision=HIGHEST, preferred_element_type=F32)
    bl = b[CHUNK - 1:CHUNK, :]
    ref = b[CHUNK // 2 - 1:CHUNK // 2, :]
    eb = jnp.exp(b)
    ea = jnp.exp(b - ref)
    ebn = jnp.exp(ref - b)
    ek = jnp.exp(bl - b)
    ebl = jnp.exp(bl)
    return low, eb, ea, ebn, ek, ebl


def _hgrn_fwd(proj, hgrn_lb, rider=None):
    T = proj.shape[0]
    cb = _tile(T, (512, 256, 128, 64))
    nchunk = cb // CHUNK

    def body(*refs):
        hq_ref, hf_ref, hi_ref, lb_ref, o_ref, st_ref, state = _carry(rider, refs, 4, 2, 1, pl.program_id(0), T // cb)

        @pl.when(pl.program_id(0) == 0)
        def _():
            state[...] = jnp.zeros_like(state)

        lb = _lower_bound(lb_ref)

        def chunk(c, carry):
            r0 = pl.multiple_of(c * CHUNK, CHUNK)
            for h in range(NH):
                cols = slice(h * HD, (h + 1) * HD)
                q, k, gl = _hgrn_prep(hq_ref[pl.ds(r0, CHUNK), cols], hf_ref[pl.ds(r0, CHUNK), cols], lb[:, cols])
                v = hi_ref[pl.ds(r0, CHUNK), cols]
                low, eb, ea, ebn, ek, ebl = _hgrn_factors(q, k, gl)
                s_t = state[h]
                st_ref[c, h] = s_t
                pm = jnp.where(low, _dot(q * ea, k * ebn, NT), 0.0)
                o_ref[pl.ds(r0, CHUNK), cols] = _dot(q * eb, s_t, NT) + _dot(pm, v, NN)
                state[h] = s_t * ebl + _dot(v, k * ek, TN)
            return carry

        lax.fori_loop(0, nchunk, chunk, 0)

    def col(off):
        return pl.BlockSpec((cb, WH), lambda i, o=off: (i, o))

    in_specs, out_specs, out_shape, scratch, extra = _with_rider(
        rider, [col(0), col(1), col(2), pl.BlockSpec((2, WH), lambda i: (0, 0))],
        [pl.BlockSpec((cb, WH), lambda i: (i, 0)), pl.BlockSpec((nchunk, NH, HD, HD), lambda i: (i, 0, 0, 0))],
        [jax.ShapeDtypeStruct((T, WH), F32), jax.ShapeDtypeStruct((T // CHUNK, NH, HD, HD), F32)],
        [pltpu.VMEM((NH, HD, HD), F32)])
    return pl.pallas_call(
        body, name="hgrn_fwd", grid=(T // cb,), in_specs=in_specs, out_specs=out_specs, out_shape=out_shape,
        scratch_shapes=scratch, compiler_params=_params(("arbitrary",), has_side_effects=rider is not None),
    )(proj, proj, proj, hgrn_lb, *extra)


def _hgrn_bwd(proj, hgrn_lb, states, do, rider=None):
    T = proj.shape[0]
    cb = _tile(T, (512, 256, 128, 64))
    nchunk = cb // CHUNK
    nb = T // cb

    def body(*refs):
        (hq_ref, hf_ref, hi_ref, lb_ref, st_ref, do_ref, dhq_ref, dhf_ref, dhi_ref, dlb_ref,
         dstate) = _carry(rider, refs, 6, 4, 1, pl.program_id(0), nb)

        @pl.when(pl.program_id(0) == 0)
        def _():
            dstate[...] = jnp.zeros_like(dstate)
            dlb_ref[...] = jnp.zeros_like(dlb_ref)

        lb = _lower_bound(lb_ref)
        up = _tri(CHUNK, True)
        last = lax.broadcasted_iota(jnp.int32, (CHUNK, HD), 0) == CHUNK - 1

        def chunk(cc, carry):
            c = nchunk - 1 - cc
            r0 = pl.multiple_of(c * CHUNK, CHUNK)
            for h in range(NH):
                cols = slice(h * HD, (h + 1) * HD)
                hq = hq_ref[pl.ds(r0, CHUNK), cols]
                hf = hf_ref[pl.ds(r0, CHUNK), cols]
                (q, k, gl), prep_vjp = jax.vjp(_hgrn_prep, hq, hf, lb[:, cols])
                v = hi_ref[pl.ds(r0, CHUNK), cols]
                d_o = do_ref[pl.ds(r0, CHUNK), cols]
                low, eb, ea, ebn, ek, ebl = _hgrn_factors(q, k, gl)
                s_t = st_ref[c, h]
                ds_new = dstate[h]
                qe, am, bm, kb = q * eb, q * ea, k * ebn, k * ek
                pm_t = jnp.where(up, _dot(bm, am, NT), 0.0)
                dp = jnp.where(low, _dot(d_o, v, NT), 0.0)
                dp_t = jnp.where(up, _dot(v, d_o, NT), 0.0)
                dqe = _dot(d_o, s_t, NN)
                da = _dot(dp, bm, NN)
                db_m = _dot(dp_t, am, NN)
                dkb = _dot(v, ds_new, NN)
                dv = _dot(pm_t, d_o, NN) + _dot(kb, ds_new, NT)
                dq = dqe * eb + da * ea
                dk = db_m * ebn + dkb * ek
                dbl = jnp.sum(dkb * kb, axis=0, keepdims=True) + jnp.sum(ds_new * s_t, axis=0, keepdims=True) * ebl
                db = (dqe * qe + da * am.astype(BF16).astype(F32) - db_m * bm.astype(BF16).astype(F32) - dkb * kb
                      + jnp.where(last, dbl, 0.0))
                dgl = lax.dot_general(up.astype(F32), db, NN, precision=HIGHEST, preferred_element_type=F32)
                dhq, dhf, dlb = prep_vjp((dq, dk, dgl))
                dhq_ref[pl.ds(r0, CHUNK), cols] = dhq.astype(dhq_ref.dtype)
                dhf_ref[pl.ds(r0, CHUNK), cols] = dhf.astype(dhf_ref.dtype)
                dhi_ref[pl.ds(r0, CHUNK), cols] = dv.astype(dhi_ref.dtype)
                dlb_ref[:, cols] += dlb
                dstate[h] = _dot(d_o, qe, TN) + ds_new * ebl
            return carry

        lax.fori_loop(0, nchunk, chunk, 0)

    def col(off):
        return pl.BlockSpec((cb, WH), lambda i, o=off: (nb - 1 - i, o))

    row = pl.BlockSpec((cb, WH), lambda i: (nb - 1 - i, 0))
    in_specs, out_specs, out_shape, scratch, extra = _with_rider(
        rider, [col(0), col(1), col(2), pl.BlockSpec((2, WH), lambda i: (0, 0)),
                pl.BlockSpec((nchunk, NH, HD, HD), lambda i: (nb - 1 - i, 0, 0, 0)), row],
        [row, row, row, pl.BlockSpec((1, WH), lambda i: (0, 0))],
        [jax.ShapeDtypeStruct((T, WH), BF16)] * 3 + [jax.ShapeDtypeStruct((1, WH), F32)], [pltpu.VMEM((NH, HD, HD), F32)])
    return pl.pallas_call(
        body, name="hgrn_bwd", grid=(nb,), in_specs=in_specs, out_specs=out_specs, out_shape=out_shape,
        scratch_shapes=scratch, compiler_params=_params(("arbitrary",), has_side_effects=rider is not None),
    )(proj, proj, proj, hgrn_lb, states, do, *extra)


def _log_sigmoid(z):
    return jnp.minimum(z, 0.0) - jnp.log(1.0 + jnp.exp(-jnp.abs(z)))


def _fox_cum(proj, fb_pad):
    T = proj.shape[0]
    tb = _tile(T, (256, 128))

    def body(ff_ref, fb_ref, ct_ref, cq_ref, carry):
        @pl.when(pl.program_id(0) == 0)
        def _():
            carry[...] = jnp.zeros_like(carry)

        lf = _log_sigmoid(ff_ref[...] + fb_ref[...])
        cs = lax.dot_general(_tri(tb, False).astype(F32), lf, NN, precision=HIGHEST,
                             preferred_element_type=F32) + carry[0:1, :]
        carry[0:1, :] = cs[tb - 1:tb, :]
        ct_ref[...] = cs.T[0:8, :]
        for h in range(NH):
            cq_ref[h] = jnp.broadcast_to(cs[:, h:h + 1], (tb, HD))

    return pl.pallas_call(
        body, name="fox_cum", grid=(T // tb,),
        in_specs=[pl.BlockSpec((tb, HD), lambda i: (i, CB_FF)), pl.BlockSpec((1, HD), lambda i: (0, 0))],
        out_specs=[pl.BlockSpec((8, tb), lambda i: (0, i)), pl.BlockSpec((NH, tb, HD), lambda i: (0, i, 0))],
        out_shape=[jax.ShapeDtypeStruct((8, T), F32), jax.ShapeDtypeStruct((NH, T, HD), F32)],
        scratch_shapes=[pltpu.VMEM((8, HD), F32)],
        compiler_params=_params(("arbitrary",)),
    )(proj, fb_pad)


def _fox_cum_bwd(dc, proj, fb_pad):
    T = proj.shape[0]
    tb = _tile(T, (256, 128))
    nb = T // tb

    def body(dc_ref, ff_ref, fb_ref, dff_ref, dfb_ref, carry):
        @pl.when(pl.program_id(0) == 0)
        def _():
            carry[...] = jnp.zeros_like(carry)
            dfb_ref[...] = jnp.zeros_like(dfb_ref)

        rid = lax.broadcasted_iota(jnp.int32, (8, tb), 0)
        m8 = jnp.zeros((8, tb), F32)
        for h in range(NH):
            m8 = m8 + jnp.where(rid == h, dc_ref[h], 0.0)
        dcb = jnp.concatenate([m8, jnp.zeros((HD - 8, tb), F32)], axis=0).T
        rev = lax.dot_general(_tri(tb, True).astype(F32), dcb, NN, precision=HIGHEST,
                              preferred_element_type=F32) + carry[0:1, :]
        carry[0:1, :] = rev[0:1, :]
        dff = rev * jax.nn.sigmoid(-(ff_ref[...] + fb_ref[...]))
        dff_ref[...] = dff.astype(dff_ref.dtype)
        dfb_ref[...] += jnp.sum(dff, axis=0, keepdims=True)

    return pl.pallas_call(
        body, name="fox_cum_bwd", grid=(nb,),
        in_specs=[pl.BlockSpec((NH, 8, tb), lambda i: (0, 0, nb - 1 - i)),
                  pl.BlockSpec((tb, HD), lambda i: (nb - 1 - i, CB_FF)), pl.BlockSpec((1, HD), lambda i: (0, 0))],
        out_specs=[pl.BlockSpec((tb, HD), lambda i: (nb - 1 - i, 0)), pl.BlockSpec((1, HD), lambda i: (0, 0))],
        out_shape=[jax.ShapeDtypeStruct((T, HD), BF16), jax.ShapeDtypeStruct((1, HD), F32)],
        scratch_shapes=[pltpu.VMEM((8, HD), F32)],
        compiler_params=_params(("arbitrary",)),
    )(dc, proj, fb_pad)


STRIP = 128


def _fox_scores(q, k, cq, ck, i, j, bq, bk, r0=0):
    rows = q.shape[0]
    s = _dot(q, k, NT) * SCALE + (cq - ck)
    diff = lax.broadcasted_iota(jnp.int32, (rows, bk), 1) - lax.broadcasted_iota(jnp.int32, (rows, bk), 0)
    return jnp.where(diff <= i * bq + r0 - j * bk, s, NEG)


def _heads(h):
    return slice(h * HD, (h + 1) * HD)


UNDERFLOW = -120.0


def _fox_windows(proj, cq):
    T = proj.shape[0]
    bq = _tile(T, (512, 256, 128))
    nq = T // bq
    assert nq <= HD

    def body(q_ref, k_ref, cq_ref, jlo_ref, ihi_ref, norm_s, cs_s, ce_s):
        i = pl.program_id(0)

        @pl.when(i == 0)
        def _():
            norm_s[...] = jnp.zeros_like(norm_s)
            cs_s[...] = jnp.zeros_like(cs_s)
            ce_s[...] = jnp.zeros_like(ce_s)

        lane = lax.broadcasted_iota(jnp.int32, (1, HD), 1)
        for h in range(NH):
            for row, ref in ((h, q_ref), (8 + h, k_ref)):
                x = ref[:, _heads(h)]
                biggest = jnp.max(jnp.sum(x * x, axis=1, keepdims=True), axis=0, keepdims=True)
                norm_s[row:row + 1, :] = jnp.maximum(norm_s[row:row + 1, :], jnp.broadcast_to(biggest, (1, HD)))
            cs_s[h, pl.ds(i, 1), :] = cq_ref[h, 0:1, :]
            ce_s[h:h + 1, :] = jnp.where(lane == i, cq_ref[h, bq - 1:bq, :], ce_s[h:h + 1, :])

        @pl.when(i == nq - 1)
        def _():
            rows = lax.broadcasted_iota(jnp.int32, (HD, HD), 0)
            cols = lax.broadcasted_iota(jnp.int32, (HD, HD), 1)
            need = cols == rows
            for h in range(NH):
                slack = 2.05 * SCALE * jnp.sqrt(norm_s[h:h + 1, :] * norm_s[8 + h:9 + h, :])
                bound = cs_s[h] - ce_s[h:h + 1, :] + slack
                need = need | ((bound >= UNDERFLOW) & (cols < rows))
            need = need & (rows < nq) & (cols < nq)
            jlo = jnp.min(jnp.where(need, cols, HD).astype(F32), axis=1, keepdims=True)
            ihi = jnp.max(jnp.where(need, rows, -1).astype(F32), axis=0, keepdims=True)
            jlo_ref[...] = jnp.broadcast_to(jlo, (HD, HD)).astype(jnp.int32)
            ihi_ref[...] = jnp.broadcast_to(ihi, (8, HD)).astype(jnp.int32)

    jlo, ihi = pl.pallas_call(
        body, name="fox_windows", grid=(nq,),
        in_specs=[pl.BlockSpec((bq, WH), lambda i: (i, CB_FQ // NH)), pl.BlockSpec((bq, WH), lambda i: (i, CB_FK // NH)),
                  pl.BlockSpec((NH, bq, HD), lambda i: (0, i, 0))],
        out_specs=[pl.BlockSpec((HD, HD), lambda i: (0, 0)), pl.BlockSpec((8, HD), lambda i: (0, 0))],
        out_shape=[jax.ShapeDtypeStruct((HD, HD), jnp.int32), jax.ShapeDtypeStruct((8, HD), jnp.int32)],
        scratch_shapes=[pltpu.VMEM((16, HD), F32), pltpu.VMEM((NH, HD, HD), F32), pltpu.VMEM((8, HD), F32)],
        compiler_params=_params(("arbitrary",)),
    )(proj, proj, cq)
    return jnp.concatenate([jlo[:nq, 0], ihi[0, :nq]])


def _fox_fwd(win, proj, ct, cq):
    T = proj.shape[0]
    bq = bk = _tile(T, (512, 256, 128))
    nq = nk = T // bq

    def body(win_ref, q_ref, k_ref, v_ref, ct_ref, cq_ref, o_ref, lse_ref, m_s, l_s, acc_s):
        i, jj = pl.program_id(0), pl.program_id(1)
        j = win_ref[i] + jj

        @pl.when(jj == 0)
        def _():
            m_s[...] = jnp.full_like(m_s, NEG)
            l_s[...] = jnp.zeros_like(l_s)
            acc_s[...] = jnp.zeros_like(acc_s)

        @pl.when(j <= i)
        def _():
            for h in range(NH):
                hs = _heads(h)
                k, v, ck = k_ref[:, hs], v_ref[:, hs], ct_ref[h:h + 1, :]
                for r0 in range(0, bq, STRIP):
                    rs = slice(r0, r0 + STRIP)
                    s = _fox_scores(q_ref[rs, hs], k, cq_ref[h, rs, 0:1], ck, i, j, bq, bk, r0)
                    m_prev = m_s[h, rs]
                    m_new = jnp.maximum(m_prev, jnp.max(s, axis=1, keepdims=True))
                    alpha = jnp.exp(m_prev - m_new)
                    p = jnp.exp(s - m_new)
                    l_s[h, rs] = alpha * l_s[h, rs] + jnp.sum(p, axis=1, keepdims=True)
                    acc_s[rs, hs] = alpha * acc_s[rs, hs] + _dot(p, v, NN)
                    m_s[h, rs] = m_new

        @pl.when(jj == nk - 1)
        def _():
            for h in range(NH):
                o_ref[:, _heads(h)] = acc_s[:, _heads(h)] / l_s[h]
                lse_ref[h] = jnp.broadcast_to(m_s[h] + jnp.log(l_s[h]), (bq, HD))

    def key_block(i, jj, win):
        return jnp.minimum(win[i] + jj, i)

    def kv(off):
        return pl.BlockSpec((bk, WH), lambda i, jj, win, o=off // NH: (key_block(i, jj, win), o))

    stat = pl.BlockSpec((NH, bq, HD), lambda i, jj, win: (0, i, 0))
    return pl.pallas_call(
        body, name="fox_fwd",
        grid_spec=pltpu.PrefetchScalarGridSpec(
            num_scalar_prefetch=1, grid=(nq, nk),
            in_specs=[pl.BlockSpec((bq, WH), lambda i, jj, win: (i, CB_FQ // NH)), kv(CB_FK), kv(CB_FV),
                      pl.BlockSpec((8, bk), lambda i, jj, win: (0, key_block(i, jj, win))), stat],
            out_specs=[pl.BlockSpec((bq, WH), lambda i, jj, win: (i, 0)), stat],
            scratch_shapes=[pltpu.VMEM((NH, bq, 1), F32), pltpu.VMEM((NH, bq, 1), F32), pltpu.VMEM((bq, WH), F32)]),
        out_shape=[jax.ShapeDtypeStruct((T, WH), F32), jax.ShapeDtypeStruct((NH, T, HD), F32)],
        compiler_params=_params(("parallel", "arbitrary")),
    )(win, proj, proj, proj, ct, cq)


def _fox_bwd_dq(win, proj, ct, cq, lse, do):
    T = proj.shape[0]
    bq = bk = _tile(T, (512, 256, 128))
    nq = nk = T // bq

    def body(win_ref, q_ref, k_ref, v_ref, ct_ref, cq_ref, lse_ref, do_ref, dq_ref, delta_ref, acc_s, delta_s, psum_s):
        i, jj = pl.program_id(0), pl.program_id(1)
        j = win_ref[i] + jj % nk

        @pl.when(jj == 0)
        def _():
            acc_s[...] = jnp.zeros_like(acc_s)
            delta_s[...] = jnp.zeros_like(delta_s)
            psum_s[...] = jnp.zeros_like(psum_s)

        def probs(h):
            hs = _heads(h)
            k = k_ref[:, hs]
            s = _fox_scores(q_ref[:, hs], k, cq_ref[h, :, 0:1], ct_ref[h:h + 1, :], i, j, bq, bk)
            return k, jnp.exp(s - lse_ref[h, :, 0:1]), _dot(do_ref[:, hs], v_ref[:, hs], NT)

        @pl.when((j <= i) & (jj < nk))
        def _():
            for h in range(NH):
                _, p, dp = probs(h)
                delta_s[h] += jnp.sum(p * dp, axis=1, keepdims=True)
                psum_s[h] += jnp.sum(p, axis=1, keepdims=True)

        @pl.when((j <= i) & (jj >= nk))
        def _():
            for h in range(NH):
                k, p, dp = probs(h)
                ds = p * (dp - delta_s[h] / psum_s[h])
                acc_s[:, _heads(h)] += _dot(ds, k, NN) * SCALE

        @pl.when(jj == 2 * nk - 1)
        def _():
            dq_ref[...] = acc_s[...].astype(dq_ref.dtype)
            for h in range(NH):
                delta_ref[h] = jnp.broadcast_to(delta_s[h] / psum_s[h], (bq, HD))

    def key_block(i, jj, win):
        return jnp.minimum(win[i] + jj % nk, i)

    def kv(off):
        return pl.BlockSpec((bk, WH), lambda i, jj, win, o=off // NH: (key_block(i, jj, win), o))

    qrow = pl.BlockSpec((bq, WH), lambda i, jj, win: (i, 0))
    stat = pl.BlockSpec((NH, bq, HD), lambda i, jj, win: (0, i, 0))
    return pl.pallas_call(
        body, name="fox_bwd_dq",
        grid_spec=pltpu.PrefetchScalarGridSpec(
            num_scalar_prefetch=1, grid=(nq, 2 * nk),
            in_specs=[pl.BlockSpec((bq, WH), lambda i, jj, win: (i, CB_FQ // NH)), kv(CB_FK), kv(CB_FV),
                      pl.BlockSpec((8, bk), lambda i, jj, win: (0, key_block(i, jj, win))), stat, stat, qrow],
            out_specs=[qrow, stat],
            scratch_shapes=[pltpu.VMEM((bq, WH), F32), pltpu.VMEM((NH, bq, 1), F32), pltpu.VMEM((NH, bq, 1), F32)]),
        out_shape=[jax.ShapeDtypeStruct((T, WH), BF16), jax.ShapeDtypeStruct((NH, T, HD), F32)],
        compiler_params=_params(("parallel", "arbitrary")),
    )(win, proj, proj, proj, ct, cq, lse, do)


def _fox_bwd_dkv(win, proj, ct, cq, lse, delta, do):
    T = proj.shape[0]
    bq = bk = _tile(T, (512, 256, 128))
    nq = nk = T // bq

    def body(win_ref, q_ref, k_ref, v_ref, ct_ref, cq_ref, lse_ref, delta_ref, do_ref, dk_ref, dv_ref, dc_ref,
             dk_s, dv_s, dc_s):
        j, ii = pl.program_id(0), pl.program_id(1)
        i = j + ii

        @pl.when(ii == 0)
        def _():
            dk_s[...] = jnp.zeros_like(dk_s)
            dv_s[...] = jnp.zeros_like(dv_s)
            dc_s[...] = jnp.zeros_like(dc_s)

        @pl.when(i <= win_ref[nq + j])
        def _():
            for h in range(NH):
                hs = _heads(h)
                q = q_ref[:, hs]
                d_o = do_ref[:, hs]
                s = _fox_scores(q, k_ref[:, hs], cq_ref[h, :, 0:1], ct_ref[h:h + 1, :], i, j, bq, bk)
                p = jnp.exp(s - lse_ref[h, :, 0:1])
                dv_s[:, hs] += _dot(p, d_o, TN)
                dp = _dot(d_o, v_ref[:, hs], NT)
                ds = p * (dp - delta_ref[h, :, 0:1])
                dk_s[:, hs] += _dot(ds, q, TN) * SCALE
                dc_s[h:h + 1, :] -= jnp.sum(ds, axis=0, keepdims=True)

        @pl.when(ii == nq - 1)
        def _():
            dk_ref[...] = dk_s[...].astype(dk_ref.dtype)
            dv_ref[...] = dv_s[...].astype(dv_ref.dtype)
            for h in range(NH):
                dc_ref[h] = jnp.broadcast_to(dc_s[h:h + 1, :], (8, bk))

    def query_block(j, ii, win):
        return jnp.minimum(j + ii, win[nq + j])

    def kv(off):
        return pl.BlockSpec((bk, WH), lambda j, ii, win, o=off // NH: (j, o))

    qrow = pl.BlockSpec((bq, WH), lambda j, ii, win: (query_block(j, ii, win), 0))
    stat = pl.BlockSpec((NH, bq, HD), lambda j, ii, win: (0, query_block(j, ii, win), 0))
    krow = pl.BlockSpec((bk, WH), lambda j, ii, win: (j, 0))
    return pl.pallas_call(
        body, name="fox_bwd_dkv",
        grid_spec=pltpu.PrefetchScalarGridSpec(
            num_scalar_prefetch=1, grid=(nk, nq),
            in_specs=[pl.BlockSpec((bq, WH), lambda j, ii, win: (query_block(j, ii, win), CB_FQ // NH)), kv(CB_FK),
                      kv(CB_FV), pl.BlockSpec((8, bk), lambda j, ii, win: (0, j)), stat, stat, stat, qrow],
            out_specs=[krow, krow, pl.BlockSpec((NH, 8, bk), lambda j, ii, win: (0, 0, j))],
            scratch_shapes=[pltpu.VMEM((bk, WH), F32), pltpu.VMEM((bk, WH), F32), pltpu.VMEM((8, bk), F32)]),
        out_shape=[jax.ShapeDtypeStruct((T, WH), BF16), jax.ShapeDtypeStruct((T, WH), BF16),
                   jax.ShapeDtypeStruct((NH, 8, T), F32)],
        compiler_params=_params(("parallel", "arbitrary")),
    )(win, proj, proj, proj, ct, cq, lse, delta, do)


def _mem_probs(q, mk):
    s = _dot(q, mk, NT) * SCALE
    e = jnp.exp(s - jnp.max(s, axis=1, keepdims=True))
    return e / jnp.sum(e, axis=1, keepdims=True)


def _mem_fwd(proj, mem_kv):
    T = proj.shape[0]
    tr = _tile(T, (512, 256, 128))
    M = mem_kv.shape[0]

    def body(q_ref, mk_ref, mv_ref, o_ref):
        o_ref[...] = _dot(_mem_probs(q_ref[...], mk_ref[...]), mv_ref[...], NN)

    return pl.pallas_call(
        body, name="mem_fwd", grid=(NM, T // tr),
        in_specs=[pl.BlockSpec((tr, HD), lambda h, i: (i, CB_MQ + h)),
                  pl.BlockSpec((M, HD), lambda h, i: (0, h)), pl.BlockSpec((M, HD), lambda h, i: (0, NM + h))],
        out_specs=pl.BlockSpec((tr, HD), lambda h, i: (i, h)),
        out_shape=jax.ShapeDtypeStruct((T, WM), F32),
        compiler_params=_params(("parallel", "parallel")),
    )(proj, mem_kv, mem_kv)


def _mem_bwd(proj, mem_kv, do):
    T = proj.shape[0]
    tr = _tile(T, (512, 256, 128))
    M = mem_kv.shape[0]

    def body(q_ref, mk_ref, mv_ref, do_ref, dq_ref, dmk_ref, dmv_ref):
        @pl.when(pl.program_id(1) == 0)
        def _():
            dmk_ref[...] = jnp.zeros_like(dmk_ref)
            dmv_ref[...] = jnp.zeros_like(dmv_ref)

        q, mk, d_o = q_ref[...], mk_ref[...], do_ref[...]
        p = _mem_probs(q, mk)
        dmv_ref[...] += _dot(p, d_o, TN)
        dp = _dot(d_o, mv_ref[...], NT)
        ds = p * (dp - jnp.sum(p * dp, axis=1, keepdims=True))
        dq_ref[...] = (_dot(ds, mk, NN) * SCALE).astype(dq_ref.dtype)
        dmk_ref[...] += _dot(ds, q, TN) * SCALE

    acc = pl.BlockSpec((M, HD), lambda h, i: (0, h))
    row = pl.BlockSpec((tr, HD), lambda h, i: (i, h))
    return pl.pallas_call(
        body, name="mem_bwd", grid=(NM, T // tr),
        in_specs=[pl.BlockSpec((tr, HD), lambda h, i: (i, CB_MQ + h)),
                  pl.BlockSpec((M, HD), lambda h, i: (0, h)), pl.BlockSpec((M, HD), lambda h, i: (0, NM + h)), row],
        out_specs=[row, acc, acc],
        out_shape=[jax.ShapeDtypeStruct((T, WM), BF16), jax.ShapeDtypeStruct((M, WM), F32),
                   jax.ShapeDtypeStruct((M, WM), F32)],
        compiler_params=_params(("parallel", "arbitrary")),
    )(proj, mem_kv, mem_kv, do)


def _mesh_place():
    x, y, c = lax.axis_index("x"), lax.axis_index("y"), lax.axis_index("c")
    return x, y, c


CHIP_FLIPS = (4, 2, 6)
CHIP_OF_SLOT = (0,) + CHIP_FLIPS


def _peer(x, y, c, k):
    px = 1 - x if k & 4 else x
    py = 1 - y if k & 2 else y
    pc = 1 - c if k & 1 else c
    return (px, py, pc), 4 * px + 2 * py + pc


class _Gather:
    def __init__(self, shapes, pad_rows):
        self.shapes, self.pad_rows, self.n = shapes, pad_rows, len(shapes)
        self.npad = sum(1 for p in pad_rows if p)

    def zeros(self):
        return jnp.zeros((max(self.pad_rows) or 16, self.shapes[0][1]), BF16)

    def out_shape(self):
        return [jax.ShapeDtypeStruct((NDEV * r + p, c), BF16) for (r, c), p in zip(self.shapes, self.pad_rows)]

    def sems(self):
        return [pltpu.SemaphoreType.DMA((self.n, NDEV - 1)), pltpu.SemaphoreType.DMA((self.n, NDEV - 1)),
                pltpu.SemaphoreType.DMA((self.n + self.npad,))]

    def _copies(self, ins, z_ref, outs, send_sems, recv_sems, loc_sems):
        x, y, c = _mesh_place()
        me = 4 * x + 2 * y + c
        sibling, _ = _peer(x, y, c, 1)
        local, first, arrive, forward = [], [], [], []
        ip = 0
        for w in range(self.n):
            r = ins[w].shape[0]
            dst = outs[w].at[pl.ds(pl.multiple_of(me * r, 16), r), :]
            local.append(functools.partial(pltpu.make_async_copy, ins[w], dst, loc_sems.at[w]))
            if self.pad_rows[w]:
                local.append(functools.partial(pltpu.make_async_copy, z_ref.at[pl.ds(0, self.pad_rows[w]), :],
                                               outs[w].at[pl.ds(NDEV * r, self.pad_rows[w]), :], loc_sems.at[self.n + ip]))
                ip += 1

            def remote(src, dst_, s, to):
                return functools.partial(pltpu.make_async_remote_copy, src_ref=src, dst_ref=dst_, send_sem=send_sems.at[w, s],
                                         recv_sem=recv_sems.at[w, s], device_id=to, device_id_type=MESH)

            for s, k in enumerate((1,) + CHIP_FLIPS):
                first.append(remote(ins[w], dst, s, _peer(x, y, c, k)[0]))
            for s, k in enumerate(CHIP_FLIPS):
                _, pidx = _peer(x, y, c, k)
                rows = outs[w].at[pl.ds(pl.multiple_of(pidx * r, 16), r), :]
                arrive.append(remote(rows, rows, 1 + s, sibling))
                forward.append(remote(rows, rows, 4 + s, sibling))
        return local, first, arrive, forward


    def start(self, *refs):
        local, first, _, _ = self._copies(*refs)
        for make in local + first:
            make().start()

    def forward(self, *refs):
        _, _, arrive, forward = self._copies(*refs)
        for a, f in zip(arrive, forward):
            a().wait_recv()
            f().start()

    def finish(self, *refs):
        local, first, _, forward = self._copies(*refs)
        for make in local + first[0::4] + forward:
            make().wait()
        for s in (1, 2, 3):
            for make in first[s::4]:
                make().wait_send()


def _all_gather(shards, pad_rows):
    n = len(shards)
    plan = _Gather([s.shape for s in shards], pad_rows)

    def body(*refs):
        args = (refs[:n], refs[n], refs[n + 1:2 * n + 1]) + tuple(refs[2 * n + 1:])
        plan.start(*args)
        plan.forward(*args)
        plan.finish(*args)

    any_spec = pl.BlockSpec(memory_space=pl.ANY)
    return pl.pallas_call(
        body, name="all_gather_weights",
        in_specs=[any_spec] * (n + 1), out_specs=[any_spec] * n,
        out_shape=plan.out_shape(),
        scratch_shapes=plan.sems(),
        compiler_params=pltpu.CompilerParams(has_side_effects=True),
    )(*shards, plan.zeros())


def _exchange_in_chip(grads, shard_rows, name):
    n = len(grads)
    ns = len(CHIP_OF_SLOT)

    def body(*refs):
        ins, theirs = refs[:n], refs[n:2 * n]
        send_sems, recv_sems = refs[2 * n:]
        x, y, c = _mesh_place()
        sibling, _ = _peer(x, y, c, 1)
        copies = []
        for w in range(n):
            r = shard_rows[w]
            for s, k in enumerate(CHIP_OF_SLOT):
                _, other = _peer(x, y, c, k | 1)
                cp = pltpu.make_async_remote_copy(
                    src_ref=ins[w].at[pl.ds(pl.multiple_of(other * r, 16), r), :], dst_ref=theirs[w].at[s],
                    send_sem=send_sems.at[w, s], recv_sem=recv_sems.at[w, s], device_id=sibling, device_id_type=MESH)
                cp.start()
                copies.append(cp)
        for cp in copies:
            cp.wait()

    any_spec = pl.BlockSpec(memory_space=pl.ANY)
    return pl.pallas_call(
        body, name=name,
        in_specs=[any_spec] * n, out_specs=[any_spec] * n,
        out_shape=[jax.ShapeDtypeStruct((ns, r, g.shape[1]), g.dtype) for g, r in zip(grads, shard_rows)],
        scratch_shapes=[pltpu.SemaphoreType.DMA((n, ns)), pltpu.SemaphoreType.DMA((n, ns))],
        compiler_params=pltpu.CompilerParams(has_side_effects=True),
    )(*grads)


def _pair_sum(grad, theirs, name):
    ns, r, c = theirs.shape
    tr = r if r * c <= 2 * 1024 * 1024 else _tile(r, (256, 128, 64, 32, 16))
    per_block = r // tr

    def body(a_ref, b_ref, o_ref):
        o_ref[...] = (a_ref[...].astype(F32) + b_ref[...].astype(F32)).astype(o_ref.dtype)

    def owner_rows(s, i):
        x, y, c_ = _mesh_place()
        fx, fy = s % 2, s // 2
        px, py = x + fx - 2 * x * fx, y + fy - 2 * y * fy
        return ((4 * px + 2 * py + c_) * per_block + i, 0)

    slot = pl.BlockSpec((None, tr, c), lambda s, i: (s, i, 0))
    return pl.pallas_call(
        body, name=name, grid=(ns, per_block),
        in_specs=[pl.BlockSpec((tr, c), owner_rows), slot], out_specs=slot,
        out_shape=jax.ShapeDtypeStruct((ns, r, c), theirs.dtype),
        compiler_params=_params(("parallel", "parallel")),
    )(grad, theirs)


def _exchange_between_chips(pairs, name):
    n = len(pairs)
    plan = _ChipExchange([p.shape for p in pairs])

    def body(*refs):
        args = (refs[:n], refs[n:2 * n]) + tuple(refs[2 * n:])
        plan.start(*args)
        plan.finish(*args)

    any_spec = pl.BlockSpec(memory_space=pl.ANY)
    return pl.pallas_call(
        body, name=name,
        in_specs=[any_spec] * n, out_specs=[any_spec] * n,
        out_shape=plan.out_shape(), scratch_shapes=plan.sems(),
        compiler_params=pltpu.CompilerParams(has_side_effects=True),
    )(*pairs)


class _ChipExchange:
    def __init__(self, shapes):
        self.shapes, self.n, self.ns = shapes, len(shapes), len(CHIP_OF_SLOT) - 1

    def out_shape(self):
        return [jax.ShapeDtypeStruct((self.ns,) + tuple(s[1:]), BF16) for s in self.shapes]

    def sems(self):
        return [pltpu.SemaphoreType.DMA((self.n, self.ns)), pltpu.SemaphoreType.DMA((self.n, self.ns))]

    def _copies(self, ins, outs, send_sems, recv_sems):
        x, y, c = _mesh_place()
        copies = []
        for w in range(self.n):
            for s, k in enumerate(CHIP_OF_SLOT[1:]):
                peer, _ = _peer(x, y, c, k)
                copies.append(pltpu.make_async_remote_copy(
                    src_ref=ins[w].at[s + 1], dst_ref=outs[w].at[s], send_sem=send_sems.at[w, s],
                    recv_sem=recv_sems.at[w, s], device_id=peer, device_id_type=MESH))
        return copies

    def start(self, *refs):
        for cp in self._copies(*refs):
            cp.start()

    def finish(self, *refs):
        for cp in self._copies(*refs):
            cp.wait()


def _sum_chips(pair, recv, name):
    _, r, c = recv.shape
    tr = _tile(r, (128, 64, 32, 16))

    def body(p_ref, x_ref, o_ref):
        acc = p_ref[...].astype(F32)
        for s in range(x_ref.shape[0]):
            acc = acc + x_ref[s].astype(F32)
        o_ref[...] = acc

    return pl.pallas_call(
        body, name=name, grid=(r // tr,),
        in_specs=[pl.BlockSpec((None, tr, c), lambda i: (0, i, 0)), pl.BlockSpec((recv.shape[0], tr, c), lambda i: (0, i, 0))],
        out_specs=pl.BlockSpec((tr, c), lambda i: (i, 0)),
        out_shape=jax.ShapeDtypeStruct((r, c), F32),
        compiler_params=_params(("parallel",)),
    )(pair, recv)


def _all_reduce_small(part):
    R, W = part.shape

    def body(x_ref, o_ref, buf, send_sems, recv_sems):
        x, y, c = _mesh_place()
        me = 4 * x + 2 * y + c
        buf[me] = x_ref[...]
        copies = []
        for k in range(1, NDEV):
            peer, _ = _peer(x, y, c, k)
            cp = pltpu.make_async_remote_copy(src_ref=x_ref, dst_ref=buf.at[me], send_sem=send_sems.at[k - 1],
                                              recv_sem=recv_sems.at[k - 1], device_id=peer, device_id_type=MESH)
            cp.start()
            copies.append(cp)
        for cp in copies:
            cp.wait()
        acc = buf[0]
        for d in range(1, NDEV):
            acc = acc + buf[d]
        o_ref[...] = acc

    vm = pl.BlockSpec(memory_space=pltpu.VMEM)
    return pl.pallas_call(
        body, name="all_reduce_small", in_specs=[vm], out_specs=vm,
        out_shape=jax.ShapeDtypeStruct((R, W), F32),
        scratch_shapes=[pltpu.VMEM((NDEV, R, W), F32), pltpu.SemaphoreType.DMA((NDEV - 1,)),
                        pltpu.SemaphoreType.DMA((NDEV - 1,))],
        compiler_params=pltpu.CompilerParams(has_side_effects=True),
    )(part)


def _adam_math(w, g, m, v):
    m2 = ADAM_B1 * m + (1.0 - ADAM_B1) * g
    v2 = ADAM_B2 * v + (1.0 - ADAM_B2) * (g * g)
    m_hat = m2 / (1.0 - ADAM_B1 ** ADAM_STEP)
    v_hat = v2 / (1.0 - ADAM_B2 ** ADAM_STEP)
    delta = -ADAM_LR * (m_hat / (jnp.sqrt(v_hat) + ADAM_EPS) + ADAM_WD * w)
    return delta, m2, v2


def _adamw(w, g, m, v, name):
    r, c = w.shape
    tr = r
    for cand in (1024, 512, 256, 128, 64, 32, 16, 8):
        if r % cand == 0 and cand * c <= 256 * 1024:
            tr = cand
            break

    def body(w_ref, g_ref, m_ref, v_ref, d_ref, m2_ref, v2_ref):
        d_ref[...], m2_ref[...], v2_ref[...] = _adam_math(w_ref[...], g_ref[...], m_ref[...], v_ref[...])

    spec = pl.BlockSpec((tr, c), lambda i: (i, 0))
    return pl.pallas_call(
        body, name=name, grid=(r // tr,), in_specs=[spec] * 4, out_specs=[spec] * 3,
        out_shape=[jax.ShapeDtypeStruct((r, c), F32)] * 3,
        compiler_params=_params(("parallel",)),
    )(w, g, m, v)


GAINS = ("ffn1_pre", "ffn1_post", "mix_pre", "mix_post", "mem_norm", "ffn2_pre", "ffn2_post")
GAIN_ROWS = D // HD
ROW_LB = len(GAINS) * GAIN_ROWS
ROWS_GRAD_IN = ROW_LB + 24
ROWS_PACKED = ROW_LB + 32


def _small_update(gsum, w_p, m_p, v_p):
    def body(g_ref, w_ref, m_ref, v_ref, go_ref, d_ref, m2_ref, v2_ref):
        a0 = w_ref[ROW_LB:ROW_LB + 8, :]
        a1 = w_ref[ROW_LB + 8:ROW_LB + 16, :]
        mx = jnp.maximum(a0, a1)
        e0, e1 = jnp.exp(a0 - mx), jnp.exp(a1 - mx)
        lb = e0 / (e0 + e1)
        da0 = g_ref[ROW_LB:ROW_LB + 8, :] * lb * (1.0 - lb)
        g = jnp.concatenate([g_ref[0:ROW_LB, :], da0, -da0, g_ref[ROW_LB + 8:ROWS_GRAD_IN, :]], axis=0)
        go_ref[...] = g
        d_ref[...], m2_ref[...], v2_ref[...] = _adam_math(w_ref[...], g, m_ref[...], v_ref[...])

    vm = pl.BlockSpec(memory_space=pltpu.VMEM)
    return pl.pallas_call(
        body, name="small_update", in_specs=[vm] * 4, out_specs=[vm] * 4,
        out_shape=[jax.ShapeDtypeStruct((ROWS_PACKED, HD), F32)] * 4,
    )(gsum, w_p, m_p, v_p)


def _rows8(a):
    a = a.reshape(-1)
    rows = -(-a.shape[0] // HD)
    rows8 = -(-rows // 8) * 8
    return jnp.pad(a, (0, rows8 * HD - a.shape[0])).reshape(rows8, HD)


def _pack_small(gains, lb0, lb1, gnorm, fb):
    return jnp.concatenate([_rows8(g) for g in gains] + [_rows8(lb0), _rows8(lb1), _rows8(gnorm), _rows8(fb)], axis=0)


def _unpack_small(p):
    out = {}
    for i, name in enumerate(GAINS):
        out[name] = p[i * GAIN_ROWS:(i + 1) * GAIN_ROWS].reshape(1, D)
    lb0 = p[ROW_LB:ROW_LB + NH].reshape(1, WH)
    lb1 = p[ROW_LB + 8:ROW_LB + 8 + NH].reshape(1, WH)
    out["hgrn_lb"] = jnp.concatenate([lb0, lb1], axis=0)
    out["hgrn_gnorm"] = p[ROW_LB + 16:ROW_LB + 16 + NH].reshape(1, WH)
    out["fox_fb"] = p[ROW_LB + 24:ROW_LB + 25, 0:NH]
    return out


def _ffn_forward(xin, pre, post, wg_t, wu_t, wd, tag, rider=None):
    T = xin.shape[0]
    tr = _tile(T, (256, 128))
    (n,) = _rowwise(_norm_fn, [(xin, 0)], [(pre, None)], [BF16], f"{tag}_pre", tr, D, 1)
    g, u, a, *carried = _ffn_up(n, wg_t, wu_t, f"{tag}_up", rider)
    if wd is None:
        wd = carried[0]
    h = _mm(a, wd, "nn", F32, f"{tag}_down")
    (xout,) = _rowwise(functools.partial(_resid_fn, 0.5), [(xin, 0), (h, 0)], [(post, None)], [F32], f"{tag}_post", tr, D, 1)
    return xout, (xin, n, g, u, a, h), carried


def _ffn_backward(dxout, saved, pre, post, wg_t, wu_t, wd, tag, rider=None, exchange=None):
    xin, n, g, u, a, h = saved
    T = xin.shape[0]
    tr = _tile(T, (256, 128))
    dh, dpost = _rowwise_bwd(functools.partial(_resid_h_fn, 0.5), [(h, 0)], [(post, None)], [(dxout, 0)], [0], [BF16],
                             f"{tag}_post_bwd", tr, D, 1)
    dwd = _mm(a, dh, "tn", BF16, f"{tag}_dwd")
    dg, du, *carried = _ffn_act_bwd(dh, wd, g, u, f"{tag}_act_bwd", rider)
    dwg = _mm(dg, n, "tn", BF16, f"{tag}_dwg")
    dwu = _mm(du, n, "tn", BF16, f"{tag}_dwu")
    if exchange is None:
        dn = _mm(dg, wg_t, "nn", F32, f"{tag}_dn_g")
        dn = _mm(du, wu_t, "nn", F32, f"{tag}_dn_u", add=dn)
    else:
        ride_a, ride_b, take = exchange(dwg, dwu, dwd)
        dn, *got_a = _mm(dg, wg_t, "nn", F32, f"{tag}_dn_g", rider=ride_a)
        dn, *got_b = _mm(du, wu_t, "nn", F32, f"{tag}_dn_u", add=dn, rider=ride_b)
        take(got_a, got_b)
    dxin, dpre = _rowwise_bwd(_norm_res_fn, [(xin, 0)], [(pre, None)], [(dxout, 0), (dn, 0)], [0], [F32],
                              f"{tag}_pre_bwd", tr, D, 1)
    return dxin, (dwg, dwu, dwd), dpre, dpost, carried


GATHER_FIRST = ("ffn1_wg", "ffn1_wu")
GATHER_IN_FFN1_UP = ("ffn1_wd", "w_in", "w_gate")
GATHER_IN_PROJ = ("w_mem_kv", "w_hgrn_out", "w_fox_out", "w_mem_out", "w_o")
GATHER_IN_GATE = ("ffn2_wg", "ffn2_wu")
GATHER_IN_HGRN = ("ffn2_wd",)
GROUP_FFN1 = ("ffn1_wg", "ffn1_wu", "ffn1_wd")
GROUP_MIX = ("w_in", "w_mem_kv", "w_hgrn_out", "w_fox_out", "w_mem_out", "w_gate", "w_o")
GROUP_FFN2 = ("ffn2_wg", "ffn2_wu", "ffn2_wd")


def _gather_rider(blocks, names):
    plan = _Gather([blocks[n].shape for n in names], [FFN_PAD.get(n, 0) for n in names])
    return _Rider(plan, [blocks[n] for n in names] + [plan.zeros()], GATHER_STEPS)


def _local_step(x, mem, tgt, small, wts=None, blocks=None):
    T = x.shape[0]
    tr = _tile(T, (256, 128))
    fb_pad = jnp.pad(small["fox_fb"], ((0, 0), (0, HD - NH)))
    dist = blocks is not None
    if dist:
        wts = dict(zip(GATHER_FIRST, _all_gather([blocks[n] for n in GATHER_FIRST], [FFN_PAD[n] for n in GATHER_FIRST])))

    def riding(names):
        return _gather_rider(blocks, names) if dist else None

    x1, ffn1_saved, carried = _ffn_forward(x, small["ffn1_pre"], small["ffn1_post"], wts["ffn1_wg"], wts["ffn1_wu"],
                                           wts.get("ffn1_wd"), "ffn1", riding(GATHER_IN_FFN1_UP))
    wts.update(zip(GATHER_IN_FFN1_UP, carried))
    (un,) = _rowwise(_norm_fn, [(x1, 0)], [(small["mix_pre"], None)], [BF16], "mix_pre", tr, D, 1)
    if dist:
        proj, *carried = _mm(un, wts["w_in"], "nn", F32, "proj", rider=riding(GATHER_IN_PROJ))
        wts.update(zip(GATHER_IN_PROJ, carried))
        z, *carried = _mm(un, wts["w_gate"], "nt", F32, "gate_logits", rider=riding(GATHER_IN_GATE))
        wts.update(zip(GATHER_IN_GATE, carried))
    else:
        proj = _mm(un, wts["w_in"], "nn", F32, "proj")
        z = _mm(un, wts["w_gate"], "nt", F32, "gate_logits")
    (memn,) = _rowwise(_norm_fn, [(mem, 0)], [(small["mem_norm"], None)], [BF16], "mem_norm", mem.shape[0], D, 1)
    mem_kv = _mm(memn, wts["w_mem_kv"], "nn", F32, "mem_kv")

    o_raw, states, *carried = _hgrn_fwd(proj, small["hgrn_lb"], riding(GATHER_IN_HGRN))
    wts.update(zip(GATHER_IN_HGRN, carried))
    (o_h,) = _rowwise(_hpost_fn, [(o_raw, 0), (proj, CB_HOG)], [(small["hgrn_gnorm"], 0)], [BF16], "hgrn_post",
                      tr, HD, NH)
    ct, cq = _fox_cum(proj, fb_pad)
    win = _fox_windows(proj, cq)
    o_f, lse = _fox_fwd(win, proj, ct, cq)
    o_m = _mem_fwd(proj, mem_kv)

    yh = _mm(o_h, wts["w_hgrn_out"], "nt", F32, "hgrn_out")
    yf = _mm(o_f, wts["w_fox_out"], "nt", F32, "fox_out")
    ym = _mm(o_m, wts["w_mem_out"], "nt", F32, "mem_out")
    zc = D // 512
    merge_rows = [(z, 0), (z, zc), (z, 2 * zc), (yh, 0), (yf, 0), (ym, 0)]
    (merged,) = _rowwise(_merge_fn, merge_rows, [], [BF16], "merge", tr, 512, zc)
    m = _mm(merged, wts["w_o"], "nn", F32, "mix_out")
    (x2,) = _rowwise(functools.partial(_resid_fn, 1.0), [(x1, 0), (m, 0)], [(small["mix_post"], None)], [F32], "mix_post",
                     tr, D, 1)
    x3, ffn2_saved, _ = _ffn_forward(x2, small["ffn2_pre"], small["ffn2_post"], wts["ffn2_wg"], wts["ffn2_wu"],
                                     wts["ffn2_wd"], "ffn2")
    dy, loss_part = _loss(x3, tgt, "loss")

    gw, gs, reduced = {}, {}, {}

    def pair_sums(names, tag):
        if not dist:
            return None, None
        theirs = _exchange_in_chip([gw[n] for n in names], [blocks[n].shape[0] for n in names], f"reduce_in_chip_{tag}")
        pairs = [_pair_sum(gw[n], t_, f"pair_{n}") for n, t_ in zip(names, theirs)]
        return pairs, _Rider(_ChipExchange([p.shape for p in pairs]), pairs, EXCHANGE_STEPS)

    def chip_sums(names, pairs, recv):
        for n, p_, r_ in zip(names, pairs or (), recv):
            reduced[n] = _sum_chips(p_, r_, f"sum_{n}")

    dx2, (gw["ffn2_wg"], gw["ffn2_wu"], gw["ffn2_wd"]), gs["ffn2_pre"], gs["ffn2_post"], _ = _ffn_backward(
        dy, ffn2_saved, small["ffn2_pre"], small["ffn2_post"], wts["ffn2_wg"], wts["ffn2_wu"], wts["ffn2_wd"], "ffn2")
    pairs_ffn2, ride_ffn2_grads = pair_sums(GROUP_FFN2, "ffn2")

    dm, gs["mix_post"] = _rowwise_bwd(functools.partial(_resid_h_fn, 1.0), [(m, 0)], [(small["mix_post"], None)],
                                      [(dx2, 0)], [0], [BF16], "mix_post_bwd", tr, D, 1)
    dmerged = _mm(dm, wts["w_o"], "nt", F32, "d_merged")
    gw["w_o"] = _mm(merged, dm, "tn", BF16, "d_w_o")
    dz0, dz1, dz2, dyh, dyf, dym = _rowwise_bwd(_merge_fn, merge_rows, [], [(dmerged, 0)], [0, 1, 2, 3, 4, 5], [BF16] * 6,
                                                "merge_bwd", tr, 512, zc)
    dz = jnp.concatenate([dz0, dz1, dz2], axis=1)
    gw["w_gate"] = _mm(dz, un, "tn", BF16, "d_w_gate")
    dun = _mm(dz, wts["w_gate"], "nn", F32, "d_un_gate")

    do_h = _mm(dyh, wts["w_hgrn_out"], "nn", F32, "d_o_h")
    gw["w_hgrn_out"] = _mm(dyh, o_h, "tn", BF16, "d_w_hgrn_out")
    do_f = _mm(dyf, wts["w_fox_out"], "nn", F32, "d_o_f")
    gw["w_fox_out"] = _mm(dyf, o_f, "tn", BF16, "d_w_fox_out")
    do_m = _mm(dym, wts["w_mem_out"], "nn", F32, "d_o_m")
    gw["w_mem_out"] = _mm(dym, o_m, "tn", BF16, "d_w_mem_out")

    do_raw, dhog, gs["hgrn_gnorm"] = _rowwise_bwd(_hpost_fn, [(o_raw, 0), (proj, CB_HOG)], [(small["hgrn_gnorm"], 0)],
                                                  [(do_h, 0)], [0, 1], [F32, BF16], "hgrn_post_bwd", tr, HD, NH)
    dhq, dhf, dhi, gs["hgrn_lb"], *carried = _hgrn_bwd(proj, small["hgrn_lb"], states, do_raw, ride_ffn2_grads)
    chip_sums(GROUP_FFN2, pairs_ffn2, carried)
    dfq, delta = _fox_bwd_dq(win, proj, ct, cq, lse, do_f)
    dfk, dfv, dc = _fox_bwd_dkv(win, proj, ct, cq, lse, delta, do_f)
    dff, dfb = _fox_cum_bwd(dc, proj, fb_pad)
    gs["fox_fb"] = dfb
    dmq, dmk, dmv = _mem_bwd(proj, mem_kv, do_m)

    dproj = jnp.concatenate([dhq, dhf, dhi, dhog, dfq, dfk, dfv, dff, dmq, jnp.zeros((T, HD), BF16)], axis=1)
    gw["w_in"] = _mm(un, dproj, "tn", BF16, "d_w_in")
    dun = _mm(dproj, wts["w_in"], "nt", F32, "d_un_proj", add=dun)
    dx1, gs["mix_pre"] = _rowwise_bwd(_norm_res_fn, [(x1, 0)], [(small["mix_pre"], None)], [(dx2, 0), (dun, 0)], [0], [F32],
                                      "mix_pre_bwd", tr, D, 1)

    dmem_kv = jnp.concatenate([dmk, dmv], axis=1)
    gw["w_mem_kv"] = _mm(memn, dmem_kv, "tn", BF16, "d_w_mem_kv")
    dmemn = _mm(dmem_kv, wts["w_mem_kv"], "nt", F32, "d_memn")
    _, gs["mem_norm"] = _rowwise_bwd(_norm_fn, [(mem, 0)], [(small["mem_norm"], None)], [(dmemn, 0)], [0], [BF16],
                                     "mem_norm_bwd", mem.shape[0], D, 1)

    pairs_mix, ride_mix_grads = pair_sums(GROUP_MIX, "mix")

    def own_exchange(dwg, dwu, dwd):
        gw.update(ffn1_wg=dwg, ffn1_wu=dwu, ffn1_wd=dwd)
        pairs, _ = pair_sums(GROUP_FFN1, "ffn1")
        first, second = pairs[:2], pairs[2:]

        def take(got_a, got_b):
            chip_sums(GROUP_FFN1, pairs, list(got_a) + list(got_b))

        return (_Rider(_ChipExchange([p.shape for p in first]), first, EXCHANGE_STEPS),
                _Rider(_ChipExchange([p.shape for p in second]), second, EXCHANGE_STEPS), take)

    dx, (gw["ffn1_wg"], gw["ffn1_wu"], gw["ffn1_wd"]), gs["ffn1_pre"], gs["ffn1_post"], carried = _ffn_backward(
        dx1, ffn1_saved, small["ffn1_pre"], small["ffn1_post"], wts["ffn1_wg"], wts["ffn1_wu"], wts["ffn1_wd"], "ffn1",
        ride_mix_grads, own_exchange if dist else None)
    chip_sums(GROUP_MIX, pairs_mix, carried)
    return loss_part, dx, (reduced if dist else gw), gs


BIG = ("ffn1_wg", "ffn1_wu", "ffn1_wd", "w_in", "w_mem_kv", "w_hgrn_out", "w_fox_out", "w_mem_out", "w_gate", "w_o",
       "ffn2_wg", "ffn2_wu", "ffn2_wd")
TRANSPOSED = ("ffn1_wg", "ffn1_wu", "ffn2_wg", "ffn2_wu", "w_hgrn_out", "w_fox_out", "w_mem_out", "w_gate")
FFN_PAD = {"ffn1_wg": FP - F, "ffn1_wu": FP - F, "ffn1_wd": FP - F, "ffn2_wg": FP - F, "ffn2_wu": FP - F,
           "ffn2_wd": FP - F}
SMALL = GAINS + ("hgrn_lb", "hgrn_gnorm", "fox_fb")
WEIGHTS = ("ffn1_pre", "ffn1_post", "ffn1_wg", "ffn1_wu", "ffn1_wd", "mix_pre", "mix_post", "mem_norm", "w_in", "hgrn_lb",
           "hgrn_gnorm", "fox_fb", "w_mem_kv", "w_hgrn_out", "w_fox_out", "w_mem_out", "w_gate", "w_o", "ffn2_pre",
           "ffn2_post", "ffn2_wg", "ffn2_wu", "ffn2_wd")


def _to_gather_layout(name, w):
    if name in TRANSPOSED:
        w = w.T
    if name == "w_in":
        r = w.shape[0]
        w = jnp.concatenate([w[:, :MQ_COL], jnp.zeros((r, FF_COL + HD - MQ_COL), w.dtype), w[:, MQ_COL:],
                             jnp.zeros((r, P - FF_COL - HD - WM), w.dtype)], axis=1)
    return w.astype(BF16)


def _from_gather_layout(name, g):
    if name == "w_in":
        g = jnp.concatenate([g[:, :MQ_COL], g[:, FF_COL + HD:FF_COL + HD + WM]], axis=1)
    if name in TRANSPOSED:
        g = g.T
    return g


def kernel(x, mem, ffn1_pre, ffn1_post, ffn1_wg, ffn1_wu, ffn1_wd, mix_pre, mix_post, mem_norm, w_in, hgrn_lb, hgrn_gnorm, fox_fb, w_mem_kv, w_hgrn_out, w_fox_out, w_mem_out, w_gate, w_o, ffn2_pre, ffn2_post, ffn2_wg, ffn2_wu, ffn2_wd, loss_target, m_ffn1_pre, m_ffn1_post, m_ffn1_wg, m_ffn1_wu, m_ffn1_wd, m_mix_pre, m_mix_post, m_mem_norm, m_w_in, m_hgrn_lb, m_hgrn_gnorm, m_fox_fb, m_w_mem_kv, m_w_hgrn_out, m_w_fox_out, m_w_mem_out, m_w_gate, m_w_o, m_ffn2_pre, m_ffn2_post, m_ffn2_wg, m_ffn2_wu, m_ffn2_wd, v_ffn1_pre, v_ffn1_post, v_ffn1_wg, v_ffn1_wu, v_ffn1_wd, v_mix_pre, v_mix_post, v_mem_norm, v_w_in, v_hgrn_lb, v_hgrn_gnorm, v_fox_fb, v_w_mem_kv, v_w_hgrn_out, v_w_fox_out, v_w_mem_out, v_w_gate, v_w_o, v_ffn2_pre, v_ffn2_post, v_ffn2_wg, v_ffn2_wu, v_ffn2_wd):
    a = dict(locals())
    small = {n: a[n] for n in SMALL}
    shard = {n: a[n][0] if a[n].ndim == 3 else a[n] for n in BIG}

    blocks = {n: _to_gather_layout(n, shard[n]) for n in BIG}
    loss_part, dx, reduced, gs = _local_step(x[0], mem[0], loss_target[0], small, blocks=blocks)
    loss = lax.psum(0.5 / D * jnp.sum(loss_part), ("x", "y", "c"))

    grads, deltas, new_m, new_v = {}, {}, {}, {}
    for n in BIG:
        g = _from_gather_layout(n, reduced[n])
        d, m2, v2 = _adamw(shard[n], g, a["m_" + n].reshape(g.shape), a["v_" + n].reshape(g.shape), f"adamw_{n}")
        full = a[n].shape
        grads[n], deltas[n], new_m[n], new_v[n] = g.reshape(full), d.reshape(full), m2.reshape(full), v2.reshape(full)

    part = jnp.concatenate([_rows8(gs[n]) for n in GAINS] + [_rows8(gs["hgrn_lb"]), _rows8(gs["hgrn_gnorm"]),
                                                             _rows8(gs["fox_fb"][:, :NH])], axis=0)
    gsum = _all_reduce_small(part)

    def packed(prefix):
        lb = a[prefix + "hgrn_lb"]
        return _pack_small([a[prefix + n] for n in GAINS], lb[0], lb[1], a[prefix + "hgrn_gnorm"], a[prefix + "fox_fb"])

    g_p, d_p, m_p, v_p = _small_update(gsum, packed(""), packed("m_"), packed("v_"))
    for dst, p in ((grads, g_p), (deltas, d_p), (new_m, m_p), (new_v, v_p)):
        dst.update(_unpack_small(p))

    return (loss, dx[None], *[grads[n] for n in WEIGHTS], *[deltas[n] for n in WEIGHTS],
            *[new_m[n] for n in WEIGHTS], *[new_v[n] for n in WEIGHTS])
```

```python
import functools

import jax
import jax.numpy as jnp
from jax import lax
from jax.experimental import pallas as pl
from jax.experimental.pallas import tpu as pltpu

F32 = jnp.float32
BF16 = jnp.bfloat16
HIGHEST = lax.Precision.HIGHEST

NDEV = 8
D = 2048
F = 5504
FP = 5632
HD = 128
NH = 6
NM = 4
WH = NH * HD
WM = NM * HD
P = 6144
FF_COL = 5376
MQ_COL = 5382
CHUNK = 64
EPS = 1e-6
SCALE = HD ** -0.5
NEG = -1e30
VMEM_LIMIT = 48 * 1024 * 1024

CB_HQ, CB_HF, CB_HI, CB_HOG, CB_FQ, CB_FK, CB_FV, CB_FF, CB_MQ = 0, 6, 12, 18, 24, 30, 36, 42, 43

ADAM_LR, ADAM_B1, ADAM_B2, ADAM_EPS, ADAM_WD, ADAM_STEP = 0.001, 0.9, 0.999, 1e-08, 0.01, 10

NT = (((1,), (1,)), ((), ()))
NN = (((1,), (0,)), ((), ()))
TN = (((0,), (0,)), ((), ()))
MESH = pl.DeviceIdType.MESH


def _params(sem=None, **kw):
    return pltpu.CompilerParams(dimension_semantics=sem, vmem_limit_bytes=VMEM_LIMIT, **kw)


def _tile(n, prefs):
    for p in prefs:
        if p <= n and n % p == 0:
            return p
    return n


def _dot(a, b, dims):
    return lax.dot_general(a.astype(BF16), b.astype(BF16), dims, preferred_element_type=F32)


def _mm(a, b, mode, out_dtype, name, add=None, rider=None):
    if mode == "nn":
        (M, K), (K2, N) = a.shape, b.shape
    elif mode == "nt":
        (M, K), (N, K2) = a.shape, b.shape
    else:
        (K, M), (K2, N) = a.shape, b.shape
    assert K == K2, (a.shape, b.shape, mode)
    if mode == "tn":
        tm = _tile(M, (512, 256, 128))
        tn = _tile(N, (1024, 768, 512, 256, 128))
        tk = _tile(K, (4096, 2048, 1024, 512, 256, 128))
    else:
        tm = _tile(M, (1024, 512, 256, 128)) if K <= 2048 else _tile(M, (512, 256, 128))
        tn = _tile(N, (512, 768, 256, 128))
        tk = K if K <= 6144 else _tile(K, (2048, 1024, 512, 256, 128))
    nk = K // tk
    dims = {"nn": NN, "nt": NT, "tn": TN}[mode]
    has_add = add is not None

    ni, nj = M // tm, N // tn
    n_in = 3 if has_add else 2

    def body(*refs):
        step = (pl.program_id(0) * nj + pl.program_id(1)) * nk + pl.program_id(2)
        refs = _carry(rider, refs, n_in, 1, 1, step, ni * nj * nk)
        a_ref, b_ref = refs[0], refs[1]
        c_ref = refs[2] if has_add else None
        o_ref = refs[3] if has_add else refs[2]
        acc_ref = refs[-1]
        k = pl.program_id(2)
        part = _dot(a_ref[...], b_ref[...], dims)

        def finish(r):
            if has_add:
                r = r + c_ref[...].astype(F32)
            o_ref[...] = r.astype(o_ref.dtype)

        if nk == 1:
            finish(part)
        else:
            @pl.when(k == 0)
            def _():
                acc_ref[...] = part

            @pl.when(k > 0)
            def _():
                acc_ref[...] += part

            @pl.when(k == nk - 1)
            def _():
                finish(acc_ref[...])

    if mode == "nn":
        a_spec = pl.BlockSpec((tm, tk), lambda i, j, k: (i, k))
        b_spec = pl.BlockSpec((tk, tn), lambda i, j, k: (k, j))
    elif mode == "nt":
        a_spec = pl.BlockSpec((tm, tk), lambda i, j, k: (i, k))
        b_spec = pl.BlockSpec((tn, tk), lambda i, j, k: (j, k))
    else:
        a_spec = pl.BlockSpec((tk, tm), lambda i, j, k: (k, i))
        b_spec = pl.BlockSpec((tk, tn), lambda i, j, k: (k, j))
    o_spec = pl.BlockSpec((tm, tn), lambda i, j, k: (i, j))
    args = (a, b) + ((add,) if has_add else ())
    in_specs, out_specs, out_shape, scratch, extra = _with_rider(
        rider, [a_spec, b_spec] + ([o_spec] if has_add else []), [o_spec], [jax.ShapeDtypeStruct((M, N), out_dtype)],
        [pltpu.VMEM((tm, tn) if nk > 1 else (8, 128), F32)])
    out = pl.pallas_call(
        body, name=name, grid=(ni, nj, nk), in_specs=in_specs, out_specs=out_specs, out_shape=out_shape,
        scratch_shapes=scratch,
        compiler_params=_params(("arbitrary",) * 3 if rider else ("parallel", "parallel", "arbitrary"),
                                has_side_effects=rider is not None),
    )(*args, *extra)
    return out if rider else out[0]


class _Rider:
    def __init__(self, plan, inputs, steps):
        self.plan, self.inputs, self.steps = plan, list(inputs), steps
        self.n_out = len(plan.out_shape())
        self.n_sem = len(plan.sems())

    def run(self, step, total, in_refs, out_refs, sem_refs):
        n = self.plan.n
        if isinstance(self.plan, _Gather):
            args = (in_refs[:n], in_refs[n], out_refs) + tuple(sem_refs)
        else:
            args = (in_refs, out_refs) + tuple(sem_refs)
        for frac, method in self.steps:
            @pl.when(step == int(frac * (total - 1)))
            def _(method=method):
                getattr(self.plan, method)(*args)


GATHER_STEPS = ((0.0, "start"), (0.6, "forward"), (1.0, "finish"))
EXCHANGE_STEPS = ((0.0, "start"), (1.0, "finish"))


def _carry(rider, refs, n_in, n_out, n_scratch, step, total):
    if rider is None:
        return refs
    ri, ro, rs = len(rider.inputs), rider.n_out, rider.n_sem
    own_in, rid_in = refs[:n_in], refs[n_in:n_in + ri]
    own_out, rid_out = refs[n_in + ri:n_in + ri + n_out], refs[n_in + ri + n_out:n_in + ri + n_out + ro]
    own_scr, rid_sem = refs[n_in + ri + n_out + ro:n_in + ri + n_out + ro + n_scratch], refs[len(refs) - rs:]
    rider.run(step, total, rid_in, rid_out, rid_sem)
    return tuple(own_in) + tuple(own_out) + tuple(own_scr)


def _with_rider(rider, in_specs, out_specs, out_shape, scratch):
    if rider is None:
        return in_specs, out_specs, out_shape, scratch, ()
    any_spec = pl.BlockSpec(memory_space=pl.ANY)
    return (list(in_specs) + [any_spec] * len(rider.inputs), list(out_specs) + [any_spec] * rider.n_out,
            list(out_shape) + rider.plan.out_shape(), list(scratch) + rider.plan.sems(), tuple(rider.inputs))


def _ffn_up(n, wg_t, wu_t, name, rider=None):
    T = n.shape[0]
    tm = _tile(T, (1024, 512, 256, 128))
    tn = 512
    ni, nj = T // tm, FP // tn

    def body(*refs):
        step = pl.program_id(0) * nj + pl.program_id(1)
        n_ref, wg_ref, wu_ref, g_ref, u_ref, a_ref = _carry(rider, refs, 3, 3, 0, step, ni * nj)
        x = n_ref[...]
        g = _dot(x, wg_ref[...], NT)
        u = _dot(x, wu_ref[...], NT)
        g_ref[...] = g
        u_ref[...] = u
        a_ref[...] = (g * jax.nn.sigmoid(g) * u).astype(BF16)

    w_spec = pl.BlockSpec((tn, D), lambda i, j: (j, 0))
    o_spec = pl.BlockSpec((tm, tn), lambda i, j: (i, j))
    in_specs, out_specs, out_shape, scratch, extra = _with_rider(
        rider, [pl.BlockSpec((tm, D), lambda i, j: (i, 0)), w_spec, w_spec], [o_spec, o_spec, o_spec],
        [jax.ShapeDtypeStruct((T, FP), F32), jax.ShapeDtypeStruct((T, FP), F32), jax.ShapeDtypeStruct((T, FP), BF16)], [])
    return pl.pallas_call(
        body, name=name, grid=(ni, nj), in_specs=in_specs, out_specs=out_specs, out_shape=out_shape,
        scratch_shapes=scratch,
        compiler_params=_params(("arbitrary", "arbitrary") if rider else ("parallel", "parallel"),
                                has_side_effects=rider is not None),
    )(n, wg_t, wu_t, *extra)


def _ffn_act_bwd(dh, wd, g, u, name, rider=None):
    T = dh.shape[0]
    tm = _tile(T, (1024, 512, 256, 128))
    tn = 512
    ni, nj = T // tm, FP // tn

    def body(*refs):
        step = pl.program_id(0) * nj + pl.program_id(1)
        dh_ref, wd_ref, g_ref, u_ref, dg_ref, du_ref = _carry(rider, refs, 4, 2, 0, step, ni * nj)
        da = _dot(dh_ref[...], wd_ref[...], NT)
        g = g_ref[...]
        sg = jax.nn.sigmoid(g)
        dg_ref[...] = (da * u_ref[...] * (sg * (1.0 + g * (1.0 - sg)))).astype(dg_ref.dtype)
        du_ref[...] = (da * (g * sg)).astype(du_ref.dtype)

    tile = pl.BlockSpec((tm, tn), lambda i, j: (i, j))
    in_specs, out_specs, out_shape, scratch, extra = _with_rider(
        rider, [pl.BlockSpec((tm, D), lambda i, j: (i, 0)), pl.BlockSpec((tn, D), lambda i, j: (j, 0)), tile, tile],
        [tile, tile], [jax.ShapeDtypeStruct((T, FP), BF16), jax.ShapeDtypeStruct((T, FP), BF16)], [])
    return pl.pallas_call(
        body, name=name, grid=(ni, nj), in_specs=in_specs, out_specs=out_specs, out_shape=out_shape,
        scratch_shapes=scratch,
        compiler_params=_params(("arbitrary", "arbitrary") if rider else ("parallel", "parallel"),
                                has_side_effects=rider is not None),
    )(dh, wd, g, u, *extra)


def _row_specs(rows, tr, cw):
    return [pl.BlockSpec((tr, cw), lambda j, i, o=off: (i, o + j)) for _, off in rows]


def _const_specs(consts, cw):
    specs = []
    for arr, off in consts:
        if off is None:
            specs.append(pl.BlockSpec(arr.shape, lambda j, i: (0, 0)))
        else:
            specs.append(pl.BlockSpec((arr.shape[0], cw), lambda j, i, o=off: (0, o + j)))
    return specs


def _rowwise(fn, rows, consts, out_dtypes, name, tr, cw, ncol):
    T = rows[0][0].shape[0]
    nr, nc = len(rows), len(consts)

    def body(*refs):
        r = [x[...].astype(F32) for x in refs[:nr]]
        c = [x[...] for x in refs[nr:nr + nc]]
        res = fn(*r, *c)
        for o_ref, v in zip(refs[nr + nc:], res):
            o_ref[...] = v.astype(o_ref.dtype)

    o_spec = pl.BlockSpec((tr, cw), lambda j, i: (i, j))
    return pl.pallas_call(
        body, name=name, grid=(ncol, T // tr),
        in_specs=_row_specs(rows, tr, cw) + _const_specs(consts, cw),
        out_specs=[o_spec] * len(out_dtypes),
        out_shape=[jax.ShapeDtypeStruct((T, ncol * cw), dt) for dt in out_dtypes],
        compiler_params=_params(("parallel", "parallel")),
    )(*[a for a, _ in rows], *[a for a, _ in consts])


def _rowwise_bwd(fn, rows, consts, cots, diff, ddtypes, name, tr, cw, ncol):
    T = rows[0][0].shape[0]
    nr, nc, nt, nd = len(rows), len(consts), len(cots), len(diff)

    def body(*refs):
        r = [x[...].astype(F32) for x in refs[:nr]]
        c = [x[...] for x in refs[nr:nr + nc]]
        ct = [x[...].astype(F32) for x in refs[nr + nc:nr + nc + nt]]
        drow_refs = refs[nr + nc + nt:nr + nc + nt + nd]
        dconst_refs = refs[nr + nc + nt + nd:]
        i = pl.program_id(1)

        def f(*args):
            full = list(r)
            for idx, a in zip(diff, args[:nd]):
                full[idx] = a
            return tuple(fn(*full, *args[nd:]))

        _, vjp = jax.vjp(f, *[r[d] for d in diff], *c)
        g = vjp(tuple(ct))
        for o_ref, v in zip(drow_refs, g[:nd]):
            o_ref[...] = v.astype(o_ref.dtype)

        @pl.when(i == 0)
        def _():
            for o_ref in dconst_refs:
                o_ref[...] = jnp.zeros_like(o_ref)

        for o_ref, v in zip(dconst_refs, g[nd:]):
            o_ref[...] += v

    o_spec = pl.BlockSpec((tr, cw), lambda j, i: (i, j))
    out_shape = [jax.ShapeDtypeStruct((T, ncol * cw), dt) for dt in ddtypes]
    out_shape += [jax.ShapeDtypeStruct(a.shape, F32) for a, _ in consts]
    return pl.pallas_call(
        body, name=name, grid=(ncol, T // tr),
        in_specs=_row_specs(rows, tr, cw) + _const_specs(consts, cw) + _row_specs(cots, tr, cw),
        out_specs=[o_spec] * nd + _const_specs(consts, cw),
        out_shape=out_shape,
        compiler_params=_params(("parallel", "arbitrary")),
    )(*[a for a, _ in rows], *[a for a, _ in consts], *[a for a, _ in cots])


def _rms(x, g):
    return x * lax.rsqrt(jnp.mean(x * x, axis=-1, keepdims=True) + EPS) * g


def _silu(x):
    return x * jax.nn.sigmoid(x)


def _norm_fn(x, g):
    return (_rms(x, g),)


def _norm_res_fn(x, g):
    return (x, _rms(x, g))


def _resid_fn(scale, x, h, g):
    return (x + scale * _rms(h, g),)


def _resid_h_fn(scale, h, g):
    return (scale * _rms(h, g),)


def _hpost_fn(o, hog, gn):
    return (_rms(o, gn) * _silu(hog),)


def _merge_fn(z0, z1, z2, yh, yf, ym):
    return (jax.nn.sigmoid(z0) * yh + jax.nn.sigmoid(z1) * yf + jax.nn.sigmoid(z2) * ym,)


def _loss(x3, tgt, name):
    T = x3.shape[0]
    tr = _tile(T, (256, 128))

    def body(x_ref, t_ref, dy_ref, s_ref):
        i = pl.program_id(0)
        e = x_ref[...] - t_ref[...]
        dy_ref[...] = e * (1.0 / D)
        col = jnp.sum(e * e, axis=0, keepdims=True)
        tot = col[:, 0:HD]
        for k in range(1, D // HD):
            tot = tot + col[:, k * HD:(k + 1) * HD]

        @pl.when(i == 0)
        def _():
            s_ref[...] = jnp.zeros_like(s_ref)

        s_ref[...] += tot

    spec = pl.BlockSpec((tr, D), lambda i: (i, 0))
    return pl.pallas_call(
        body, name=name, grid=(T // tr,), in_specs=[spec, spec],
        out_specs=[spec, pl.BlockSpec((1, HD), lambda i: (0, 0))],
        out_shape=[jax.ShapeDtypeStruct((T, D), F32), jax.ShapeDtypeStruct((1, HD), F32)],
        compiler_params=_params(("arbitrary",)),
    )(x3, tgt)


def _lower_bound(lb_ref):
    a0 = lb_ref[0:1, :]
    a1 = lb_ref[1:2, :]
    mx = jnp.maximum(a0, a1)
    e0 = jnp.exp(a0 - mx)
    return e0 / (e0 + jnp.exp(a1 - mx))


def _hgrn_prep(hq, hf, lb):
    g = lb + (1.0 - lb) * jax.nn.sigmoid(hf)
    return _silu(hq), 1.0 - g, jnp.log(g)


def _tri(n, upper):
    r = lax.broadcasted_iota(jnp.int32, (n, n), 0)
    c = lax.broadcasted_iota(jnp.int32, (n, n), 1)
    return (c >= r) if upper else (c <= r)


def _hgrn_factors(q, k, gl):
    low = _tri(CHUNK, False)
    b = lax.dot_general(low.astype(F32), gl, NN, precision=HIGHEST, preferred_element_type=F32)
    bl = b[CHUNK - 1:CHUNK, :]
    ref = b[CHUNK // 2 - 1:CHUNK // 2, :]
    eb = jnp.exp(b)
    ea = jnp.exp(b - ref)
    ebn = jnp.exp(ref - b)
    ek = jnp.exp(bl - b)
    ebl = jnp.exp(bl)
    return low, eb, ea, ebn, ek, ebl


def _hgrn_fwd(proj, hgrn_lb, rider=None):
    T = proj.shape[0]
    cb = _tile(T, (512, 256, 128, 64))
    nchunk = cb // CHUNK

    def body(*refs):
        hq_ref, hf_ref, hi_ref, lb_ref, o_ref, st_ref, state = _carry(rider, refs, 4, 2, 1, pl.program_id(0), T // cb)

        @pl.when(pl.program_id(0) == 0)
        def _():
            state[...] = jnp.zeros_like(state)

        lb = _lower_bound(lb_ref)

        def chunk(c, carry):
            r0 = pl.multiple_of(c * CHUNK, CHUNK)
            for h in range(NH):
                cols = slice(h * HD, (h + 1) * HD)
                q, k, gl = _hgrn_prep(hq_ref[pl.ds(r0, CHUNK), cols], hf_ref[pl.ds(r0, CHUNK), cols], lb[:, cols])
                v = hi_ref[pl.ds(r0, CHUNK), cols]
                low, eb, ea, ebn, ek, ebl = _hgrn_factors(q, k, gl)
                s_t = state[h]
                st_ref[c, h] = s_t
                pm = jnp.where(low, _dot(q * ea, k * ebn, NT), 0.0)
                o_ref[pl.ds(r0, CHUNK), cols] = _dot(q * eb, s_t, NT) + _dot(pm, v, NN)
                state[h] = s_t * ebl + _dot(v, k * ek, TN)
            return carry

        lax.fori_loop(0, nchunk, chunk, 0)

    def col(off):
        return pl.BlockSpec((cb, WH), lambda i, o=off: (i, o))

    in_specs, out_specs, out_shape, scratch, extra = _with_rider(
        rider, [col(0), col(1), col(2), pl.BlockSpec((2, WH), lambda i: (0, 0))],
        [pl.BlockSpec((cb, WH), lambda i: (i, 0)), pl.BlockSpec((nchunk, NH, HD, HD), lambda i: (i, 0, 0, 0))],
        [jax.ShapeDtypeStruct((T, WH), F32), jax.ShapeDtypeStruct((T // CHUNK, NH, HD, HD), F32)],
        [pltpu.VMEM((NH, HD, HD), F32)])
    return pl.pallas_call(
        body, name="hgrn_fwd", grid=(T // cb,), in_specs=in_specs, out_specs=out_specs, out_shape=out_shape,
        scratch_shapes=scratch, compiler_params=_params(("arbitrary",), has_side_effects=rider is not None),
    )(proj, proj, proj, hgrn_lb, *extra)


def _hgrn_bwd(proj, hgrn_lb, states, do, rider=None):
    T = proj.shape[0]
    cb = _tile(T, (512, 256, 128, 64))
    nchunk = cb // CHUNK
    nb = T // cb

    def body(*refs):
        (hq_ref, hf_ref, hi_ref, lb_ref, st_ref, do_ref, dhq_ref, dhf_ref, dhi_ref, dlb_ref,
         dstate) = _carry(rider, refs, 6, 4, 1, pl.program_id(0), nb)

        @pl.when(pl.program_id(0) == 0)
        def _():
            dstate[...] = jnp.zeros_like(dstate)
            dlb_ref[...] = jnp.zeros_like(dlb_ref)

        lb = _lower_bound(lb_ref)
        up = _tri(CHUNK, True)
        last = lax.broadcasted_iota(jnp.int32, (CHUNK, HD), 0) == CHUNK - 1

        def chunk(cc, carry):
            c = nchunk - 1 - cc
            r0 = pl.multiple_of(c * CHUNK, CHUNK)
            for h in range(NH):
                cols = slice(h * HD, (h + 1) * HD)
                hq = hq_ref[pl.ds(r0, CHUNK), cols]
                hf = hf_ref[pl.ds(r0, CHUNK), cols]
                (q, k, gl), prep_vjp = jax.vjp(_hgrn_prep, hq, hf, lb[:, cols])
                v = hi_ref[pl.ds(r0, CHUNK), cols]
                d_o = do_ref[pl.ds(r0, CHUNK), cols]
                low, eb, ea, ebn, ek, ebl = _hgrn_factors(q, k, gl)
                s_t = st_ref[c, h]
                ds_new = dstate[h]
                qe, am, bm, kb = q * eb, q * ea, k * ebn, k * ek
                pm_t = jnp.where(up, _dot(bm, am, NT), 0.0)
                dp = jnp.where(low, _dot(d_o, v, NT), 0.0)
                dp_t = jnp.where(up, _dot(v, d_o, NT), 0.0)
                dqe = _dot(d_o, s_t, NN)
                da = _dot(dp, bm, NN)
                db_m = _dot(dp_t, am, NN)
                dkb = _dot(v, ds_new, NN)
                dv = _dot(pm_t, d_o, NN) + _dot(kb, ds_new, NT)
                dq = dqe * eb + da * ea
                dk = db_m * ebn + dkb * ek
                dbl = jnp.sum(dkb * kb, axis=0, keepdims=True) + jnp.sum(ds_new * s_t, axis=0, keepdims=True) * ebl
                db = (dqe * qe + da * am.astype(BF16).astype(F32) - db_m * bm.astype(BF16).astype(F32) - dkb * kb
                      + jnp.where(last, dbl, 0.0))
                dgl = lax.dot_general(up.astype(F32), db, NN, precision=HIGHEST, preferred_element_type=F32)
                dhq, dhf, dlb = prep_vjp((dq, dk, dgl))
                dhq_ref[pl.ds(r0, CHUNK), cols] = dhq.astype(dhq_ref.dtype)
                dhf_ref[pl.ds(r0, CHUNK), cols] = dhf.astype(dhf_ref.dtype)
                dhi_ref[pl.ds(r0, CHUNK), cols] = dv.astype(dhi_ref.dtype)
                dlb_ref[:, cols] += dlb
                dstate[h] = _dot(d_o, qe, TN) + ds_new * ebl
            return carry

        lax.fori_loop(0, nchunk, chunk, 0)

    def col(off):
        return pl.BlockSpec((cb, WH), lambda i, o=off: (nb - 1 - i, o))

    row = pl.BlockSpec((cb, WH), lambda i: (nb - 1 - i, 0))
    in_specs, out_specs, out_shape, scratch, extra = _with_rider(
        rider, [col(0), col(1), col(2), pl.BlockSpec((2, WH), lambda i: (0, 0)),
                pl.BlockSpec((nchunk, NH, HD, HD), lambda i: (nb - 1 - i, 0, 0, 0)), row],
        [row, row, row, pl.BlockSpec((1, WH), lambda i: (0, 0))],
        [jax.ShapeDtypeStruct((T, WH), BF16)] * 3 + [jax.ShapeDtypeStruct((1, WH), F32)], [pltpu.VMEM((NH, HD, HD), F32)])
    return pl.pallas_call(
        body, name="hgrn_bwd", grid=(nb,), in_specs=in_specs, out_specs=out_specs, out_shape=out_shape,
        scratch_shapes=scratch, compiler_params=_params(("arbitrary",), has_side_effects=rider is not None),
    )(proj, proj, proj, hgrn_lb, states, do, *extra)


def _log_sigmoid(z):
    return jnp.minimum(z, 0.0) - jnp.log(1.0 + jnp.exp(-jnp.abs(z)))


def _fox_cum(proj, fb_pad):
    T = proj.shape[0]
    tb = _tile(T, (256, 128))

    def body(ff_ref, fb_ref, ct_ref, cq_ref, carry):
        @pl.when(pl.program_id(0) == 0)
        def _():
            carry[...] = jnp.zeros_like(carry)

        lf = _log_sigmoid(ff_ref[...] + fb_ref[...])
        cs = lax.dot_general(_tri(tb, False).astype(F32), lf, NN, precision=HIGHEST,
                             preferred_element_type=F32) + carry[0:1, :]
        carry[0:1, :] = cs[tb - 1:tb, :]
        ct_ref[...] = cs.T[0:8, :]
        for h in range(NH):
            cq_ref[h] = jnp.broadcast_to(cs[:, h:h + 1], (tb, HD))

    return pl.pallas_call(
        body, name="fox_cum", grid=(T // tb,),
        in_specs=[pl.BlockSpec((tb, HD), lambda i: (i, CB_FF)), pl.BlockSpec((1, HD), lambda i: (0, 0))],
        out_specs=[pl.BlockSpec((8, tb), lambda i: (0, i)), pl.BlockSpec((NH, tb, HD), lambda i: (0, i, 0))],
        out_shape=[jax.ShapeDtypeStruct((8, T), F32), jax.ShapeDtypeStruct((NH, T, HD), F32)],
        scratch_shapes=[pltpu.VMEM((8, HD), F32)],
        compiler_params=_params(("arbitrary",)),
    )(proj, fb_pad)


def _fox_cum_bwd(dc, proj, fb_pad):
    T = proj.shape[0]
    tb = _tile(T, (256, 128))
    nb = T // tb

    def body(dc_ref, ff_ref, fb_ref, dff_ref, dfb_ref, carry):
        @pl.when(pl.program_id(0) == 0)
        def _():
            carry[...] = jnp.zeros_like(carry)
            dfb_ref[...] = jnp.zeros_like(dfb_ref)

        rid = lax.broadcasted_iota(jnp.int32, (8, tb), 0)
        m8 = jnp.zeros((8, tb), F32)
        for h in range(NH):
            m8 = m8 + jnp.where(rid == h, dc_ref[h], 0.0)
        dcb = jnp.concatenate([m8, jnp.zeros((HD - 8, tb), F32)], axis=0).T
        rev = lax.dot_general(_tri(tb, True).astype(F32), dcb, NN, precision=HIGHEST,
                              preferred_element_type=F32) + carry[0:1, :]
        carry[0:1, :] = rev[0:1, :]
        dff = rev * jax.nn.sigmoid(-(ff_ref[...] + fb_ref[...]))
        dff_ref[...] = dff.astype(dff_ref.dtype)
        dfb_ref[...] += jnp.sum(dff, axis=0, keepdims=True)

    return pl.pallas_call(
        body, name="fox_cum_bwd", grid=(nb,),
        in_specs=[pl.BlockSpec((NH, 8, tb), lambda i: (0, 0, nb - 1 - i)),
                  pl.BlockSpec((tb, HD), lambda i: (nb - 1 - i, CB_FF)), pl.BlockSpec((1, HD), lambda i: (0, 0))],
        out_specs=[pl.BlockSpec((tb, HD), lambda i: (nb - 1 - i, 0)), pl.BlockSpec((1, HD), lambda i: (0, 0))],
        out_shape=[jax.ShapeDtypeStruct((T, HD), BF16), jax.ShapeDtypeStruct((1, HD), F32)],
        scratch_shapes=[pltpu.VMEM((8, HD), F32)],
        compiler_params=_params(("arbitrary",)),
    )(dc, proj, fb_pad)


STRIP = 128


def _fox_scores(q, k, cq, ck, i, j, bq, bk, r0=0):
    rows = q.shape[0]
    s = _dot(q, k, NT) * SCALE + (cq - ck)
    diff = lax.broadcasted_iota(jnp.int32, (rows, bk), 1) - lax.broadcasted_iota(jnp.int32, (rows, bk), 0)
    return jnp.where(diff <= i * bq + r0 - j * bk, s, NEG)


def _heads(h):
    return slice(h * HD, (h + 1) * HD)


UNDERFLOW = -120.0


def _fox_windows(proj, cq):
    T = proj.shape[0]
    bq = _tile(T, (512, 256, 128))
    nq = T // bq
    assert nq <= HD

    def body(q_ref, k_ref, cq_ref, jlo_ref, ihi_ref, norm_s, cs_s, ce_s):
        i = pl.program_id(0)

        @pl.when(i == 0)
        def _():
            norm_s[...] = jnp.zeros_like(norm_s)
            cs_s[...] = jnp.zeros_like(cs_s)
            ce_s[...] = jnp.zeros_like(ce_s)

        lane = lax.broadcasted_iota(jnp.int32, (1, HD), 1)
        for h in range(NH):
            for row, ref in ((h, q_ref), (8 + h, k_ref)):
                x = ref[:, _heads(h)]
                biggest = jnp.max(jnp.sum(x * x, axis=1, keepdims=True), axis=0, keepdims=True)
                norm_s[row:row + 1, :] = jnp.maximum(norm_s[row:row + 1, :], jnp.broadcast_to(biggest, (1, HD)))
            cs_s[h, pl.ds(i, 1), :] = cq_ref[h, 0:1, :]
            ce_s[h:h + 1, :] = jnp.where(lane == i, cq_ref[h, bq - 1:bq, :], ce_s[h:h + 1, :])

        @pl.when(i == nq - 1)
        def _():
            rows = lax.broadcasted_iota(jnp.int32, (HD, HD), 0)
            cols = lax.broadcasted_iota(jnp.int32, (HD, HD), 1)
            need = cols == rows
            for h in range(NH):
                slack = 2.05 * SCALE * jnp.sqrt(norm_s[h:h + 1, :] * norm_s[8 + h:9 + h, :])
                bound = cs_s[h] - ce_s[h:h + 1, :] + slack
                need = need | ((bound >= UNDERFLOW) & (cols < rows))
            need = need & (rows < nq) & (cols < nq)
            jlo = jnp.min(jnp.where(need, cols, HD).astype(F32), axis=1, keepdims=True)
            ihi = jnp.max(jnp.where(need, rows, -1).astype(F32), axis=0, keepdims=True)
            jlo_ref[...] = jnp.broadcast_to(jlo, (HD, HD)).astype(jnp.int32)
            ihi_ref[...] = jnp.broadcast_to(ihi, (8, HD)).astype(jnp.int32)

    jlo, ihi = pl.pallas_call(
        body, name="fox_windows", grid=(nq,),
        in_specs=[pl.BlockSpec((bq, WH), lambda i: (i, CB_FQ // NH)), pl.BlockSpec((bq, WH), lambda i: (i, CB_FK // NH)),
                  pl.BlockSpec((NH, bq, HD), lambda i: (0, i, 0))],
        out_specs=[pl.BlockSpec((HD, HD), lambda i: (0, 0)), pl.BlockSpec((8, HD), lambda i: (0, 0))],
        out_shape=[jax.ShapeDtypeStruct((HD, HD), jnp.int32), jax.ShapeDtypeStruct((8, HD), jnp.int32)],
        scratch_shapes=[pltpu.VMEM((16, HD), F32), pltpu.VMEM((NH, HD, HD), F32), pltpu.VMEM((8, HD), F32)],
        compiler_params=_params(("arbitrary",)),
    )(proj, proj, cq)
    return jnp.concatenate([jlo[:nq, 0], ihi[0, :nq]])


def _fox_fwd(win, proj, ct, cq):
    T = proj.shape[0]
    bq = bk = _tile(T, (512, 256, 128))
    nq = nk = T // bq

    def body(win_ref, q_ref, k_ref, v_ref, ct_ref, cq_ref, o_ref, lse_ref, m_s, l_s, acc_s):
        i, jj = pl.program_id(0), pl.program_id(1)
        j = win_ref[i] + jj

        @pl.when(jj == 0)
        def _():
            m_s[...] = jnp.full_like(m_s, NEG)
            l_s[...] = jnp.zeros_like(l_s)
            acc_s[...] = jnp.zeros_like(acc_s)

        @pl.when(j <= i)
        def _():
            for h in range(NH):
                hs = _heads(h)
                k, v, ck = k_ref[:, hs], v_ref[:, hs], ct_ref[h:h + 1, :]
                for r0 in range(0, bq, STRIP):
                    rs = slice(r0, r0 + STRIP)
                    s = _fox_scores(q_ref[rs, hs], k, cq_ref[h, rs, 0:1], ck, i, j, bq, bk, r0)
                    m_prev = m_s[h, rs]
                    m_new = jnp.maximum(m_prev, jnp.max(s, axis=1, keepdims=True))
                    alpha = jnp.exp(m_prev - m_new)
                    p = jnp.exp(s - m_new)
                    l_s[h, rs] = alpha * l_s[h, rs] + jnp.sum(p, axis=1, keepdims=True)
                    acc_s[rs, hs] = alpha * acc_s[rs, hs] + _dot(p, v, NN)
                    m_s[h, rs] = m_new

        @pl.when(jj == nk - 1)
        def _():
            for h in range(NH):
                o_ref[:, _heads(h)] = acc_s[:, _heads(h)] / l_s[h]
                lse_ref[h] = jnp.broadcast_to(m_s[h] + jnp.log(l_s[h]), (bq, HD))

    def key_block(i, jj, win):
        return jnp.minimum(win[i] + jj, i)

    def kv(off):
        return pl.BlockSpec((bk, WH), lambda i, jj, win, o=off // NH: (key_block(i, jj, win), o))

    stat = pl.BlockSpec((NH, bq, HD), lambda i, jj, win: (0, i, 0))
    return pl.pallas_call(
        body, name="fox_fwd",
        grid_spec=pltpu.PrefetchScalarGridSpec(
            num_scalar_prefetch=1, grid=(nq, nk),
            in_specs=[pl.BlockSpec((bq, WH), lambda i, jj, win: (i, CB_FQ // NH)), kv(CB_FK), kv(CB_FV),
                      pl.BlockSpec((8, bk), lambda i, jj, win: (0, key_block(i, jj, win))), stat],
            out_specs=[pl.BlockSpec((bq, WH), lambda i, jj, win: (i, 0)), stat],
            scratch_shapes=[pltpu.VMEM((NH, bq, 1), F32), pltpu.VMEM((NH, bq, 1), F32), pltpu.VMEM((bq, WH), F32)]),
        out_shape=[jax.ShapeDtypeStruct((T, WH), F32), jax.ShapeDtypeStruct((NH, T, HD), F32)],
        compiler_params=_params(("parallel", "arbitrary")),
    )(win, proj, proj, proj, ct, cq)


def _fox_bwd_dq(win, proj, ct, cq, lse, do):
    T = proj.shape[0]
    bq = bk = _tile(T, (512, 256, 128))
    nq = nk = T // bq

    def body(win_ref, q_ref, k_ref, v_ref, ct_ref, cq_ref, lse_ref, do_ref, dq_ref, delta_ref, acc_s, delta_s, psum_s):
        i, jj = pl.program_id(0), pl.program_id(1)
        j = win_ref[i] + jj % nk

        @pl.when(jj == 0)
        def _():
            acc_s[...] = jnp.zeros_like(acc_s)
            delta_s[...] = jnp.zeros_like(delta_s)
            psum_s[...] = jnp.zeros_like(psum_s)

        def probs(h):
            hs = _heads(h)
            k = k_ref[:, hs]
            s = _fox_scores(q_ref[:, hs], k, cq_ref[h, :, 0:1], ct_ref[h:h + 1, :], i, j, bq, bk)
            return k, jnp.exp(s - lse_ref[h, :, 0:1]), _dot(do_ref[:, hs], v_ref[:, hs], NT)

        @pl.when((j <= i) & (jj < nk))
        def _():
            for h in range(NH):
                _, p, dp = probs(h)
                delta_s[h] += jnp.sum(p * dp, axis=1, keepdims=True)
                psum_s[h] += jnp.sum(p, axis=1, keepdims=True)

        @pl.when((j <= i) & (jj >= nk))
        def _():
            for h in range(NH):
                k, p, dp = probs(h)
                ds = p * (dp - delta_s[h] / psum_s[h])
                acc_s[:, _heads(h)] += _dot(ds, k, NN) * SCALE

        @pl.when(jj == 2 * nk - 1)
        def _():
            dq_ref[...] = acc_s[...].astype(dq_ref.dtype)
            for h in range(NH):
                delta_ref[h] = jnp.broadcast_to(delta_s[h] / psum_s[h], (bq, HD))

    def key_block(i, jj, win):
        return jnp.minimum(win[i] + jj % nk, i)

    def kv(off):
        return pl.BlockSpec((bk, WH), lambda i, jj, win, o=off // NH: (key_block(i, jj, win), o))

    qrow = pl.BlockSpec((bq, WH), lambda i, jj, win: (i, 0))
    stat = pl.BlockSpec((NH, bq, HD), lambda i, jj, win: (0, i, 0))
    return pl.pallas_call(
        body, name="fox_bwd_dq",
        grid_spec=pltpu.PrefetchScalarGridSpec(
            num_scalar_prefetch=1, grid=(nq, 2 * nk),
            in_specs=[pl.BlockSpec((bq, WH), lambda i, jj, win: (i, CB_FQ // NH)), kv(CB_FK), kv(CB_FV),
                      pl.BlockSpec((8, bk), lambda i, jj, win: (0, key_block(i, jj, win))), stat, stat, qrow],
            out_specs=[qrow, stat],
            scratch_shapes=[pltpu.VMEM((bq, WH), F32), pltpu.VMEM((NH, bq, 1), F32), pltpu.VMEM((NH, bq, 1), F32)]),
        out_shape=[jax.ShapeDtypeStruct((T, WH), BF16), jax.ShapeDtypeStruct((NH, T, HD), F32)],
        compiler_params=_params(("parallel", "arbitrary")),
    )(win, proj, proj, proj, ct, cq, lse, do)


def _fox_bwd_dkv(win, proj, ct, cq, lse, delta, do):
    T = proj.shape[0]
    bq = bk = _tile(T, (512, 256, 128))
    nq = nk = T // bq

    def body(win_ref, q_ref, k_ref, v_ref, ct_ref, cq_ref, lse_ref, delta_ref, do_ref, dk_ref, dv_ref, dc_ref,
             dk_s, dv_s, dc_s):
        j, ii = pl.program_id(0), pl.program_id(1)
        i = j + ii

        @pl.when(ii == 0)
        def _():
            dk_s[...] = jnp.zeros_like(dk_s)
            dv_s[...] = jnp.zeros_like(dv_s)
            dc_s[...] = jnp.zeros_like(dc_s)

        @pl.when(i <= win_ref[nq + j])
        def _():
            for h in range(NH):
                hs = _heads(h)
                q = q_ref[:, hs]
                d_o = do_ref[:, hs]
                s = _fox_scores(q, k_ref[:, hs], cq_ref[h, :, 0:1], ct_ref[h:h + 1, :], i, j, bq, bk)
                p = jnp.exp(s - lse_ref[h, :, 0:1])
                dv_s[:, hs] += _dot(p, d_o, TN)
                dp = _dot(d_o, v_ref[:, hs], NT)
                ds = p * (dp - delta_ref[h, :, 0:1])
                dk_s[:, hs] += _dot(ds, q, TN) * SCALE
                dc_s[h:h + 1, :] -= jnp.sum(ds, axis=0, keepdims=True)

        @pl.when(ii == nq - 1)
        def _():
            dk_ref[...] = dk_s[...].astype(dk_ref.dtype)
            dv_ref[...] = dv_s[...].astype(dv_ref.dtype)
            for h in range(NH):
                dc_ref[h] = jnp.broadcast_to(dc_s[h:h + 1, :], (8, bk))

    def query_block(j, ii, win):
        return jnp.minimum(j + ii, win[nq + j])

    def kv(off):
        return pl.BlockSpec((bk, WH), lambda j, ii, win, o=off // NH: (j, o))

    qrow = pl.BlockSpec((bq, WH), lambda j, ii, win: (query_block(j, ii, win), 0))
    stat = pl.BlockSpec((NH, bq, HD), lambda j, ii, win: (0, query_block(j, ii, win), 0))
    krow = pl.BlockSpec((bk, WH), lambda j, ii, win: (j, 0))
    return pl.pallas_call(
        body, name="fox_bwd_dkv",
        grid_spec=pltpu.PrefetchScalarGridSpec(
            num_scalar_prefetch=1, grid=(nk, nq),
            in_specs=[pl.BlockSpec((bq, WH), lambda j, ii, win: (query_block(j, ii, win), CB_FQ // NH)), kv(CB_FK),
                      kv(CB_FV), pl.BlockSpec((8, bk), lambda j, ii, win: (0, j)), stat, stat, stat, qrow],
            out_specs=[krow, krow, pl.BlockSpec((NH, 8, bk), lambda j, ii, win: (0, 0, j))],
            scratch_shapes=[pltpu.VMEM((bk, WH), F32), pltpu.VMEM((bk, WH), F32), pltpu.VMEM((8, bk), F32)]),
        out_shape=[jax.ShapeDtypeStruct((T, WH), BF16), jax.ShapeDtypeStruct((T, WH), BF16),
                   jax.ShapeDtypeStruct((NH, 8, T), F32)],
        compiler_params=_params(("parallel", "arbitrary")),
    )(win, proj, proj, proj, ct, cq, lse, delta, do)


def _mem_probs(q, mk):
    s = _dot(q, mk, NT) * SCALE
    e = jnp.exp(s - jnp.max(s, axis=1, keepdims=True))
    return e / jnp.sum(e, axis=1, keepdims=True)


def _mem_fwd(proj, mem_kv):
    T = proj.shape[0]
    tr = _tile(T, (512, 256, 128))
    M = mem_kv.shape[0]

    def body(q_ref, mk_ref, mv_ref, o_ref):
        o_ref[...] = _dot(_mem_probs(q_ref[...], mk_ref[...]), mv_ref[...], NN)

    return pl.pallas_call(
        body, name="mem_fwd", grid=(NM, T // tr),
        in_specs=[pl.BlockSpec((tr, HD), lambda h, i: (i, CB_MQ + h)),
                  pl.BlockSpec((M, HD), lambda h, i: (0, h)), pl.BlockSpec((M, HD), lambda h, i: (0, NM + h))],
        out_specs=pl.BlockSpec((tr, HD), lambda h, i: (i, h)),
        out_shape=jax.ShapeDtypeStruct((T, WM), F32),
        compiler_params=_params(("parallel", "parallel")),
    )(proj, mem_kv, mem_kv)


def _mem_bwd(proj, mem_kv, do):
    T = proj.shape[0]
    tr = _tile(T, (512, 256, 128))
    M = mem_kv.shape[0]

    def body(q_ref, mk_ref, mv_ref, do_ref, dq_ref, dmk_ref, dmv_ref):
        @pl.when(pl.program_id(1) == 0)
        def _():
            dmk_ref[...] = jnp.zeros_like(dmk_ref)
            dmv_ref[...] = jnp.zeros_like(dmv_ref)

        q, mk, d_o = q_ref[...], mk_ref[...], do_ref[...]
        p = _mem_probs(q, mk)
        dmv_ref[...] += _dot(p, d_o, TN)
        dp = _dot(d_o, mv_ref[...], NT)
        ds = p * (dp - jnp.sum(p * dp, axis=1, keepdims=True))
        dq_ref[...] = (_dot(ds, mk, NN) * SCALE).astype(dq_ref.dtype)
        dmk_ref[...] += _dot(ds, q, TN) * SCALE

    acc = pl.BlockSpec((M, HD), lambda h, i: (0, h))
    row = pl.BlockSpec((tr, HD), lambda h, i: (i, h))
    return pl.pallas_call(
        body, name="mem_bwd", grid=(NM, T // tr),
        in_specs=[pl.BlockSpec((tr, HD), lambda h, i: (i, CB_MQ + h)),
                  pl.BlockSpec((M, HD), lambda h, i: (0, h)), pl.BlockSpec((M, HD), lambda h, i: (0, NM + h)), row],
        out_specs=[row, acc, acc],
        out_shape=[jax.ShapeDtypeStruct((T, WM), BF16), jax.ShapeDtypeStruct((M, WM), F32),
                   jax.ShapeDtypeStruct((M, WM), F32)],
        compiler_params=_params(("parallel", "arbitrary")),
    )(proj, mem_kv, mem_kv, do)


def _mesh_place():
    x, y, c = lax.axis_index("x"), lax.axis_index("y"), lax.axis_index("c")
    return x, y, c


CHIP_FLIPS = (4, 2, 6)
CHIP_OF_SLOT = (0,) + CHIP_FLIPS


def _peer(x, y, c, k):
    px = 1 - x if k & 4 else x
    py = 1 - y if k & 2 else y
    pc = 1 - c if k & 1 else c
    return (px, py, pc), 4 * px + 2 * py + pc


class _Gather:
    def __init__(self, shapes, pad_rows):
        self.shapes, self.pad_rows, self.n = shapes, pad_rows, len(shapes)
        self.npad = sum(1 for p in pad_rows if p)

    def zeros(self):
        return jnp.zeros((max(self.pad_rows) or 16, self.shapes[0][1]), BF16)

    def out_shape(self):
        return [jax.ShapeDtypeStruct((NDEV * r + p, c), BF16) for (r, c), p in zip(self.shapes, self.pad_rows)]

    def sems(self):
        return [pltpu.SemaphoreType.DMA((self.n, NDEV - 1)), pltpu.SemaphoreType.DMA((self.n, NDEV - 1)),
                pltpu.SemaphoreType.DMA((self.n + self.npad,))]

    def _copies(self, ins, z_ref, outs, send_sems, recv_sems, loc_sems):
        x, y, c = _mesh_place()
        me = 4 * x + 2 * y + c
        sibling, _ = _peer(x, y, c, 1)
        local, first, arrive, forward = [], [], [], []
        ip = 0
        for w in range(self.n):
            r = ins[w].shape[0]
            dst = outs[w].at[pl.ds(pl.multiple_of(me * r, 16), r), :]
            local.append(functools.partial(pltpu.make_async_copy, ins[w], dst, loc_sems.at[w]))
            if self.pad_rows[w]:
                local.append(functools.partial(pltpu.make_async_copy, z_ref.at[pl.ds(0, self.pad_rows[w]), :],
                                               outs[w].at[pl.ds(NDEV * r, self.pad_rows[w]), :], loc_sems.at[self.n + ip]))
                ip += 1

            def remote(src, dst_, s, to):
                return functools.partial(pltpu.make_async_remote_copy, src_ref=src, dst_ref=dst_, send_sem=send_sems.at[w, s],
                                         recv_sem=recv_sems.at[w, s], device_id=to, device_id_type=MESH)

            for s, k in enumerate((1,) + CHIP_FLIPS):
                first.append(remote(ins[w], dst, s, _peer(x, y, c, k)[0]))
            for s, k in enumerate(CHIP_FLIPS):
                _, pidx = _peer(x, y, c, k)
                rows = outs[w].at[pl.ds(pl.multiple_of(pidx * r, 16), r), :]
                arrive.append(remote(rows, rows, 1 + s, sibling))
                forward.append(remote(rows, rows, 4 + s, sibling))
        return local, first, arrive, forward


    def start(self, *refs):
        local, first, _, _ = self._copies(*refs)
        for make in local + first:
            make().start()

    def forward(self, *refs):
        _, _, arrive, forward = self._copies(*refs)
        for a, f in zip(arrive, forward):
            a().wait_recv()
            f().start()

    def finish(self, *refs):
        local, first, _, forward = self._copies(*refs)
        for make in local + first[0::4] + forward:
            make().wait()
        for s in (1, 2, 3):
            for make in first[s::4]:
                make().wait_send()


def _all_gather(shards, pad_rows):
    n = len(shards)
    plan = _Gather([s.shape for s in shards], pad_rows)

    def body(*refs):
        args = (refs[:n], refs[n], refs[n + 1:2 * n + 1]) + tuple(refs[2 * n + 1:])
        plan.start(*args)
        plan.forward(*args)
        plan.finish(*args)

    any_spec = pl.BlockSpec(memory_space=pl.ANY)
    return pl.pallas_call(
        body, name="all_gather_weights",
        in_specs=[any_spec] * (n + 1), out_specs=[any_spec] * n,
        out_shape=plan.out_shape(),
        scratch_shapes=plan.sems(),
        compiler_params=pltpu.CompilerParams(has_side_effects=True),
    )(*shards, plan.zeros())


def _exchange_in_chip(grads, shard_rows, name):
    n = len(grads)
    plan = _InChip([g.shape for g in grads], shard_rows)

    def body(*refs):
        args = (refs[:n], refs[n:2 * n]) + tuple(refs[2 * n:])
        plan.start(*args)
        plan.finish(*args)

    any_spec = pl.BlockSpec(memory_space=pl.ANY)
    return pl.pallas_call(
        body, name=name,
        in_specs=[any_spec] * n, out_specs=[any_spec] * n, out_shape=plan.out_shape(), scratch_shapes=plan.sems(),
        compiler_params=pltpu.CompilerParams(has_side_effects=True),
    )(*grads)


class _InChip:
    def __init__(self, shapes, shard_rows):
        self.shapes, self.rows, self.n, self.ns = shapes, shard_rows, len(shapes), len(CHIP_OF_SLOT)

    def out_shape(self):
        return [jax.ShapeDtypeStruct((self.ns, r, s[1]), BF16) for s, r in zip(self.shapes, self.rows)]

    def sems(self):
        return [pltpu.SemaphoreType.DMA((self.n, self.ns)), pltpu.SemaphoreType.DMA((self.n, self.ns))]

    def _copies(self, ins, theirs, send_sems, recv_sems):
        x, y, c = _mesh_place()
        sibling, _ = _peer(x, y, c, 1)
        copies = []
        for w in range(self.n):
            r = self.rows[w]
            for s, k in enumerate(CHIP_OF_SLOT):
                _, other = _peer(x, y, c, k | 1)
                copies.append(pltpu.make_async_remote_copy(
                    src_ref=ins[w].at[pl.ds(pl.multiple_of(other * r, 16), r), :], dst_ref=theirs[w].at[s],
                    send_sem=send_sems.at[w, s], recv_sem=recv_sems.at[w, s], device_id=sibling, device_id_type=MESH))
        return copies

    def start(self, *refs):
        for cp in self._copies(*refs):
            cp.start()

    def finish(self, *refs):
        for cp in self._copies(*refs):
            cp.wait()


def _pair_sum(grad, theirs, name):
    ns, r, c = theirs.shape
    tr = r if r * c <= 2 * 1024 * 1024 else _tile(r, (256, 128, 64, 32, 16))
    per_block = r // tr

    def body(a_ref, b_ref, o_ref):
        o_ref[...] = (a_ref[...].astype(F32) + b_ref[...].astype(F32)).astype(o_ref.dtype)

    def owner_rows(s, i):
        x, y, c_ = _mesh_place()
        fx, fy = s % 2, s // 2
        px, py = x + fx - 2 * x * fx, y + fy - 2 * y * fy
        return ((4 * px + 2 * py + c_) * per_block + i, 0)

    slot = pl.BlockSpec((None, tr, c), lambda s, i: (s, i, 0))
    return pl.pallas_call(
        body, name=name, grid=(ns, per_block),
        in_specs=[pl.BlockSpec((tr, c), owner_rows), slot], out_specs=slot,
        out_shape=jax.ShapeDtypeStruct((ns, r, c), theirs.dtype),
        compiler_params=_params(("parallel", "parallel")),
    )(grad, theirs)


def _exchange_between_chips(pairs, name):
    n = len(pairs)
    plan = _ChipExchange([p.shape for p in pairs])

    def body(*refs):
        args = (refs[:n], refs[n:2 * n]) + tuple(refs[2 * n:])
        plan.start(*args)
        plan.finish(*args)

    any_spec = pl.BlockSpec(memory_space=pl.ANY)
    return pl.pallas_call(
        body, name=name,
        in_specs=[any_spec] * n, out_specs=[any_spec] * n,
        out_shape=plan.out_shape(), scratch_shapes=plan.sems(),
        compiler_params=pltpu.CompilerParams(has_side_effects=True),
    )(*pairs)


class _ChipExchange:
    def __init__(self, shapes):
        self.shapes, self.n, self.ns = shapes, len(shapes), len(CHIP_OF_SLOT) - 1

    def out_shape(self):
        return [jax.ShapeDtypeStruct((self.ns,) + tuple(s[1:]), BF16) for s in self.shapes]

    def sems(self):
        return [pltpu.SemaphoreType.DMA((self.n, self.ns)), pltpu.SemaphoreType.DMA((self.n, self.ns))]

    def _copies(self, ins, outs, send_sems, recv_sems):
        x, y, c = _mesh_place()
        copies = []
        for w in range(self.n):
            for s, k in enumerate(CHIP_OF_SLOT[1:]):
                peer, _ = _peer(x, y, c, k)
                copies.append(pltpu.make_async_remote_copy(
                    src_ref=ins[w].at[s + 1], dst_ref=outs[w].at[s], send_sem=send_sems.at[w, s],
                    recv_sem=recv_sems.at[w, s], device_id=peer, device_id_type=MESH))
        return copies

    def start(self, *refs):
        for cp in self._copies(*refs):
            cp.start()

    def finish(self, *refs):
        for cp in self._copies(*refs):
            cp.wait()


def _sum_chips(pair, recv, name):
    _, r, c = recv.shape
    tr = _tile(r, (128, 64, 32, 16))

    def body(p_ref, x_ref, o_ref):
        acc = p_ref[...].astype(F32)
        for s in range(x_ref.shape[0]):
            acc = acc + x_ref[s].astype(F32)
        o_ref[...] = acc

    return pl.pallas_call(
        body, name=name, grid=(r // tr,),
        in_specs=[pl.BlockSpec((None, tr, c), lambda i: (0, i, 0)), pl.BlockSpec((recv.shape[0], tr, c), lambda i: (0, i, 0))],
        out_specs=pl.BlockSpec((tr, c), lambda i: (i, 0)),
        out_shape=jax.ShapeDtypeStruct((r, c), F32),
        compiler_params=_params(("parallel",)),
    )(pair, recv)


def _all_reduce_small(part):
    R, W = part.shape

    def body(x_ref, o_ref, buf, send_sems, recv_sems):
        x, y, c = _mesh_place()
        me = 4 * x + 2 * y + c
        buf[me] = x_ref[...]
        copies = []
        for k in range(1, NDEV):
            peer, _ = _peer(x, y, c, k)
            cp = pltpu.make_async_remote_copy(src_ref=x_ref, dst_ref=buf.at[me], send_sem=send_sems.at[k - 1],
                                              recv_sem=recv_sems.at[k - 1], device_id=peer, device_id_type=MESH)
            cp.start()
            copies.append(cp)
        for cp in copies:
            cp.wait()
        acc = buf[0]
        for d in range(1, NDEV):
            acc = acc + buf[d]
        o_ref[...] = acc

    vm = pl.BlockSpec(memory_space=pltpu.VMEM)
    return pl.pallas_call(
        body, name="all_reduce_small", in_specs=[vm], out_specs=vm,
        out_shape=jax.ShapeDtypeStruct((R, W), F32),
        scratch_shapes=[pltpu.VMEM((NDEV, R, W), F32), pltpu.SemaphoreType.DMA((NDEV - 1,)),
                        pltpu.SemaphoreType.DMA((NDEV - 1,))],
        compiler_params=pltpu.CompilerParams(has_side_effects=True),
    )(part)


def _adam_math(w, g, m, v):
    m2 = ADAM_B1 * m + (1.0 - ADAM_B1) * g
    v2 = ADAM_B2 * v + (1.0 - ADAM_B2) * (g * g)
    m_hat = m2 / (1.0 - ADAM_B1 ** ADAM_STEP)
    v_hat = v2 / (1.0 - ADAM_B2 ** ADAM_STEP)
    delta = -ADAM_LR * (m_hat / (jnp.sqrt(v_hat) + ADAM_EPS) + ADAM_WD * w)
    return delta, m2, v2


def _adamw(w, g, m, v, name):
    r, c = w.shape
    tr = r
    for cand in (1024, 512, 256, 128, 64, 32, 16, 8):
        if r % cand == 0 and cand * c <= 256 * 1024:
            tr = cand
            break

    def body(w_ref, g_ref, m_ref, v_ref, d_ref, m2_ref, v2_ref):
        d_ref[...], m2_ref[...], v2_ref[...] = _adam_math(w_ref[...], g_ref[...], m_ref[...], v_ref[...])

    spec = pl.BlockSpec((tr, c), lambda i: (i, 0))
    return pl.pallas_call(
        body, name=name, grid=(r // tr,), in_specs=[spec] * 4, out_specs=[spec] * 3,
        out_shape=[jax.ShapeDtypeStruct((r, c), F32)] * 3,
        compiler_params=_params(("parallel",)),
    )(w, g, m, v)


GAINS = ("ffn1_pre", "ffn1_post", "mix_pre", "mix_post", "mem_norm", "ffn2_pre", "ffn2_post")
GAIN_ROWS = D // HD
ROW_LB = len(GAINS) * GAIN_ROWS
ROWS_GRAD_IN = ROW_LB + 24
ROWS_PACKED = ROW_LB + 32


def _small_update(gsum, w_p, m_p, v_p):
    def body(g_ref, w_ref, m_ref, v_ref, go_ref, d_ref, m2_ref, v2_ref):
        a0 = w_ref[ROW_LB:ROW_LB + 8, :]
        a1 = w_ref[ROW_LB + 8:ROW_LB + 16, :]
        mx = jnp.maximum(a0, a1)
        e0, e1 = jnp.exp(a0 - mx), jnp.exp(a1 - mx)
        lb = e0 / (e0 + e1)
        da0 = g_ref[ROW_LB:ROW_LB + 8, :] * lb * (1.0 - lb)
        g = jnp.concatenate([g_ref[0:ROW_LB, :], da0, -da0, g_ref[ROW_LB + 8:ROWS_GRAD_IN, :]], axis=0)
        go_ref[...] = g
        d_ref[...], m2_ref[...], v2_ref[...] = _adam_math(w_ref[...], g, m_ref[...], v_ref[...])

    vm = pl.BlockSpec(memory_space=pltpu.VMEM)
    return pl.pallas_call(
        body, name="small_update", in_specs=[vm] * 4, out_specs=[vm] * 4,
        out_shape=[jax.ShapeDtypeStruct((ROWS_PACKED, HD), F32)] * 4,
    )(gsum, w_p, m_p, v_p)


def _rows8(a):
    a = a.reshape(-1)
    rows = -(-a.shape[0] // HD)
    rows8 = -(-rows // 8) * 8
    return jnp.pad(a, (0, rows8 * HD - a.shape[0])).reshape(rows8, HD)


def _pack_small(gains, lb0, lb1, gnorm, fb):
    return jnp.concatenate([_rows8(g) for g in gains] + [_rows8(lb0), _rows8(lb1), _rows8(gnorm), _rows8(fb)], axis=0)


def _unpack_small(p):
    out = {}
    for i, name in enumerate(GAINS):
        out[name] = p[i * GAIN_ROWS:(i + 1) * GAIN_ROWS].reshape(1, D)
    lb0 = p[ROW_LB:ROW_LB + NH].reshape(1, WH)
    lb1 = p[ROW_LB + 8:ROW_LB + 8 + NH].reshape(1, WH)
    out["hgrn_lb"] = jnp.concatenate([lb0, lb1], axis=0)
    out["hgrn_gnorm"] = p[ROW_LB + 16:ROW_LB + 16 + NH].reshape(1, WH)
    out["fox_fb"] = p[ROW_LB + 24:ROW_LB + 25, 0:NH]
    return out


def _ffn_forward(xin, pre, post, wg_t, wu_t, wd, tag, rider=None, rider_down=None):
    T = xin.shape[0]
    tr = _tile(T, (256, 128))
    (n,) = _rowwise(_norm_fn, [(xin, 0)], [(pre, None)], [BF16], f"{tag}_pre", tr, D, 1)
    g, u, a, *carried = _ffn_up(n, wg_t, wu_t, f"{tag}_up", rider)
    if wd is None:
        wd = carried[0]
    if rider_down is None:
        h = _mm(a, wd, "nn", F32, f"{tag}_down")
    else:
        h, *more = _mm(a, wd, "nn", F32, f"{tag}_down", rider=rider_down)
        carried = carried + more
    (xout,) = _rowwise(functools.partial(_resid_fn, 0.5), [(xin, 0), (h, 0)], [(post, None)], [F32], f"{tag}_post", tr, D, 1)
    return xout, (xin, n, g, u, a, h), carried


def _mm_out(res):
    return (res[0], list(res[1:])) if isinstance(res, (list, tuple)) else (res, [])


def _ffn_backward(dxout, saved, pre, post, wg_t, wu_t, wd, tag, rider=None, exchange=None, rider_dwd=None, after_dwd=None):
    xin, n, g, u, a, h = saved
    T = xin.shape[0]
    tr = _tile(T, (256, 128))
    dh, dpost = _rowwise_bwd(functools.partial(_resid_h_fn, 0.5), [(h, 0)], [(post, None)], [(dxout, 0)], [0], [BF16],
                             f"{tag}_post_bwd", tr, D, 1)
    dwd, got = _mm_out(_mm(a, dh, "tn", BF16, f"{tag}_dwd", rider=rider_dwd))
    if after_dwd is not None:
        rider = after_dwd(got)
    dg, du, *carried = _ffn_act_bwd(dh, wd, g, u, f"{tag}_act_bwd", rider)
    dwg = _mm(dg, n, "tn", BF16, f"{tag}_dwg")
    dwu = _mm(du, n, "tn", BF16, f"{tag}_dwu")
    if exchange is None:
        dn = _mm(dg, wg_t, "nn", F32, f"{tag}_dn_g")
        dn = _mm(du, wu_t, "nn", F32, f"{tag}_dn_u", add=dn)
    else:
        ride_a, ride_b, take = exchange(dwg, dwu, dwd)
        dn, got_a = _mm_out(_mm(dg, wg_t, "nn", F32, f"{tag}_dn_g", rider=ride_a))
        dn, got_b = _mm_out(_mm(du, wu_t, "nn", F32, f"{tag}_dn_u", add=dn, rider=ride_b))
        take(got_a, got_b)
    dxin, dpre = _rowwise_bwd(_norm_res_fn, [(xin, 0)], [(pre, None)], [(dxout, 0), (dn, 0)], [0], [F32],
                              f"{tag}_pre_bwd", tr, D, 1)
    return dxin, (dwg, dwu, dwd), dpre, dpost, carried


GATHER_FIRST = ("ffn1_wg", "ffn1_wu")
GATHER_IN_FFN1_UP = ("ffn1_wd", "w_in")
GATHER_IN_FFN1_DOWN = ("w_gate",)
GATHER_IN_PROJ = ("w_mem_kv", "w_hgrn_out", "w_fox_out", "w_mem_out", "w_o")
GATHER_IN_GATE = ("ffn2_wg",)
GATHER_IN_HGRN = ("ffn2_wu",)
GATHER_IN_FFN2_UP = ("ffn2_wd",)
GROUP_FFN1 = ("ffn1_wg", "ffn1_wu", "ffn1_wd")
GROUP_MIX = ("w_in", "w_mem_kv", "w_hgrn_out", "w_fox_out", "w_mem_out", "w_gate", "w_o")
GROUP_FFN2 = ("ffn2_wg", "ffn2_wu", "ffn2_wd")


def _gather_rider(blocks, names):
    plan = _Gather([blocks[n].shape for n in names], [FFN_PAD.get(n, 0) for n in names])
    return _Rider(plan, [blocks[n] for n in names] + [plan.zeros()], GATHER_STEPS)


def _local_step(x, mem, tgt, small, wts=None, blocks=None):
    T = x.shape[0]
    tr = _tile(T, (256, 128))
    fb_pad = jnp.pad(small["fox_fb"], ((0, 0), (0, HD - NH)))
    dist = blocks is not None
    if dist:
        wts = dict(zip(GATHER_FIRST, _all_gather([blocks[n] for n in GATHER_FIRST], [FFN_PAD[n] for n in GATHER_FIRST])))

    def riding(names):
        return _gather_rider(blocks, names) if dist else None

    x1, ffn1_saved, carried = _ffn_forward(x, small["ffn1_pre"], small["ffn1_post"], wts["ffn1_wg"], wts["ffn1_wu"],
                                           wts.get("ffn1_wd"), "ffn1", riding(GATHER_IN_FFN1_UP),
                                           riding(GATHER_IN_FFN1_DOWN))
    wts.update(zip(GATHER_IN_FFN1_UP + GATHER_IN_FFN1_DOWN, carried))
    (un,) = _rowwise(_norm_fn, [(x1, 0)], [(small["mix_pre"], None)], [BF16], "mix_pre", tr, D, 1)
    if dist:
        proj, *carried = _mm(un, wts["w_in"], "nn", F32, "proj", rider=riding(GATHER_IN_PROJ))
        wts.update(zip(GATHER_IN_PROJ, carried))
        z, *carried = _mm(un, wts["w_gate"], "nt", F32, "gate_logits", rider=riding(GATHER_IN_GATE))
        wts.update(zip(GATHER_IN_GATE, carried))
    else:
        proj = _mm(un, wts["w_in"], "nn", F32, "proj")
        z = _mm(un, wts["w_gate"], "nt", F32, "gate_logits")
    (memn,) = _rowwise(_norm_fn, [(mem, 0)], [(small["mem_norm"], None)], [BF16], "mem_norm", mem.shape[0], D, 1)
    mem_kv = _mm(memn, wts["w_mem_kv"], "nn", F32, "mem_kv")

    o_raw, states, *carried = _hgrn_fwd(proj, small["hgrn_lb"], riding(GATHER_IN_HGRN))
    wts.update(zip(GATHER_IN_HGRN, carried))
    (o_h,) = _rowwise(_hpost_fn, [(o_raw, 0), (proj, CB_HOG)], [(small["hgrn_gnorm"], 0)], [BF16], "hgrn_post",
                      tr, HD, NH)
    ct, cq = _fox_cum(proj, fb_pad)
    win = _fox_windows(proj, cq)
    o_f, lse = _fox_fwd(win, proj, ct, cq)
    o_m = _mem_fwd(proj, mem_kv)

    yh = _mm(o_h, wts["w_hgrn_out"], "nt", F32, "hgrn_out")
    yf = _mm(o_f, wts["w_fox_out"], "nt", F32, "fox_out")
    ym = _mm(o_m, wts["w_mem_out"], "nt", F32, "mem_out")
    zc = D // 512
    merge_rows = [(z, 0), (z, zc), (z, 2 * zc), (yh, 0), (yf, 0), (ym, 0)]
    (merged,) = _rowwise(_merge_fn, merge_rows, [], [BF16], "merge", tr, 512, zc)
    m = _mm(merged, wts["w_o"], "nn", F32, "mix_out")
    (x2,) = _rowwise(functools.partial(_resid_fn, 1.0), [(x1, 0), (m, 0)], [(small["mix_post"], None)], [F32], "mix_post",
                     tr, D, 1)
    x3, ffn2_saved, carried = _ffn_forward(x2, small["ffn2_pre"], small["ffn2_post"], wts["ffn2_wg"], wts["ffn2_wu"],
                                           wts.get("ffn2_wd"), "ffn2", riding(GATHER_IN_FFN2_UP))
    wts.update(zip(GATHER_IN_FFN2_UP, carried))
    dy, loss_part = _loss(x3, tgt, "loss")

    gw, gs, reduced = {}, {}, {}

    def pair_sums(names, tag):
        if not dist:
            return None, None
        theirs = brought.get(tag)
        if theirs is None:
            theirs = _exchange_in_chip([gw[n] for n in names], [blocks[n].shape[0] for n in names], f"reduce_in_chip_{tag}")
        pairs = [_pair_sum(gw[n], t_, f"pair_{n}") for n, t_ in zip(names, theirs)]
        return pairs, _Rider(_ChipExchange([p.shape for p in pairs]), pairs, EXCHANGE_STEPS)

    def chip_sums(names, pairs, recv):
        for n, p_, r_ in zip(names, pairs or (), recv):
            reduced[n] = _sum_chips(p_, r_, f"sum_{n}")

    brought = {}

    def in_chip_rider(names):
        grads_ = [gw[n] for n in names]
        return _Rider(_InChip([g_.shape for g_ in grads_], [blocks[n].shape[0] for n in names]), grads_, EXCHANGE_STEPS)

    def ffn2_exchange(dwg, dwu, dwd):
        gw.update(ffn2_wg=dwg, ffn2_wu=dwu, ffn2_wd=dwd)
        return in_chip_rider(GROUP_FFN2), None, lambda got_a, got_b: brought.update(ffn2=got_a)

    dx2, (gw["ffn2_wg"], gw["ffn2_wu"], gw["ffn2_wd"]), gs["ffn2_pre"], gs["ffn2_post"], _ = _ffn_backward(
        dy, ffn2_saved, small["ffn2_pre"], small["ffn2_post"], wts["ffn2_wg"], wts["ffn2_wu"], wts["ffn2_wd"], "ffn2",
        exchange=ffn2_exchange if dist else None)
    pairs_ffn2, ride_ffn2_grads = pair_sums(GROUP_FFN2, "ffn2")

    dm, gs["mix_post"] = _rowwise_bwd(functools.partial(_resid_h_fn, 1.0), [(m, 0)], [(small["mix_post"], None)],
                                      [(dx2, 0)], [0], [BF16], "mix_post_bwd", tr, D, 1)
    dmerged = _mm(dm, wts["w_o"], "nt", F32, "d_merged")
    gw["w_o"] = _mm(merged, dm, "tn", BF16, "d_w_o")
    dz0, dz1, dz2, dyh, dyf, dym = _rowwise_bwd(_merge_fn, merge_rows, [], [(dmerged, 0)], [0, 1, 2, 3, 4, 5], [BF16] * 6,
                                                "merge_bwd", tr, 512, zc)
    dz = jnp.concatenate([dz0, dz1, dz2], axis=1)
    gw["w_gate"] = _mm(dz, un, "tn", BF16, "d_w_gate")
    dun = _mm(dz, wts["w_gate"], "nn", F32, "d_un_gate")

    do_h = _mm(dyh, wts["w_hgrn_out"], "nn", F32, "d_o_h")
    gw["w_hgrn_out"] = _mm(dyh, o_h, "tn", BF16, "d_w_hgrn_out")
    do_f = _mm(dyf, wts["w_fox_out"], "nn", F32, "d_o_f")
    gw["w_fox_out"] = _mm(dyf, o_f, "tn", BF16, "d_w_fox_out")
    do_m = _mm(dym, wts["w_mem_out"], "nn", F32, "d_o_m")
    gw["w_mem_out"] = _mm(dym, o_m, "tn", BF16, "d_w_mem_out")

    do_raw, dhog, gs["hgrn_gnorm"] = _rowwise_bwd(_hpost_fn, [(o_raw, 0), (proj, CB_HOG)], [(small["hgrn_gnorm"], 0)],
                                                  [(do_h, 0)], [0, 1], [F32, BF16], "hgrn_post_bwd", tr, HD, NH)
    dhq, dhf, dhi, gs["hgrn_lb"], *carried = _hgrn_bwd(proj, small["hgrn_lb"], states, do_raw, ride_ffn2_grads)
    chip_sums(GROUP_FFN2, pairs_ffn2, carried)
    dfq, delta = _fox_bwd_dq(win, proj, ct, cq, lse, do_f)
    dfk, dfv, dc = _fox_bwd_dkv(win, proj, ct, cq, lse, delta, do_f)
    dff, dfb = _fox_cum_bwd(dc, proj, fb_pad)
    gs["fox_fb"] = dfb
    dmq, dmk, dmv = _mem_bwd(proj, mem_kv, do_m)

    dproj = jnp.concatenate([dhq, dhf, dhi, dhog, dfq, dfk, dfv, dff, dmq, jnp.zeros((T, HD), BF16)], axis=1)
    gw["w_in"] = _mm(un, dproj, "tn", BF16, "d_w_in")
    dun = _mm(dproj, wts["w_in"], "nt", F32, "d_un_proj", add=dun)
    dx1, gs["mix_pre"] = _rowwise_bwd(_norm_res_fn, [(x1, 0)], [(small["mix_pre"], None)], [(dx2, 0), (dun, 0)], [0], [F32],
                                      "mix_pre_bwd", tr, D, 1)

    dmem_kv = jnp.concatenate([dmk, dmv], axis=1)
    gw["w_mem_kv"] = _mm(memn, dmem_kv, "tn", BF16, "d_w_mem_kv")
    dmemn = _mm(dmem_kv, wts["w_mem_kv"], "nt", F32, "d_memn")
    _, gs["mem_norm"] = _rowwise_bwd(_norm_fn, [(mem, 0)], [(small["mem_norm"], None)], [(dmemn, 0)], [0], [BF16],
                                     "mem_norm_bwd", mem.shape[0], D, 1)

    mix = {}

    def mix_after_dwd(got):
        brought.update(mix=got)
        mix["pairs"], rider = pair_sums(GROUP_MIX, "mix")
        return rider

    def own_exchange(dwg, dwu, dwd):
        gw.update(ffn1_wg=dwg, ffn1_wu=dwu, ffn1_wd=dwd)
        pairs, _ = pair_sums(GROUP_FFN1, "ffn1")
        first, second = pairs[:2], pairs[2:]

        def take(got_a, got_b):
            chip_sums(GROUP_FFN1, pairs, list(got_a) + list(got_b))

        return (_Rider(_ChipExchange([p.shape for p in first]), first, EXCHANGE_STEPS),
                _Rider(_ChipExchange([p.shape for p in second]), second, EXCHANGE_STEPS), take)

    dx, (gw["ffn1_wg"], gw["ffn1_wu"], gw["ffn1_wd"]), gs["ffn1_pre"], gs["ffn1_post"], carried = _ffn_backward(
        dx1, ffn1_saved, small["ffn1_pre"], small["ffn1_post"], wts["ffn1_wg"], wts["ffn1_wu"], wts["ffn1_wd"], "ffn1",
        exchange=own_exchange if dist else None, rider_dwd=in_chip_rider(GROUP_MIX) if dist else None,
        after_dwd=mix_after_dwd if dist else None)
    chip_sums(GROUP_MIX, mix.get("pairs"), carried)
    return loss_part, dx, (reduced if dist else gw), gs


BIG = ("ffn1_wg", "ffn1_wu", "ffn1_wd", "w_in", "w_mem_kv", "w_hgrn_out", "w_fox_out", "w_mem_out", "w_gate", "w_o",
       "ffn2_wg", "ffn2_wu", "ffn2_wd")
TRANSPOSED = ("ffn1_wg", "ffn1_wu", "ffn2_wg", "ffn2_wu", "w_hgrn_out", "w_fox_out", "w_mem_out", "w_gate")
FFN_PAD = {"ffn1_wg": FP - F, "ffn1_wu": FP - F, "ffn1_wd": FP - F, "ffn2_wg": FP - F, "ffn2_wu": FP - F,
           "ffn2_wd": FP - F}
SMALL = GAINS + ("hgrn_lb", "hgrn_gnorm", "fox_fb")
WEIGHTS = ("ffn1_pre", "ffn1_post", "ffn1_wg", "ffn1_wu", "ffn1_wd", "mix_pre", "mix_post", "mem_norm", "w_in", "hgrn_lb",
           "hgrn_gnorm", "fox_fb", "w_mem_kv", "w_hgrn_out", "w_fox_out", "w_mem_out", "w_gate", "w_o", "ffn2_pre",
           "ffn2_post", "ffn2_wg", "ffn2_wu", "ffn2_wd")


def _to_gather_layout(name, w):
    if name in TRANSPOSED:
        w = w.T
    if name == "w_in":
        r = w.shape[0]
        w = jnp.concatenate([w[:, :MQ_COL], jnp.zeros((r, FF_COL + HD - MQ_COL), w.dtype), w[:, MQ_COL:],
                             jnp.zeros((r, P - FF_COL - HD - WM), w.dtype)], axis=1)
    return w.astype(BF16)


def _from_gather_layout(name, g):
    if name == "w_in":
        g = jnp.concatenate([g[:, :MQ_COL], g[:, FF_COL + HD:FF_COL + HD + WM]], axis=1)
    if name in TRANSPOSED:
        g = g.T
    return g


def kernel(x, mem, ffn1_pre, ffn1_post, ffn1_wg, ffn1_wu, ffn1_wd, mix_pre, mix_post, mem_norm, w_in, hgrn_lb, hgrn_gnorm, fox_fb, w_mem_kv, w_hgrn_out, w_fox_out, w_mem_out, w_gate, w_o, ffn2_pre, ffn2_post, ffn2_wg, ffn2_wu, ffn2_wd, loss_target, m_ffn1_pre, m_ffn1_post, m_ffn1_wg, m_ffn1_wu, m_ffn1_wd, m_mix_pre, m_mix_post, m_mem_norm, m_w_in, m_hgrn_lb, m_hgrn_gnorm, m_fox_fb, m_w_mem_kv, m_w_hgrn_out, m_w_fox_out, m_w_mem_out, m_w_gate, m_w_o, m_ffn2_pre, m_ffn2_post, m_ffn2_wg, m_ffn2_wu, m_ffn2_wd, v_ffn1_pre, v_ffn1_post, v_ffn1_wg, v_ffn1_wu, v_ffn1_wd, v_mix_pre, v_mix_post, v_mem_norm, v_w_in, v_hgrn_lb, v_hgrn_gnorm, v_fox_fb, v_w_mem_kv, v_w_hgrn_out, v_w_fox_out, v_w_mem_out, v_w_gate, v_w_o, v_ffn2_pre, v_ffn2_post, v_ffn2_wg, v_ffn2_wu, v_ffn2_wd):
    a = dict(locals())
    small = {n: a[n] for n in SMALL}
    shard = {n: a[n][0] if a[n].ndim == 3 else a[n] for n in BIG}

    blocks = {n: _to_gather_layout(n, shard[n]) for n in BIG}
    loss_part, dx, reduced, gs = _local_step(x[0], mem[0], loss_target[0], small, blocks=blocks)
    loss = lax.psum(0.5 / D * jnp.sum(loss_part), ("x", "y", "c"))

    grads, deltas, new_m, new_v = {}, {}, {}, {}
    for n in BIG:
        g = _from_gather_layout(n, reduced[n])
        d, m2, v2 = _adamw(shard[n], g, a["m_" + n].reshape(g.shape), a["v_" + n].reshape(g.shape), f"adamw_{n}")
        full = a[n].shape
        grads[n], deltas[n], new_m[n], new_v[n] = g.reshape(full), d.reshape(full), m2.reshape(full), v2.reshape(full)

    part = jnp.concatenate([_rows8(gs[n]) for n in GAINS] + [_rows8(gs["hgrn_lb"]), _rows8(gs["hgrn_gnorm"]),
                                                             _rows8(gs["fox_fb"][:, :NH])], axis=0)
    gsum = _all_reduce_small(part)

    def packed(prefix):
        lb = a[prefix + "hgrn_lb"]
        return _pack_small([a[prefix + n] for n in GAINS], lb[0], lb[1], a[prefix + "hgrn_gnorm"], a[prefix + "fox_fb"])

    g_p, d_p, m_p, v_p = _small_update(gsum, packed(""), packed("m_"), packed("v_"))
    for dst, p in ((grads, g_p), (deltas, d_p), (new_m, m_p), (new_v, v_p)):
        dst.update(_unpack_small(p))

    return (loss, dx[None], *[grads[n] for n in WEIGHTS], *[deltas[n] for n in WEIGHTS],
            *[new_m[n] for n in WEIGHTS], *[new_v[n] for n in WEIGHTS])
```

```python
import functools

import jax
import jax.numpy as jnp
from jax import lax
from jax.experimental import pallas as pl
from jax.experimental.pallas import tpu as pltpu

F32 = jnp.float32
BF16 = jnp.bfloat16
HIGHEST = lax.Precision.HIGHEST

NDEV = 8
D = 2048
F = 5504
FP = 5632
HD = 128
NH = 6
NM = 4
WH = NH * HD
WM = NM * HD
P = 6144
FF_COL = 5376
MQ_COL = 5382
CHUNK = 64
EPS = 1e-6
SCALE = HD ** -0.5
NEG = -1e30
VMEM_LIMIT = 48 * 1024 * 1024

CB_HQ, CB_HF, CB_HI, CB_HOG, CB_FQ, CB_FK, CB_FV, CB_FF, CB_MQ = 0, 6, 12, 18, 24, 30, 36, 42, 43

ADAM_LR, ADAM_B1, ADAM_B2, ADAM_EPS, ADAM_WD, ADAM_STEP = 0.001, 0.9, 0.999, 1e-08, 0.01, 10

NT = (((1,), (1,)), ((), ()))
NN = (((1,), (0,)), ((), ()))
TN = (((0,), (0,)), ((), ()))
MESH = pl.DeviceIdType.MESH


def _params(sem=None, **kw):
    return pltpu.CompilerParams(dimension_semantics=sem, vmem_limit_bytes=VMEM_LIMIT, **kw)


def _tile(n, prefs):
    for p in prefs:
        if p <= n and n % p == 0:
            return p
    return n


def _dot(a, b, dims):
    return lax.dot_general(a.astype(BF16), b.astype(BF16), dims, preferred_element_type=F32)


def _mm(a, b, mode, out_dtype, name, add=None, rider=None):
    if mode == "nn":
        (M, K), (K2, N) = a.shape, b.shape
    elif mode == "nt":
        (M, K), (N, K2) = a.shape, b.shape
    else:
        (K, M), (K2, N) = a.shape, b.shape
    assert K == K2, (a.shape, b.shape, mode)
    if mode == "tn":
        tm = _tile(M, (512, 256, 128))
        tn = _tile(N, (1024, 768, 512, 256, 128))
        tk = _tile(K, (4096, 2048, 1024, 512, 256, 128))
    else:
        tm = _tile(M, (1024, 512, 256, 128)) if K <= 2048 else _tile(M, (512, 256, 128))
        tn = _tile(N, (512, 768, 256, 128))
        tk = K if K <= 6144 else _tile(K, (2048, 1024, 512, 256, 128))
    nk = K // tk
    dims = {"nn": NN, "nt": NT, "tn": TN}[mode]
    has_add = add is not None

    ni, nj = M // tm, N // tn
    n_in = 3 if has_add else 2

    def body(*refs):
        step = (pl.program_id(0) * nj + pl.program_id(1)) * nk + pl.program_id(2)
        refs = _carry(rider, refs, n_in, 1, 1, step, ni * nj * nk)
        a_ref, b_ref = refs[0], refs[1]
        c_ref = refs[2] if has_add else None
        o_ref = refs[3] if has_add else refs[2]
        acc_ref = refs[-1]
        k = pl.program_id(2)
        part = _dot(a_ref[...], b_ref[...], dims)

        def finish(r):
            if has_add:
                r = r + c_ref[...].astype(F32)
            o_ref[...] = r.astype(o_ref.dtype)

        if nk == 1:
            finish(part)
        else:
            @pl.when(k == 0)
            def _():
                acc_ref[...] = part

            @pl.when(k > 0)
            def _():
                acc_ref[...] += part

            @pl.when(k == nk - 1)
            def _():
                finish(acc_ref[...])

    if mode == "nn":
        a_spec = pl.BlockSpec((tm, tk), lambda i, j, k: (i, k))
        b_spec = pl.BlockSpec((tk, tn), lambda i, j, k: (k, j))
    elif mode == "nt":
        a_spec = pl.BlockSpec((tm, tk), lambda i, j, k: (i, k))
        b_spec = pl.BlockSpec((tn, tk), lambda i, j, k: (j, k))
    else:
        a_spec = pl.BlockSpec((tk, tm), lambda i, j, k: (k, i))
        b_spec = pl.BlockSpec((tk, tn), lambda i, j, k: (k, j))
    o_spec = pl.BlockSpec((tm, tn), lambda i, j, k: (i, j))
    args = (a, b) + ((add,) if has_add else ())
    in_specs, out_specs, out_shape, scratch, extra = _with_rider(
        rider, [a_spec, b_spec] + ([o_spec] if has_add else []), [o_spec], [jax.ShapeDtypeStruct((M, N), out_dtype)],
        [pltpu.VMEM((tm, tn) if nk > 1 else (8, 128), F32)])
    out = pl.pallas_call(
        body, name=name, grid=(ni, nj, nk), in_specs=in_specs, out_specs=out_specs, out_shape=out_shape,
        scratch_shapes=scratch,
        compiler_params=_params(("arbitrary",) * 3 if rider else ("parallel", "parallel", "arbitrary"),
                                has_side_effects=rider is not None),
    )(*args, *extra)
    return out if rider else out[0]


class _Rider:
    def __init__(self, plan, inputs, steps):
        self.plan, self.inputs, self.steps = plan, list(inputs), steps
        self.n_out = len(plan.out_shape())
        self.n_sem = len(plan.sems())

    def run(self, step, total, in_refs, out_refs, sem_refs):
        n = self.plan.n
        if isinstance(self.plan, _Gather):
            args = (in_refs[:n], in_refs[n], out_refs) + tuple(sem_refs)
        else:
            args = (in_refs, out_refs) + tuple(sem_refs)
        for frac, method in self.steps:
            @pl.when(step == int(frac * (total - 1)))
            def _(method=method):
                getattr(self.plan, method)(*args)


GATHER_STEPS = ((0.0, "start"), (0.6, "forward"), (1.0, "finish"))
EXCHANGE_STEPS = ((0.0, "start"), (1.0, "finish"))


def _carry(rider, refs, n_in, n_out, n_scratch, step, total):
    if rider is None:
        return refs
    ri, ro, rs = len(rider.inputs), rider.n_out, rider.n_sem
    own_in, rid_in = refs[:n_in], refs[n_in:n_in + ri]
    own_out, rid_out = refs[n_in + ri:n_in + ri + n_out], refs[n_in + ri + n_out:n_in + ri + n_out + ro]
    own_scr, rid_sem = refs[n_in + ri + n_out + ro:n_in + ri + n_out + ro + n_scratch], refs[len(refs) - rs:]
    rider.run(step, total, rid_in, rid_out, rid_sem)
    return tuple(own_in) + tuple(own_out) + tuple(own_scr)


def _with_rider(rider, in_specs, out_specs, out_shape, scratch):
    if rider is None:
        return in_specs, out_specs, out_shape, scratch, ()
    any_spec = pl.BlockSpec(memory_space=pl.ANY)
    return (list(in_specs) + [any_spec] * len(rider.inputs), list(out_specs) + [any_spec] * rider.n_out,
            list(out_shape) + rider.plan.out_shape(), list(scratch) + rider.plan.sems(), tuple(rider.inputs))


def _ffn_up(n, wg_t, wu_t, name, rider=None):
    T = n.shape[0]
    tm = _tile(T, (1024, 512, 256, 128))
    tn = 512
    ni, nj = T // tm, FP // tn

    def body(*refs):
        step = pl.program_id(0) * nj + pl.program_id(1)
        n_ref, wg_ref, wu_ref, g_ref, u_ref, a_ref = _carry(rider, refs, 3, 3, 0, step, ni * nj)
        x = n_ref[...]
        g = _dot(x, wg_ref[...], NT)
        u = _dot(x, wu_ref[...], NT)
        g_ref[...] = g
        u_ref[...] = u
        a_ref[...] = (g * jax.nn.sigmoid(g) * u).astype(BF16)

    w_spec = pl.BlockSpec((tn, D), lambda i, j: (j, 0))
    o_spec = pl.BlockSpec((tm, tn), lambda i, j: (i, j))
    in_specs, out_specs, out_shape, scratch, extra = _with_rider(
        rider, [pl.BlockSpec((tm, D), lambda i, j: (i, 0)), w_spec, w_spec], [o_spec, o_spec, o_spec],
        [jax.ShapeDtypeStruct((T, FP), F32), jax.ShapeDtypeStruct((T, FP), F32), jax.ShapeDtypeStruct((T, FP), BF16)], [])
    return pl.pallas_call(
        body, name=name, grid=(ni, nj), in_specs=in_specs, out_specs=out_specs, out_shape=out_shape,
        scratch_shapes=scratch,
        compiler_params=_params(("arbitrary", "arbitrary") if rider else ("parallel", "parallel"),
                                has_side_effects=rider is not None),
    )(n, wg_t, wu_t, *extra)


def _ffn_act_bwd(dh, wd, g, u, name, rider=None):
    T = dh.shape[0]
    tm = _tile(T, (1024, 512, 256, 128))
    tn = 512
    ni, nj = T // tm, FP // tn

    def body(*refs):
        step = pl.program_id(0) * nj + pl.program_id(1)
        dh_ref, wd_ref, g_ref, u_ref, dg_ref, du_ref = _carry(rider, refs, 4, 2, 0, step, ni * nj)
        da = _dot(dh_ref[...], wd_ref[...], NT)
        g = g_ref[...]
        sg = jax.nn.sigmoid(g)
        dg_ref[...] = (da * u_ref[...] * (sg * (1.0 + g * (1.0 - sg)))).astype(dg_ref.dtype)
        du_ref[...] = (da * (g * sg)).astype(du_ref.dtype)

    tile = pl.BlockSpec((tm, tn), lambda i, j: (i, j))
    in_specs, out_specs, out_shape, scratch, extra = _with_rider(
        rider, [pl.BlockSpec((tm, D), lambda i, j: (i, 0)), pl.BlockSpec((tn, D), lambda i, j: (j, 0)), tile, tile],
        [tile, tile], [jax.ShapeDtypeStruct((T, FP), BF16), jax.ShapeDtypeStruct((T, FP), BF16)], [])
    return pl.pallas_call(
        body, name=name, grid=(ni, nj), in_specs=in_specs, out_specs=out_specs, out_shape=out_shape,
        scratch_shapes=scratch,
        compiler_params=_params(("arbitrary", "arbitrary") if rider else ("parallel", "parallel"),
                                has_side_effects=rider is not None),
    )(dh, wd, g, u, *extra)


def _row_specs(rows, tr, cw):
    return [pl.BlockSpec((tr, cw), lambda j, i, o=off: (i, o + j)) for _, off in rows]


def _const_specs(consts, cw):
    specs = []
    for arr, off in consts:
        if off is None:
            specs.append(pl.BlockSpec(arr.shape, lambda j, i: (0, 0)))
        else:
            specs.append(pl.BlockSpec((arr.shape[0], cw), lambda j, i, o=off: (0, o + j)))
    return specs


def _rowwise(fn, rows, consts, out_dtypes, name, tr, cw, ncol):
    T = rows[0][0].shape[0]
    nr, nc = len(rows), len(consts)

    def body(*refs):
        r = [x[...].astype(F32) for x in refs[:nr]]
        c = [x[...] for x in refs[nr:nr + nc]]
        res = fn(*r, *c)
        for o_ref, v in zip(refs[nr + nc:], res):
            o_ref[...] = v.astype(o_ref.dtype)

    o_spec = pl.BlockSpec((tr, cw), lambda j, i: (i, j))
    return pl.pallas_call(
        body, name=name, grid=(ncol, T // tr),
        in_specs=_row_specs(rows, tr, cw) + _const_specs(consts, cw),
        out_specs=[o_spec] * len(out_dtypes),
        out_shape=[jax.ShapeDtypeStruct((T, ncol * cw), dt) for dt in out_dtypes],
        compiler_params=_params(("parallel", "parallel")),
    )(*[a for a, _ in rows], *[a for a, _ in consts])


def _rowwise_bwd(fn, rows, consts, cots, diff, ddtypes, name, tr, cw, ncol):
    T = rows[0][0].shape[0]
    nr, nc, nt, nd = len(rows), len(consts), len(cots), len(diff)

    def body(*refs):
        r = [x[...].astype(F32) for x in refs[:nr]]
        c = [x[...] for x in refs[nr:nr + nc]]
        ct = [x[...].astype(F32) for x in refs[nr + nc:nr + nc + nt]]
        drow_refs = refs[nr + nc + nt:nr + nc + nt + nd]
        dconst_refs = refs[nr + nc + nt + nd:]
        i = pl.program_id(1)

        def f(*args):
            full = list(r)
            for idx, a in zip(diff, args[:nd]):
                full[idx] = a
            return tuple(fn(*full, *args[nd:]))

        _, vjp = jax.vjp(f, *[r[d] for d in diff], *c)
        g = vjp(tuple(ct))
        for o_ref, v in zip(drow_refs, g[:nd]):
            o_ref[...] = v.astype(o_ref.dtype)

        @pl.when(i == 0)
        def _():
            for o_ref in dconst_refs:
                o_ref[...] = jnp.zeros_like(o_ref)

        for o_ref, v in zip(dconst_refs, g[nd:]):
            o_ref[...] += v

    o_spec = pl.BlockSpec((tr, cw), lambda j, i: (i, j))
    out_shape = [jax.ShapeDtypeStruct((T, ncol * cw), dt) for dt in ddtypes]
    out_shape += [jax.ShapeDtypeStruct(a.shape, F32) for a, _ in consts]
    return pl.pallas_call(
        body, name=name, grid=(ncol, T // tr),
        in_specs=_row_specs(rows, tr, cw) + _const_specs(consts, cw) + _row_specs(cots, tr, cw),
        out_specs=[o_spec] * nd + _const_specs(consts, cw),
        out_shape=out_shape,
        compiler_params=_params(("parallel", "arbitrary")),
    )(*[a for a, _ in rows], *[a for a, _ in consts], *[a for a, _ in cots])


def _rms(x, g):
    return x * lax.rsqrt(jnp.mean(x * x, axis=-1, keepdims=True) + EPS) * g


def _silu(x):
    return x * jax.nn.sigmoid(x)


def _norm_fn(x, g):
    return (_rms(x, g),)


def _norm_res_fn(x, g):
    return (x, _rms(x, g))


def _post_pre_fn(scale, x, h, g_post, g_pre):
    xn = x + scale * _rms(h, g_post)
    return (xn, _rms(xn, g_pre))


def _resid_h_fn(scale, h, g):
    return (scale * _rms(h, g),)


def _hpost_fn(o, hog, gn):
    return (_rms(o, gn) * _silu(hog),)


def _merge_fn(z0, z1, z2, yh, yf, ym):
    return (jax.nn.sigmoid(z0) * yh + jax.nn.sigmoid(z1) * yf + jax.nn.sigmoid(z2) * ym,)


def _loss(x2, h, g_post, tgt, name):
    T = x2.shape[0]
    tr = _tile(T, (256, 128))

    def body(x_ref, h_ref, g_ref, t_ref, dy_ref, s_ref):
        i = pl.program_id(0)
        e = x_ref[...] + 0.5 * _rms(h_ref[...], g_ref[...]) - t_ref[...]
        dy_ref[...] = e * (1.0 / D)
        col = jnp.sum(e * e, axis=0, keepdims=True)
        tot = col[:, 0:HD]
        for k in range(1, D // HD):
            tot = tot + col[:, k * HD:(k + 1) * HD]

        @pl.when(i == 0)
        def _():
            s_ref[...] = jnp.zeros_like(s_ref)

        s_ref[...] += tot

    spec = pl.BlockSpec((tr, D), lambda i: (i, 0))
    return pl.pallas_call(
        body, name=name, grid=(T // tr,), in_specs=[spec, spec, pl.BlockSpec((1, D), lambda i: (0, 0)), spec],
        out_specs=[spec, pl.BlockSpec((1, HD), lambda i: (0, 0))],
        out_shape=[jax.ShapeDtypeStruct((T, D), F32), jax.ShapeDtypeStruct((1, HD), F32)],
        compiler_params=_params(("arbitrary",)),
    )(x2, h, g_post, tgt)


def _lower_bound(lb_ref):
    a0 = lb_ref[0:1, :]
    a1 = lb_ref[1:2, :]
    mx = jnp.maximum(a0, a1)
    e0 = jnp.exp(a0 - mx)
    return e0 / (e0 + jnp.exp(a1 - mx))


def _hgrn_prep(hq, hf, lb):
    g = lb + (1.0 - lb) * jax.nn.sigmoid(hf)
    return _silu(hq), 1.0 - g, jnp.log(g)


def _tri(n, upper):
    r = lax.broadcasted_iota(jnp.int32, (n, n), 0)
    c = lax.broadcasted_iota(jnp.int32, (n, n), 1)
    return (c >= r) if upper else (c <= r)


def _hgrn_factors(q, k, gl):
    low = _tri(CHUNK, False)
    b = lax.dot_general(low.astype(F32), gl, NN, precision=HIGHEST, preferred_element_type=F32)
    bl = b[CHUNK - 1:CHUNK, :]
    ref = b[CHUNK // 2 - 1:CHUNK // 2, :]
    eb = jnp.exp(b)
    ea = jnp.exp(b - ref)
    ebn = jnp.exp(ref - b)
    ek = jnp.exp(bl - b)
    ebl = jnp.exp(bl)
    return low, eb, ea, ebn, ek, ebl


def _hgrn_fwd(proj, hgrn_lb, rider=None):
    T = proj.shape[0]
    cb = _tile(T, (512, 256, 128, 64))
    nchunk = cb // CHUNK

    def body(*refs):
        hq_ref, hf_ref, hi_ref, lb_ref, o_ref, st_ref, state = _carry(rider, refs, 4, 2, 1, pl.program_id(0), T // cb)

        @pl.when(pl.program_id(0) == 0)
        def _():
            state[...] = jnp.zeros_like(state)

        lb = _lower_bound(lb_ref)

        def chunk(c, carry):
            r0 = pl.multiple_of(c * CHUNK, CHUNK)
            for h in range(NH):
                cols = slice(h * HD, (h + 1) * HD)
                q, k, gl = _hgrn_prep(hq_ref[pl.ds(r0, CHUNK), cols], hf_ref[pl.ds(r0, CHUNK), cols], lb[:, cols])
                v = hi_ref[pl.ds(r0, CHUNK), cols]
                low, eb, ea, ebn, ek, ebl = _hgrn_factors(q, k, gl)
                s_t = state[h]
                st_ref[c, h] = s_t
                pm = jnp.where(low, _dot(q * ea, k * ebn, NT), 0.0)
                o_ref[pl.ds(r0, CHUNK), cols] = _dot(q * eb, s_t, NT) + _dot(pm, v, NN)
                state[h] = s_t * ebl + _dot(v, k * ek, TN)
            return carry

        lax.fori_loop(0, nchunk, chunk, 0)

    def col(off):
        return pl.BlockSpec((cb, WH), lambda i, o=off: (i, o))

    in_specs, out_specs, out_shape, scratch, extra = _with_rider(
        rider, [col(0), col(1), col(2), pl.BlockSpec((2, WH), lambda i: (0, 0))],
        [pl.BlockSpec((cb, WH), lambda i: (i, 0)), pl.BlockSpec((nchunk, NH, HD, HD), lambda i: (i, 0, 0, 0))],
        [jax.ShapeDtypeStruct((T, WH), F32), jax.ShapeDtypeStruct((T // CHUNK, NH, HD, HD), F32)],
        [pltpu.VMEM((NH, HD, HD), F32)])
    return pl.pallas_call(
        body, name="hgrn_fwd", grid=(T // cb,), in_specs=in_specs, out_specs=out_specs, out_shape=out_shape,
        scratch_shapes=scratch, compiler_params=_params(("arbitrary",), has_side_effects=rider is not None),
    )(proj, proj, proj, hgrn_lb, *extra)


def _hgrn_bwd(proj, hgrn_lb, states, do, rider=None):
    T = proj.shape[0]
    cb = _tile(T, (512, 256, 128, 64))
    nchunk = cb // CHUNK
    nb = T // cb

    def body(*refs):
        (hq_ref, hf_ref, hi_ref, lb_ref, st_ref, do_ref, dhq_ref, dhf_ref, dhi_ref, dlb_ref,
         dstate) = _carry(rider, refs, 6, 4, 1, pl.program_id(0), nb)

        @pl.when(pl.program_id(0) == 0)
        def _():
            dstate[...] = jnp.zeros_like(dstate)
            dlb_ref[...] = jnp.zeros_like(dlb_ref)

        lb = _lower_bound(lb_ref)
        up = _tri(CHUNK, True)
        last = lax.broadcasted_iota(jnp.int32, (CHUNK, HD), 0) == CHUNK - 1

        def chunk(cc, carry):
            c = nchunk - 1 - cc
            r0 = pl.multiple_of(c * CHUNK, CHUNK)
            for h in range(NH):
                cols = slice(h * HD, (h + 1) * HD)
                hq = hq_ref[pl.ds(r0, CHUNK), cols]
                hf = hf_ref[pl.ds(r0, CHUNK), cols]
                (q, k, gl), prep_vjp = jax.vjp(_hgrn_prep, hq, hf, lb[:, cols])
                v = hi_ref[pl.ds(r0, CHUNK), cols]
                d_o = do_ref[pl.ds(r0, CHUNK), cols]
                low, eb, ea, ebn, ek, ebl = _hgrn_factors(q, k, gl)
                s_t = st_ref[c, h]
                ds_new = dstate[h]
                qe, am, bm, kb = q * eb, q * ea, k * ebn, k * ek
                pm_t = jnp.where(up, _dot(bm, am, NT), 0.0)
                dp = jnp.where(low, _dot(d_o, v, NT), 0.0)
                dp_t = jnp.where(up, _dot(v, d_o, NT), 0.0)
                dqe = _dot(d_o, s_t, NN)
                da = _dot(dp, bm, NN)
                db_m = _dot(dp_t, am, NN)
                dkb = _dot(v, ds_new, NN)
                dv = _dot(pm_t, d_o, NN) + _dot(kb, ds_new, NT)
                dq = dqe * eb + da * ea
                dk = db_m * ebn + dkb * ek
                dbl = jnp.sum(dkb * kb, axis=0, keepdims=True) + jnp.sum(ds_new * s_t, axis=0, keepdims=True) * ebl
                db = (dqe * qe + da * am.astype(BF16).astype(F32) - db_m * bm.astype(BF16).astype(F32) - dkb * kb
                      + jnp.where(last, dbl, 0.0))
                dgl = lax.dot_general(up.astype(F32), db, NN, precision=HIGHEST, preferred_element_type=F32)
                dhq, dhf, dlb = prep_vjp((dq, dk, dgl))
                dhq_ref[pl.ds(r0, CHUNK), cols] = dhq.astype(dhq_ref.dtype)
                dhf_ref[pl.ds(r0, CHUNK), cols] = dhf.astype(dhf_ref.dtype)
                dhi_ref[pl.ds(r0, CHUNK), cols] = dv.astype(dhi_ref.dtype)
                dlb_ref[:, cols] += dlb
                dstate[h] = _dot(d_o, qe, TN) + ds_new * ebl
            return carry

        lax.fori_loop(0, nchunk, chunk, 0)

    def col(off):
        return pl.BlockSpec((cb, WH), lambda i, o=off: (nb - 1 - i, o))

    row = pl.BlockSpec((cb, WH), lambda i: (nb - 1 - i, 0))
    in_specs, out_specs, out_shape, scratch, extra = _with_rider(
        rider, [col(0), col(1), col(2), pl.BlockSpec((2, WH), lambda i: (0, 0)),
                pl.BlockSpec((nchunk, NH, HD, HD), lambda i: (nb - 1 - i, 0, 0, 0)), row],
        [row, row, row, pl.BlockSpec((1, WH), lambda i: (0, 0))],
        [jax.ShapeDtypeStruct((T, WH), BF16)] * 3 + [jax.ShapeDtypeStruct((1, WH), F32)], [pltpu.VMEM((NH, HD, HD), F32)])
    return pl.pallas_call(
        body, name="hgrn_bwd", grid=(nb,), in_specs=in_specs, out_specs=out_specs, out_shape=out_shape,
        scratch_shapes=scratch, compiler_params=_params(("arbitrary",), has_side_effects=rider is not None),
    )(proj, proj, proj, hgrn_lb, states, do, *extra)


def _log_sigmoid(z):
    return jnp.minimum(z, 0.0) - jnp.log(1.0 + jnp.exp(-jnp.abs(z)))


def _fox_cum(proj, fb_pad):
    T = proj.shape[0]
    tb = _tile(T, (256, 128))

    def body(ff_ref, fb_ref, ct_ref, cq_ref, carry):
        @pl.when(pl.program_id(0) == 0)
        def _():
            carry[...] = jnp.zeros_like(carry)

        lf = _log_sigmoid(ff_ref[...] + fb_ref[...])
        cs = lax.dot_general(_tri(tb, False).astype(F32), lf, NN, precision=HIGHEST,
                             preferred_element_type=F32) + carry[0:1, :]
        carry[0:1, :] = cs[tb - 1:tb, :]
        ct_ref[...] = cs.T[0:8, :]
        for h in range(NH):
            cq_ref[h] = jnp.broadcast_to(cs[:, h:h + 1], (tb, HD))

    return pl.pallas_call(
        body, name="fox_cum", grid=(T // tb,),
        in_specs=[pl.BlockSpec((tb, HD), lambda i: (i, CB_FF)), pl.BlockSpec((1, HD), lambda i: (0, 0))],
        out_specs=[pl.BlockSpec((8, tb), lambda i: (0, i)), pl.BlockSpec((NH, tb, HD), lambda i: (0, i, 0))],
        out_shape=[jax.ShapeDtypeStruct((8, T), F32), jax.ShapeDtypeStruct((NH, T, HD), F32)],
        scratch_shapes=[pltpu.VMEM((8, HD), F32)],
        compiler_params=_params(("arbitrary",)),
    )(proj, fb_pad)


def _fox_cum_bwd(dc, proj, fb_pad):
    T = proj.shape[0]
    tb = _tile(T, (256, 128))
    nb = T // tb

    def body(dc_ref, ff_ref, fb_ref, dff_ref, dfb_ref, carry):
        @pl.when(pl.program_id(0) == 0)
        def _():
            carry[...] = jnp.zeros_like(carry)
            dfb_ref[...] = jnp.zeros_like(dfb_ref)

        rid = lax.broadcasted_iota(jnp.int32, (8, tb), 0)
        m8 = jnp.zeros((8, tb), F32)
        for h in range(NH):
            m8 = m8 + jnp.where(rid == h, dc_ref[h], 0.0)
        dcb = jnp.concatenate([m8, jnp.zeros((HD - 8, tb), F32)], axis=0).T
        rev = lax.dot_general(_tri(tb, True).astype(F32), dcb, NN, precision=HIGHEST,
                              preferred_element_type=F32) + carry[0:1, :]
        carry[0:1, :] = rev[0:1, :]
        dff = rev * jax.nn.sigmoid(-(ff_ref[...] + fb_ref[...]))
        dff_ref[...] = dff.astype(dff_ref.dtype)
        dfb_ref[...] += jnp.sum(dff, axis=0, keepdims=True)

    return pl.pallas_call(
        body, name="fox_cum_bwd", grid=(nb,),
        in_specs=[pl.BlockSpec((NH, 8, tb), lambda i: (0, 0, nb - 1 - i)),
                  pl.BlockSpec((tb, HD), lambda i: (nb - 1 - i, CB_FF)), pl.BlockSpec((1, HD), lambda i: (0, 0))],
        out_specs=[pl.BlockSpec((tb, HD), lambda i: (nb - 1 - i, 0)), pl.BlockSpec((1, HD), lambda i: (0, 0))],
        out_shape=[jax.ShapeDtypeStruct((T, HD), BF16), jax.ShapeDtypeStruct((1, HD), F32)],
        scratch_shapes=[pltpu.VMEM((8, HD), F32)],
        compiler_params=_params(("arbitrary",)),
    )(dc, proj, fb_pad)


STRIP = 128


def _fox_scores(q, k, cq, ck, i, j, bq, bk, r0=0):
    rows = q.shape[0]
    s = _dot(q, k, NT) * SCALE + (cq - ck)
    diff = lax.broadcasted_iota(jnp.int32, (rows, bk), 1) - lax.broadcasted_iota(jnp.int32, (rows, bk), 0)
    return jnp.where(diff <= i * bq + r0 - j * bk, s, NEG)


def _heads(h):
    return slice(h * HD, (h + 1) * HD)


UNDERFLOW = -120.0


def _fox_windows(proj, cq):
    T = proj.shape[0]
    bq = _tile(T, (512, 256, 128))
    nq = T // bq
    assert nq <= HD

    def body(q_ref, k_ref, cq_ref, jlo_ref, ihi_ref, norm_s, cs_s, ce_s):
        i = pl.program_id(0)

        @pl.when(i == 0)
        def _():
            norm_s[...] = jnp.zeros_like(norm_s)
            cs_s[...] = jnp.zeros_like(cs_s)
            ce_s[...] = jnp.zeros_like(ce_s)

        lane = lax.broadcasted_iota(jnp.int32, (1, HD), 1)
        for h in range(NH):
            for row, ref in ((h, q_ref), (8 + h, k_ref)):
                x = ref[:, _heads(h)]
                biggest = jnp.max(jnp.sum(x * x, axis=1, keepdims=True), axis=0, keepdims=True)
                norm_s[row:row + 1, :] = jnp.maximum(norm_s[row:row + 1, :], jnp.broadcast_to(biggest, (1, HD)))
            cs_s[h, pl.ds(i, 1), :] = cq_ref[h, 0:1, :]
            ce_s[h:h + 1, :] = jnp.where(lane == i, cq_ref[h, bq - 1:bq, :], ce_s[h:h + 1, :])

        @pl.when(i == nq - 1)
        def _():
            rows = lax.broadcasted_iota(jnp.int32, (HD, HD), 0)
            cols = lax.broadcasted_iota(jnp.int32, (HD, HD), 1)
            need = cols == rows
            for h in range(NH):
                slack = 2.05 * SCALE * jnp.sqrt(norm_s[h:h + 1, :] * norm_s[8 + h:9 + h, :])
                bound = cs_s[h] - ce_s[h:h + 1, :] + slack
                need = need | ((bound >= UNDERFLOW) & (cols < rows))
            need = need & (rows < nq) & (cols < nq)
            jlo = jnp.min(jnp.where(need, cols, HD).astype(F32), axis=1, keepdims=True)
            ihi = jnp.max(jnp.where(need, rows, -1).astype(F32), axis=0, keepdims=True)
            jlo_ref[...] = jnp.broadcast_to(jlo, (HD, HD)).astype(jnp.int32)
            ihi_ref[...] = jnp.broadcast_to(ihi, (8, HD)).astype(jnp.int32)

    jlo, ihi = pl.pallas_call(
        body, name="fox_windows", grid=(nq,),
        in_specs=[pl.BlockSpec((bq, WH), lambda i: (i, CB_FQ // NH)), pl.BlockSpec((bq, WH), lambda i: (i, CB_FK // NH)),
                  pl.BlockSpec((NH, bq, HD), lambda i: (0, i, 0))],
        out_specs=[pl.BlockSpec((HD, HD), lambda i: (0, 0)), pl.BlockSpec((8, HD), lambda i: (0, 0))],
        out_shape=[jax.ShapeDtypeStruct((HD, HD), jnp.int32), jax.ShapeDtypeStruct((8, HD), jnp.int32)],
        scratch_shapes=[pltpu.VMEM((16, HD), F32), pltpu.VMEM((NH, HD, HD), F32), pltpu.VMEM((8, HD), F32)],
        compiler_params=_params(("arbitrary",)),
    )(proj, proj, cq)
    return jnp.concatenate([jlo[:nq, 0], ihi[0, :nq]])


def _fox_fwd(win, proj, ct, cq):
    T = proj.shape[0]
    bq = bk = _tile(T, (512, 256, 128))
    nq = nk = T // bq

    def body(win_ref, q_ref, k_ref, v_ref, ct_ref, cq_ref, o_ref, lse_ref, m_s, l_s, acc_s):
        i, jj = pl.program_id(0), pl.program_id(1)
        j = win_ref[i] + jj

        @pl.when(jj == 0)
        def _():
            m_s[...] = jnp.full_like(m_s, NEG)
            l_s[...] = jnp.zeros_like(l_s)
            acc_s[...] = jnp.zeros_like(acc_s)

        @pl.when(j <= i)
        def _():
            for h in range(NH):
                hs = _heads(h)
                k, v, ck = k_ref[:, hs], v_ref[:, hs], ct_ref[h:h + 1, :]
                for r0 in range(0, bq, STRIP):
                    rs = slice(r0, r0 + STRIP)
                    s = _fox_scores(q_ref[rs, hs], k, cq_ref[h, rs, 0:1], ck, i, j, bq, bk, r0)
                    m_prev = m_s[h, rs]
                    m_new = jnp.maximum(m_prev, jnp.max(s, axis=1, keepdims=True))
                    alpha = jnp.exp(m_prev - m_new)
                    p = jnp.exp(s - m_new)
                    l_s[h, rs] = alpha * l_s[h, rs] + jnp.sum(p, axis=1, keepdims=True)
                    acc_s[rs, hs] = alpha * acc_s[rs, hs] + _dot(p, v, NN)
                    m_s[h, rs] = m_new

        @pl.when(jj == nk - 1)
        def _():
            for h in range(NH):
                o_ref[:, _heads(h)] = acc_s[:, _heads(h)] / l_s[h]
                lse_ref[h] = jnp.broadcast_to(m_s[h] + jnp.log(l_s[h]), (bq, HD))

    def key_block(i, jj, win):
        return jnp.minimum(win[i] + jj, i)

    def kv(off):
        return pl.BlockSpec((bk, WH), lambda i, jj, win, o=off // NH: (key_block(i, jj, win), o))

    stat = pl.BlockSpec((NH, bq, HD), lambda i, jj, win: (0, i, 0))
    return pl.pallas_call(
        body, name="fox_fwd",
        grid_spec=pltpu.PrefetchScalarGridSpec(
            num_scalar_prefetch=1, grid=(nq, nk),
            in_specs=[pl.BlockSpec((bq, WH), lambda i, jj, win: (i, CB_FQ // NH)), kv(CB_FK), kv(CB_FV),
                      pl.BlockSpec((8, bk), lambda i, jj, win: (0, key_block(i, jj, win))), stat],
            out_specs=[pl.BlockSpec((bq, WH), lambda i, jj, win: (i, 0)), stat],
            scratch_shapes=[pltpu.VMEM((NH, bq, 1), F32), pltpu.VMEM((NH, bq, 1), F32), pltpu.VMEM((bq, WH), F32)]),
        out_shape=[jax.ShapeDtypeStruct((T, WH), F32), jax.ShapeDtypeStruct((NH, T, HD), F32)],
        compiler_params=_params(("parallel", "arbitrary")),
    )(win, proj, proj, proj, ct, cq)


def _fox_bwd_dq(win, proj, ct, cq, lse, do):
    T = proj.shape[0]
    bq = bk = _tile(T, (512, 256, 128))
    nq = nk = T // bq

    def body(win_ref, q_ref, k_ref, v_ref, ct_ref, cq_ref, lse_ref, do_ref, dq_ref, delta_ref, acc_s, delta_s, psum_s):
        i, jj = pl.program_id(0), pl.program_id(1)
        j = win_ref[i] + jj % nk

        @pl.when(jj == 0)
        def _():
            acc_s[...] = jnp.zeros_like(acc_s)
            delta_s[...] = jnp.zeros_like(delta_s)
            psum_s[...] = jnp.zeros_like(psum_s)

        def probs(h):
            hs = _heads(h)
            k = k_ref[:, hs]
            s = _fox_scores(q_ref[:, hs], k, cq_ref[h, :, 0:1], ct_ref[h:h + 1, :], i, j, bq, bk)
            return k, jnp.exp(s - lse_ref[h, :, 0:1]), _dot(do_ref[:, hs], v_ref[:, hs], NT)

        @pl.when((j <= i) & (jj < nk))
        def _():
            for h in range(NH):
                _, p, dp = probs(h)
                delta_s[h] += jnp.sum(p * dp, axis=1, keepdims=True)
                psum_s[h] += jnp.sum(p, axis=1, keepdims=True)

        @pl.when((j <= i) & (jj >= nk))
        def _():
            for h in range(NH):
                k, p, dp = probs(h)
                ds = p * (dp - delta_s[h] / psum_s[h])
                acc_s[:, _heads(h)] += _dot(ds, k, NN) * SCALE

        @pl.when(jj == 2 * nk - 1)
        def _():
            dq_ref[...] = acc_s[...].astype(dq_ref.dtype)
            for h in range(NH):
                delta_ref[h] = jnp.broadcast_to(delta_s[h] / psum_s[h], (bq, HD))

    def key_block(i, jj, win):
        return jnp.minimum(win[i] + jj % nk, i)

    def kv(off):
        return pl.BlockSpec((bk, WH), lambda i, jj, win, o=off // NH: (key_block(i, jj, win), o))

    qrow = pl.BlockSpec((bq, WH), lambda i, jj, win: (i, 0))
    stat = pl.BlockSpec((NH, bq, HD), lambda i, jj, win: (0, i, 0))
    return pl.pallas_call(
        body, name="fox_bwd_dq",
        grid_spec=pltpu.PrefetchScalarGridSpec(
            num_scalar_prefetch=1, grid=(nq, 2 * nk),
            in_specs=[pl.BlockSpec((bq, WH), lambda i, jj, win: (i, CB_FQ // NH)), kv(CB_FK), kv(CB_FV),
                      pl.BlockSpec((8, bk), lambda i, jj, win: (0, key_block(i, jj, win))), stat, stat, qrow],
            out_specs=[qrow, stat],
            scratch_shapes=[pltpu.VMEM((bq, WH), F32), pltpu.VMEM((NH, bq, 1), F32), pltpu.VMEM((NH, bq, 1), F32)]),
        out_shape=[jax.ShapeDtypeStruct((T, WH), BF16), jax.ShapeDtypeStruct((NH, T, HD), F32)],
        compiler_params=_params(("parallel", "arbitrary")),
    )(win, proj, proj, proj, ct, cq, lse, do)


def _fox_bwd_dkv(win, proj, ct, cq, lse, delta, do):
    T = proj.shape[0]
    bq = bk = _tile(T, (512, 256, 128))
    nq = nk = T // bq

    def body(win_ref, q_ref, k_ref, v_ref, ct_ref, cq_ref, lse_ref, delta_ref, do_ref, dk_ref, dv_ref, dc_ref,
             dk_s, dv_s, dc_s):
        j, ii = pl.program_id(0), pl.program_id(1)
        i = j + ii

        @pl.when(ii == 0)
        def _():
            dk_s[...] = jnp.zeros_like(dk_s)
            dv_s[...] = jnp.zeros_like(dv_s)
            dc_s[...] = jnp.zeros_like(dc_s)

        @pl.when(i <= win_ref[nq + j])
        def _():
            for h in range(NH):
                hs = _heads(h)
                q = q_ref[:, hs]
                d_o = do_ref[:, hs]
                s = _fox_scores(q, k_ref[:, hs], cq_ref[h, :, 0:1], ct_ref[h:h + 1, :], i, j, bq, bk)
                p = jnp.exp(s - lse_ref[h, :, 0:1])
                dv_s[:, hs] += _dot(p, d_o, TN)
                dp = _dot(d_o, v_ref[:, hs], NT)
                ds = p * (dp - delta_ref[h, :, 0:1])
                dk_s[:, hs] += _dot(ds, q, TN) * SCALE
                dc_s[h:h + 1, :] -= jnp.sum(ds, axis=0, keepdims=True)

        @pl.when(ii == nq - 1)
        def _():
            dk_ref[...] = dk_s[...].astype(dk_ref.dtype)
            dv_ref[...] = dv_s[...].astype(dv_ref.dtype)
            for h in range(NH):
                dc_ref[h] = jnp.broadcast_to(dc_s[h:h + 1, :], (8, bk))

    def query_block(j, ii, win):
        return jnp.minimum(j + ii, win[nq + j])

    def kv(off):
        return pl.BlockSpec((bk, WH), lambda j, ii, win, o=off // NH: (j, o))

    qrow = pl.BlockSpec((bq, WH), lambda j, ii, win: (query_block(j, ii, win), 0))
    stat = pl.BlockSpec((NH, bq, HD), lambda j, ii, win: (0, query_block(j, ii, win), 0))
    krow = pl.BlockSpec((bk, WH), lambda j, ii, win: (j, 0))
    return pl.pallas_call(
        body, name="fox_bwd_dkv",
        grid_spec=pltpu.PrefetchScalarGridSpec(
            num_scalar_prefetch=1, grid=(nk, nq),
            in_specs=[pl.BlockSpec((bq, WH), lambda j, ii, win: (query_block(j, ii, win), CB_FQ // NH)), kv(CB_FK),
                      kv(CB_FV), pl.BlockSpec((8, bk), lambda j, ii, win: (0, j)), stat, stat, stat, qrow],
            out_specs=[krow, krow, pl.BlockSpec((NH, 8, bk), lambda j, ii, win: (0, 0, j))],
            scratch_shapes=[pltpu.VMEM((bk, WH), F32), pltpu.VMEM((bk, WH), F32), pltpu.VMEM((8, bk), F32)]),
        out_shape=[jax.ShapeDtypeStruct((T, WH), BF16), jax.ShapeDtypeStruct((T, WH), BF16),
                   jax.ShapeDtypeStruct((NH, 8, T), F32)],
        compiler_params=_params(("parallel", "arbitrary")),
    )(win, proj, proj, proj, ct, cq, lse, delta, do)


def _mem_probs(q, mk):
    s = _dot(q, mk, NT) * SCALE
    e = jnp.exp(s - jnp.max(s, axis=1, keepdims=True))
    return e / jnp.sum(e, axis=1, keepdims=True)


def _mem_fwd(proj, mem_kv):
    T = proj.shape[0]
    tr = _tile(T, (512, 256, 128))
    M = mem_kv.shape[0]

    def body(q_ref, mk_ref, mv_ref, o_ref):
        o_ref[...] = _dot(_mem_probs(q_ref[...], mk_ref[...]), mv_ref[...], NN)

    return pl.pallas_call(
        body, name="mem_fwd", grid=(NM, T // tr),
        in_specs=[pl.BlockSpec((tr, HD), lambda h, i: (i, CB_MQ + h)),
                  pl.BlockSpec((M, HD), lambda h, i: (0, h)), pl.BlockSpec((M, HD), lambda h, i: (0, NM + h))],
        out_specs=pl.BlockSpec((tr, HD), lambda h, i: (i, h)),
        out_shape=jax.ShapeDtypeStruct((T, WM), F32),
        compiler_params=_params(("parallel", "parallel")),
    )(proj, mem_kv, mem_kv)


def _mem_bwd(proj, mem_kv, do):
    T = proj.shape[0]
    tr = _tile(T, (512, 256, 128))
    M = mem_kv.shape[0]

    def body(q_ref, mk_ref, mv_ref, do_ref, dq_ref, dmk_ref, dmv_ref):
        @pl.when(pl.program_id(1) == 0)
        def _():
            dmk_ref[...] = jnp.zeros_like(dmk_ref)
            dmv_ref[...] = jnp.zeros_like(dmv_ref)

        q, mk, d_o = q_ref[...], mk_ref[...], do_ref[...]
        p = _mem_probs(q, mk)
        dmv_ref[...] += _dot(p, d_o, TN)
        dp = _dot(d_o, mv_ref[...], NT)
        ds = p * (dp - jnp.sum(p * dp, axis=1, keepdims=True))
        dq_ref[...] = (_dot(ds, mk, NN) * SCALE).astype(dq_ref.dtype)
        dmk_ref[...] += _dot(ds, q, TN) * SCALE

    acc = pl.BlockSpec((M, HD), lambda h, i: (0, h))
    row = pl.BlockSpec((tr, HD), lambda h, i: (i, h))
    return pl.pallas_call(
        body, name="mem_bwd", grid=(NM, T // tr),
        in_specs=[pl.BlockSpec((tr, HD), lambda h, i: (i, CB_MQ + h)),
                  pl.BlockSpec((M, HD), lambda h, i: (0, h)), pl.BlockSpec((M, HD), lambda h, i: (0, NM + h)), row],
        out_specs=[row, acc, acc],
        out_shape=[jax.ShapeDtypeStruct((T, WM), BF16), jax.ShapeDtypeStruct((M, WM), F32),
                   jax.ShapeDtypeStruct((M, WM), F32)],
        compiler_params=_params(("parallel", "arbitrary")),
    )(proj, mem_kv, mem_kv, do)


def _mesh_place():
    x, y, c = lax.axis_index("x"), lax.axis_index("y"), lax.axis_index("c")
    return x, y, c


CHIP_FLIPS = (4, 2, 6)
CHIP_OF_SLOT = (0,) + CHIP_FLIPS


def _peer(x, y, c, k):
    px = 1 - x if k & 4 else x
    py = 1 - y if k & 2 else y
    pc = 1 - c if k & 1 else c
    return (px, py, pc), 4 * px + 2 * py + pc


class _Gather:
    def __init__(self, shapes, pad_rows):
        self.shapes, self.pad_rows, self.n = shapes, pad_rows, len(shapes)
        self.npad = sum(1 for p in pad_rows if p)

    def zeros(self):
        return jnp.zeros((max(self.pad_rows) or 16, self.shapes[0][1]), BF16)

    def out_shape(self):
        return [jax.ShapeDtypeStruct((NDEV * r + p, c), BF16) for (r, c), p in zip(self.shapes, self.pad_rows)]

    def sems(self):
        return [pltpu.SemaphoreType.DMA((self.n, NDEV - 1)), pltpu.SemaphoreType.DMA((self.n, NDEV - 1)),
                pltpu.SemaphoreType.DMA((self.n + self.npad,))]

    def _copies(self, ins, z_ref, outs, send_sems, recv_sems, loc_sems):
        x, y, c = _mesh_place()
        me = 4 * x + 2 * y + c
        sibling, _ = _peer(x, y, c, 1)
        local, first, arrive, forward = [], [], [], []
        ip = 0
        for w in range(self.n):
            r = ins[w].shape[0]
            dst = outs[w].at[pl.ds(pl.multiple_of(me * r, 16), r), :]
            local.append(functools.partial(pltpu.make_async_copy, ins[w], dst, loc_sems.at[w]))
            if self.pad_rows[w]:
                local.append(functools.partial(pltpu.make_async_copy, z_ref.at[pl.ds(0, self.pad_rows[w]), :],
                                               outs[w].at[pl.ds(NDEV * r, self.pad_rows[w]), :], loc_sems.at[self.n + ip]))
                ip += 1

            def remote(src, dst_, s, to):
                return functools.partial(pltpu.make_async_remote_copy, src_ref=src, dst_ref=dst_, send_sem=send_sems.at[w, s],
                                         recv_sem=recv_sems.at[w, s], device_id=to, device_id_type=MESH)

            for s, k in enumerate((1,) + CHIP_FLIPS):
                first.append(remote(ins[w], dst, s, _peer(x, y, c, k)[0]))
            for s, k in enumerate(CHIP_FLIPS):
                _, pidx = _peer(x, y, c, k)
                rows = outs[w].at[pl.ds(pl.multiple_of(pidx * r, 16), r), :]
                arrive.append(remote(rows, rows, 1 + s, sibling))
                forward.append(remote(rows, rows, 4 + s, sibling))
        return local, first, arrive, forward


    def start(self, *refs):
        local, first, _, _ = self._copies(*refs)
        for make in local + first:
            make().start()

    def forward(self, *refs):
        _, _, arrive, forward = self._copies(*refs)
        for a, f in zip(arrive, forward):
            a().wait_recv()
            f().start()

    def finish(self, *refs):
        local, first, _, forward = self._copies(*refs)
        for make in local + first[0::4] + forward:
            make().wait()
        for s in (1, 2, 3):
            for make in first[s::4]:
                make().wait_send()


def _all_gather(shards, pad_rows):
    n = len(shards)
    plan = _Gather([s.shape for s in shards], pad_rows)

    def body(*refs):
        args = (refs[:n], refs[n], refs[n + 1:2 * n + 1]) + tuple(refs[2 * n + 1:])
        plan.start(*args)
        plan.forward(*args)
        plan.finish(*args)

    any_spec = pl.BlockSpec(memory_space=pl.ANY)
    return pl.pallas_call(
        body, name="all_gather_weights",
        in_specs=[any_spec] * (n + 1), out_specs=[any_spec] * n,
        out_shape=plan.out_shape(),
        scratch_shapes=plan.sems(),
        compiler_params=pltpu.CompilerParams(has_side_effects=True),
    )(*shards, plan.zeros())


def _exchange_in_chip(grads, shard_rows, name):
    n = len(grads)
    plan = _InChip([g.shape for g in grads], shard_rows)

    def body(*refs):
        args = (refs[:n], refs[n:2 * n]) + tuple(refs[2 * n:])
        plan.start(*args)
        plan.finish(*args)

    any_spec = pl.BlockSpec(memory_space=pl.ANY)
    return pl.pallas_call(
        body, name=name,
        in_specs=[any_spec] * n, out_specs=[any_spec] * n, out_shape=plan.out_shape(), scratch_shapes=plan.sems(),
        compiler_params=pltpu.CompilerParams(has_side_effects=True),
    )(*grads)


class _InChip:
    def __init__(self, shapes, shard_rows):
        self.shapes, self.rows, self.n, self.ns = shapes, shard_rows, len(shapes), len(CHIP_OF_SLOT)

    def out_shape(self):
        return [jax.ShapeDtypeStruct((self.ns, r, s[1]), BF16) for s, r in zip(self.shapes, self.rows)]

    def sems(self):
        return [pltpu.SemaphoreType.DMA((self.n, self.ns)), pltpu.SemaphoreType.DMA((self.n, self.ns))]

    def _copies(self, ins, theirs, send_sems, recv_sems):
        x, y, c = _mesh_place()
        sibling, _ = _peer(x, y, c, 1)
        copies = []
        for w in range(self.n):
            r = self.rows[w]
            for s, k in enumerate(CHIP_OF_SLOT):
                _, other = _peer(x, y, c, k | 1)
                copies.append(pltpu.make_async_remote_copy(
                    src_ref=ins[w].at[pl.ds(pl.multiple_of(other * r, 16), r), :], dst_ref=theirs[w].at[s],
                    send_sem=send_sems.at[w, s], recv_sem=recv_sems.at[w, s], device_id=sibling, device_id_type=MESH))
        return copies

    def start(self, *refs):
        for cp in self._copies(*refs):
            cp.start()

    def finish(self, *refs):
        for cp in self._copies(*refs):
            cp.wait()


def _pair_sum(grad, theirs, name):
    ns, r, c = theirs.shape
    tr = r if r * c <= 2 * 1024 * 1024 else _tile(r, (256, 128, 64, 32, 16))
    per_block = r // tr

    def body(a_ref, b_ref, o_ref):
        o_ref[...] = (a_ref[...].astype(F32) + b_ref[...].astype(F32)).astype(o_ref.dtype)

    def owner_rows(s, i):
        x, y, c_ = _mesh_place()
        fx, fy = s % 2, s // 2
        px, py = x + fx - 2 * x * fx, y + fy - 2 * y * fy
        return ((4 * px + 2 * py + c_) * per_block + i, 0)

    slot = pl.BlockSpec((None, tr, c), lambda s, i: (s, i, 0))
    return pl.pallas_call(
        body, name=name, grid=(ns, per_block),
        in_specs=[pl.BlockSpec((tr, c), owner_rows), slot], out_specs=slot,
        out_shape=jax.ShapeDtypeStruct((ns, r, c), theirs.dtype),
        compiler_params=_params(("parallel", "parallel")),
    )(grad, theirs)


def _exchange_between_chips(pairs, name):
    n = len(pairs)
    plan = _ChipExchange([p.shape for p in pairs])

    def body(*refs):
        args = (refs[:n], refs[n:2 * n]) + tuple(refs[2 * n:])
        plan.start(*args)
        plan.finish(*args)

    any_spec = pl.BlockSpec(memory_space=pl.ANY)
    return pl.pallas_call(
        body, name=name,
        in_specs=[any_spec] * n, out_specs=[any_spec] * n,
        out_shape=plan.out_shape(), scratch_shapes=plan.sems(),
        compiler_params=pltpu.CompilerParams(has_side_effects=True),
    )(*pairs)


class _ChipExchange:
    def __init__(self, shapes):
        self.shapes, self.n, self.ns = shapes, len(shapes), len(CHIP_OF_SLOT) - 1

    def out_shape(self):
        return [jax.ShapeDtypeStruct((self.ns,) + tuple(s[1:]), BF16) for s in self.shapes]

    def sems(self):
        return [pltpu.SemaphoreType.DMA((self.n, self.ns)), pltpu.SemaphoreType.DMA((self.n, self.ns))]

    def _copies(self, ins, outs, send_sems, recv_sems):
        x, y, c = _mesh_place()
        copies = []
        for w in range(self.n):
            for s, k in enumerate(CHIP_OF_SLOT[1:]):
                peer, _ = _peer(x, y, c, k)
                copies.append(pltpu.make_async_remote_copy(
                    src_ref=ins[w].at[s + 1], dst_ref=outs[w].at[s], send_sem=send_sems.at[w, s],
                    recv_sem=recv_sems.at[w, s], device_id=peer, device_id_type=MESH))
        return copies

    def start(self, *refs):
        for cp in self._copies(*refs):
            cp.start()

    def finish(self, *refs):
        for cp in self._copies(*refs):
            cp.wait()


def _sum_chips(pair, recv, name):
    _, r, c = recv.shape
    tr = _tile(r, (128, 64, 32, 16))

    def body(p_ref, x_ref, o_ref):
        acc = p_ref[...].astype(F32)
        for s in range(x_ref.shape[0]):
            acc = acc + x_ref[s].astype(F32)
        o_ref[...] = acc

    return pl.pallas_call(
        body, name=name, grid=(r // tr,),
        in_specs=[pl.BlockSpec((None, tr, c), lambda i: (0, i, 0)), pl.BlockSpec((recv.shape[0], tr, c), lambda i: (0, i, 0))],
        out_specs=pl.BlockSpec((tr, c), lambda i: (i, 0)),
        out_shape=jax.ShapeDtypeStruct((r, c), F32),
        compiler_params=_params(("parallel",)),
    )(pair, recv)


def _all_reduce_small(part):
    R, W = part.shape

    def body(x_ref, o_ref, buf, send_sems, recv_sems):
        x, y, c = _mesh_place()
        me = 4 * x + 2 * y + c
        buf[me] = x_ref[...]
        copies = []
        for k in range(1, NDEV):
            peer, _ = _peer(x, y, c, k)
            cp = pltpu.make_async_remote_copy(src_ref=x_ref, dst_ref=buf.at[me], send_sem=send_sems.at[k - 1],
                                              recv_sem=recv_sems.at[k - 1], device_id=peer, device_id_type=MESH)
            cp.start()
            copies.append(cp)
        for cp in copies:
            cp.wait()
        acc = buf[0]
        for d in range(1, NDEV):
            acc = acc + buf[d]
        o_ref[...] = acc

    vm = pl.BlockSpec(memory_space=pltpu.VMEM)
    return pl.pallas_call(
        body, name="all_reduce_small", in_specs=[vm], out_specs=vm,
        out_shape=jax.ShapeDtypeStruct((R, W), F32),
        scratch_shapes=[pltpu.VMEM((NDEV, R, W), F32), pltpu.SemaphoreType.DMA((NDEV - 1,)),
                        pltpu.SemaphoreType.DMA((NDEV - 1,))],
        compiler_params=pltpu.CompilerParams(has_side_effects=True),
    )(part)


def _adam_math(w, g, m, v):
    m2 = ADAM_B1 * m + (1.0 - ADAM_B1) * g
    v2 = ADAM_B2 * v + (1.0 - ADAM_B2) * (g * g)
    m_hat = m2 / (1.0 - ADAM_B1 ** ADAM_STEP)
    v_hat = v2 / (1.0 - ADAM_B2 ** ADAM_STEP)
    delta = -ADAM_LR * (m_hat / (jnp.sqrt(v_hat) + ADAM_EPS) + ADAM_WD * w)
    return delta, m2, v2


def _adamw(w, g, m, v, name):
    r, c = w.shape
    tr = r
    for cand in (1024, 512, 256, 128, 64, 32, 16, 8):
        if r % cand == 0 and cand * c <= 256 * 1024:
            tr = cand
            break

    def body(w_ref, g_ref, m_ref, v_ref, d_ref, m2_ref, v2_ref):
        d_ref[...], m2_ref[...], v2_ref[...] = _adam_math(w_ref[...], g_ref[...], m_ref[...], v_ref[...])

    spec = pl.BlockSpec((tr, c), lambda i: (i, 0))
    return pl.pallas_call(
        body, name=name, grid=(r // tr,), in_specs=[spec] * 4, out_specs=[spec] * 3,
        out_shape=[jax.ShapeDtypeStruct((r, c), F32)] * 3,
        compiler_params=_params(("parallel",)),
    )(w, g, m, v)


GAINS = ("ffn1_pre", "ffn1_post", "mix_pre", "mix_post", "mem_norm", "ffn2_pre", "ffn2_post")
GAIN_ROWS = D // HD
ROW_LB = len(GAINS) * GAIN_ROWS
ROWS_GRAD_IN = ROW_LB + 24
ROWS_PACKED = ROW_LB + 32


def _small_update(gsum, w_p, m_p, v_p):
    def body(g_ref, w_ref, m_ref, v_ref, go_ref, d_ref, m2_ref, v2_ref):
        a0 = w_ref[ROW_LB:ROW_LB + 8, :]
        a1 = w_ref[ROW_LB + 8:ROW_LB + 16, :]
        mx = jnp.maximum(a0, a1)
        e0, e1 = jnp.exp(a0 - mx), jnp.exp(a1 - mx)
        lb = e0 / (e0 + e1)
        da0 = g_ref[ROW_LB:ROW_LB + 8, :] * lb * (1.0 - lb)
        g = jnp.concatenate([g_ref[0:ROW_LB, :], da0, -da0, g_ref[ROW_LB + 8:ROWS_GRAD_IN, :]], axis=0)
        go_ref[...] = g
        d_ref[...], m2_ref[...], v2_ref[...] = _adam_math(w_ref[...], g, m_ref[...], v_ref[...])

    vm = pl.BlockSpec(memory_space=pltpu.VMEM)
    return pl.pallas_call(
        body, name="small_update", in_specs=[vm] * 4, out_specs=[vm] * 4,
        out_shape=[jax.ShapeDtypeStruct((ROWS_PACKED, HD), F32)] * 4,
    )(gsum, w_p, m_p, v_p)


def _rows8(a):
    a = a.reshape(-1)
    rows = -(-a.shape[0] // HD)
    rows8 = -(-rows // 8) * 8
    return jnp.pad(a, (0, rows8 * HD - a.shape[0])).reshape(rows8, HD)


def _pack_small(gains, lb0, lb1, gnorm, fb):
    return jnp.concatenate([_rows8(g) for g in gains] + [_rows8(lb0), _rows8(lb1), _rows8(gnorm), _rows8(fb)], axis=0)


def _unpack_small(p):
    out = {}
    for i, name in enumerate(GAINS):
        out[name] = p[i * GAIN_ROWS:(i + 1) * GAIN_ROWS].reshape(1, D)
    lb0 = p[ROW_LB:ROW_LB + NH].reshape(1, WH)
    lb1 = p[ROW_LB + 8:ROW_LB + 8 + NH].reshape(1, WH)
    out["hgrn_lb"] = jnp.concatenate([lb0, lb1], axis=0)
    out["hgrn_gnorm"] = p[ROW_LB + 16:ROW_LB + 16 + NH].reshape(1, WH)
    out["fox_fb"] = p[ROW_LB + 24:ROW_LB + 25, 0:NH]
    return out


def _ffn_forward(n, wg_t, wu_t, wd, tag, rider=None, rider_down=None):
    g, u, a, *carried = _ffn_up(n, wg_t, wu_t, f"{tag}_up", rider)
    if wd is None:
        wd = carried[0]
    if rider_down is None:
        h = _mm(a, wd, "nn", F32, f"{tag}_down")
    else:
        h, *more = _mm(a, wd, "nn", F32, f"{tag}_down", rider=rider_down)
        carried = carried + more
    return h, (n, g, u, a), carried


def _mm_out(res):
    return (res[0], list(res[1:])) if isinstance(res, (list, tuple)) else (res, [])


def _ffn_backward(dh, saved, wg_t, wu_t, wd, tag, rider=None, exchange=None, rider_dwd=None, after_dwd=None):
    n, g, u, a = saved
    dwd, got = _mm_out(_mm(a, dh, "tn", BF16, f"{tag}_dwd", rider=rider_dwd))
    rider_dwg = None
    if after_dwd is not None:
        rider, rider_dwg = after_dwd(got)
    dg, du, *carried = _ffn_act_bwd(dh, wd, g, u, f"{tag}_act_bwd", rider)
    dwg, got = _mm_out(_mm(dg, n, "tn", BF16, f"{tag}_dwg", rider=rider_dwg))
    carried = carried + got
    dwu = _mm(du, n, "tn", BF16, f"{tag}_dwu")
    if exchange is None:
        dn = _mm(dg, wg_t, "nn", F32, f"{tag}_dn_g")
        dn = _mm(du, wu_t, "nn", F32, f"{tag}_dn_u", add=dn)
    else:
        ride_a, ride_b, take = exchange(dwg, dwu, dwd)
        dn, got_a = _mm_out(_mm(dg, wg_t, "nn", F32, f"{tag}_dn_g", rider=ride_a))
        dn, got_b = _mm_out(_mm(du, wu_t, "nn", F32, f"{tag}_dn_u", add=dn, rider=ride_b))
        take(got_a, got_b)
    return dn, (dwg, dwu, dwd), carried


GATHER_FIRST = ("ffn1_wg", "ffn1_wu")
GATHER_IN_FFN1_UP = ("ffn1_wd", "w_in")
GATHER_IN_FFN1_DOWN = ("w_gate",)
GATHER_IN_PROJ = ("w_mem_kv", "w_hgrn_out", "w_fox_out", "w_mem_out", "w_o")
GATHER_IN_GATE = ("ffn2_wg",)
GATHER_IN_HGRN = ("ffn2_wu",)
GATHER_IN_FFN2_UP = ("ffn2_wd",)
GROUP_FFN1 = ("ffn1_wg", "ffn1_wu", "ffn1_wd")
GROUP_MIX = ("w_in", "w_mem_kv", "w_hgrn_out", "w_fox_out", "w_mem_out", "w_gate", "w_o")
GROUP_FFN2 = ("ffn2_wg", "ffn2_wu", "ffn2_wd")


def _gather_rider(blocks, names):
    plan = _Gather([blocks[n].shape for n in names], [FFN_PAD.get(n, 0) for n in names])
    return _Rider(plan, [blocks[n] for n in names] + [plan.zeros()], GATHER_STEPS)


def _local_step(x, mem, tgt, small, wts=None, blocks=None):
    T = x.shape[0]
    tr = _tile(T, (256, 128))
    fb_pad = jnp.pad(small["fox_fb"], ((0, 0), (0, HD - NH)))
    dist = blocks is not None
    if dist:
        wts = dict(zip(GATHER_FIRST, _all_gather([blocks[n] for n in GATHER_FIRST], [FFN_PAD[n] for n in GATHER_FIRST])))

    def riding(names):
        return _gather_rider(blocks, names) if dist else None

    (n1,) = _rowwise(_norm_fn, [(x, 0)], [(small["ffn1_pre"], None)], [BF16], "ffn1_pre", tr, D, 1)
    h1, ffn1_saved, carried = _ffn_forward(n1, wts["ffn1_wg"], wts["ffn1_wu"], wts.get("ffn1_wd"), "ffn1",
                                           riding(GATHER_IN_FFN1_UP), riding(GATHER_IN_FFN1_DOWN))
    wts.update(zip(GATHER_IN_FFN1_UP + GATHER_IN_FFN1_DOWN, carried))
    ffn1_out = functools.partial(_post_pre_fn, 0.5)
    x1, un = _rowwise(ffn1_out, [(x, 0), (h1, 0)], [(small["ffn1_post"], None), (small["mix_pre"], None)], [F32, BF16],
                      "ffn1_post_mix_pre", tr, D, 1)
    if dist:
        proj, *carried = _mm(un, wts["w_in"], "nn", F32, "proj", rider=riding(GATHER_IN_PROJ))
        wts.update(zip(GATHER_IN_PROJ, carried))
        z, *carried = _mm(un, wts["w_gate"], "nt", F32, "gate_logits", rider=riding(GATHER_IN_GATE))
        wts.update(zip(GATHER_IN_GATE, carried))
    else:
        proj = _mm(un, wts["w_in"], "nn", F32, "proj")
        z = _mm(un, wts["w_gate"], "nt", F32, "gate_logits")
    (memn,) = _rowwise(_norm_fn, [(mem, 0)], [(small["mem_norm"], None)], [BF16], "mem_norm", mem.shape[0], D, 1)
    mem_kv = _mm(memn, wts["w_mem_kv"], "nn", F32, "mem_kv")

    o_raw, states, *carried = _hgrn_fwd(proj, small["hgrn_lb"], riding(GATHER_IN_HGRN))
    wts.update(zip(GATHER_IN_HGRN, carried))
    (o_h,) = _rowwise(_hpost_fn, [(o_raw, 0), (proj, CB_HOG)], [(small["hgrn_gnorm"], 0)], [BF16], "hgrn_post",
                      tr, HD, NH)
    ct, cq = _fox_cum(proj, fb_pad)
    win = _fox_windows(proj, cq)
    o_f, lse = _fox_fwd(win, proj, ct, cq)
    o_m = _mem_fwd(proj, mem_kv)

    yh = _mm(o_h, wts["w_hgrn_out"], "nt", F32, "hgrn_out")
    yf = _mm(o_f, wts["w_fox_out"], "nt", F32, "fox_out")
    ym = _mm(o_m, wts["w_mem_out"], "nt", F32, "mem_out")
    zc = D // 512
    merge_rows = [(z, 0), (z, zc), (z, 2 * zc), (yh, 0), (yf, 0), (ym, 0)]
    (merged,) = _rowwise(_merge_fn, merge_rows, [], [BF16], "merge", tr, 512, zc)
    m = _mm(merged, wts["w_o"], "nn", F32, "mix_out")
    mix_out = functools.partial(_post_pre_fn, 1.0)
    x2, n2 = _rowwise(mix_out, [(x1, 0), (m, 0)], [(small["mix_post"], None), (small["ffn2_pre"], None)], [F32, BF16],
                      "mix_post_ffn2_pre", tr, D, 1)
    h2, ffn2_saved, carried = _ffn_forward(n2, wts["ffn2_wg"], wts["ffn2_wu"], wts.get("ffn2_wd"), "ffn2",
                                           riding(GATHER_IN_FFN2_UP))
    wts.update(zip(GATHER_IN_FFN2_UP, carried))
    dy, loss_part = _loss(x2, h2, small["ffn2_post"], tgt, "loss")

    gw, gs, reduced = {}, {}, {}

    def pair_sums(names, tag):
        if not dist:
            return None, None
        theirs = brought.get(tag)
        if theirs is None:
            theirs = _exchange_in_chip([gw[n] for n in names], [blocks[n].shape[0] for n in names], f"reduce_in_chip_{tag}")
        pairs = [_pair_sum(gw[n], t_, f"pair_{n}") for n, t_ in zip(names, theirs)]
        return pairs, _Rider(_ChipExchange([p.shape for p in pairs]), pairs, EXCHANGE_STEPS)

    def chip_sums(names, pairs, recv):
        for n, p_, r_ in zip(names, pairs or (), recv):
            reduced[n] = _sum_chips(p_, r_, f"sum_{n}")

    brought = {}

    def in_chip_rider(names):
        grads_ = [gw[n] for n in names]
        return _Rider(_InChip([g_.shape for g_ in grads_], [blocks[n].shape[0] for n in names]), grads_, EXCHANGE_STEPS)

    def ffn2_exchange(dwg, dwu, dwd):
        gw.update(ffn2_wg=dwg, ffn2_wu=dwu, ffn2_wd=dwd)
        return in_chip_rider(GROUP_FFN2), None, lambda got_a, got_b: brought.update(ffn2=got_a)

    dh2, gs["ffn2_post"] = _rowwise_bwd(functools.partial(_resid_h_fn, 0.5), [(h2, 0)], [(small["ffn2_post"], None)],
                                        [(dy, 0)], [0], [BF16], "ffn2_post_bwd", tr, D, 1)
    dn2, (gw["ffn2_wg"], gw["ffn2_wu"], gw["ffn2_wd"]), _ = _ffn_backward(
        dh2, ffn2_saved, wts["ffn2_wg"], wts["ffn2_wu"], wts["ffn2_wd"], "ffn2", exchange=ffn2_exchange if dist else None)
    pairs_ffn2, ride_ffn2_grads = pair_sums(GROUP_FFN2, "ffn2")

    dx1, dm, gs["mix_post"], gs["ffn2_pre"] = _rowwise_bwd(
        mix_out, [(x1, 0), (m, 0)], [(small["mix_post"], None), (small["ffn2_pre"], None)], [(dy, 0), (dn2, 0)], [0, 1],
        [F32, BF16], "mix_post_ffn2_pre_bwd", tr, D, 1)
    dmerged = _mm(dm, wts["w_o"], "nt", F32, "d_merged")
    gw["w_o"] = _mm(merged, dm, "tn", BF16, "d_w_o")
    dz0, dz1, dz2, dyh, dyf, dym = _rowwise_bwd(_merge_fn, merge_rows, [], [(dmerged, 0)], [0, 1, 2, 3, 4, 5], [BF16] * 6,
                                                "merge_bwd", tr, 512, zc)
    dz = jnp.concatenate([dz0, dz1, dz2], axis=1)
    gw["w_gate"] = _mm(dz, un, "tn", BF16, "d_w_gate")
    dun = _mm(dz, wts["w_gate"], "nn", F32, "d_un_gate")

    do_h = _mm(dyh, wts["w_hgrn_out"], "nn", F32, "d_o_h")
    gw["w_hgrn_out"] = _mm(dyh, o_h, "tn", BF16, "d_w_hgrn_out")
    do_f = _mm(dyf, wts["w_fox_out"], "nn", F32, "d_o_f")
    gw["w_fox_out"] = _mm(dyf, o_f, "tn", BF16, "d_w_fox_out")
    do_m = _mm(dym, wts["w_mem_out"], "nn", F32, "d_o_m")
    gw["w_mem_out"] = _mm(dym, o_m, "tn", BF16, "d_w_mem_out")

    do_raw, dhog, gs["hgrn_gnorm"] = _rowwise_bwd(_hpost_fn, [(o_raw, 0), (proj, CB_HOG)], [(small["hgrn_gnorm"], 0)],
                                                  [(do_h, 0)], [0, 1], [F32, BF16], "hgrn_post_bwd", tr, HD, NH)
    dhq, dhf, dhi, gs["hgrn_lb"], *carried = _hgrn_bwd(proj, small["hgrn_lb"], states, do_raw, ride_ffn2_grads)
    chip_sums(GROUP_FFN2, pairs_ffn2, carried)
    dfq, delta = _fox_bwd_dq(win, proj, ct, cq, lse, do_f)
    dfk, dfv, dc = _fox_bwd_dkv(win, proj, ct, cq, lse, delta, do_f)
    dff, dfb = _fox_cum_bwd(dc, proj, fb_pad)
    gs["fox_fb"] = dfb
    dmq, dmk, dmv = _mem_bwd(proj, mem_kv, do_m)

    dproj = jnp.concatenate([dhq, dhf, dhi, dhog, dfq, dfk, dfv, dff, dmq, jnp.zeros((T, HD), BF16)], axis=1)
    gw["w_in"] = _mm(un, dproj, "tn", BF16, "d_w_in")
    dun = _mm(dproj, wts["w_in"], "nt", F32, "d_un_proj", add=dun)
    dx0, dh1, gs["ffn1_post"], gs["mix_pre"] = _rowwise_bwd(
        ffn1_out, [(x, 0), (h1, 0)], [(small["ffn1_post"], None), (small["mix_pre"], None)], [(dx1, 0), (dun, 0)], [0, 1],
        [F32, BF16], "ffn1_post_mix_pre_bwd", tr, D, 1)

    dmem_kv = jnp.concatenate([dmk, dmv], axis=1)
    gw["w_mem_kv"] = _mm(memn, dmem_kv, "tn", BF16, "d_w_mem_kv")
    dmemn = _mm(dmem_kv, wts["w_mem_kv"], "nt", F32, "d_memn")
    _, gs["mem_norm"] = _rowwise_bwd(_norm_fn, [(mem, 0)], [(small["mem_norm"], None)], [(dmemn, 0)], [0], [BF16],
                                     "mem_norm_bwd", mem.shape[0], D, 1)

    mix = {}

    def mix_after_dwd(got):
        brought.update(mix=got)
        pairs, _ = pair_sums(GROUP_MIX, "mix")
        mix["names"] = GROUP_MIX[1:] + GROUP_MIX[:1]
        mix["pairs"] = pairs[1:] + pairs[:1]
        return tuple(_Rider(_ChipExchange([p.shape for p in part]), part, EXCHANGE_STEPS) for part in (pairs[1:], pairs[:1]))

    def own_exchange(dwg, dwu, dwd):
        gw.update(ffn1_wg=dwg, ffn1_wu=dwu, ffn1_wd=dwd)
        pairs, _ = pair_sums(GROUP_FFN1, "ffn1")
        first, second = pairs[:2], pairs[2:]

        def take(got_a, got_b):
            chip_sums(GROUP_FFN1, pairs, list(got_a) + list(got_b))

        return (_Rider(_ChipExchange([p.shape for p in first]), first, EXCHANGE_STEPS),
                _Rider(_ChipExchange([p.shape for p in second]), second, EXCHANGE_STEPS), take)

    dn1, (gw["ffn1_wg"], gw["ffn1_wu"], gw["ffn1_wd"]), carried = _ffn_backward(
        dh1, ffn1_saved, wts["ffn1_wg"], wts["ffn1_wu"], wts["ffn1_wd"], "ffn1",
        exchange=own_exchange if dist else None, rider_dwd=in_chip_rider(GROUP_MIX) if dist else None,
        after_dwd=mix_after_dwd if dist else None)
    chip_sums(mix.get("names", ()), mix.get("pairs"), carried)
    dx, gs["ffn1_pre"] = _rowwise_bwd(_norm_res_fn, [(x, 0)], [(small["ffn1_pre"], None)], [(dx0, 0), (dn1, 0)], [0], [F32],
                                      "ffn1_pre_bwd", tr, D, 1)
    return loss_part, dx, (reduced if dist else gw), gs


BIG = ("ffn1_wg", "ffn1_wu", "ffn1_wd", "w_in", "w_mem_kv", "w_hgrn_out", "w_fox_out", "w_mem_out", "w_gate", "w_o",
       "ffn2_wg", "ffn2_wu", "ffn2_wd")
TRANSPOSED = ("ffn1_wg", "ffn1_wu", "ffn2_wg", "ffn2_wu", "w_hgrn_out", "w_fox_out", "w_mem_out", "w_gate")
FFN_PAD = {"ffn1_wg": FP - F, "ffn1_wu": FP - F, "ffn1_wd": FP - F, "ffn2_wg": FP - F, "ffn2_wu": FP - F,
           "ffn2_wd": FP - F}
SMALL = GAINS + ("hgrn_lb", "hgrn_gnorm", "fox_fb")
WEIGHTS = ("ffn1_pre", "ffn1_post", "ffn1_wg", "ffn1_wu", "ffn1_wd", "mix_pre", "mix_post", "mem_norm", "w_in", "hgrn_lb",
           "hgrn_gnorm", "fox_fb", "w_mem_kv", "w_hgrn_out", "w_fox_out", "w_mem_out", "w_gate", "w_o", "ffn2_pre",
           "ffn2_post", "ffn2_wg", "ffn2_wu", "ffn2_wd")


def _to_gather_layout(name, w):
    if name in TRANSPOSED:
        w = w.T
    if name == "w_in":
        r = w.shape[0]
        w = jnp.concatenate([w[:, :MQ_COL], jnp.zeros((r, FF_COL + HD - MQ_COL), w.dtype), w[:, MQ_COL:],
                             jnp.zeros((r, P - FF_COL - HD - WM), w.dtype)], axis=1)
    return w.astype(BF16)


def _from_gather_layout(name, g):
    if name == "w_in":
        g = jnp.concatenate([g[:, :MQ_COL], g[:, FF_COL + HD:FF_COL + HD + WM]], axis=1)
    if name in TRANSPOSED:
        g = g.T
    return g


def kernel(x, mem, ffn1_pre, ffn1_post, ffn1_wg, ffn1_wu, ffn1_wd, mix_pre, mix_post, mem_norm, w_in, hgrn_lb, hgrn_gnorm, fox_fb, w_mem_kv, w_hgrn_out, w_fox_out, w_mem_out, w_gate, w_o, ffn2_pre, ffn2_post, ffn2_wg, ffn2_wu, ffn2_wd, loss_target, m_ffn1_pre, m_ffn1_post, m_ffn1_wg, m_ffn1_wu, m_ffn1_wd, m_mix_pre, m_mix_post, m_mem_norm, m_w_in, m_hgrn_lb, m_hgrn_gnorm, m_fox_fb, m_w_mem_kv, m_w_hgrn_out, m_w_fox_out, m_w_mem_out, m_w_gate, m_w_o, m_ffn2_pre, m_ffn2_post, m_ffn2_wg, m_ffn2_wu, m_ffn2_wd, v_ffn1_pre, v_ffn1_post, v_ffn1_wg, v_ffn1_wu, v_ffn1_wd, v_mix_pre, v_mix_post, v_mem_norm, v_w_in, v_hgrn_lb, v_hgrn_gnorm, v_fox_fb, v_w_mem_kv, v_w_hgrn_out, v_w_fox_out, v_w_mem_out, v_w_gate, v_w_o, v_ffn2_pre, v_ffn2_post, v_ffn2_wg, v_ffn2_wu, v_ffn2_wd):
    a = dict(locals())
    small = {n: a[n] for n in SMALL}
    shard = {n: a[n][0] if a[n].ndim == 3 else a[n] for n in BIG}

    blocks = {n: _to_gather_layout(n, shard[n]) for n in BIG}
    loss_part, dx, reduced, gs = _local_step(x[0], mem[0], loss_target[0], small, blocks=blocks)
    loss = lax.psum(0.5 / D * jnp.sum(loss_part), ("x", "y", "c"))

    grads, deltas, new_m, new_v = {}, {}, {}, {}
    for n in BIG:
        g = _from_gather_layout(n, reduced[n])
        d, m2, v2 = _adamw(shard[n], g, a["m_" + n].reshape(g.shape), a["v_" + n].reshape(g.shape), f"adamw_{n}")
        full = a[n].shape
        grads[n], deltas[n], new_m[n], new_v[n] = g.reshape(full), d.reshape(full), m2.reshape(full), v2.reshape(full)

    part = jnp.concatenate([_rows8(gs[n]) for n in GAINS] + [_rows8(gs["hgrn_lb"]), _rows8(gs["hgrn_gnorm"]),
                                                             _rows8(gs["fox_fb"][:, :NH])], axis=0)
    gsum = _all_reduce_small(part)

    def packed(prefix):
        lb = a[prefix + "hgrn_lb"]
        return _pack_small([a[prefix + n] for n in GAINS], lb[0], lb[1], a[prefix + "hgrn_gnorm"], a[prefix + "fox_fb"])

    g_p, d_p, m_p, v_p = _small_update(gsum, packed(""), packed("m_"), packed("v_"))
    for dst, p in ((grads, g_p), (deltas, d_p), (new_m, m_p), (new_v, v_p)):
        dst.update(_unpack_small(p))

    return (loss, dx[None], *[grads[n] for n in WEIGHTS], *[deltas[n] for n in WEIGHTS],
            *[new_m[n] for n in WEIGHTS], *[new_v[n] for n in WEIGHTS])
```

```python
import functools

import jax
import jax.numpy as jnp
from jax import lax
from jax.experimental import pallas as pl
from jax.experimental.pallas import tpu as pltpu

F32 = jnp.float32
BF16 = jnp.bfloat16
HIGHEST = lax.Precision.HIGHEST

NDEV = 8
D = 2048
F = 5504
FP = 5632
HD = 128
NH = 6
NM = 4
WH = NH * HD
WM = NM * HD
P = 6144
FF_COL = 5376
MQ_COL = 5382
CHUNK = 64
EPS = 1e-6
SCALE = HD ** -0.5
NEG = -1e30
VMEM_LIMIT = 48 * 1024 * 1024

CB_HQ, CB_HF, CB_HI, CB_HOG, CB_FQ, CB_FK, CB_FV, CB_FF, CB_MQ = 0, 6, 12, 18, 24, 30, 36, 42, 43

ADAM_LR, ADAM_B1, ADAM_B2, ADAM_EPS, ADAM_WD, ADAM_STEP = 0.001, 0.9, 0.999, 1e-08, 0.01, 10

NT = (((1,), (1,)), ((), ()))
NN = (((1,), (0,)), ((), ()))
TN = (((0,), (0,)), ((), ()))
MESH = pl.DeviceIdType.MESH


def _params(sem=None, **kw):
    return pltpu.CompilerParams(dimension_semantics=sem, vmem_limit_bytes=VMEM_LIMIT, **kw)


def _tile(n, prefs):
    for p in prefs:
        if p <= n and n % p == 0:
            return p
    return n


def _dot(a, b, dims):
    return lax.dot_general(a.astype(BF16), b.astype(BF16), dims, preferred_element_type=F32)


def _mm(a, b, mode, out_dtype, name, add=None, rider=None):
    if mode == "nn":
        (M, K), (K2, N) = a.shape, b.shape
    elif mode == "nt":
        (M, K), (N, K2) = a.shape, b.shape
    else:
        (K, M), (K2, N) = a.shape, b.shape
    assert K == K2, (a.shape, b.shape, mode)
    if mode == "tn":
        tm = _tile(M, (512, 256, 128))
        tn = _tile(N, (1024, 768, 512, 256, 128))
        tk = _tile(K, (4096, 2048, 1024, 512, 256, 128))
    else:
        tm = _tile(M, (1024, 512, 256, 128)) if K <= 2048 else _tile(M, (512, 256, 128))
        tn = _tile(N, (512, 768, 256, 128))
        tk = K if K <= 6144 else _tile(K, (2048, 1024, 512, 256, 128))
    nk = K // tk
    dims = {"nn": NN, "nt": NT, "tn": TN}[mode]
    has_add = add is not None

    ni, nj = M // tm, N // tn
    n_in = 3 if has_add else 2

    def body(*refs):
        step = (pl.program_id(0) * nj + pl.program_id(1)) * nk + pl.program_id(2)
        refs = _carry(rider, refs, n_in, 1, 1, step, ni * nj * nk)
        a_ref, b_ref = refs[0], refs[1]
        c_ref = refs[2] if has_add else None
        o_ref = refs[3] if has_add else refs[2]
        acc_ref = refs[-1]
        k = pl.program_id(2)
        part = _dot(a_ref[...], b_ref[...], dims)

        def finish(r):
            if has_add:
                r = r + c_ref[...].astype(F32)
            o_ref[...] = r.astype(o_ref.dtype)

        if nk == 1:
            finish(part)
        else:
            @pl.when(k == 0)
            def _():
                acc_ref[...] = part

            @pl.when(k > 0)
            def _():
                acc_ref[...] += part

            @pl.when(k == nk - 1)
            def _():
                finish(acc_ref[...])

    if mode == "nn":
        a_spec = pl.BlockSpec((tm, tk), lambda i, j, k: (i, k))
        b_spec = pl.BlockSpec((tk, tn), lambda i, j, k: (k, j))
    elif mode == "nt":
        a_spec = pl.BlockSpec((tm, tk), lambda i, j, k: (i, k))
        b_spec = pl.BlockSpec((tn, tk), lambda i, j, k: (j, k))
    else:
        a_spec = pl.BlockSpec((tk, tm), lambda i, j, k: (k, i))
        b_spec = pl.BlockSpec((tk, tn), lambda i, j, k: (k, j))
    o_spec = pl.BlockSpec((tm, tn), lambda i, j, k: (i, j))
    args = (a, b) + ((add,) if has_add else ())
    in_specs, out_specs, out_shape, scratch, extra = _with_rider(
        rider, [a_spec, b_spec] + ([o_spec] if has_add else []), [o_spec], [jax.ShapeDtypeStruct((M, N), out_dtype)],
        [pltpu.VMEM((tm, tn) if nk > 1 else (8, 128), F32)])
    out = pl.pallas_call(
        body, name=name, grid=(ni, nj, nk), in_specs=in_specs, out_specs=out_specs, out_shape=out_shape,
        scratch_shapes=scratch,
        compiler_params=_params(("arbitrary",) * 3 if rider else ("parallel", "parallel", "arbitrary"),
                                has_side_effects=rider is not None),
    )(*args, *extra)
    return out if rider else out[0]


class _Rider:
    def __init__(self, plan, inputs, steps):
        self.plan, self.inputs, self.steps = plan, list(inputs), steps
        self.n_out = len(plan.out_shape())
        self.n_sem = len(plan.sems())

    def run(self, step, total, in_refs, out_refs, sem_refs):
        n = self.plan.n
        if isinstance(self.plan, _Gather):
            args = (in_refs[:n], in_refs[n], out_refs) + tuple(sem_refs)
        else:
            args = (in_refs, out_refs) + tuple(sem_refs)
        for frac, method in self.steps:
            @pl.when(step == int(frac * (total - 1)))
            def _(method=method):
                getattr(self.plan, method)(*args)


GATHER_STEPS = ((0.0, "start"), (0.6, "forward"), (1.0, "finish"))
EXCHANGE_STEPS = ((0.0, "start"), (1.0, "finish"))


def _carry(rider, refs, n_in, n_out, n_scratch, step, total):
    if rider is None:
        return refs
    ri, ro, rs = len(rider.inputs), rider.n_out, rider.n_sem
    own_in, rid_in = refs[:n_in], refs[n_in:n_in + ri]
    own_out, rid_out = refs[n_in + ri:n_in + ri + n_out], refs[n_in + ri + n_out:n_in + ri + n_out + ro]
    own_scr, rid_sem = refs[n_in + ri + n_out + ro:n_in + ri + n_out + ro + n_scratch], refs[len(refs) - rs:]
    rider.run(step, total, rid_in, rid_out, rid_sem)
    return tuple(own_in) + tuple(own_out) + tuple(own_scr)


def _with_rider(rider, in_specs, out_specs, out_shape, scratch):
    if rider is None:
        return in_specs, out_specs, out_shape, scratch, ()
    any_spec = pl.BlockSpec(memory_space=pl.ANY)
    return (list(in_specs) + [any_spec] * len(rider.inputs), list(out_specs) + [any_spec] * rider.n_out,
            list(out_shape) + rider.plan.out_shape(), list(scratch) + rider.plan.sems(), tuple(rider.inputs))


def _ffn_up(n, wg_t, wu_t, name, rider=None):
    T = n.shape[0]
    tm = _tile(T, (1024, 512, 256, 128))
    tn = 512
    ni, nj = T // tm, FP // tn

    def body(*refs):
        step = pl.program_id(0) * nj + pl.program_id(1)
        n_ref, wg_ref, wu_ref, g_ref, u_ref, a_ref = _carry(rider, refs, 3, 3, 0, step, ni * nj)
        x = n_ref[...]
        g = _dot(x, wg_ref[...], NT)
        u = _dot(x, wu_ref[...], NT)
        g_ref[...] = g
        u_ref[...] = u
        a_ref[...] = (g * jax.nn.sigmoid(g) * u).astype(BF16)

    w_spec = pl.BlockSpec((tn, D), lambda i, j: (j, 0))
    o_spec = pl.BlockSpec((tm, tn), lambda i, j: (i, j))
    in_specs, out_specs, out_shape, scratch, extra = _with_rider(
        rider, [pl.BlockSpec((tm, D), lambda i, j: (i, 0)), w_spec, w_spec], [o_spec, o_spec, o_spec],
        [jax.ShapeDtypeStruct((T, FP), F32), jax.ShapeDtypeStruct((T, FP), F32), jax.ShapeDtypeStruct((T, FP), BF16)], [])
    return pl.pallas_call(
        body, name=name, grid=(ni, nj), in_specs=in_specs, out_specs=out_specs, out_shape=out_shape,
        scratch_shapes=scratch,
        compiler_params=_params(("arbitrary", "arbitrary") if rider else ("parallel", "parallel"),
                                has_side_effects=rider is not None),
    )(n, wg_t, wu_t, *extra)


def _ffn_act_bwd(dh, wd, g, u, name, rider=None):
    T = dh.shape[0]
    tm = _tile(T, (1024, 512, 256, 128))
    tn = 512
    ni, nj = T // tm, FP // tn

    def body(*refs):
        step = pl.program_id(0) * nj + pl.program_id(1)
        dh_ref, wd_ref, g_ref, u_ref, dg_ref, du_ref = _carry(rider, refs, 4, 2, 0, step, ni * nj)
        da = _dot(dh_ref[...], wd_ref[...], NT)
        g = g_ref[...]
        sg = jax.nn.sigmoid(g)
        dg_ref[...] = (da * u_ref[...] * (sg * (1.0 + g * (1.0 - sg)))).astype(dg_ref.dtype)
        du_ref[...] = (da * (g * sg)).astype(du_ref.dtype)

    tile = pl.BlockSpec((tm, tn), lambda i, j: (i, j))
    in_specs, out_specs, out_shape, scratch, extra = _with_rider(
        rider, [pl.BlockSpec((tm, D), lambda i, j: (i, 0)), pl.BlockSpec((tn, D), lambda i, j: (j, 0)), tile, tile],
        [tile, tile], [jax.ShapeDtypeStruct((T, FP), BF16), jax.ShapeDtypeStruct((T, FP), BF16)], [])
    return pl.pallas_call(
        body, name=name, grid=(ni, nj), in_specs=in_specs, out_specs=out_specs, out_shape=out_shape,
        scratch_shapes=scratch,
        compiler_params=_params(("arbitrary", "arbitrary") if rider else ("parallel", "parallel"),
                                has_side_effects=rider is not None),
    )(dh, wd, g, u, *extra)


def _row_specs(rows, tr, cw):
    return [pl.BlockSpec((tr, cw), lambda j, i, o=off: (i, o + j)) for _, off in rows]


def _const_specs(consts, cw):
    specs = []
    for arr, off in consts:
        if off is None:
            specs.append(pl.BlockSpec(arr.shape, lambda j, i: (0, 0)))
        else:
            specs.append(pl.BlockSpec((arr.shape[0], cw), lambda j, i, o=off: (0, o + j)))
    return specs


def _rowwise(fn, rows, consts, out_dtypes, name, tr, cw, ncol):
    T = rows[0][0].shape[0]
    nr, nc = len(rows), len(consts)

    def body(*refs):
        r = [x[...].astype(F32) for x in refs[:nr]]
        c = [x[...] for x in refs[nr:nr + nc]]
        res = fn(*r, *c)
        for o_ref, v in zip(refs[nr + nc:], res):
            o_ref[...] = v.astype(o_ref.dtype)

    o_spec = pl.BlockSpec((tr, cw), lambda j, i: (i, j))
    return pl.pallas_call(
        body, name=name, grid=(ncol, T // tr),
        in_specs=_row_specs(rows, tr, cw) + _const_specs(consts, cw),
        out_specs=[o_spec] * len(out_dtypes),
        out_shape=[jax.ShapeDtypeStruct((T, ncol * cw), dt) for dt in out_dtypes],
        compiler_params=_params(("parallel", "parallel")),
    )(*[a for a, _ in rows], *[a for a, _ in consts])


def _rowwise_bwd(fn, rows, consts, cots, diff, ddtypes, name, tr, cw, ncol):
    T = rows[0][0].shape[0]
    nr, nc, nt, nd = len(rows), len(consts), len(cots), len(diff)

    def body(*refs):
        r = [x[...].astype(F32) for x in refs[:nr]]
        c = [x[...] for x in refs[nr:nr + nc]]
        ct = [x[...].astype(F32) for x in refs[nr + nc:nr + nc + nt]]
        drow_refs = refs[nr + nc + nt:nr + nc + nt + nd]
        dconst_refs = refs[nr + nc + nt + nd:]
        i = pl.program_id(1)

        def f(*args):
            full = list(r)
            for idx, a in zip(diff, args[:nd]):
                full[idx] = a
            return tuple(fn(*full, *args[nd:]))

        _, vjp = jax.vjp(f, *[r[d] for d in diff], *c)
        g = vjp(tuple(ct))
        for o_ref, v in zip(drow_refs, g[:nd]):
            o_ref[...] = v.astype(o_ref.dtype)

        @pl.when(i == 0)
        def _():
            for o_ref in dconst_refs:
                o_ref[...] = jnp.zeros_like(o_ref)

        for o_ref, v in zip(dconst_refs, g[nd:]):
            o_ref[...] += v

    o_spec = pl.BlockSpec((tr, cw), lambda j, i: (i, j))
    out_shape = [jax.ShapeDtypeStruct((T, ncol * cw), dt) for dt in ddtypes]
    out_shape += [jax.ShapeDtypeStruct(a.shape, F32) for a, _ in consts]
    return pl.pallas_call(
        body, name=name, grid=(ncol, T // tr),
        in_specs=_row_specs(rows, tr, cw) + _const_specs(consts, cw) + _row_specs(cots, tr, cw),
        out_specs=[o_spec] * nd + _const_specs(consts, cw),
        out_shape=out_shape,
        compiler_params=_params(("parallel", "arbitrary")),
    )(*[a for a, _ in rows], *[a for a, _ in consts], *[a for a, _ in cots])


def _rms(x, g):
    return x * lax.rsqrt(jnp.mean(x * x, axis=-1, keepdims=True) + EPS) * g


def _silu(x):
    return x * jax.nn.sigmoid(x)


def _norm_fn(x, g):
    return (_rms(x, g),)


def _norm_res_fn(x, g):
    return (x, _rms(x, g))


def _post_pre_fn(scale, x, h, g_post, g_pre):
    xn = x + scale * _rms(h, g_post)
    return (xn, _rms(xn, g_pre))


def _resid_h_fn(scale, h, g):
    return (scale * _rms(h, g),)


def _hpost_fn(o, hog, gn):
    return (_rms(o, gn) * _silu(hog),)


def _merge_fn(z0, z1, z2, yh, yf, ym):
    return (jax.nn.sigmoid(z0) * yh + jax.nn.sigmoid(z1) * yf + jax.nn.sigmoid(z2) * ym,)


def _loss(x2, h, g_post, tgt, name):
    T = x2.shape[0]
    tr = _tile(T, (256, 128))

    def body(x_ref, h_ref, g_ref, t_ref, dy_ref, s_ref):
        i = pl.program_id(0)
        e = x_ref[...] + 0.5 * _rms(h_ref[...], g_ref[...]) - t_ref[...]
        dy_ref[...] = e * (1.0 / D)
        col = jnp.sum(e * e, axis=0, keepdims=True)
        tot = col[:, 0:HD]
        for k in range(1, D // HD):
            tot = tot + col[:, k * HD:(k + 1) * HD]

        @pl.when(i == 0)
        def _():
            s_ref[...] = jnp.zeros_like(s_ref)

        s_ref[...] += tot

    spec = pl.BlockSpec((tr, D), lambda i: (i, 0))
    return pl.pallas_call(
        body, name=name, grid=(T // tr,), in_specs=[spec, spec, pl.BlockSpec((1, D), lambda i: (0, 0)), spec],
        out_specs=[spec, pl.BlockSpec((1, HD), lambda i: (0, 0))],
        out_shape=[jax.ShapeDtypeStruct((T, D), F32), jax.ShapeDtypeStruct((1, HD), F32)],
        compiler_params=_params(("arbitrary",)),
    )(x2, h, g_post, tgt)


def _lower_bound(lb_ref):
    a0 = lb_ref[0:1, :]
    a1 = lb_ref[1:2, :]
    mx = jnp.maximum(a0, a1)
    e0 = jnp.exp(a0 - mx)
    return e0 / (e0 + jnp.exp(a1 - mx))


def _hgrn_prep(hq, hf, lb):
    g = lb + (1.0 - lb) * jax.nn.sigmoid(hf)
    return _silu(hq), 1.0 - g, jnp.log(g)


def _tri(n, upper):
    r = lax.broadcasted_iota(jnp.int32, (n, n), 0)
    c = lax.broadcasted_iota(jnp.int32, (n, n), 1)
    return (c >= r) if upper else (c <= r)


def _hgrn_factors(q, k, gl):
    low = _tri(CHUNK, False)
    b = lax.dot_general(low.astype(F32), gl, NN, precision=HIGHEST, preferred_element_type=F32)
    bl = b[CHUNK - 1:CHUNK, :]
    ref = b[CHUNK // 2 - 1:CHUNK // 2, :]
    eb = jnp.exp(b)
    ea = jnp.exp(b - ref)
    ebn = jnp.exp(ref - b)
    ek = jnp.exp(bl - b)
    ebl = jnp.exp(bl)
    return low, eb, ea, ebn, ek, ebl


def _hgrn_fwd(proj, hgrn_lb, rider=None):
    T = proj.shape[0]
    cb = _tile(T, (512, 256, 128, 64))
    nchunk = cb // CHUNK

    def body(*refs):
        hq_ref, hf_ref, hi_ref, lb_ref, o_ref, st_ref, state = _carry(rider, refs, 4, 2, 1, pl.program_id(0), T // cb)

        @pl.when(pl.program_id(0) == 0)
        def _():
            state[...] = jnp.zeros_like(state)

        lb = _lower_bound(lb_ref)

        def chunk(c, carry):
            r0 = pl.multiple_of(c * CHUNK, CHUNK)
            for h in range(NH):
                cols = slice(h * HD, (h + 1) * HD)
                q, k, gl = _hgrn_prep(hq_ref[pl.ds(r0, CHUNK), cols], hf_ref[pl.ds(r0, CHUNK), cols], lb[:, cols])
                v = hi_ref[pl.ds(r0, CHUNK), cols]
                low, eb, ea, ebn, ek, ebl = _hgrn_factors(q, k, gl)
                s_t = state[h]
                st_ref[c, h] = s_t
                pm = jnp.where(low, _dot(q * ea, k * ebn, NT), 0.0)
                o_ref[pl.ds(r0, CHUNK), cols] = _dot(q * eb, s_t, NT) + _dot(pm, v, NN)
                state[h] = s_t * ebl + _dot(v, k * ek, TN)
            return carry

        lax.fori_loop(0, nchunk, chunk, 0)

    def col(off):
        return pl.BlockSpec((cb, WH), lambda i, o=off: (i, o))

    in_specs, out_specs, out_shape, scratch, extra = _with_rider(
        rider, [col(0), col(1), col(2), pl.BlockSpec((2, WH), lambda i: (0, 0))],
        [pl.BlockSpec((cb, WH), lambda i: (i, 0)), pl.BlockSpec((nchunk, NH, HD, HD), lambda i: (i, 0, 0, 0))],
        [jax.ShapeDtypeStruct((T, WH), F32), jax.ShapeDtypeStruct((T // CHUNK, NH, HD, HD), F32)],
        [pltpu.VMEM((NH, HD, HD), F32)])
    return pl.pallas_call(
        body, name="hgrn_fwd", grid=(T // cb,), in_specs=in_specs, out_specs=out_specs, out_shape=out_shape,
        scratch_shapes=scratch, compiler_params=_params(("arbitrary",), has_side_effects=rider is not None),
    )(proj, proj, proj, hgrn_lb, *extra)


def _hgrn_bwd(proj, hgrn_lb, states, do, rider=None):
    T = proj.shape[0]
    cb = _tile(T, (512, 256, 128, 64))
    nchunk = cb // CHUNK
    nb = T // cb

    def body(*refs):
        (hq_ref, hf_ref, hi_ref, lb_ref, st_ref, do_ref, dhq_ref, dhf_ref, dhi_ref, dlb_ref,
         dstate) = _carry(rider, refs, 6, 4, 1, pl.program_id(0), nb)

        @pl.when(pl.program_id(0) == 0)
        def _():
            dstate[...] = jnp.zeros_like(dstate)
            dlb_ref[...] = jnp.zeros_like(dlb_ref)

        lb = _lower_bound(lb_ref)
        up = _tri(CHUNK, True)
        last = lax.broadcasted_iota(jnp.int32, (CHUNK, HD), 0) == CHUNK - 1

        def chunk(cc, carry):
            c = nchunk - 1 - cc
            r0 = pl.multiple_of(c * CHUNK, CHUNK)
            for h in range(NH):
                cols = slice(h * HD, (h + 1) * HD)
                hq = hq_ref[pl.ds(r0, CHUNK), cols]
                hf = hf_ref[pl.ds(r0, CHUNK), cols]
                (q, k, gl), prep_vjp = jax.vjp(_hgrn_prep, hq, hf, lb[:, cols])
                v = hi_ref[pl.ds(r0, CHUNK), cols]
                d_o = do_ref[pl.ds(r0, CHUNK), cols]
                low, eb, ea, ebn, ek, ebl = _hgrn_factors(q, k, gl)
                s_t = st_ref[c, h]
                ds_new = dstate[h]
                qe, am, bm, kb = q * eb, q * ea, k * ebn, k * ek
                pm_t = jnp.where(up, _dot(bm, am, NT), 0.0)
                dp = jnp.where(low, _dot(d_o, v, NT), 0.0)
                dp_t = jnp.where(up, _dot(v, d_o, NT), 0.0)
                dqe = _dot(d_o, s_t, NN)
                da = _dot(dp, bm, NN)
                db_m = _dot(dp_t, am, NN)
                dkb = _dot(v, ds_new, NN)
                dv = _dot(pm_t, d_o, NN) + _dot(kb, ds_new, NT)
                dq = dqe * eb + da * ea
                dk = db_m * ebn + dkb * ek
                dbl = jnp.sum(dkb * kb, axis=0, keepdims=True) + jnp.sum(ds_new * s_t, axis=0, keepdims=True) * ebl
                db = (dqe * qe + da * am.astype(BF16).astype(F32) - db_m * bm.astype(BF16).astype(F32) - dkb * kb
                      + jnp.where(last, dbl, 0.0))
                dgl = lax.dot_general(up.astype(F32), db, NN, precision=HIGHEST, preferred_element_type=F32)
                dhq, dhf, dlb = prep_vjp((dq, dk, dgl))
                dhq_ref[pl.ds(r0, CHUNK), cols] = dhq.astype(dhq_ref.dtype)
                dhf_ref[pl.ds(r0, CHUNK), cols] = dhf.astype(dhf_ref.dtype)
                dhi_ref[pl.ds(r0, CHUNK), cols] = dv.astype(dhi_ref.dtype)
                dlb_ref[:, cols] += dlb
                dstate[h] = _dot(d_o, qe, TN) + ds_new * ebl
            return carry

        lax.fori_loop(0, nchunk, chunk, 0)

    def col(off):
        return pl.BlockSpec((cb, WH), lambda i, o=off: (nb - 1 - i, o))

    row = pl.BlockSpec((cb, WH), lambda i: (nb - 1 - i, 0))
    in_specs, out_specs, out_shape, scratch, extra = _with_rider(
        rider, [col(0), col(1), col(2), pl.BlockSpec((2, WH), lambda i: (0, 0)),
                pl.BlockSpec((nchunk, NH, HD, HD), lambda i: (nb - 1 - i, 0, 0, 0)), row],
        [row, row, row, pl.BlockSpec((1, WH), lambda i: (0, 0))],
        [jax.ShapeDtypeStruct((T, WH), BF16)] * 3 + [jax.ShapeDtypeStruct((1, WH), F32)], [pltpu.VMEM((NH, HD, HD), F32)])
    return pl.pallas_call(
        body, name="hgrn_bwd", grid=(nb,), in_specs=in_specs, out_specs=out_specs, out_shape=out_shape,
        scratch_shapes=scratch, compiler_params=_params(("arbitrary",), has_side_effects=rider is not None),
    )(proj, proj, proj, hgrn_lb, states, do, *extra)


def _log_sigmoid(z):
    return jnp.minimum(z, 0.0) - jnp.log(1.0 + jnp.exp(-jnp.abs(z)))


def _fox_cum(proj, fb_pad):
    T = proj.shape[0]
    tb = _tile(T, (256, 128))

    def body(ff_ref, fb_ref, ct_ref, cq_ref, carry):
        @pl.when(pl.program_id(0) == 0)
        def _():
            carry[...] = jnp.zeros_like(carry)

        lf = _log_sigmoid(ff_ref[...] + fb_ref[...])
        cs = lax.dot_general(_tri(tb, False).astype(F32), lf, NN, precision=HIGHEST,
                             preferred_element_type=F32) + carry[0:1, :]
        carry[0:1, :] = cs[tb - 1:tb, :]
        ct_ref[...] = cs.T[0:8, :]
        for h in range(NH):
            cq_ref[h] = jnp.broadcast_to(cs[:, h:h + 1], (tb, HD))

    return pl.pallas_call(
        body, name="fox_cum", grid=(T // tb,),
        in_specs=[pl.BlockSpec((tb, HD), lambda i: (i, CB_FF)), pl.BlockSpec((1, HD), lambda i: (0, 0))],
        out_specs=[pl.BlockSpec((8, tb), lambda i: (0, i)), pl.BlockSpec((NH, tb, HD), lambda i: (0, i, 0))],
        out_shape=[jax.ShapeDtypeStruct((8, T), F32), jax.ShapeDtypeStruct((NH, T, HD), F32)],
        scratch_shapes=[pltpu.VMEM((8, HD), F32)],
        compiler_params=_params(("arbitrary",)),
    )(proj, fb_pad)


def _fox_cum_bwd(dc, proj, fb_pad):
    T = proj.shape[0]
    tb = _tile(T, (256, 128))
    nb = T // tb

    def body(dc_ref, ff_ref, fb_ref, dff_ref, dfb_ref, carry):
        @pl.when(pl.program_id(0) == 0)
        def _():
            carry[...] = jnp.zeros_like(carry)
            dfb_ref[...] = jnp.zeros_like(dfb_ref)

        rid = lax.broadcasted_iota(jnp.int32, (8, tb), 0)
        m8 = jnp.zeros((8, tb), F32)
        for h in range(NH):
            m8 = m8 + jnp.where(rid == h, dc_ref[h], 0.0)
        dcb = jnp.concatenate([m8, jnp.zeros((HD - 8, tb), F32)], axis=0).T
        rev = lax.dot_general(_tri(tb, True).astype(F32), dcb, NN, precision=HIGHEST,
                              preferred_element_type=F32) + carry[0:1, :]
        carry[0:1, :] = rev[0:1, :]
        dff = rev * jax.nn.sigmoid(-(ff_ref[...] + fb_ref[...]))
        dff_ref[...] = dff.astype(dff_ref.dtype)
        dfb_ref[...] += jnp.sum(dff, axis=0, keepdims=True)

    return pl.pallas_call(
        body, name="fox_cum_bwd", grid=(nb,),
        in_specs=[pl.BlockSpec((NH, 8, tb), lambda i: (0, 0, nb - 1 - i)),
                  pl.BlockSpec((tb, HD), lambda i: (nb - 1 - i, CB_FF)), pl.BlockSpec((1, HD), lambda i: (0, 0))],
        out_specs=[pl.BlockSpec((tb, HD), lambda i: (nb - 1 - i, 0)), pl.BlockSpec((1, HD), lambda i: (0, 0))],
        out_shape=[jax.ShapeDtypeStruct((T, HD), BF16), jax.ShapeDtypeStruct((1, HD), F32)],
        scratch_shapes=[pltpu.VMEM((8, HD), F32)],
        compiler_params=_params(("arbitrary",)),
    )(dc, proj, fb_pad)


STRIP = 128


def _fox_scores(q, k, cq, ck, i, j, bq, bk, r0=0):
    rows = q.shape[0]
    s = _dot(q, k, NT) * SCALE + (cq - ck)
    diff = lax.broadcasted_iota(jnp.int32, (rows, bk), 1) - lax.broadcasted_iota(jnp.int32, (rows, bk), 0)
    return jnp.where(diff <= i * bq + r0 - j * bk, s, NEG)


def _heads(h):
    return slice(h * HD, (h + 1) * HD)


UNDERFLOW = -120.0


def _fox_windows(proj, cq):
    T = proj.shape[0]
    bq = _tile(T, (512, 256, 128))
    nq = T // bq
    assert nq <= HD

    def body(q_ref, k_ref, cq_ref, jlo_ref, ihi_ref, norm_s, cs_s, ce_s):
        i = pl.program_id(0)

        @pl.when(i == 0)
        def _():
            norm_s[...] = jnp.zeros_like(norm_s)
            cs_s[...] = jnp.zeros_like(cs_s)
            ce_s[...] = jnp.zeros_like(ce_s)

        lane = lax.broadcasted_iota(jnp.int32, (1, HD), 1)
        for h in range(NH):
            for row, ref in ((h, q_ref), (8 + h, k_ref)):
                x = ref[:, _heads(h)]
                biggest = jnp.max(jnp.sum(x * x, axis=1, keepdims=True), axis=0, keepdims=True)
                norm_s[row:row + 1, :] = jnp.maximum(norm_s[row:row + 1, :], jnp.broadcast_to(biggest, (1, HD)))
            cs_s[h, pl.ds(i, 1), :] = cq_ref[h, 0:1, :]
            ce_s[h:h + 1, :] = jnp.where(lane == i, cq_ref[h, bq - 1:bq, :], ce_s[h:h + 1, :])

        @pl.when(i == nq - 1)
        def _():
            rows = lax.broadcasted_iota(jnp.int32, (HD, HD), 0)
            cols = lax.broadcasted_iota(jnp.int32, (HD, HD), 1)
            need = cols == rows
            for h in range(NH):
                slack = 2.05 * SCALE * jnp.sqrt(norm_s[h:h + 1, :] * norm_s[8 + h:9 + h, :])
                bound = cs_s[h] - ce_s[h:h + 1, :] + slack
                need = need | ((bound >= UNDERFLOW) & (cols < rows))
            need = need & (rows < nq) & (cols < nq)
            jlo = jnp.min(jnp.where(need, cols, HD).astype(F32), axis=1, keepdims=True)
            ihi = jnp.max(jnp.where(need, rows, -1).astype(F32), axis=0, keepdims=True)
            jlo_ref[...] = jnp.broadcast_to(jlo, (HD, HD)).astype(jnp.int32)
            ihi_ref[...] = jnp.broadcast_to(ihi, (8, HD)).astype(jnp.int32)

    jlo, ihi = pl.pallas_call(
        body, name="fox_windows", grid=(nq,),
        in_specs=[pl.BlockSpec((bq, WH), lambda i: (i, CB_FQ // NH)), pl.BlockSpec((bq, WH), lambda i: (i, CB_FK // NH)),
                  pl.BlockSpec((NH, bq, HD), lambda i: (0, i, 0))],
        out_specs=[pl.BlockSpec((HD, HD), lambda i: (0, 0)), pl.BlockSpec((8, HD), lambda i: (0, 0))],
        out_shape=[jax.ShapeDtypeStruct((HD, HD), jnp.int32), jax.ShapeDtypeStruct((8, HD), jnp.int32)],
        scratch_shapes=[pltpu.VMEM((16, HD), F32), pltpu.VMEM((NH, HD, HD), F32), pltpu.VMEM((8, HD), F32)],
        compiler_params=_params(("arbitrary",)),
    )(proj, proj, cq)
    return jnp.concatenate([jlo[:nq, 0], ihi[0, :nq]])


def _fox_fwd(win, proj, ct, cq):
    T = proj.shape[0]
    bq = bk = _tile(T, (512, 256, 128))
    nq = nk = T // bq

    def body(win_ref, q_ref, k_ref, v_ref, ct_ref, cq_ref, o_ref, lse_ref, m_s, l_s, acc_s):
        i, jj = pl.program_id(0), pl.program_id(1)
        j = win_ref[i] + jj

        @pl.when(jj == 0)
        def _():
            m_s[...] = jnp.full_like(m_s, NEG)
            l_s[...] = jnp.zeros_like(l_s)
            acc_s[...] = jnp.zeros_like(acc_s)

        @pl.when(j <= i)
        def _():
            for h in range(NH):
                hs = _heads(h)
                k, v, ck = k_ref[:, hs], v_ref[:, hs], ct_ref[h:h + 1, :]
                for r0 in range(0, bq, STRIP):
                    rs = slice(r0, r0 + STRIP)
                    s = _fox_scores(q_ref[rs, hs], k, cq_ref[h, rs, 0:1], ck, i, j, bq, bk, r0)
                    m_prev = m_s[h, rs]
                    m_new = jnp.maximum(m_prev, jnp.max(s, axis=1, keepdims=True))
                    alpha = jnp.exp(m_prev - m_new)
                    p = jnp.exp(s - m_new)
                    l_s[h, rs] = alpha * l_s[h, rs] + jnp.sum(p, axis=1, keepdims=True)
                    acc_s[rs, hs] = alpha * acc_s[rs, hs] + _dot(p, v, NN)
                    m_s[h, rs] = m_new

        @pl.when(jj == nk - 1)
        def _():
            for h in range(NH):
                o_ref[:, _heads(h)] = acc_s[:, _heads(h)] / l_s[h]
                lse_ref[h] = jnp.broadcast_to(m_s[h] + jnp.log(l_s[h]), (bq, HD))

    def key_block(i, jj, win):
        return jnp.minimum(win[i] + jj, i)

    def kv(off):
        return pl.BlockSpec((bk, WH), lambda i, jj, win, o=off // NH: (key_block(i, jj, win), o))

    stat = pl.BlockSpec((NH, bq, HD), lambda i, jj, win: (0, i, 0))
    return pl.pallas_call(
        body, name="fox_fwd",
        grid_spec=pltpu.PrefetchScalarGridSpec(
            num_scalar_prefetch=1, grid=(nq, nk),
            in_specs=[pl.BlockSpec((bq, WH), lambda i, jj, win: (i, CB_FQ // NH)), kv(CB_FK), kv(CB_FV),
                      pl.BlockSpec((8, bk), lambda i, jj, win: (0, key_block(i, jj, win))), stat],
            out_specs=[pl.BlockSpec((bq, WH), lambda i, jj, win: (i, 0)), stat],
            scratch_shapes=[pltpu.VMEM((NH, bq, 1), F32), pltpu.VMEM((NH, bq, 1), F32), pltpu.VMEM((bq, WH), F32)]),
        out_shape=[jax.ShapeDtypeStruct((T, WH), F32), jax.ShapeDtypeStruct((NH, T, HD), F32)],
        compiler_params=_params(("parallel", "arbitrary")),
    )(win, proj, proj, proj, ct, cq)


def _fox_bwd_dq(win, proj, ct, cq, lse, do):
    T = proj.shape[0]
    bq = bk = _tile(T, (512, 256, 128))
    nq = nk = T // bq

    def body(win_ref, q_ref, k_ref, v_ref, ct_ref, cq_ref, lse_ref, do_ref, dq_ref, delta_ref, acc_s, delta_s, psum_s):
        i, jj = pl.program_id(0), pl.program_id(1)
        j = win_ref[i] + jj % nk

        @pl.when(jj == 0)
        def _():
            acc_s[...] = jnp.zeros_like(acc_s)
            delta_s[...] = jnp.zeros_like(delta_s)
            psum_s[...] = jnp.zeros_like(psum_s)

        def probs(h):
            hs = _heads(h)
            k = k_ref[:, hs]
            s = _fox_scores(q_ref[:, hs], k, cq_ref[h, :, 0:1], ct_ref[h:h + 1, :], i, j, bq, bk)
            return k, jnp.exp(s - lse_ref[h, :, 0:1]), _dot(do_ref[:, hs], v_ref[:, hs], NT)

        @pl.when((j <= i) & (jj < nk))
        def _():
            for h in range(NH):
                _, p, dp = probs(h)
                delta_s[h] += jnp.sum(p * dp, axis=1, keepdims=True)
                psum_s[h] += jnp.sum(p, axis=1, keepdims=True)

        @pl.when((j <= i) & (jj >= nk))
        def _():
            for h in range(NH):
                k, p, dp = probs(h)
                ds = p * (dp - delta_s[h] / psum_s[h])
                acc_s[:, _heads(h)] += _dot(ds, k, NN) * SCALE

        @pl.when(jj == 2 * nk - 1)
        def _():
            dq_ref[...] = acc_s[...].astype(dq_ref.dtype)
            for h in range(NH):
                delta_ref[h] = jnp.broadcast_to(delta_s[h] / psum_s[h], (bq, HD))

    def key_block(i, jj, win):
        return jnp.minimum(win[i] + jj % nk, i)

    def kv(off):
        return pl.BlockSpec((bk, WH), lambda i, jj, win, o=off // NH: (key_block(i, jj, win), o))

    qrow = pl.BlockSpec((bq, WH), lambda i, jj, win: (i, 0))
    stat = pl.BlockSpec((NH, bq, HD), lambda i, jj, win: (0, i, 0))
    return pl.pallas_call(
        body, name="fox_bwd_dq",
        grid_spec=pltpu.PrefetchScalarGridSpec(
            num_scalar_prefetch=1, grid=(nq, 2 * nk),
            in_specs=[pl.BlockSpec((bq, WH), lambda i, jj, win: (i, CB_FQ // NH)), kv(CB_FK), kv(CB_FV),
                      pl.BlockSpec((8, bk), lambda i, jj, win: (0, key_block(i, jj, win))), stat, stat, qrow],
            out_specs=[qrow, stat],
            scratch_shapes=[pltpu.VMEM((bq, WH), F32), pltpu.VMEM((NH, bq, 1), F32), pltpu.VMEM((NH, bq, 1), F32)]),
        out_shape=[jax.ShapeDtypeStruct((T, WH), BF16), jax.ShapeDtypeStruct((NH, T, HD), F32)],
        compiler_params=_params(("parallel", "arbitrary")),
    )(win, proj, proj, proj, ct, cq, lse, do)


def _fox_bwd_dkv(win, proj, ct, cq, lse, delta, do):
    T = proj.shape[0]
    bq = bk = _tile(T, (512, 256, 128))
    nq = nk = T // bq

    def body(win_ref, q_ref, k_ref, v_ref, ct_ref, cq_ref, lse_ref, delta_ref, do_ref, dk_ref, dv_ref, dc_ref,
             dk_s, dv_s, dc_s):
        j, ii = pl.program_id(0), pl.program_id(1)
        i = j + ii

        @pl.when(ii == 0)
        def _():
            dk_s[...] = jnp.zeros_like(dk_s)
            dv_s[...] = jnp.zeros_like(dv_s)
            dc_s[...] = jnp.zeros_like(dc_s)

        @pl.when(i <= win_ref[nq + j])
        def _():
            for h in range(NH):
                hs = _heads(h)
                q = q_ref[:, hs]
                d_o = do_ref[:, hs]
                s = _fox_scores(q, k_ref[:, hs], cq_ref[h, :, 0:1], ct_ref[h:h + 1, :], i, j, bq, bk)
                p = jnp.exp(s - lse_ref[h, :, 0:1])
                dv_s[:, hs] += _dot(p, d_o, TN)
                dp = _dot(d_o, v_ref[:, hs], NT)
                ds = p * (dp - delta_ref[h, :, 0:1])
                dk_s[:, hs] += _dot(ds, q, TN) * SCALE
                dc_s[h:h + 1, :] -= jnp.sum(ds, axis=0, keepdims=True)

        @pl.when(ii == nq - 1)
        def _():
            dk_ref[...] = dk_s[...].astype(dk_ref.dtype)
            dv_ref[...] = dv_s[...].astype(dv_ref.dtype)
            for h in range(NH):
                dc_ref[h] = jnp.broadcast_to(dc_s[h:h + 1, :], (8, bk))

    def query_block(j, ii, win):
        return jnp.minimum(j + ii, win[nq + j])

    def kv(off):
        return pl.BlockSpec((bk, WH), lambda j, ii, win, o=off // NH: (j, o))

    qrow = pl.BlockSpec((bq, WH), lambda j, ii, win: (query_block(j, ii, win), 0))
    stat = pl.BlockSpec((NH, bq, HD), lambda j, ii, win: (0, query_block(j, ii, win), 0))
    krow = pl.BlockSpec((bk, WH), lambda j, ii, win: (j, 0))
    return pl.pallas_call(
        body, name="fox_bwd_dkv",
        grid_spec=pltpu.PrefetchScalarGridSpec(
            num_scalar_prefetch=1, grid=(nk, nq),
            in_specs=[pl.BlockSpec((bq, WH), lambda j, ii, win: (query_block(j, ii, win), CB_FQ // NH)), kv(CB_FK),
                      kv(CB_FV), pl.BlockSpec((8, bk), lambda j, ii, win: (0, j)), stat, stat, stat, qrow],
            out_specs=[krow, krow, pl.BlockSpec((NH, 8, bk), lambda j, ii, win: (0, 0, j))],
            scratch_shapes=[pltpu.VMEM((bk, WH), F32), pltpu.VMEM((bk, WH), F32), pltpu.VMEM((8, bk), F32)]),
        out_shape=[jax.ShapeDtypeStruct((T, WH), BF16), jax.ShapeDtypeStruct((T, WH), BF16),
                   jax.ShapeDtypeStruct((NH, 8, T), F32)],
        compiler_params=_params(("parallel", "arbitrary")),
    )(win, proj, proj, proj, ct, cq, lse, delta, do)


def _mem_probs(q, mk):
    s = _dot(q, mk, NT) * SCALE
    e = jnp.exp(s - jnp.max(s, axis=1, keepdims=True))
    return e / jnp.sum(e, axis=1, keepdims=True)


def _mem_fwd(proj, mem_kv):
    T = proj.shape[0]
    tr = _tile(T, (512, 256, 128))
    M = mem_kv.shape[0]

    def body(q_ref, mk_ref, mv_ref, o_ref):
        o_ref[...] = _dot(_mem_probs(q_ref[...], mk_ref[...]), mv_ref[...], NN)

    return pl.pallas_call(
        body, name="mem_fwd", grid=(NM, T // tr),
        in_specs=[pl.BlockSpec((tr, HD), lambda h, i: (i, CB_MQ + h)),
                  pl.BlockSpec((M, HD), lambda h, i: (0, h)), pl.BlockSpec((M, HD), lambda h, i: (0, NM + h))],
        out_specs=pl.BlockSpec((tr, HD), lambda h, i: (i, h)),
        out_shape=jax.ShapeDtypeStruct((T, WM), F32),
        compiler_params=_params(("parallel", "parallel")),
    )(proj, mem_kv, mem_kv)


def _mem_bwd(proj, mem_kv, do):
    T = proj.shape[0]
    tr = _tile(T, (512, 256, 128))
    M = mem_kv.shape[0]

    def body(q_ref, mk_ref, mv_ref, do_ref, dq_ref, dmk_ref, dmv_ref):
        @pl.when(pl.program_id(1) == 0)
        def _():
            dmk_ref[...] = jnp.zeros_like(dmk_ref)
            dmv_ref[...] = jnp.zeros_like(dmv_ref)

        q, mk, d_o = q_ref[...], mk_ref[...], do_ref[...]
        p = _mem_probs(q, mk)
        dmv_ref[...] += _dot(p, d_o, TN)
        dp = _dot(d_o, mv_ref[...], NT)
        ds = p * (dp - jnp.sum(p * dp, axis=1, keepdims=True))
        dq_ref[...] = (_dot(ds, mk, NN) * SCALE).astype(dq_ref.dtype)
        dmk_ref[...] += _dot(ds, q, TN) * SCALE

    acc = pl.BlockSpec((M, HD), lambda h, i: (0, h))
    row = pl.BlockSpec((tr, HD), lambda h, i: (i, h))
    return pl.pallas_call(
        body, name="mem_bwd", grid=(NM, T // tr),
        in_specs=[pl.BlockSpec((tr, HD), lambda h, i: (i, CB_MQ + h)),
                  pl.BlockSpec((M, HD), lambda h, i: (0, h)), pl.BlockSpec((M, HD), lambda h, i: (0, NM + h)), row],
        out_specs=[row, acc, acc],
        out_shape=[jax.ShapeDtypeStruct((T, WM), BF16), jax.ShapeDtypeStruct((M, WM), F32),
                   jax.ShapeDtypeStruct((M, WM), F32)],
        compiler_params=_params(("parallel", "arbitrary")),
    )(proj, mem_kv, mem_kv, do)


def _mesh_place():
    x, y, c = lax.axis_index("x"), lax.axis_index("y"), lax.axis_index("c")
    return x, y, c


CHIP_FLIPS = (4, 2, 6)
CHIP_OF_SLOT = (0,) + CHIP_FLIPS


def _peer(x, y, c, k):
    px = 1 - x if k & 4 else x
    py = 1 - y if k & 2 else y
    pc = 1 - c if k & 1 else c
    return (px, py, pc), 4 * px + 2 * py + pc


class _Gather:
    def __init__(self, shapes, pad_rows):
        self.shapes, self.pad_rows, self.n = shapes, pad_rows, len(shapes)
        self.npad = sum(1 for p in pad_rows if p)

    def zeros(self):
        return jnp.zeros((max(self.pad_rows) or 16, self.shapes[0][1]), BF16)

    def out_shape(self):
        return [jax.ShapeDtypeStruct((NDEV * r + p, c), BF16) for (r, c), p in zip(self.shapes, self.pad_rows)]

    def sems(self):
        return [pltpu.SemaphoreType.DMA((self.n, NDEV - 1)), pltpu.SemaphoreType.DMA((self.n, NDEV - 1)),
                pltpu.SemaphoreType.DMA((self.n + self.npad,))]

    def _copies(self, ins, z_ref, outs, send_sems, recv_sems, loc_sems):
        x, y, c = _mesh_place()
        me = 4 * x + 2 * y + c
        sibling, _ = _peer(x, y, c, 1)
        local, first, arrive, forward = [], [], [], []
        ip = 0
        for w in range(self.n):
            r = ins[w].shape[0]
            dst = outs[w].at[pl.ds(pl.multiple_of(me * r, 16), r), :]
            local.append(functools.partial(pltpu.make_async_copy, ins[w], dst, loc_sems.at[w]))
            if self.pad_rows[w]:
                local.append(functools.partial(pltpu.make_async_copy, z_ref.at[pl.ds(0, self.pad_rows[w]), :],
                                               outs[w].at[pl.ds(NDEV * r, self.pad_rows[w]), :], loc_sems.at[self.n + ip]))
                ip += 1

            def remote(src, dst_, s, to):
                return functools.partial(pltpu.make_async_remote_copy, src_ref=src, dst_ref=dst_, send_sem=send_sems.at[w, s],
                                         recv_sem=recv_sems.at[w, s], device_id=to, device_id_type=MESH)

            for s, k in enumerate((1,) + CHIP_FLIPS):
                first.append(remote(ins[w], dst, s, _peer(x, y, c, k)[0]))
            for s, k in enumerate(CHIP_FLIPS):
                _, pidx = _peer(x, y, c, k)
                rows = outs[w].at[pl.ds(pl.multiple_of(pidx * r, 16), r), :]
                arrive.append(remote(rows, rows, 1 + s, sibling))
                forward.append(remote(rows, rows, 4 + s, sibling))
        return local, first, arrive, forward


    def start(self, *refs):
        local, first, _, _ = self._copies(*refs)
        for make in local + first:
            make().start()

    def forward(self, *refs):
        _, _, arrive, forward = self._copies(*refs)
        for a, f in zip(arrive, forward):
            a().wait_recv()
            f().start()

    def finish(self, *refs):
        local, first, _, forward = self._copies(*refs)
        for make in local + first[0::4] + forward:
            make().wait()
        for s in (1, 2, 3):
            for make in first[s::4]:
                make().wait_send()


def _all_gather(shards, pad_rows):
    n = len(shards)
    plan = _Gather([s.shape for s in shards], pad_rows)

    def body(*refs):
        args = (refs[:n], refs[n], refs[n + 1:2 * n + 1]) + tuple(refs[2 * n + 1:])
        plan.start(*args)
        plan.forward(*args)
        plan.finish(*args)

    any_spec = pl.BlockSpec(memory_space=pl.ANY)
    return pl.pallas_call(
        body, name="all_gather_weights",
        in_specs=[any_spec] * (n + 1), out_specs=[any_spec] * n,
        out_shape=plan.out_shape(),
        scratch_shapes=plan.sems(),
        compiler_params=pltpu.CompilerParams(has_side_effects=True),
    )(*shards, plan.zeros())


def _exchange_in_chip(grads, shard_rows, name):
    n = len(grads)
    plan = _InChip([g.shape for g in grads], shard_rows)

    def body(*refs):
        args = (refs[:n], refs[n:2 * n]) + tuple(refs[2 * n:])
        plan.start(*args)
        plan.finish(*args)

    any_spec = pl.BlockSpec(memory_space=pl.ANY)
    return pl.pallas_call(
        body, name=name,
        in_specs=[any_spec] * n, out_specs=[any_spec] * n, out_shape=plan.out_shape(), scratch_shapes=plan.sems(),
        compiler_params=pltpu.CompilerParams(has_side_effects=True),
    )(*grads)


class _InChip:
    def __init__(self, shapes, shard_rows):
        self.shapes, self.rows, self.n, self.ns = shapes, shard_rows, len(shapes), len(CHIP_OF_SLOT)

    def out_shape(self):
        return [jax.ShapeDtypeStruct((self.ns, r, s[1]), BF16) for s, r in zip(self.shapes, self.rows)]

    def sems(self):
        return [pltpu.SemaphoreType.DMA((self.n, self.ns)), pltpu.SemaphoreType.DMA((self.n, self.ns))]

    def _copies(self, ins, theirs, send_sems, recv_sems):
        x, y, c = _mesh_place()
        sibling, _ = _peer(x, y, c, 1)
        copies = []
        for w in range(self.n):
            r = self.rows[w]
            for s, k in enumerate(CHIP_OF_SLOT):
                _, other = _peer(x, y, c, k | 1)
                copies.append(pltpu.make_async_remote_copy(
                    src_ref=ins[w].at[pl.ds(pl.multiple_of(other * r, 16), r), :], dst_ref=theirs[w].at[s],
                    send_sem=send_sems.at[w, s], recv_sem=recv_sems.at[w, s], device_id=sibling, device_id_type=MESH))
        return copies

    def start(self, *refs):
        for cp in self._copies(*refs):
            cp.start()

    def finish(self, *refs):
        for cp in self._copies(*refs):
            cp.wait()


def _pair_sum(grad, theirs, name):
    ns, r, c = theirs.shape
    tr = r if r * c <= 2 * 1024 * 1024 else _tile(r, (256, 128, 64, 32, 16))
    per_block = r // tr

    def body(a_ref, b_ref, o_ref):
        o_ref[...] = (a_ref[...].astype(F32) + b_ref[...].astype(F32)).astype(o_ref.dtype)

    def owner_rows(s, i):
        x, y, c_ = _mesh_place()
        fx, fy = s % 2, s // 2
        px, py = x + fx - 2 * x * fx, y + fy - 2 * y * fy
        return ((4 * px + 2 * py + c_) * per_block + i, 0)

    slot = pl.BlockSpec((None, tr, c), lambda s, i: (s, i, 0))
    return pl.pallas_call(
        body, name=name, grid=(ns, per_block),
        in_specs=[pl.BlockSpec((tr, c), owner_rows), slot], out_specs=slot,
        out_shape=jax.ShapeDtypeStruct((ns, r, c), theirs.dtype),
        compiler_params=_params(("parallel", "parallel")),
    )(grad, theirs)


def _exchange_between_chips(pairs, name):
    n = len(pairs)
    plan = _ChipExchange([p.shape for p in pairs])

    def body(*refs):
        args = (refs[:n], refs[n:2 * n]) + tuple(refs[2 * n:])
        plan.start(*args)
        plan.finish(*args)

    any_spec = pl.BlockSpec(memory_space=pl.ANY)
    return pl.pallas_call(
        body, name=name,
        in_specs=[any_spec] * n, out_specs=[any_spec] * n,
        out_shape=plan.out_shape(), scratch_shapes=plan.sems(),
        compiler_params=pltpu.CompilerParams(has_side_effects=True),
    )(*pairs)


class _ChipExchange:
    def __init__(self, shapes):
        self.shapes, self.n, self.ns = shapes, len(shapes), len(CHIP_OF_SLOT) - 1

    def out_shape(self):
        return [jax.ShapeDtypeStruct((self.ns,) + tuple(s[1:]), BF16) for s in self.shapes]

    def sems(self):
        return [pltpu.SemaphoreType.DMA((self.n, self.ns)), pltpu.SemaphoreType.DMA((self.n, self.ns))]

    def _copies(self, ins, outs, send_sems, recv_sems):
        x, y, c = _mesh_place()
        copies = []
        for w in range(self.n):
            for s, k in enumerate(CHIP_OF_SLOT[1:]):
                peer, _ = _peer(x, y, c, k)
                copies.append(pltpu.make_async_remote_copy(
                    src_ref=ins[w].at[s + 1], dst_ref=outs[w].at[s], send_sem=send_sems.at[w, s],
                    recv_sem=recv_sems.at[w, s], device_id=peer, device_id_type=MESH))
        return copies

    def start(self, *refs):
        for cp in self._copies(*refs):
            cp.start()

    def finish(self, *refs):
        for cp in self._copies(*refs):
            cp.wait()


def _sum_chips(pair, recv, name):
    ns, r, c = recv.shape
    tr, tc = _panel(r, c)

    def body(p_ref, x_ref, o_ref):
        acc = p_ref[...].astype(F32)
        for s in range(x_ref.shape[0]):
            acc = acc + x_ref[s].astype(F32)
        o_ref[...] = acc

    return pl.pallas_call(
        body, name=name, grid=(r // tr, c // tc),
        in_specs=[pl.BlockSpec((None, tr, tc), lambda i, j: (0, i, j)), pl.BlockSpec((ns, tr, tc), lambda i, j: (0, i, j))],
        out_specs=pl.BlockSpec((tr, tc), lambda i, j: (i, j)),
        out_shape=jax.ShapeDtypeStruct((r, c), F32),
        compiler_params=_params(("parallel", "parallel")),
    )(pair, recv)


def _panel(r, c):
    for tr in (1024, 512, 256, 128):
        if r % tr == 0 and tr * c <= 512 * 1024:
            return tr, c
    for tc in (2048, 1024, 512, 256, 128):
        if c % tc == 0 and r * tc <= 512 * 1024:
            return r, tc
    return _tile(r, (64, 32, 16, 8)), c


def _all_reduce_small(part):
    R, W = part.shape

    def body(x_ref, o_ref, buf, send_sems, recv_sems):
        x, y, c = _mesh_place()
        me = 4 * x + 2 * y + c
        buf[me] = x_ref[...]
        copies = []
        for k in range(1, NDEV):
            peer, _ = _peer(x, y, c, k)
            cp = pltpu.make_async_remote_copy(src_ref=x_ref, dst_ref=buf.at[me], send_sem=send_sems.at[k - 1],
                                              recv_sem=recv_sems.at[k - 1], device_id=peer, device_id_type=MESH)
            cp.start()
            copies.append(cp)
        for cp in copies:
            cp.wait()
        acc = buf[0]
        for d in range(1, NDEV):
            acc = acc + buf[d]
        o_ref[...] = acc

    vm = pl.BlockSpec(memory_space=pltpu.VMEM)
    return pl.pallas_call(
        body, name="all_reduce_small", in_specs=[vm], out_specs=vm,
        out_shape=jax.ShapeDtypeStruct((R, W), F32),
        scratch_shapes=[pltpu.VMEM((NDEV, R, W), F32), pltpu.SemaphoreType.DMA((NDEV - 1,)),
                        pltpu.SemaphoreType.DMA((NDEV - 1,))],
        compiler_params=pltpu.CompilerParams(has_side_effects=True),
    )(part)


def _adam_math(w, g, m, v):
    m2 = ADAM_B1 * m + (1.0 - ADAM_B1) * g
    v2 = ADAM_B2 * v + (1.0 - ADAM_B2) * (g * g)
    m_hat = m2 / (1.0 - ADAM_B1 ** ADAM_STEP)
    v_hat = v2 / (1.0 - ADAM_B2 ** ADAM_STEP)
    delta = -ADAM_LR * (m_hat / (jnp.sqrt(v_hat) + ADAM_EPS) + ADAM_WD * w)
    return delta, m2, v2


def _adamw(w, g, m, v, name):
    r, c = w.shape
    tr, tc = _panel(r, c)

    def body(w_ref, g_ref, m_ref, v_ref, d_ref, m2_ref, v2_ref):
        d_ref[...], m2_ref[...], v2_ref[...] = _adam_math(w_ref[...], g_ref[...], m_ref[...], v_ref[...])

    spec = pl.BlockSpec((tr, tc), lambda i, j: (i, j))
    return pl.pallas_call(
        body, name=name, grid=(r // tr, c // tc), in_specs=[spec] * 4, out_specs=[spec] * 3,
        out_shape=[jax.ShapeDtypeStruct((r, c), F32)] * 3,
        compiler_params=_params(("parallel", "parallel")),
    )(w, g, m, v)


GAINS = ("ffn1_pre", "ffn1_post", "mix_pre", "mix_post", "mem_norm", "ffn2_pre", "ffn2_post")
GAIN_ROWS = D // HD
ROW_LB = len(GAINS) * GAIN_ROWS
ROWS_GRAD_IN = ROW_LB + 24
ROWS_PACKED = ROW_LB + 32


def _small_update(gsum, w_p, m_p, v_p):
    def body(g_ref, w_ref, m_ref, v_ref, go_ref, d_ref, m2_ref, v2_ref):
        a0 = w_ref[ROW_LB:ROW_LB + 8, :]
        a1 = w_ref[ROW_LB + 8:ROW_LB + 16, :]
        mx = jnp.maximum(a0, a1)
        e0, e1 = jnp.exp(a0 - mx), jnp.exp(a1 - mx)
        lb = e0 / (e0 + e1)
        da0 = g_ref[ROW_LB:ROW_LB + 8, :] * lb * (1.0 - lb)
        g = jnp.concatenate([g_ref[0:ROW_LB, :], da0, -da0, g_ref[ROW_LB + 8:ROWS_GRAD_IN, :]], axis=0)
        go_ref[...] = g
        d_ref[...], m2_ref[...], v2_ref[...] = _adam_math(w_ref[...], g, m_ref[...], v_ref[...])

    vm = pl.BlockSpec(memory_space=pltpu.VMEM)
    return pl.pallas_call(
        body, name="small_update", in_specs=[vm] * 4, out_specs=[vm] * 4,
        out_shape=[jax.ShapeDtypeStruct((ROWS_PACKED, HD), F32)] * 4,
    )(gsum, w_p, m_p, v_p)


def _rows8(a):
    a = a.reshape(-1)
    rows = -(-a.shape[0] // HD)
    rows8 = -(-rows // 8) * 8
    return jnp.pad(a, (0, rows8 * HD - a.shape[0])).reshape(rows8, HD)


def _pack_small(gains, lb0, lb1, gnorm, fb):
    return jnp.concatenate([_rows8(g) for g in gains] + [_rows8(lb0), _rows8(lb1), _rows8(gnorm), _rows8(fb)], axis=0)


def _unpack_small(p):
    out = {}
    for i, name in enumerate(GAINS):
        out[name] = p[i * GAIN_ROWS:(i + 1) * GAIN_ROWS].reshape(1, D)
    lb0 = p[ROW_LB:ROW_LB + NH].reshape(1, WH)
    lb1 = p[ROW_LB + 8:ROW_LB + 8 + NH].reshape(1, WH)
    out["hgrn_lb"] = jnp.concatenate([lb0, lb1], axis=0)
    out["hgrn_gnorm"] = p[ROW_LB + 16:ROW_LB + 16 + NH].reshape(1, WH)
    out["fox_fb"] = p[ROW_LB + 24:ROW_LB + 25, 0:NH]
    return out


def _ffn_forward(n, wg_t, wu_t, wd, tag, rider=None, rider_down=None):
    g, u, a, *carried = _ffn_up(n, wg_t, wu_t, f"{tag}_up", rider)
    if wd is None:
        wd = carried[0]
    if rider_down is None:
        h = _mm(a, wd, "nn", F32, f"{tag}_down")
    else:
        h, *more = _mm(a, wd, "nn", F32, f"{tag}_down", rider=rider_down)
        carried = carried + more
    return h, (n, g, u, a), carried


def _mm_out(res):
    return (res[0], list(res[1:])) if isinstance(res, (list, tuple)) else (res, [])


def _ffn_backward(dh, saved, wg_t, wu_t, wd, tag, rider=None, exchange=None, rider_dwd=None, after_dwd=None):
    n, g, u, a = saved
    dwd, got = _mm_out(_mm(a, dh, "tn", BF16, f"{tag}_dwd", rider=rider_dwd))
    rider_dwg = None
    if after_dwd is not None:
        rider, rider_dwg = after_dwd(got)
    dg, du, *carried = _ffn_act_bwd(dh, wd, g, u, f"{tag}_act_bwd", rider)
    dwg, got = _mm_out(_mm(dg, n, "tn", BF16, f"{tag}_dwg", rider=rider_dwg))
    carried = carried + got
    dwu = _mm(du, n, "tn", BF16, f"{tag}_dwu")
    if exchange is None:
        dn = _mm(dg, wg_t, "nn", F32, f"{tag}_dn_g")
        dn = _mm(du, wu_t, "nn", F32, f"{tag}_dn_u", add=dn)
    else:
        ride_a, ride_b, take = exchange(dwg, dwu, dwd)
        dn, got_a = _mm_out(_mm(dg, wg_t, "nn", F32, f"{tag}_dn_g", rider=ride_a))
        dn, got_b = _mm_out(_mm(du, wu_t, "nn", F32, f"{tag}_dn_u", add=dn, rider=ride_b))
        take(got_a, got_b)
    return dn, (dwg, dwu, dwd), carried


GATHER_FIRST = ("ffn1_wg", "ffn1_wu")
GATHER_IN_FFN1_UP = ("ffn1_wd", "w_in")
GATHER_IN_FFN1_DOWN = ("w_gate",)
GATHER_IN_PROJ = ("w_mem_kv", "w_hgrn_out", "w_fox_out", "w_mem_out", "w_o")
GATHER_IN_GATE = ("ffn2_wg",)
GATHER_IN_HGRN = ("ffn2_wu",)
GATHER_IN_FFN2_UP = ("ffn2_wd",)
GROUP_FFN1 = ("ffn1_wg", "ffn1_wu", "ffn1_wd")
GROUP_MIX = ("w_in", "w_mem_kv", "w_hgrn_out", "w_fox_out", "w_mem_out", "w_gate", "w_o")
GROUP_FFN2 = ("ffn2_wg", "ffn2_wu", "ffn2_wd")


def _gather_rider(blocks, names):
    plan = _Gather([blocks[n].shape for n in names], [FFN_PAD.get(n, 0) for n in names])
    return _Rider(plan, [blocks[n] for n in names] + [plan.zeros()], GATHER_STEPS)


def _local_step(x, mem, tgt, small, wts=None, blocks=None):
    T = x.shape[0]
    tr = _tile(T, (256, 128))
    fb_pad = jnp.pad(small["fox_fb"], ((0, 0), (0, HD - NH)))
    dist = blocks is not None
    if dist:
        wts = dict(zip(GATHER_FIRST, _all_gather([blocks[n] for n in GATHER_FIRST], [FFN_PAD[n] for n in GATHER_FIRST])))

    def riding(names):
        return _gather_rider(blocks, names) if dist else None

    (n1,) = _rowwise(_norm_fn, [(x, 0)], [(small["ffn1_pre"], None)], [BF16], "ffn1_pre", tr, D, 1)
    h1, ffn1_saved, carried = _ffn_forward(n1, wts["ffn1_wg"], wts["ffn1_wu"], wts.get("ffn1_wd"), "ffn1",
                                           riding(GATHER_IN_FFN1_UP), riding(GATHER_IN_FFN1_DOWN))
    wts.update(zip(GATHER_IN_FFN1_UP + GATHER_IN_FFN1_DOWN, carried))
    ffn1_out = functools.partial(_post_pre_fn, 0.5)
    x1, un = _rowwise(ffn1_out, [(x, 0), (h1, 0)], [(small["ffn1_post"], None), (small["mix_pre"], None)], [F32, BF16],
                      "ffn1_post_mix_pre", tr, D, 1)
    if dist:
        proj, *carried = _mm(un, wts["w_in"], "nn", F32, "proj", rider=riding(GATHER_IN_PROJ))
        wts.update(zip(GATHER_IN_PROJ, carried))
        z, *carried = _mm(un, wts["w_gate"], "nt", F32, "gate_logits", rider=riding(GATHER_IN_GATE))
        wts.update(zip(GATHER_IN_GATE, carried))
    else:
        proj = _mm(un, wts["w_in"], "nn", F32, "proj")
        z = _mm(un, wts["w_gate"], "nt", F32, "gate_logits")
    (memn,) = _rowwise(_norm_fn, [(mem, 0)], [(small["mem_norm"], None)], [BF16], "mem_norm", mem.shape[0], D, 1)
    mem_kv = _mm(memn, wts["w_mem_kv"], "nn", F32, "mem_kv")

    o_raw, states, *carried = _hgrn_fwd(proj, small["hgrn_lb"], riding(GATHER_IN_HGRN))
    wts.update(zip(GATHER_IN_HGRN, carried))
    tr_head = _tile(T, (1024, 512, 256, 128))
    (o_h,) = _rowwise(_hpost_fn, [(o_raw, 0), (proj, CB_HOG)], [(small["hgrn_gnorm"], 0)], [BF16], "hgrn_post",
                      tr_head, HD, NH)
    ct, cq = _fox_cum(proj, fb_pad)
    win = _fox_windows(proj, cq)
    o_f, lse = _fox_fwd(win, proj, ct, cq)
    o_m = _mem_fwd(proj, mem_kv)

    yh = _mm(o_h, wts["w_hgrn_out"], "nt", F32, "hgrn_out")
    yf = _mm(o_f, wts["w_fox_out"], "nt", F32, "fox_out")
    ym = _mm(o_m, wts["w_mem_out"], "nt", F32, "mem_out")
    zc = D // 512
    merge_rows = [(z, 0), (z, zc), (z, 2 * zc), (yh, 0), (yf, 0), (ym, 0)]
    tr_merge = _tile(T, (512, 256, 128))
    (merged,) = _rowwise(_merge_fn, merge_rows, [], [BF16], "merge", tr_merge, 512, zc)
    m = _mm(merged, wts["w_o"], "nn", F32, "mix_out")
    mix_out = functools.partial(_post_pre_fn, 1.0)
    x2, n2 = _rowwise(mix_out, [(x1, 0), (m, 0)], [(small["mix_post"], None), (small["ffn2_pre"], None)], [F32, BF16],
                      "mix_post_ffn2_pre", tr, D, 1)
    h2, ffn2_saved, carried = _ffn_forward(n2, wts["ffn2_wg"], wts["ffn2_wu"], wts.get("ffn2_wd"), "ffn2",
                                           riding(GATHER_IN_FFN2_UP))
    wts.update(zip(GATHER_IN_FFN2_UP, carried))
    dy, loss_part = _loss(x2, h2, small["ffn2_post"], tgt, "loss")

    gw, gs, reduced = {}, {}, {}

    def pair_sums(names, tag):
        if not dist:
            return None, None
        theirs = brought.get(tag)
        if theirs is None:
            theirs = _exchange_in_chip([gw[n] for n in names], [blocks[n].shape[0] for n in names], f"reduce_in_chip_{tag}")
        pairs = [_pair_sum(gw[n], t_, f"pair_{n}") for n, t_ in zip(names, theirs)]
        return pairs, _Rider(_ChipExchange([p.shape for p in pairs]), pairs, EXCHANGE_STEPS)

    def chip_sums(names, pairs, recv):
        for n, p_, r_ in zip(names, pairs or (), recv):
            reduced[n] = _sum_chips(p_, r_, f"sum_{n}")

    brought = {}

    def in_chip_rider(names):
        grads_ = [gw[n] for n in names]
        return _Rider(_InChip([g_.shape for g_ in grads_], [blocks[n].shape[0] for n in names]), grads_, EXCHANGE_STEPS)

    def ffn2_exchange(dwg, dwu, dwd):
        gw.update(ffn2_wg=dwg, ffn2_wu=dwu, ffn2_wd=dwd)
        return in_chip_rider(GROUP_FFN2), None, lambda got_a, got_b: brought.update(ffn2=got_a)

    dh2, gs["ffn2_post"] = _rowwise_bwd(functools.partial(_resid_h_fn, 0.5), [(h2, 0)], [(small["ffn2_post"], None)],
                                        [(dy, 0)], [0], [BF16], "ffn2_post_bwd", tr, D, 1)
    dn2, (gw["ffn2_wg"], gw["ffn2_wu"], gw["ffn2_wd"]), _ = _ffn_backward(
        dh2, ffn2_saved, wts["ffn2_wg"], wts["ffn2_wu"], wts["ffn2_wd"], "ffn2", exchange=ffn2_exchange if dist else None)
    pairs_ffn2, ride_ffn2_grads = pair_sums(GROUP_FFN2, "ffn2")

    dx1, dm, gs["mix_post"], gs["ffn2_pre"] = _rowwise_bwd(
        mix_out, [(x1, 0), (m, 0)], [(small["mix_post"], None), (small["ffn2_pre"], None)], [(dy, 0), (dn2, 0)], [0, 1],
        [F32, BF16], "mix_post_ffn2_pre_bwd", tr, D, 1)
    dmerged = _mm(dm, wts["w_o"], "nt", F32, "d_merged")
    gw["w_o"] = _mm(merged, dm, "tn", BF16, "d_w_o")
    dz0, dz1, dz2, dyh, dyf, dym = _rowwise_bwd(_merge_fn, merge_rows, [], [(dmerged, 0)], [0, 1, 2, 3, 4, 5], [BF16] * 6,
                                                "merge_bwd", tr_merge, 512, zc)
    dz = jnp.concatenate([dz0, dz1, dz2], axis=1)
    gw["w_gate"] = _mm(dz, un, "tn", BF16, "d_w_gate")
    dun = _mm(dz, wts["w_gate"], "nn", F32, "d_un_gate")

    do_h = _mm(dyh, wts["w_hgrn_out"], "nn", F32, "d_o_h")
    gw["w_hgrn_out"] = _mm(dyh, o_h, "tn", BF16, "d_w_hgrn_out")
    do_f = _mm(dyf, wts["w_fox_out"], "nn", F32, "d_o_f")
    gw["w_fox_out"] = _mm(dyf, o_f, "tn", BF16, "d_w_fox_out")
    do_m = _mm(dym, wts["w_mem_out"], "nn", F32, "d_o_m")
    gw["w_mem_out"] = _mm(dym, o_m, "tn", BF16, "d_w_mem_out")

    do_raw, dhog, gs["hgrn_gnorm"] = _rowwise_bwd(_hpost_fn, [(o_raw, 0), (proj, CB_HOG)], [(small["hgrn_gnorm"], 0)],
                                                  [(do_h, 0)], [0, 1], [F32, BF16], "hgrn_post_bwd", tr_head, HD, NH)
    dhq, dhf, dhi, gs["hgrn_lb"], *carried = _hgrn_bwd(proj, small["hgrn_lb"], states, do_raw, ride_ffn2_grads)
    chip_sums(GROUP_FFN2, pairs_ffn2, carried)
    dfq, delta = _fox_bwd_dq(win, proj, ct, cq, lse, do_f)
    dfk, dfv, dc = _fox_bwd_dkv(win, proj, ct, cq, lse, delta, do_f)
    dff, dfb = _fox_cum_bwd(dc, proj, fb_pad)
    gs["fox_fb"] = dfb
    dmq, dmk, dmv = _mem_bwd(proj, mem_kv, do_m)

    dproj = jnp.concatenate([dhq, dhf, dhi, dhog, dfq, dfk, dfv, dff, dmq, jnp.zeros((T, HD), BF16)], axis=1)
    gw["w_in"] = _mm(un, dproj, "tn", BF16, "d_w_in")
    dun = _mm(dproj, wts["w_in"], "nt", F32, "d_un_proj", add=dun)
    dx0, dh1, gs["ffn1_post"], gs["mix_pre"] = _rowwise_bwd(
        ffn1_out, [(x, 0), (h1, 0)], [(small["ffn1_post"], None), (small["mix_pre"], None)], [(dx1, 0), (dun, 0)], [0, 1],
        [F32, BF16], "ffn1_post_mix_pre_bwd", tr, D, 1)

    dmem_kv = jnp.concatenate([dmk, dmv], axis=1)
    gw["w_mem_kv"] = _mm(memn, dmem_kv, "tn", BF16, "d_w_mem_kv")
    dmemn = _mm(dmem_kv, wts["w_mem_kv"], "nt", F32, "d_memn")
    _, gs["mem_norm"] = _rowwise_bwd(_norm_fn, [(mem, 0)], [(small["mem_norm"], None)], [(dmemn, 0)], [0], [BF16],
                                     "mem_norm_bwd", mem.shape[0], D, 1)

    mix = {}

    def mix_after_dwd(got):
        brought.update(mix=got)
        pairs, _ = pair_sums(GROUP_MIX, "mix")
        mix["names"] = GROUP_MIX[1:] + GROUP_MIX[:1]
        mix["pairs"] = pairs[1:] + pairs[:1]
        return tuple(_Rider(_ChipExchange([p.shape for p in part]), part, EXCHANGE_STEPS) for part in (pairs[1:], pairs[:1]))

    def own_exchange(dwg, dwu, dwd):
        gw.update(ffn1_wg=dwg, ffn1_wu=dwu, ffn1_wd=dwd)
        pairs, _ = pair_sums(GROUP_FFN1, "ffn1")
        first, second = pairs[:2], pairs[2:]

        def take(got_a, got_b):
            chip_sums(GROUP_FFN1, pairs, list(got_a) + list(got_b))

        return (_Rider(_ChipExchange([p.shape for p in first]), first, EXCHANGE_STEPS),
                _Rider(_ChipExchange([p.shape for p in second]), second, EXCHANGE_STEPS), take)

    dn1, (gw["ffn1_wg"], gw["ffn1_wu"], gw["ffn1_wd"]), carried = _ffn_backward(
        dh1, ffn1_saved, wts["ffn1_wg"], wts["ffn1_wu"], wts["ffn1_wd"], "ffn1",
        exchange=own_exchange if dist else None, rider_dwd=in_chip_rider(GROUP_MIX) if dist else None,
        after_dwd=mix_after_dwd if dist else None)
    chip_sums(mix.get("names", ()), mix.get("pairs"), carried)
    dx, gs["ffn1_pre"] = _rowwise_bwd(_norm_res_fn, [(x, 0)], [(small["ffn1_pre"], None)], [(dx0, 0), (dn1, 0)], [0], [F32],
                                      "ffn1_pre_bwd", tr, D, 1)
    return loss_part, dx, (reduced if dist else gw), gs


BIG = ("ffn1_wg", "ffn1_wu", "ffn1_wd", "w_in", "w_mem_kv", "w_hgrn_out", "w_fox_out", "w_mem_out", "w_gate", "w_o",
       "ffn2_wg", "ffn2_wu", "ffn2_wd")
TRANSPOSED = ("ffn1_wg", "ffn1_wu", "ffn2_wg", "ffn2_wu", "w_hgrn_out", "w_fox_out", "w_mem_out", "w_gate")
FFN_PAD = {"ffn1_wg": FP - F, "ffn1_wu": FP - F, "ffn1_wd": FP - F, "ffn2_wg": FP - F, "ffn2_wu": FP - F,
           "ffn2_wd": FP - F}
SMALL = GAINS + ("hgrn_lb", "hgrn_gnorm", "fox_fb")
WEIGHTS = ("ffn1_pre", "ffn1_post", "ffn1_wg", "ffn1_wu", "ffn1_wd", "mix_pre", "mix_post", "mem_norm", "w_in", "hgrn_lb",
           "hgrn_gnorm", "fox_fb", "w_mem_kv", "w_hgrn_out", "w_fox_out", "w_mem_out", "w_gate", "w_o", "ffn2_pre",
           "ffn2_post", "ffn2_wg", "ffn2_wu", "ffn2_wd")


def _to_gather_layout(name, w):
    if name in TRANSPOSED:
        w = w.T
    if name == "w_in":
        r = w.shape[0]
        w = jnp.concatenate([w[:, :MQ_COL], jnp.zeros((r, FF_COL + HD - MQ_COL), w.dtype), w[:, MQ_COL:],
                             jnp.zeros((r, P - FF_COL - HD - WM), w.dtype)], axis=1)
    return w.astype(BF16)


def _from_gather_layout(name, g):
    if name == "w_in":
        g = jnp.concatenate([g[:, :MQ_COL], g[:, FF_COL + HD:FF_COL + HD + WM]], axis=1)
    if name in TRANSPOSED:
        g = g.T
    return g


def kernel(x, mem, ffn1_pre, ffn1_post, ffn1_wg, ffn1_wu, ffn1_wd, mix_pre, mix_post, mem_norm, w_in, hgrn_lb, hgrn_gnorm, fox_fb, w_mem_kv, w_hgrn_out, w_fox_out, w_mem_out, w_gate, w_o, ffn2_pre, ffn2_post, ffn2_wg, ffn2_wu, ffn2_wd, loss_target, m_ffn1_pre, m_ffn1_post, m_ffn1_wg, m_ffn1_wu, m_ffn1_wd, m_mix_pre, m_mix_post, m_mem_norm, m_w_in, m_hgrn_lb, m_hgrn_gnorm, m_fox_fb, m_w_mem_kv, m_w_hgrn_out, m_w_fox_out, m_w_mem_out, m_w_gate, m_w_o, m_ffn2_pre, m_ffn2_post, m_ffn2_wg, m_ffn2_wu, m_ffn2_wd, v_ffn1_pre, v_ffn1_post, v_ffn1_wg, v_ffn1_wu, v_ffn1_wd, v_mix_pre, v_mix_post, v_mem_norm, v_w_in, v_hgrn_lb, v_hgrn_gnorm, v_fox_fb, v_w_mem_kv, v_w_hgrn_out, v_w_fox_out, v_w_mem_out, v_w_gate, v_w_o, v_ffn2_pre, v_ffn2_post, v_ffn2_wg, v_ffn2_wu, v_ffn2_wd):
    a = dict(locals())
    small = {n: a[n] for n in SMALL}
    shard = {n: a[n][0] if a[n].ndim == 3 else a[n] for n in BIG}

    blocks = {n: _to_gather_layout(n, shard[n]) for n in BIG}
    loss_part, dx, reduced, gs = _local_step(x[0], mem[0], loss_target[0], small, blocks=blocks)
    loss = lax.psum(0.5 / D * jnp.sum(loss_part), ("x", "y", "c"))

    grads, deltas, new_m, new_v = {}, {}, {}, {}
    for n in BIG:
        g = _from_gather_layout(n, reduced[n])
        d, m2, v2 = _adamw(shard[n], g, a["m_" + n].reshape(g.shape), a["v_" + n].reshape(g.shape), f"adamw_{n}")
        full = a[n].shape
        grads[n], deltas[n], new_m[n], new_v[n] = g.reshape(full), d.reshape(full), m2.reshape(full), v2.reshape(full)

    part = jnp.concatenate([_rows8(gs[n]) for n in GAINS] + [_rows8(gs["hgrn_lb"]), _rows8(gs["hgrn_gnorm"]),
                                                             _rows8(gs["fox_fb"][:, :NH])], axis=0)
    gsum = _all_reduce_small(part)

    def packed(prefix):
        lb = a[prefix + "hgrn_lb"]
        return _pack_small([a[prefix + n] for n in GAINS], lb[0], lb[1], a[prefix + "hgrn_gnorm"], a[prefix + "fox_fb"])

    g_p, d_p, m_p, v_p = _small_update(gsum, packed(""), packed("m_"), packed("v_"))
    for dst, p in ((grads, g_p), (deltas, d_p), (new_m, m_p), (new_v, v_p)):
        dst.update(_unpack_small(p))

    return (loss, dx[None], *[grads[n] for n in WEIGHTS], *[deltas[n] for n in WEIGHTS],
            *[new_m[n] for n in WEIGHTS], *[new_v[n] for n in WEIGHTS])
```

```python
import functools

import jax
import jax.numpy as jnp
from jax import lax
from jax.experimental import pallas as pl
from jax.experimental.pallas import tpu as pltpu

F32 = jnp.float32
BF16 = jnp.bfloat16
HIGHEST = lax.Precision.HIGHEST

NDEV = 8
D = 2048
F = 5504
FP = 5632
HD = 128
NH = 6
NM = 4
WH = NH * HD
WM = NM * HD
P = 6144
FF_COL = 5376
MQ_COL = 5382
CHUNK = 64
EPS = 1e-6
SCALE = HD ** -0.5
NEG = -1e30
VMEM_LIMIT = 48 * 1024 * 1024

CB_HQ, CB_HF, CB_HI, CB_HOG, CB_FQ, CB_FK, CB_FV, CB_FF, CB_MQ = 0, 6, 12, 18, 24, 30, 36, 42, 43

ADAM_LR, ADAM_B1, ADAM_B2, ADAM_EPS, ADAM_WD, ADAM_STEP = 0.001, 0.9, 0.999, 1e-08, 0.01, 10

NT = (((1,), (1,)), ((), ()))
NN = (((1,), (0,)), ((), ()))
TN = (((0,), (0,)), ((), ()))
MESH = pl.DeviceIdType.MESH


def _params(sem=None, **kw):
    return pltpu.CompilerParams(dimension_semantics=sem, vmem_limit_bytes=VMEM_LIMIT, **kw)


def _tile(n, prefs):
    for p in prefs:
        if p <= n and n % p == 0:
            return p
    return n


def _dot(a, b, dims):
    return lax.dot_general(a.astype(BF16), b.astype(BF16), dims, preferred_element_type=F32)


def _mm(a, b, mode, out_dtype, name, add=None, rider=None):
    if mode == "nn":
        (M, K), (K2, N) = a.shape, b.shape
    elif mode == "nt":
        (M, K), (N, K2) = a.shape, b.shape
    else:
        (K, M), (K2, N) = a.shape, b.shape
    assert K == K2, (a.shape, b.shape, mode)
    if mode == "tn":
        tm = _tile(M, (512, 256, 128))
        tn = _tile(N, (1024, 768, 512, 256, 128))
        tk = _tile(K, (4096, 2048, 1024, 512, 256, 128))
    else:
        tm = _tile(M, (1024, 512, 256, 128)) if K <= 2048 else _tile(M, (512, 256, 128))
        tn = _tile(N, (512, 768, 256, 128))
        tk = K if K <= 6144 else _tile(K, (2048, 1024, 512, 256, 128))
    nk = K // tk
    dims = {"nn": NN, "nt": NT, "tn": TN}[mode]
    has_add = add is not None

    ni, nj = M // tm, N // tn
    n_in = 3 if has_add else 2

    def body(*refs):
        step = (pl.program_id(0) * nj + pl.program_id(1)) * nk + pl.program_id(2)
        refs = _carry(rider, refs, n_in, 1, 1, step, ni * nj * nk)
        a_ref, b_ref = refs[0], refs[1]
        c_ref = refs[2] if has_add else None
        o_ref = refs[3] if has_add else refs[2]
        acc_ref = refs[-1]
        k = pl.program_id(2)
        part = _dot(a_ref[...], b_ref[...], dims)

        def finish(r):
            if has_add:
                r = r + c_ref[...].astype(F32)
            o_ref[...] = r.astype(o_ref.dtype)

        if nk == 1:
            finish(part)
        else:
            @pl.when(k == 0)
            def _():
                acc_ref[...] = part

            @pl.when(k > 0)
            def _():
                acc_ref[...] += part

            @pl.when(k == nk - 1)
            def _():
                finish(acc_ref[...])

    if mode == "nn":
        a_spec = pl.BlockSpec((tm, tk), lambda i, j, k: (i, k))
        b_spec = pl.BlockSpec((tk, tn), lambda i, j, k: (k, j))
    elif mode == "nt":
        a_spec = pl.BlockSpec((tm, tk), lambda i, j, k: (i, k))
        b_spec = pl.BlockSpec((tn, tk), lambda i, j, k: (j, k))
    else:
        a_spec = pl.BlockSpec((tk, tm), lambda i, j, k: (k, i))
        b_spec = pl.BlockSpec((tk, tn), lambda i, j, k: (k, j))
    o_spec = pl.BlockSpec((tm, tn), lambda i, j, k: (i, j))
    args = (a, b) + ((add,) if has_add else ())
    in_specs, out_specs, out_shape, scratch, extra = _with_rider(
        rider, [a_spec, b_spec] + ([o_spec] if has_add else []), [o_spec], [jax.ShapeDtypeStruct((M, N), out_dtype)],
        [pltpu.VMEM((tm, tn) if nk > 1 else (8, 128), F32)])
    out = pl.pallas_call(
        body, name=name, grid=(ni, nj, nk), in_specs=in_specs, out_specs=out_specs, out_shape=out_shape,
        scratch_shapes=scratch,
        compiler_params=_params(("arbitrary",) * 3 if rider else ("parallel", "parallel", "arbitrary"),
                                has_side_effects=rider is not None),
    )(*args, *extra)
    return out if rider else out[0]


class _Rider:
    def __init__(self, plan, inputs, steps):
        self.plan, self.inputs, self.steps = plan, list(inputs), steps
        self.n_out = len(plan.out_shape())
        self.n_sem = len(plan.sems())

    def run(self, step, total, in_refs, out_refs, sem_refs):
        n = self.plan.n
        if isinstance(self.plan, _Gather):
            args = (in_refs[:n], in_refs[n], out_refs) + tuple(sem_refs)
        else:
            args = (in_refs, out_refs) + tuple(sem_refs)
        for frac, method in self.steps:
            @pl.when(step == int(frac * (total - 1)))
            def _(method=method):
                getattr(self.plan, method)(*args)


GATHER_STEPS = ((0.0, "start"), (0.6, "forward"), (1.0, "finish"))
EXCHANGE_STEPS = ((0.0, "start"), (1.0, "finish"))


def _carry(rider, refs, n_in, n_out, n_scratch, step, total):
    if rider is None:
        return refs
    ri, ro, rs = len(rider.inputs), rider.n_out, rider.n_sem
    own_in, rid_in = refs[:n_in], refs[n_in:n_in + ri]
    own_out, rid_out = refs[n_in + ri:n_in + ri + n_out], refs[n_in + ri + n_out:n_in + ri + n_out + ro]
    own_scr, rid_sem = refs[n_in + ri + n_out + ro:n_in + ri + n_out + ro + n_scratch], refs[len(refs) - rs:]
    rider.run(step, total, rid_in, rid_out, rid_sem)
    return tuple(own_in) + tuple(own_out) + tuple(own_scr)


def _with_rider(rider, in_specs, out_specs, out_shape, scratch):
    if rider is None:
        return in_specs, out_specs, out_shape, scratch, ()
    any_spec = pl.BlockSpec(memory_space=pl.ANY)
    return (list(in_specs) + [any_spec] * len(rider.inputs), list(out_specs) + [any_spec] * rider.n_out,
            list(out_shape) + rider.plan.out_shape(), list(scratch) + rider.plan.sems(), tuple(rider.inputs))


def _ffn_up(n, wg_t, wu_t, name, rider=None):
    T = n.shape[0]
    tm = _tile(T, (1024, 512, 256, 128))
    tn = 512
    ni, nj = T // tm, FP // tn

    def body(*refs):
        step = pl.program_id(0) * nj + pl.program_id(1)
        n_ref, wg_ref, wu_ref, g_ref, u_ref, a_ref = _carry(rider, refs, 3, 3, 0, step, ni * nj)
        x = n_ref[...]
        g = _dot(x, wg_ref[...], NT)
        u = _dot(x, wu_ref[...], NT)
        g_ref[...] = g
        u_ref[...] = u
        a_ref[...] = (g * jax.nn.sigmoid(g) * u).astype(BF16)

    w_spec = pl.BlockSpec((tn, D), lambda i, j: (j, 0))
    o_spec = pl.BlockSpec((tm, tn), lambda i, j: (i, j))
    in_specs, out_specs, out_shape, scratch, extra = _with_rider(
        rider, [pl.BlockSpec((tm, D), lambda i, j: (i, 0)), w_spec, w_spec], [o_spec, o_spec, o_spec],
        [jax.ShapeDtypeStruct((T, FP), F32), jax.ShapeDtypeStruct((T, FP), F32), jax.ShapeDtypeStruct((T, FP), BF16)], [])
    return pl.pallas_call(
        body, name=name, grid=(ni, nj), in_specs=in_specs, out_specs=out_specs, out_shape=out_shape,
        scratch_shapes=scratch,
        compiler_params=_params(("arbitrary", "arbitrary") if rider else ("parallel", "parallel"),
                                has_side_effects=rider is not None),
    )(n, wg_t, wu_t, *extra)


def _ffn_act_bwd(dh, wd, g, u, name, rider=None):
    T = dh.shape[0]
    tm = _tile(T, (1024, 512, 256, 128))
    tn = 512
    ni, nj = T // tm, FP // tn

    def body(*refs):
        step = pl.program_id(0) * nj + pl.program_id(1)
        dh_ref, wd_ref, g_ref, u_ref, dg_ref, du_ref = _carry(rider, refs, 4, 2, 0, step, ni * nj)
        da = _dot(dh_ref[...], wd_ref[...], NT)
        g = g_ref[...]
        sg = jax.nn.sigmoid(g)
        dg_ref[...] = (da * u_ref[...] * (sg * (1.0 + g * (1.0 - sg)))).astype(dg_ref.dtype)
        du_ref[...] = (da * (g * sg)).astype(du_ref.dtype)

    tile = pl.BlockSpec((tm, tn), lambda i, j: (i, j))
    in_specs, out_specs, out_shape, scratch, extra = _with_rider(
        rider, [pl.BlockSpec((tm, D), lambda i, j: (i, 0)), pl.BlockSpec((tn, D), lambda i, j: (j, 0)), tile, tile],
        [tile, tile], [jax.ShapeDtypeStruct((T, FP), BF16), jax.ShapeDtypeStruct((T, FP), BF16)], [])
    return pl.pallas_call(
        body, name=name, grid=(ni, nj), in_specs=in_specs, out_specs=out_specs, out_shape=out_shape,
        scratch_shapes=scratch,
        compiler_params=_params(("arbitrary", "arbitrary") if rider else ("parallel", "parallel"),
                                has_side_effects=rider is not None),
    )(dh, wd, g, u, *extra)


def _row_specs(rows, tr, cw):
    return [pl.BlockSpec((tr, cw), lambda j, i, o=off: (i, o + j)) for _, off in rows]


def _const_specs(consts, cw):
    specs = []
    for arr, off in consts:
        if off is None:
            specs.append(pl.BlockSpec(arr.shape, lambda j, i: (0, 0)))
        else:
            specs.append(pl.BlockSpec((arr.shape[0], cw), lambda j, i, o=off: (0, o + j)))
    return specs


def _rowwise(fn, rows, consts, out_dtypes, name, tr, cw, ncol):
    T = rows[0][0].shape[0]
    nr, nc = len(rows), len(consts)

    def body(*refs):
        r = [x[...].astype(F32) for x in refs[:nr]]
        c = [x[...] for x in refs[nr:nr + nc]]
        res = fn(*r, *c)
        for o_ref, v in zip(refs[nr + nc:], res):
            o_ref[...] = v.astype(o_ref.dtype)

    o_spec = pl.BlockSpec((tr, cw), lambda j, i: (i, j))
    return pl.pallas_call(
        body, name=name, grid=(ncol, T // tr),
        in_specs=_row_specs(rows, tr, cw) + _const_specs(consts, cw),
        out_specs=[o_spec] * len(out_dtypes),
        out_shape=[jax.ShapeDtypeStruct((T, ncol * cw), dt) for dt in out_dtypes],
        compiler_params=_params(("parallel", "parallel")),
    )(*[a for a, _ in rows], *[a for a, _ in consts])


def _rowwise_bwd(fn, rows, consts, cots, diff, ddtypes, name, tr, cw, ncol):
    T = rows[0][0].shape[0]
    nr, nc, nt, nd = len(rows), len(consts), len(cots), len(diff)

    def body(*refs):
        r = [x[...].astype(F32) for x in refs[:nr]]
        c = [x[...] for x in refs[nr:nr + nc]]
        ct = [x[...].astype(F32) for x in refs[nr + nc:nr + nc + nt]]
        drow_refs = refs[nr + nc + nt:nr + nc + nt + nd]
        dconst_refs = refs[nr + nc + nt + nd:]
        i = pl.program_id(1)

        def f(*args):
            full = list(r)
            for idx, a in zip(diff, args[:nd]):
                full[idx] = a
            return tuple(fn(*full, *args[nd:]))

        _, vjp = jax.vjp(f, *[r[d] for d in diff], *c)
        g = vjp(tuple(ct))
        for o_ref, v in zip(drow_refs, g[:nd]):
            o_ref[...] = v.astype(o_ref.dtype)

        @pl.when(i == 0)
        def _():
            for o_ref in dconst_refs:
                o_ref[...] = jnp.zeros_like(o_ref)

        for o_ref, v in zip(dconst_refs, g[nd:]):
            o_ref[...] += v

    o_spec = pl.BlockSpec((tr, cw), lambda j, i: (i, j))
    out_shape = [jax.ShapeDtypeStruct((T, ncol * cw), dt) for dt in ddtypes]
    out_shape += [jax.ShapeDtypeStruct(a.shape, F32) for a, _ in consts]
    return pl.pallas_call(
        body, name=name, grid=(ncol, T // tr),
        in_specs=_row_specs(rows, tr, cw) + _const_specs(consts, cw) + _row_specs(cots, tr, cw),
        out_specs=[o_spec] * nd + _const_specs(consts, cw),
        out_shape=out_shape,
        compiler_params=_params(("parallel", "arbitrary")),
    )(*[a for a, _ in rows], *[a for a, _ in consts], *[a for a, _ in cots])


def _rms(x, g):
    return x * lax.rsqrt(jnp.mean(x * x, axis=-1, keepdims=True) + EPS) * g


def _silu(x):
    return x * jax.nn.sigmoid(x)


def _norm_fn(x, g):
    return (_rms(x, g),)


def _norm_res_fn(x, g):
    return (x, _rms(x, g))


def _post_pre_fn(scale, x, h, g_post, g_pre):
    xn = x + scale * _rms(h, g_post)
    return (xn, _rms(xn, g_pre))


def _resid_h_fn(scale, h, g):
    return (scale * _rms(h, g),)


def _hpost_fn(o, hog, gn):
    return (_rms(o, gn) * _silu(hog),)


def _merge_fn(z0, z1, z2, yh, yf, ym):
    return (jax.nn.sigmoid(z0) * yh + jax.nn.sigmoid(z1) * yf + jax.nn.sigmoid(z2) * ym,)


def _loss(x2, h, g_post, tgt, name):
    T = x2.shape[0]
    tr = _tile(T, (256, 128))

    def body(x_ref, h_ref, g_ref, t_ref, dy_ref, s_ref):
        i = pl.program_id(0)
        e = x_ref[...] + 0.5 * _rms(h_ref[...], g_ref[...]) - t_ref[...]
        dy_ref[...] = e * (1.0 / D)
        col = jnp.sum(e * e, axis=0, keepdims=True)
        tot = col[:, 0:HD]
        for k in range(1, D // HD):
            tot = tot + col[:, k * HD:(k + 1) * HD]

        @pl.when(i == 0)
        def _():
            s_ref[...] = jnp.zeros_like(s_ref)

        s_ref[...] += tot

    spec = pl.BlockSpec((tr, D), lambda i: (i, 0))
    return pl.pallas_call(
        body, name=name, grid=(T // tr,), in_specs=[spec, spec, pl.BlockSpec((1, D), lambda i: (0, 0)), spec],
        out_specs=[spec, pl.BlockSpec((1, HD), lambda i: (0, 0))],
        out_shape=[jax.ShapeDtypeStruct((T, D), F32), jax.ShapeDtypeStruct((1, HD), F32)],
        compiler_params=_params(("arbitrary",)),
    )(x2, h, g_post, tgt)


def _lower_bound(lb_ref):
    a0 = lb_ref[0:1, :]
    a1 = lb_ref[1:2, :]
    mx = jnp.maximum(a0, a1)
    e0 = jnp.exp(a0 - mx)
    return e0 / (e0 + jnp.exp(a1 - mx))


def _hgrn_prep(hq, hf, lb):
    g = lb + (1.0 - lb) * jax.nn.sigmoid(hf)
    return _silu(hq), 1.0 - g, jnp.log(g)


def _tri(n, upper):
    r = lax.broadcasted_iota(jnp.int32, (n, n), 0)
    c = lax.broadcasted_iota(jnp.int32, (n, n), 1)
    return (c >= r) if upper else (c <= r)


def _hgrn_factors(q, k, gl):
    low = _tri(CHUNK, False)
    b = lax.dot_general(low.astype(F32), gl, NN, precision=HIGHEST, preferred_element_type=F32)
    bl = b[CHUNK - 1:CHUNK, :]
    ref = b[CHUNK // 2 - 1:CHUNK // 2, :]
    eb = jnp.exp(b)
    ea = jnp.exp(b - ref)
    ebn = jnp.exp(ref - b)
    ek = jnp.exp(bl - b)
    ebl = jnp.exp(bl)
    return low, eb, ea, ebn, ek, ebl


def _hgrn_fwd(proj, hgrn_lb, rider=None):
    T = proj.shape[0]
    cb = _tile(T, (512, 256, 128, 64))
    nchunk = cb // CHUNK

    def body(*refs):
        hq_ref, hf_ref, hi_ref, lb_ref, o_ref, st_ref, state = _carry(rider, refs, 4, 2, 1, pl.program_id(0), T // cb)

        @pl.when(pl.program_id(0) == 0)
        def _():
            state[...] = jnp.zeros_like(state)

        lb = _lower_bound(lb_ref)

        def chunk(c, carry):
            r0 = pl.multiple_of(c * CHUNK, CHUNK)
            for h in range(NH):
                cols = slice(h * HD, (h + 1) * HD)
                q, k, gl = _hgrn_prep(hq_ref[pl.ds(r0, CHUNK), cols], hf_ref[pl.ds(r0, CHUNK), cols], lb[:, cols])
                v = hi_ref[pl.ds(r0, CHUNK), cols]
                low, eb, ea, ebn, ek, ebl = _hgrn_factors(q, k, gl)
                s_t = state[h]
                st_ref[c, h] = s_t
                pm = jnp.where(low, _dot(q * ea, k * ebn, NT), 0.0)
                o_ref[pl.ds(r0, CHUNK), cols] = _dot(q * eb, s_t, NT) + _dot(pm, v, NN)
                state[h] = s_t * ebl + _dot(v, k * ek, TN)
            return carry

        lax.fori_loop(0, nchunk, chunk, 0)

    def col(off):
        return pl.BlockSpec((cb, WH), lambda i, o=off: (i, o))

    in_specs, out_specs, out_shape, scratch, extra = _with_rider(
        rider, [col(0), col(1), col(2), pl.BlockSpec((2, WH), lambda i: (0, 0))],
        [pl.BlockSpec((cb, WH), lambda i: (i, 0)), pl.BlockSpec((nchunk, NH, HD, HD), lambda i: (i, 0, 0, 0))],
        [jax.ShapeDtypeStruct((T, WH), F32), jax.ShapeDtypeStruct((T // CHUNK, NH, HD, HD), F32)],
        [pltpu.VMEM((NH, HD, HD), F32)])
    return pl.pallas_call(
        body, name="hgrn_fwd", grid=(T // cb,), in_specs=in_specs, out_specs=out_specs, out_shape=out_shape,
        scratch_shapes=scratch, compiler_params=_params(("arbitrary",), has_side_effects=rider is not None),
    )(proj, proj, proj, hgrn_lb, *extra)


def _hgrn_bwd(proj, hgrn_lb, states, do, rider=None):
    T = proj.shape[0]
    cb = _tile(T, (512, 256, 128, 64))
    nchunk = cb // CHUNK
    nb = T // cb

    def body(*refs):
        (hq_ref, hf_ref, hi_ref, lb_ref, st_ref, do_ref, dhq_ref, dhf_ref, dhi_ref, dlb_ref,
         dstate) = _carry(rider, refs, 6, 4, 1, pl.program_id(0), nb)

        @pl.when(pl.program_id(0) == 0)
        def _():
            dstate[...] = jnp.zeros_like(dstate)
            dlb_ref[...] = jnp.zeros_like(dlb_ref)

        lb = _lower_bound(lb_ref)
        up = _tri(CHUNK, True)
        last = lax.broadcasted_iota(jnp.int32, (CHUNK, HD), 0) == CHUNK - 1

        def chunk(cc, carry):
            c = nchunk - 1 - cc
            r0 = pl.multiple_of(c * CHUNK, CHUNK)
            for h in range(NH):
                cols = slice(h * HD, (h + 1) * HD)
                hq = hq_ref[pl.ds(r0, CHUNK), cols]
                hf = hf_ref[pl.ds(r0, CHUNK), cols]
                (q, k, gl), prep_vjp = jax.vjp(_hgrn_prep, hq, hf, lb[:, cols])
                v = hi_ref[pl.ds(r0, CHUNK), cols]
                d_o = do_ref[pl.ds(r0, CHUNK), cols]
                low, eb, ea, ebn, ek, ebl = _hgrn_factors(q, k, gl)
                s_t = st_ref[c, h]
                ds_new = dstate[h]
                qe, am, bm, kb = q * eb, q * ea, k * ebn, k * ek
                pm_t = jnp.where(up, _dot(bm, am, NT), 0.0)
                dp = jnp.where(low, _dot(d_o, v, NT), 0.0)
                dp_t = jnp.where(up, _dot(v, d_o, NT), 0.0)
                dqe = _dot(d_o, s_t, NN)
                da = _dot(dp, bm, NN)
                db_m = _dot(dp_t, am, NN)
                dkb = _dot(v, ds_new, NN)
                dv = _dot(pm_t, d_o, NN) + _dot(kb, ds_new, NT)
                dq = dqe * eb + da * ea
                dk = db_m * ebn + dkb * ek
                dbl = jnp.sum(dkb * kb, axis=0, keepdims=True) + jnp.sum(ds_new * s_t, axis=0, keepdims=True) * ebl
                db = (dqe * qe + da * am.astype(BF16).astype(F32) - db_m * bm.astype(BF16).astype(F32) - dkb * kb
                      + jnp.where(last, dbl, 0.0))
                dgl = lax.dot_general(up.astype(F32), db, NN, precision=HIGHEST, preferred_element_type=F32)
                dhq, dhf, dlb = prep_vjp((dq, dk, dgl))
                dhq_ref[pl.ds(r0, CHUNK), cols] = dhq.astype(dhq_ref.dtype)
                dhf_ref[pl.ds(r0, CHUNK), cols] = dhf.astype(dhf_ref.dtype)
                dhi_ref[pl.ds(r0, CHUNK), cols] = dv.astype(dhi_ref.dtype)
                dlb_ref[:, cols] += dlb
                dstate[h] = _dot(d_o, qe, TN) + ds_new * ebl
            return carry

        lax.fori_loop(0, nchunk, chunk, 0)

    def col(off):
        return pl.BlockSpec((cb, WH), lambda i, o=off: (nb - 1 - i, o))

    row = pl.BlockSpec((cb, WH), lambda i: (nb - 1 - i, 0))
    in_specs, out_specs, out_shape, scratch, extra = _with_rider(
        rider, [col(0), col(1), col(2), pl.BlockSpec((2, WH), lambda i: (0, 0)),
                pl.BlockSpec((nchunk, NH, HD, HD), lambda i: (nb - 1 - i, 0, 0, 0)), row],
        [row, row, row, pl.BlockSpec((1, WH), lambda i: (0, 0))],
        [jax.ShapeDtypeStruct((T, WH), BF16)] * 3 + [jax.ShapeDtypeStruct((1, WH), F32)], [pltpu.VMEM((NH, HD, HD), F32)])
    return pl.pallas_call(
        body, name="hgrn_bwd", grid=(nb,), in_specs=in_specs, out_specs=out_specs, out_shape=out_shape,
        scratch_shapes=scratch, compiler_params=_params(("arbitrary",), has_side_effects=rider is not None),
    )(proj, proj, proj, hgrn_lb, states, do, *extra)


def _log_sigmoid(z):
    return jnp.minimum(z, 0.0) - jnp.log(1.0 + jnp.exp(-jnp.abs(z)))


def _fox_cum(proj, fb_pad):
    T = proj.shape[0]
    tb = _tile(T, (256, 128))

    def body(ff_ref, fb_ref, ct_ref, cq_ref, carry):
        @pl.when(pl.program_id(0) == 0)
        def _():
            carry[...] = jnp.zeros_like(carry)

        lf = _log_sigmoid(ff_ref[...] + fb_ref[...])
        cs = lax.dot_general(_tri(tb, False).astype(F32), lf, NN, precision=HIGHEST,
                             preferred_element_type=F32) + carry[0:1, :]
        carry[0:1, :] = cs[tb - 1:tb, :]
        ct_ref[...] = cs.T[0:8, :]
        for h in range(NH):
            cq_ref[h] = jnp.broadcast_to(cs[:, h:h + 1], (tb, HD))

    return pl.pallas_call(
        body, name="fox_cum", grid=(T // tb,),
        in_specs=[pl.BlockSpec((tb, HD), lambda i: (i, CB_FF)), pl.BlockSpec((1, HD), lambda i: (0, 0))],
        out_specs=[pl.BlockSpec((8, tb), lambda i: (0, i)), pl.BlockSpec((NH, tb, HD), lambda i: (0, i, 0))],
        out_shape=[jax.ShapeDtypeStruct((8, T), F32), jax.ShapeDtypeStruct((NH, T, HD), F32)],
        scratch_shapes=[pltpu.VMEM((8, HD), F32)],
        compiler_params=_params(("arbitrary",)),
    )(proj, fb_pad)


def _fox_cum_bwd(dc, proj, fb_pad):
    T = proj.shape[0]
    tb = _tile(T, (256, 128))
    nb = T // tb

    def body(dc_ref, ff_ref, fb_ref, dff_ref, dfb_ref, carry):
        @pl.when(pl.program_id(0) == 0)
        def _():
            carry[...] = jnp.zeros_like(carry)
            dfb_ref[...] = jnp.zeros_like(dfb_ref)

        rid = lax.broadcasted_iota(jnp.int32, (8, tb), 0)
        m8 = jnp.zeros((8, tb), F32)
        for h in range(NH):
            m8 = m8 + jnp.where(rid == h, dc_ref[h], 0.0)
        dcb = jnp.concatenate([m8, jnp.zeros((HD - 8, tb), F32)], axis=0).T
        rev = lax.dot_general(_tri(tb, True).astype(F32), dcb, NN, precision=HIGHEST,
                              preferred_element_type=F32) + carry[0:1, :]
        carry[0:1, :] = rev[0:1, :]
        dff = rev * jax.nn.sigmoid(-(ff_ref[...] + fb_ref[...]))
        dff_ref[...] = dff.astype(dff_ref.dtype)
        dfb_ref[...] += jnp.sum(dff, axis=0, keepdims=True)

    return pl.pallas_call(
        body, name="fox_cum_bwd", grid=(nb,),
        in_specs=[pl.BlockSpec((NH, 8, tb), lambda i: (0, 0, nb - 1 - i)),
                  pl.BlockSpec((tb, HD), lambda i: (nb - 1 - i, CB_FF)), pl.BlockSpec((1, HD), lambda i: (0, 0))],
        out_specs=[pl.BlockSpec((tb, HD), lambda i: (nb - 1 - i, 0)), pl.BlockSpec((1, HD), lambda i: (0, 0))],
        out_shape=[jax.ShapeDtypeStruct((T, HD), BF16), jax.ShapeDtypeStruct((1, HD), F32)],
        scratch_shapes=[pltpu.VMEM((8, HD), F32)],
        compiler_params=_params(("arbitrary",)),
    )(dc, proj, fb_pad)


STRIP = 128


def _fox_scores(q, k, cq, ck, i, j, bq, bk, r0=0):
    rows = q.shape[0]
    s = _dot(q, k, NT) * SCALE + (cq - ck)
    diff = lax.broadcasted_iota(jnp.int32, (rows, bk), 1) - lax.broadcasted_iota(jnp.int32, (rows, bk), 0)
    return jnp.where(diff <= i * bq + r0 - j * bk, s, NEG)


def _heads(h):
    return slice(h * HD, (h + 1) * HD)


UNDERFLOW = -120.0


def _fox_windows(proj, cq):
    T = proj.shape[0]
    bq = _tile(T, (512, 256, 128))
    nq = T // bq
    assert nq <= HD

    def body(q_ref, k_ref, cq_ref, jlo_ref, ihi_ref, norm_s, cs_s, ce_s):
        i = pl.program_id(0)

        @pl.when(i == 0)
        def _():
            norm_s[...] = jnp.zeros_like(norm_s)
            cs_s[...] = jnp.zeros_like(cs_s)
            ce_s[...] = jnp.zeros_like(ce_s)

        lane = lax.broadcasted_iota(jnp.int32, (1, HD), 1)
        for h in range(NH):
            for row, ref in ((h, q_ref), (8 + h, k_ref)):
                x = ref[:, _heads(h)]
                biggest = jnp.max(jnp.sum(x * x, axis=1, keepdims=True), axis=0, keepdims=True)
                norm_s[row:row + 1, :] = jnp.maximum(norm_s[row:row + 1, :], jnp.broadcast_to(biggest, (1, HD)))
            cs_s[h, pl.ds(i, 1), :] = cq_ref[h, 0:1, :]
            ce_s[h:h + 1, :] = jnp.where(lane == i, cq_ref[h, bq - 1:bq, :], ce_s[h:h + 1, :])

        @pl.when(i == nq - 1)
        def _():
            rows = lax.broadcasted_iota(jnp.int32, (HD, HD), 0)
            cols = lax.broadcasted_iota(jnp.int32, (HD, HD), 1)
            need = cols == rows
            for h in range(NH):
                slack = 2.05 * SCALE * jnp.sqrt(norm_s[h:h + 1, :] * norm_s[8 + h:9 + h, :])
                bound = cs_s[h] - ce_s[h:h + 1, :] + slack
                need = need | ((bound >= UNDERFLOW) & (cols < rows))
            need = need & (rows < nq) & (cols < nq)
            jlo = jnp.min(jnp.where(need, cols, HD).astype(F32), axis=1, keepdims=True)
            ihi = jnp.max(jnp.where(need, rows, -1).astype(F32), axis=0, keepdims=True)
            jlo_ref[...] = jnp.broadcast_to(jlo, (HD, HD)).astype(jnp.int32)
            ihi_ref[...] = jnp.broadcast_to(ihi, (8, HD)).astype(jnp.int32)

    jlo, ihi = pl.pallas_call(
        body, name="fox_windows", grid=(nq,),
        in_specs=[pl.BlockSpec((bq, WH), lambda i: (i, CB_FQ // NH)), pl.BlockSpec((bq, WH), lambda i: (i, CB_FK // NH)),
                  pl.BlockSpec((NH, bq, HD), lambda i: (0, i, 0))],
        out_specs=[pl.BlockSpec((HD, HD), lambda i: (0, 0)), pl.BlockSpec((8, HD), lambda i: (0, 0))],
        out_shape=[jax.ShapeDtypeStruct((HD, HD), jnp.int32), jax.ShapeDtypeStruct((8, HD), jnp.int32)],
        scratch_shapes=[pltpu.VMEM((16, HD), F32), pltpu.VMEM((NH, HD, HD), F32), pltpu.VMEM((8, HD), F32)],
        compiler_params=_params(("arbitrary",)),
    )(proj, proj, cq)
    return jnp.concatenate([jlo[:nq, 0], ihi[0, :nq]])


def _fox_fwd(win, proj, ct, cq):
    T = proj.shape[0]
    bq = bk = _tile(T, (512, 256, 128))
    nq = nk = T // bq

    def body(win_ref, q_ref, k_ref, v_ref, ct_ref, cq_ref, o_ref, lse_ref, m_s, l_s, acc_s):
        i, jj = pl.program_id(0), pl.program_id(1)
        j = win_ref[i] + jj

        @pl.when(jj == 0)
        def _():
            m_s[...] = jnp.full_like(m_s, NEG)
            l_s[...] = jnp.zeros_like(l_s)
            acc_s[...] = jnp.zeros_like(acc_s)

        @pl.when(j <= i)
        def _():
            for h in range(NH):
                hs = _heads(h)
                k, v, ck = k_ref[:, hs], v_ref[:, hs], ct_ref[h:h + 1, :]
                for r0 in range(0, bq, STRIP):
                    rs = slice(r0, r0 + STRIP)
                    s = _fox_scores(q_ref[rs, hs], k, cq_ref[h, rs, 0:1], ck, i, j, bq, bk, r0)
                    m_prev = m_s[h, rs]
                    m_new = jnp.maximum(m_prev, jnp.max(s, axis=1, keepdims=True))
                    alpha = jnp.exp(m_prev - m_new)
                    p = jnp.exp(s - m_new)
                    l_s[h, rs] = alpha * l_s[h, rs] + jnp.sum(p, axis=1, keepdims=True)
                    acc_s[rs, hs] = alpha * acc_s[rs, hs] + _dot(p, v, NN)
                    m_s[h, rs] = m_new

        @pl.when(jj == nk - 1)
        def _():
            for h in range(NH):
                o_ref[:, _heads(h)] = acc_s[:, _heads(h)] / l_s[h]
                lse_ref[h] = jnp.broadcast_to(m_s[h] + jnp.log(l_s[h]), (bq, HD))

    def key_block(i, jj, win):
        return jnp.minimum(win[i] + jj, i)

    def kv(off):
        return pl.BlockSpec((bk, WH), lambda i, jj, win, o=off // NH: (key_block(i, jj, win), o))

    stat = pl.BlockSpec((NH, bq, HD), lambda i, jj, win: (0, i, 0))
    return pl.pallas_call(
        body, name="fox_fwd",
        grid_spec=pltpu.PrefetchScalarGridSpec(
            num_scalar_prefetch=1, grid=(nq, nk),
            in_specs=[pl.BlockSpec((bq, WH), lambda i, jj, win: (i, CB_FQ // NH)), kv(CB_FK), kv(CB_FV),
                      pl.BlockSpec((8, bk), lambda i, jj, win: (0, key_block(i, jj, win))), stat],
            out_specs=[pl.BlockSpec((bq, WH), lambda i, jj, win: (i, 0)), stat],
            scratch_shapes=[pltpu.VMEM((NH, bq, 1), F32), pltpu.VMEM((NH, bq, 1), F32), pltpu.VMEM((bq, WH), F32)]),
        out_shape=[jax.ShapeDtypeStruct((T, WH), F32), jax.ShapeDtypeStruct((NH, T, HD), F32)],
        compiler_params=_params(("parallel", "arbitrary")),
    )(win, proj, proj, proj, ct, cq)


def _fox_bwd_dq(win, proj, ct, cq, lse, do):
    T = proj.shape[0]
    bq = bk = _tile(T, (512, 256, 128))
    nq = nk = T // bq

    def body(win_ref, q_ref, k_ref, v_ref, ct_ref, cq_ref, lse_ref, do_ref, dq_ref, delta_ref, acc_s, delta_s, psum_s):
        i, jj = pl.program_id(0), pl.program_id(1)
        j = win_ref[i] + jj % nk

        @pl.when(jj == 0)
        def _():
            acc_s[...] = jnp.zeros_like(acc_s)
            delta_s[...] = jnp.zeros_like(delta_s)
            psum_s[...] = jnp.zeros_like(psum_s)

        def probs(h):
            hs = _heads(h)
            k = k_ref[:, hs]
            s = _fox_scores(q_ref[:, hs], k, cq_ref[h, :, 0:1], ct_ref[h:h + 1, :], i, j, bq, bk)
            return k, jnp.exp(s - lse_ref[h, :, 0:1]), _dot(do_ref[:, hs], v_ref[:, hs], NT)

        @pl.when((j <= i) & (jj < nk))
        def _():
            for h in range(NH):
                _, p, dp = probs(h)
                delta_s[h] += jnp.sum(p * dp, axis=1, keepdims=True)
                psum_s[h] += jnp.sum(p, axis=1, keepdims=True)

        @pl.when((j <= i) & (jj >= nk))
        def _():
            for h in range(NH):
                k, p, dp = probs(h)
                ds = p * (dp - delta_s[h] / psum_s[h])
                acc_s[:, _heads(h)] += _dot(ds, k, NN) * SCALE

        @pl.when(jj == 2 * nk - 1)
        def _():
            dq_ref[...] = acc_s[...].astype(dq_ref.dtype)
            for h in range(NH):
                delta_ref[h] = jnp.broadcast_to(delta_s[h] / psum_s[h], (bq, HD))

    def key_block(i, jj, win):
        return jnp.minimum(win[i] + jj % nk, i)

    def kv(off):
        return pl.BlockSpec((bk, WH), lambda i, jj, win, o=off // NH: (key_block(i, jj, win), o))

    qrow = pl.BlockSpec((bq, WH), lambda i, jj, win: (i, 0))
    stat = pl.BlockSpec((NH, bq, HD), lambda i, jj, win: (0, i, 0))
    return pl.pallas_call(
        body, name="fox_bwd_dq",
        grid_spec=pltpu.PrefetchScalarGridSpec(
            num_scalar_prefetch=1, grid=(nq, 2 * nk),
            in_specs=[pl.BlockSpec((bq, WH), lambda i, jj, win: (i, CB_FQ // NH)), kv(CB_FK), kv(CB_FV),
                      pl.BlockSpec((8, bk), lambda i, jj, win: (0, key_block(i, jj, win))), stat, stat, qrow],
            out_specs=[qrow, stat],
            scratch_shapes=[pltpu.VMEM((bq, WH), F32), pltpu.VMEM((NH, bq, 1), F32), pltpu.VMEM((NH, bq, 1), F32)]),
        out_shape=[jax.ShapeDtypeStruct((T, WH), BF16), jax.ShapeDtypeStruct((NH, T, HD), F32)],
        compiler_params=_params(("parallel", "arbitrary")),
    )(win, proj, proj, proj, ct, cq, lse, do)


def _fox_bwd_dkv(win, proj, ct, cq, lse, delta, do):
    T = proj.shape[0]
    bq = bk = _tile(T, (512, 256, 128))
    nq = nk = T // bq

    def body(win_ref, q_ref, k_ref, v_ref, ct_ref, cq_ref, lse_ref, delta_ref, do_ref, dk_ref, dv_ref, dc_ref,
             dk_s, dv_s, dc_s):
        j, ii = pl.program_id(0), pl.program_id(1)
        i = j + ii

        @pl.when(ii == 0)
        def _():
            dk_s[...] = jnp.zeros_like(dk_s)
            dv_s[...] = jnp.zeros_like(dv_s)
            dc_s[...] = jnp.zeros_like(dc_s)

        @pl.when(i <= win_ref[nq + j])
        def _():
            for h in range(NH):
                hs = _heads(h)
                q = q_ref[:, hs]
                d_o = do_ref[:, hs]
                s = _fox_scores(q, k_ref[:, hs], cq_ref[h, :, 0:1], ct_ref[h:h + 1, :], i, j, bq, bk)
                p = jnp.exp(s - lse_ref[h, :, 0:1])
                dv_s[:, hs] += _dot(p, d_o, TN)
                dp = _dot(d_o, v_ref[:, hs], NT)
                ds = p * (dp - delta_ref[h, :, 0:1])
                dk_s[:, hs] += _dot(ds, q, TN) * SCALE
                dc_s[h:h + 1, :] -= jnp.sum(ds, axis=0, keepdims=True)

        @pl.when(ii == nq - 1)
        def _():
            dk_ref[...] = dk_s[...].astype(dk_ref.dtype)
            dv_ref[...] = dv_s[...].astype(dv_ref.dtype)
            for h in range(NH):
                dc_ref[h] = jnp.broadcast_to(dc_s[h:h + 1, :], (8, bk))

    def query_block(j, ii, win):
        return jnp.minimum(j + ii, win[nq + j])

    def kv(off):
        return pl.BlockSpec((bk, WH), lambda j, ii, win, o=off // NH: (j, o))

    qrow = pl.BlockSpec((bq, WH), lambda j, ii, win: (query_block(j, ii, win), 0))
    stat = pl.BlockSpec((NH, bq, HD), lambda j, ii, win: (0, query_block(j, ii, win), 0))
    krow = pl.BlockSpec((bk, WH), lambda j, ii, win: (j, 0))
    return pl.pallas_call(
        body, name="fox_bwd_dkv",
        grid_spec=pltpu.PrefetchScalarGridSpec(
            num_scalar_prefetch=1, grid=(nk, nq),
            in_specs=[pl.BlockSpec((bq, WH), lambda j, ii, win: (query_block(j, ii, win), CB_FQ // NH)), kv(CB_FK),
                      kv(CB_FV), pl.BlockSpec((8, bk), lambda j, ii, win: (0, j)), stat, stat, stat, qrow],
            out_specs=[krow, krow, pl.BlockSpec((NH, 8, bk), lambda j, ii, win: (0, 0, j))],
            scratch_shapes=[pltpu.VMEM((bk, WH), F32), pltpu.VMEM((bk, WH), F32), pltpu.VMEM((8, bk), F32)]),
        out_shape=[jax.ShapeDtypeStruct((T, WH), BF16), jax.ShapeDtypeStruct((T, WH), BF16),
                   jax.ShapeDtypeStruct((NH, 8, T), F32)],
        compiler_params=_params(("parallel", "arbitrary")),
    )(win, proj, proj, proj, ct, cq, lse, delta, do)


def _mem_probs(q, mk):
    s = _dot(q, mk, NT) * SCALE
    e = jnp.exp(s - jnp.max(s, axis=1, keepdims=True))
    return e / jnp.sum(e, axis=1, keepdims=True)


def _mem_fwd(proj, mem_kv):
    T = proj.shape[0]
    tr = _tile(T, (512, 256, 128))
    M = mem_kv.shape[0]

    def body(q_ref, mk_ref, mv_ref, o_ref):
        o_ref[...] = _dot(_mem_probs(q_ref[...], mk_ref[...]), mv_ref[...], NN)

    return pl.pallas_call(
        body, name="mem_fwd", grid=(NM, T // tr),
        in_specs=[pl.BlockSpec((tr, HD), lambda h, i: (i, CB_MQ + h)),
                  pl.BlockSpec((M, HD), lambda h, i: (0, h)), pl.BlockSpec((M, HD), lambda h, i: (0, NM + h))],
        out_specs=pl.BlockSpec((tr, HD), lambda h, i: (i, h)),
        out_shape=jax.ShapeDtypeStruct((T, WM), F32),
        compiler_params=_params(("parallel", "parallel")),
    )(proj, mem_kv, mem_kv)


def _mem_bwd(proj, mem_kv, do):
    T = proj.shape[0]
    tr = _tile(T, (512, 256, 128))
    M = mem_kv.shape[0]

    def body(q_ref, mk_ref, mv_ref, do_ref, dq_ref, dmk_ref, dmv_ref):
        @pl.when(pl.program_id(1) == 0)
        def _():
            dmk_ref[...] = jnp.zeros_like(dmk_ref)
            dmv_ref[...] = jnp.zeros_like(dmv_ref)

        q, mk, d_o = q_ref[...], mk_ref[...], do_ref[...]
        p = _mem_probs(q, mk)
        dmv_ref[...] += _dot(p, d_o, TN)
        dp = _dot(d_o, mv_ref[...], NT)
        ds = p * (dp - jnp.sum(p * dp, axis=1, keepdims=True))
        dq_ref[...] = (_dot(ds, mk, NN) * SCALE).astype(dq_ref.dtype)
        dmk_ref[...] += _dot(ds, q, TN) * SCALE

    acc = pl.BlockSpec((M, HD), lambda h, i: (0, h))
    row = pl.BlockSpec((tr, HD), lambda h, i: (i, h))
    return pl.pallas_call(
        body, name="mem_bwd", grid=(NM, T // tr),
        in_specs=[pl.BlockSpec((tr, HD), lambda h, i: (i, CB_MQ + h)),
                  pl.BlockSpec((M, HD), lambda h, i: (0, h)), pl.BlockSpec((M, HD), lambda h, i: (0, NM + h)), row],
        out_specs=[row, acc, acc],
        out_shape=[jax.ShapeDtypeStruct((T, WM), BF16), jax.ShapeDtypeStruct((M, WM), F32),
                   jax.ShapeDtypeStruct((M, WM), F32)],
        compiler_params=_params(("parallel", "arbitrary")),
    )(proj, mem_kv, mem_kv, do)


def _mesh_place():
    x, y, c = lax.axis_index("x"), lax.axis_index("y"), lax.axis_index("c")
    return x, y, c


CHIP_FLIPS = (4, 2, 6)
CHIP_OF_SLOT = (0,) + CHIP_FLIPS


def _peer(x, y, c, k):
    px = 1 - x if k & 4 else x
    py = 1 - y if k & 2 else y
    pc = 1 - c if k & 1 else c
    return (px, py, pc), 4 * px + 2 * py + pc


class _Gather:
    def __init__(self, shapes, pad_rows):
        self.shapes, self.pad_rows, self.n = shapes, pad_rows, len(shapes)
        self.npad = sum(1 for p in pad_rows if p)

    def zeros(self):
        return jnp.zeros((max(self.pad_rows) or 16, self.shapes[0][1]), BF16)

    def out_shape(self):
        return [jax.ShapeDtypeStruct((NDEV * r + p, c), BF16) for (r, c), p in zip(self.shapes, self.pad_rows)]

    def sems(self):
        return [pltpu.SemaphoreType.DMA((self.n, NDEV - 1)), pltpu.SemaphoreType.DMA((self.n, NDEV - 1)),
                pltpu.SemaphoreType.DMA((self.n + self.npad,))]

    def _copies(self, ins, z_ref, outs, send_sems, recv_sems, loc_sems):
        x, y, c = _mesh_place()
        me = 4 * x + 2 * y + c
        sibling, _ = _peer(x, y, c, 1)
        local, first, arrive, forward = [], [], [], []
        ip = 0
        for w in range(self.n):
            r = ins[w].shape[0]
            dst = outs[w].at[pl.ds(pl.multiple_of(me * r, 16), r), :]
            local.append(functools.partial(pltpu.make_async_copy, ins[w], dst, loc_sems.at[w]))
            if self.pad_rows[w]:
                local.append(functools.partial(pltpu.make_async_copy, z_ref.at[pl.ds(0, self.pad_rows[w]), :],
                                               outs[w].at[pl.ds(NDEV * r, self.pad_rows[w]), :], loc_sems.at[self.n + ip]))
                ip += 1

            def remote(src, dst_, s, to):
                return functools.partial(pltpu.make_async_remote_copy, src_ref=src, dst_ref=dst_, send_sem=send_sems.at[w, s],
                                         recv_sem=recv_sems.at[w, s], device_id=to, device_id_type=MESH)

            for s, k in enumerate((1,) + CHIP_FLIPS):
                first.append(remote(ins[w], dst, s, _peer(x, y, c, k)[0]))
            for s, k in enumerate(CHIP_FLIPS):
                _, pidx = _peer(x, y, c, k)
                rows = outs[w].at[pl.ds(pl.multiple_of(pidx * r, 16), r), :]
                arrive.append(remote(rows, rows, 1 + s, sibling))
                forward.append(remote(rows, rows, 4 + s, sibling))
        return local, first, arrive, forward


    def start(self, *refs):
        local, first, _, _ = self._copies(*refs)
        for make in local + first:
            make().start()

    def forward(self, *refs):
        _, _, arrive, forward = self._copies(*refs)
        for a, f in zip(arrive, forward):
            a().wait_recv()
            f().start()

    def finish(self, *refs):
        local, first, _, forward = self._copies(*refs)
        for make in local + first[0::4] + forward:
            make().wait()
        for s in (1, 2, 3):
            for make in first[s::4]:
                make().wait_send()


def _all_gather(shards, pad_rows):
    n = len(shards)
    plan = _Gather([s.shape for s in shards], pad_rows)

    def body(*refs):
        args = (refs[:n], refs[n], refs[n + 1:2 * n + 1]) + tuple(refs[2 * n + 1:])
        plan.start(*args)
        plan.forward(*args)
        plan.finish(*args)

    any_spec = pl.BlockSpec(memory_space=pl.ANY)
    return pl.pallas_call(
        body, name="all_gather_weights",
        in_specs=[any_spec] * (n + 1), out_specs=[any_spec] * n,
        out_shape=plan.out_shape(),
        scratch_shapes=plan.sems(),
        compiler_params=pltpu.CompilerParams(has_side_effects=True),
    )(*shards, plan.zeros())


def _exchange_in_chip(grads, shard_rows, name):
    n = len(grads)
    plan = _InChip([g.shape for g in grads], shard_rows)

    def body(*refs):
        args = (refs[:n], refs[n:2 * n]) + tuple(refs[2 * n:])
        plan.start(*args)
        plan.finish(*args)

    any_spec = pl.BlockSpec(memory_space=pl.ANY)
    return pl.pallas_call(
        body, name=name,
        in_specs=[any_spec] * n, out_specs=[any_spec] * n, out_shape=plan.out_shape(), scratch_shapes=plan.sems(),
        compiler_params=pltpu.CompilerParams(has_side_effects=True),
    )(*grads)


class _InChip:
    def __init__(self, shapes, shard_rows):
        self.shapes, self.rows, self.n, self.ns = shapes, shard_rows, len(shapes), len(CHIP_OF_SLOT)

    def out_shape(self):
        return [jax.ShapeDtypeStruct((self.ns, r, s[1]), BF16) for s, r in zip(self.shapes, self.rows)]

    def sems(self):
        return [pltpu.SemaphoreType.DMA((self.n, self.ns)), pltpu.SemaphoreType.DMA((self.n, self.ns))]

    def _copies(self, ins, theirs, send_sems, recv_sems):
        x, y, c = _mesh_place()
        sibling, _ = _peer(x, y, c, 1)
        copies = []
        for w in range(self.n):
            r = self.rows[w]
            for s, k in enumerate(CHIP_OF_SLOT):
                _, other = _peer(x, y, c, k | 1)
                copies.append(pltpu.make_async_remote_copy(
                    src_ref=ins[w].at[pl.ds(pl.multiple_of(other * r, 16), r), :], dst_ref=theirs[w].at[s],
                    send_sem=send_sems.at[w, s], recv_sem=recv_sems.at[w, s], device_id=sibling, device_id_type=MESH))
        return copies

    def start(self, *refs):
        for cp in self._copies(*refs):
            cp.start()

    def finish(self, *refs):
        for cp in self._copies(*refs):
            cp.wait()


def _pair_sum(grad, theirs, name):
    ns, r, c = theirs.shape
    tr = r if r * c <= 2 * 1024 * 1024 else _tile(r, (256, 128, 64, 32, 16))
    per_block = r // tr

    def body(a_ref, b_ref, o_ref):
        o_ref[...] = (a_ref[...].astype(F32) + b_ref[...].astype(F32)).astype(o_ref.dtype)

    def owner_rows(s, i):
        x, y, c_ = _mesh_place()
        fx, fy = s % 2, s // 2
        px, py = x + fx - 2 * x * fx, y + fy - 2 * y * fy
        return ((4 * px + 2 * py + c_) * per_block + i, 0)

    slot = pl.BlockSpec((None, tr, c), lambda s, i: (s, i, 0))
    return pl.pallas_call(
        body, name=name, grid=(ns, per_block),
        in_specs=[pl.BlockSpec((tr, c), owner_rows), slot], out_specs=slot,
        out_shape=jax.ShapeDtypeStruct((ns, r, c), theirs.dtype),
        compiler_params=_params(("parallel", "parallel")),
    )(grad, theirs)


def _exchange_between_chips(pairs, name):
    n = len(pairs)
    plan = _ChipExchange([p.shape for p in pairs])

    def body(*refs):
        args = (refs[:n], refs[n:2 * n]) + tuple(refs[2 * n:])
        plan.start(*args)
        plan.finish(*args)

    any_spec = pl.BlockSpec(memory_space=pl.ANY)
    return pl.pallas_call(
        body, name=name,
        in_specs=[any_spec] * n, out_specs=[any_spec] * n,
        out_shape=plan.out_shape(), scratch_shapes=plan.sems(),
        compiler_params=pltpu.CompilerParams(has_side_effects=True),
    )(*pairs)


class _ChipExchange:
    def __init__(self, shapes):
        self.shapes, self.n, self.ns = shapes, len(shapes), len(CHIP_OF_SLOT) - 1

    def out_shape(self):
        return [jax.ShapeDtypeStruct((self.ns,) + tuple(s[1:]), BF16) for s in self.shapes]

    def sems(self):
        return [pltpu.SemaphoreType.DMA((self.n, self.ns)), pltpu.SemaphoreType.DMA((self.n, self.ns))]

    def _copies(self, ins, outs, send_sems, recv_sems):
        x, y, c = _mesh_place()
        copies = []
        for w in range(self.n):
            for s, k in enumerate(CHIP_OF_SLOT[1:]):
                peer, _ = _peer(x, y, c, k)
                copies.append(pltpu.make_async_remote_copy(
                    src_ref=ins[w].at[s + 1], dst_ref=outs[w].at[s], send_sem=send_sems.at[w, s],
                    recv_sem=recv_sems.at[w, s], device_id=peer, device_id_type=MESH))
        return copies

    def start(self, *refs):
        for cp in self._copies(*refs):
            cp.start()

    def finish(self, *refs):
        for cp in self._copies(*refs):
            cp.wait()


def _sum_chips(pair, recv, name):
    ns, r, c = recv.shape
    tr, tc = _panel(r, c)

    def body(p_ref, x_ref, o_ref):
        acc = p_ref[...].astype(F32)
        for s in range(x_ref.shape[0]):
            acc = acc + x_ref[s].astype(F32)
        o_ref[...] = acc

    return pl.pallas_call(
        body, name=name, grid=(r // tr, c // tc),
        in_specs=[pl.BlockSpec((None, tr, tc), lambda i, j: (0, i, j)), pl.BlockSpec((ns, tr, tc), lambda i, j: (0, i, j))],
        out_specs=pl.BlockSpec((tr, tc), lambda i, j: (i, j)),
        out_shape=jax.ShapeDtypeStruct((r, c), F32),
        compiler_params=_params(("parallel", "parallel")),
    )(pair, recv)


def _panel(r, c):
    for tr in (1024, 512, 256, 128):
        if r % tr == 0 and tr * c <= 512 * 1024:
            return tr, c
    for tc in (2048, 1024, 512, 256, 128):
        if c % tc == 0 and r * tc <= 512 * 1024:
            return r, tc
    return _tile(r, (64, 32, 16, 8)), c


def _all_reduce_small(part):
    R, W = part.shape

    def body(x_ref, o_ref, buf, send_sems, recv_sems):
        x, y, c = _mesh_place()
        me = 4 * x + 2 * y + c
        buf[me] = x_ref[...]
        copies = []
        for k in range(1, NDEV):
            peer, _ = _peer(x, y, c, k)
            cp = pltpu.make_async_remote_copy(src_ref=x_ref, dst_ref=buf.at[me], send_sem=send_sems.at[k - 1],
                                              recv_sem=recv_sems.at[k - 1], device_id=peer, device_id_type=MESH)
            cp.start()
            copies.append(cp)
        for cp in copies:
            cp.wait()
        acc = buf[0]
        for d in range(1, NDEV):
            acc = acc + buf[d]
        o_ref[...] = acc

    vm = pl.BlockSpec(memory_space=pltpu.VMEM)
    return pl.pallas_call(
        body, name="all_reduce_small", in_specs=[vm], out_specs=vm,
        out_shape=jax.ShapeDtypeStruct((R, W), F32),
        scratch_shapes=[pltpu.VMEM((NDEV, R, W), F32), pltpu.SemaphoreType.DMA((NDEV - 1,)),
                        pltpu.SemaphoreType.DMA((NDEV - 1,))],
        compiler_params=pltpu.CompilerParams(has_side_effects=True),
    )(part)


def _adam_math(w, g, m, v):
    m2 = ADAM_B1 * m + (1.0 - ADAM_B1) * g
    v2 = ADAM_B2 * v + (1.0 - ADAM_B2) * (g * g)
    m_hat = m2 / (1.0 - ADAM_B1 ** ADAM_STEP)
    v_hat = v2 / (1.0 - ADAM_B2 ** ADAM_STEP)
    delta = -ADAM_LR * (m_hat / (jnp.sqrt(v_hat) + ADAM_EPS) + ADAM_WD * w)
    return delta, m2, v2


def _adamw(w, g, m, v, name):
    r, c = w.shape
    tr, tc = _panel(r, c)

    def body(w_ref, g_ref, m_ref, v_ref, d_ref, m2_ref, v2_ref):
        d_ref[...], m2_ref[...], v2_ref[...] = _adam_math(w_ref[...], g_ref[...], m_ref[...], v_ref[...])

    spec = pl.BlockSpec((tr, tc), lambda i, j: (i, j))
    return pl.pallas_call(
        body, name=name, grid=(r // tr, c // tc), in_specs=[spec] * 4, out_specs=[spec] * 3,
        out_shape=[jax.ShapeDtypeStruct((r, c), F32)] * 3,
        compiler_params=_params(("parallel", "parallel")),
    )(w, g, m, v)


GAINS = ("ffn1_pre", "ffn1_post", "mix_pre", "mix_post", "mem_norm", "ffn2_pre", "ffn2_post")
GAIN_ROWS = D // HD
ROW_LB = len(GAINS) * GAIN_ROWS
ROWS_GRAD_IN = ROW_LB + 24
ROWS_PACKED = ROW_LB + 32


def _small_update(gsum, w_p, m_p, v_p):
    def body(g_ref, w_ref, m_ref, v_ref, go_ref, d_ref, m2_ref, v2_ref):
        a0 = w_ref[ROW_LB:ROW_LB + 8, :]
        a1 = w_ref[ROW_LB + 8:ROW_LB + 16, :]
        mx = jnp.maximum(a0, a1)
        e0, e1 = jnp.exp(a0 - mx), jnp.exp(a1 - mx)
        lb = e0 / (e0 + e1)
        da0 = g_ref[ROW_LB:ROW_LB + 8, :] * lb * (1.0 - lb)
        g = jnp.concatenate([g_ref[0:ROW_LB, :], da0, -da0, g_ref[ROW_LB + 8:ROWS_GRAD_IN, :]], axis=0)
        go_ref[...] = g
        d_ref[...], m2_ref[...], v2_ref[...] = _adam_math(w_ref[...], g, m_ref[...], v_ref[...])

    vm = pl.BlockSpec(memory_space=pltpu.VMEM)
    return pl.pallas_call(
        body, name="small_update", in_specs=[vm] * 4, out_specs=[vm] * 4,
        out_shape=[jax.ShapeDtypeStruct((ROWS_PACKED, HD), F32)] * 4,
    )(gsum, w_p, m_p, v_p)


def _rows8(a):
    a = a.reshape(-1)
    rows = -(-a.shape[0] // HD)
    rows8 = -(-rows // 8) * 8
    return jnp.pad(a, (0, rows8 * HD - a.shape[0])).reshape(rows8, HD)


def _pack_small(gains, lb0, lb1, gnorm, fb):
    return jnp.concatenate([_rows8(g) for g in gains] + [_rows8(lb0), _rows8(lb1), _rows8(gnorm), _rows8(fb)], axis=0)


def _unpack_small(p):
    out = {}
    for i, name in enumerate(GAINS):
        out[name] = p[i * GAIN_ROWS:(i + 1) * GAIN_ROWS].reshape(1, D)
    lb0 = p[ROW_LB:ROW_LB + NH].reshape(1, WH)
    lb1 = p[ROW_LB + 8:ROW_LB + 8 + NH].reshape(1, WH)
    out["hgrn_lb"] = jnp.concatenate([lb0, lb1], axis=0)
    out["hgrn_gnorm"] = p[ROW_LB + 16:ROW_LB + 16 + NH].reshape(1, WH)
    out["fox_fb"] = p[ROW_LB + 24:ROW_LB + 25, 0:NH]
    return out


def _ffn_forward(n, wg_t, wu_t, wd, tag, rider=None, rider_down=None):
    g, u, a, *carried = _ffn_up(n, wg_t, wu_t, f"{tag}_up", rider)
    if wd is None:
        wd = carried[0]
    if rider_down is None:
        h = _mm(a, wd, "nn", F32, f"{tag}_down")
    else:
        h, *more = _mm(a, wd, "nn", F32, f"{tag}_down", rider=rider_down)
        carried = carried + more
    return h, (n, g, u, a), carried


def _mm_out(res):
    return (res[0], list(res[1:])) if isinstance(res, (list, tuple)) else (res, [])


def _ffn_backward(dh, saved, wg_t, wu_t, wd, tag, rider=None, exchange=None, rider_dwd=None, after_dwd=None):
    n, g, u, a = saved
    dwd, got = _mm_out(_mm(a, dh, "tn", BF16, f"{tag}_dwd", rider=rider_dwd))
    rider_dwg = None
    if after_dwd is not None:
        rider, rider_dwg = after_dwd(got)
    dg, du, *carried = _ffn_act_bwd(dh, wd, g, u, f"{tag}_act_bwd", rider)
    dwg, got = _mm_out(_mm(dg, n, "tn", BF16, f"{tag}_dwg", rider=rider_dwg))
    carried = carried + got
    dwu = _mm(du, n, "tn", BF16, f"{tag}_dwu")
    if exchange is None:
        dn = _mm(dg, wg_t, "nn", F32, f"{tag}_dn_g")
        dn = _mm(du, wu_t, "nn", F32, f"{tag}_dn_u", add=dn)
    else:
        ride_a, ride_b, take = exchange(dwg, dwu, dwd)
        dn, got_a = _mm_out(_mm(dg, wg_t, "nn", F32, f"{tag}_dn_g", rider=ride_a))
        dn, got_b = _mm_out(_mm(du, wu_t, "nn", F32, f"{tag}_dn_u", add=dn, rider=ride_b))
        take(got_a, got_b)
    return dn, (dwg, dwu, dwd), carried


GATHER_FIRST = ("ffn1_wg", "ffn1_wu")
GATHER_IN_FFN1_UP = ("ffn1_wd", "w_in")
GATHER_IN_FFN1_DOWN = ("w_gate",)
GATHER_IN_PROJ = ("w_mem_kv", "w_hgrn_out", "w_fox_out", "w_mem_out", "w_o")
GATHER_IN_GATE = ("ffn2_wg",)
GATHER_IN_HGRN = ("ffn2_wu",)
GATHER_IN_FFN2_UP = ("ffn2_wd",)
GROUP_FFN1 = ("ffn1_wg", "ffn1_wu", "ffn1_wd")
GROUP_MIX = ("w_in", "w_mem_kv", "w_hgrn_out", "w_fox_out", "w_mem_out", "w_gate", "w_o")
GROUP_FFN2 = ("ffn2_wg", "ffn2_wu", "ffn2_wd")


def _gather_rider(blocks, names):
    plan = _Gather([blocks[n].shape for n in names], [FFN_PAD.get(n, 0) for n in names])
    return _Rider(plan, [blocks[n] for n in names] + [plan.zeros()], GATHER_STEPS)


def _local_step(x, mem, tgt, small, wts=None, blocks=None):
    T = x.shape[0]
    tr = _tile(T, (256, 128))
    fb_pad = jnp.pad(small["fox_fb"], ((0, 0), (0, HD - NH)))
    dist = blocks is not None
    if dist:
        wts = dict(zip(GATHER_FIRST, _all_gather([blocks[n] for n in GATHER_FIRST], [FFN_PAD[n] for n in GATHER_FIRST])))

    def riding(names):
        return _gather_rider(blocks, names) if dist else None

    (n1,) = _rowwise(_norm_fn, [(x, 0)], [(small["ffn1_pre"], None)], [BF16], "ffn1_pre", tr, D, 1)
    h1, ffn1_saved, carried = _ffn_forward(n1, wts["ffn1_wg"], wts["ffn1_wu"], wts.get("ffn1_wd"), "ffn1",
                                           riding(GATHER_IN_FFN1_UP), riding(GATHER_IN_FFN1_DOWN))
    wts.update(zip(GATHER_IN_FFN1_UP + GATHER_IN_FFN1_DOWN, carried))
    ffn1_out = functools.partial(_post_pre_fn, 0.5)
    x1, un = _rowwise(ffn1_out, [(x, 0), (h1, 0)], [(small["ffn1_post"], None), (small["mix_pre"], None)], [F32, BF16],
                      "ffn1_post_mix_pre", tr, D, 1)
    if dist:
        proj, *carried = _mm(un, wts["w_in"], "nn", F32, "proj", rider=riding(GATHER_IN_PROJ))
        wts.update(zip(GATHER_IN_PROJ, carried))
        z, *carried = _mm(un, wts["w_gate"], "nt", BF16, "gate_logits", rider=riding(GATHER_IN_GATE))
        wts.update(zip(GATHER_IN_GATE, carried))
    else:
        proj = _mm(un, wts["w_in"], "nn", F32, "proj")
        z = _mm(un, wts["w_gate"], "nt", BF16, "gate_logits")
    (memn,) = _rowwise(_norm_fn, [(mem, 0)], [(small["mem_norm"], None)], [BF16], "mem_norm", mem.shape[0], D, 1)
    mem_kv = _mm(memn, wts["w_mem_kv"], "nn", F32, "mem_kv")

    o_raw, states, *carried = _hgrn_fwd(proj, small["hgrn_lb"], riding(GATHER_IN_HGRN))
    wts.update(zip(GATHER_IN_HGRN, carried))
    tr_head = _tile(T, (1024, 512, 256, 128))
    (o_h,) = _rowwise(_hpost_fn, [(o_raw, 0), (proj, CB_HOG)], [(small["hgrn_gnorm"], 0)], [BF16], "hgrn_post",
                      tr_head, HD, NH)
    ct, cq = _fox_cum(proj, fb_pad)
    win = _fox_windows(proj, cq)
    o_f, lse = _fox_fwd(win, proj, ct, cq)
    o_m = _mem_fwd(proj, mem_kv)

    yh = _mm(o_h, wts["w_hgrn_out"], "nt", BF16, "hgrn_out")
    yf = _mm(o_f, wts["w_fox_out"], "nt", BF16, "fox_out")
    ym = _mm(o_m, wts["w_mem_out"], "nt", BF16, "mem_out")
    zc = D // 512
    merge_rows = [(z, 0), (z, zc), (z, 2 * zc), (yh, 0), (yf, 0), (ym, 0)]
    tr_merge = _tile(T, (512, 256, 128))
    (merged,) = _rowwise(_merge_fn, merge_rows, [], [BF16], "merge", tr_merge, 512, zc)
    m = _mm(merged, wts["w_o"], "nn", F32, "mix_out")
    mix_out = functools.partial(_post_pre_fn, 1.0)
    x2, n2 = _rowwise(mix_out, [(x1, 0), (m, 0)], [(small["mix_post"], None), (small["ffn2_pre"], None)], [F32, BF16],
                      "mix_post_ffn2_pre", tr, D, 1)
    h2, ffn2_saved, carried = _ffn_forward(n2, wts["ffn2_wg"], wts["ffn2_wu"], wts.get("ffn2_wd"), "ffn2",
                                           riding(GATHER_IN_FFN2_UP))
    wts.update(zip(GATHER_IN_FFN2_UP, carried))
    dy, loss_part = _loss(x2, h2, small["ffn2_post"], tgt, "loss")

    gw, gs, reduced = {}, {}, {}

    def pair_sums(names, tag):
        if not dist:
            return None, None
        theirs = brought.get(tag)
        if theirs is None:
            theirs = _exchange_in_chip([gw[n] for n in names], [blocks[n].shape[0] for n in names], f"reduce_in_chip_{tag}")
        pairs = [_pair_sum(gw[n], t_, f"pair_{n}") for n, t_ in zip(names, theirs)]
        return pairs, _Rider(_ChipExchange([p.shape for p in pairs]), pairs, EXCHANGE_STEPS)

    def chip_sums(names, pairs, recv):
        for n, p_, r_ in zip(names, pairs or (), recv):
            reduced[n] = _sum_chips(p_, r_, f"sum_{n}")

    brought = {}

    def in_chip_rider(names):
        grads_ = [gw[n] for n in names]
        return _Rider(_InChip([g_.shape for g_ in grads_], [blocks[n].shape[0] for n in names]), grads_, EXCHANGE_STEPS)

    def ffn2_exchange(dwg, dwu, dwd):
        gw.update(ffn2_wg=dwg, ffn2_wu=dwu, ffn2_wd=dwd)
        return in_chip_rider(GROUP_FFN2), None, lambda got_a, got_b: brought.update(ffn2=got_a)

    dh2, gs["ffn2_post"] = _rowwise_bwd(functools.partial(_resid_h_fn, 0.5), [(h2, 0)], [(small["ffn2_post"], None)],
                                        [(dy, 0)], [0], [BF16], "ffn2_post_bwd", tr, D, 1)
    dn2, (gw["ffn2_wg"], gw["ffn2_wu"], gw["ffn2_wd"]), _ = _ffn_backward(
        dh2, ffn2_saved, wts["ffn2_wg"], wts["ffn2_wu"], wts["ffn2_wd"], "ffn2", exchange=ffn2_exchange if dist else None)
    pairs_ffn2, ride_ffn2_grads = pair_sums(GROUP_FFN2, "ffn2")

    dx1, dm, gs["mix_post"], gs["ffn2_pre"] = _rowwise_bwd(
        mix_out, [(x1, 0), (m, 0)], [(small["mix_post"], None), (small["ffn2_pre"], None)], [(dy, 0), (dn2, 0)], [0, 1],
        [F32, BF16], "mix_post_ffn2_pre_bwd", tr, D, 1)
    dmerged = _mm(dm, wts["w_o"], "nt", F32, "d_merged")
    gw["w_o"] = _mm(merged, dm, "tn", BF16, "d_w_o")
    dz0, dz1, dz2, dyh, dyf, dym = _rowwise_bwd(_merge_fn, merge_rows, [], [(dmerged, 0)], [0, 1, 2, 3, 4, 5], [BF16] * 6,
                                                "merge_bwd", tr_merge, 512, zc)
    dz = jnp.concatenate([dz0, dz1, dz2], axis=1)
    gw["w_gate"] = _mm(dz, un, "tn", BF16, "d_w_gate")
    dun = _mm(dz, wts["w_gate"], "nn", F32, "d_un_gate")

    do_h = _mm(dyh, wts["w_hgrn_out"], "nn", F32, "d_o_h")
    gw["w_hgrn_out"] = _mm(dyh, o_h, "tn", BF16, "d_w_hgrn_out")
    do_f = _mm(dyf, wts["w_fox_out"], "nn", F32, "d_o_f")
    gw["w_fox_out"] = _mm(dyf, o_f, "tn", BF16, "d_w_fox_out")
    do_m = _mm(dym, wts["w_mem_out"], "nn", F32, "d_o_m")
    gw["w_mem_out"] = _mm(dym, o_m, "tn", BF16, "d_w_mem_out")

    do_raw, dhog, gs["hgrn_gnorm"] = _rowwise_bwd(_hpost_fn, [(o_raw, 0), (proj, CB_HOG)], [(small["hgrn_gnorm"], 0)],
                                                  [(do_h, 0)], [0, 1], [F32, BF16], "hgrn_post_bwd", tr_head, HD, NH)
    dhq, dhf, dhi, gs["hgrn_lb"], *carried = _hgrn_bwd(proj, small["hgrn_lb"], states, do_raw, ride_ffn2_grads)
    chip_sums(GROUP_FFN2, pairs_ffn2, carried)
    dfq, delta = _fox_bwd_dq(win, proj, ct, cq, lse, do_f)
    dfk, dfv, dc = _fox_bwd_dkv(win, proj, ct, cq, lse, delta, do_f)
    dff, dfb = _fox_cum_bwd(dc, proj, fb_pad)
    gs["fox_fb"] = dfb
    dmq, dmk, dmv = _mem_bwd(proj, mem_kv, do_m)

    dproj = jnp.concatenate([dhq, dhf, dhi, dhog, dfq, dfk, dfv, dff, dmq, jnp.zeros((T, HD), BF16)], axis=1)
    gw["w_in"] = _mm(un, dproj, "tn", BF16, "d_w_in")
    dun = _mm(dproj, wts["w_in"], "nt", F32, "d_un_proj", add=dun)
    dx0, dh1, gs["ffn1_post"], gs["mix_pre"] = _rowwise_bwd(
        ffn1_out, [(x, 0), (h1, 0)], [(small["ffn1_post"], None), (small["mix_pre"], None)], [(dx1, 0), (dun, 0)], [0, 1],
        [F32, BF16], "ffn1_post_mix_pre_bwd", tr, D, 1)

    dmem_kv = jnp.concatenate([dmk, dmv], axis=1)
    gw["w_mem_kv"] = _mm(memn, dmem_kv, "tn", BF16, "d_w_mem_kv")
    dmemn = _mm(dmem_kv, wts["w_mem_kv"], "nt", F32, "d_memn")
    _, gs["mem_norm"] = _rowwise_bwd(_norm_fn, [(mem, 0)], [(small["mem_norm"], None)], [(dmemn, 0)], [0], [BF16],
                                     "mem_norm_bwd", mem.shape[0], D, 1)

    mix = {}

    def mix_after_dwd(got):
        brought.update(mix=got)
        pairs, _ = pair_sums(GROUP_MIX, "mix")
        mix["names"] = GROUP_MIX[1:] + GROUP_MIX[:1]
        mix["pairs"] = pairs[1:] + pairs[:1]
        return tuple(_Rider(_ChipExchange([p.shape for p in part]), part, EXCHANGE_STEPS) for part in (pairs[1:], pairs[:1]))

    def own_exchange(dwg, dwu, dwd):
        gw.update(ffn1_wg=dwg, ffn1_wu=dwu, ffn1_wd=dwd)
        pairs, _ = pair_sums(GROUP_FFN1, "ffn1")
        first, second = pairs[:2], pairs[2:]

        def take(got_a, got_b):
            chip_sums(GROUP_FFN1, pairs, list(got_a) + list(got_b))

        return (_Rider(_ChipExchange([p.shape for p in first]), first, EXCHANGE_STEPS),
                _Rider(_ChipExchange([p.shape for p in second]), second, EXCHANGE_STEPS), take)

    dn1, (gw["ffn1_wg"], gw["ffn1_wu"], gw["ffn1_wd"]), carried = _ffn_backward(
        dh1, ffn1_saved, wts["ffn1_wg"], wts["ffn1_wu"], wts["ffn1_wd"], "ffn1",
        exchange=own_exchange if dist else None, rider_dwd=in_chip_rider(GROUP_MIX) if dist else None,
        after_dwd=mix_after_dwd if dist else None)
    chip_sums(mix.get("names", ()), mix.get("pairs"), carried)
    dx, gs["ffn1_pre"] = _rowwise_bwd(_norm_res_fn, [(x, 0)], [(small["ffn1_pre"], None)], [(dx0, 0), (dn1, 0)], [0], [F32],
                                      "ffn1_pre_bwd", tr, D, 1)
    return loss_part, dx, (reduced if dist else gw), gs


BIG = ("ffn1_wg", "ffn1_wu", "ffn1_wd", "w_in", "w_mem_kv", "w_hgrn_out", "w_fox_out", "w_mem_out", "w_gate", "w_o",
       "ffn2_wg", "ffn2_wu", "ffn2_wd")
TRANSPOSED = ("ffn1_wg", "ffn1_wu", "ffn2_wg", "ffn2_wu", "w_hgrn_out", "w_fox_out", "w_mem_out", "w_gate")
FFN_PAD = {"ffn1_wg": FP - F, "ffn1_wu": FP - F, "ffn1_wd": FP - F, "ffn2_wg": FP - F, "ffn2_wu": FP - F,
           "ffn2_wd": FP - F}
SMALL = GAINS + ("hgrn_lb", "hgrn_gnorm", "fox_fb")
WEIGHTS = ("ffn1_pre", "ffn1_post", "ffn1_wg", "ffn1_wu", "ffn1_wd", "mix_pre", "mix_post", "mem_norm", "w_in", "hgrn_lb",
           "hgrn_gnorm", "fox_fb", "w_mem_kv", "w_hgrn_out", "w_fox_out", "w_mem_out", "w_gate", "w_o", "ffn2_pre",
           "ffn2_post", "ffn2_wg", "ffn2_wu", "ffn2_wd")


def _to_gather_layout(name, w):
    if name in TRANSPOSED:
        w = w.T
    if name == "w_in":
        r = w.shape[0]
        w = jnp.concatenate([w[:, :MQ_COL], jnp.zeros((r, FF_COL + HD - MQ_COL), w.dtype), w[:, MQ_COL:],
                             jnp.zeros((r, P - FF_COL - HD - WM), w.dtype)], axis=1)
    return w.astype(BF16)


def _from_gather_layout(name, g):
    if name == "w_in":
        g = jnp.concatenate([g[:, :MQ_COL], g[:, FF_COL + HD:FF_COL + HD + WM]], axis=1)
    if name in TRANSPOSED:
        g = g.T
    return g


def kernel(x, mem, ffn1_pre, ffn1_post, ffn1_wg, ffn1_wu, ffn1_wd, mix_pre, mix_post, mem_norm, w_in, hgrn_lb, hgrn_gnorm, fox_fb, w_mem_kv, w_hgrn_out, w_fox_out, w_mem_out, w_gate, w_o, ffn2_pre, ffn2_post, ffn2_wg, ffn2_wu, ffn2_wd, loss_target, m_ffn1_pre, m_ffn1_post, m_ffn1_wg, m_ffn1_wu, m_ffn1_wd, m_mix_pre, m_mix_post, m_mem_norm, m_w_in, m_hgrn_lb, m_hgrn_gnorm, m_fox_fb, m_w_mem_kv, m_w_hgrn_out, m_w_fox_out, m_w_mem_out, m_w_gate, m_w_o, m_ffn2_pre, m_ffn2_post, m_ffn2_wg, m_ffn2_wu, m_ffn2_wd, v_ffn1_pre, v_ffn1_post, v_ffn1_wg, v_ffn1_wu, v_ffn1_wd, v_mix_pre, v_mix_post, v_mem_norm, v_w_in, v_hgrn_lb, v_hgrn_gnorm, v_fox_fb, v_w_mem_kv, v_w_hgrn_out, v_w_fox_out, v_w_mem_out, v_w_gate, v_w_o, v_ffn2_pre, v_ffn2_post, v_ffn2_wg, v_ffn2_wu, v_ffn2_wd):
    a = dict(locals())
    small = {n: a[n] for n in SMALL}
    shard = {n: a[n][0] if a[n].ndim == 3 else a[n] for n in BIG}

    blocks = {n: _to_gather_layout(n, shard[n]) for n in BIG}
    loss_part, dx, reduced, gs = _local_step(x[0], mem[0], loss_target[0], small, blocks=blocks)
    loss = lax.psum(0.5 / D * jnp.sum(loss_part), ("x", "y", "c"))

    grads, deltas, new_m, new_v = {}, {}, {}, {}
    for n in BIG:
        g = _from_gather_layout(n, reduced[n])
        d, m2, v2 = _adamw(shard[n], g, a["m_" + n].reshape(g.shape), a["v_" + n].reshape(g.shape), f"adamw_{n}")
        full = a[n].shape
        grads[n], deltas[n], new_m[n], new_v[n] = g.reshape(full), d.reshape(full), m2.reshape(full), v2.reshape(full)

    part = jnp.concatenate([_rows8(gs[n]) for n in GAINS] + [_rows8(gs["hgrn_lb"]), _rows8(gs["hgrn_gnorm"]),
                                                             _rows8(gs["fox_fb"][:, :NH])], axis=0)
    gsum = _all_reduce_small(part)

    def packed(prefix):
        lb = a[prefix + "hgrn_lb"]
        return _pack_small([a[prefix + n] for n in GAINS], lb[0], lb[1], a[prefix + "hgrn_gnorm"], a[prefix + "fox_fb"])

    g_p, d_p, m_p, v_p = _small_update(gsum, packed(""), packed("m_"), packed("v_"))
    for dst, p in ((grads, g_p), (deltas, d_p), (new_m, m_p), (new_v, v_p)):
        dst.update(_unpack_small(p))

    return (loss, dx[None], *[grads[n] for n in WEIGHTS], *[deltas[n] for n in WEIGHTS],
            *[new_m[n] for n in WEIGHTS], *[new_v[n] for n in WEIGHTS])
```

```python
import functools

import jax
import jax.numpy as jnp
from jax import lax
from jax.experimental import pallas as pl
from jax.experimental.pallas import tpu as pltpu

F32 = jnp.float32
BF16 = jnp.bfloat16
HIGHEST = lax.Precision.HIGHEST

NDEV = 8
D = 2048
F = 5504
FP = 5632
HD = 128
NH = 6
NM = 4
WH = NH * HD
WM = NM * HD
P = 6144
FF_COL = 5376
MQ_COL = 5382
CHUNK = 64
EPS = 1e-6
SCALE = HD ** -0.5
NEG = -1e30
VMEM_LIMIT = 48 * 1024 * 1024

CB_HQ, CB_HF, CB_HI, CB_HOG, CB_FQ, CB_FK, CB_FV, CB_FF, CB_MQ = 0, 6, 12, 18, 24, 30, 36, 42, 43

ADAM_LR, ADAM_B1, ADAM_B2, ADAM_EPS, ADAM_WD, ADAM_STEP = 0.001, 0.9, 0.999, 1e-08, 0.01, 10

NT = (((1,), (1,)), ((), ()))
NN = (((1,), (0,)), ((), ()))
TN = (((0,), (0,)), ((), ()))
MESH = pl.DeviceIdType.MESH


def _params(sem=None, **kw):
    return pltpu.CompilerParams(dimension_semantics=sem, vmem_limit_bytes=VMEM_LIMIT, **kw)


def _tile(n, prefs):
    for p in prefs:
        if p <= n and n % p == 0:
            return p
    return n


def _dot(a, b, dims):
    return lax.dot_general(a.astype(BF16), b.astype(BF16), dims, preferred_element_type=F32)


def _mm(a, b, mode, out_dtype, name, add=None, rider=None):
    if mode == "nn":
        (M, K), (K2, N) = a.shape, b.shape
    elif mode == "nt":
        (M, K), (N, K2) = a.shape, b.shape
    else:
        (K, M), (K2, N) = a.shape, b.shape
    assert K == K2, (a.shape, b.shape, mode)
    if mode == "tn":
        tm = _tile(M, (512, 256, 128))
        tn = _tile(N, (1024, 768, 512, 256, 128))
        tk = _tile(K, (4096, 2048, 1024, 512, 256, 128))
    else:
        tm = _tile(M, (1024, 512, 256, 128)) if K <= 2048 else _tile(M, (512, 256, 128))
        tn = _tile(N, (512, 768, 256, 128))
        tk = K if K <= 6144 else _tile(K, (2048, 1024, 512, 256, 128))
    nk = K // tk
    dims = {"nn": NN, "nt": NT, "tn": TN}[mode]
    has_add = add is not None

    ni, nj = M // tm, N // tn
    n_in = 3 if has_add else 2

    def body(*refs):
        step = (pl.program_id(0) * nj + pl.program_id(1)) * nk + pl.program_id(2)
        refs = _carry(rider, refs, n_in, 1, 1, step, ni * nj * nk)
        a_ref, b_ref = refs[0], refs[1]
        c_ref = refs[2] if has_add else None
        o_ref = refs[3] if has_add else refs[2]
        acc_ref = refs[-1]
        k = pl.program_id(2)
        part = _dot(a_ref[...], b_ref[...], dims)

        def finish(r):
            if has_add:
                r = r + c_ref[...].astype(F32)
            o_ref[...] = r.astype(o_ref.dtype)

        if nk == 1:
            finish(part)
        else:
            @pl.when(k == 0)
            def _():
                acc_ref[...] = part

            @pl.when(k > 0)
            def _():
                acc_ref[...] += part

            @pl.when(k == nk - 1)
            def _():
                finish(acc_ref[...])

    if mode == "nn":
        a_spec = pl.BlockSpec((tm, tk), lambda i, j, k: (i, k))
        b_spec = pl.BlockSpec((tk, tn), lambda i, j, k: (k, j))
    elif mode == "nt":
        a_spec = pl.BlockSpec((tm, tk), lambda i, j, k: (i, k))
        b_spec = pl.BlockSpec((tn, tk), lambda i, j, k: (j, k))
    else:
        a_spec = pl.BlockSpec((tk, tm), lambda i, j, k: (k, i))
        b_spec = pl.BlockSpec((tk, tn), lambda i, j, k: (k, j))
    o_spec = pl.BlockSpec((tm, tn), lambda i, j, k: (i, j))
    args = (a, b) + ((add,) if has_add else ())
    in_specs, out_specs, out_shape, scratch, extra = _with_rider(
        rider, [a_spec, b_spec] + ([o_spec] if has_add else []), [o_spec], [jax.ShapeDtypeStruct((M, N), out_dtype)],
        [pltpu.VMEM((tm, tn) if nk > 1 else (8, 128), F32)])
    out = pl.pallas_call(
        body, name=name, grid=(ni, nj, nk), in_specs=in_specs, out_specs=out_specs, out_shape=out_shape,
        scratch_shapes=scratch,
        compiler_params=_params(("arbitrary",) * 3 if rider else ("parallel", "parallel", "arbitrary"),
                                has_side_effects=rider is not None),
    )(*args, *extra)
    return out if rider else out[0]


class _Rider:
    def __init__(self, plan, inputs, steps):
        self.plan, self.inputs, self.steps = plan, list(inputs), steps
        self.n_out = len(plan.out_shape())
        self.n_sem = len(plan.sems())

    def run(self, step, total, in_refs, out_refs, sem_refs):
        n = self.plan.n
        if isinstance(self.plan, _Gather):
            args = (in_refs[:n], in_refs[n], out_refs) + tuple(sem_refs)
        else:
            args = (in_refs, out_refs) + tuple(sem_refs)
        for frac, method in self.steps:
            @pl.when(step == int(frac * (total - 1)))
            def _(method=method):
                getattr(self.plan, method)(*args)


GATHER_STEPS = ((0.0, "start"), (0.6, "forward"), (1.0, "finish"))
EXCHANGE_STEPS = ((0.0, "start"), (1.0, "finish"))


def _carry(rider, refs, n_in, n_out, n_scratch, step, total):
    if rider is None:
        return refs
    ri, ro, rs = len(rider.inputs), rider.n_out, rider.n_sem
    own_in, rid_in = refs[:n_in], refs[n_in:n_in + ri]
    own_out, rid_out = refs[n_in + ri:n_in + ri + n_out], refs[n_in + ri + n_out:n_in + ri + n_out + ro]
    own_scr, rid_sem = refs[n_in + ri + n_out + ro:n_in + ri + n_out + ro + n_scratch], refs[len(refs) - rs:]
    rider.run(step, total, rid_in, rid_out, rid_sem)
    return tuple(own_in) + tuple(own_out) + tuple(own_scr)


def _with_rider(rider, in_specs, out_specs, out_shape, scratch):
    if rider is None:
        return in_specs, out_specs, out_shape, scratch, ()
    any_spec = pl.BlockSpec(memory_space=pl.ANY)
    return (list(in_specs) + [any_spec] * len(rider.inputs), list(out_specs) + [any_spec] * rider.n_out,
            list(out_shape) + rider.plan.out_shape(), list(scratch) + rider.plan.sems(), tuple(rider.inputs))


def _ffn_up(n, wg_t, wu_t, name, rider=None):
    T = n.shape[0]
    tm = _tile(T, (1024, 512, 256, 128))
    tn = 512
    ni, nj = T // tm, FP // tn

    def body(*refs):
        step = pl.program_id(0) * nj + pl.program_id(1)
        n_ref, wg_ref, wu_ref, g_ref, u_ref, a_ref = _carry(rider, refs, 3, 3, 0, step, ni * nj)
        x = n_ref[...]
        g = _dot(x, wg_ref[...], NT)
        u = _dot(x, wu_ref[...], NT)
        g_ref[...] = g.astype(g_ref.dtype)
        u_ref[...] = u.astype(u_ref.dtype)
        a_ref[...] = (g * jax.nn.sigmoid(g) * u).astype(BF16)

    w_spec = pl.BlockSpec((tn, D), lambda i, j: (j, 0))
    o_spec = pl.BlockSpec((tm, tn), lambda i, j: (i, j))
    in_specs, out_specs, out_shape, scratch, extra = _with_rider(
        rider, [pl.BlockSpec((tm, D), lambda i, j: (i, 0)), w_spec, w_spec], [o_spec, o_spec, o_spec],
        [jax.ShapeDtypeStruct((T, FP), BF16)] * 3, [])
    return pl.pallas_call(
        body, name=name, grid=(ni, nj), in_specs=in_specs, out_specs=out_specs, out_shape=out_shape,
        scratch_shapes=scratch,
        compiler_params=_params(("arbitrary", "arbitrary") if rider else ("parallel", "parallel"),
                                has_side_effects=rider is not None),
    )(n, wg_t, wu_t, *extra)


def _ffn_act_bwd(dh, wd, g, u, name, rider=None):
    T = dh.shape[0]
    tm = _tile(T, (1024, 512, 256, 128))
    tn = 512
    ni, nj = T // tm, FP // tn

    def body(*refs):
        step = pl.program_id(0) * nj + pl.program_id(1)
        dh_ref, wd_ref, g_ref, u_ref, dg_ref, du_ref = _carry(rider, refs, 4, 2, 0, step, ni * nj)
        da = _dot(dh_ref[...], wd_ref[...], NT)
        g = g_ref[...].astype(F32)
        sg = jax.nn.sigmoid(g)
        dg_ref[...] = (da * u_ref[...].astype(F32) * (sg * (1.0 + g * (1.0 - sg)))).astype(dg_ref.dtype)
        du_ref[...] = (da * (g * sg)).astype(du_ref.dtype)

    tile = pl.BlockSpec((tm, tn), lambda i, j: (i, j))
    in_specs, out_specs, out_shape, scratch, extra = _with_rider(
        rider, [pl.BlockSpec((tm, D), lambda i, j: (i, 0)), pl.BlockSpec((tn, D), lambda i, j: (j, 0)), tile, tile],
        [tile, tile], [jax.ShapeDtypeStruct((T, FP), BF16), jax.ShapeDtypeStruct((T, FP), BF16)], [])
    return pl.pallas_call(
        body, name=name, grid=(ni, nj), in_specs=in_specs, out_specs=out_specs, out_shape=out_shape,
        scratch_shapes=scratch,
        compiler_params=_params(("arbitrary", "arbitrary") if rider else ("parallel", "parallel"),
                                has_side_effects=rider is not None),
    )(dh, wd, g, u, *extra)


def _row_specs(rows, tr, cw):
    return [pl.BlockSpec((tr, cw), lambda j, i, o=off: (i, o + j)) for _, off in rows]


def _const_specs(consts, cw):
    specs = []
    for arr, off in consts:
        if off is None:
            specs.append(pl.BlockSpec(arr.shape, lambda j, i: (0, 0)))
        else:
            specs.append(pl.BlockSpec((arr.shape[0], cw), lambda j, i, o=off: (0, o + j)))
    return specs


def _rowwise(fn, rows, consts, out_dtypes, name, tr, cw, ncol):
    T = rows[0][0].shape[0]
    nr, nc = len(rows), len(consts)

    def body(*refs):
        r = [x[...].astype(F32) for x in refs[:nr]]
        c = [x[...] for x in refs[nr:nr + nc]]
        res = fn(*r, *c)
        for o_ref, v in zip(refs[nr + nc:], res):
            o_ref[...] = v.astype(o_ref.dtype)

    o_spec = pl.BlockSpec((tr, cw), lambda j, i: (i, j))
    return pl.pallas_call(
        body, name=name, grid=(ncol, T // tr),
        in_specs=_row_specs(rows, tr, cw) + _const_specs(consts, cw),
        out_specs=[o_spec] * len(out_dtypes),
        out_shape=[jax.ShapeDtypeStruct((T, ncol * cw), dt) for dt in out_dtypes],
        compiler_params=_params(("parallel", "parallel")),
    )(*[a for a, _ in rows], *[a for a, _ in consts])


def _rowwise_bwd(fn, rows, consts, cots, diff, ddtypes, name, tr, cw, ncol):
    T = rows[0][0].shape[0]
    nr, nc, nt, nd = len(rows), len(consts), len(cots), len(diff)

    def body(*refs):
        r = [x[...].astype(F32) for x in refs[:nr]]
        c = [x[...] for x in refs[nr:nr + nc]]
        ct = [x[...].astype(F32) for x in refs[nr + nc:nr + nc + nt]]
        drow_refs = refs[nr + nc + nt:nr + nc + nt + nd]
        dconst_refs = refs[nr + nc + nt + nd:]
        i = pl.program_id(1)

        def f(*args):
            full = list(r)
            for idx, a in zip(diff, args[:nd]):
                full[idx] = a
            return tuple(fn(*full, *args[nd:]))

        _, vjp = jax.vjp(f, *[r[d] for d in diff], *c)
        g = vjp(tuple(ct))
        for o_ref, v in zip(drow_refs, g[:nd]):
            o_ref[...] = v.astype(o_ref.dtype)

        @pl.when(i == 0)
        def _():
            for o_ref in dconst_refs:
                o_ref[...] = jnp.zeros_like(o_ref)

        for o_ref, v in zip(dconst_refs, g[nd:]):
            o_ref[...] += v

    o_spec = pl.BlockSpec((tr, cw), lambda j, i: (i, j))
    out_shape = [jax.ShapeDtypeStruct((T, ncol * cw), dt) for dt in ddtypes]
    out_shape += [jax.ShapeDtypeStruct(a.shape, F32) for a, _ in consts]
    return pl.pallas_call(
        body, name=name, grid=(ncol, T // tr),
        in_specs=_row_specs(rows, tr, cw) + _const_specs(consts, cw) + _row_specs(cots, tr, cw),
        out_specs=[o_spec] * nd + _const_specs(consts, cw),
        out_shape=out_shape,
        compiler_params=_params(("parallel", "arbitrary")),
    )(*[a for a, _ in rows], *[a for a, _ in consts], *[a for a, _ in cots])


def _rms(x, g):
    return x * lax.rsqrt(jnp.mean(x * x, axis=-1, keepdims=True) + EPS) * g


def _silu(x):
    return x * jax.nn.sigmoid(x)


def _norm_fn(x, g):
    return (_rms(x, g),)


def _norm_res_fn(x, g):
    return (x, _rms(x, g))


def _post_pre_fn(scale, x, h, g_post, g_pre):
    xn = x + scale * _rms(h, g_post)
    return (xn, _rms(xn, g_pre))


def _resid_h_fn(scale, h, g):
    return (scale * _rms(h, g),)


def _hpost_fn(o, hog, gn):
    return (_rms(o, gn) * _silu(hog),)


def _merge_fn(z0, z1, z2, yh, yf, ym):
    return (jax.nn.sigmoid(z0) * yh + jax.nn.sigmoid(z1) * yf + jax.nn.sigmoid(z2) * ym,)


def _loss(x2, h, g_post, tgt, name):
    T = x2.shape[0]
    tr = _tile(T, (256, 128))

    def body(x_ref, h_ref, g_ref, t_ref, dy_ref, s_ref):
        i = pl.program_id(0)
        e = x_ref[...] + 0.5 * _rms(h_ref[...], g_ref[...]) - t_ref[...]
        dy_ref[...] = e * (1.0 / D)
        col = jnp.sum(e * e, axis=0, keepdims=True)
        tot = col[:, 0:HD]
        for k in range(1, D // HD):
            tot = tot + col[:, k * HD:(k + 1) * HD]

        @pl.when(i == 0)
        def _():
            s_ref[...] = jnp.zeros_like(s_ref)

        s_ref[...] += tot

    spec = pl.BlockSpec((tr, D), lambda i: (i, 0))
    return pl.pallas_call(
        body, name=name, grid=(T // tr,), in_specs=[spec, spec, pl.BlockSpec((1, D), lambda i: (0, 0)), spec],
        out_specs=[spec, pl.BlockSpec((1, HD), lambda i: (0, 0))],
        out_shape=[jax.ShapeDtypeStruct((T, D), F32), jax.ShapeDtypeStruct((1, HD), F32)],
        compiler_params=_params(("arbitrary",)),
    )(x2, h, g_post, tgt)


def _lower_bound(lb_ref):
    a0 = lb_ref[0:1, :]
    a1 = lb_ref[1:2, :]
    mx = jnp.maximum(a0, a1)
    e0 = jnp.exp(a0 - mx)
    return e0 / (e0 + jnp.exp(a1 - mx))


def _hgrn_prep(hq, hf, lb):
    g = lb + (1.0 - lb) * jax.nn.sigmoid(hf)
    return _silu(hq), 1.0 - g, jnp.log(g)


def _tri(n, upper):
    r = lax.broadcasted_iota(jnp.int32, (n, n), 0)
    c = lax.broadcasted_iota(jnp.int32, (n, n), 1)
    return (c >= r) if upper else (c <= r)


def _hgrn_factors(q, k, gl):
    low = _tri(CHUNK, False)
    b = lax.dot_general(low.astype(F32), gl, NN, precision=HIGHEST, preferred_element_type=F32)
    bl = b[CHUNK - 1:CHUNK, :]
    ref = b[CHUNK // 2 - 1:CHUNK // 2, :]
    eb = jnp.exp(b)
    ea = jnp.exp(b - ref)
    ebn = jnp.exp(ref - b)
    ek = jnp.exp(bl - b)
    ebl = jnp.exp(bl)
    return low, eb, ea, ebn, ek, ebl


def _hgrn_fwd(proj, hgrn_lb, rider=None):
    T = proj.shape[0]
    cb = _tile(T, (512, 256, 128, 64))
    nchunk = cb // CHUNK

    def body(*refs):
        hq_ref, hf_ref, hi_ref, lb_ref, o_ref, st_ref, state = _carry(rider, refs, 4, 2, 1, pl.program_id(0), T // cb)

        @pl.when(pl.program_id(0) == 0)
        def _():
            state[...] = jnp.zeros_like(state)

        lb = _lower_bound(lb_ref)

        def chunk(c, carry):
            r0 = pl.multiple_of(c * CHUNK, CHUNK)
            for h in range(NH):
                cols = slice(h * HD, (h + 1) * HD)
                q, k, gl = _hgrn_prep(hq_ref[pl.ds(r0, CHUNK), cols], hf_ref[pl.ds(r0, CHUNK), cols], lb[:, cols])
                v = hi_ref[pl.ds(r0, CHUNK), cols]
                low, eb, ea, ebn, ek, ebl = _hgrn_factors(q, k, gl)
                s_t = state[h]
                st_ref[c, h] = s_t
                pm = jnp.where(low, _dot(q * ea, k * ebn, NT), 0.0)
                o_ref[pl.ds(r0, CHUNK), cols] = _dot(q * eb, s_t, NT) + _dot(pm, v, NN)
                state[h] = s_t * ebl + _dot(v, k * ek, TN)
            return carry

        lax.fori_loop(0, nchunk, chunk, 0)

    def col(off):
        return pl.BlockSpec((cb, WH), lambda i, o=off: (i, o))

    in_specs, out_specs, out_shape, scratch, extra = _with_rider(
        rider, [col(0), col(1), col(2), pl.BlockSpec((2, WH), lambda i: (0, 0))],
        [pl.BlockSpec((cb, WH), lambda i: (i, 0)), pl.BlockSpec((nchunk, NH, HD, HD), lambda i: (i, 0, 0, 0))],
        [jax.ShapeDtypeStruct((T, WH), F32), jax.ShapeDtypeStruct((T // CHUNK, NH, HD, HD), F32)],
        [pltpu.VMEM((NH, HD, HD), F32)])
    return pl.pallas_call(
        body, name="hgrn_fwd", grid=(T // cb,), in_specs=in_specs, out_specs=out_specs, out_shape=out_shape,
        scratch_shapes=scratch, compiler_params=_params(("arbitrary",), has_side_effects=rider is not None),
    )(proj, proj, proj, hgrn_lb, *extra)


def _hgrn_bwd(proj, hgrn_lb, states, do, rider=None):
    T = proj.shape[0]
    cb = _tile(T, (512, 256, 128, 64))
    nchunk = cb // CHUNK
    nb = T // cb

    def body(*refs):
        (hq_ref, hf_ref, hi_ref, lb_ref, st_ref, do_ref, dhq_ref, dhf_ref, dhi_ref, dlb_ref,
         dstate) = _carry(rider, refs, 6, 4, 1, pl.program_id(0), nb)

        @pl.when(pl.program_id(0) == 0)
        def _():
            dstate[...] = jnp.zeros_like(dstate)
            dlb_ref[...] = jnp.zeros_like(dlb_ref)

        lb = _lower_bound(lb_ref)
        up = _tri(CHUNK, True)
        last = lax.broadcasted_iota(jnp.int32, (CHUNK, HD), 0) == CHUNK - 1

        def chunk(cc, carry):
            c = nchunk - 1 - cc
            r0 = pl.multiple_of(c * CHUNK, CHUNK)
            for h in range(NH):
                cols = slice(h * HD, (h + 1) * HD)
                hq = hq_ref[pl.ds(r0, CHUNK), cols]
                hf = hf_ref[pl.ds(r0, CHUNK), cols]
                (q, k, gl), prep_vjp = jax.vjp(_hgrn_prep, hq, hf, lb[:, cols])
                v = hi_ref[pl.ds(r0, CHUNK), cols]
                d_o = do_ref[pl.ds(r0, CHUNK), cols]
                low, eb, ea, ebn, ek, ebl = _hgrn_factors(q, k, gl)
                s_t = st_ref[c, h]
                ds_new = dstate[h]
                qe, am, bm, kb = q * eb, q * ea, k * ebn, k * ek
                pm_t = jnp.where(up, _dot(bm, am, NT), 0.0)
                dp = jnp.where(low, _dot(d_o, v, NT), 0.0)
                dp_t = jnp.where(up, _dot(v, d_o, NT), 0.0)
                dqe = _dot(d_o, s_t, NN)
                da = _dot(dp, bm, NN)
                db_m = _dot(dp_t, am, NN)
                dkb = _dot(v, ds_new, NN)
                dv = _dot(pm_t, d_o, NN) + _dot(kb, ds_new, NT)
                dq = dqe * eb + da * ea
                dk = db_m * ebn + dkb * ek
                dbl = jnp.sum(dkb * kb, axis=0, keepdims=True) + jnp.sum(ds_new * s_t, axis=0, keepdims=True) * ebl
                db = (dqe * qe + da * am.astype(BF16).astype(F32) - db_m * bm.astype(BF16).astype(F32) - dkb * kb
                      + jnp.where(last, dbl, 0.0))
                dgl = lax.dot_general(up.astype(F32), db, NN, precision=HIGHEST, preferred_element_type=F32)
                dhq, dhf, dlb = prep_vjp((dq, dk, dgl))
                dhq_ref[pl.ds(r0, CHUNK), cols] = dhq.astype(dhq_ref.dtype)
                dhf_ref[pl.ds(r0, CHUNK), cols] = dhf.astype(dhf_ref.dtype)
                dhi_ref[pl.ds(r0, CHUNK), cols] = dv.astype(dhi_ref.dtype)
                dlb_ref[:, cols] += dlb
                dstate[h] = _dot(d_o, qe, TN) + ds_new * ebl
            return carry

        lax.fori_loop(0, nchunk, chunk, 0)

    def col(off):
        return pl.BlockSpec((cb, WH), lambda i, o=off: (nb - 1 - i, o))

    row = pl.BlockSpec((cb, WH), lambda i: (nb - 1 - i, 0))
    in_specs, out_specs, out_shape, scratch, extra = _with_rider(
        rider, [col(0), col(1), col(2), pl.BlockSpec((2, WH), lambda i: (0, 0)),
                pl.BlockSpec((nchunk, NH, HD, HD), lambda i: (nb - 1 - i, 0, 0, 0)), row],
        [row, row, row, pl.BlockSpec((1, WH), lambda i: (0, 0))],
        [jax.ShapeDtypeStruct((T, WH), BF16)] * 3 + [jax.ShapeDtypeStruct((1, WH), F32)], [pltpu.VMEM((NH, HD, HD), F32)])
    return pl.pallas_call(
        body, name="hgrn_bwd", grid=(nb,), in_specs=in_specs, out_specs=out_specs, out_shape=out_shape,
        scratch_shapes=scratch, compiler_params=_params(("arbitrary",), has_side_effects=rider is not None),
    )(proj, proj, proj, hgrn_lb, states, do, *extra)


def _log_sigmoid(z):
    return jnp.minimum(z, 0.0) - jnp.log(1.0 + jnp.exp(-jnp.abs(z)))


def _fox_cum(proj, fb_pad):
    T = proj.shape[0]
    tb = _tile(T, (256, 128))

    def body(ff_ref, fb_ref, ct_ref, cq_ref, carry):
        @pl.when(pl.program_id(0) == 0)
        def _():
            carry[...] = jnp.zeros_like(carry)

        lf = _log_sigmoid(ff_ref[...] + fb_ref[...])
        cs = lax.dot_general(_tri(tb, False).astype(F32), lf, NN, precision=HIGHEST,
                             preferred_element_type=F32) + carry[0:1, :]
        carry[0:1, :] = cs[tb - 1:tb, :]
        ct_ref[...] = cs.T[0:8, :]
        for h in range(NH):
            cq_ref[h] = jnp.broadcast_to(cs[:, h:h + 1], (tb, HD))

    return pl.pallas_call(
        body, name="fox_cum", grid=(T // tb,),
        in_specs=[pl.BlockSpec((tb, HD), lambda i: (i, CB_FF)), pl.BlockSpec((1, HD), lambda i: (0, 0))],
        out_specs=[pl.BlockSpec((8, tb), lambda i: (0, i)), pl.BlockSpec((NH, tb, HD), lambda i: (0, i, 0))],
        out_shape=[jax.ShapeDtypeStruct((8, T), F32), jax.ShapeDtypeStruct((NH, T, HD), F32)],
        scratch_shapes=[pltpu.VMEM((8, HD), F32)],
        compiler_params=_params(("arbitrary",)),
    )(proj, fb_pad)


def _fox_cum_bwd(dc, proj, fb_pad):
    T = proj.shape[0]
    tb = _tile(T, (256, 128))
    nb = T // tb

    def body(dc_ref, ff_ref, fb_ref, dff_ref, dfb_ref, carry):
        @pl.when(pl.program_id(0) == 0)
        def _():
            carry[...] = jnp.zeros_like(carry)
            dfb_ref[...] = jnp.zeros_like(dfb_ref)

        rid = lax.broadcasted_iota(jnp.int32, (8, tb), 0)
        m8 = jnp.zeros((8, tb), F32)
        for h in range(NH):
            m8 = m8 + jnp.where(rid == h, dc_ref[h], 0.0)
        dcb = jnp.concatenate([m8, jnp.zeros((HD - 8, tb), F32)], axis=0).T
        rev = lax.dot_general(_tri(tb, True).astype(F32), dcb, NN, precision=HIGHEST,
                              preferred_element_type=F32) + carry[0:1, :]
        carry[0:1, :] = rev[0:1, :]
        dff = rev * jax.nn.sigmoid(-(ff_ref[...] + fb_ref[...]))
        dff_ref[...] = dff.astype(dff_ref.dtype)
        dfb_ref[...] += jnp.sum(dff, axis=0, keepdims=True)

    return pl.pallas_call(
        body, name="fox_cum_bwd", grid=(nb,),
        in_specs=[pl.BlockSpec((NH, 8, tb), lambda i: (0, 0, nb - 1 - i)),
                  pl.BlockSpec((tb, HD), lambda i: (nb - 1 - i, CB_FF)), pl.BlockSpec((1, HD), lambda i: (0, 0))],
        out_specs=[pl.BlockSpec((tb, HD), lambda i: (nb - 1 - i, 0)), pl.BlockSpec((1, HD), lambda i: (0, 0))],
        out_shape=[jax.ShapeDtypeStruct((T, HD), BF16), jax.ShapeDtypeStruct((1, HD), F32)],
        scratch_shapes=[pltpu.VMEM((8, HD), F32)],
        compiler_params=_params(("arbitrary",)),
    )(dc, proj, fb_pad)


STRIP = 128


def _fox_scores(q, k, cq, ck, i, j, bq, bk, r0=0):
    rows = q.shape[0]
    s = _dot(q, k, NT) * SCALE + (cq - ck)
    diff = lax.broadcasted_iota(jnp.int32, (rows, bk), 1) - lax.broadcasted_iota(jnp.int32, (rows, bk), 0)
    return jnp.where(diff <= i * bq + r0 - j * bk, s, NEG)


def _heads(h):
    return slice(h * HD, (h + 1) * HD)


UNDERFLOW = -120.0


def _fox_windows(proj, cq):
    T = proj.shape[0]
    bq = _tile(T, (512, 256, 128))
    nq = T // bq
    assert nq <= HD

    def body(q_ref, k_ref, cq_ref, jlo_ref, ihi_ref, norm_s, cs_s, ce_s):
        i = pl.program_id(0)

        @pl.when(i == 0)
        def _():
            norm_s[...] = jnp.zeros_like(norm_s)
            cs_s[...] = jnp.zeros_like(cs_s)
            ce_s[...] = jnp.zeros_like(ce_s)

        lane = lax.broadcasted_iota(jnp.int32, (1, HD), 1)
        for h in range(NH):
            for row, ref in ((h, q_ref), (8 + h, k_ref)):
                x = ref[:, _heads(h)]
                biggest = jnp.max(jnp.sum(x * x, axis=1, keepdims=True), axis=0, keepdims=True)
                norm_s[row:row + 1, :] = jnp.maximum(norm_s[row:row + 1, :], jnp.broadcast_to(biggest, (1, HD)))
            cs_s[h, pl.ds(i, 1), :] = cq_ref[h, 0:1, :]
            ce_s[h:h + 1, :] = jnp.where(lane == i, cq_ref[h, bq - 1:bq, :], ce_s[h:h + 1, :])

        @pl.when(i == nq - 1)
        def _():
            rows = lax.broadcasted_iota(jnp.int32, (HD, HD), 0)
            cols = lax.broadcasted_iota(jnp.int32, (HD, HD), 1)
            need = cols == rows
            for h in range(NH):
                slack = 2.05 * SCALE * jnp.sqrt(norm_s[h:h + 1, :] * norm_s[8 + h:9 + h, :])
                bound = cs_s[h] - ce_s[h:h + 1, :] + slack
                need = need | ((bound >= UNDERFLOW) & (cols < rows))
            need = need & (rows < nq) & (cols < nq)
            jlo = jnp.min(jnp.where(need, cols, HD).astype(F32), axis=1, keepdims=True)
            ihi = jnp.max(jnp.where(need, rows, -1).astype(F32), axis=0, keepdims=True)
            jlo_ref[...] = jnp.broadcast_to(jlo, (HD, HD)).astype(jnp.int32)
            ihi_ref[...] = jnp.broadcast_to(ihi, (8, HD)).astype(jnp.int32)

    jlo, ihi = pl.pallas_call(
        body, name="fox_windows", grid=(nq,),
        in_specs=[pl.BlockSpec((bq, WH), lambda i: (i, CB_FQ // NH)), pl.BlockSpec((bq, WH), lambda i: (i, CB_FK // NH)),
                  pl.BlockSpec((NH, bq, HD), lambda i: (0, i, 0))],
        out_specs=[pl.BlockSpec((HD, HD), lambda i: (0, 0)), pl.BlockSpec((8, HD), lambda i: (0, 0))],
        out_shape=[jax.ShapeDtypeStruct((HD, HD), jnp.int32), jax.ShapeDtypeStruct((8, HD), jnp.int32)],
        scratch_shapes=[pltpu.VMEM((16, HD), F32), pltpu.VMEM((NH, HD, HD), F32), pltpu.VMEM((8, HD), F32)],
        compiler_params=_params(("arbitrary",)),
    )(proj, proj, cq)
    return jnp.concatenate([jlo[:nq, 0], ihi[0, :nq]])


def _fox_fwd(win, proj, ct, cq):
    T = proj.shape[0]
    bq = bk = _tile(T, (512, 256, 128))
    nq = nk = T // bq

    def body(win_ref, q_ref, k_ref, v_ref, ct_ref, cq_ref, o_ref, lse_ref, m_s, l_s, acc_s):
        i, jj = pl.program_id(0), pl.program_id(1)
        j = win_ref[i] + jj

        @pl.when(jj == 0)
        def _():
            m_s[...] = jnp.full_like(m_s, NEG)
            l_s[...] = jnp.zeros_like(l_s)
            acc_s[...] = jnp.zeros_like(acc_s)

        @pl.when(j <= i)
        def _():
            for h in range(NH):
                hs = _heads(h)
                k, v, ck = k_ref[:, hs], v_ref[:, hs], ct_ref[h:h + 1, :]
                for r0 in range(0, bq, STRIP):
                    rs = slice(r0, r0 + STRIP)
                    s = _fox_scores(q_ref[rs, hs], k, cq_ref[h, rs, 0:1], ck, i, j, bq, bk, r0)
                    m_prev = m_s[h, rs]
                    m_new = jnp.maximum(m_prev, jnp.max(s, axis=1, keepdims=True))
                    alpha = jnp.exp(m_prev - m_new)
                    p = jnp.exp(s - m_new)
                    l_s[h, rs] = alpha * l_s[h, rs] + jnp.sum(p, axis=1, keepdims=True)
                    acc_s[rs, hs] = alpha * acc_s[rs, hs] + _dot(p, v, NN)
                    m_s[h, rs] = m_new

        @pl.when(jj == nk - 1)
        def _():
            for h in range(NH):
                o_ref[:, _heads(h)] = acc_s[:, _heads(h)] / l_s[h]
                lse_ref[h] = jnp.broadcast_to(m_s[h] + jnp.log(l_s[h]), (bq, HD))

    def key_block(i, jj, win):
        return jnp.minimum(win[i] + jj, i)

    def kv(off):
        return pl.BlockSpec((bk, WH), lambda i, jj, win, o=off // NH: (key_block(i, jj, win), o))

    stat = pl.BlockSpec((NH, bq, HD), lambda i, jj, win: (0, i, 0))
    return pl.pallas_call(
        body, name="fox_fwd",
        grid_spec=pltpu.PrefetchScalarGridSpec(
            num_scalar_prefetch=1, grid=(nq, nk),
            in_specs=[pl.BlockSpec((bq, WH), lambda i, jj, win: (i, CB_FQ // NH)), kv(CB_FK), kv(CB_FV),
                      pl.BlockSpec((8, bk), lambda i, jj, win: (0, key_block(i, jj, win))), stat],
            out_specs=[pl.BlockSpec((bq, WH), lambda i, jj, win: (i, 0)), stat],
            scratch_shapes=[pltpu.VMEM((NH, bq, 1), F32), pltpu.VMEM((NH, bq, 1), F32), pltpu.VMEM((bq, WH), F32)]),
        out_shape=[jax.ShapeDtypeStruct((T, WH), F32), jax.ShapeDtypeStruct((NH, T, HD), F32)],
        compiler_params=_params(("parallel", "arbitrary")),
    )(win, proj, proj, proj, ct, cq)


def _fox_bwd_dq(win, proj, ct, cq, lse, do):
    T = proj.shape[0]
    bq = bk = _tile(T, (512, 256, 128))
    nq = nk = T // bq

    def body(win_ref, q_ref, k_ref, v_ref, ct_ref, cq_ref, lse_ref, do_ref, dq_ref, delta_ref, acc_s, delta_s, psum_s):
        i, jj = pl.program_id(0), pl.program_id(1)
        j = win_ref[i] + jj % nk

        @pl.when(jj == 0)
        def _():
            acc_s[...] = jnp.zeros_like(acc_s)
            delta_s[...] = jnp.zeros_like(delta_s)
            psum_s[...] = jnp.zeros_like(psum_s)

        def probs(h):
            hs = _heads(h)
            k = k_ref[:, hs]
            s = _fox_scores(q_ref[:, hs], k, cq_ref[h, :, 0:1], ct_ref[h:h + 1, :], i, j, bq, bk)
            return k, jnp.exp(s - lse_ref[h, :, 0:1]), _dot(do_ref[:, hs], v_ref[:, hs], NT)

        @pl.when((j <= i) & (jj < nk))
        def _():
            for h in range(NH):
                _, p, dp = probs(h)
                delta_s[h] += jnp.sum(p * dp, axis=1, keepdims=True)
                psum_s[h] += jnp.sum(p, axis=1, keepdims=True)

        @pl.when((j <= i) & (jj >= nk))
        def _():
            for h in range(NH):
                k, p, dp = probs(h)
                ds = p * (dp - delta_s[h] / psum_s[h])
                acc_s[:, _heads(h)] += _dot(ds, k, NN) * SCALE

        @pl.when(jj == 2 * nk - 1)
        def _():
            dq_ref[...] = acc_s[...].astype(dq_ref.dtype)
            for h in range(NH):
                delta_ref[h] = jnp.broadcast_to(delta_s[h] / psum_s[h], (bq, HD))

    def key_block(i, jj, win):
        return jnp.minimum(win[i] + jj % nk, i)

    def kv(off):
        return pl.BlockSpec((bk, WH), lambda i, jj, win, o=off // NH: (key_block(i, jj, win), o))

    qrow = pl.BlockSpec((bq, WH), lambda i, jj, win: (i, 0))
    stat = pl.BlockSpec((NH, bq, HD), lambda i, jj, win: (0, i, 0))
    return pl.pallas_call(
        body, name="fox_bwd_dq",
        grid_spec=pltpu.PrefetchScalarGridSpec(
            num_scalar_prefetch=1, grid=(nq, 2 * nk),
            in_specs=[pl.BlockSpec((bq, WH), lambda i, jj, win: (i, CB_FQ // NH)), kv(CB_FK), kv(CB_FV),
                      pl.BlockSpec((8, bk), lambda i, jj, win: (0, key_block(i, jj, win))), stat, stat, qrow],
            out_specs=[qrow, stat],
            scratch_shapes=[pltpu.VMEM((bq, WH), F32), pltpu.VMEM((NH, bq, 1), F32), pltpu.VMEM((NH, bq, 1), F32)]),
        out_shape=[jax.ShapeDtypeStruct((T, WH), BF16), jax.ShapeDtypeStruct((NH, T, HD), F32)],
        compiler_params=_params(("parallel", "arbitrary")),
    )(win, proj, proj, proj, ct, cq, lse, do)


def _fox_bwd_dkv(win, proj, ct, cq, lse, delta, do):
    T = proj.shape[0]
    bq = bk = _tile(T, (512, 256, 128))
    nq = nk = T // bq

    def body(win_ref, q_ref, k_ref, v_ref, ct_ref, cq_ref, lse_ref, delta_ref, do_ref, dk_ref, dv_ref, dc_ref,
             dk_s, dv_s, dc_s):
        j, ii = pl.program_id(0), pl.program_id(1)
        i = j + ii

        @pl.when(ii == 0)
        def _():
            dk_s[...] = jnp.zeros_like(dk_s)
            dv_s[...] = jnp.zeros_like(dv_s)
            dc_s[...] = jnp.zeros_like(dc_s)

        @pl.when(i <= win_ref[nq + j])
        def _():
            for h in range(NH):
                hs = _heads(h)
                q = q_ref[:, hs]
                d_o = do_ref[:, hs]
                s = _fox_scores(q, k_ref[:, hs], cq_ref[h, :, 0:1], ct_ref[h:h + 1, :], i, j, bq, bk)
                p = jnp.exp(s - lse_ref[h, :, 0:1])
                dv_s[:, hs] += _dot(p, d_o, TN)
                dp = _dot(d_o, v_ref[:, hs], NT)
                ds = p * (dp - delta_ref[h, :, 0:1])
                dk_s[:, hs] += _dot(ds, q, TN) * SCALE
                dc_s[h:h + 1, :] -= jnp.sum(ds, axis=0, keepdims=True)

        @pl.when(ii == nq - 1)
        def _():
            dk_ref[...] = dk_s[...].astype(dk_ref.dtype)
            dv_ref[...] = dv_s[...].astype(dv_ref.dtype)
            for h in range(NH):
                dc_ref[h] = jnp.broadcast_to(dc_s[h:h + 1, :], (8, bk))

    def query_block(j, ii, win):
        return jnp.minimum(j + ii, win[nq + j])

    def kv(off):
        return pl.BlockSpec((bk, WH), lambda j, ii, win, o=off // NH: (j, o))

    qrow = pl.BlockSpec((bq, WH), lambda j, ii, win: (query_block(j, ii, win), 0))
    stat = pl.BlockSpec((NH, bq, HD), lambda j, ii, win: (0, query_block(j, ii, win), 0))
    krow = pl.BlockSpec((bk, WH), lambda j, ii, win: (j, 0))
    return pl.pallas_call(
        body, name="fox_bwd_dkv",
        grid_spec=pltpu.PrefetchScalarGridSpec(
            num_scalar_prefetch=1, grid=(nk, nq),
            in_specs=[pl.BlockSpec((bq, WH), lambda j, ii, win: (query_block(j, ii, win), CB_FQ // NH)), kv(CB_FK),
                      kv(CB_FV), pl.BlockSpec((8, bk), lambda j, ii, win: (0, j)), stat, stat, stat, qrow],
            out_specs=[krow, krow, pl.BlockSpec((NH, 8, bk), lambda j, ii, win: (0, 0, j))],
            scratch_shapes=[pltpu.VMEM((bk, WH), F32), pltpu.VMEM((bk, WH), F32), pltpu.VMEM((8, bk), F32)]),
        out_shape=[jax.ShapeDtypeStruct((T, WH), BF16), jax.ShapeDtypeStruct((T, WH), BF16),
                   jax.ShapeDtypeStruct((NH, 8, T), F32)],
        compiler_params=_params(("parallel", "arbitrary")),
    )(win, proj, proj, proj, ct, cq, lse, delta, do)


def _mem_probs(q, mk):
    s = _dot(q, mk, NT) * SCALE
    e = jnp.exp(s - jnp.max(s, axis=1, keepdims=True))
    return e / jnp.sum(e, axis=1, keepdims=True)


def _mem_fwd(proj, mem_kv):
    T = proj.shape[0]
    tr = _tile(T, (512, 256, 128))
    M = mem_kv.shape[0]

    def body(q_ref, mk_ref, mv_ref, o_ref):
        o_ref[...] = _dot(_mem_probs(q_ref[...], mk_ref[...]), mv_ref[...], NN)

    return pl.pallas_call(
        body, name="mem_fwd", grid=(NM, T // tr),
        in_specs=[pl.BlockSpec((tr, HD), lambda h, i: (i, CB_MQ + h)),
                  pl.BlockSpec((M, HD), lambda h, i: (0, h)), pl.BlockSpec((M, HD), lambda h, i: (0, NM + h))],
        out_specs=pl.BlockSpec((tr, HD), lambda h, i: (i, h)),
        out_shape=jax.ShapeDtypeStruct((T, WM), F32),
        compiler_params=_params(("parallel", "parallel")),
    )(proj, mem_kv, mem_kv)


def _mem_bwd(proj, mem_kv, do):
    T = proj.shape[0]
    tr = _tile(T, (512, 256, 128))
    M = mem_kv.shape[0]

    def body(q_ref, mk_ref, mv_ref, do_ref, dq_ref, dmk_ref, dmv_ref):
        @pl.when(pl.program_id(1) == 0)
        def _():
            dmk_ref[...] = jnp.zeros_like(dmk_ref)
            dmv_ref[...] = jnp.zeros_like(dmv_ref)

        q, mk, d_o = q_ref[...], mk_ref[...], do_ref[...]
        p = _mem_probs(q, mk)
        dmv_ref[...] += _dot(p, d_o, TN)
        dp = _dot(d_o, mv_ref[...], NT)
        ds = p * (dp - jnp.sum(p * dp, axis=1, keepdims=True))
        dq_ref[...] = (_dot(ds, mk, NN) * SCALE).astype(dq_ref.dtype)
        dmk_ref[...] += _dot(ds, q, TN) * SCALE

    acc = pl.BlockSpec((M, HD), lambda h, i: (0, h))
    row = pl.BlockSpec((tr, HD), lambda h, i: (i, h))
    return pl.pallas_call(
        body, name="mem_bwd", grid=(NM, T // tr),
        in_specs=[pl.BlockSpec((tr, HD), lambda h, i: (i, CB_MQ + h)),
                  pl.BlockSpec((M, HD), lambda h, i: (0, h)), pl.BlockSpec((M, HD), lambda h, i: (0, NM + h)), row],
        out_specs=[row, acc, acc],
        out_shape=[jax.ShapeDtypeStruct((T, WM), BF16), jax.ShapeDtypeStruct((M, WM), F32),
                   jax.ShapeDtypeStruct((M, WM), F32)],
        compiler_params=_params(("parallel", "arbitrary")),
    )(proj, mem_kv, mem_kv, do)


def _mesh_place():
    x, y, c = lax.axis_index("x"), lax.axis_index("y"), lax.axis_index("c")
    return x, y, c


CHIP_FLIPS = (4, 2, 6)
CHIP_OF_SLOT = (0,) + CHIP_FLIPS


def _peer(x, y, c, k):
    px = 1 - x if k & 4 else x
    py = 1 - y if k & 2 else y
    pc = 1 - c if k & 1 else c
    return (px, py, pc), 4 * px + 2 * py + pc


class _Gather:
    def __init__(self, shapes, pad_rows):
        self.shapes, self.pad_rows, self.n = shapes, pad_rows, len(shapes)
        self.npad = sum(1 for p in pad_rows if p)

    def zeros(self):
        return jnp.zeros((max(self.pad_rows) or 16, self.shapes[0][1]), BF16)

    def out_shape(self):
        return [jax.ShapeDtypeStruct((NDEV * r + p, c), BF16) for (r, c), p in zip(self.shapes, self.pad_rows)]

    def sems(self):
        return [pltpu.SemaphoreType.DMA((self.n, NDEV - 1)), pltpu.SemaphoreType.DMA((self.n, NDEV - 1)),
                pltpu.SemaphoreType.DMA((self.n + self.npad,))]

    def _copies(self, ins, z_ref, outs, send_sems, recv_sems, loc_sems):
        x, y, c = _mesh_place()
        me = 4 * x + 2 * y + c
        sibling, _ = _peer(x, y, c, 1)
        local, first, arrive, forward = [], [], [], []
        ip = 0
        for w in range(self.n):
            r = ins[w].shape[0]
            dst = outs[w].at[pl.ds(pl.multiple_of(me * r, 16), r), :]
            local.append(functools.partial(pltpu.make_async_copy, ins[w], dst, loc_sems.at[w]))
            if self.pad_rows[w]:
                local.append(functools.partial(pltpu.make_async_copy, z_ref.at[pl.ds(0, self.pad_rows[w]), :],
                                               outs[w].at[pl.ds(NDEV * r, self.pad_rows[w]), :], loc_sems.at[self.n + ip]))
                ip += 1

            def remote(src, dst_, s, to):
                return functools.partial(pltpu.make_async_remote_copy, src_ref=src, dst_ref=dst_, send_sem=send_sems.at[w, s],
                                         recv_sem=recv_sems.at[w, s], device_id=to, device_id_type=MESH)

            for s, k in enumerate((1,) + CHIP_FLIPS):
                first.append(remote(ins[w], dst, s, _peer(x, y, c, k)[0]))
            for s, k in enumerate(CHIP_FLIPS):
                _, pidx = _peer(x, y, c, k)
                rows = outs[w].at[pl.ds(pl.multiple_of(pidx * r, 16), r), :]
                arrive.append(remote(rows, rows, 1 + s, sibling))
                forward.append(remote(rows, rows, 4 + s, sibling))
        return local, first, arrive, forward


    def start(self, *refs):
        local, first, _, _ = self._copies(*refs)
        for make in local + first:
            make().start()

    def forward(self, *refs):
        _, _, arrive, forward = self._copies(*refs)
        for a, f in zip(arrive, forward):
            a().wait_recv()
            f().start()

    def finish(self, *refs):
        local, first, _, forward = self._copies(*refs)
        for make in local + first[0::4] + forward:
            make().wait()
        for s in (1, 2, 3):
            for make in first[s::4]:
                make().wait_send()


def _all_gather(shards, pad_rows):
    n = len(shards)
    plan = _Gather([s.shape for s in shards], pad_rows)

    def body(*refs):
        args = (refs[:n], refs[n], refs[n + 1:2 * n + 1]) + tuple(refs[2 * n + 1:])
        plan.start(*args)
        plan.forward(*args)
        plan.finish(*args)

    any_spec = pl.BlockSpec(memory_space=pl.ANY)
    return pl.pallas_call(
        body, name="all_gather_weights",
        in_specs=[any_spec] * (n + 1), out_specs=[any_spec] * n,
        out_shape=plan.out_shape(),
        scratch_shapes=plan.sems(),
        compiler_params=pltpu.CompilerParams(has_side_effects=True),
    )(*shards, plan.zeros())


def _exchange_in_chip(grads, shard_rows, name):
    n = len(grads)
    plan = _InChip([g.shape for g in grads], shard_rows)

    def body(*refs):
        args = (refs[:n], refs[n:2 * n]) + tuple(refs[2 * n:])
        plan.start(*args)
        plan.finish(*args)

    any_spec = pl.BlockSpec(memory_space=pl.ANY)
    return pl.pallas_call(
        body, name=name,
        in_specs=[any_spec] * n, out_specs=[any_spec] * n, out_shape=plan.out_shape(), scratch_shapes=plan.sems(),
        compiler_params=pltpu.CompilerParams(has_side_effects=True),
    )(*grads)


class _InChip:
    def __init__(self, shapes, shard_rows):
        self.shapes, self.rows, self.n, self.ns = shapes, shard_rows, len(shapes), len(CHIP_OF_SLOT)

    def out_shape(self):
        return [jax.ShapeDtypeStruct((self.ns, r, s[1]), BF16) for s, r in zip(self.shapes, self.rows)]

    def sems(self):
        return [pltpu.SemaphoreType.DMA((self.n, self.ns)), pltpu.SemaphoreType.DMA((self.n, self.ns))]

    def _copies(self, ins, theirs, send_sems, recv_sems):
        x, y, c = _mesh_place()
        sibling, _ = _peer(x, y, c, 1)
        copies = []
        for w in range(self.n):
            r = self.rows[w]
            for s, k in enumerate(CHIP_OF_SLOT):
                _, other = _peer(x, y, c, k | 1)
                copies.append(pltpu.make_async_remote_copy(
                    src_ref=ins[w].at[pl.ds(pl.multiple_of(other * r, 16), r), :], dst_ref=theirs[w].at[s],
                    send_sem=send_sems.at[w, s], recv_sem=recv_sems.at[w, s], device_id=sibling, device_id_type=MESH))
        return copies

    def start(self, *refs):
        for cp in self._copies(*refs):
            cp.start()

    def finish(self, *refs):
        for cp in self._copies(*refs):
            cp.wait()


def _pair_sum(grad, theirs, name):
    ns, r, c = theirs.shape
    tr = r if r * c <= 2 * 1024 * 1024 else _tile(r, (256, 128, 64, 32, 16))
    per_block = r // tr

    def body(a_ref, b_ref, o_ref):
        o_ref[...] = (a_ref[...].astype(F32) + b_ref[...].astype(F32)).astype(o_ref.dtype)

    def owner_rows(s, i):
        x, y, c_ = _mesh_place()
        fx, fy = s % 2, s // 2
        px, py = x + fx - 2 * x * fx, y + fy - 2 * y * fy
        return ((4 * px + 2 * py + c_) * per_block + i, 0)

    slot = pl.BlockSpec((None, tr, c), lambda s, i: (s, i, 0))
    return pl.pallas_call(
        body, name=name, grid=(ns, per_block),
        in_specs=[pl.BlockSpec((tr, c), owner_rows), slot], out_specs=slot,
        out_shape=jax.ShapeDtypeStruct((ns, r, c), theirs.dtype),
        compiler_params=_params(("parallel", "parallel")),
    )(grad, theirs)


def _exchange_between_chips(pairs, name):
    n = len(pairs)
    plan = _ChipExchange([p.shape for p in pairs])

    def body(*refs):
        args = (refs[:n], refs[n:2 * n]) + tuple(refs[2 * n:])
        plan.start(*args)
        plan.finish(*args)

    any_spec = pl.BlockSpec(memory_space=pl.ANY)
    return pl.pallas_call(
        body, name=name,
        in_specs=[any_spec] * n, out_specs=[any_spec] * n,
        out_shape=plan.out_shape(), scratch_shapes=plan.sems(),
        compiler_params=pltpu.CompilerParams(has_side_effects=True),
    )(*pairs)


class _ChipExchange:
    def __init__(self, shapes):
        self.shapes, self.n, self.ns = shapes, len(shapes), len(CHIP_OF_SLOT) - 1

    def out_shape(self):
        return [jax.ShapeDtypeStruct((self.ns,) + tuple(s[1:]), BF16) for s in self.shapes]

    def sems(self):
        return [pltpu.SemaphoreType.DMA((self.n, self.ns)), pltpu.SemaphoreType.DMA((self.n, self.ns))]

    def _copies(self, ins, outs, send_sems, recv_sems):
        x, y, c = _mesh_place()
        copies = []
        for w in range(self.n):
            for s, k in enumerate(CHIP_OF_SLOT[1:]):
                peer, _ = _peer(x, y, c, k)
                copies.append(pltpu.make_async_remote_copy(
                    src_ref=ins[w].at[s + 1], dst_ref=outs[w].at[s], send_sem=send_sems.at[w, s],
                    recv_sem=recv_sems.at[w, s], device_id=peer, device_id_type=MESH))
        return copies

    def start(self, *refs):
        for cp in self._copies(*refs):
            cp.start()

    def finish(self, *refs):
        for cp in self._copies(*refs):
            cp.wait()


def _sum_chips(pair, recv, name):
    ns, r, c = recv.shape
    tr, tc = _panel(r, c)

    def body(p_ref, x_ref, o_ref):
        acc = p_ref[...].astype(F32)
        for s in range(x_ref.shape[0]):
            acc = acc + x_ref[s].astype(F32)
        o_ref[...] = acc

    return pl.pallas_call(
        body, name=name, grid=(r // tr, c // tc),
        in_specs=[pl.BlockSpec((None, tr, tc), lambda i, j: (0, i, j)), pl.BlockSpec((ns, tr, tc), lambda i, j: (0, i, j))],
        out_specs=pl.BlockSpec((tr, tc), lambda i, j: (i, j)),
        out_shape=jax.ShapeDtypeStruct((r, c), F32),
        compiler_params=_params(("parallel", "parallel")),
    )(pair, recv)


def _panel(r, c):
    for tr in (1024, 512, 256, 128):
        if r % tr == 0 and tr * c <= 512 * 1024:
            return tr, c
    for tc in (2048, 1024, 512, 256, 128):
        if c % tc == 0 and r * tc <= 512 * 1024:
            return r, tc
    return _tile(r, (64, 32, 16, 8)), c


def _all_reduce_small(part):
    R, W = part.shape

    def body(x_ref, o_ref, buf, send_sems, recv_sems):
        x, y, c = _mesh_place()
        me = 4 * x + 2 * y + c
        buf[me] = x_ref[...]
        copies = []
        for k in range(1, NDEV):
            peer, _ = _peer(x, y, c, k)
            cp = pltpu.make_async_remote_copy(src_ref=x_ref, dst_ref=buf.at[me], send_sem=send_sems.at[k - 1],
                                              recv_sem=recv_sems.at[k - 1], device_id=peer, device_id_type=MESH)
            cp.start()
            copies.append(cp)
        for cp in copies:
            cp.wait()
        acc = buf[0]
        for d in range(1, NDEV):
            acc = acc + buf[d]
        o_ref[...] = acc

    vm = pl.BlockSpec(memory_space=pltpu.VMEM)
    return pl.pallas_call(
        body, name="all_reduce_small", in_specs=[vm], out_specs=vm,
        out_shape=jax.ShapeDtypeStruct((R, W), F32),
        scratch_shapes=[pltpu.VMEM((NDEV, R, W), F32), pltpu.SemaphoreType.DMA((NDEV - 1,)),
                        pltpu.SemaphoreType.DMA((NDEV - 1,))],
        compiler_params=pltpu.CompilerParams(has_side_effects=True),
    )(part)


def _adam_math(w, g, m, v):
    m2 = ADAM_B1 * m + (1.0 - ADAM_B1) * g
    v2 = ADAM_B2 * v + (1.0 - ADAM_B2) * (g * g)
    m_hat = m2 / (1.0 - ADAM_B1 ** ADAM_STEP)
    v_hat = v2 / (1.0 - ADAM_B2 ** ADAM_STEP)
    delta = -ADAM_LR * (m_hat / (jnp.sqrt(v_hat) + ADAM_EPS) + ADAM_WD * w)
    return delta, m2, v2


def _adamw(w, g, m, v, name):
    r, c = w.shape
    tr, tc = _panel(r, c)

    def body(w_ref, g_ref, m_ref, v_ref, d_ref, m2_ref, v2_ref):
        d_ref[...], m2_ref[...], v2_ref[...] = _adam_math(w_ref[...], g_ref[...], m_ref[...], v_ref[...])

    spec = pl.BlockSpec((tr, tc), lambda i, j: (i, j))
    return pl.pallas_call(
        body, name=name, grid=(r // tr, c // tc), in_specs=[spec] * 4, out_specs=[spec] * 3,
        out_shape=[jax.ShapeDtypeStruct((r, c), F32)] * 3,
        compiler_params=_params(("parallel", "parallel")),
    )(w, g, m, v)


GAINS = ("ffn1_pre", "ffn1_post", "mix_pre", "mix_post", "mem_norm", "ffn2_pre", "ffn2_post")
GAIN_ROWS = D // HD
ROW_LB = len(GAINS) * GAIN_ROWS
ROWS_GRAD_IN = ROW_LB + 24
ROWS_PACKED = ROW_LB + 32


def _small_update(gsum, w_p, m_p, v_p):
    def body(g_ref, w_ref, m_ref, v_ref, go_ref, d_ref, m2_ref, v2_ref):
        a0 = w_ref[ROW_LB:ROW_LB + 8, :]
        a1 = w_ref[ROW_LB + 8:ROW_LB + 16, :]
        mx = jnp.maximum(a0, a1)
        e0, e1 = jnp.exp(a0 - mx), jnp.exp(a1 - mx)
        lb = e0 / (e0 + e1)
        da0 = g_ref[ROW_LB:ROW_LB + 8, :] * lb * (1.0 - lb)
        g = jnp.concatenate([g_ref[0:ROW_LB, :], da0, -da0, g_ref[ROW_LB + 8:ROWS_GRAD_IN, :]], axis=0)
        go_ref[...] = g
        d_ref[...], m2_ref[...], v2_ref[...] = _adam_math(w_ref[...], g, m_ref[...], v_ref[...])

    vm = pl.BlockSpec(memory_space=pltpu.VMEM)
    return pl.pallas_call(
        body, name="small_update", in_specs=[vm] * 4, out_specs=[vm] * 4,
        out_shape=[jax.ShapeDtypeStruct((ROWS_PACKED, HD), F32)] * 4,
    )(gsum, w_p, m_p, v_p)


def _rows8(a):
    a = a.reshape(-1)
    rows = -(-a.shape[0] // HD)
    rows8 = -(-rows // 8) * 8
    return jnp.pad(a, (0, rows8 * HD - a.shape[0])).reshape(rows8, HD)


def _pack_small(gains, lb0, lb1, gnorm, fb):
    return jnp.concatenate([_rows8(g) for g in gains] + [_rows8(lb0), _rows8(lb1), _rows8(gnorm), _rows8(fb)], axis=0)


def _unpack_small(p):
    out = {}
    for i, name in enumerate(GAINS):
        out[name] = p[i * GAIN_ROWS:(i + 1) * GAIN_ROWS].reshape(1, D)
    lb0 = p[ROW_LB:ROW_LB + NH].reshape(1, WH)
    lb1 = p[ROW_LB + 8:ROW_LB + 8 + NH].reshape(1, WH)
    out["hgrn_lb"] = jnp.concatenate([lb0, lb1], axis=0)
    out["hgrn_gnorm"] = p[ROW_LB + 16:ROW_LB + 16 + NH].reshape(1, WH)
    out["fox_fb"] = p[ROW_LB + 24:ROW_LB + 25, 0:NH]
    return out


def _ffn_forward(n, wg_t, wu_t, wd, tag, rider=None, rider_down=None):
    g, u, a, *carried = _ffn_up(n, wg_t, wu_t, f"{tag}_up", rider)
    if wd is None:
        wd = carried[0]
    if rider_down is None:
        h = _mm(a, wd, "nn", F32, f"{tag}_down")
    else:
        h, *more = _mm(a, wd, "nn", F32, f"{tag}_down", rider=rider_down)
        carried = carried + more
    return h, (n, g, u, a), carried


def _mm_out(res):
    return (res[0], list(res[1:])) if isinstance(res, (list, tuple)) else (res, [])


def _ffn_backward(dh, saved, wg_t, wu_t, wd, tag, rider=None, exchange=None, rider_dwd=None, after_dwd=None):
    n, g, u, a = saved
    dwd, got = _mm_out(_mm(a, dh, "tn", BF16, f"{tag}_dwd", rider=rider_dwd))
    rider_dwg = None
    if after_dwd is not None:
        rider, rider_dwg = after_dwd(got)
    dg, du, *carried = _ffn_act_bwd(dh, wd, g, u, f"{tag}_act_bwd", rider)
    dwg, got = _mm_out(_mm(dg, n, "tn", BF16, f"{tag}_dwg", rider=rider_dwg))
    carried = carried + got
    dwu = _mm(du, n, "tn", BF16, f"{tag}_dwu")
    if exchange is None:
        dn = _mm(dg, wg_t, "nn", F32, f"{tag}_dn_g")
        dn = _mm(du, wu_t, "nn", F32, f"{tag}_dn_u", add=dn)
    else:
        ride_a, ride_b, take = exchange(dwg, dwu, dwd)
        dn, got_a = _mm_out(_mm(dg, wg_t, "nn", F32, f"{tag}_dn_g", rider=ride_a))
        dn, got_b = _mm_out(_mm(du, wu_t, "nn", F32, f"{tag}_dn_u", add=dn, rider=ride_b))
        take(got_a, got_b)
    return dn, (dwg, dwu, dwd), carried


GATHER_FIRST = ("ffn1_wg", "ffn1_wu")
GATHER_IN_FFN1_UP = ("ffn1_wd", "w_in")
GATHER_IN_FFN1_DOWN = ("w_gate",)
GATHER_IN_PROJ = ("w_mem_kv", "w_hgrn_out", "w_fox_out", "w_mem_out", "w_o")
GATHER_IN_GATE = ("ffn2_wg",)
GATHER_IN_HGRN = ("ffn2_wu",)
GATHER_IN_FFN2_UP = ("ffn2_wd",)
GROUP_FFN1 = ("ffn1_wg", "ffn1_wu", "ffn1_wd")
GROUP_MIX = ("w_in", "w_mem_kv", "w_hgrn_out", "w_fox_out", "w_mem_out", "w_gate", "w_o")
GROUP_FFN2 = ("ffn2_wg", "ffn2_wu", "ffn2_wd")


def _gather_rider(blocks, names):
    plan = _Gather([blocks[n].shape for n in names], [FFN_PAD.get(n, 0) for n in names])
    return _Rider(plan, [blocks[n] for n in names] + [plan.zeros()], GATHER_STEPS)


def _local_step(x, mem, tgt, small, wts=None, blocks=None):
    T = x.shape[0]
    tr = _tile(T, (256, 128))
    fb_pad = jnp.pad(small["fox_fb"], ((0, 0), (0, HD - NH)))
    dist = blocks is not None
    if dist:
        wts = dict(zip(GATHER_FIRST, _all_gather([blocks[n] for n in GATHER_FIRST], [FFN_PAD[n] for n in GATHER_FIRST])))

    def riding(names):
        return _gather_rider(blocks, names) if dist else None

    (n1,) = _rowwise(_norm_fn, [(x, 0)], [(small["ffn1_pre"], None)], [BF16], "ffn1_pre", tr, D, 1)
    h1, ffn1_saved, carried = _ffn_forward(n1, wts["ffn1_wg"], wts["ffn1_wu"], wts.get("ffn1_wd"), "ffn1",
                                           riding(GATHER_IN_FFN1_UP), riding(GATHER_IN_FFN1_DOWN))
    wts.update(zip(GATHER_IN_FFN1_UP + GATHER_IN_FFN1_DOWN, carried))
    ffn1_out = functools.partial(_post_pre_fn, 0.5)
    x1, un = _rowwise(ffn1_out, [(x, 0), (h1, 0)], [(small["ffn1_post"], None), (small["mix_pre"], None)], [F32, BF16],
                      "ffn1_post_mix_pre", tr, D, 1)
    if dist:
        proj, *carried = _mm(un, wts["w_in"], "nn", F32, "proj", rider=riding(GATHER_IN_PROJ))
        wts.update(zip(GATHER_IN_PROJ, carried))
        z, *carried = _mm(un, wts["w_gate"], "nt", BF16, "gate_logits", rider=riding(GATHER_IN_GATE))
        wts.update(zip(GATHER_IN_GATE, carried))
    else:
        proj = _mm(un, wts["w_in"], "nn", F32, "proj")
        z = _mm(un, wts["w_gate"], "nt", BF16, "gate_logits")
    (memn,) = _rowwise(_norm_fn, [(mem, 0)], [(small["mem_norm"], None)], [BF16], "mem_norm", mem.shape[0], D, 1)
    mem_kv = _mm(memn, wts["w_mem_kv"], "nn", F32, "mem_kv")

    o_raw, states, *carried = _hgrn_fwd(proj, small["hgrn_lb"], riding(GATHER_IN_HGRN))
    wts.update(zip(GATHER_IN_HGRN, carried))
    tr_head = _tile(T, (1024, 512, 256, 128))
    (o_h,) = _rowwise(_hpost_fn, [(o_raw, 0), (proj, CB_HOG)], [(small["hgrn_gnorm"], 0)], [BF16], "hgrn_post",
                      tr_head, HD, NH)
    ct, cq = _fox_cum(proj, fb_pad)
    win = _fox_windows(proj, cq)
    o_f, lse = _fox_fwd(win, proj, ct, cq)
    o_m = _mem_fwd(proj, mem_kv)

    yh = _mm(o_h, wts["w_hgrn_out"], "nt", BF16, "hgrn_out")
    yf = _mm(o_f, wts["w_fox_out"], "nt", BF16, "fox_out")
    ym = _mm(o_m, wts["w_mem_out"], "nt", BF16, "mem_out")
    zc = D // 512
    merge_rows = [(z, 0), (z, zc), (z, 2 * zc), (yh, 0), (yf, 0), (ym, 0)]
    tr_merge = _tile(T, (512, 256, 128))
    (merged,) = _rowwise(_merge_fn, merge_rows, [], [BF16], "merge", tr_merge, 512, zc)
    m = _mm(merged, wts["w_o"], "nn", F32, "mix_out")
    mix_out = functools.partial(_post_pre_fn, 1.0)
    x2, n2 = _rowwise(mix_out, [(x1, 0), (m, 0)], [(small["mix_post"], None), (small["ffn2_pre"], None)], [F32, BF16],
                      "mix_post_ffn2_pre", tr, D, 1)
    h2, ffn2_saved, carried = _ffn_forward(n2, wts["ffn2_wg"], wts["ffn2_wu"], wts.get("ffn2_wd"), "ffn2",
                                           riding(GATHER_IN_FFN2_UP))
    wts.update(zip(GATHER_IN_FFN2_UP, carried))
    dy, loss_part = _loss(x2, h2, small["ffn2_post"], tgt, "loss")

    gw, gs, reduced = {}, {}, {}

    def pair_sums(names, tag):
        if not dist:
            return None, None
        theirs = brought.get(tag)
        if theirs is None:
            theirs = _exchange_in_chip([gw[n] for n in names], [blocks[n].shape[0] for n in names], f"reduce_in_chip_{tag}")
        pairs = [_pair_sum(gw[n], t_, f"pair_{n}") for n, t_ in zip(names, theirs)]
        return pairs, _Rider(_ChipExchange([p.shape for p in pairs]), pairs, EXCHANGE_STEPS)

    def chip_sums(names, pairs, recv):
        for n, p_, r_ in zip(names, pairs or (), recv):
            reduced[n] = _sum_chips(p_, r_, f"sum_{n}")

    brought = {}

    def in_chip_rider(names):
        grads_ = [gw[n] for n in names]
        return _Rider(_InChip([g_.shape for g_ in grads_], [blocks[n].shape[0] for n in names]), grads_, EXCHANGE_STEPS)

    def ffn2_exchange(dwg, dwu, dwd):
        gw.update(ffn2_wg=dwg, ffn2_wu=dwu, ffn2_wd=dwd)
        return in_chip_rider(GROUP_FFN2), None, lambda got_a, got_b: brought.update(ffn2=got_a)

    dh2, gs["ffn2_post"] = _rowwise_bwd(functools.partial(_resid_h_fn, 0.5), [(h2, 0)], [(small["ffn2_post"], None)],
                                        [(dy, 0)], [0], [BF16], "ffn2_post_bwd", tr, D, 1)
    dn2, (gw["ffn2_wg"], gw["ffn2_wu"], gw["ffn2_wd"]), _ = _ffn_backward(
        dh2, ffn2_saved, wts["ffn2_wg"], wts["ffn2_wu"], wts["ffn2_wd"], "ffn2", exchange=ffn2_exchange if dist else None)
    pairs_ffn2, ride_ffn2_grads = pair_sums(GROUP_FFN2, "ffn2")

    dx1, dm, gs["mix_post"], gs["ffn2_pre"] = _rowwise_bwd(
        mix_out, [(x1, 0), (m, 0)], [(small["mix_post"], None), (small["ffn2_pre"], None)], [(dy, 0), (dn2, 0)], [0, 1],
        [F32, BF16], "mix_post_ffn2_pre_bwd", tr, D, 1)
    dmerged = _mm(dm, wts["w_o"], "nt", F32, "d_merged")
    gw["w_o"] = _mm(merged, dm, "tn", BF16, "d_w_o")
    dz0, dz1, dz2, dyh, dyf, dym = _rowwise_bwd(_merge_fn, merge_rows, [], [(dmerged, 0)], [0, 1, 2, 3, 4, 5], [BF16] * 6,
                                                "merge_bwd", tr_merge, 512, zc)
    dz = jnp.concatenate([dz0, dz1, dz2], axis=1)
    gw["w_gate"] = _mm(dz, un, "tn", BF16, "d_w_gate")
    dun = _mm(dz, wts["w_gate"], "nn", F32, "d_un_gate")

    do_h = _mm(dyh, wts["w_hgrn_out"], "nn", F32, "d_o_h")
    gw["w_hgrn_out"] = _mm(dyh, o_h, "tn", BF16, "d_w_hgrn_out")
    do_f = _mm(dyf, wts["w_fox_out"], "nn", F32, "d_o_f")
    gw["w_fox_out"] = _mm(dyf, o_f, "tn", BF16, "d_w_fox_out")
    do_m = _mm(dym, wts["w_mem_out"], "nn", F32, "d_o_m")
    gw["w_mem_out"] = _mm(dym, o_m, "tn", BF16, "d_w_mem_out")

    do_raw, dhog, gs["hgrn_gnorm"] = _rowwise_bwd(_hpost_fn, [(o_raw, 0), (proj, CB_HOG)], [(small["hgrn_gnorm"], 0)],
                                                  [(do_h, 0)], [0, 1], [F32, BF16], "hgrn_post_bwd", tr_head, HD, NH)
    dhq, dhf, dhi, gs["hgrn_lb"], *carried = _hgrn_bwd(proj, small["hgrn_lb"], states, do_raw, ride_ffn2_grads)
    chip_sums(GROUP_FFN2, pairs_ffn2, carried)
    dfq, delta = _fox_bwd_dq(win, proj, ct, cq, lse, do_f)
    dfk, dfv, dc = _fox_bwd_dkv(win, proj, ct, cq, lse, delta, do_f)
    dff, dfb = _fox_cum_bwd(dc, proj, fb_pad)
    gs["fox_fb"] = dfb
    dmq, dmk, dmv = _mem_bwd(proj, mem_kv, do_m)

    dproj = jnp.concatenate([dhq, dhf, dhi, dhog, dfq, dfk, dfv, dff, dmq, jnp.zeros((T, HD), BF16)], axis=1)
    gw["w_in"] = _mm(un, dproj, "tn", BF16, "d_w_in")
    dun = _mm(dproj, wts["w_in"], "nt", F32, "d_un_proj", add=dun)
    dx0, dh1, gs["ffn1_post"], gs["mix_pre"] = _rowwise_bwd(
        ffn1_out, [(x, 0), (h1, 0)], [(small["ffn1_post"], None), (small["mix_pre"], None)], [(dx1, 0), (dun, 0)], [0, 1],
        [F32, BF16], "ffn1_post_mix_pre_bwd", tr, D, 1)

    dmem_kv = jnp.concatenate([dmk, dmv], axis=1)
    gw["w_mem_kv"] = _mm(memn, dmem_kv, "tn", BF16, "d_w_mem_kv")
    dmemn = _mm(dmem_kv, wts["w_mem_kv"], "nt", F32, "d_memn")
    _, gs["mem_norm"] = _rowwise_bwd(_norm_fn, [(mem, 0)], [(small["mem_norm"], None)], [(dmemn, 0)], [0], [BF16],
                                     "mem_norm_bwd", mem.shape[0], D, 1)

    mix = {}

    def mix_after_dwd(got):
        brought.update(mix=got)
        pairs, _ = pair_sums(GROUP_MIX, "mix")
        mix["names"] = GROUP_MIX[1:] + GROUP_MIX[:1]
        mix["pairs"] = pairs[1:] + pairs[:1]
        return tuple(_Rider(_ChipExchange([p.shape for p in part]), part, EXCHANGE_STEPS) for part in (pairs[1:], pairs[:1]))

    def own_exchange(dwg, dwu, dwd):
        gw.update(ffn1_wg=dwg, ffn1_wu=dwu, ffn1_wd=dwd)
        pairs, _ = pair_sums(GROUP_FFN1, "ffn1")
        first, second = pairs[:2], pairs[2:]

        def take(got_a, got_b):
            chip_sums(GROUP_FFN1, pairs, list(got_a) + list(got_b))

        return (_Rider(_ChipExchange([p.shape for p in first]), first, EXCHANGE_STEPS),
                _Rider(_ChipExchange([p.shape for p in second]), second, EXCHANGE_STEPS), take)

    dn1, (gw["ffn1_wg"], gw["ffn1_wu"], gw["ffn1_wd"]), carried = _ffn_backward(
        dh1, ffn1_saved, wts["ffn1_wg"], wts["ffn1_wu"], wts["ffn1_wd"], "ffn1",
        exchange=own_exchange if dist else None, rider_dwd=in_chip_rider(GROUP_MIX) if dist else None,
        after_dwd=mix_after_dwd if dist else None)
    chip_sums(mix.get("names", ()), mix.get("pairs"), carried)
    dx, gs["ffn1_pre"] = _rowwise_bwd(_norm_res_fn, [(x, 0)], [(small["ffn1_pre"], None)], [(dx0, 0), (dn1, 0)], [0], [F32],
                                      "ffn1_pre_bwd", tr, D, 1)
    return loss_part, dx, (reduced if dist else gw), gs


BIG = ("ffn1_wg", "ffn1_wu", "ffn1_wd", "w_in", "w_mem_kv", "w_hgrn_out", "w_fox_out", "w_mem_out", "w_gate", "w_o",
       "ffn2_wg", "ffn2_wu", "ffn2_wd")
TRANSPOSED = ("ffn1_wg", "ffn1_wu", "ffn2_wg", "ffn2_wu", "w_hgrn_out", "w_fox_out", "w_mem_out", "w_gate")
FFN_PAD = {"ffn1_wg": FP - F, "ffn1_wu": FP - F, "ffn1_wd": FP - F, "ffn2_wg": FP - F, "ffn2_wu": FP - F,
           "ffn2_wd": FP - F}
SMALL = GAINS + ("hgrn_lb", "hgrn_gnorm", "fox_fb")
WEIGHTS = ("ffn1_pre", "ffn1_post", "ffn1_wg", "ffn1_wu", "ffn1_wd", "mix_pre", "mix_post", "mem_norm", "w_in", "hgrn_lb",
           "hgrn_gnorm", "fox_fb", "w_mem_kv", "w_hgrn_out", "w_fox_out", "w_mem_out", "w_gate", "w_o", "ffn2_pre",
           "ffn2_post", "ffn2_wg", "ffn2_wu", "ffn2_wd")


def _to_gather_layout(name, w):
    if name in TRANSPOSED:
        w = w.T
    if name == "w_in":
        r = w.shape[0]
        w = jnp.concatenate([w[:, :MQ_COL], jnp.zeros((r, FF_COL + HD - MQ_COL), w.dtype), w[:, MQ_COL:],
                             jnp.zeros((r, P - FF_COL - HD - WM), w.dtype)], axis=1)
    return w.astype(BF16)


def _from_gather_layout(name, g):
    if name == "w_in":
        g = jnp.concatenate([g[:, :MQ_COL], g[:, FF_COL + HD:FF_COL + HD + WM]], axis=1)
    if name in TRANSPOSED:
        g = g.T
    return g


def kernel(x, mem, ffn1_pre, ffn1_post, ffn1_wg, ffn1_wu, ffn1_wd, mix_pre, mix_post, mem_norm, w_in, hgrn_lb, hgrn_gnorm, fox_fb, w_mem_kv, w_hgrn_out, w_fox_out, w_mem_out, w_gate, w_o, ffn2_pre, ffn2_post, ffn2_wg, ffn2_wu, ffn2_wd, loss_target, m_ffn1_pre, m_ffn1_post, m_ffn1_wg, m_ffn1_wu, m_ffn1_wd, m_mix_pre, m_mix_post, m_mem_norm, m_w_in, m_hgrn_lb, m_hgrn_gnorm, m_fox_fb, m_w_mem_kv, m_w_hgrn_out, m_w_fox_out, m_w_mem_out, m_w_gate, m_w_o, m_ffn2_pre, m_ffn2_post, m_ffn2_wg, m_ffn2_wu, m_ffn2_wd, v_ffn1_pre, v_ffn1_post, v_ffn1_wg, v_ffn1_wu, v_ffn1_wd, v_mix_pre, v_mix_post, v_mem_norm, v_w_in, v_hgrn_lb, v_hgrn_gnorm, v_fox_fb, v_w_mem_kv, v_w_hgrn_out, v_w_fox_out, v_w_mem_out, v_w_gate, v_w_o, v_ffn2_pre, v_ffn2_post, v_ffn2_wg, v_ffn2_wu, v_ffn2_wd):
    a = dict(locals())
    small = {n: a[n] for n in SMALL}
    shard = {n: a[n][0] if a[n].ndim == 3 else a[n] for n in BIG}

    blocks = {n: _to_gather_layout(n, shard[n]) for n in BIG}
    loss_part, dx, reduced, gs = _local_step(x[0], mem[0], loss_target[0], small, blocks=blocks)
    loss = lax.psum(0.5 / D * jnp.sum(loss_part), ("x", "y", "c"))

    grads, deltas, new_m, new_v = {}, {}, {}, {}
    for n in BIG:
        g = _from_gather_layout(n, reduced[n])
        d, m2, v2 = _adamw(shard[n], g, a["m_" + n].reshape(g.shape), a["v_" + n].reshape(g.shape), f"adamw_{n}")
        full = a[n].shape
        grads[n], deltas[n], new_m[n], new_v[n] = g.reshape(full), d.reshape(full), m2.reshape(full), v2.reshape(full)

    part = jnp.concatenate([_rows8(gs[n]) for n in GAINS] + [_rows8(gs["hgrn_lb"]), _rows8(gs["hgrn_gnorm"]),
                                                             _rows8(gs["fox_fb"][:, :NH])], axis=0)
    gsum = _all_reduce_small(part)

    def packed(prefix):
        lb = a[prefix + "hgrn_lb"]
        return _pack_small([a[prefix + n] for n in GAINS], lb[0], lb[1], a[prefix + "hgrn_gnorm"], a[prefix + "fox_fb"])

    g_p, d_p, m_p, v_p = _small_update(gsum, packed(""), packed("m_"), packed("v_"))
    for dst, p in ((grads, g_p), (deltas, d_p), (new_m, m_p), (new_v, v_p)):
        dst.update(_unpack_small(p))

    return (loss, dx[None], *[grads[n] for n in WEIGHTS], *[deltas[n] for n in WEIGHTS],
            *[new_m[n] for n in WEIGHTS], *[new_v[n] for n in WEIGHTS])
```

```python
import functools

import jax
import jax.numpy as jnp
from jax import lax
from jax.experimental import pallas as pl
from jax.experimental.pallas import tpu as pltpu

F32 = jnp.float32
BF16 = jnp.bfloat16
HIGHEST = lax.Precision.HIGHEST

NDEV = 8
D = 2048
F = 5504
FP = 5632
HD = 128
NH = 6
NM = 4
WH = NH * HD
WM = NM * HD
P = 6144
FF_COL = 5376
MQ_COL = 5382
CHUNK = 64
EPS = 1e-6
SCALE = HD ** -0.5
NEG = -1e30
VMEM_LIMIT = 48 * 1024 * 1024

CB_HQ, CB_HF, CB_HI, CB_HOG, CB_FQ, CB_FK, CB_FV, CB_FF, CB_MQ = 0, 6, 12, 18, 24, 30, 36, 42, 43

ADAM_LR, ADAM_B1, ADAM_B2, ADAM_EPS, ADAM_WD, ADAM_STEP = 0.001, 0.9, 0.999, 1e-08, 0.01, 10

NT = (((1,), (1,)), ((), ()))
NN = (((1,), (0,)), ((), ()))
TN = (((0,), (0,)), ((), ()))
MESH = pl.DeviceIdType.MESH


def _params(sem=None, **kw):
    return pltpu.CompilerParams(dimension_semantics=sem, vmem_limit_bytes=VMEM_LIMIT, **kw)


def _tile(n, prefs):
    for p in prefs:
        if p <= n and n % p == 0:
            return p
    return n


def _dot(a, b, dims):
    return lax.dot_general(a.astype(BF16), b.astype(BF16), dims, preferred_element_type=F32)


def _mm(a, b, mode, out_dtype, name, add=None, rider=None):
    if mode == "nn":
        (M, K), (K2, N) = a.shape, b.shape
    elif mode == "nt":
        (M, K), (N, K2) = a.shape, b.shape
    else:
        (K, M), (K2, N) = a.shape, b.shape
    assert K == K2, (a.shape, b.shape, mode)
    if mode == "tn":
        tm = _tile(M, (512, 256, 128))
        tn = _tile(N, (1024, 768, 512, 256, 128))
        tk = _tile(K, (4096, 2048, 1024, 512, 256, 128))
    else:
        tm = _tile(M, (1024, 512, 256, 128)) if K <= 2048 else _tile(M, (512, 256, 128))
        tn = _tile(N, (512, 768, 256, 128))
        tk = K if K <= 6144 else _tile(K, (2048, 1024, 512, 256, 128))
    nk = K // tk
    dims = {"nn": NN, "nt": NT, "tn": TN}[mode]
    has_add = add is not None

    ni, nj = M // tm, N // tn
    n_in = 3 if has_add else 2

    def body(*refs):
        step = (pl.program_id(0) * nj + pl.program_id(1)) * nk + pl.program_id(2)
        refs = _carry(rider, refs, n_in, 1, 1, step, ni * nj * nk)
        a_ref, b_ref = refs[0], refs[1]
        c_ref = refs[2] if has_add else None
        o_ref = refs[3] if has_add else refs[2]
        acc_ref = refs[-1]
        k = pl.program_id(2)
        part = _dot(a_ref[...], b_ref[...], dims)

        def finish(r):
            if has_add:
                r = r + c_ref[...].astype(F32)
            o_ref[...] = r.astype(o_ref.dtype)

        if nk == 1:
            finish(part)
        else:
            @pl.when(k == 0)
            def _():
                acc_ref[...] = part

            @pl.when(k > 0)
            def _():
                acc_ref[...] += part

            @pl.when(k == nk - 1)
            def _():
                finish(acc_ref[...])

    if mode == "nn":
        a_spec = pl.BlockSpec((tm, tk), lambda i, j, k: (i, k))
        b_spec = pl.BlockSpec((tk, tn), lambda i, j, k: (k, j))
    elif mode == "nt":
        a_spec = pl.BlockSpec((tm, tk), lambda i, j, k: (i, k))
        b_spec = pl.BlockSpec((tn, tk), lambda i, j, k: (j, k))
    else:
        a_spec = pl.BlockSpec((tk, tm), lambda i, j, k: (k, i))
        b_spec = pl.BlockSpec((tk, tn), lambda i, j, k: (k, j))
    o_spec = pl.BlockSpec((tm, tn), lambda i, j, k: (i, j))
    args = (a, b) + ((add,) if has_add else ())
    in_specs, out_specs, out_shape, scratch, extra = _with_rider(
        rider, [a_spec, b_spec] + ([o_spec] if has_add else []), [o_spec], [jax.ShapeDtypeStruct((M, N), out_dtype)],
        [pltpu.VMEM((tm, tn) if nk > 1 else (8, 128), F32)])
    out = pl.pallas_call(
        body, name=name, grid=(ni, nj, nk), in_specs=in_specs, out_specs=out_specs, out_shape=out_shape,
        scratch_shapes=scratch,
        compiler_params=_params(("arbitrary",) * 3 if rider else ("parallel", "parallel", "arbitrary"),
                                has_side_effects=rider is not None),
    )(*args, *extra)
    return out if rider else out[0]


class _Rider:
    def __init__(self, plan, inputs, steps):
        self.plan, self.inputs, self.steps = plan, list(inputs), steps
        self.n_out = len(plan.out_shape())
        self.n_sem = len(plan.sems())

    def run(self, step, total, in_refs, out_refs, sem_refs):
        n = self.plan.n
        if isinstance(self.plan, _Gather):
            args = (in_refs[:n], in_refs[n], out_refs) + tuple(sem_refs)
        else:
            args = (in_refs, out_refs) + tuple(sem_refs)
        for frac, method in self.steps:
            @pl.when(step == int(frac * (total - 1)))
            def _(method=method):
                getattr(self.plan, method)(*args)


GATHER_STEPS = ((0.0, "start"), (0.6, "forward"), (1.0, "finish"))
EXCHANGE_STEPS = ((0.0, "start"), (1.0, "finish"))


def _carry(rider, refs, n_in, n_out, n_scratch, step, total):
    if rider is None:
        return refs
    ri, ro, rs = len(rider.inputs), rider.n_out, rider.n_sem
    own_in, rid_in = refs[:n_in], refs[n_in:n_in + ri]
    own_out, rid_out = refs[n_in + ri:n_in + ri + n_out], refs[n_in + ri + n_out:n_in + ri + n_out + ro]
    own_scr, rid_sem = refs[n_in + ri + n_out + ro:n_in + ri + n_out + ro + n_scratch], refs[len(refs) - rs:]
    rider.run(step, total, rid_in, rid_out, rid_sem)
    return tuple(own_in) + tuple(own_out) + tuple(own_scr)


def _with_rider(rider, in_specs, out_specs, out_shape, scratch):
    if rider is None:
        return in_specs, out_specs, out_shape, scratch, ()
    any_spec = pl.BlockSpec(memory_space=pl.ANY)
    return (list(in_specs) + [any_spec] * len(rider.inputs), list(out_specs) + [any_spec] * rider.n_out,
            list(out_shape) + rider.plan.out_shape(), list(scratch) + rider.plan.sems(), tuple(rider.inputs))


def _ffn_up(n, wg_t, wu_t, name, rider=None):
    T = n.shape[0]
    tm = _tile(T, (1024, 512, 256, 128))
    tn = 512
    ni, nj = T // tm, FP // tn

    def body(*refs):
        step = pl.program_id(0) * nj + pl.program_id(1)
        n_ref, wg_ref, wu_ref, g_ref, u_ref, a_ref = _carry(rider, refs, 3, 3, 0, step, ni * nj)
        x = n_ref[...]
        g = _dot(x, wg_ref[...], NT)
        u = _dot(x, wu_ref[...], NT)
        g_ref[...] = g.astype(g_ref.dtype)
        u_ref[...] = u.astype(u_ref.dtype)
        a_ref[...] = (g * jax.nn.sigmoid(g) * u).astype(BF16)

    w_spec = pl.BlockSpec((tn, D), lambda i, j: (j, 0))
    o_spec = pl.BlockSpec((tm, tn), lambda i, j: (i, j))
    in_specs, out_specs, out_shape, scratch, extra = _with_rider(
        rider, [pl.BlockSpec((tm, D), lambda i, j: (i, 0)), w_spec, w_spec], [o_spec, o_spec, o_spec],
        [jax.ShapeDtypeStruct((T, FP), BF16)] * 3, [])
    return pl.pallas_call(
        body, name=name, grid=(ni, nj), in_specs=in_specs, out_specs=out_specs, out_shape=out_shape,
        scratch_shapes=scratch,
        compiler_params=_params(("arbitrary", "arbitrary") if rider else ("parallel", "parallel"),
                                has_side_effects=rider is not None),
    )(n, wg_t, wu_t, *extra)


def _ffn_act_bwd(dh, wd, g, u, name, rider=None):
    T = dh.shape[0]
    tm = _tile(T, (1024, 512, 256, 128))
    tn = 512
    ni, nj = T // tm, FP // tn

    def body(*refs):
        step = pl.program_id(0) * nj + pl.program_id(1)
        dh_ref, wd_ref, g_ref, u_ref, dg_ref, du_ref = _carry(rider, refs, 4, 2, 0, step, ni * nj)
        da = _dot(dh_ref[...], wd_ref[...], NT)
        g = g_ref[...].astype(F32)
        sg = jax.nn.sigmoid(g)
        dg_ref[...] = (da * u_ref[...].astype(F32) * (sg * (1.0 + g * (1.0 - sg)))).astype(dg_ref.dtype)
        du_ref[...] = (da * (g * sg)).astype(du_ref.dtype)

    tile = pl.BlockSpec((tm, tn), lambda i, j: (i, j))
    in_specs, out_specs, out_shape, scratch, extra = _with_rider(
        rider, [pl.BlockSpec((tm, D), lambda i, j: (i, 0)), pl.BlockSpec((tn, D), lambda i, j: (j, 0)), tile, tile],
        [tile, tile], [jax.ShapeDtypeStruct((T, FP), BF16), jax.ShapeDtypeStruct((T, FP), BF16)], [])
    return pl.pallas_call(
        body, name=name, grid=(ni, nj), in_specs=in_specs, out_specs=out_specs, out_shape=out_shape,
        scratch_shapes=scratch,
        compiler_params=_params(("arbitrary", "arbitrary") if rider else ("parallel", "parallel"),
                                has_side_effects=rider is not None),
    )(dh, wd, g, u, *extra)


def _row_specs(rows, tr, cw):
    return [pl.BlockSpec((tr, cw), lambda j, i, o=off: (i, o + j)) for _, off in rows]


def _const_specs(consts, cw):
    specs = []
    for arr, off in consts:
        if off is None:
            specs.append(pl.BlockSpec(arr.shape, lambda j, i: (0, 0)))
        else:
            specs.append(pl.BlockSpec((arr.shape[0], cw), lambda j, i, o=off: (0, o + j)))
    return specs


def _rowwise(fn, rows, consts, out_dtypes, name, tr, cw, ncol):
    T = rows[0][0].shape[0]
    nr, nc = len(rows), len(consts)

    def body(*refs):
        r = [x[...].astype(F32) for x in refs[:nr]]
        c = [x[...] for x in refs[nr:nr + nc]]
        res = fn(*r, *c)
        for o_ref, v in zip(refs[nr + nc:], res):
            o_ref[...] = v.astype(o_ref.dtype)

    o_spec = pl.BlockSpec((tr, cw), lambda j, i: (i, j))
    return pl.pallas_call(
        body, name=name, grid=(ncol, T // tr),
        in_specs=_row_specs(rows, tr, cw) + _const_specs(consts, cw),
        out_specs=[o_spec] * len(out_dtypes),
        out_shape=[jax.ShapeDtypeStruct((T, ncol * cw), dt) for dt in out_dtypes],
        compiler_params=_params(("parallel", "parallel")),
    )(*[a for a, _ in rows], *[a for a, _ in consts])


def _rowwise_bwd(fn, rows, consts, cots, diff, ddtypes, name, tr, cw, ncol):
    T = rows[0][0].shape[0]
    nr, nc, nt, nd = len(rows), len(consts), len(cots), len(diff)

    def body(*refs):
        r = [x[...].astype(F32) for x in refs[:nr]]
        c = [x[...] for x in refs[nr:nr + nc]]
        ct = [x[...].astype(F32) for x in refs[nr + nc:nr + nc + nt]]
        drow_refs = refs[nr + nc + nt:nr + nc + nt + nd]
        dconst_refs = refs[nr + nc + nt + nd:]
        i = pl.program_id(1)

        def f(*args):
            full = list(r)
            for idx, a in zip(diff, args[:nd]):
                full[idx] = a
            return tuple(fn(*full, *args[nd:]))

        _, vjp = jax.vjp(f, *[r[d] for d in diff], *c)
        g = vjp(tuple(ct))
        for o_ref, v in zip(drow_refs, g[:nd]):
            o_ref[...] = v.astype(o_ref.dtype)

        @pl.when(i == 0)
        def _():
            for o_ref in dconst_refs:
                o_ref[...] = jnp.zeros_like(o_ref)

        for o_ref, v in zip(dconst_refs, g[nd:]):
            o_ref[...] += v

    o_spec = pl.BlockSpec((tr, cw), lambda j, i: (i, j))
    out_shape = [jax.ShapeDtypeStruct((T, ncol * cw), dt) for dt in ddtypes]
    out_shape += [jax.ShapeDtypeStruct(a.shape, F32) for a, _ in consts]
    return pl.pallas_call(
        body, name=name, grid=(ncol, T // tr),
        in_specs=_row_specs(rows, tr, cw) + _const_specs(consts, cw) + _row_specs(cots, tr, cw),
        out_specs=[o_spec] * nd + _const_specs(consts, cw),
        out_shape=out_shape,
        compiler_params=_params(("parallel", "arbitrary")),
    )(*[a for a, _ in rows], *[a for a, _ in consts], *[a for a, _ in cots])


def _rms(x, g):
    return x * lax.rsqrt(jnp.mean(x * x, axis=-1, keepdims=True) + EPS) * g


def _silu(x):
    return x * jax.nn.sigmoid(x)


def _norm_fn(x, g):
    return (_rms(x, g),)


def _norm_res_fn(x, g):
    return (x, _rms(x, g))


def _post_pre_fn(scale, x, h, g_post, g_pre):
    xn = x + scale * _rms(h, g_post)
    return (xn, _rms(xn, g_pre))


def _resid_h_fn(scale, h, g):
    return (scale * _rms(h, g),)


def _hpost_fn(o, hog, gn):
    return (_rms(o, gn) * _silu(hog),)


def _merge_fn(z0, z1, z2, yh, yf, ym):
    return (jax.nn.sigmoid(z0) * yh + jax.nn.sigmoid(z1) * yf + jax.nn.sigmoid(z2) * ym,)


def _loss(x2, h, g_post, tgt, name):
    T = x2.shape[0]
    tr = _tile(T, (256, 128))

    def body(x_ref, h_ref, g_ref, t_ref, dy_ref, s_ref):
        i = pl.program_id(0)
        e = x_ref[...] + 0.5 * _rms(h_ref[...], g_ref[...]) - t_ref[...]
        dy_ref[...] = e * (1.0 / D)
        col = jnp.sum(e * e, axis=0, keepdims=True)
        tot = col[:, 0:HD]
        for k in range(1, D // HD):
            tot = tot + col[:, k * HD:(k + 1) * HD]

        @pl.when(i == 0)
        def _():
            s_ref[...] = jnp.zeros_like(s_ref)

        s_ref[...] += tot

    spec = pl.BlockSpec((tr, D), lambda i: (i, 0))
    return pl.pallas_call(
        body, name=name, grid=(T // tr,), in_specs=[spec, spec, pl.BlockSpec((1, D), lambda i: (0, 0)), spec],
        out_specs=[spec, pl.BlockSpec((1, HD), lambda i: (0, 0))],
        out_shape=[jax.ShapeDtypeStruct((T, D), F32), jax.ShapeDtypeStruct((1, HD), F32)],
        compiler_params=_params(("arbitrary",)),
    )(x2, h, g_post, tgt)


def _lower_bound(lb_ref):
    a0 = lb_ref[0:1, :]
    a1 = lb_ref[1:2, :]
    mx = jnp.maximum(a0, a1)
    e0 = jnp.exp(a0 - mx)
    return e0 / (e0 + jnp.exp(a1 - mx))


def _hgrn_prep(hq, hf, lb):
    g = lb + (1.0 - lb) * jax.nn.sigmoid(hf)
    return _silu(hq), 1.0 - g, jnp.log(g)


def _tri(n, upper):
    r = lax.broadcasted_iota(jnp.int32, (n, n), 0)
    c = lax.broadcasted_iota(jnp.int32, (n, n), 1)
    return (c >= r) if upper else (c <= r)


def _hgrn_factors(q, k, gl):
    low = _tri(CHUNK, False)
    b = lax.dot_general(low.astype(F32), gl, NN, precision=HIGHEST, preferred_element_type=F32)
    bl = b[CHUNK - 1:CHUNK, :]
    ref = b[CHUNK // 2 - 1:CHUNK // 2, :]
    eb = jnp.exp(b)
    ea = jnp.exp(b - ref)
    ebn = jnp.exp(ref - b)
    ek = jnp.exp(bl - b)
    ebl = jnp.exp(bl)
    return low, eb, ea, ebn, ek, ebl


def _hgrn_fwd(proj, hgrn_lb, rider=None):
    T = proj.shape[0]
    cb = _tile(T, (512, 256, 128, 64))
    nchunk = cb // CHUNK

    def body(*refs):
        hq_ref, hf_ref, hi_ref, lb_ref, o_ref, st_ref, state = _carry(rider, refs, 4, 2, 1, pl.program_id(0), T // cb)

        @pl.when(pl.program_id(0) == 0)
        def _():
            state[...] = jnp.zeros_like(state)

        lb = _lower_bound(lb_ref)

        def chunk(c, carry):
            r0 = pl.multiple_of(c * CHUNK, CHUNK)
            for h in range(NH):
                cols = slice(h * HD, (h + 1) * HD)
                q, k, gl = _hgrn_prep(hq_ref[pl.ds(r0, CHUNK), cols], hf_ref[pl.ds(r0, CHUNK), cols], lb[:, cols])
                v = hi_ref[pl.ds(r0, CHUNK), cols]
                low, eb, ea, ebn, ek, ebl = _hgrn_factors(q, k, gl)
                s_t = state[h]
                st_ref[c, h] = s_t
                pm = jnp.where(low, _dot(q * ea, k * ebn, NT), 0.0)
                o_ref[pl.ds(r0, CHUNK), cols] = _dot(q * eb, s_t, NT) + _dot(pm, v, NN)
                state[h] = s_t * ebl + _dot(v, k * ek, TN)
            return carry

        lax.fori_loop(0, nchunk, chunk, 0)

    def col(off):
        return pl.BlockSpec((cb, WH), lambda i, o=off: (i, o))

    in_specs, out_specs, out_shape, scratch, extra = _with_rider(
        rider, [col(0), col(1), col(2), pl.BlockSpec((2, WH), lambda i: (0, 0))],
        [pl.BlockSpec((cb, WH), lambda i: (i, 0)), pl.BlockSpec((nchunk, NH, HD, HD), lambda i: (i, 0, 0, 0))],
        [jax.ShapeDtypeStruct((T, WH), F32), jax.ShapeDtypeStruct((T // CHUNK, NH, HD, HD), F32)],
        [pltpu.VMEM((NH, HD, HD), F32)])
    return pl.pallas_call(
        body, name="hgrn_fwd", grid=(T // cb,), in_specs=in_specs, out_specs=out_specs, out_shape=out_shape,
        scratch_shapes=scratch, compiler_params=_params(("arbitrary",), has_side_effects=rider is not None),
    )(proj, proj, proj, hgrn_lb, *extra)


def _hgrn_bwd(proj, hgrn_lb, states, do, rider=None):
    T = proj.shape[0]
    cb = _tile(T, (512, 256, 128, 64))
    nchunk = cb // CHUNK
    nb = T // cb

    def body(*refs):
        (hq_ref, hf_ref, hi_ref, lb_ref, st_ref, do_ref, dhq_ref, dhf_ref, dhi_ref, dlb_ref,
         dstate) = _carry(rider, refs, 6, 4, 1, pl.program_id(0), nb)

        @pl.when(pl.program_id(0) == 0)
        def _():
            dstate[...] = jnp.zeros_like(dstate)
            dlb_ref[...] = jnp.zeros_like(dlb_ref)

        lb = _lower_bound(lb_ref)
        up = _tri(CHUNK, True)
        last = lax.broadcasted_iota(jnp.int32, (CHUNK, HD), 0) == CHUNK - 1

        def chunk(cc, carry):
            c = nchunk - 1 - cc
            r0 = pl.multiple_of(c * CHUNK, CHUNK)
            for h in range(NH):
                cols = slice(h * HD, (h + 1) * HD)
                hq = hq_ref[pl.ds(r0, CHUNK), cols]
                hf = hf_ref[pl.ds(r0, CHUNK), cols]
                (q, k, gl), prep_vjp = jax.vjp(_hgrn_prep, hq, hf, lb[:, cols])
                v = hi_ref[pl.ds(r0, CHUNK), cols]
                d_o = do_ref[pl.ds(r0, CHUNK), cols]
                low, eb, ea, ebn, ek, ebl = _hgrn_factors(q, k, gl)
                s_t = st_ref[c, h]
                ds_new = dstate[h]
                qe, am, bm, kb = q * eb, q * ea, k * ebn, k * ek
                pm_t = jnp.where(up, _dot(bm, am, NT), 0.0)
                dp = jnp.where(low, _dot(d_o, v, NT), 0.0)
                dp_t = jnp.where(up, _dot(v, d_o, NT), 0.0)
                dqe = _dot(d_o, s_t, NN)
                da = _dot(dp, bm, NN)
                db_m = _dot(dp_t, am, NN)
                dkb = _dot(v, ds_new, NN)
                dv = _dot(pm_t, d_o, NN) + _dot(kb, ds_new, NT)
                dq = dqe * eb + da * ea
                dk = db_m * ebn + dkb * ek
                dbl = jnp.sum(dkb * kb, axis=0, keepdims=True) + jnp.sum(ds_new * s_t, axis=0, keepdims=True) * ebl
                db = (dqe * qe + da * am.astype(BF16).astype(F32) - db_m * bm.astype(BF16).astype(F32) - dkb * kb
                      + jnp.where(last, dbl, 0.0))
                dgl = lax.dot_general(up.astype(F32), db, NN, precision=HIGHEST, preferred_element_type=F32)
                dhq, dhf, dlb = prep_vjp((dq, dk, dgl))
                dhq_ref[pl.ds(r0, CHUNK), cols] = dhq.astype(dhq_ref.dtype)
                dhf_ref[pl.ds(r0, CHUNK), cols] = dhf.astype(dhf_ref.dtype)
                dhi_ref[pl.ds(r0, CHUNK), cols] = dv.astype(dhi_ref.dtype)
                dlb_ref[:, cols] += dlb
                dstate[h] = _dot(d_o, qe, TN) + ds_new * ebl
            return carry

        lax.fori_loop(0, nchunk, chunk, 0)

    def col(off):
        return pl.BlockSpec((cb, WH), lambda i, o=off: (nb - 1 - i, o))

    row = pl.BlockSpec((cb, WH), lambda i: (nb - 1 - i, 0))
    in_specs, out_specs, out_shape, scratch, extra = _with_rider(
        rider, [col(0), col(1), col(2), pl.BlockSpec((2, WH), lambda i: (0, 0)),
                pl.BlockSpec((nchunk, NH, HD, HD), lambda i: (nb - 1 - i, 0, 0, 0)), row],
        [row, row, row, pl.BlockSpec((1, WH), lambda i: (0, 0))],
        [jax.ShapeDtypeStruct((T, WH), BF16)] * 3 + [jax.ShapeDtypeStruct((1, WH), F32)], [pltpu.VMEM((NH, HD, HD), F32)])
    return pl.pallas_call(
        body, name="hgrn_bwd", grid=(nb,), in_specs=in_specs, out_specs=out_specs, out_shape=out_shape,
        scratch_shapes=scratch, compiler_params=_params(("arbitrary",), has_side_effects=rider is not None),
    )(proj, proj, proj, hgrn_lb, states, do, *extra)


def _log_sigmoid(z):
    return jnp.minimum(z, 0.0) - jnp.log(1.0 + jnp.exp(-jnp.abs(z)))


def _fox_cum(proj, fb_pad):
    T = proj.shape[0]
    tb = _tile(T, (256, 128))

    def body(ff_ref, fb_ref, ct_ref, cq_ref, carry):
        @pl.when(pl.program_id(0) == 0)
        def _():
            carry[...] = jnp.zeros_like(carry)

        lf = _log_sigmoid(ff_ref[...] + fb_ref[...])
        cs = lax.dot_general(_tri(tb, False).astype(F32), lf, NN, precision=HIGHEST,
                             preferred_element_type=F32) + carry[0:1, :]
        carry[0:1, :] = cs[tb - 1:tb, :]
        ct_ref[...] = cs.T[0:8, :]
        for h in range(NH):
            cq_ref[h] = jnp.broadcast_to(cs[:, h:h + 1], (tb, HD))

    return pl.pallas_call(
        body, name="fox_cum", grid=(T // tb,),
        in_specs=[pl.BlockSpec((tb, HD), lambda i: (i, CB_FF)), pl.BlockSpec((1, HD), lambda i: (0, 0))],
        out_specs=[pl.BlockSpec((8, tb), lambda i: (0, i)), pl.BlockSpec((NH, tb, HD), lambda i: (0, i, 0))],
        out_shape=[jax.ShapeDtypeStruct((8, T), F32), jax.ShapeDtypeStruct((NH, T, HD), F32)],
        scratch_shapes=[pltpu.VMEM((8, HD), F32)],
        compiler_params=_params(("arbitrary",)),
    )(proj, fb_pad)


def _fox_cum_bwd(dc, proj, fb_pad):
    T = proj.shape[0]
    tb = _tile(T, (256, 128))
    nb = T // tb

    def body(dc_ref, ff_ref, fb_ref, dff_ref, dfb_ref, carry):
        @pl.when(pl.program_id(0) == 0)
        def _():
            carry[...] = jnp.zeros_like(carry)
            dfb_ref[...] = jnp.zeros_like(dfb_ref)

        rid = lax.broadcasted_iota(jnp.int32, (8, tb), 0)
        m8 = jnp.zeros((8, tb), F32)
        for h in range(NH):
            m8 = m8 + jnp.where(rid == h, dc_ref[h], 0.0)
        dcb = jnp.concatenate([m8, jnp.zeros((HD - 8, tb), F32)], axis=0).T
        rev = lax.dot_general(_tri(tb, True).astype(F32), dcb, NN, precision=HIGHEST,
                              preferred_element_type=F32) + carry[0:1, :]
        carry[0:1, :] = rev[0:1, :]
        dff = rev * jax.nn.sigmoid(-(ff_ref[...] + fb_ref[...]))
        dff_ref[...] = dff.astype(dff_ref.dtype)
        dfb_ref[...] += jnp.sum(dff, axis=0, keepdims=True)

    return pl.pallas_call(
        body, name="fox_cum_bwd", grid=(nb,),
        in_specs=[pl.BlockSpec((NH, 8, tb), lambda i: (0, 0, nb - 1 - i)),
                  pl.BlockSpec((tb, HD), lambda i: (nb - 1 - i, CB_FF)), pl.BlockSpec((1, HD), lambda i: (0, 0))],
        out_specs=[pl.BlockSpec((tb, HD), lambda i: (nb - 1 - i, 0)), pl.BlockSpec((1, HD), lambda i: (0, 0))],
        out_shape=[jax.ShapeDtypeStruct((T, HD), BF16), jax.ShapeDtypeStruct((1, HD), F32)],
        scratch_shapes=[pltpu.VMEM((8, HD), F32)],
        compiler_params=_params(("arbitrary",)),
    )(dc, proj, fb_pad)


STRIP = 128


def _fox_scores(q, k, cq, ck, i, j, bq, bk, r0=0):
    rows = q.shape[0]
    s = _dot(q, k, NT) * SCALE + (cq - ck)
    diff = lax.broadcasted_iota(jnp.int32, (rows, bk), 1) - lax.broadcasted_iota(jnp.int32, (rows, bk), 0)
    return jnp.where(diff <= i * bq + r0 - j * bk, s, NEG)


def _heads(h):
    return slice(h * HD, (h + 1) * HD)


UNDERFLOW = -105.0


def _fox_windows(proj, cq):
    T = proj.shape[0]
    bq = _tile(T, (512, 256, 128))
    nq = T // bq
    assert nq <= HD

    def body(q_ref, k_ref, cq_ref, jlo_ref, ihi_ref, norm_s, cs_s, ce_s):
        i = pl.program_id(0)

        @pl.when(i == 0)
        def _():
            norm_s[...] = jnp.zeros_like(norm_s)
            cs_s[...] = jnp.zeros_like(cs_s)
            ce_s[...] = jnp.zeros_like(ce_s)

        lane = lax.broadcasted_iota(jnp.int32, (1, HD), 1)
        for h in range(NH):
            for row, ref in ((h, q_ref), (8 + h, k_ref)):
                x = ref[:, _heads(h)]
                biggest = jnp.max(jnp.sum(x * x, axis=1, keepdims=True), axis=0, keepdims=True)
                norm_s[row:row + 1, :] = jnp.maximum(norm_s[row:row + 1, :], jnp.broadcast_to(biggest, (1, HD)))
            cs_s[h, pl.ds(i, 1), :] = cq_ref[h, 0:1, :]
            ce_s[h:h + 1, :] = jnp.where(lane == i, cq_ref[h, bq - 1:bq, :], ce_s[h:h + 1, :])

        @pl.when(i == nq - 1)
        def _():
            rows = lax.broadcasted_iota(jnp.int32, (HD, HD), 0)
            cols = lax.broadcasted_iota(jnp.int32, (HD, HD), 1)
            need = cols == rows
            for h in range(NH):
                slack = 2.05 * SCALE * jnp.sqrt(norm_s[h:h + 1, :] * norm_s[8 + h:9 + h, :])
                bound = cs_s[h] - ce_s[h:h + 1, :] + slack
                need = need | ((bound >= UNDERFLOW) & (cols < rows))
            need = need & (rows < nq) & (cols < nq)
            jlo = jnp.min(jnp.where(need, cols, HD).astype(F32), axis=1, keepdims=True)
            ihi = jnp.max(jnp.where(need, rows, -1).astype(F32), axis=0, keepdims=True)
            jlo_ref[...] = jnp.broadcast_to(jlo, (HD, HD)).astype(jnp.int32)
            ihi_ref[...] = jnp.broadcast_to(ihi, (8, HD)).astype(jnp.int32)

    jlo, ihi = pl.pallas_call(
        body, name="fox_windows", grid=(nq,),
        in_specs=[pl.BlockSpec((bq, WH), lambda i: (i, CB_FQ // NH)), pl.BlockSpec((bq, WH), lambda i: (i, CB_FK // NH)),
                  pl.BlockSpec((NH, bq, HD), lambda i: (0, i, 0))],
        out_specs=[pl.BlockSpec((HD, HD), lambda i: (0, 0)), pl.BlockSpec((8, HD), lambda i: (0, 0))],
        out_shape=[jax.ShapeDtypeStruct((HD, HD), jnp.int32), jax.ShapeDtypeStruct((8, HD), jnp.int32)],
        scratch_shapes=[pltpu.VMEM((16, HD), F32), pltpu.VMEM((NH, HD, HD), F32), pltpu.VMEM((8, HD), F32)],
        compiler_params=_params(("arbitrary",)),
    )(proj, proj, cq)
    return jnp.concatenate([jlo[:nq, 0], ihi[0, :nq]])


def _fox_fwd(win, proj, ct, cq):
    T = proj.shape[0]
    bq = bk = _tile(T, (512, 256, 128))
    nq = nk = T // bq

    def body(win_ref, q_ref, k_ref, v_ref, ct_ref, cq_ref, o_ref, lse_ref, m_s, l_s, acc_s):
        i, jj = pl.program_id(0), pl.program_id(1)
        j = win_ref[i] + jj

        @pl.when(jj == 0)
        def _():
            m_s[...] = jnp.full_like(m_s, NEG)
            l_s[...] = jnp.zeros_like(l_s)
            acc_s[...] = jnp.zeros_like(acc_s)

        @pl.when(j <= i)
        def _():
            for h in range(NH):
                hs = _heads(h)
                k, v, ck = k_ref[:, hs], v_ref[:, hs], ct_ref[h:h + 1, :]
                for r0 in range(0, bq, STRIP):
                    rs = slice(r0, r0 + STRIP)
                    s = _fox_scores(q_ref[rs, hs], k, cq_ref[h, rs, 0:1], ck, i, j, bq, bk, r0)
                    m_prev = m_s[h, rs]
                    m_new = jnp.maximum(m_prev, jnp.max(s, axis=1, keepdims=True))
                    alpha = jnp.exp(m_prev - m_new)
                    p = jnp.exp(s - m_new)
                    l_s[h, rs] = alpha * l_s[h, rs] + jnp.sum(p, axis=1, keepdims=True)
                    acc_s[rs, hs] = alpha * acc_s[rs, hs] + _dot(p, v, NN)
                    m_s[h, rs] = m_new

        @pl.when(jj == nk - 1)
        def _():
            for h in range(NH):
                o_ref[:, _heads(h)] = acc_s[:, _heads(h)] / l_s[h]
                lse_ref[h] = jnp.broadcast_to(m_s[h] + jnp.log(l_s[h]), (bq, HD))

    def key_block(i, jj, win):
        return jnp.minimum(win[i] + jj, i)

    def kv(off):
        return pl.BlockSpec((bk, WH), lambda i, jj, win, o=off // NH: (key_block(i, jj, win), o))

    stat = pl.BlockSpec((NH, bq, HD), lambda i, jj, win: (0, i, 0))
    return pl.pallas_call(
        body, name="fox_fwd",
        grid_spec=pltpu.PrefetchScalarGridSpec(
            num_scalar_prefetch=1, grid=(nq, nk),
            in_specs=[pl.BlockSpec((bq, WH), lambda i, jj, win: (i, CB_FQ // NH)), kv(CB_FK), kv(CB_FV),
                      pl.BlockSpec((8, bk), lambda i, jj, win: (0, key_block(i, jj, win))), stat],
            out_specs=[pl.BlockSpec((bq, WH), lambda i, jj, win: (i, 0)), stat],
            scratch_shapes=[pltpu.VMEM((NH, bq, 1), F32), pltpu.VMEM((NH, bq, 1), F32), pltpu.VMEM((bq, WH), F32)]),
        out_shape=[jax.ShapeDtypeStruct((T, WH), F32), jax.ShapeDtypeStruct((NH, T, HD), F32)],
        compiler_params=_params(("parallel", "arbitrary")),
    )(win, proj, proj, proj, ct, cq)


def _fox_bwd_dq(win, proj, ct, cq, lse, do):
    T = proj.shape[0]
    bq = bk = _tile(T, (512, 256, 128))
    nq = nk = T // bq

    def body(win_ref, q_ref, k_ref, v_ref, ct_ref, cq_ref, lse_ref, do_ref, dq_ref, delta_ref, acc_s, delta_s, psum_s):
        i, jj = pl.program_id(0), pl.program_id(1)
        j = win_ref[i] + jj % nk

        @pl.when(jj == 0)
        def _():
            acc_s[...] = jnp.zeros_like(acc_s)
            delta_s[...] = jnp.zeros_like(delta_s)
            psum_s[...] = jnp.zeros_like(psum_s)

        def probs(h):
            hs = _heads(h)
            k = k_ref[:, hs]
            s = _fox_scores(q_ref[:, hs], k, cq_ref[h, :, 0:1], ct_ref[h:h + 1, :], i, j, bq, bk)
            return k, jnp.exp(s - lse_ref[h, :, 0:1]), _dot(do_ref[:, hs], v_ref[:, hs], NT)

        @pl.when((j <= i) & (jj < nk))
        def _():
            for h in range(NH):
                _, p, dp = probs(h)
                delta_s[h] += jnp.sum(p * dp, axis=1, keepdims=True)
                psum_s[h] += jnp.sum(p, axis=1, keepdims=True)

        @pl.when((j <= i) & (jj >= nk))
        def _():
            for h in range(NH):
                k, p, dp = probs(h)
                ds = p * (dp - delta_s[h] / psum_s[h])
                acc_s[:, _heads(h)] += _dot(ds, k, NN) * SCALE

        @pl.when(jj == 2 * nk - 1)
        def _():
            dq_ref[...] = acc_s[...].astype(dq_ref.dtype)
            for h in range(NH):
                delta_ref[h] = jnp.broadcast_to(delta_s[h] / psum_s[h], (bq, HD))

    def key_block(i, jj, win):
        return jnp.minimum(win[i] + jj % nk, i)

    def kv(off):
        return pl.BlockSpec((bk, WH), lambda i, jj, win, o=off // NH: (key_block(i, jj, win), o))

    qrow = pl.BlockSpec((bq, WH), lambda i, jj, win: (i, 0))
    stat = pl.BlockSpec((NH, bq, HD), lambda i, jj, win: (0, i, 0))
    return pl.pallas_call(
        body, name="fox_bwd_dq",
        grid_spec=pltpu.PrefetchScalarGridSpec(
            num_scalar_prefetch=1, grid=(nq, 2 * nk),
            in_specs=[pl.BlockSpec((bq, WH), lambda i, jj, win: (i, CB_FQ // NH)), kv(CB_FK), kv(CB_FV),
                      pl.BlockSpec((8, bk), lambda i, jj, win: (0, key_block(i, jj, win))), stat, stat, qrow],
            out_specs=[qrow, stat],
            scratch_shapes=[pltpu.VMEM((bq, WH), F32), pltpu.VMEM((NH, bq, 1), F32), pltpu.VMEM((NH, bq, 1), F32)]),
        out_shape=[jax.ShapeDtypeStruct((T, WH), BF16), jax.ShapeDtypeStruct((NH, T, HD), F32)],
        compiler_params=_params(("parallel", "arbitrary")),
    )(win, proj, proj, proj, ct, cq, lse, do)


def _fox_bwd_dkv(win, proj, ct, cq, lse, delta, do):
    T = proj.shape[0]
    bq = bk = _tile(T, (512, 256, 128))
    nq = nk = T // bq

    def body(win_ref, q_ref, k_ref, v_ref, ct_ref, cq_ref, lse_ref, delta_ref, do_ref, dk_ref, dv_ref, dc_ref,
             dk_s, dv_s, dc_s):
        j, ii = pl.program_id(0), pl.program_id(1)
        i = j + ii

        @pl.when(ii == 0)
        def _():
            dk_s[...] = jnp.zeros_like(dk_s)
            dv_s[...] = jnp.zeros_like(dv_s)
            dc_s[...] = jnp.zeros_like(dc_s)

        @pl.when(i <= win_ref[nq + j])
        def _():
            for h in range(NH):
                hs = _heads(h)
                q = q_ref[:, hs]
                d_o = do_ref[:, hs]
                s = _fox_scores(q, k_ref[:, hs], cq_ref[h, :, 0:1], ct_ref[h:h + 1, :], i, j, bq, bk)
                p = jnp.exp(s - lse_ref[h, :, 0:1])
                dv_s[:, hs] += _dot(p, d_o, TN)
                dp = _dot(d_o, v_ref[:, hs], NT)
                ds = p * (dp - delta_ref[h, :, 0:1])
                dk_s[:, hs] += _dot(ds, q, TN) * SCALE
                dc_s[h:h + 1, :] -= jnp.sum(ds, axis=0, keepdims=True)

        @pl.when(ii == nq - 1)
        def _():
            dk_ref[...] = dk_s[...].astype(dk_ref.dtype)
            dv_ref[...] = dv_s[...].astype(dv_ref.dtype)
            for h in range(NH):
                dc_ref[h] = jnp.broadcast_to(dc_s[h:h + 1, :], (8, bk))

    def query_block(j, ii, win):
        return jnp.minimum(j + ii, win[nq + j])

    def kv(off):
        return pl.BlockSpec((bk, WH), lambda j, ii, win, o=off // NH: (j, o))

    qrow = pl.BlockSpec((bq, WH), lambda j, ii, win: (query_block(j, ii, win), 0))
    stat = pl.BlockSpec((NH, bq, HD), lambda j, ii, win: (0, query_block(j, ii, win), 0))
    krow = pl.BlockSpec((bk, WH), lambda j, ii, win: (j, 0))
    return pl.pallas_call(
        body, name="fox_bwd_dkv",
        grid_spec=pltpu.PrefetchScalarGridSpec(
            num_scalar_prefetch=1, grid=(nk, nq),
            in_specs=[pl.BlockSpec((bq, WH), lambda j, ii, win: (query_block(j, ii, win), CB_FQ // NH)), kv(CB_FK),
                      kv(CB_FV), pl.BlockSpec((8, bk), lambda j, ii, win: (0, j)), stat, stat, stat, qrow],
            out_specs=[krow, krow, pl.BlockSpec((NH, 8, bk), lambda j, ii, win: (0, 0, j))],
            scratch_shapes=[pltpu.VMEM((bk, WH), F32), pltpu.VMEM((bk, WH), F32), pltpu.VMEM((8, bk), F32)]),
        out_shape=[jax.ShapeDtypeStruct((T, WH), BF16), jax.ShapeDtypeStruct((T, WH), BF16),
                   jax.ShapeDtypeStruct((NH, 8, T), F32)],
        compiler_params=_params(("parallel", "arbitrary")),
    )(win, proj, proj, proj, ct, cq, lse, delta, do)


def _mem_probs(q, mk):
    s = _dot(q, mk, NT) * SCALE
    e = jnp.exp(s - jnp.max(s, axis=1, keepdims=True))
    return e / jnp.sum(e, axis=1, keepdims=True)


def _mem_fwd(proj, mem_kv):
    T = proj.shape[0]
    tr = _tile(T, (512, 256, 128))
    M = mem_kv.shape[0]

    def body(q_ref, mk_ref, mv_ref, o_ref):
        o_ref[...] = _dot(_mem_probs(q_ref[...], mk_ref[...]), mv_ref[...], NN)

    return pl.pallas_call(
        body, name="mem_fwd", grid=(NM, T // tr),
        in_specs=[pl.BlockSpec((tr, HD), lambda h, i: (i, CB_MQ + h)),
                  pl.BlockSpec((M, HD), lambda h, i: (0, h)), pl.BlockSpec((M, HD), lambda h, i: (0, NM + h))],
        out_specs=pl.BlockSpec((tr, HD), lambda h, i: (i, h)),
        out_shape=jax.ShapeDtypeStruct((T, WM), F32),
        compiler_params=_params(("parallel", "parallel")),
    )(proj, mem_kv, mem_kv)


def _mem_bwd(proj, mem_kv, do):
    T = proj.shape[0]
    tr = _tile(T, (512, 256, 128))
    M = mem_kv.shape[0]

    def body(q_ref, mk_ref, mv_ref, do_ref, dq_ref, dmk_ref, dmv_ref):
        @pl.when(pl.program_id(1) == 0)
        def _():
            dmk_ref[...] = jnp.zeros_like(dmk_ref)
            dmv_ref[...] = jnp.zeros_like(dmv_ref)

        q, mk, d_o = q_ref[...], mk_ref[...], do_ref[...]
        p = _mem_probs(q, mk)
        dmv_ref[...] += _dot(p, d_o, TN)
        dp = _dot(d_o, mv_ref[...], NT)
        ds = p * (dp - jnp.sum(p * dp, axis=1, keepdims=True))
        dq_ref[...] = (_dot(ds, mk, NN) * SCALE).astype(dq_ref.dtype)
        dmk_ref[...] += _dot(ds, q, TN) * SCALE

    acc = pl.BlockSpec((M, HD), lambda h, i: (0, h))
    row = pl.BlockSpec((tr, HD), lambda h, i: (i, h))
    return pl.pallas_call(
        body, name="mem_bwd", grid=(NM, T // tr),
        in_specs=[pl.BlockSpec((tr, HD), lambda h, i: (i, CB_MQ + h)),
                  pl.BlockSpec((M, HD), lambda h, i: (0, h)), pl.BlockSpec((M, HD), lambda h, i: (0, NM + h)), row],
        out_specs=[row, acc, acc],
        out_shape=[jax.ShapeDtypeStruct((T, WM), BF16), jax.ShapeDtypeStruct((M, WM), F32),
                   jax.ShapeDtypeStruct((M, WM), F32)],
        compiler_params=_params(("parallel", "arbitrary")),
    )(proj, mem_kv, mem_kv, do)


def _mesh_place():
    x, y, c = lax.axis_index("x"), lax.axis_index("y"), lax.axis_index("c")
    return x, y, c


CHIP_FLIPS = (4, 2, 6)
CHIP_OF_SLOT = (0,) + CHIP_FLIPS


def _peer(x, y, c, k):
    px = 1 - x if k & 4 else x
    py = 1 - y if k & 2 else y
    pc = 1 - c if k & 1 else c
    return (px, py, pc), 4 * px + 2 * py + pc


class _Gather:
    def __init__(self, shapes, pad_rows):
        self.shapes, self.pad_rows, self.n = shapes, pad_rows, len(shapes)
        self.npad = sum(1 for p in pad_rows if p)

    def zeros(self):
        return jnp.zeros((max(self.pad_rows) or 16, self.shapes[0][1]), BF16)

    def out_shape(self):
        return [jax.ShapeDtypeStruct((NDEV * r + p, c), BF16) for (r, c), p in zip(self.shapes, self.pad_rows)]

    def sems(self):
        return [pltpu.SemaphoreType.DMA((self.n, NDEV - 1)), pltpu.SemaphoreType.DMA((self.n, NDEV - 1)),
                pltpu.SemaphoreType.DMA((self.n + self.npad,))]

    def _copies(self, ins, z_ref, outs, send_sems, recv_sems, loc_sems):
        x, y, c = _mesh_place()
        me = 4 * x + 2 * y + c
        sibling, _ = _peer(x, y, c, 1)
        local, first, arrive, forward = [], [], [], []
        ip = 0
        for w in range(self.n):
            r = ins[w].shape[0]
            dst = outs[w].at[pl.ds(pl.multiple_of(me * r, 16), r), :]
            local.append(functools.partial(pltpu.make_async_copy, ins[w], dst, loc_sems.at[w]))
            if self.pad_rows[w]:
                local.append(functools.partial(pltpu.make_async_copy, z_ref.at[pl.ds(0, self.pad_rows[w]), :],
                                               outs[w].at[pl.ds(NDEV * r, self.pad_rows[w]), :], loc_sems.at[self.n + ip]))
                ip += 1

            def remote(src, dst_, s, to):
                return functools.partial(pltpu.make_async_remote_copy, src_ref=src, dst_ref=dst_, send_sem=send_sems.at[w, s],
                                         recv_sem=recv_sems.at[w, s], device_id=to, device_id_type=MESH)

            for s, k in enumerate((1,) + CHIP_FLIPS):
                first.append(remote(ins[w], dst, s, _peer(x, y, c, k)[0]))
            for s, k in enumerate(CHIP_FLIPS):
                _, pidx = _peer(x, y, c, k)
                rows = outs[w].at[pl.ds(pl.multiple_of(pidx * r, 16), r), :]
                arrive.append(remote(rows, rows, 1 + s, sibling))
                forward.append(remote(rows, rows, 4 + s, sibling))
        return local, first, arrive, forward


    def start(self, *refs):
        local, first, _, _ = self._copies(*refs)
        for make in local + first:
            make().start()

    def forward(self, *refs):
        _, _, arrive, forward = self._copies(*refs)
        for a, f in zip(arrive, forward):
            a().wait_recv()
            f().start()

    def finish(self, *refs):
        local, first, _, forward = self._copies(*refs)
        for make in local + first[0::4] + forward:
            make().wait()
        for s in (1, 2, 3):
            for make in first[s::4]:
                make().wait_send()


def _all_gather(shards, pad_rows):
    n = len(shards)
    plan = _Gather([s.shape for s in shards], pad_rows)

    def body(*refs):
        args = (refs[:n], refs[n], refs[n + 1:2 * n + 1]) + tuple(refs[2 * n + 1:])
        plan.start(*args)
        plan.forward(*args)
        plan.finish(*args)

    any_spec = pl.BlockSpec(memory_space=pl.ANY)
    return pl.pallas_call(
        body, name="all_gather_weights",
        in_specs=[any_spec] * (n + 1), out_specs=[any_spec] * n,
        out_shape=plan.out_shape(),
        scratch_shapes=plan.sems(),
        compiler_params=pltpu.CompilerParams(has_side_effects=True),
    )(*shards, plan.zeros())


def _exchange_in_chip(grads, shard_rows, name):
    n = len(grads)
    plan = _InChip([g.shape for g in grads], shard_rows)

    def body(*refs):
        args = (refs[:n], refs[n:2 * n]) + tuple(refs[2 * n:])
        plan.start(*args)
        plan.finish(*args)

    any_spec = pl.BlockSpec(memory_space=pl.ANY)
    return pl.pallas_call(
        body, name=name,
        in_specs=[any_spec] * n, out_specs=[any_spec] * n, out_shape=plan.out_shape(), scratch_shapes=plan.sems(),
        compiler_params=pltpu.CompilerParams(has_side_effects=True),
    )(*grads)


class _InChip:
    def __init__(self, shapes, shard_rows):
        self.shapes, self.rows, self.n, self.ns = shapes, shard_rows, len(shapes), len(CHIP_OF_SLOT)

    def out_shape(self):
        return [jax.ShapeDtypeStruct((self.ns, r, s[1]), BF16) for s, r in zip(self.shapes, self.rows)]

    def sems(self):
        return [pltpu.SemaphoreType.DMA((self.n, self.ns)), pltpu.SemaphoreType.DMA((self.n, self.ns))]

    def _copies(self, ins, theirs, send_sems, recv_sems):
        x, y, c = _mesh_place()
        sibling, _ = _peer(x, y, c, 1)
        copies = []
        for w in range(self.n):
            r = self.rows[w]
            for s, k in enumerate(CHIP_OF_SLOT):
                _, other = _peer(x, y, c, k | 1)
                copies.append(pltpu.make_async_remote_copy(
                    src_ref=ins[w].at[pl.ds(pl.multiple_of(other * r, 16), r), :], dst_ref=theirs[w].at[s],
                    send_sem=send_sems.at[w, s], recv_sem=recv_sems.at[w, s], device_id=sibling, device_id_type=MESH))
        return copies

    def start(self, *refs):
        for cp in self._copies(*refs):
            cp.start()

    def finish(self, *refs):
        for cp in self._copies(*refs):
            cp.wait()


def _pair_sum(grad, theirs, name):
    ns, r, c = theirs.shape
    tr = r if r * c <= 2 * 1024 * 1024 else _tile(r, (256, 128, 64, 32, 16))
    per_block = r // tr

    def body(a_ref, b_ref, o_ref):
        o_ref[...] = (a_ref[...].astype(F32) + b_ref[...].astype(F32)).astype(o_ref.dtype)

    def owner_rows(s, i):
        x, y, c_ = _mesh_place()
        fx, fy = s % 2, s // 2
        px, py = x + fx - 2 * x * fx, y + fy - 2 * y * fy
        return ((4 * px + 2 * py + c_) * per_block + i, 0)

    slot = pl.BlockSpec((None, tr, c), lambda s, i: (s, i, 0))
    return pl.pallas_call(
        body, name=name, grid=(ns, per_block),
        in_specs=[pl.BlockSpec((tr, c), owner_rows), slot], out_specs=slot,
        out_shape=jax.ShapeDtypeStruct((ns, r, c), theirs.dtype),
        compiler_params=_params(("parallel", "parallel")),
    )(grad, theirs)


class _ChipExchange:
    def __init__(self, shapes):
        self.shapes, self.n, self.ns = shapes, len(shapes), len(CHIP_OF_SLOT) - 1

    def out_shape(self):
        return [jax.ShapeDtypeStruct((self.ns,) + tuple(s[1:]), BF16) for s in self.shapes]

    def sems(self):
        return [pltpu.SemaphoreType.DMA((self.n, self.ns)), pltpu.SemaphoreType.DMA((self.n, self.ns))]

    def _copies(self, ins, outs, send_sems, recv_sems):
        x, y, c = _mesh_place()
        copies = []
        for w in range(self.n):
            for s, k in enumerate(CHIP_OF_SLOT[1:]):
                peer, _ = _peer(x, y, c, k)
                copies.append(pltpu.make_async_remote_copy(
                    src_ref=ins[w].at[s + 1], dst_ref=outs[w].at[s], send_sem=send_sems.at[w, s],
                    recv_sem=recv_sems.at[w, s], device_id=peer, device_id_type=MESH))
        return copies

    def start(self, *refs):
        for cp in self._copies(*refs):
            cp.start()

    def finish(self, *refs):
        for cp in self._copies(*refs):
            cp.wait()


def _sum_chips(pair, recv, name):
    ns, r, c = recv.shape
    tr, tc = _panel(r, c)

    def body(p_ref, x_ref, o_ref):
        acc = p_ref[...].astype(F32)
        for s in range(x_ref.shape[0]):
            acc = acc + x_ref[s].astype(F32)
        o_ref[...] = acc

    return pl.pallas_call(
        body, name=name, grid=(r // tr, c // tc),
        in_specs=[pl.BlockSpec((None, tr, tc), lambda i, j: (0, i, j)), pl.BlockSpec((ns, tr, tc), lambda i, j: (0, i, j))],
        out_specs=pl.BlockSpec((tr, tc), lambda i, j: (i, j)),
        out_shape=jax.ShapeDtypeStruct((r, c), F32),
        compiler_params=_params(("parallel", "parallel")),
    )(pair, recv)


def _panel(r, c):
    for tr in (1024, 512, 256, 128):
        if r % tr == 0 and tr * c <= 512 * 1024:
            return tr, c
    for tc in (2048, 1024, 512, 256, 128):
        if c % tc == 0 and r * tc <= 512 * 1024:
            return r, tc
    return _tile(r, (64, 32, 16, 8)), c


def _all_reduce_small(part):
    R, W = part.shape

    def body(x_ref, o_ref, buf, send_sems, recv_sems):
        x, y, c = _mesh_place()
        me = 4 * x + 2 * y + c
        buf[me] = x_ref[...]
        copies = []
        for k in range(1, NDEV):
            peer, _ = _peer(x, y, c, k)
            cp = pltpu.make_async_remote_copy(src_ref=x_ref, dst_ref=buf.at[me], send_sem=send_sems.at[k - 1],
                                              recv_sem=recv_sems.at[k - 1], device_id=peer, device_id_type=MESH)
            cp.start()
            copies.append(cp)
        for cp in copies:
            cp.wait()
        acc = buf[0]
        for d in range(1, NDEV):
            acc = acc + buf[d]
        o_ref[...] = acc

    vm = pl.BlockSpec(memory_space=pltpu.VMEM)
    return pl.pallas_call(
        body, name="all_reduce_small", in_specs=[vm], out_specs=vm,
        out_shape=jax.ShapeDtypeStruct((R, W), F32),
        scratch_shapes=[pltpu.VMEM((NDEV, R, W), F32), pltpu.SemaphoreType.DMA((NDEV - 1,)),
                        pltpu.SemaphoreType.DMA((NDEV - 1,))],
        compiler_params=pltpu.CompilerParams(has_side_effects=True),
    )(part)


def _adam_math(w, g, m, v):
    m2 = ADAM_B1 * m + (1.0 - ADAM_B1) * g
    v2 = ADAM_B2 * v + (1.0 - ADAM_B2) * (g * g)
    m_hat = m2 / (1.0 - ADAM_B1 ** ADAM_STEP)
    v_hat = v2 / (1.0 - ADAM_B2 ** ADAM_STEP)
    delta = -ADAM_LR * (m_hat / (jnp.sqrt(v_hat) + ADAM_EPS) + ADAM_WD * w)
    return delta, m2, v2


def _adamw(w, g, m, v, name):
    r, c = w.shape
    tr, tc = _panel(r, c)

    def body(w_ref, g_ref, m_ref, v_ref, d_ref, m2_ref, v2_ref):
        d_ref[...], m2_ref[...], v2_ref[...] = _adam_math(w_ref[...], g_ref[...], m_ref[...], v_ref[...])

    spec = pl.BlockSpec((tr, tc), lambda i, j: (i, j))
    return pl.pallas_call(
        body, name=name, grid=(r // tr, c // tc), in_specs=[spec] * 4, out_specs=[spec] * 3,
        out_shape=[jax.ShapeDtypeStruct((r, c), F32)] * 3,
        compiler_params=_params(("parallel", "parallel")),
    )(w, g, m, v)


GAINS = ("ffn1_pre", "ffn1_post", "mix_pre", "mix_post", "mem_norm", "ffn2_pre", "ffn2_post")
GAIN_ROWS = D // HD
ROW_LB = len(GAINS) * GAIN_ROWS
ROWS_GRAD_IN = ROW_LB + 24
ROWS_PACKED = ROW_LB + 32


def _small_update(gsum, w_p, m_p, v_p):
    def body(g_ref, w_ref, m_ref, v_ref, go_ref, d_ref, m2_ref, v2_ref):
        a0 = w_ref[ROW_LB:ROW_LB + 8, :]
        a1 = w_ref[ROW_LB + 8:ROW_LB + 16, :]
        mx = jnp.maximum(a0, a1)
        e0, e1 = jnp.exp(a0 - mx), jnp.exp(a1 - mx)
        lb = e0 / (e0 + e1)
        da0 = g_ref[ROW_LB:ROW_LB + 8, :] * lb * (1.0 - lb)
        g = jnp.concatenate([g_ref[0:ROW_LB, :], da0, -da0, g_ref[ROW_LB + 8:ROWS_GRAD_IN, :]], axis=0)
        go_ref[...] = g
        d_ref[...], m2_ref[...], v2_ref[...] = _adam_math(w_ref[...], g, m_ref[...], v_ref[...])

    vm = pl.BlockSpec(memory_space=pltpu.VMEM)
    return pl.pallas_call(
        body, name="small_update", in_specs=[vm] * 4, out_specs=[vm] * 4,
        out_shape=[jax.ShapeDtypeStruct((ROWS_PACKED, HD), F32)] * 4,
    )(gsum, w_p, m_p, v_p)


def _rows8(a):
    a = a.reshape(-1)
    rows = -(-a.shape[0] // HD)
    rows8 = -(-rows // 8) * 8
    return jnp.pad(a, (0, rows8 * HD - a.shape[0])).reshape(rows8, HD)


def _pack_small(gains, lb0, lb1, gnorm, fb):
    return jnp.concatenate([_rows8(g) for g in gains] + [_rows8(lb0), _rows8(lb1), _rows8(gnorm), _rows8(fb)], axis=0)


def _unpack_small(p):
    out = {}
    for i, name in enumerate(GAINS):
        out[name] = p[i * GAIN_ROWS:(i + 1) * GAIN_ROWS].reshape(1, D)
    lb0 = p[ROW_LB:ROW_LB + NH].reshape(1, WH)
    lb1 = p[ROW_LB + 8:ROW_LB + 8 + NH].reshape(1, WH)
    out["hgrn_lb"] = jnp.concatenate([lb0, lb1], axis=0)
    out["hgrn_gnorm"] = p[ROW_LB + 16:ROW_LB + 16 + NH].reshape(1, WH)
    out["fox_fb"] = p[ROW_LB + 24:ROW_LB + 25, 0:NH]
    return out


def _ffn_forward(n, wg_t, wu_t, wd, tag, rider=None, rider_down=None):
    g, u, a, *carried = _ffn_up(n, wg_t, wu_t, f"{tag}_up", rider)
    if wd is None:
        wd = carried[0]
    if rider_down is None:
        h = _mm(a, wd, "nn", F32, f"{tag}_down")
    else:
        h, *more = _mm(a, wd, "nn", F32, f"{tag}_down", rider=rider_down)
        carried = carried + more
    return h, (n, g, u, a), carried


def _mm_out(res):
    return (res[0], list(res[1:])) if isinstance(res, (list, tuple)) else (res, [])


def _ffn_backward(dh, saved, wg_t, wu_t, wd, tag, rider=None, exchange=None, rider_dwd=None, after_dwd=None):
    n, g, u, a = saved
    dwd, got = _mm_out(_mm(a, dh, "tn", BF16, f"{tag}_dwd", rider=rider_dwd))
    rider_dwg = None
    if after_dwd is not None:
        rider, rider_dwg = after_dwd(got)
    dg, du, *carried = _ffn_act_bwd(dh, wd, g, u, f"{tag}_act_bwd", rider)
    dwg, got = _mm_out(_mm(dg, n, "tn", BF16, f"{tag}_dwg", rider=rider_dwg))
    carried = carried + got
    dwu = _mm(du, n, "tn", BF16, f"{tag}_dwu")
    if exchange is None:
        dn = _mm(dg, wg_t, "nn", F32, f"{tag}_dn_g")
        dn = _mm(du, wu_t, "nn", F32, f"{tag}_dn_u", add=dn)
    else:
        ride_a, ride_b, take = exchange(dwg, dwu, dwd)
        dn, got_a = _mm_out(_mm(dg, wg_t, "nn", F32, f"{tag}_dn_g", rider=ride_a))
        dn, got_b = _mm_out(_mm(du, wu_t, "nn", F32, f"{tag}_dn_u", add=dn, rider=ride_b))
        take(got_a, got_b)
    return dn, (dwg, dwu, dwd), carried


GATHER_FIRST = ("ffn1_wg", "ffn1_wu")
GATHER_IN_FFN1_UP = ("ffn1_wd", "w_in")
GATHER_IN_FFN1_DOWN = ("w_gate",)
GATHER_IN_PROJ = ("w_mem_kv", "w_hgrn_out", "w_fox_out", "w_mem_out", "w_o")
GATHER_IN_GATE = ("ffn2_wg",)
GATHER_IN_HGRN = ("ffn2_wu",)
GATHER_IN_FFN2_UP = ("ffn2_wd",)
GROUP_FFN1 = ("ffn1_wg", "ffn1_wu", "ffn1_wd")
GROUP_MIX = ("w_in", "w_mem_kv", "w_hgrn_out", "w_fox_out", "w_mem_out", "w_gate", "w_o")
GROUP_FFN2 = ("ffn2_wg", "ffn2_wu", "ffn2_wd")


def _gather_rider(blocks, names):
    plan = _Gather([blocks[n].shape for n in names], [FFN_PAD.get(n, 0) for n in names])
    return _Rider(plan, [blocks[n] for n in names] + [plan.zeros()], GATHER_STEPS)


def _local_step(x, mem, tgt, small, wts=None, blocks=None):
    T = x.shape[0]
    tr = _tile(T, (256, 128))
    fb_pad = jnp.pad(small["fox_fb"], ((0, 0), (0, HD - NH)))
    dist = blocks is not None
    if dist:
        wts = dict(zip(GATHER_FIRST, _all_gather([blocks[n] for n in GATHER_FIRST], [FFN_PAD[n] for n in GATHER_FIRST])))

    def riding(names):
        return _gather_rider(blocks, names) if dist else None

    (n1,) = _rowwise(_norm_fn, [(x, 0)], [(small["ffn1_pre"], None)], [BF16], "ffn1_pre", tr, D, 1)
    h1, ffn1_saved, carried = _ffn_forward(n1, wts["ffn1_wg"], wts["ffn1_wu"], wts.get("ffn1_wd"), "ffn1",
                                           riding(GATHER_IN_FFN1_UP), riding(GATHER_IN_FFN1_DOWN))
    wts.update(zip(GATHER_IN_FFN1_UP + GATHER_IN_FFN1_DOWN, carried))
    ffn1_out = functools.partial(_post_pre_fn, 0.5)
    x1, un = _rowwise(ffn1_out, [(x, 0), (h1, 0)], [(small["ffn1_post"], None), (small["mix_pre"], None)], [F32, BF16],
                      "ffn1_post_mix_pre", tr, D, 1)
    if dist:
        proj, *carried = _mm(un, wts["w_in"], "nn", F32, "proj", rider=riding(GATHER_IN_PROJ))
        wts.update(zip(GATHER_IN_PROJ, carried))
        z, *carried = _mm(un, wts["w_gate"], "nt", BF16, "gate_logits", rider=riding(GATHER_IN_GATE))
        wts.update(zip(GATHER_IN_GATE, carried))
    else:
        proj = _mm(un, wts["w_in"], "nn", F32, "proj")
        z = _mm(un, wts["w_gate"], "nt", BF16, "gate_logits")
    (memn,) = _rowwise(_norm_fn, [(mem, 0)], [(small["mem_norm"], None)], [BF16], "mem_norm", mem.shape[0], D, 1)
    mem_kv = _mm(memn, wts["w_mem_kv"], "nn", F32, "mem_kv")

    o_raw, states, *carried = _hgrn_fwd(proj, small["hgrn_lb"], riding(GATHER_IN_HGRN))
    wts.update(zip(GATHER_IN_HGRN, carried))
    tr_head = _tile(T, (1024, 512, 256, 128))
    (o_h,) = _rowwise(_hpost_fn, [(o_raw, 0), (proj, CB_HOG)], [(small["hgrn_gnorm"], 0)], [BF16], "hgrn_post",
                      tr_head, HD, NH)
    ct, cq = _fox_cum(proj, fb_pad)
    win = _fox_windows(proj, cq)
    o_f, lse = _fox_fwd(win, proj, ct, cq)
    o_m = _mem_fwd(proj, mem_kv)

    yh = _mm(o_h, wts["w_hgrn_out"], "nt", BF16, "hgrn_out")
    yf = _mm(o_f, wts["w_fox_out"], "nt", BF16, "fox_out")
    ym = _mm(o_m, wts["w_mem_out"], "nt", BF16, "mem_out")
    zc = D // 512
    merge_rows = [(z, 0), (z, zc), (z, 2 * zc), (yh, 0), (yf, 0), (ym, 0)]
    tr_merge = _tile(T, (512, 256, 128))
    (merged,) = _rowwise(_merge_fn, merge_rows, [], [BF16], "merge", tr_merge, 512, zc)
    m = _mm(merged, wts["w_o"], "nn", F32, "mix_out")
    mix_out = functools.partial(_post_pre_fn, 1.0)
    x2, n2 = _rowwise(mix_out, [(x1, 0), (m, 0)], [(small["mix_post"], None), (small["ffn2_pre"], None)], [F32, BF16],
                      "mix_post_ffn2_pre", tr, D, 1)
    h2, ffn2_saved, carried = _ffn_forward(n2, wts["ffn2_wg"], wts["ffn2_wu"], wts.get("ffn2_wd"), "ffn2",
                                           riding(GATHER_IN_FFN2_UP))
    wts.update(zip(GATHER_IN_FFN2_UP, carried))
    dy, loss_part = _loss(x2, h2, small["ffn2_post"], tgt, "loss")

    gw, gs, reduced = {}, {}, {}

    def pair_sums(names, tag):
        if not dist:
            return None, None
        theirs = brought.get(tag)
        if theirs is None:
            theirs = _exchange_in_chip([gw[n] for n in names], [blocks[n].shape[0] for n in names], f"reduce_in_chip_{tag}")
        pairs = [_pair_sum(gw[n], t_, f"pair_{n}") for n, t_ in zip(names, theirs)]
        return pairs, _Rider(_ChipExchange([p.shape for p in pairs]), pairs, EXCHANGE_STEPS)

    def chip_sums(names, pairs, recv):
        for n, p_, r_ in zip(names, pairs or (), recv):
            reduced[n] = _sum_chips(p_, r_, f"sum_{n}")

    brought = {}

    def in_chip_rider(names):
        grads_ = [gw[n] for n in names]
        return _Rider(_InChip([g_.shape for g_ in grads_], [blocks[n].shape[0] for n in names]), grads_, EXCHANGE_STEPS)

    def ffn2_exchange(dwg, dwu, dwd):
        gw.update(ffn2_wg=dwg, ffn2_wu=dwu, ffn2_wd=dwd)
        return in_chip_rider(GROUP_FFN2), None, lambda got_a, got_b: brought.update(ffn2=got_a)

    dh2, gs["ffn2_post"] = _rowwise_bwd(functools.partial(_resid_h_fn, 0.5), [(h2, 0)], [(small["ffn2_post"], None)],
                                        [(dy, 0)], [0], [BF16], "ffn2_post_bwd", tr, D, 1)
    dn2, (gw["ffn2_wg"], gw["ffn2_wu"], gw["ffn2_wd"]), _ = _ffn_backward(
        dh2, ffn2_saved, wts["ffn2_wg"], wts["ffn2_wu"], wts["ffn2_wd"], "ffn2", exchange=ffn2_exchange if dist else None)
    pairs_ffn2, ride_ffn2_grads = pair_sums(GROUP_FFN2, "ffn2")

    dx1, dm, gs["mix_post"], gs["ffn2_pre"] = _rowwise_bwd(
        mix_out, [(x1, 0), (m, 0)], [(small["mix_post"], None), (small["ffn2_pre"], None)], [(dy, 0), (dn2, 0)], [0, 1],
        [F32, BF16], "mix_post_ffn2_pre_bwd", tr, D, 1)
    dmerged = _mm(dm, wts["w_o"], "nt", F32, "d_merged")
    gw["w_o"] = _mm(merged, dm, "tn", BF16, "d_w_o")
    dz0, dz1, dz2, dyh, dyf, dym = _rowwise_bwd(_merge_fn, merge_rows, [], [(dmerged, 0)], [0, 1, 2, 3, 4, 5], [BF16] * 6,
                                                "merge_bwd", tr_merge, 512, zc)
    dz = jnp.concatenate([dz0, dz1, dz2], axis=1)
    gw["w_gate"] = _mm(dz, un, "tn", BF16, "d_w_gate")
    dun = _mm(dz, wts["w_gate"], "nn", F32, "d_un_gate")

    do_h = _mm(dyh, wts["w_hgrn_out"], "nn", F32, "d_o_h")
    gw["w_hgrn_out"] = _mm(dyh, o_h, "tn", BF16, "d_w_hgrn_out")
    do_f = _mm(dyf, wts["w_fox_out"], "nn", F32, "d_o_f")
    gw["w_fox_out"] = _mm(dyf, o_f, "tn", BF16, "d_w_fox_out")
    do_m = _mm(dym, wts["w_mem_out"], "nn", F32, "d_o_m")
    gw["w_mem_out"] = _mm(dym, o_m, "tn", BF16, "d_w_mem_out")

    do_raw, dhog, gs["hgrn_gnorm"] = _rowwise_bwd(_hpost_fn, [(o_raw, 0), (proj, CB_HOG)], [(small["hgrn_gnorm"], 0)],
                                                  [(do_h, 0)], [0, 1], [F32, BF16], "hgrn_post_bwd", tr_head, HD, NH)
    dhq, dhf, dhi, gs["hgrn_lb"], *carried = _hgrn_bwd(proj, small["hgrn_lb"], states, do_raw, ride_ffn2_grads)
    chip_sums(GROUP_FFN2, pairs_ffn2, carried)
    dfq, delta = _fox_bwd_dq(win, proj, ct, cq, lse, do_f)
    dfk, dfv, dc = _fox_bwd_dkv(win, proj, ct, cq, lse, delta, do_f)
    dff, dfb = _fox_cum_bwd(dc, proj, fb_pad)
    gs["fox_fb"] = dfb
    dmq, dmk, dmv = _mem_bwd(proj, mem_kv, do_m)

    dproj = jnp.concatenate([dhq, dhf, dhi, dhog, dfq, dfk, dfv, dff, dmq, jnp.zeros((T, HD), BF16)], axis=1)
    gw["w_in"] = _mm(un, dproj, "tn", BF16, "d_w_in")
    dun = _mm(dproj, wts["w_in"], "nt", F32, "d_un_proj", add=dun)
    dx0, dh1, gs["ffn1_post"], gs["mix_pre"] = _rowwise_bwd(
        ffn1_out, [(x, 0), (h1, 0)], [(small["ffn1_post"], None), (small["mix_pre"], None)], [(dx1, 0), (dun, 0)], [0, 1],
        [F32, BF16], "ffn1_post_mix_pre_bwd", tr, D, 1)

    dmem_kv = jnp.concatenate([dmk, dmv], axis=1)
    gw["w_mem_kv"] = _mm(memn, dmem_kv, "tn", BF16, "d_w_mem_kv")
    dmemn = _mm(dmem_kv, wts["w_mem_kv"], "nt", F32, "d_memn")
    _, gs["mem_norm"] = _rowwise_bwd(_norm_fn, [(mem, 0)], [(small["mem_norm"], None)], [(dmemn, 0)], [0], [BF16],
                                     "mem_norm_bwd", mem.shape[0], D, 1)

    mix = {}

    def mix_after_dwd(got):
        brought.update(mix=got)
        pairs, _ = pair_sums(GROUP_MIX, "mix")
        mix["names"] = GROUP_MIX[1:] + GROUP_MIX[:1]
        mix["pairs"] = pairs[1:] + pairs[:1]
        return tuple(_Rider(_ChipExchange([p.shape for p in part]), part, EXCHANGE_STEPS) for part in (pairs[1:], pairs[:1]))

    def own_exchange(dwg, dwu, dwd):
        gw.update(ffn1_wg=dwg, ffn1_wu=dwu, ffn1_wd=dwd)
        pairs, _ = pair_sums(GROUP_FFN1, "ffn1")
        first, second = pairs[:2], pairs[2:]

        def take(got_a, got_b):
            chip_sums(GROUP_FFN1, pairs, list(got_a) + list(got_b))

        return (_Rider(_ChipExchange([p.shape for p in first]), first, EXCHANGE_STEPS),
                _Rider(_ChipExchange([p.shape for p in second]), second, EXCHANGE_STEPS), take)

    dn1, (gw["ffn1_wg"], gw["ffn1_wu"], gw["ffn1_wd"]), carried = _ffn_backward(
        dh1, ffn1_saved, wts["ffn1_wg"], wts["ffn1_wu"], wts["ffn1_wd"], "ffn1",
        exchange=own_exchange if dist else None, rider_dwd=in_chip_rider(GROUP_MIX) if dist else None,
        after_dwd=mix_after_dwd if dist else None)
    chip_sums(mix.get("names", ()), mix.get("pairs"), carried)
    dx, gs["ffn1_pre"] = _rowwise_bwd(_norm_res_fn, [(x, 0)], [(small["ffn1_pre"], None)], [(dx0, 0), (dn1, 0)], [0], [F32],
                                      "ffn1_pre_bwd", tr, D, 1)
    return loss_part, dx, (reduced if dist else gw), gs


BIG = ("ffn1_wg", "ffn1_wu", "ffn1_wd", "w_in", "w_mem_kv", "w_hgrn_out", "w_fox_out", "w_mem_out", "w_gate", "w_o",
       "ffn2_wg", "ffn2_wu", "ffn2_wd")
TRANSPOSED = ("ffn1_wg", "ffn1_wu", "ffn2_wg", "ffn2_wu", "w_hgrn_out", "w_fox_out", "w_mem_out", "w_gate")
FFN_PAD = {"ffn1_wg": FP - F, "ffn1_wu": FP - F, "ffn1_wd": FP - F, "ffn2_wg": FP - F, "ffn2_wu": FP - F,
           "ffn2_wd": FP - F}
SMALL = GAINS + ("hgrn_lb", "hgrn_gnorm", "fox_fb")
WEIGHTS = ("ffn1_pre", "ffn1_post", "ffn1_wg", "ffn1_wu", "ffn1_wd", "mix_pre", "mix_post", "mem_norm", "w_in", "hgrn_lb",
           "hgrn_gnorm", "fox_fb", "w_mem_kv", "w_hgrn_out", "w_fox_out", "w_mem_out", "w_gate", "w_o", "ffn2_pre",
           "ffn2_post", "ffn2_wg", "ffn2_wu", "ffn2_wd")


def _to_gather_layout(name, w):
    if name in TRANSPOSED:
        w = w.T
    if name == "w_in":
        r = w.shape[0]
        w = jnp.concatenate([w[:, :MQ_COL], jnp.zeros((r, FF_COL + HD - MQ_COL), w.dtype), w[:, MQ_COL:],
                             jnp.zeros((r, P - FF_COL - HD - WM), w.dtype)], axis=1)
    return w.astype(BF16)


def _from_gather_layout(name, g):
    if name == "w_in":
        g = jnp.concatenate([g[:, :MQ_COL], g[:, FF_COL + HD:FF_COL + HD + WM]], axis=1)
    if name in TRANSPOSED:
        g = g.T
    return g


def kernel(x, mem, ffn1_pre, ffn1_post, ffn1_wg, ffn1_wu, ffn1_wd, mix_pre, mix_post, mem_norm, w_in, hgrn_lb, hgrn_gnorm, fox_fb, w_mem_kv, w_hgrn_out, w_fox_out, w_mem_out, w_gate, w_o, ffn2_pre, ffn2_post, ffn2_wg, ffn2_wu, ffn2_wd, loss_target, m_ffn1_pre, m_ffn1_post, m_ffn1_wg, m_ffn1_wu, m_ffn1_wd, m_mix_pre, m_mix_post, m_mem_norm, m_w_in, m_hgrn_lb, m_hgrn_gnorm, m_fox_fb, m_w_mem_kv, m_w_hgrn_out, m_w_fox_out, m_w_mem_out, m_w_gate, m_w_o, m_ffn2_pre, m_ffn2_post, m_ffn2_wg, m_ffn2_wu, m_ffn2_wd, v_ffn1_pre, v_ffn1_post, v_ffn1_wg, v_ffn1_wu, v_ffn1_wd, v_mix_pre, v_mix_post, v_mem_norm, v_w_in, v_hgrn_lb, v_hgrn_gnorm, v_fox_fb, v_w_mem_kv, v_w_hgrn_out, v_w_fox_out, v_w_mem_out, v_w_gate, v_w_o, v_ffn2_pre, v_ffn2_post, v_ffn2_wg, v_ffn2_wu, v_ffn2_wd):
    a = dict(locals())
    small = {n: a[n] for n in SMALL}
    shard = {n: a[n][0] if a[n].ndim == 3 else a[n] for n in BIG}

    blocks = {n: _to_gather_layout(n, shard[n]) for n in BIG}
    loss_part, dx, reduced, gs = _local_step(x[0], mem[0], loss_target[0], small, blocks=blocks)
    loss = lax.psum(0.5 / D * jnp.sum(loss_part), ("x", "y", "c"))

    grads, deltas, new_m, new_v = {}, {}, {}, {}
    for n in BIG:
        g = _from_gather_layout(n, reduced[n])
        d, m2, v2 = _adamw(shard[n], g, a["m_" + n].reshape(g.shape), a["v_" + n].reshape(g.shape), f"adamw_{n}")
        full = a[n].shape
        grads[n], deltas[n], new_m[n], new_v[n] = g.reshape(full), d.reshape(full), m2.reshape(full), v2.reshape(full)

    part = jnp.concatenate([_rows8(gs[n]) for n in GAINS] + [_rows8(gs["hgrn_lb"]), _rows8(gs["hgrn_gnorm"]),
                                                             _rows8(gs["fox_fb"][:, :NH])], axis=0)
    gsum = _all_reduce_small(part)

    def packed(prefix):
        lb = a[prefix + "hgrn_lb"]
        return _pack_small([a[prefix + n] for n in GAINS], lb[0], lb[1], a[prefix + "hgrn_gnorm"], a[prefix + "fox_fb"])

    g_p, d_p, m_p, v_p = _small_update(gsum, packed(""), packed("m_"), packed("v_"))
    for dst, p in ((grads, g_p), (deltas, d_p), (new_m, m_p), (new_v, v_p)):
        dst.update(_unpack_small(p))

    return (loss, dx[None], *[grads[n] for n in WEIGHTS], *[deltas[n] for n in WEIGHTS],
            *[new_m[n] for n in WEIGHTS], *[new_v[n] for n in WEIGHTS])
```

```python
import functools

import jax
import jax.numpy as jnp
from jax import lax
from jax.experimental import pallas as pl
from jax.experimental.pallas import tpu as pltpu

F32 = jnp.float32
BF16 = jnp.bfloat16
HIGHEST = lax.Precision.HIGHEST

NDEV = 8
D = 2048
F = 5504
FP = 5632
HD = 128
NH = 6
NM = 4
WH = NH * HD
WM = NM * HD
P = 6144
FF_COL = 5376
MQ_COL = 5382
CHUNK = 64
EPS = 1e-6
SCALE = HD ** -0.5
NEG = -1e30
VMEM_LIMIT = 48 * 1024 * 1024

CB_HQ, CB_HF, CB_HI, CB_HOG, CB_FQ, CB_FK, CB_FV, CB_FF, CB_MQ = 0, 6, 12, 18, 24, 30, 36, 42, 43

ADAM_LR, ADAM_B1, ADAM_B2, ADAM_EPS, ADAM_WD, ADAM_STEP = 0.001, 0.9, 0.999, 1e-08, 0.01, 10

NT = (((1,), (1,)), ((), ()))
NN = (((1,), (0,)), ((), ()))
TN = (((0,), (0,)), ((), ()))
MESH = pl.DeviceIdType.MESH


def _params(sem=None, **kw):
    return pltpu.CompilerParams(dimension_semantics=sem, vmem_limit_bytes=VMEM_LIMIT, **kw)


def _tile(n, prefs):
    for p in prefs:
        if p <= n and n % p == 0:
            return p
    return n


def _dot(a, b, dims):
    return lax.dot_general(a.astype(BF16), b.astype(BF16), dims, preferred_element_type=F32)


def _mm(a, b, mode, out_dtype, name, add=None, rider=None):
    if mode == "nn":
        (M, K), (K2, N) = a.shape, b.shape
    elif mode == "nt":
        (M, K), (N, K2) = a.shape, b.shape
    else:
        (K, M), (K2, N) = a.shape, b.shape
    assert K == K2, (a.shape, b.shape, mode)
    if mode == "tn":
        tm = _tile(M, (512, 256, 128))
        tn = _tile(N, (1024, 768, 512, 256, 128))
        tk = _tile(K, (4096, 2048, 1024, 512, 256, 128))
    else:
        tm = _tile(M, (1024, 512, 256, 128)) if K <= 2048 else _tile(M, (512, 256, 128))
        tn = _tile(N, (512, 768, 256, 128))
        tk = K if K <= 6144 else _tile(K, (2048, 1024, 512, 256, 128))
    nk = K // tk
    dims = {"nn": NN, "nt": NT, "tn": TN}[mode]
    has_add = add is not None

    ni, nj = M // tm, N // tn
    n_in = 3 if has_add else 2

    def body(*refs):
        step = (pl.program_id(0) * nj + pl.program_id(1)) * nk + pl.program_id(2)
        refs = _carry(rider, refs, n_in, 1, 1, step, ni * nj * nk)
        a_ref, b_ref = refs[0], refs[1]
        c_ref = refs[2] if has_add else None
        o_ref = refs[3] if has_add else refs[2]
        acc_ref = refs[-1]
        k = pl.program_id(2)
        part = _dot(a_ref[...], b_ref[...], dims)

        def finish(r):
            if has_add:
                r = r + c_ref[...].astype(F32)
            o_ref[...] = r.astype(o_ref.dtype)

        if nk == 1:
            finish(part)
        else:
            @pl.when(k == 0)
            def _():
                acc_ref[...] = part

            @pl.when(k > 0)
            def _():
                acc_ref[...] += part

            @pl.when(k == nk - 1)
            def _():
                finish(acc_ref[...])

    if mode == "nn":
        a_spec = pl.BlockSpec((tm, tk), lambda i, j, k: (i, k))
        b_spec = pl.BlockSpec((tk, tn), lambda i, j, k: (k, j))
    elif mode == "nt":
        a_spec = pl.BlockSpec((tm, tk), lambda i, j, k: (i, k))
        b_spec = pl.BlockSpec((tn, tk), lambda i, j, k: (j, k))
    else:
        a_spec = pl.BlockSpec((tk, tm), lambda i, j, k: (k, i))
        b_spec = pl.BlockSpec((tk, tn), lambda i, j, k: (k, j))
    o_spec = pl.BlockSpec((tm, tn), lambda i, j, k: (i, j))
    args = (a, b) + ((add,) if has_add else ())
    in_specs, out_specs, out_shape, scratch, extra = _with_rider(
        rider, [a_spec, b_spec] + ([o_spec] if has_add else []), [o_spec], [jax.ShapeDtypeStruct((M, N), out_dtype)],
        [pltpu.VMEM((tm, tn) if nk > 1 else (8, 128), F32)])
    out = pl.pallas_call(
        body, name=name, grid=(ni, nj, nk), in_specs=in_specs, out_specs=out_specs, out_shape=out_shape,
        scratch_shapes=scratch,
        compiler_params=_params(("arbitrary",) * 3 if rider else ("parallel", "parallel", "arbitrary"),
                                has_side_effects=rider is not None),
    )(*args, *extra)
    return out if rider else out[0]


class _Rider:
    def __init__(self, plan, inputs, steps):
        self.plan, self.inputs, self.steps = plan, list(inputs), steps
        self.n_out = len(plan.out_shape())
        self.n_sem = len(plan.sems())

    def run(self, step, total, in_refs, out_refs, sem_refs):
        n = self.plan.n
        if isinstance(self.plan, _Gather):
            args = (in_refs[:n], in_refs[n], out_refs) + tuple(sem_refs)
        else:
            args = (in_refs, out_refs) + tuple(sem_refs)
        for frac, method in self.steps:
            @pl.when(step == int(frac * (total - 1)))
            def _(method=method):
                getattr(self.plan, method)(*args)


GATHER_STEPS = ((0.0, "start"), (0.6, "forward"), (1.0, "finish"))
EXCHANGE_STEPS = ((0.0, "start"), (1.0, "finish"))


def _carry(rider, refs, n_in, n_out, n_scratch, step, total):
    if rider is None:
        return refs
    ri, ro, rs = len(rider.inputs), rider.n_out, rider.n_sem
    own_in, rid_in = refs[:n_in], refs[n_in:n_in + ri]
    own_out, rid_out = refs[n_in + ri:n_in + ri + n_out], refs[n_in + ri + n_out:n_in + ri + n_out + ro]
    own_scr, rid_sem = refs[n_in + ri + n_out + ro:n_in + ri + n_out + ro + n_scratch], refs[len(refs) - rs:]
    rider.run(step, total, rid_in, rid_out, rid_sem)
    return tuple(own_in) + tuple(own_out) + tuple(own_scr)


def _with_rider(rider, in_specs, out_specs, out_shape, scratch):
    if rider is None:
        return in_specs, out_specs, out_shape, scratch, ()
    any_spec = pl.BlockSpec(memory_space=pl.ANY)
    return (list(in_specs) + [any_spec] * len(rider.inputs), list(out_specs) + [any_spec] * rider.n_out,
            list(out_shape) + rider.plan.out_shape(), list(scratch) + rider.plan.sems(), tuple(rider.inputs))


def _ffn_up(n, wg_t, wu_t, name, rider=None):
    T = n.shape[0]
    tm = _tile(T, (1024, 512, 256, 128))
    tn = 512
    ni, nj = T // tm, FP // tn

    def body(*refs):
        step = pl.program_id(0) * nj + pl.program_id(1)
        n_ref, wg_ref, wu_ref, g_ref, u_ref, a_ref = _carry(rider, refs, 3, 3, 0, step, ni * nj)
        x = n_ref[...]
        g = _dot(x, wg_ref[...], NT)
        u = _dot(x, wu_ref[...], NT)
        g_ref[...] = g.astype(g_ref.dtype)
        u_ref[...] = u.astype(u_ref.dtype)
        a_ref[...] = (g * jax.nn.sigmoid(g) * u).astype(BF16)

    w_spec = pl.BlockSpec((tn, D), lambda i, j: (j, 0))
    o_spec = pl.BlockSpec((tm, tn), lambda i, j: (i, j))
    in_specs, out_specs, out_shape, scratch, extra = _with_rider(
        rider, [pl.BlockSpec((tm, D), lambda i, j: (i, 0)), w_spec, w_spec], [o_spec, o_spec, o_spec],
        [jax.ShapeDtypeStruct((T, FP), BF16)] * 3, [])
    return pl.pallas_call(
        body, name=name, grid=(ni, nj), in_specs=in_specs, out_specs=out_specs, out_shape=out_shape,
        scratch_shapes=scratch,
        compiler_params=_params(("arbitrary", "arbitrary") if rider else ("parallel", "parallel"),
                                has_side_effects=rider is not None),
    )(n, wg_t, wu_t, *extra)


def _ffn_act_bwd(dh, wd, g, u, name, rider=None):
    T = dh.shape[0]
    tm = _tile(T, (1024, 512, 256, 128))
    tn = 512
    ni, nj = T // tm, FP // tn

    def body(*refs):
        step = pl.program_id(0) * nj + pl.program_id(1)
        dh_ref, wd_ref, g_ref, u_ref, dg_ref, du_ref = _carry(rider, refs, 4, 2, 0, step, ni * nj)
        da = _dot(dh_ref[...], wd_ref[...], NT)
        g = g_ref[...].astype(F32)
        sg = jax.nn.sigmoid(g)
        dg_ref[...] = (da * u_ref[...].astype(F32) * (sg * (1.0 + g * (1.0 - sg)))).astype(dg_ref.dtype)
        du_ref[...] = (da * (g * sg)).astype(du_ref.dtype)

    tile = pl.BlockSpec((tm, tn), lambda i, j: (i, j))
    in_specs, out_specs, out_shape, scratch, extra = _with_rider(
        rider, [pl.BlockSpec((tm, D), lambda i, j: (i, 0)), pl.BlockSpec((tn, D), lambda i, j: (j, 0)), tile, tile],
        [tile, tile], [jax.ShapeDtypeStruct((T, FP), BF16), jax.ShapeDtypeStruct((T, FP), BF16)], [])
    return pl.pallas_call(
        body, name=name, grid=(ni, nj), in_specs=in_specs, out_specs=out_specs, out_shape=out_shape,
        scratch_shapes=scratch,
        compiler_params=_params(("arbitrary", "arbitrary") if rider else ("parallel", "parallel"),
                                has_side_effects=rider is not None),
    )(dh, wd, g, u, *extra)


def _row_specs(rows, tr, cw):
    return [pl.BlockSpec((tr, cw), lambda j, i, o=off: (i, o + j)) for _, off in rows]


def _const_specs(consts, cw):
    specs = []
    for arr, off in consts:
        if off is None:
            specs.append(pl.BlockSpec(arr.shape, lambda j, i: (0, 0)))
        else:
            specs.append(pl.BlockSpec((arr.shape[0], cw), lambda j, i, o=off: (0, o + j)))
    return specs


def _rowwise(fn, rows, consts, out_dtypes, name, tr, cw, ncol):
    T = rows[0][0].shape[0]
    nr, nc = len(rows), len(consts)

    def body(*refs):
        r = [x[...].astype(F32) for x in refs[:nr]]
        c = [x[...] for x in refs[nr:nr + nc]]
        res = fn(*r, *c)
        for o_ref, v in zip(refs[nr + nc:], res):
            o_ref[...] = v.astype(o_ref.dtype)

    o_spec = pl.BlockSpec((tr, cw), lambda j, i: (i, j))
    return pl.pallas_call(
        body, name=name, grid=(ncol, T // tr),
        in_specs=_row_specs(rows, tr, cw) + _const_specs(consts, cw),
        out_specs=[o_spec] * len(out_dtypes),
        out_shape=[jax.ShapeDtypeStruct((T, ncol * cw), dt) for dt in out_dtypes],
        compiler_params=_params(("parallel", "parallel")),
    )(*[a for a, _ in rows], *[a for a, _ in consts])


def _rowwise_bwd(fn, rows, consts, cots, diff, ddtypes, name, tr, cw, ncol):
    T = rows[0][0].shape[0]
    nr, nc, nt, nd = len(rows), len(consts), len(cots), len(diff)

    def body(*refs):
        r = [x[...].astype(F32) for x in refs[:nr]]
        c = [x[...] for x in refs[nr:nr + nc]]
        ct = [x[...].astype(F32) for x in refs[nr + nc:nr + nc + nt]]
        drow_refs = refs[nr + nc + nt:nr + nc + nt + nd]
        dconst_refs = refs[nr + nc + nt + nd:]
        i = pl.program_id(1)

        def f(*args):
            full = list(r)
            for idx, a in zip(diff, args[:nd]):
                full[idx] = a
            return tuple(fn(*full, *args[nd:]))

        _, vjp = jax.vjp(f, *[r[d] for d in diff], *c)
        g = vjp(tuple(ct))
        for o_ref, v in zip(drow_refs, g[:nd]):
            o_ref[...] = v.astype(o_ref.dtype)

        @pl.when(i == 0)
        def _():
            for o_ref in dconst_refs:
                o_ref[...] = jnp.zeros_like(o_ref)

        for o_ref, v in zip(dconst_refs, g[nd:]):
            o_ref[...] += v

    o_spec = pl.BlockSpec((tr, cw), lambda j, i: (i, j))
    out_shape = [jax.ShapeDtypeStruct((T, ncol * cw), dt) for dt in ddtypes]
    out_shape += [jax.ShapeDtypeStruct(a.shape, F32) for a, _ in consts]
    return pl.pallas_call(
        body, name=name, grid=(ncol, T // tr),
        in_specs=_row_specs(rows, tr, cw) + _const_specs(consts, cw) + _row_specs(cots, tr, cw),
        out_specs=[o_spec] * nd + _const_specs(consts, cw),
        out_shape=out_shape,
        compiler_params=_params(("parallel", "arbitrary")),
    )(*[a for a, _ in rows], *[a for a, _ in consts], *[a for a, _ in cots])


def _rms(x, g):
    return x * lax.rsqrt(jnp.mean(x * x, axis=-1, keepdims=True) + EPS) * g


def _silu(x):
    return x * jax.nn.sigmoid(x)


def _norm_fn(x, g):
    return (_rms(x, g),)


def _norm_res_fn(x, g):
    return (x, _rms(x, g))


def _post_pre_fn(scale, x, h, g_post, g_pre):
    xn = x + scale * _rms(h, g_post)
    return (xn, _rms(xn, g_pre))


def _resid_h_fn(scale, h, g):
    return (scale * _rms(h, g),)


def _hpost_fn(o, hog, gn):
    return (_rms(o, gn) * _silu(hog),)


def _merge_fn(z0, z1, z2, yh, yf, ym):
    return (jax.nn.sigmoid(z0) * yh + jax.nn.sigmoid(z1) * yf + jax.nn.sigmoid(z2) * ym,)


def _loss(x2, h, g_post, tgt, name):
    T = x2.shape[0]
    tr = _tile(T, (256, 128))

    def body(x_ref, h_ref, g_ref, t_ref, dy_ref, s_ref):
        i = pl.program_id(0)
        e = x_ref[...] + 0.5 * _rms(h_ref[...], g_ref[...]) - t_ref[...]
        dy_ref[...] = e * (1.0 / D)
        col = jnp.sum(e * e, axis=0, keepdims=True)
        tot = col[:, 0:HD]
        for k in range(1, D // HD):
            tot = tot + col[:, k * HD:(k + 1) * HD]

        @pl.when(i == 0)
        def _():
            s_ref[...] = jnp.zeros_like(s_ref)

        s_ref[...] += tot

    spec = pl.BlockSpec((tr, D), lambda i: (i, 0))
    return pl.pallas_call(
        body, name=name, grid=(T // tr,), in_specs=[spec, spec, pl.BlockSpec((1, D), lambda i: (0, 0)), spec],
        out_specs=[spec, pl.BlockSpec((1, HD), lambda i: (0, 0))],
        out_shape=[jax.ShapeDtypeStruct((T, D), F32), jax.ShapeDtypeStruct((1, HD), F32)],
        compiler_params=_params(("arbitrary",)),
    )(x2, h, g_post, tgt)


def _lower_bound(lb_ref):
    a0 = lb_ref[0:1, :]
    a1 = lb_ref[1:2, :]
    mx = jnp.maximum(a0, a1)
    e0 = jnp.exp(a0 - mx)
    return e0 / (e0 + jnp.exp(a1 - mx))


def _hgrn_prep(hq, hf, lb):
    g = lb + (1.0 - lb) * jax.nn.sigmoid(hf)
    return _silu(hq), 1.0 - g, jnp.log(g)


def _tri(n, upper):
    r = lax.broadcasted_iota(jnp.int32, (n, n), 0)
    c = lax.broadcasted_iota(jnp.int32, (n, n), 1)
    return (c >= r) if upper else (c <= r)


def _hgrn_factors(q, k, gl):
    low = _tri(CHUNK, False)
    b = lax.dot_general(low.astype(F32), gl, NN, precision=HIGHEST, preferred_element_type=F32)
    bl = b[CHUNK - 1:CHUNK, :]
    ref = b[CHUNK // 2 - 1:CHUNK // 2, :]
    eb = jnp.exp(b)
    ea = jnp.exp(b - ref)
    ebn = jnp.exp(ref - b)
    ek = jnp.exp(bl - b)
    ebl = jnp.exp(bl)
    return low, eb, ea, ebn, ek, ebl


def _hgrn_fwd(proj, hgrn_lb, rider=None):
    T = proj.shape[0]
    cb = _tile(T, (512, 256, 128, 64))
    nchunk = cb // CHUNK

    def body(*refs):
        hq_ref, hf_ref, hi_ref, lb_ref, o_ref, st_ref, state = _carry(rider, refs, 4, 2, 1, pl.program_id(0), T // cb)

        @pl.when(pl.program_id(0) == 0)
        def _():
            state[...] = jnp.zeros_like(state)

        lb = _lower_bound(lb_ref)

        def chunk(c, carry):
            r0 = pl.multiple_of(c * CHUNK, CHUNK)
            for h in range(NH):
                cols = slice(h * HD, (h + 1) * HD)
                q, k, gl = _hgrn_prep(hq_ref[pl.ds(r0, CHUNK), cols], hf_ref[pl.ds(r0, CHUNK), cols], lb[:, cols])
                v = hi_ref[pl.ds(r0, CHUNK), cols]
                low, eb, ea, ebn, ek, ebl = _hgrn_factors(q, k, gl)
                s_t = state[h]
                st_ref[c, h] = s_t
                pm = jnp.where(low, _dot(q * ea, k * ebn, NT), 0.0)
                o_ref[pl.ds(r0, CHUNK), cols] = _dot(q * eb, s_t, NT) + _dot(pm, v, NN)
                state[h] = s_t * ebl + _dot(v, k * ek, TN)
            return carry

        lax.fori_loop(0, nchunk, chunk, 0)

    def col(off):
        return pl.BlockSpec((cb, WH), lambda i, o=off: (i, o))

    in_specs, out_specs, out_shape, scratch, extra = _with_rider(
        rider, [col(0), col(1), col(2), pl.BlockSpec((2, WH), lambda i: (0, 0))],
        [pl.BlockSpec((cb, WH), lambda i: (i, 0)), pl.BlockSpec((nchunk, NH, HD, HD), lambda i: (i, 0, 0, 0))],
        [jax.ShapeDtypeStruct((T, WH), F32), jax.ShapeDtypeStruct((T // CHUNK, NH, HD, HD), F32)],
        [pltpu.VMEM((NH, HD, HD), F32)])
    return pl.pallas_call(
        body, name="hgrn_fwd", grid=(T // cb,), in_specs=in_specs, out_specs=out_specs, out_shape=out_shape,
        scratch_shapes=scratch, compiler_params=_params(("arbitrary",), has_side_effects=rider is not None),
    )(proj, proj, proj, hgrn_lb, *extra)


def _hgrn_bwd(proj, hgrn_lb, states, do, rider=None):
    T = proj.shape[0]
    cb = _tile(T, (512, 256, 128, 64))
    nchunk = cb // CHUNK
    nb = T // cb

    def body(*refs):
        (hq_ref, hf_ref, hi_ref, lb_ref, st_ref, do_ref, dhq_ref, dhf_ref, dhi_ref, dlb_ref,
         dstate) = _carry(rider, refs, 6, 4, 1, pl.program_id(0), nb)

        @pl.when(pl.program_id(0) == 0)
        def _():
            dstate[...] = jnp.zeros_like(dstate)
            dlb_ref[...] = jnp.zeros_like(dlb_ref)

        lb = _lower_bound(lb_ref)
        up = _tri(CHUNK, True)
        last = lax.broadcasted_iota(jnp.int32, (CHUNK, HD), 0) == CHUNK - 1

        def chunk(cc, carry):
            c = nchunk - 1 - cc
            r0 = pl.multiple_of(c * CHUNK, CHUNK)
            for h in range(NH):
                cols = slice(h * HD, (h + 1) * HD)
                hq = hq_ref[pl.ds(r0, CHUNK), cols]
                hf = hf_ref[pl.ds(r0, CHUNK), cols]
                (q, k, gl), prep_vjp = jax.vjp(_hgrn_prep, hq, hf, lb[:, cols])
                v = hi_ref[pl.ds(r0, CHUNK), cols]
                d_o = do_ref[pl.ds(r0, CHUNK), cols]
                low, eb, ea, ebn, ek, ebl = _hgrn_factors(q, k, gl)
                s_t = st_ref[c, h]
                ds_new = dstate[h]
                qe, am, bm, kb = q * eb, q * ea, k * ebn, k * ek
                pm_t = jnp.where(up, _dot(bm, am, NT), 0.0)
                dp = jnp.where(low, _dot(d_o, v, NT), 0.0)
                dp_t = jnp.where(up, _dot(v, d_o, NT), 0.0)
                dqe = _dot(d_o, s_t, NN)
                da = _dot(dp, bm, NN)
                db_m = _dot(dp_t, am, NN)
                dkb = _dot(v, ds_new, NN)
                dv = _dot(pm_t, d_o, NN) + _dot(kb, ds_new, NT)
                dq = dqe * eb + da * ea
                dk = db_m * ebn + dkb * ek
                dbl = jnp.sum(dkb * kb, axis=0, keepdims=True) + jnp.sum(ds_new * s_t, axis=0, keepdims=True) * ebl
                db = (dqe * qe + da * am.astype(BF16).astype(F32) - db_m * bm.astype(BF16).astype(F32) - dkb * kb
                      + jnp.where(last, dbl, 0.0))
                dgl = lax.dot_general(up.astype(F32), db, NN, precision=HIGHEST, preferred_element_type=F32)
                dhq, dhf, dlb = prep_vjp((dq, dk, dgl))
                dhq_ref[pl.ds(r0, CHUNK), cols] = dhq.astype(dhq_ref.dtype)
                dhf_ref[pl.ds(r0, CHUNK), cols] = dhf.astype(dhf_ref.dtype)
                dhi_ref[pl.ds(r0, CHUNK), cols] = dv.astype(dhi_ref.dtype)
                dlb_ref[:, cols] += dlb
                dstate[h] = _dot(d_o, qe, TN) + ds_new * ebl
            return carry

        lax.fori_loop(0, nchunk, chunk, 0)

    def col(off):
        return pl.BlockSpec((cb, WH), lambda i, o=off: (nb - 1 - i, o))

    row = pl.BlockSpec((cb, WH), lambda i: (nb - 1 - i, 0))
    in_specs, out_specs, out_shape, scratch, extra = _with_rider(
        rider, [col(0), col(1), col(2), pl.BlockSpec((2, WH), lambda i: (0, 0)),
                pl.BlockSpec((nchunk, NH, HD, HD), lambda i: (nb - 1 - i, 0, 0, 0)), row],
        [row, row, row, pl.BlockSpec((1, WH), lambda i: (0, 0))],
        [jax.ShapeDtypeStruct((T, WH), BF16)] * 3 + [jax.ShapeDtypeStruct((1, WH), F32)], [pltpu.VMEM((NH, HD, HD), F32)])
    return pl.pallas_call(
        body, name="hgrn_bwd", grid=(nb,), in_specs=in_specs, out_specs=out_specs, out_shape=out_shape,
        scratch_shapes=scratch, compiler_params=_params(("arbitrary",), has_side_effects=rider is not None),
    )(proj, proj, proj, hgrn_lb, states, do, *extra)


def _log_sigmoid(z):
    return jnp.minimum(z, 0.0) - jnp.log(1.0 + jnp.exp(-jnp.abs(z)))


def _fox_cum(proj, fb_pad):
    T = proj.shape[0]
    tb = _tile(T, (256, 128))

    def body(ff_ref, fb_ref, ct_ref, cq_ref, carry):
        @pl.when(pl.program_id(0) == 0)
        def _():
            carry[...] = jnp.zeros_like(carry)

        lf = _log_sigmoid(ff_ref[...] + fb_ref[...])
        cs = lax.dot_general(_tri(tb, False).astype(F32), lf, NN, precision=HIGHEST,
                             preferred_element_type=F32) + carry[0:1, :]
        carry[0:1, :] = cs[tb - 1:tb, :]
        ct_ref[...] = cs.T[0:8, :]
        for h in range(NH):
            cq_ref[h] = jnp.broadcast_to(cs[:, h:h + 1], (tb, HD))

    return pl.pallas_call(
        body, name="fox_cum", grid=(T // tb,),
        in_specs=[pl.BlockSpec((tb, HD), lambda i: (i, CB_FF)), pl.BlockSpec((1, HD), lambda i: (0, 0))],
        out_specs=[pl.BlockSpec((8, tb), lambda i: (0, i)), pl.BlockSpec((NH, tb, HD), lambda i: (0, i, 0))],
        out_shape=[jax.ShapeDtypeStruct((8, T), F32), jax.ShapeDtypeStruct((NH, T, HD), F32)],
        scratch_shapes=[pltpu.VMEM((8, HD), F32)],
        compiler_params=_params(("arbitrary",)),
    )(proj, fb_pad)


def _fox_cum_bwd(dc, proj, fb_pad):
    T = proj.shape[0]
    tb = _tile(T, (256, 128))
    nb = T // tb

    def body(dc_ref, ff_ref, fb_ref, dff_ref, dfb_ref, carry):
        @pl.when(pl.program_id(0) == 0)
        def _():
            carry[...] = jnp.zeros_like(carry)
            dfb_ref[...] = jnp.zeros_like(dfb_ref)

        rid = lax.broadcasted_iota(jnp.int32, (8, tb), 0)
        m8 = jnp.zeros((8, tb), F32)
        for h in range(NH):
            m8 = m8 + jnp.where(rid == h, dc_ref[h], 0.0)
        dcb = jnp.concatenate([m8, jnp.zeros((HD - 8, tb), F32)], axis=0).T
        rev = lax.dot_general(_tri(tb, True).astype(F32), dcb, NN, precision=HIGHEST,
                              preferred_element_type=F32) + carry[0:1, :]
        carry[0:1, :] = rev[0:1, :]
        dff = rev * jax.nn.sigmoid(-(ff_ref[...] + fb_ref[...]))
        dff_ref[...] = dff.astype(dff_ref.dtype)
        dfb_ref[...] += jnp.sum(dff, axis=0, keepdims=True)

    return pl.pallas_call(
        body, name="fox_cum_bwd", grid=(nb,),
        in_specs=[pl.BlockSpec((NH, 8, tb), lambda i: (0, 0, nb - 1 - i)),
                  pl.BlockSpec((tb, HD), lambda i: (nb - 1 - i, CB_FF)), pl.BlockSpec((1, HD), lambda i: (0, 0))],
        out_specs=[pl.BlockSpec((tb, HD), lambda i: (nb - 1 - i, 0)), pl.BlockSpec((1, HD), lambda i: (0, 0))],
        out_shape=[jax.ShapeDtypeStruct((T, HD), BF16), jax.ShapeDtypeStruct((1, HD), F32)],
        scratch_shapes=[pltpu.VMEM((8, HD), F32)],
        compiler_params=_params(("arbitrary",)),
    )(dc, proj, fb_pad)


STRIP = 128


def _fox_scores(q, k, cq, ck, i, j, bq, bk, r0=0):
    rows = q.shape[0]
    s = _dot(q, k, NT) * SCALE + (cq - ck)
    diff = lax.broadcasted_iota(jnp.int32, (rows, bk), 1) - lax.broadcasted_iota(jnp.int32, (rows, bk), 0)
    return jnp.where(diff <= i * bq + r0 - j * bk, s, NEG)


def _heads(h):
    return slice(h * HD, (h + 1) * HD)


UNDERFLOW = -105.0


def _fox_windows(proj, cq):
    T = proj.shape[0]
    bq = _tile(T, (512, 256, 128))
    nq = T // bq
    assert nq <= HD

    def body(q_ref, k_ref, cq_ref, jlo_ref, ihi_ref, jloh_ref, ihih_ref, norm_s, cs_s, ce_s):
        i = pl.program_id(0)

        @pl.when(i == 0)
        def _():
            norm_s[...] = jnp.zeros_like(norm_s)
            cs_s[...] = jnp.zeros_like(cs_s)
            ce_s[...] = jnp.zeros_like(ce_s)

        lane = lax.broadcasted_iota(jnp.int32, (1, HD), 1)
        for h in range(NH):
            for row, ref in ((h, q_ref), (8 + h, k_ref)):
                x = ref[:, _heads(h)]
                biggest = jnp.max(jnp.sum(x * x, axis=1, keepdims=True), axis=0, keepdims=True)
                norm_s[row:row + 1, :] = jnp.maximum(norm_s[row:row + 1, :], jnp.broadcast_to(biggest, (1, HD)))
            cs_s[h, pl.ds(i, 1), :] = cq_ref[h, 0:1, :]
            ce_s[h:h + 1, :] = jnp.where(lane == i, cq_ref[h, bq - 1:bq, :], ce_s[h:h + 1, :])

        @pl.when(i == nq - 1)
        def _():
            rows = lax.broadcasted_iota(jnp.int32, (HD, HD), 0)
            cols = lax.broadcasted_iota(jnp.int32, (HD, HD), 1)
            valid = (rows < nq) & (cols < nq)

            def first_key(need):
                return jnp.broadcast_to(jnp.min(jnp.where(need, cols, HD).astype(F32), axis=1, keepdims=True),
                                        (HD, HD)).astype(jnp.int32)

            def last_query(need):
                return jnp.broadcast_to(jnp.max(jnp.where(need, rows, -1).astype(F32), axis=0, keepdims=True),
                                        (8, HD)).astype(jnp.int32)

            need = cols == rows
            for h in range(NH):
                slack = 2.05 * SCALE * jnp.sqrt(norm_s[h:h + 1, :] * norm_s[8 + h:9 + h, :])
                bound = cs_s[h] - ce_s[h:h + 1, :] + slack
                need_h = ((cols == rows) | ((bound >= UNDERFLOW) & (cols < rows))) & valid
                jloh_ref[h] = first_key(need_h)
                ihih_ref[h] = last_query(need_h)
                need = need | need_h
            jlo_ref[...] = first_key(need & valid)
            ihi_ref[...] = last_query(need & valid)

    jlo, ihi, jloh, ihih = pl.pallas_call(
        body, name="fox_windows", grid=(nq,),
        in_specs=[pl.BlockSpec((bq, WH), lambda i: (i, CB_FQ // NH)), pl.BlockSpec((bq, WH), lambda i: (i, CB_FK // NH)),
                  pl.BlockSpec((NH, bq, HD), lambda i: (0, i, 0))],
        out_specs=[pl.BlockSpec((HD, HD), lambda i: (0, 0)), pl.BlockSpec((8, HD), lambda i: (0, 0)),
                   pl.BlockSpec((NH, HD, HD), lambda i: (0, 0, 0)), pl.BlockSpec((NH, 8, HD), lambda i: (0, 0, 0))],
        out_shape=[jax.ShapeDtypeStruct((HD, HD), jnp.int32), jax.ShapeDtypeStruct((8, HD), jnp.int32),
                   jax.ShapeDtypeStruct((NH, HD, HD), jnp.int32), jax.ShapeDtypeStruct((NH, 8, HD), jnp.int32)],
        scratch_shapes=[pltpu.VMEM((16, HD), F32), pltpu.VMEM((NH, HD, HD), F32), pltpu.VMEM((8, HD), F32)],
        compiler_params=_params(("arbitrary",)),
    )(proj, proj, cq)
    return jnp.concatenate([jlo[:nq, 0], ihi[0, :nq], jloh[:, :nq, 0].reshape(-1), ihih[:, 0, :nq].reshape(-1)])


def _fox_fwd(win, proj, ct, cq):
    T = proj.shape[0]
    bq = bk = _tile(T, (512, 256, 128))
    nq = nk = T // bq

    def body(win_ref, q_ref, k_ref, v_ref, ct_ref, cq_ref, o_ref, lse_ref, m_s, l_s, acc_s):
        i, jj = pl.program_id(0), pl.program_id(1)
        j = win_ref[i] + jj

        @pl.when(jj == 0)
        def _():
            m_s[...] = jnp.full_like(m_s, NEG)
            l_s[...] = jnp.zeros_like(l_s)
            acc_s[...] = jnp.zeros_like(acc_s)

        def head_step(h):
            hs = _heads(h)
            k, v, ck = k_ref[:, hs], v_ref[:, hs], ct_ref[h:h + 1, :]
            for r0 in range(0, bq, STRIP):
                rs = slice(r0, r0 + STRIP)
                s = _fox_scores(q_ref[rs, hs], k, cq_ref[h, rs, 0:1], ck, i, j, bq, bk, r0)
                m_prev = m_s[h, rs]
                m_new = jnp.maximum(m_prev, jnp.max(s, axis=1, keepdims=True))
                alpha = jnp.exp(m_prev - m_new)
                p = jnp.exp(s - m_new)
                l_s[h, rs] = alpha * l_s[h, rs] + jnp.sum(p, axis=1, keepdims=True)
                acc_s[rs, hs] = alpha * acc_s[rs, hs] + _dot(p, v, NN)
                m_s[h, rs] = m_new

        for h in range(NH):
            pl.when((j <= i) & (j >= win_ref[2 * nq + h * nq + i]))(functools.partial(head_step, h))

        @pl.when(jj == nk - 1)
        def _():
            for h in range(NH):
                o_ref[:, _heads(h)] = acc_s[:, _heads(h)] / l_s[h]
                lse_ref[h] = jnp.broadcast_to(m_s[h] + jnp.log(l_s[h]), (bq, HD))

    def key_block(i, jj, win):
        return jnp.minimum(win[i] + jj, i)

    def kv(off):
        return pl.BlockSpec((bk, WH), lambda i, jj, win, o=off // NH: (key_block(i, jj, win), o))

    stat = pl.BlockSpec((NH, bq, HD), lambda i, jj, win: (0, i, 0))
    return pl.pallas_call(
        body, name="fox_fwd",
        grid_spec=pltpu.PrefetchScalarGridSpec(
            num_scalar_prefetch=1, grid=(nq, nk),
            in_specs=[pl.BlockSpec((bq, WH), lambda i, jj, win: (i, CB_FQ // NH)), kv(CB_FK), kv(CB_FV),
                      pl.BlockSpec((8, bk), lambda i, jj, win: (0, key_block(i, jj, win))), stat],
            out_specs=[pl.BlockSpec((bq, WH), lambda i, jj, win: (i, 0)), stat],
            scratch_shapes=[pltpu.VMEM((NH, bq, 1), F32), pltpu.VMEM((NH, bq, 1), F32), pltpu.VMEM((bq, WH), F32)]),
        out_shape=[jax.ShapeDtypeStruct((T, WH), F32), jax.ShapeDtypeStruct((NH, T, HD), F32)],
        compiler_params=_params(("parallel", "arbitrary")),
    )(win, proj, proj, proj, ct, cq)


def _fox_bwd_dq(win, proj, ct, cq, lse, do):
    T = proj.shape[0]
    bq = bk = _tile(T, (512, 256, 128))
    nq = nk = T // bq

    def body(win_ref, q_ref, k_ref, v_ref, ct_ref, cq_ref, lse_ref, do_ref, dq_ref, delta_ref, acc_s, delta_s, psum_s):
        i, jj = pl.program_id(0), pl.program_id(1)
        j = win_ref[i] + jj % nk

        @pl.when(jj == 0)
        def _():
            acc_s[...] = jnp.zeros_like(acc_s)
            delta_s[...] = jnp.zeros_like(delta_s)
            psum_s[...] = jnp.zeros_like(psum_s)

        def probs(h):
            hs = _heads(h)
            k = k_ref[:, hs]
            s = _fox_scores(q_ref[:, hs], k, cq_ref[h, :, 0:1], ct_ref[h:h + 1, :], i, j, bq, bk)
            return k, jnp.exp(s - lse_ref[h, :, 0:1]), _dot(do_ref[:, hs], v_ref[:, hs], NT)

        def first_sweep(h):
            _, p, dp = probs(h)
            delta_s[h] += jnp.sum(p * dp, axis=1, keepdims=True)
            psum_s[h] += jnp.sum(p, axis=1, keepdims=True)

        def second_sweep(h):
            k, p, dp = probs(h)
            ds = p * (dp - delta_s[h] / psum_s[h])
            acc_s[:, _heads(h)] += _dot(ds, k, NN) * SCALE

        for h in range(NH):
            mine = (j <= i) & (j >= win_ref[2 * nq + h * nq + i])
            pl.when(mine & (jj < nk))(functools.partial(first_sweep, h))
            pl.when(mine & (jj >= nk))(functools.partial(second_sweep, h))

        @pl.when(jj == 2 * nk - 1)
        def _():
            dq_ref[...] = acc_s[...].astype(dq_ref.dtype)
            for h in range(NH):
                delta_ref[h] = jnp.broadcast_to(delta_s[h] / psum_s[h], (bq, HD))

    def key_block(i, jj, win):
        return jnp.minimum(win[i] + jj % nk, i)

    def kv(off):
        return pl.BlockSpec((bk, WH), lambda i, jj, win, o=off // NH: (key_block(i, jj, win), o))

    qrow = pl.BlockSpec((bq, WH), lambda i, jj, win: (i, 0))
    stat = pl.BlockSpec((NH, bq, HD), lambda i, jj, win: (0, i, 0))
    return pl.pallas_call(
        body, name="fox_bwd_dq",
        grid_spec=pltpu.PrefetchScalarGridSpec(
            num_scalar_prefetch=1, grid=(nq, 2 * nk),
            in_specs=[pl.BlockSpec((bq, WH), lambda i, jj, win: (i, CB_FQ // NH)), kv(CB_FK), kv(CB_FV),
                      pl.BlockSpec((8, bk), lambda i, jj, win: (0, key_block(i, jj, win))), stat, stat, qrow],
            out_specs=[qrow, stat],
            scratch_shapes=[pltpu.VMEM((bq, WH), F32), pltpu.VMEM((NH, bq, 1), F32), pltpu.VMEM((NH, bq, 1), F32)]),
        out_shape=[jax.ShapeDtypeStruct((T, WH), BF16), jax.ShapeDtypeStruct((NH, T, HD), F32)],
        compiler_params=_params(("parallel", "arbitrary")),
    )(win, proj, proj, proj, ct, cq, lse, do)


def _fox_bwd_dkv(win, proj, ct, cq, lse, delta, do):
    T = proj.shape[0]
    bq = bk = _tile(T, (512, 256, 128))
    nq = nk = T // bq

    def body(win_ref, q_ref, k_ref, v_ref, ct_ref, cq_ref, lse_ref, delta_ref, do_ref, dk_ref, dv_ref, dc_ref,
             dk_s, dv_s, dc_s):
        j, ii = pl.program_id(0), pl.program_id(1)
        i = j + ii

        @pl.when(ii == 0)
        def _():
            dk_s[...] = jnp.zeros_like(dk_s)
            dv_s[...] = jnp.zeros_like(dv_s)
            dc_s[...] = jnp.zeros_like(dc_s)

        def head_step(h):
            hs = _heads(h)
            q = q_ref[:, hs]
            d_o = do_ref[:, hs]
            s = _fox_scores(q, k_ref[:, hs], cq_ref[h, :, 0:1], ct_ref[h:h + 1, :], i, j, bq, bk)
            p = jnp.exp(s - lse_ref[h, :, 0:1])
            dv_s[:, hs] += _dot(p, d_o, TN)
            dp = _dot(d_o, v_ref[:, hs], NT)
            ds = p * (dp - delta_ref[h, :, 0:1])
            dk_s[:, hs] += _dot(ds, q, TN) * SCALE
            dc_s[h:h + 1, :] -= jnp.sum(ds, axis=0, keepdims=True)

        for h in range(NH):
            pl.when(i <= win_ref[2 * nq + NH * nq + h * nk + j])(functools.partial(head_step, h))

        @pl.when(ii == nq - 1)
        def _():
            dk_ref[...] = dk_s[...].astype(dk_ref.dtype)
            dv_ref[...] = dv_s[...].astype(dv_ref.dtype)
            for h in range(NH):
                dc_ref[h] = jnp.broadcast_to(dc_s[h:h + 1, :], (8, bk))

    def query_block(j, ii, win):
        return jnp.minimum(j + ii, win[nq + j])

    def kv(off):
        return pl.BlockSpec((bk, WH), lambda j, ii, win, o=off // NH: (j, o))

    qrow = pl.BlockSpec((bq, WH), lambda j, ii, win: (query_block(j, ii, win), 0))
    stat = pl.BlockSpec((NH, bq, HD), lambda j, ii, win: (0, query_block(j, ii, win), 0))
    krow = pl.BlockSpec((bk, WH), lambda j, ii, win: (j, 0))
    return pl.pallas_call(
        body, name="fox_bwd_dkv",
        grid_spec=pltpu.PrefetchScalarGridSpec(
            num_scalar_prefetch=1, grid=(nk, nq),
            in_specs=[pl.BlockSpec((bq, WH), lambda j, ii, win: (query_block(j, ii, win), CB_FQ // NH)), kv(CB_FK),
                      kv(CB_FV), pl.BlockSpec((8, bk), lambda j, ii, win: (0, j)), stat, stat, stat, qrow],
            out_specs=[krow, krow, pl.BlockSpec((NH, 8, bk), lambda j, ii, win: (0, 0, j))],
            scratch_shapes=[pltpu.VMEM((bk, WH), F32), pltpu.VMEM((bk, WH), F32), pltpu.VMEM((8, bk), F32)]),
        out_shape=[jax.ShapeDtypeStruct((T, WH), BF16), jax.ShapeDtypeStruct((T, WH), BF16),
                   jax.ShapeDtypeStruct((NH, 8, T), F32)],
        compiler_params=_params(("parallel", "arbitrary")),
    )(win, proj, proj, proj, ct, cq, lse, delta, do)


def _mem_probs(q, mk):
    s = _dot(q, mk, NT) * SCALE
    e = jnp.exp(s - jnp.max(s, axis=1, keepdims=True))
    return e / jnp.sum(e, axis=1, keepdims=True)


def _mem_fwd(proj, mem_kv):
    T = proj.shape[0]
    tr = _tile(T, (512, 256, 128))
    M = mem_kv.shape[0]

    def body(q_ref, mk_ref, mv_ref, o_ref):
        o_ref[...] = _dot(_mem_probs(q_ref[...], mk_ref[...]), mv_ref[...], NN)

    return pl.pallas_call(
        body, name="mem_fwd", grid=(NM, T // tr),
        in_specs=[pl.BlockSpec((tr, HD), lambda h, i: (i, CB_MQ + h)),
                  pl.BlockSpec((M, HD), lambda h, i: (0, h)), pl.BlockSpec((M, HD), lambda h, i: (0, NM + h))],
        out_specs=pl.BlockSpec((tr, HD), lambda h, i: (i, h)),
        out_shape=jax.ShapeDtypeStruct((T, WM), F32),
        compiler_params=_params(("parallel", "parallel")),
    )(proj, mem_kv, mem_kv)


def _mem_bwd(proj, mem_kv, do):
    T = proj.shape[0]
    tr = _tile(T, (512, 256, 128))
    M = mem_kv.shape[0]

    def body(q_ref, mk_ref, mv_ref, do_ref, dq_ref, dmk_ref, dmv_ref):
        @pl.when(pl.program_id(1) == 0)
        def _():
            dmk_ref[...] = jnp.zeros_like(dmk_ref)
            dmv_ref[...] = jnp.zeros_like(dmv_ref)

        q, mk, d_o = q_ref[...], mk_ref[...], do_ref[...]
        p = _mem_probs(q, mk)
        dmv_ref[...] += _dot(p, d_o, TN)
        dp = _dot(d_o, mv_ref[...], NT)
        ds = p * (dp - jnp.sum(p * dp, axis=1, keepdims=True))
        dq_ref[...] = (_dot(ds, mk, NN) * SCALE).astype(dq_ref.dtype)
        dmk_ref[...] += _dot(ds, q, TN) * SCALE

    acc = pl.BlockSpec((M, HD), lambda h, i: (0, h))
    row = pl.BlockSpec((tr, HD), lambda h, i: (i, h))
    return pl.pallas_call(
        body, name="mem_bwd", grid=(NM, T // tr),
        in_specs=[pl.BlockSpec((tr, HD), lambda h, i: (i, CB_MQ + h)),
                  pl.BlockSpec((M, HD), lambda h, i: (0, h)), pl.BlockSpec((M, HD), lambda h, i: (0, NM + h)), row],
        out_specs=[row, acc, acc],
        out_shape=[jax.ShapeDtypeStruct((T, WM), BF16), jax.ShapeDtypeStruct((M, WM), F32),
                   jax.ShapeDtypeStruct((M, WM), F32)],
        compiler_params=_params(("parallel", "arbitrary")),
    )(proj, mem_kv, mem_kv, do)


def _mesh_place():
    x, y, c = lax.axis_index("x"), lax.axis_index("y"), lax.axis_index("c")
    return x, y, c


CHIP_FLIPS = (4, 2, 6)
CHIP_OF_SLOT = (0,) + CHIP_FLIPS


def _peer(x, y, c, k):
    px = 1 - x if k & 4 else x
    py = 1 - y if k & 2 else y
    pc = 1 - c if k & 1 else c
    return (px, py, pc), 4 * px + 2 * py + pc


class _Gather:
    def __init__(self, shapes, pad_rows):
        self.shapes, self.pad_rows, self.n = shapes, pad_rows, len(shapes)
        self.npad = sum(1 for p in pad_rows if p)

    def zeros(self):
        return jnp.zeros((max(self.pad_rows) or 16, self.shapes[0][1]), BF16)

    def out_shape(self):
        return [jax.ShapeDtypeStruct((NDEV * r + p, c), BF16) for (r, c), p in zip(self.shapes, self.pad_rows)]

    def sems(self):
        return [pltpu.SemaphoreType.DMA((self.n, NDEV - 1)), pltpu.SemaphoreType.DMA((self.n, NDEV - 1)),
                pltpu.SemaphoreType.DMA((self.n + self.npad,))]

    def _copies(self, ins, z_ref, outs, send_sems, recv_sems, loc_sems):
        x, y, c = _mesh_place()
        me = 4 * x + 2 * y + c
        sibling, _ = _peer(x, y, c, 1)
        local, first, arrive, forward = [], [], [], []
        ip = 0
        for w in range(self.n):
            r = ins[w].shape[0]
            dst = outs[w].at[pl.ds(pl.multiple_of(me * r, 16), r), :]
            local.append(functools.partial(pltpu.make_async_copy, ins[w], dst, loc_sems.at[w]))
            if self.pad_rows[w]:
                local.append(functools.partial(pltpu.make_async_copy, z_ref.at[pl.ds(0, self.pad_rows[w]), :],
                                               outs[w].at[pl.ds(NDEV * r, self.pad_rows[w]), :], loc_sems.at[self.n + ip]))
                ip += 1

            def remote(src, dst_, s, to):
                return functools.partial(pltpu.make_async_remote_copy, src_ref=src, dst_ref=dst_, send_sem=send_sems.at[w, s],
                                         recv_sem=recv_sems.at[w, s], device_id=to, device_id_type=MESH)

            for s, k in enumerate((1,) + CHIP_FLIPS):
                first.append(remote(ins[w], dst, s, _peer(x, y, c, k)[0]))
            for s, k in enumerate(CHIP_FLIPS):
                _, pidx = _peer(x, y, c, k)
                rows = outs[w].at[pl.ds(pl.multiple_of(pidx * r, 16), r), :]
                arrive.append(remote(rows, rows, 1 + s, sibling))
                forward.append(remote(rows, rows, 4 + s, sibling))
        return local, first, arrive, forward


    def start(self, *refs):
        local, first, _, _ = self._copies(*refs)
        for make in local + first:
            make().start()

    def forward(self, *refs):
        _, _, arrive, forward = self._copies(*refs)
        for a, f in zip(arrive, forward):
            a().wait_recv()
            f().start()

    def finish(self, *refs):
        local, first, _, forward = self._copies(*refs)
        for make in local + first[0::4] + forward:
            make().wait()
        for s in (1, 2, 3):
            for make in first[s::4]:
                make().wait_send()


def _all_gather(shards, pad_rows):
    n = len(shards)
    plan = _Gather([s.shape for s in shards], pad_rows)

    def body(*refs):
        args = (refs[:n], refs[n], refs[n + 1:2 * n + 1]) + tuple(refs[2 * n + 1:])
        plan.start(*args)
        plan.forward(*args)
        plan.finish(*args)

    any_spec = pl.BlockSpec(memory_space=pl.ANY)
    return pl.pallas_call(
        body, name="all_gather_weights",
        in_specs=[any_spec] * (n + 1), out_specs=[any_spec] * n,
        out_shape=plan.out_shape(),
        scratch_shapes=plan.sems(),
        compiler_params=pltpu.CompilerParams(has_side_effects=True),
    )(*shards, plan.zeros())


def _exchange_in_chip(grads, shard_rows, name):
    n = len(grads)
    plan = _InChip([g.shape for g in grads], shard_rows)

    def body(*refs):
        args = (refs[:n], refs[n:2 * n]) + tuple(refs[2 * n:])
        plan.start(*args)
        plan.finish(*args)

    any_spec = pl.BlockSpec(memory_space=pl.ANY)
    return pl.pallas_call(
        body, name=name,
        in_specs=[any_spec] * n, out_specs=[any_spec] * n, out_shape=plan.out_shape(), scratch_shapes=plan.sems(),
        compiler_params=pltpu.CompilerParams(has_side_effects=True),
    )(*grads)


class _InChip:
    def __init__(self, shapes, shard_rows):
        self.shapes, self.rows, self.n, self.ns = shapes, shard_rows, len(shapes), len(CHIP_OF_SLOT)

    def out_shape(self):
        return [jax.ShapeDtypeStruct((self.ns, r, s[1]), BF16) for s, r in zip(self.shapes, self.rows)]

    def sems(self):
        return [pltpu.SemaphoreType.DMA((self.n, self.ns)), pltpu.SemaphoreType.DMA((self.n, self.ns))]

    def _copies(self, ins, theirs, send_sems, recv_sems):
        x, y, c = _mesh_place()
        sibling, _ = _peer(x, y, c, 1)
        copies = []
        for w in range(self.n):
            r = self.rows[w]
            for s, k in enumerate(CHIP_OF_SLOT):
                _, other = _peer(x, y, c, k | 1)
                copies.append(pltpu.make_async_remote_copy(
                    src_ref=ins[w].at[pl.ds(pl.multiple_of(other * r, 16), r), :], dst_ref=theirs[w].at[s],
                    send_sem=send_sems.at[w, s], recv_sem=recv_sems.at[w, s], device_id=sibling, device_id_type=MESH))
        return copies

    def start(self, *refs):
        for cp in self._copies(*refs):
            cp.start()

    def finish(self, *refs):
        for cp in self._copies(*refs):
            cp.wait()


def _pair_sum(grad, theirs, name):
    ns, r, c = theirs.shape
    tr = r if r * c <= 2 * 1024 * 1024 else _tile(r, (256, 128, 64, 32, 16))
    per_block = r // tr

    def body(a_ref, b_ref, o_ref):
        o_ref[...] = (a_ref[...].astype(F32) + b_ref[...].astype(F32)).astype(o_ref.dtype)

    def owner_rows(s, i):
        x, y, c_ = _mesh_place()
        fx, fy = s % 2, s // 2
        px, py = x + fx - 2 * x * fx, y + fy - 2 * y * fy
        return ((4 * px + 2 * py + c_) * per_block + i, 0)

    slot = pl.BlockSpec((None, tr, c), lambda s, i: (s, i, 0))
    return pl.pallas_call(
        body, name=name, grid=(ns, per_block),
        in_specs=[pl.BlockSpec((tr, c), owner_rows), slot], out_specs=slot,
        out_shape=jax.ShapeDtypeStruct((ns, r, c), theirs.dtype),
        compiler_params=_params(("parallel", "parallel")),
    )(grad, theirs)


class _ChipExchange:
    def __init__(self, shapes):
        self.shapes, self.n, self.ns = shapes, len(shapes), len(CHIP_OF_SLOT) - 1

    def out_shape(self):
        return [jax.ShapeDtypeStruct((self.ns,) + tuple(s[1:]), BF16) for s in self.shapes]

    def sems(self):
        return [pltpu.SemaphoreType.DMA((self.n, self.ns)), pltpu.SemaphoreType.DMA((self.n, self.ns))]

    def _copies(self, ins, outs, send_sems, recv_sems):
        x, y, c = _mesh_place()
        copies = []
        for w in range(self.n):
            for s, k in enumerate(CHIP_OF_SLOT[1:]):
                peer, _ = _peer(x, y, c, k)
                copies.append(pltpu.make_async_remote_copy(
                    src_ref=ins[w].at[s + 1], dst_ref=outs[w].at[s], send_sem=send_sems.at[w, s],
                    recv_sem=recv_sems.at[w, s], device_id=peer, device_id_type=MESH))
        return copies

    def start(self, *refs):
        for cp in self._copies(*refs):
            cp.start()

    def finish(self, *refs):
        for cp in self._copies(*refs):
            cp.wait()


def _sum_chips(pair, recv, name):
    ns, r, c = recv.shape
    tr, tc = _panel(r, c)

    def body(p_ref, x_ref, o_ref):
        acc = p_ref[...].astype(F32)
        for s in range(x_ref.shape[0]):
            acc = acc + x_ref[s].astype(F32)
        o_ref[...] = acc

    return pl.pallas_call(
        body, name=name, grid=(r // tr, c // tc),
        in_specs=[pl.BlockSpec((None, tr, tc), lambda i, j: (0, i, j)), pl.BlockSpec((ns, tr, tc), lambda i, j: (0, i, j))],
        out_specs=pl.BlockSpec((tr, tc), lambda i, j: (i, j)),
        out_shape=jax.ShapeDtypeStruct((r, c), F32),
        compiler_params=_params(("parallel", "parallel")),
    )(pair, recv)


def _panel(r, c):
    for tr in (1024, 512, 256, 128):
        if r % tr == 0 and tr * c <= 512 * 1024:
            return tr, c
    for tc in (2048, 1024, 512, 256, 128):
        if c % tc == 0 and r * tc <= 512 * 1024:
            return r, tc
    return _tile(r, (64, 32, 16, 8)), c


def _all_reduce_small(part):
    R, W = part.shape

    def body(x_ref, o_ref, buf, send_sems, recv_sems):
        x, y, c = _mesh_place()
        me = 4 * x + 2 * y + c
        buf[me] = x_ref[...]
        copies = []
        for k in range(1, NDEV):
            peer, _ = _peer(x, y, c, k)
            cp = pltpu.make_async_remote_copy(src_ref=x_ref, dst_ref=buf.at[me], send_sem=send_sems.at[k - 1],
                                              recv_sem=recv_sems.at[k - 1], device_id=peer, device_id_type=MESH)
            cp.start()
            copies.append(cp)
        for cp in copies:
            cp.wait()
        acc = buf[0]
        for d in range(1, NDEV):
            acc = acc + buf[d]
        o_ref[...] = acc

    vm = pl.BlockSpec(memory_space=pltpu.VMEM)
    return pl.pallas_call(
        body, name="all_reduce_small", in_specs=[vm], out_specs=vm,
        out_shape=jax.ShapeDtypeStruct((R, W), F32),
        scratch_shapes=[pltpu.VMEM((NDEV, R, W), F32), pltpu.SemaphoreType.DMA((NDEV - 1,)),
                        pltpu.SemaphoreType.DMA((NDEV - 1,))],
        compiler_params=pltpu.CompilerParams(has_side_effects=True),
    )(part)


def _adam_math(w, g, m, v):
    m2 = ADAM_B1 * m + (1.0 - ADAM_B1) * g
    v2 = ADAM_B2 * v + (1.0 - ADAM_B2) * (g * g)
    m_hat = m2 / (1.0 - ADAM_B1 ** ADAM_STEP)
    v_hat = v2 / (1.0 - ADAM_B2 ** ADAM_STEP)
    delta = -ADAM_LR * (m_hat / (jnp.sqrt(v_hat) + ADAM_EPS) + ADAM_WD * w)
    return delta, m2, v2


def _adamw(w, g, m, v, name):
    r, c = w.shape
    tr, tc = _panel(r, c)

    def body(w_ref, g_ref, m_ref, v_ref, d_ref, m2_ref, v2_ref):
        d_ref[...], m2_ref[...], v2_ref[...] = _adam_math(w_ref[...], g_ref[...], m_ref[...], v_ref[...])

    spec = pl.BlockSpec((tr, tc), lambda i, j: (i, j))
    return pl.pallas_call(
        body, name=name, grid=(r // tr, c // tc), in_specs=[spec] * 4, out_specs=[spec] * 3,
        out_shape=[jax.ShapeDtypeStruct((r, c), F32)] * 3,
        compiler_params=_params(("parallel", "parallel")),
    )(w, g, m, v)


GAINS = ("ffn1_pre", "ffn1_post", "mix_pre", "mix_post", "mem_norm", "ffn2_pre", "ffn2_post")
GAIN_ROWS = D // HD
ROW_LB = len(GAINS) * GAIN_ROWS
ROWS_GRAD_IN = ROW_LB + 24
ROWS_PACKED = ROW_LB + 32


def _small_update(gsum, w_p, m_p, v_p):
    def body(g_ref, w_ref, m_ref, v_ref, go_ref, d_ref, m2_ref, v2_ref):
        a0 = w_ref[ROW_LB:ROW_LB + 8, :]
        a1 = w_ref[ROW_LB + 8:ROW_LB + 16, :]
        mx = jnp.maximum(a0, a1)
        e0, e1 = jnp.exp(a0 - mx), jnp.exp(a1 - mx)
        lb = e0 / (e0 + e1)
        da0 = g_ref[ROW_LB:ROW_LB + 8, :] * lb * (1.0 - lb)
        g = jnp.concatenate([g_ref[0:ROW_LB, :], da0, -da0, g_ref[ROW_LB + 8:ROWS_GRAD_IN, :]], axis=0)
        go_ref[...] = g
        d_ref[...], m2_ref[...], v2_ref[...] = _adam_math(w_ref[...], g, m_ref[...], v_ref[...])

    vm = pl.BlockSpec(memory_space=pltpu.VMEM)
    return pl.pallas_call(
        body, name="small_update", in_specs=[vm] * 4, out_specs=[vm] * 4,
        out_shape=[jax.ShapeDtypeStruct((ROWS_PACKED, HD), F32)] * 4,
    )(gsum, w_p, m_p, v_p)


def _rows8(a):
    a = a.reshape(-1)
    rows = -(-a.shape[0] // HD)
    rows8 = -(-rows // 8) * 8
    return jnp.pad(a, (0, rows8 * HD - a.shape[0])).reshape(rows8, HD)


def _pack_small(gains, lb0, lb1, gnorm, fb):
    return jnp.concatenate([_rows8(g) for g in gains] + [_rows8(lb0), _rows8(lb1), _rows8(gnorm), _rows8(fb)], axis=0)


def _unpack_small(p):
    out = {}
    for i, name in enumerate(GAINS):
        out[name] = p[i * GAIN_ROWS:(i + 1) * GAIN_ROWS].reshape(1, D)
    lb0 = p[ROW_LB:ROW_LB + NH].reshape(1, WH)
    lb1 = p[ROW_LB + 8:ROW_LB + 8 + NH].reshape(1, WH)
    out["hgrn_lb"] = jnp.concatenate([lb0, lb1], axis=0)
    out["hgrn_gnorm"] = p[ROW_LB + 16:ROW_LB + 16 + NH].reshape(1, WH)
    out["fox_fb"] = p[ROW_LB + 24:ROW_LB + 25, 0:NH]
    return out


def _ffn_forward(n, wg_t, wu_t, wd, tag, rider=None, rider_down=None):
    g, u, a, *carried = _ffn_up(n, wg_t, wu_t, f"{tag}_up", rider)
    if wd is None:
        wd = carried[0]
    if rider_down is None:
        h = _mm(a, wd, "nn", F32, f"{tag}_down")
    else:
        h, *more = _mm(a, wd, "nn", F32, f"{tag}_down", rider=rider_down)
        carried = carried + more
    return h, (n, g, u, a), carried


def _mm_out(res):
    return (res[0], list(res[1:])) if isinstance(res, (list, tuple)) else (res, [])


def _ffn_backward(dh, saved, wg_t, wu_t, wd, tag, rider=None, exchange=None, rider_dwd=None, after_dwd=None):
    n, g, u, a = saved
    dwd, got = _mm_out(_mm(a, dh, "tn", BF16, f"{tag}_dwd", rider=rider_dwd))
    rider_dwg = None
    if after_dwd is not None:
        rider, rider_dwg = after_dwd(got)
    dg, du, *carried = _ffn_act_bwd(dh, wd, g, u, f"{tag}_act_bwd", rider)
    dwg, got = _mm_out(_mm(dg, n, "tn", BF16, f"{tag}_dwg", rider=rider_dwg))
    carried = carried + got
    dwu = _mm(du, n, "tn", BF16, f"{tag}_dwu")
    if exchange is None:
        dn = _mm(dg, wg_t, "nn", F32, f"{tag}_dn_g")
        dn = _mm(du, wu_t, "nn", F32, f"{tag}_dn_u", add=dn)
    else:
        ride_a, ride_b, take = exchange(dwg, dwu, dwd)
        dn, got_a = _mm_out(_mm(dg, wg_t, "nn", F32, f"{tag}_dn_g", rider=ride_a))
        dn, got_b = _mm_out(_mm(du, wu_t, "nn", F32, f"{tag}_dn_u", add=dn, rider=ride_b))
        take(got_a, got_b)
    return dn, (dwg, dwu, dwd), carried


GATHER_FIRST = ("ffn1_wg", "ffn1_wu")
GATHER_IN_FFN1_UP = ("ffn1_wd", "w_in")
GATHER_IN_FFN1_DOWN = ("w_gate",)
GATHER_IN_PROJ = ("w_mem_kv", "w_hgrn_out", "w_fox_out", "w_mem_out", "w_o")
GATHER_IN_GATE = ("ffn2_wg",)
GATHER_IN_HGRN = ("ffn2_wu",)
GATHER_IN_FFN2_UP = ("ffn2_wd",)
GROUP_FFN1 = ("ffn1_wg", "ffn1_wu", "ffn1_wd")
GROUP_MIX = ("w_in", "w_mem_kv", "w_hgrn_out", "w_fox_out", "w_mem_out", "w_gate", "w_o")
GROUP_FFN2 = ("ffn2_wg", "ffn2_wu", "ffn2_wd")


def _gather_rider(blocks, names):
    plan = _Gather([blocks[n].shape for n in names], [FFN_PAD.get(n, 0) for n in names])
    return _Rider(plan, [blocks[n] for n in names] + [plan.zeros()], GATHER_STEPS)


def _local_step(x, mem, tgt, small, wts=None, blocks=None):
    T = x.shape[0]
    tr = _tile(T, (256, 128))
    fb_pad = jnp.pad(small["fox_fb"], ((0, 0), (0, HD - NH)))
    dist = blocks is not None
    if dist:
        wts = dict(zip(GATHER_FIRST, _all_gather([blocks[n] for n in GATHER_FIRST], [FFN_PAD[n] for n in GATHER_FIRST])))

    def riding(names):
        return _gather_rider(blocks, names) if dist else None

    (n1,) = _rowwise(_norm_fn, [(x, 0)], [(small["ffn1_pre"], None)], [BF16], "ffn1_pre", tr, D, 1)
    h1, ffn1_saved, carried = _ffn_forward(n1, wts["ffn1_wg"], wts["ffn1_wu"], wts.get("ffn1_wd"), "ffn1",
                                           riding(GATHER_IN_FFN1_UP), riding(GATHER_IN_FFN1_DOWN))
    wts.update(zip(GATHER_IN_FFN1_UP + GATHER_IN_FFN1_DOWN, carried))
    ffn1_out = functools.partial(_post_pre_fn, 0.5)
    x1, un = _rowwise(ffn1_out, [(x, 0), (h1, 0)], [(small["ffn1_post"], None), (small["mix_pre"], None)], [F32, BF16],
                      "ffn1_post_mix_pre", tr, D, 1)
    if dist:
        proj, *carried = _mm(un, wts["w_in"], "nn", F32, "proj", rider=riding(GATHER_IN_PROJ))
        wts.update(zip(GATHER_IN_PROJ, carried))
        z, *carried = _mm(un, wts["w_gate"], "nt", BF16, "gate_logits", rider=riding(GATHER_IN_GATE))
        wts.update(zip(GATHER_IN_GATE, carried))
    else:
        proj = _mm(un, wts["w_in"], "nn", F32, "proj")
        z = _mm(un, wts["w_gate"], "nt", BF16, "gate_logits")
    (memn,) = _rowwise(_norm_fn, [(mem, 0)], [(small["mem_norm"], None)], [BF16], "mem_norm", mem.shape[0], D, 1)
    mem_kv = _mm(memn, wts["w_mem_kv"], "nn", F32, "mem_kv")

    o_raw, states, *carried = _hgrn_fwd(proj, small["hgrn_lb"], riding(GATHER_IN_HGRN))
    wts.update(zip(GATHER_IN_HGRN, carried))
    tr_head = _tile(T, (1024, 512, 256, 128))
    (o_h,) = _rowwise(_hpost_fn, [(o_raw, 0), (proj, CB_HOG)], [(small["hgrn_gnorm"], 0)], [BF16], "hgrn_post",
                      tr_head, HD, NH)
    ct, cq = _fox_cum(proj, fb_pad)
    win = _fox_windows(proj, cq)
    o_f, lse = _fox_fwd(win, proj, ct, cq)
    o_m = _mem_fwd(proj, mem_kv)

    yh = _mm(o_h, wts["w_hgrn_out"], "nt", BF16, "hgrn_out")
    yf = _mm(o_f, wts["w_fox_out"], "nt", BF16, "fox_out")
    ym = _mm(o_m, wts["w_mem_out"], "nt", BF16, "mem_out")
    zc = D // 512
    merge_rows = [(z, 0), (z, zc), (z, 2 * zc), (yh, 0), (yf, 0), (ym, 0)]
    tr_merge = _tile(T, (512, 256, 128))
    (merged,) = _rowwise(_merge_fn, merge_rows, [], [BF16], "merge", tr_merge, 512, zc)
    m = _mm(merged, wts["w_o"], "nn", F32, "mix_out")
    mix_out = functools.partial(_post_pre_fn, 1.0)
    x2, n2 = _rowwise(mix_out, [(x1, 0), (m, 0)], [(small["mix_post"], None), (small["ffn2_pre"], None)], [F32, BF16],
                      "mix_post_ffn2_pre", tr, D, 1)
    h2, ffn2_saved, carried = _ffn_forward(n2, wts["ffn2_wg"], wts["ffn2_wu"], wts.get("ffn2_wd"), "ffn2",
                                           riding(GATHER_IN_FFN2_UP))
    wts.update(zip(GATHER_IN_FFN2_UP, carried))
    dy, loss_part = _loss(x2, h2, small["ffn2_post"], tgt, "loss")

    gw, gs, reduced = {}, {}, {}

    def pair_sums(names, tag):
        if not dist:
            return None, None
        theirs = brought.get(tag)
        if theirs is None:
            theirs = _exchange_in_chip([gw[n] for n in names], [blocks[n].shape[0] for n in names], f"reduce_in_chip_{tag}")
        pairs = [_pair_sum(gw[n], t_, f"pair_{n}") for n, t_ in zip(names, theirs)]
        return pairs, _Rider(_ChipExchange([p.shape for p in pairs]), pairs, EXCHANGE_STEPS)

    def chip_sums(names, pairs, recv):
        for n, p_, r_ in zip(names, pairs or (), recv):
            reduced[n] = _sum_chips(p_, r_, f"sum_{n}")

    brought = {}

    def in_chip_rider(names):
        grads_ = [gw[n] for n in names]
        return _Rider(_InChip([g_.shape for g_ in grads_], [blocks[n].shape[0] for n in names]), grads_, EXCHANGE_STEPS)

    def ffn2_exchange(dwg, dwu, dwd):
        gw.update(ffn2_wg=dwg, ffn2_wu=dwu, ffn2_wd=dwd)
        return in_chip_rider(GROUP_FFN2), None, lambda got_a, got_b: brought.update(ffn2=got_a)

    dh2, gs["ffn2_post"] = _rowwise_bwd(functools.partial(_resid_h_fn, 0.5), [(h2, 0)], [(small["ffn2_post"], None)],
                                        [(dy, 0)], [0], [BF16], "ffn2_post_bwd", tr, D, 1)
    dn2, (gw["ffn2_wg"], gw["ffn2_wu"], gw["ffn2_wd"]), _ = _ffn_backward(
        dh2, ffn2_saved, wts["ffn2_wg"], wts["ffn2_wu"], wts["ffn2_wd"], "ffn2", exchange=ffn2_exchange if dist else None)
    pairs_ffn2, ride_ffn2_grads = pair_sums(GROUP_FFN2, "ffn2")

    dx1, dm, gs["mix_post"], gs["ffn2_pre"] = _rowwise_bwd(
        mix_out, [(x1, 0), (m, 0)], [(small["mix_post"], None), (small["ffn2_pre"], None)], [(dy, 0), (dn2, 0)], [0, 1],
        [F32, BF16], "mix_post_ffn2_pre_bwd", tr, D, 1)
    dmerged = _mm(dm, wts["w_o"], "nt", F32, "d_merged")
    gw["w_o"] = _mm(merged, dm, "tn", BF16, "d_w_o")
    dz0, dz1, dz2, dyh, dyf, dym = _rowwise_bwd(_merge_fn, merge_rows, [], [(dmerged, 0)], [0, 1, 2, 3, 4, 5], [BF16] * 6,
                                                "merge_bwd", tr_merge, 512, zc)
    dz = jnp.concatenate([dz0, dz1, dz2], axis=1)
    gw["w_gate"] = _mm(dz, un, "tn", BF16, "d_w_gate")
    dun = _mm(dz, wts["w_gate"], "nn", F32, "d_un_gate")

    do_h = _mm(dyh, wts["w_hgrn_out"], "nn", F32, "d_o_h")
    gw["w_hgrn_out"] = _mm(dyh, o_h, "tn", BF16, "d_w_hgrn_out")
    do_f = _mm(dyf, wts["w_fox_out"], "nn", F32, "d_o_f")
    gw["w_fox_out"] = _mm(dyf, o_f, "tn", BF16, "d_w_fox_out")
    do_m = _mm(dym, wts["w_mem_out"], "nn", F32, "d_o_m")
    gw["w_mem_out"] = _mm(dym, o_m, "tn", BF16, "d_w_mem_out")

    do_raw, dhog, gs["hgrn_gnorm"] = _rowwise_bwd(_hpost_fn, [(o_raw, 0), (proj, CB_HOG)], [(small["hgrn_gnorm"], 0)],
                                                  [(do_h, 0)], [0, 1], [F32, BF16], "hgrn_post_bwd", tr_head, HD, NH)
    dhq, dhf, dhi, gs["hgrn_lb"], *carried = _hgrn_bwd(proj, small["hgrn_lb"], states, do_raw, ride_ffn2_grads)
    chip_sums(GROUP_FFN2, pairs_ffn2, carried)
    dfq, delta = _fox_bwd_dq(win, proj, ct, cq, lse, do_f)
    dfk, dfv, dc = _fox_bwd_dkv(win, proj, ct, cq, lse, delta, do_f)
    dff, dfb = _fox_cum_bwd(dc, proj, fb_pad)
    gs["fox_fb"] = dfb
    dmq, dmk, dmv = _mem_bwd(proj, mem_kv, do_m)

    dproj = jnp.concatenate([dhq, dhf, dhi, dhog, dfq, dfk, dfv, dff, dmq, jnp.zeros((T, HD), BF16)], axis=1)
    gw["w_in"] = _mm(un, dproj, "tn", BF16, "d_w_in")
    dun = _mm(dproj, wts["w_in"], "nt", F32, "d_un_proj", add=dun)
    dx0, dh1, gs["ffn1_post"], gs["mix_pre"] = _rowwise_bwd(
        ffn1_out, [(x, 0), (h1, 0)], [(small["ffn1_post"], None), (small["mix_pre"], None)], [(dx1, 0), (dun, 0)], [0, 1],
        [F32, BF16], "ffn1_post_mix_pre_bwd", tr, D, 1)

    dmem_kv = jnp.concatenate([dmk, dmv], axis=1)
    gw["w_mem_kv"] = _mm(memn, dmem_kv, "tn", BF16, "d_w_mem_kv")
    dmemn = _mm(dmem_kv, wts["w_mem_kv"], "nt", F32, "d_memn")
    _, gs["mem_norm"] = _rowwise_bwd(_norm_fn, [(mem, 0)], [(small["mem_norm"], None)], [(dmemn, 0)], [0], [BF16],
                                     "mem_norm_bwd", mem.shape[0], D, 1)

    mix = {}

    def mix_after_dwd(got):
        brought.update(mix=got)
        pairs, _ = pair_sums(GROUP_MIX, "mix")
        mix["names"] = GROUP_MIX[1:] + GROUP_MIX[:1]
        mix["pairs"] = pairs[1:] + pairs[:1]
        return tuple(_Rider(_ChipExchange([p.shape for p in part]), part, EXCHANGE_STEPS) for part in (pairs[1:], pairs[:1]))

    def own_exchange(dwg, dwu, dwd):
        gw.update(ffn1_wg=dwg, ffn1_wu=dwu, ffn1_wd=dwd)
        pairs, _ = pair_sums(GROUP_FFN1, "ffn1")
        first, second = pairs[:2], pairs[2:]

        def take(got_a, got_b):
            chip_sums(GROUP_FFN1, pairs, list(got_a) + list(got_b))

        return (_Rider(_ChipExchange([p.shape for p in first]), first, EXCHANGE_STEPS),
                _Rider(_ChipExchange([p.shape for p in second]), second, EXCHANGE_STEPS), take)

    dn1, (gw["ffn1_wg"], gw["ffn1_wu"], gw["ffn1_wd"]), carried = _ffn_backward(
        dh1, ffn1_saved, wts["ffn1_wg"], wts["ffn1_wu"], wts["ffn1_wd"], "ffn1",
        exchange=own_exchange if dist else None, rider_dwd=in_chip_rider(GROUP_MIX) if dist else None,
        after_dwd=mix_after_dwd if dist else None)
    chip_sums(mix.get("names", ()), mix.get("pairs"), carried)
    dx, gs["ffn1_pre"] = _rowwise_bwd(_norm_res_fn, [(x, 0)], [(small["ffn1_pre"], None)], [(dx0, 0), (dn1, 0)], [0], [F32],
                                      "ffn1_pre_bwd", tr, D, 1)
    return loss_part, dx, (reduced if dist else gw), gs


BIG = ("ffn1_wg", "ffn1_wu", "ffn1_wd", "w_in", "w_mem_kv", "w_hgrn_out", "w_fox_out", "w_mem_out", "w_gate", "w_o",
       "ffn2_wg", "ffn2_wu", "ffn2_wd")
TRANSPOSED = ("ffn1_wg", "ffn1_wu", "ffn2_wg", "ffn2_wu", "w_hgrn_out", "w_fox_out", "w_mem_out", "w_gate")
FFN_PAD = {"ffn1_wg": FP - F, "ffn1_wu": FP - F, "ffn1_wd": FP - F, "ffn2_wg": FP - F, "ffn2_wu": FP - F,
           "ffn2_wd": FP - F}
SMALL = GAINS + ("hgrn_lb", "hgrn_gnorm", "fox_fb")
WEIGHTS = ("ffn1_pre", "ffn1_post", "ffn1_wg", "ffn1_wu", "ffn1_wd", "mix_pre", "mix_post", "mem_norm", "w_in", "hgrn_lb",
           "hgrn_gnorm", "fox_fb", "w_mem_kv", "w_hgrn_out", "w_fox_out", "w_mem_out", "w_gate", "w_o", "ffn2_pre",
           "ffn2_post", "ffn2_wg", "ffn2_wu", "ffn2_wd")


def _to_gather_layout(name, w):
    if name in TRANSPOSED:
        w = w.T
    if name == "w_in":
        r = w.shape[0]
        w = jnp.concatenate([w[:, :MQ_COL], jnp.zeros((r, FF_COL + HD - MQ_COL), w.dtype), w[:, MQ_COL:],
                             jnp.zeros((r, P - FF_COL - HD - WM), w.dtype)], axis=1)
    return w.astype(BF16)


def _from_gather_layout(name, g):
    if name == "w_in":
        g = jnp.concatenate([g[:, :MQ_COL], g[:, FF_COL + HD:FF_COL + HD + WM]], axis=1)
    if name in TRANSPOSED:
        g = g.T
    return g


def kernel(x, mem, ffn1_pre, ffn1_post, ffn1_wg, ffn1_wu, ffn1_wd, mix_pre, mix_post, mem_norm, w_in, hgrn_lb, hgrn_gnorm, fox_fb, w_mem_kv, w_hgrn_out, w_fox_out, w_mem_out, w_gate, w_o, ffn2_pre, ffn2_post, ffn2_wg, ffn2_wu, ffn2_wd, loss_target, m_ffn1_pre, m_ffn1_post, m_ffn1_wg, m_ffn1_wu, m_ffn1_wd, m_mix_pre, m_mix_post, m_mem_norm, m_w_in, m_hgrn_lb, m_hgrn_gnorm, m_fox_fb, m_w_mem_kv, m_w_hgrn_out, m_w_fox_out, m_w_mem_out, m_w_gate, m_w_o, m_ffn2_pre, m_ffn2_post, m_ffn2_wg, m_ffn2_wu, m_ffn2_wd, v_ffn1_pre, v_ffn1_post, v_ffn1_wg, v_ffn1_wu, v_ffn1_wd, v_mix_pre, v_mix_post, v_mem_norm, v_w_in, v_hgrn_lb, v_hgrn_gnorm, v_fox_fb, v_w_mem_kv, v_w_hgrn_out, v_w_fox_out, v_w_mem_out, v_w_gate, v_w_o, v_ffn2_pre, v_ffn2_post, v_ffn2_wg, v_ffn2_wu, v_ffn2_wd):
    a = dict(locals())
    small = {n: a[n] for n in SMALL}
    shard = {n: a[n][0] if a[n].ndim == 3 else a[n] for n in BIG}

    blocks = {n: _to_gather_layout(n, shard[n]) for n in BIG}
    loss_part, dx, reduced, gs = _local_step(x[0], mem[0], loss_target[0], small, blocks=blocks)
    loss = lax.psum(0.5 / D * jnp.sum(loss_part), ("x", "y", "c"))

    grads, deltas, new_m, new_v = {}, {}, {}, {}
    for n in BIG:
        g = _from_gather_layout(n, reduced[n])
        d, m2, v2 = _adamw(shard[n], g, a["m_" + n].reshape(g.shape), a["v_" + n].reshape(g.shape), f"adamw_{n}")
        full = a[n].shape
        grads[n], deltas[n], new_m[n], new_v[n] = g.reshape(full), d.reshape(full), m2.reshape(full), v2.reshape(full)

    part = jnp.concatenate([_rows8(gs[n]) for n in GAINS] + [_rows8(gs["hgrn_lb"]), _rows8(gs["hgrn_gnorm"]),
                                                             _rows8(gs["fox_fb"][:, :NH])], axis=0)
    gsum = _all_reduce_small(part)

    def packed(prefix):
        lb = a[prefix + "hgrn_lb"]
        return _pack_small([a[prefix + n] for n in GAINS], lb[0], lb[1], a[prefix + "hgrn_gnorm"], a[prefix + "fox_fb"])

    g_p, d_p, m_p, v_p = _small_update(gsum, packed(""), packed("m_"), packed("v_"))
    for dst, p in ((grads, g_p), (deltas, d_p), (new_m, m_p), (new_v, v_p)):
        dst.update(_unpack_small(p))

    return (loss, dx[None], *[grads[n] for n in WEIGHTS], *[deltas[n] for n in WEIGHTS],
            *[new_m[n] for n in WEIGHTS], *[new_v[n] for n in WEIGHTS])
```

```python
import functools

import jax
import jax.numpy as jnp
from jax import lax
from jax.experimental import pallas as pl
from jax.experimental.pallas import tpu as pltpu

F32 = jnp.float32
BF16 = jnp.bfloat16
HIGHEST = lax.Precision.HIGHEST

NDEV = 8
D = 2048
F = 5504
FP = 5632
HD = 128
NH = 6
NM = 4
WH = NH * HD
WM = NM * HD
P = 6144
FF_COL = 5376
MQ_COL = 5382
CHUNK = 64
EPS = 1e-6
SCALE = HD ** -0.5
NEG = -1e30
VMEM_LIMIT = 48 * 1024 * 1024

CB_HQ, CB_HF, CB_HI, CB_HOG, CB_FQ, CB_FK, CB_FV, CB_FF, CB_MQ = 0, 6, 12, 18, 24, 30, 36, 42, 43

ADAM_LR, ADAM_B1, ADAM_B2, ADAM_EPS, ADAM_WD, ADAM_STEP = 0.001, 0.9, 0.999, 1e-08, 0.01, 10

NT = (((1,), (1,)), ((), ()))
NN = (((1,), (0,)), ((), ()))
TN = (((0,), (0,)), ((), ()))
MESH = pl.DeviceIdType.MESH


def _params(sem=None, **kw):
    return pltpu.CompilerParams(dimension_semantics=sem, vmem_limit_bytes=VMEM_LIMIT, **kw)


def _tile(n, prefs):
    for p in prefs:
        if p <= n and n % p == 0:
            return p
    return n


def _dot(a, b, dims):
    return lax.dot_general(a.astype(BF16), b.astype(BF16), dims, preferred_element_type=F32)


def _mm(a, b, mode, out_dtype, name, add=None, rider=None):
    if mode == "nn":
        (M, K), (K2, N) = a.shape, b.shape
    elif mode == "nt":
        (M, K), (N, K2) = a.shape, b.shape
    else:
        (K, M), (K2, N) = a.shape, b.shape
    assert K == K2, (a.shape, b.shape, mode)
    if mode == "tn":
        tm = _tile(M, (512, 256, 128))
        tn = _tile(N, (1024, 768, 512, 256, 128))
        tk = _tile(K, (4096, 2048, 1024, 512, 256, 128))
    else:
        tm = _tile(M, (1024, 512, 256, 128)) if K <= 2048 else _tile(M, (512, 256, 128))
        tn = _tile(N, (512, 768, 256, 128))
        tk = K if K <= 6144 else _tile(K, (2048, 1024, 512, 256, 128))
    nk = K // tk
    dims = {"nn": NN, "nt": NT, "tn": TN}[mode]
    has_add = add is not None

    ni, nj = M // tm, N // tn
    n_in = 3 if has_add else 2

    def body(*refs):
        step = (pl.program_id(0) * nj + pl.program_id(1)) * nk + pl.program_id(2)
        refs = _carry(rider, refs, n_in, 1, 1, step, ni * nj * nk)
        a_ref, b_ref = refs[0], refs[1]
        c_ref = refs[2] if has_add else None
        o_ref = refs[3] if has_add else refs[2]
        acc_ref = refs[-1]
        k = pl.program_id(2)
        part = _dot(a_ref[...], b_ref[...], dims)

        def finish(r):
            if has_add:
                r = r + c_ref[...].astype(F32)
            o_ref[...] = r.astype(o_ref.dtype)

        if nk == 1:
            finish(part)
        else:
            @pl.when(k == 0)
            def _():
                acc_ref[...] = part

            @pl.when(k > 0)
            def _():
                acc_ref[...] += part

            @pl.when(k == nk - 1)
            def _():
                finish(acc_ref[...])

    if mode == "nn":
        a_spec = pl.BlockSpec((tm, tk), lambda i, j, k: (i, k))
        b_spec = pl.BlockSpec((tk, tn), lambda i, j, k: (k, j))
    elif mode == "nt":
        a_spec = pl.BlockSpec((tm, tk), lambda i, j, k: (i, k))
        b_spec = pl.BlockSpec((tn, tk), lambda i, j, k: (j, k))
    else:
        a_spec = pl.BlockSpec((tk, tm), lambda i, j, k: (k, i))
        b_spec = pl.BlockSpec((tk, tn), lambda i, j, k: (k, j))
    o_spec = pl.BlockSpec((tm, tn), lambda i, j, k: (i, j))
    args = (a, b) + ((add,) if has_add else ())
    in_specs, out_specs, out_shape, scratch, extra = _with_rider(
        rider, [a_spec, b_spec] + ([o_spec] if has_add else []), [o_spec], [jax.ShapeDtypeStruct((M, N), out_dtype)],
        [pltpu.VMEM((tm, tn) if nk > 1 else (8, 128), F32)])
    out = pl.pallas_call(
        body, name=name, grid=(ni, nj, nk), in_specs=in_specs, out_specs=out_specs, out_shape=out_shape,
        scratch_shapes=scratch,
        compiler_params=_params(("arbitrary",) * 3 if rider else ("parallel", "parallel", "arbitrary"),
                                has_side_effects=rider is not None),
    )(*args, *extra)
    return out if rider else out[0]


class _Rider:
    def __init__(self, plan, inputs, steps):
        self.plan, self.inputs, self.steps = plan, list(inputs), steps
        self.n_out = len(plan.out_shape())
        self.n_sem = len(plan.sems())

    def run(self, step, total, in_refs, out_refs, sem_refs):
        n = self.plan.n
        if isinstance(self.plan, _Gather):
            args = (in_refs[:n], in_refs[n], out_refs) + tuple(sem_refs)
        else:
            args = (in_refs, out_refs) + tuple(sem_refs)
        for frac, method in self.steps:
            @pl.when(step == int(frac * (total - 1)))
            def _(method=method):
                getattr(self.plan, method)(*args)


GATHER_STEPS = ((0.0, "start"), (0.6, "forward"), (1.0, "finish"))
EXCHANGE_STEPS = ((0.0, "start"), (1.0, "finish"))


def _carry(rider, refs, n_in, n_out, n_scratch, step, total):
    if rider is None:
        return refs
    ri, ro, rs = len(rider.inputs), rider.n_out, rider.n_sem
    own_in, rid_in = refs[:n_in], refs[n_in:n_in + ri]
    own_out, rid_out = refs[n_in + ri:n_in + ri + n_out], refs[n_in + ri + n_out:n_in + ri + n_out + ro]
    own_scr, rid_sem = refs[n_in + ri + n_out + ro:n_in + ri + n_out + ro + n_scratch], refs[len(refs) - rs:]
    rider.run(step, total, rid_in, rid_out, rid_sem)
    return tuple(own_in) + tuple(own_out) + tuple(own_scr)


def _with_rider(rider, in_specs, out_specs, out_shape, scratch):
    if rider is None:
        return in_specs, out_specs, out_shape, scratch, ()
    any_spec = pl.BlockSpec(memory_space=pl.ANY)
    return (list(in_specs) + [any_spec] * len(rider.inputs), list(out_specs) + [any_spec] * rider.n_out,
            list(out_shape) + rider.plan.out_shape(), list(scratch) + rider.plan.sems(), tuple(rider.inputs))


def _ffn_up(n, wg_t, wu_t, name, rider=None):
    T = n.shape[0]
    tm = _tile(T, (1024, 512, 256, 128))
    tn = 512
    ni, nj = T // tm, FP // tn

    def body(*refs):
        step = pl.program_id(0) * nj + pl.program_id(1)
        n_ref, wg_ref, wu_ref, g_ref, u_ref, a_ref = _carry(rider, refs, 3, 3, 0, step, ni * nj)
        x = n_ref[...]
        g = _dot(x, wg_ref[...], NT)
        u = _dot(x, wu_ref[...], NT)
        g_ref[...] = g.astype(g_ref.dtype)
        u_ref[...] = u.astype(u_ref.dtype)
        a_ref[...] = (g * jax.nn.sigmoid(g) * u).astype(BF16)

    w_spec = pl.BlockSpec((tn, D), lambda i, j: (j, 0))
    o_spec = pl.BlockSpec((tm, tn), lambda i, j: (i, j))
    in_specs, out_specs, out_shape, scratch, extra = _with_rider(
        rider, [pl.BlockSpec((tm, D), lambda i, j: (i, 0)), w_spec, w_spec], [o_spec, o_spec, o_spec],
        [jax.ShapeDtypeStruct((T, FP), BF16)] * 3, [])
    return pl.pallas_call(
        body, name=name, grid=(ni, nj), in_specs=in_specs, out_specs=out_specs, out_shape=out_shape,
        scratch_shapes=scratch,
        compiler_params=_params(("arbitrary", "arbitrary") if rider else ("parallel", "parallel"),
                                has_side_effects=rider is not None),
    )(n, wg_t, wu_t, *extra)


def _ffn_act_bwd(dh, wd, g, u, name, rider=None):
    T = dh.shape[0]
    tm = _tile(T, (1024, 512, 256, 128))
    tn = 512
    ni, nj = T // tm, FP // tn

    def body(*refs):
        step = pl.program_id(0) * nj + pl.program_id(1)
        dh_ref, wd_ref, g_ref, u_ref, dg_ref, du_ref = _carry(rider, refs, 4, 2, 0, step, ni * nj)
        da = _dot(dh_ref[...], wd_ref[...], NT)
        g = g_ref[...].astype(F32)
        sg = jax.nn.sigmoid(g)
        dg_ref[...] = (da * u_ref[...].astype(F32) * (sg * (1.0 + g * (1.0 - sg)))).astype(dg_ref.dtype)
        du_ref[...] = (da * (g * sg)).astype(du_ref.dtype)

    tile = pl.BlockSpec((tm, tn), lambda i, j: (i, j))
    in_specs, out_specs, out_shape, scratch, extra = _with_rider(
        rider, [pl.BlockSpec((tm, D), lambda i, j: (i, 0)), pl.BlockSpec((tn, D), lambda i, j: (j, 0)), tile, tile],
        [tile, tile], [jax.ShapeDtypeStruct((T, FP), BF16), jax.ShapeDtypeStruct((T, FP), BF16)], [])
    return pl.pallas_call(
        body, name=name, grid=(ni, nj), in_specs=in_specs, out_specs=out_specs, out_shape=out_shape,
        scratch_shapes=scratch,
        compiler_params=_params(("arbitrary", "arbitrary") if rider else ("parallel", "parallel"),
                                has_side_effects=rider is not None),
    )(dh, wd, g, u, *extra)


def _row_specs(rows, tr, cw):
    return [pl.BlockSpec((tr, cw), lambda j, i, o=off: (i, o + j)) for _, off in rows]


def _const_specs(consts, cw):
    specs = []
    for arr, off in consts:
        if off is None:
            specs.append(pl.BlockSpec(arr.shape, lambda j, i: (0, 0)))
        else:
            specs.append(pl.BlockSpec((arr.shape[0], cw), lambda j, i, o=off: (0, o + j)))
    return specs


def _rowwise(fn, rows, consts, out_dtypes, name, tr, cw, ncol):
    T = rows[0][0].shape[0]
    nr, nc = len(rows), len(consts)

    def body(*refs):
        r = [x[...].astype(F32) for x in refs[:nr]]
        c = [x[...] for x in refs[nr:nr + nc]]
        res = fn(*r, *c)
        for o_ref, v in zip(refs[nr + nc:], res):
            o_ref[...] = v.astype(o_ref.dtype)

    o_spec = pl.BlockSpec((tr, cw), lambda j, i: (i, j))
    return pl.pallas_call(
        body, name=name, grid=(ncol, T // tr),
        in_specs=_row_specs(rows, tr, cw) + _const_specs(consts, cw),
        out_specs=[o_spec] * len(out_dtypes),
        out_shape=[jax.ShapeDtypeStruct((T, ncol * cw), dt) for dt in out_dtypes],
        compiler_params=_params(("parallel", "parallel")),
    )(*[a for a, _ in rows], *[a for a, _ in consts])


def _rowwise_bwd(fn, rows, consts, cots, diff, ddtypes, name, tr, cw, ncol):
    T = rows[0][0].shape[0]
    nr, nc, nt, nd = len(rows), len(consts), len(cots), len(diff)

    def body(*refs):
        r = [x[...].astype(F32) for x in refs[:nr]]
        c = [x[...] for x in refs[nr:nr + nc]]
        ct = [x[...].astype(F32) for x in refs[nr + nc:nr + nc + nt]]
        drow_refs = refs[nr + nc + nt:nr + nc + nt + nd]
        dconst_refs = refs[nr + nc + nt + nd:]
        i = pl.program_id(1)

        def f(*args):
            full = list(r)
            for idx, a in zip(diff, args[:nd]):
                full[idx] = a
            return tuple(fn(*full, *args[nd:]))

        _, vjp = jax.vjp(f, *[r[d] for d in diff], *c)
        g = vjp(tuple(ct))
        for o_ref, v in zip(drow_refs, g[:nd]):
            o_ref[...] = v.astype(o_ref.dtype)

        @pl.when(i == 0)
        def _():
            for o_ref in dconst_refs:
                o_ref[...] = jnp.zeros_like(o_ref)

        for o_ref, v in zip(dconst_refs, g[nd:]):
            o_ref[...] += v

    o_spec = pl.BlockSpec((tr, cw), lambda j, i: (i, j))
    out_shape = [jax.ShapeDtypeStruct((T, ncol * cw), dt) for dt in ddtypes]
    out_shape += [jax.ShapeDtypeStruct(a.shape, F32) for a, _ in consts]
    return pl.pallas_call(
        body, name=name, grid=(ncol, T // tr),
        in_specs=_row_specs(rows, tr, cw) + _const_specs(consts, cw) + _row_specs(cots, tr, cw),
        out_specs=[o_spec] * nd + _const_specs(consts, cw),
        out_shape=out_shape,
        compiler_params=_params(("parallel", "arbitrary")),
    )(*[a for a, _ in rows], *[a for a, _ in consts], *[a for a, _ in cots])


def _rms(x, g):
    return x * lax.rsqrt(jnp.mean(x * x, axis=-1, keepdims=True) + EPS) * g


def _silu(x):
    return x * jax.nn.sigmoid(x)


def _norm_fn(x, g):
    return (_rms(x, g),)


def _norm_res_fn(x, g):
    return (x, _rms(x, g))


def _post_pre_fn(scale, x, h, g_post, g_pre):
    xn = x + scale * _rms(h, g_post)
    return (xn, _rms(xn, g_pre))


def _resid_h_fn(scale, h, g):
    return (scale * _rms(h, g),)


def _hpost_fn(o, hog, gn):
    return (_rms(o, gn) * _silu(hog),)


def _merge_fn(z0, z1, z2, yh, yf, ym):
    return (jax.nn.sigmoid(z0) * yh + jax.nn.sigmoid(z1) * yf + jax.nn.sigmoid(z2) * ym,)


def _loss(x2, h, g_post, tgt, name):
    T = x2.shape[0]
    tr = _tile(T, (256, 128))

    def body(x_ref, h_ref, g_ref, t_ref, dy_ref, s_ref):
        i = pl.program_id(0)
        e = x_ref[...] + 0.5 * _rms(h_ref[...], g_ref[...]) - t_ref[...]
        dy_ref[...] = e * (1.0 / D)
        col = jnp.sum(e * e, axis=0, keepdims=True)
        tot = col[:, 0:HD]
        for k in range(1, D // HD):
            tot = tot + col[:, k * HD:(k + 1) * HD]

        @pl.when(i == 0)
        def _():
            s_ref[...] = jnp.zeros_like(s_ref)

        s_ref[...] += tot

    spec = pl.BlockSpec((tr, D), lambda i: (i, 0))
    return pl.pallas_call(
        body, name=name, grid=(T // tr,), in_specs=[spec, spec, pl.BlockSpec((1, D), lambda i: (0, 0)), spec],
        out_specs=[spec, pl.BlockSpec((1, HD), lambda i: (0, 0))],
        out_shape=[jax.ShapeDtypeStruct((T, D), F32), jax.ShapeDtypeStruct((1, HD), F32)],
        compiler_params=_params(("arbitrary",)),
    )(x2, h, g_post, tgt)


def _lower_bound(lb_ref):
    a0 = lb_ref[0:1, :]
    a1 = lb_ref[1:2, :]
    mx = jnp.maximum(a0, a1)
    e0 = jnp.exp(a0 - mx)
    return e0 / (e0 + jnp.exp(a1 - mx))


def _hgrn_prep(hq, hf, lb):
    g = lb + (1.0 - lb) * jax.nn.sigmoid(hf)
    return _silu(hq), 1.0 - g, jnp.log(g)


def _tri(n, upper):
    r = lax.broadcasted_iota(jnp.int32, (n, n), 0)
    c = lax.broadcasted_iota(jnp.int32, (n, n), 1)
    return (c >= r) if upper else (c <= r)


def _hgrn_factors(q, k, gl):
    low = _tri(CHUNK, False)
    b = lax.dot_general(low.astype(F32), gl, NN, precision=HIGHEST, preferred_element_type=F32)
    bl = b[CHUNK - 1:CHUNK, :]
    ref = b[CHUNK // 2 - 1:CHUNK // 2, :]
    eb = jnp.exp(b)
    ea = jnp.exp(b - ref)
    ebn = jnp.exp(ref - b)
    ek = jnp.exp(bl - b)
    ebl = jnp.exp(bl)
    return low, eb, ea, ebn, ek, ebl


def _hgrn_fwd(proj, hgrn_lb, rider=None):
    T = proj.shape[0]
    cb = _tile(T, (512, 256, 128, 64))
    nchunk = cb // CHUNK

    def body(*refs):
        hq_ref, hf_ref, hi_ref, lb_ref, o_ref, st_ref, state = _carry(rider, refs, 4, 2, 1, pl.program_id(0), T // cb)

        @pl.when(pl.program_id(0) == 0)
        def _():
            state[...] = jnp.zeros_like(state)

        lb = _lower_bound(lb_ref)

        def chunk(c, carry):
            r0 = pl.multiple_of(c * CHUNK, CHUNK)
            for h in range(NH):
                cols = slice(h * HD, (h + 1) * HD)
                q, k, gl = _hgrn_prep(hq_ref[pl.ds(r0, CHUNK), cols], hf_ref[pl.ds(r0, CHUNK), cols], lb[:, cols])
                v = hi_ref[pl.ds(r0, CHUNK), cols]
                low, eb, ea, ebn, ek, ebl = _hgrn_factors(q, k, gl)
                s_t = state[h]
                st_ref[c, h] = s_t
                pm = jnp.where(low, _dot(q * ea, k * ebn, NT), 0.0)
                o_ref[pl.ds(r0, CHUNK), cols] = _dot(q * eb, s_t, NT) + _dot(pm, v, NN)
                state[h] = s_t * ebl + _dot(v, k * ek, TN)
            return carry

        lax.fori_loop(0, nchunk, chunk, 0)

    def col(off):
        return pl.BlockSpec((cb, WH), lambda i, o=off: (i, o))

    in_specs, out_specs, out_shape, scratch, extra = _with_rider(
        rider, [col(0), col(1), col(2), pl.BlockSpec((2, WH), lambda i: (0, 0))],
        [pl.BlockSpec((cb, WH), lambda i: (i, 0)), pl.BlockSpec((nchunk, NH, HD, HD), lambda i: (i, 0, 0, 0))],
        [jax.ShapeDtypeStruct((T, WH), F32), jax.ShapeDtypeStruct((T // CHUNK, NH, HD, HD), F32)],
        [pltpu.VMEM((NH, HD, HD), F32)])
    return pl.pallas_call(
        body, name="hgrn_fwd", grid=(T // cb,), in_specs=in_specs, out_specs=out_specs, out_shape=out_shape,
        scratch_shapes=scratch, compiler_params=_params(("arbitrary",), has_side_effects=rider is not None),
    )(proj, proj, proj, hgrn_lb, *extra)


def _hgrn_bwd(proj, hgrn_lb, states, do, rider=None):
    T = proj.shape[0]
    cb = _tile(T, (512, 256, 128, 64))
    nchunk = cb // CHUNK
    nb = T // cb

    def body(*refs):
        (hq_ref, hf_ref, hi_ref, lb_ref, st_ref, do_ref, dhq_ref, dhf_ref, dhi_ref, dlb_ref,
         dstate) = _carry(rider, refs, 6, 4, 1, pl.program_id(0), nb)

        @pl.when(pl.program_id(0) == 0)
        def _():
            dstate[...] = jnp.zeros_like(dstate)
            dlb_ref[...] = jnp.zeros_like(dlb_ref)

        lb = _lower_bound(lb_ref)
        up = _tri(CHUNK, True)
        last = lax.broadcasted_iota(jnp.int32, (CHUNK, HD), 0) == CHUNK - 1

        def chunk(cc, carry):
            c = nchunk - 1 - cc
            r0 = pl.multiple_of(c * CHUNK, CHUNK)
            for h in range(NH):
                cols = slice(h * HD, (h + 1) * HD)
                hq = hq_ref[pl.ds(r0, CHUNK), cols]
                hf = hf_ref[pl.ds(r0, CHUNK), cols]
                (q, k, gl), prep_vjp = jax.vjp(_hgrn_prep, hq, hf, lb[:, cols])
                v = hi_ref[pl.ds(r0, CHUNK), cols]
                d_o = do_ref[pl.ds(r0, CHUNK), cols]
                low, eb, ea, ebn, ek, ebl = _hgrn_factors(q, k, gl)
                s_t = st_ref[c, h]
                ds_new = dstate[h]
                qe, am, bm, kb = q * eb, q * ea, k * ebn, k * ek
                pm_t = jnp.where(up, _dot(bm, am, NT), 0.0)
                dp = jnp.where(low, _dot(d_o, v, NT), 0.0)
                dp_t = jnp.where(up, _dot(v, d_o, NT), 0.0)
                dqe = _dot(d_o, s_t, NN)
                da = _dot(dp, bm, NN)
                db_m = _dot(dp_t, am, NN)
                dkb = _dot(v, ds_new, NN)
                dv = _dot(pm_t, d_o, NN) + _dot(kb, ds_new, NT)
                dq = dqe * eb + da * ea
                dk = db_m * ebn + dkb * ek
                dbl = jnp.sum(dkb * kb, axis=0, keepdims=True) + jnp.sum(ds_new * s_t, axis=0, keepdims=True) * ebl
                db = (dqe * qe + da * am.astype(BF16).astype(F32) - db_m * bm.astype(BF16).astype(F32) - dkb * kb
                      + jnp.where(last, dbl, 0.0))
                dgl = lax.dot_general(up.astype(F32), db, NN, precision=HIGHEST, preferred_element_type=F32)
                dhq, dhf, dlb = prep_vjp((dq, dk, dgl))
                dhq_ref[pl.ds(r0, CHUNK), cols] = dhq.astype(dhq_ref.dtype)
                dhf_ref[pl.ds(r0, CHUNK), cols] = dhf.astype(dhf_ref.dtype)
                dhi_ref[pl.ds(r0, CHUNK), cols] = dv.astype(dhi_ref.dtype)
                dlb_ref[:, cols] += dlb
                dstate[h] = _dot(d_o, qe, TN) + ds_new * ebl
            return carry

        lax.fori_loop(0, nchunk, chunk, 0)

    def col(off):
        return pl.BlockSpec((cb, WH), lambda i, o=off: (nb - 1 - i, o))

    row = pl.BlockSpec((cb, WH), lambda i: (nb - 1 - i, 0))
    in_specs, out_specs, out_shape, scratch, extra = _with_rider(
        rider, [col(0), col(1), col(2), pl.BlockSpec((2, WH), lambda i: (0, 0)),
                pl.BlockSpec((nchunk, NH, HD, HD), lambda i: (nb - 1 - i, 0, 0, 0)), row],
        [row, row, row, pl.BlockSpec((1, WH), lambda i: (0, 0))],
        [jax.ShapeDtypeStruct((T, WH), BF16)] * 3 + [jax.ShapeDtypeStruct((1, WH), F32)], [pltpu.VMEM((NH, HD, HD), F32)])
    return pl.pallas_call(
        body, name="hgrn_bwd", grid=(nb,), in_specs=in_specs, out_specs=out_specs, out_shape=out_shape,
        scratch_shapes=scratch, compiler_params=_params(("arbitrary",), has_side_effects=rider is not None),
    )(proj, proj, proj, hgrn_lb, states, do, *extra)


def _log_sigmoid(z):
    return jnp.minimum(z, 0.0) - jnp.log(1.0 + jnp.exp(-jnp.abs(z)))


def _fox_cum(proj, fb_pad):
    T = proj.shape[0]
    tb = _tile(T, (256, 128))

    def body(ff_ref, fb_ref, ct_ref, cq_ref, carry):
        @pl.when(pl.program_id(0) == 0)
        def _():
            carry[...] = jnp.zeros_like(carry)

        lf = _log_sigmoid(ff_ref[...] + fb_ref[...])
        cs = lax.dot_general(_tri(tb, False).astype(F32), lf, NN, precision=HIGHEST,
                             preferred_element_type=F32) + carry[0:1, :]
        carry[0:1, :] = cs[tb - 1:tb, :]
        ct_ref[...] = cs.T[0:8, :]
        for h in range(NH):
            cq_ref[h] = jnp.broadcast_to(cs[:, h:h + 1], (tb, HD))

    return pl.pallas_call(
        body, name="fox_cum", grid=(T // tb,),
        in_specs=[pl.BlockSpec((tb, HD), lambda i: (i, CB_FF)), pl.BlockSpec((1, HD), lambda i: (0, 0))],
        out_specs=[pl.BlockSpec((8, tb), lambda i: (0, i)), pl.BlockSpec((NH, tb, HD), lambda i: (0, i, 0))],
        out_shape=[jax.ShapeDtypeStruct((8, T), F32), jax.ShapeDtypeStruct((NH, T, HD), F32)],
        scratch_shapes=[pltpu.VMEM((8, HD), F32)],
        compiler_params=_params(("arbitrary",)),
    )(proj, fb_pad)


def _fox_cum_bwd(dc, proj, fb_pad):
    T = proj.shape[0]
    tb = _tile(T, (256, 128))
    nb = T // tb

    def body(dc_ref, ff_ref, fb_ref, dff_ref, dfb_ref, carry):
        @pl.when(pl.program_id(0) == 0)
        def _():
            carry[...] = jnp.zeros_like(carry)
            dfb_ref[...] = jnp.zeros_like(dfb_ref)

        rid = lax.broadcasted_iota(jnp.int32, (8, tb), 0)
        m8 = jnp.zeros((8, tb), F32)
        for h in range(NH):
            m8 = m8 + jnp.where(rid == h, dc_ref[h], 0.0)
        dcb = jnp.concatenate([m8, jnp.zeros((HD - 8, tb), F32)], axis=0).T
        rev = lax.dot_general(_tri(tb, True).astype(F32), dcb, NN, precision=HIGHEST,
                              preferred_element_type=F32) + carry[0:1, :]
        carry[0:1, :] = rev[0:1, :]
        dff = rev * jax.nn.sigmoid(-(ff_ref[...] + fb_ref[...]))
        dff_ref[...] = dff.astype(dff_ref.dtype)
        dfb_ref[...] += jnp.sum(dff, axis=0, keepdims=True)

    return pl.pallas_call(
        body, name="fox_cum_bwd", grid=(nb,),
        in_specs=[pl.BlockSpec((NH, 8, tb), lambda i: (0, 0, nb - 1 - i)),
                  pl.BlockSpec((tb, HD), lambda i: (nb - 1 - i, CB_FF)), pl.BlockSpec((1, HD), lambda i: (0, 0))],
        out_specs=[pl.BlockSpec((tb, HD), lambda i: (nb - 1 - i, 0)), pl.BlockSpec((1, HD), lambda i: (0, 0))],
        out_shape=[jax.ShapeDtypeStruct((T, HD), BF16), jax.ShapeDtypeStruct((1, HD), F32)],
        scratch_shapes=[pltpu.VMEM((8, HD), F32)],
        compiler_params=_params(("arbitrary",)),
    )(dc, proj, fb_pad)


STRIP = 128


def _fox_scores(q, k, cq, ck, i, j, bq, bk, r0=0):
    rows = q.shape[0]
    s = _dot(q, k, NT) * SCALE + (cq - ck)
    diff = lax.broadcasted_iota(jnp.int32, (rows, bk), 1) - lax.broadcasted_iota(jnp.int32, (rows, bk), 0)
    return jnp.where(diff <= i * bq + r0 - j * bk, s, NEG)


def _heads(h):
    return slice(h * HD, (h + 1) * HD)


UNDERFLOW = -105.0


def _fox_windows(proj, cq):
    T = proj.shape[0]
    bq = _tile(T, (512, 256, 128))
    nq = T // bq
    assert nq <= HD

    def body(q_ref, k_ref, cq_ref, jlo_ref, ihi_ref, norm_s, cs_s, ce_s):
        i = pl.program_id(0)

        @pl.when(i == 0)
        def _():
            norm_s[...] = jnp.zeros_like(norm_s)
            cs_s[...] = jnp.zeros_like(cs_s)
            ce_s[...] = jnp.zeros_like(ce_s)

        lane = lax.broadcasted_iota(jnp.int32, (1, HD), 1)
        for h in range(NH):
            for row, ref in ((h, q_ref), (8 + h, k_ref)):
                x = ref[:, _heads(h)]
                biggest = jnp.max(jnp.sum(x * x, axis=1, keepdims=True), axis=0, keepdims=True)
                norm_s[row:row + 1, :] = jnp.maximum(norm_s[row:row + 1, :], jnp.broadcast_to(biggest, (1, HD)))
            cs_s[h, pl.ds(i, 1), :] = cq_ref[h, 0:1, :]
            ce_s[h:h + 1, :] = jnp.where(lane == i, cq_ref[h, bq - 1:bq, :], ce_s[h:h + 1, :])

        @pl.when(i == nq - 1)
        def _():
            rows = lax.broadcasted_iota(jnp.int32, (HD, HD), 0)
            cols = lax.broadcasted_iota(jnp.int32, (HD, HD), 1)
            need = cols == rows
            for h in range(NH):
                slack = 2.05 * SCALE * jnp.sqrt(norm_s[h:h + 1, :] * norm_s[8 + h:9 + h, :])
                bound = cs_s[h] - ce_s[h:h + 1, :] + slack
                need = need | ((bound >= UNDERFLOW) & (cols < rows))
            need = need & (rows < nq) & (cols < nq)
            jlo = jnp.min(jnp.where(need, cols, HD).astype(F32), axis=1, keepdims=True)
            ihi = jnp.max(jnp.where(need, rows, -1).astype(F32), axis=0, keepdims=True)
            jlo_ref[...] = jnp.broadcast_to(jlo, (HD, HD)).astype(jnp.int32)
            ihi_ref[...] = jnp.broadcast_to(ihi, (8, HD)).astype(jnp.int32)

    jlo, ihi = pl.pallas_call(
        body, name="fox_windows", grid=(nq,),
        in_specs=[pl.BlockSpec((bq, WH), lambda i: (i, CB_FQ // NH)), pl.BlockSpec((bq, WH), lambda i: (i, CB_FK // NH)),
                  pl.BlockSpec((NH, bq, HD), lambda i: (0, i, 0))],
        out_specs=[pl.BlockSpec((HD, HD), lambda i: (0, 0)), pl.BlockSpec((8, HD), lambda i: (0, 0))],
        out_shape=[jax.ShapeDtypeStruct((HD, HD), jnp.int32), jax.ShapeDtypeStruct((8, HD), jnp.int32)],
        scratch_shapes=[pltpu.VMEM((16, HD), F32), pltpu.VMEM((NH, HD, HD), F32), pltpu.VMEM((8, HD), F32)],
        compiler_params=_params(("arbitrary",)),
    )(proj, proj, cq)
    return jnp.concatenate([jlo[:nq, 0], ihi[0, :nq]])


def _fox_fwd(win, proj, ct, cq):
    T = proj.shape[0]
    bq = bk = _tile(T, (512, 256, 128))
    nq = nk = T // bq

    def body(win_ref, q_ref, k_ref, v_ref, ct_ref, cq_ref, o_ref, lse_ref, m_s, l_s, acc_s):
        i, jj = pl.program_id(0), pl.program_id(1)
        j = win_ref[i] + jj

        @pl.when(jj == 0)
        def _():
            m_s[...] = jnp.full_like(m_s, NEG)
            l_s[...] = jnp.zeros_like(l_s)
            acc_s[...] = jnp.zeros_like(acc_s)

        @pl.when(j <= i)
        def _():
            for h in range(NH):
                hs = _heads(h)
                k, v, ck = k_ref[:, hs], v_ref[:, hs], ct_ref[h:h + 1, :]
                for r0 in range(0, bq, STRIP):
                    rs = slice(r0, r0 + STRIP)
                    s = _fox_scores(q_ref[rs, hs], k, cq_ref[h, rs, 0:1], ck, i, j, bq, bk, r0)
                    m_prev = m_s[h, rs]
                    m_new = jnp.maximum(m_prev, jnp.max(s, axis=1, keepdims=True))
                    alpha = jnp.exp(m_prev - m_new)
                    p = jnp.exp(s - m_new)
                    l_s[h, rs] = alpha * l_s[h, rs] + jnp.sum(p, axis=1, keepdims=True)
                    acc_s[rs, hs] = alpha * acc_s[rs, hs] + _dot(p, v, NN)
                    m_s[h, rs] = m_new

        @pl.when(jj == nk - 1)
        def _():
            for h in range(NH):
                o_ref[:, _heads(h)] = acc_s[:, _heads(h)] / l_s[h]
                lse_ref[h] = jnp.broadcast_to(m_s[h] + jnp.log(l_s[h]), (bq, HD))

    def key_block(i, jj, win):
        return jnp.minimum(win[i] + jj, i)

    def kv(off):
        return pl.BlockSpec((bk, WH), lambda i, jj, win, o=off // NH: (key_block(i, jj, win), o))

    stat = pl.BlockSpec((NH, bq, HD), lambda i, jj, win: (0, i, 0))
    return pl.pallas_call(
        body, name="fox_fwd",
        grid_spec=pltpu.PrefetchScalarGridSpec(
            num_scalar_prefetch=1, grid=(nq, nk),
            in_specs=[pl.BlockSpec((bq, WH), lambda i, jj, win: (i, CB_FQ // NH)), kv(CB_FK), kv(CB_FV),
                      pl.BlockSpec((8, bk), lambda i, jj, win: (0, key_block(i, jj, win))), stat],
            out_specs=[pl.BlockSpec((bq, WH), lambda i, jj, win: (i, 0)), stat],
            scratch_shapes=[pltpu.VMEM((NH, bq, 1), F32), pltpu.VMEM((NH, bq, 1), F32), pltpu.VMEM((bq, WH), F32)]),
        out_shape=[jax.ShapeDtypeStruct((T, WH), F32), jax.ShapeDtypeStruct((NH, T, HD), F32)],
        compiler_params=_params(("parallel", "arbitrary")),
    )(win, proj, proj, proj, ct, cq)


def _fox_bwd_dq(win, proj, ct, cq, lse, do):
    T = proj.shape[0]
    bq = bk = _tile(T, (512, 256, 128))
    nq = nk = T // bq

    def body(win_ref, q_ref, k_ref, v_ref, ct_ref, cq_ref, lse_ref, do_ref, dq_ref, delta_ref, acc_s, delta_s, psum_s):
        i, jj = pl.program_id(0), pl.program_id(1)
        j = win_ref[i] + jj % nk

        @pl.when(jj == 0)
        def _():
            acc_s[...] = jnp.zeros_like(acc_s)
            delta_s[...] = jnp.zeros_like(delta_s)
            psum_s[...] = jnp.zeros_like(psum_s)

        half = min(bq, 2 * STRIP)

        def probs(h, r0):
            hs, rs = _heads(h), slice(r0, r0 + half)
            k = k_ref[:, hs]
            s = _fox_scores(q_ref[rs, hs], k, cq_ref[h, rs, 0:1], ct_ref[h:h + 1, :], i, j, bq, bk, r0)
            return k, jnp.exp(s - lse_ref[h, rs, 0:1]), _dot(do_ref[rs, hs], v_ref[:, hs], NT)

        @pl.when((j <= i) & (jj < nk))
        def _():
            for h in range(NH):
                for r0 in range(0, bq, half):
                    rs = slice(r0, r0 + half)
                    _, p, dp = probs(h, r0)
                    delta_s[h, rs] += jnp.sum(p * dp, axis=1, keepdims=True)
                    psum_s[h, rs] += jnp.sum(p, axis=1, keepdims=True)

        @pl.when((j <= i) & (jj >= nk))
        def _():
            for h in range(NH):
                for r0 in range(0, bq, half):
                    rs = slice(r0, r0 + half)
                    k, p, dp = probs(h, r0)
                    ds = p * (dp - delta_s[h, rs] / psum_s[h, rs])
                    acc_s[rs, _heads(h)] += _dot(ds, k, NN) * SCALE

        @pl.when(jj == 2 * nk - 1)
        def _():
            dq_ref[...] = acc_s[...].astype(dq_ref.dtype)
            for h in range(NH):
                delta_ref[h] = jnp.broadcast_to(delta_s[h] / psum_s[h], (bq, HD))

    def key_block(i, jj, win):
        return jnp.minimum(win[i] + jj % nk, i)

    def kv(off):
        return pl.BlockSpec((bk, WH), lambda i, jj, win, o=off // NH: (key_block(i, jj, win), o))

    qrow = pl.BlockSpec((bq, WH), lambda i, jj, win: (i, 0))
    stat = pl.BlockSpec((NH, bq, HD), lambda i, jj, win: (0, i, 0))
    return pl.pallas_call(
        body, name="fox_bwd_dq",
        grid_spec=pltpu.PrefetchScalarGridSpec(
            num_scalar_prefetch=1, grid=(nq, 2 * nk),
            in_specs=[pl.BlockSpec((bq, WH), lambda i, jj, win: (i, CB_FQ // NH)), kv(CB_FK), kv(CB_FV),
                      pl.BlockSpec((8, bk), lambda i, jj, win: (0, key_block(i, jj, win))), stat, stat, qrow],
            out_specs=[qrow, stat],
            scratch_shapes=[pltpu.VMEM((bq, WH), F32), pltpu.VMEM((NH, bq, 1), F32), pltpu.VMEM((NH, bq, 1), F32)]),
        out_shape=[jax.ShapeDtypeStruct((T, WH), BF16), jax.ShapeDtypeStruct((NH, T, HD), F32)],
        compiler_params=_params(("parallel", "arbitrary")),
    )(win, proj, proj, proj, ct, cq, lse, do)


def _fox_bwd_dkv(win, proj, ct, cq, lse, delta, do):
    T = proj.shape[0]
    bq = bk = _tile(T, (512, 256, 128))
    nq = nk = T // bq

    def body(win_ref, q_ref, k_ref, v_ref, ct_ref, cq_ref, lse_ref, delta_ref, do_ref, dk_ref, dv_ref, dc_ref,
             dk_s, dv_s, dc_s):
        j, ii = pl.program_id(0), pl.program_id(1)
        i = j + ii

        @pl.when(ii == 0)
        def _():
            dk_s[...] = jnp.zeros_like(dk_s)
            dv_s[...] = jnp.zeros_like(dv_s)
            dc_s[...] = jnp.zeros_like(dc_s)

        @pl.when(i <= win_ref[nq + j])
        def _():
            for h in range(NH):
                hs = _heads(h)
                q = q_ref[:, hs]
                d_o = do_ref[:, hs]
                s = _fox_scores(q, k_ref[:, hs], cq_ref[h, :, 0:1], ct_ref[h:h + 1, :], i, j, bq, bk)
                p = jnp.exp(s - lse_ref[h, :, 0:1])
                dv_s[:, hs] += _dot(p, d_o, TN)
                dp = _dot(d_o, v_ref[:, hs], NT)
                ds = p * (dp - delta_ref[h, :, 0:1])
                dk_s[:, hs] += _dot(ds, q, TN) * SCALE
                dc_s[h:h + 1, :] -= jnp.sum(ds, axis=0, keepdims=True)

        @pl.when(ii == nq - 1)
        def _():
            dk_ref[...] = dk_s[...].astype(dk_ref.dtype)
            dv_ref[...] = dv_s[...].astype(dv_ref.dtype)
            for h in range(NH):
                dc_ref[h] = jnp.broadcast_to(dc_s[h:h + 1, :], (8, bk))

    def query_block(j, ii, win):
        return jnp.minimum(j + ii, win[nq + j])

    def kv(off):
        return pl.BlockSpec((bk, WH), lambda j, ii, win, o=off // NH: (j, o))

    qrow = pl.BlockSpec((bq, WH), lambda j, ii, win: (query_block(j, ii, win), 0))
    stat = pl.BlockSpec((NH, bq, HD), lambda j, ii, win: (0, query_block(j, ii, win), 0))
    krow = pl.BlockSpec((bk, WH), lambda j, ii, win: (j, 0))
    return pl.pallas_call(
        body, name="fox_bwd_dkv",
        grid_spec=pltpu.PrefetchScalarGridSpec(
            num_scalar_prefetch=1, grid=(nk, nq),
            in_specs=[pl.BlockSpec((bq, WH), lambda j, ii, win: (query_block(j, ii, win), CB_FQ // NH)), kv(CB_FK),
                      kv(CB_FV), pl.BlockSpec((8, bk), lambda j, ii, win: (0, j)), stat, stat, stat, qrow],
            out_specs=[krow, krow, pl.BlockSpec((NH, 8, bk), lambda j, ii, win: (0, 0, j))],
            scratch_shapes=[pltpu.VMEM((bk, WH), F32), pltpu.VMEM((bk, WH), F32), pltpu.VMEM((8, bk), F32)]),
        out_shape=[jax.ShapeDtypeStruct((T, WH), BF16), jax.ShapeDtypeStruct((T, WH), BF16),
                   jax.ShapeDtypeStruct((NH, 8, T), F32)],
        compiler_params=_params(("parallel", "arbitrary")),
    )(win, proj, proj, proj, ct, cq, lse, delta, do)


def _mem_probs(q, mk):
    s = _dot(q, mk, NT) * SCALE
    e = jnp.exp(s - jnp.max(s, axis=1, keepdims=True))
    return e / jnp.sum(e, axis=1, keepdims=True)


def _mem_fwd(proj, mem_kv):
    T = proj.shape[0]
    tr = _tile(T, (512, 256, 128))
    M = mem_kv.shape[0]

    def body(q_ref, mk_ref, mv_ref, o_ref):
        o_ref[...] = _dot(_mem_probs(q_ref[...], mk_ref[...]), mv_ref[...], NN)

    return pl.pallas_call(
        body, name="mem_fwd", grid=(NM, T // tr),
        in_specs=[pl.BlockSpec((tr, HD), lambda h, i: (i, CB_MQ + h)),
                  pl.BlockSpec((M, HD), lambda h, i: (0, h)), pl.BlockSpec((M, HD), lambda h, i: (0, NM + h))],
        out_specs=pl.BlockSpec((tr, HD), lambda h, i: (i, h)),
        out_shape=jax.ShapeDtypeStruct((T, WM), F32),
        compiler_params=_params(("parallel", "parallel")),
    )(proj, mem_kv, mem_kv)


def _mem_bwd(proj, mem_kv, do):
    T = proj.shape[0]
    tr = _tile(T, (512, 256, 128))
    M = mem_kv.shape[0]

    def body(q_ref, mk_ref, mv_ref, do_ref, dq_ref, dmk_ref, dmv_ref):
        @pl.when(pl.program_id(1) == 0)
        def _():
            dmk_ref[...] = jnp.zeros_like(dmk_ref)
            dmv_ref[...] = jnp.zeros_like(dmv_ref)

        q, mk, d_o = q_ref[...], mk_ref[...], do_ref[...]
        p = _mem_probs(q, mk)
        dmv_ref[...] += _dot(p, d_o, TN)
        dp = _dot(d_o, mv_ref[...], NT)
        ds = p * (dp - jnp.sum(p * dp, axis=1, keepdims=True))
        dq_ref[...] = (_dot(ds, mk, NN) * SCALE).astype(dq_ref.dtype)
        dmk_ref[...] += _dot(ds, q, TN) * SCALE

    acc = pl.BlockSpec((M, HD), lambda h, i: (0, h))
    row = pl.BlockSpec((tr, HD), lambda h, i: (i, h))
    return pl.pallas_call(
        body, name="mem_bwd", grid=(NM, T // tr),
        in_specs=[pl.BlockSpec((tr, HD), lambda h, i: (i, CB_MQ + h)),
                  pl.BlockSpec((M, HD), lambda h, i: (0, h)), pl.BlockSpec((M, HD), lambda h, i: (0, NM + h)), row],
        out_specs=[row, acc, acc],
        out_shape=[jax.ShapeDtypeStruct((T, WM), BF16), jax.ShapeDtypeStruct((M, WM), F32),
                   jax.ShapeDtypeStruct((M, WM), F32)],
        compiler_params=_params(("parallel", "arbitrary")),
    )(proj, mem_kv, mem_kv, do)


def _mesh_place():
    x, y, c = lax.axis_index("x"), lax.axis_index("y"), lax.axis_index("c")
    return x, y, c


CHIP_FLIPS = (4, 2, 6)
CHIP_OF_SLOT = (0,) + CHIP_FLIPS


def _peer(x, y, c, k):
    px = 1 - x if k & 4 else x
    py = 1 - y if k & 2 else y
    pc = 1 - c if k & 1 else c
    return (px, py, pc), 4 * px + 2 * py + pc


class _Gather:
    def __init__(self, shapes, pad_rows):
        self.shapes, self.pad_rows, self.n = shapes, pad_rows, len(shapes)
        self.npad = sum(1 for p in pad_rows if p)

    def zeros(self):
        return jnp.zeros((max(self.pad_rows) or 16, self.shapes[0][1]), BF16)

    def out_shape(self):
        return [jax.ShapeDtypeStruct((NDEV * r + p, c), BF16) for (r, c), p in zip(self.shapes, self.pad_rows)]

    def sems(self):
        return [pltpu.SemaphoreType.DMA((self.n, NDEV - 1)), pltpu.SemaphoreType.DMA((self.n, NDEV - 1)),
                pltpu.SemaphoreType.DMA((self.n + self.npad,))]

    def _copies(self, ins, z_ref, outs, send_sems, recv_sems, loc_sems):
        x, y, c = _mesh_place()
        me = 4 * x + 2 * y + c
        sibling, _ = _peer(x, y, c, 1)
        local, first, arrive, forward = [], [], [], []
        ip = 0
        for w in range(self.n):
            r = ins[w].shape[0]
            dst = outs[w].at[pl.ds(pl.multiple_of(me * r, 16), r), :]
            local.append(functools.partial(pltpu.make_async_copy, ins[w], dst, loc_sems.at[w]))
            if self.pad_rows[w]:
                local.append(functools.partial(pltpu.make_async_copy, z_ref.at[pl.ds(0, self.pad_rows[w]), :],
                                               outs[w].at[pl.ds(NDEV * r, self.pad_rows[w]), :], loc_sems.at[self.n + ip]))
                ip += 1

            def remote(src, dst_, s, to):
                return functools.partial(pltpu.make_async_remote_copy, src_ref=src, dst_ref=dst_, send_sem=send_sems.at[w, s],
                                         recv_sem=recv_sems.at[w, s], device_id=to, device_id_type=MESH)

            for s, k in enumerate((1,) + CHIP_FLIPS):
                first.append(remote(ins[w], dst, s, _peer(x, y, c, k)[0]))
            for s, k in enumerate(CHIP_FLIPS):
                _, pidx = _peer(x, y, c, k)
                rows = outs[w].at[pl.ds(pl.multiple_of(pidx * r, 16), r), :]
                arrive.append(remote(rows, rows, 1 + s, sibling))
                forward.append(remote(rows, rows, 4 + s, sibling))
        return local, first, arrive, forward


    def start(self, *refs):
        local, first, _, _ = self._copies(*refs)
        for make in local + first:
            make().start()

    def forward(self, *refs):
        _, _, arrive, forward = self._copies(*refs)
        for a, f in zip(arrive, forward):
            a().wait_recv()
            f().start()

    def finish(self, *refs):
        local, first, _, forward = self._copies(*refs)
        for make in local + first[0::4] + forward:
            make().wait()
        for s in (1, 2, 3):
            for make in first[s::4]:
                make().wait_send()


def _all_gather(shards, pad_rows):
    n = len(shards)
    plan = _Gather([s.shape for s in shards], pad_rows)

    def body(*refs):
        args = (refs[:n], refs[n], refs[n + 1:2 * n + 1]) + tuple(refs[2 * n + 1:])
        plan.start(*args)
        plan.forward(*args)
        plan.finish(*args)

    any_spec = pl.BlockSpec(memory_space=pl.ANY)
    return pl.pallas_call(
        body, name="all_gather_weights",
        in_specs=[any_spec] * (n + 1), out_specs=[any_spec] * n,
        out_shape=plan.out_shape(),
        scratch_shapes=plan.sems(),
        compiler_params=pltpu.CompilerParams(has_side_effects=True),
    )(*shards, plan.zeros())


def _exchange_in_chip(grads, shard_rows, name):
    n = len(grads)
    plan = _InChip([g.shape for g in grads], shard_rows)

    def body(*refs):
        args = (refs[:n], refs[n:2 * n]) + tuple(refs[2 * n:])
        plan.start(*args)
        plan.finish(*args)

    any_spec = pl.BlockSpec(memory_space=pl.ANY)
    return pl.pallas_call(
        body, name=name,
        in_specs=[any_spec] * n, out_specs=[any_spec] * n, out_shape=plan.out_shape(), scratch_shapes=plan.sems(),
        compiler_params=pltpu.CompilerParams(has_side_effects=True),
    )(*grads)


class _InChip:
    def __init__(self, shapes, shard_rows):
        self.shapes, self.rows, self.n, self.ns = shapes, shard_rows, len(shapes), len(CHIP_OF_SLOT)

    def out_shape(self):
        return [jax.ShapeDtypeStruct((self.ns, r, s[1]), BF16) for s, r in zip(self.shapes, self.rows)]

    def sems(self):
        return [pltpu.SemaphoreType.DMA((self.n, self.ns)), pltpu.SemaphoreType.DMA((self.n, self.ns))]

    def _copies(self, ins, theirs, send_sems, recv_sems):
        x, y, c = _mesh_place()
        sibling, _ = _peer(x, y, c, 1)
        copies = []
        for w in range(self.n):
            r = self.rows[w]
            for s, k in enumerate(CHIP_OF_SLOT):
                _, other = _peer(x, y, c, k | 1)
                copies.append(pltpu.make_async_remote_copy(
                    src_ref=ins[w].at[pl.ds(pl.multiple_of(other * r, 16), r), :], dst_ref=theirs[w].at[s],
                    send_sem=send_sems.at[w, s], recv_sem=recv_sems.at[w, s], device_id=sibling, device_id_type=MESH))
        return copies

    def start(self, *refs):
        for cp in self._copies(*refs):
            cp.start()

    def finish(self, *refs):
        for cp in self._copies(*refs):
            cp.wait()


def _pair_sum(grad, theirs, name):
    ns, r, c = theirs.shape
    tr = r if r * c <= 2 * 1024 * 1024 else _tile(r, (256, 128, 64, 32, 16))
    per_block = r // tr

    def body(a_ref, b_ref, o_ref):
        o_ref[...] = (a_ref[...].astype(F32) + b_ref[...].astype(F32)).astype(o_ref.dtype)

    def owner_rows(s, i):
        x, y, c_ = _mesh_place()
        fx, fy = s % 2, s // 2
        px, py = x + fx - 2 * x * fx, y + fy - 2 * y * fy
        return ((4 * px + 2 * py + c_) * per_block + i, 0)

    slot = pl.BlockSpec((None, tr, c), lambda s, i: (s, i, 0))
    return pl.pallas_call(
        body, name=name, grid=(ns, per_block),
        in_specs=[pl.BlockSpec((tr, c), owner_rows), slot], out_specs=slot,
        out_shape=jax.ShapeDtypeStruct((ns, r, c), theirs.dtype),
        compiler_params=_params(("parallel", "parallel")),
    )(grad, theirs)


class _ChipExchange:
    def __init__(self, shapes):
        self.shapes, self.n, self.ns = shapes, len(shapes), len(CHIP_OF_SLOT) - 1

    def out_shape(self):
        return [jax.ShapeDtypeStruct((self.ns,) + tuple(s[1:]), BF16) for s in self.shapes]

    def sems(self):
        return [pltpu.SemaphoreType.DMA((self.n, self.ns)), pltpu.SemaphoreType.DMA((self.n, self.ns))]

    def _copies(self, ins, outs, send_sems, recv_sems):
        x, y, c = _mesh_place()
        copies = []
        for w in range(self.n):
            for s, k in enumerate(CHIP_OF_SLOT[1:]):
                peer, _ = _peer(x, y, c, k)
                copies.append(pltpu.make_async_remote_copy(
                    src_ref=ins[w].at[s + 1], dst_ref=outs[w].at[s], send_sem=send_sems.at[w, s],
                    recv_sem=recv_sems.at[w, s], device_id=peer, device_id_type=MESH))
        return copies

    def start(self, *refs):
        for cp in self._copies(*refs):
            cp.start()

    def finish(self, *refs):
        for cp in self._copies(*refs):
            cp.wait()


def _sum_chips(pair, recv, name):
    ns, r, c = recv.shape
    tr, tc = _panel(r, c)

    def body(p_ref, x_ref, o_ref):
        acc = p_ref[...].astype(F32)
        for s in range(x_ref.shape[0]):
            acc = acc + x_ref[s].astype(F32)
        o_ref[...] = acc

    return pl.pallas_call(
        body, name=name, grid=(r // tr, c // tc),
        in_specs=[pl.BlockSpec((None, tr, tc), lambda i, j: (0, i, j)), pl.BlockSpec((ns, tr, tc), lambda i, j: (0, i, j))],
        out_specs=pl.BlockSpec((tr, tc), lambda i, j: (i, j)),
        out_shape=jax.ShapeDtypeStruct((r, c), F32),
        compiler_params=_params(("parallel", "parallel")),
    )(pair, recv)


def _panel(r, c):
    for tr in (1024, 512, 256, 128):
        if r % tr == 0 and tr * c <= 512 * 1024:
            return tr, c
    for tc in (2048, 1024, 512, 256, 128):
        if c % tc == 0 and r * tc <= 512 * 1024:
            return r, tc
    return _tile(r, (64, 32, 16, 8)), c


def _all_reduce_small(part):
    R, W = part.shape

    def body(x_ref, o_ref, buf, send_sems, recv_sems):
        x, y, c = _mesh_place()
        me = 4 * x + 2 * y + c
        buf[me] = x_ref[...]
        copies = []
        for k in range(1, NDEV):
            peer, _ = _peer(x, y, c, k)
            cp = pltpu.make_async_remote_copy(src_ref=x_ref, dst_ref=buf.at[me], send_sem=send_sems.at[k - 1],
                                              recv_sem=recv_sems.at[k - 1], device_id=peer, device_id_type=MESH)
            cp.start()
            copies.append(cp)
        for cp in copies:
            cp.wait()
        acc = buf[0]
        for d in range(1, NDEV):
            acc = acc + buf[d]
        o_ref[...] = acc

    vm = pl.BlockSpec(memory_space=pltpu.VMEM)
    return pl.pallas_call(
        body, name="all_reduce_small", in_specs=[vm], out_specs=vm,
        out_shape=jax.ShapeDtypeStruct((R, W), F32),
        scratch_shapes=[pltpu.VMEM((NDEV, R, W), F32), pltpu.SemaphoreType.DMA((NDEV - 1,)),
                        pltpu.SemaphoreType.DMA((NDEV - 1,))],
        compiler_params=pltpu.CompilerParams(has_side_effects=True),
    )(part)


def _adam_math(w, g, m, v):
    m2 = ADAM_B1 * m + (1.0 - ADAM_B1) * g
    v2 = ADAM_B2 * v + (1.0 - ADAM_B2) * (g * g)
    m_hat = m2 / (1.0 - ADAM_B1 ** ADAM_STEP)
    v_hat = v2 / (1.0 - ADAM_B2 ** ADAM_STEP)
    delta = -ADAM_LR * (m_hat / (jnp.sqrt(v_hat) + ADAM_EPS) + ADAM_WD * w)
    return delta, m2, v2


def _adamw(w, g, m, v, name):
    r, c = w.shape
    tr, tc = _panel(r, c)

    def body(w_ref, g_ref, m_ref, v_ref, d_ref, m2_ref, v2_ref):
        d_ref[...], m2_ref[...], v2_ref[...] = _adam_math(w_ref[...], g_ref[...], m_ref[...], v_ref[...])

    spec = pl.BlockSpec((tr, tc), lambda i, j: (i, j))
    return pl.pallas_call(
        body, name=name, grid=(r // tr, c // tc), in_specs=[spec] * 4, out_specs=[spec] * 3,
        out_shape=[jax.ShapeDtypeStruct((r, c), F32)] * 3,
        compiler_params=_params(("parallel", "parallel")),
    )(w, g, m, v)


GAINS = ("ffn1_pre", "ffn1_post", "mix_pre", "mix_post", "mem_norm", "ffn2_pre", "ffn2_post")
GAIN_ROWS = D // HD
ROW_LB = len(GAINS) * GAIN_ROWS
ROWS_GRAD_IN = ROW_LB + 24
ROWS_PACKED = ROW_LB + 32


def _small_update(gsum, w_p, m_p, v_p):
    def body(g_ref, w_ref, m_ref, v_ref, go_ref, d_ref, m2_ref, v2_ref):
        a0 = w_ref[ROW_LB:ROW_LB + 8, :]
        a1 = w_ref[ROW_LB + 8:ROW_LB + 16, :]
        mx = jnp.maximum(a0, a1)
        e0, e1 = jnp.exp(a0 - mx), jnp.exp(a1 - mx)
        lb = e0 / (e0 + e1)
        da0 = g_ref[ROW_LB:ROW_LB + 8, :] * lb * (1.0 - lb)
        g = jnp.concatenate([g_ref[0:ROW_LB, :], da0, -da0, g_ref[ROW_LB + 8:ROWS_GRAD_IN, :]], axis=0)
        go_ref[...] = g
        d_ref[...], m2_ref[...], v2_ref[...] = _adam_math(w_ref[...], g, m_ref[...], v_ref[...])

    vm = pl.BlockSpec(memory_space=pltpu.VMEM)
    return pl.pallas_call(
        body, name="small_update", in_specs=[vm] * 4, out_specs=[vm] * 4,
        out_shape=[jax.ShapeDtypeStruct((ROWS_PACKED, HD), F32)] * 4,
    )(gsum, w_p, m_p, v_p)


def _rows8(a):
    a = a.reshape(-1)
    rows = -(-a.shape[0] // HD)
    rows8 = -(-rows // 8) * 8
    return jnp.pad(a, (0, rows8 * HD - a.shape[0])).reshape(rows8, HD)


def _pack_small(gains, lb0, lb1, gnorm, fb):
    return jnp.concatenate([_rows8(g) for g in gains] + [_rows8(lb0), _rows8(lb1), _rows8(gnorm), _rows8(fb)], axis=0)


def _unpack_small(p):
    out = {}
    for i, name in enumerate(GAINS):
        out[name] = p[i * GAIN_ROWS:(i + 1) * GAIN_ROWS].reshape(1, D)
    lb0 = p[ROW_LB:ROW_LB + NH].reshape(1, WH)
    lb1 = p[ROW_LB + 8:ROW_LB + 8 + NH].reshape(1, WH)
    out["hgrn_lb"] = jnp.concatenate([lb0, lb1], axis=0)
    out["hgrn_gnorm"] = p[ROW_LB + 16:ROW_LB + 16 + NH].reshape(1, WH)
    out["fox_fb"] = p[ROW_LB + 24:ROW_LB + 25, 0:NH]
    return out


def _ffn_forward(n, wg_t, wu_t, wd, tag, rider=None, rider_down=None):
    g, u, a, *carried = _ffn_up(n, wg_t, wu_t, f"{tag}_up", rider)
    if wd is None:
        wd = carried[0]
    if rider_down is None:
        h = _mm(a, wd, "nn", F32, f"{tag}_down")
    else:
        h, *more = _mm(a, wd, "nn", F32, f"{tag}_down", rider=rider_down)
        carried = carried + more
    return h, (n, g, u, a), carried


def _mm_out(res):
    return (res[0], list(res[1:])) if isinstance(res, (list, tuple)) else (res, [])


def _ffn_backward(dh, saved, wg_t, wu_t, wd, tag, rider=None, exchange=None, rider_dwd=None, after_dwd=None):
    n, g, u, a = saved
    dwd, got = _mm_out(_mm(a, dh, "tn", BF16, f"{tag}_dwd", rider=rider_dwd))
    rider_dwg = None
    if after_dwd is not None:
        rider, rider_dwg = after_dwd(got)
    dg, du, *carried = _ffn_act_bwd(dh, wd, g, u, f"{tag}_act_bwd", rider)
    dwg, got = _mm_out(_mm(dg, n, "tn", BF16, f"{tag}_dwg", rider=rider_dwg))
    carried = carried + got
    dwu = _mm(du, n, "tn", BF16, f"{tag}_dwu")
    if exchange is None:
        dn = _mm(dg, wg_t, "nn", F32, f"{tag}_dn_g")
        dn = _mm(du, wu_t, "nn", F32, f"{tag}_dn_u", add=dn)
    else:
        ride_a, ride_b, take = exchange(dwg, dwu, dwd)
        dn, got_a = _mm_out(_mm(dg, wg_t, "nn", F32, f"{tag}_dn_g", rider=ride_a))
        dn, got_b = _mm_out(_mm(du, wu_t, "nn", F32, f"{tag}_dn_u", add=dn, rider=ride_b))
        take(got_a, got_b)
    return dn, (dwg, dwu, dwd), carried


GATHER_FIRST = ("ffn1_wg", "ffn1_wu")
GATHER_IN_FFN1_UP = ("ffn1_wd", "w_in")
GATHER_IN_FFN1_DOWN = ("w_gate",)
GATHER_IN_PROJ = ("w_mem_kv", "w_hgrn_out", "w_fox_out", "w_mem_out", "w_o")
GATHER_IN_GATE = ("ffn2_wg",)
GATHER_IN_HGRN = ("ffn2_wu",)
GATHER_IN_FFN2_UP = ("ffn2_wd",)
GROUP_FFN1 = ("ffn1_wg", "ffn1_wu", "ffn1_wd")
GROUP_MIX = ("w_in", "w_mem_kv", "w_hgrn_out", "w_fox_out", "w_mem_out", "w_gate", "w_o")
GROUP_FFN2 = ("ffn2_wg", "ffn2_wu", "ffn2_wd")


def _gather_rider(blocks, names):
    plan = _Gather([blocks[n].shape for n in names], [FFN_PAD.get(n, 0) for n in names])
    return _Rider(plan, [blocks[n] for n in names] + [plan.zeros()], GATHER_STEPS)


def _local_step(x, mem, tgt, small, wts=None, blocks=None):
    T = x.shape[0]
    tr = _tile(T, (256, 128))
    fb_pad = jnp.pad(small["fox_fb"], ((0, 0), (0, HD - NH)))
    dist = blocks is not None
    if dist:
        wts = dict(zip(GATHER_FIRST, _all_gather([blocks[n] for n in GATHER_FIRST], [FFN_PAD[n] for n in GATHER_FIRST])))

    def riding(names):
        return _gather_rider(blocks, names) if dist else None

    (n1,) = _rowwise(_norm_fn, [(x, 0)], [(small["ffn1_pre"], None)], [BF16], "ffn1_pre", tr, D, 1)
    h1, ffn1_saved, carried = _ffn_forward(n1, wts["ffn1_wg"], wts["ffn1_wu"], wts.get("ffn1_wd"), "ffn1",
                                           riding(GATHER_IN_FFN1_UP), riding(GATHER_IN_FFN1_DOWN))
    wts.update(zip(GATHER_IN_FFN1_UP + GATHER_IN_FFN1_DOWN, carried))
    ffn1_out = functools.partial(_post_pre_fn, 0.5)
    x1, un = _rowwise(ffn1_out, [(x, 0), (h1, 0)], [(small["ffn1_post"], None), (small["mix_pre"], None)], [F32, BF16],
                      "ffn1_post_mix_pre", tr, D, 1)
    if dist:
        proj, *carried = _mm(un, wts["w_in"], "nn", F32, "proj", rider=riding(GATHER_IN_PROJ))
        wts.update(zip(GATHER_IN_PROJ, carried))
        z, *carried = _mm(un, wts["w_gate"], "nt", BF16, "gate_logits", rider=riding(GATHER_IN_GATE))
        wts.update(zip(GATHER_IN_GATE, carried))
    else:
        proj = _mm(un, wts["w_in"], "nn", F32, "proj")
        z = _mm(un, wts["w_gate"], "nt", BF16, "gate_logits")
    (memn,) = _rowwise(_norm_fn, [(mem, 0)], [(small["mem_norm"], None)], [BF16], "mem_norm", mem.shape[0], D, 1)
    mem_kv = _mm(memn, wts["w_mem_kv"], "nn", F32, "mem_kv")

    o_raw, states, *carried = _hgrn_fwd(proj, small["hgrn_lb"], riding(GATHER_IN_HGRN))
    wts.update(zip(GATHER_IN_HGRN, carried))
    tr_head = _tile(T, (1024, 512, 256, 128))
    (o_h,) = _rowwise(_hpost_fn, [(o_raw, 0), (proj, CB_HOG)], [(small["hgrn_gnorm"], 0)], [BF16], "hgrn_post",
                      tr_head, HD, NH)
    ct, cq = _fox_cum(proj, fb_pad)
    win = _fox_windows(proj, cq)
    o_f, lse = _fox_fwd(win, proj, ct, cq)
    o_m = _mem_fwd(proj, mem_kv)

    yh = _mm(o_h, wts["w_hgrn_out"], "nt", BF16, "hgrn_out")
    yf = _mm(o_f, wts["w_fox_out"], "nt", BF16, "fox_out")
    ym = _mm(o_m, wts["w_mem_out"], "nt", BF16, "mem_out")
    zc = D // 512
    merge_rows = [(z, 0), (z, zc), (z, 2 * zc), (yh, 0), (yf, 0), (ym, 0)]
    tr_merge = _tile(T, (512, 256, 128))
    (merged,) = _rowwise(_merge_fn, merge_rows, [], [BF16], "merge", tr_merge, 512, zc)
    m = _mm(merged, wts["w_o"], "nn", F32, "mix_out")
    mix_out = functools.partial(_post_pre_fn, 1.0)
    x2, n2 = _rowwise(mix_out, [(x1, 0), (m, 0)], [(small["mix_post"], None), (small["ffn2_pre"], None)], [F32, BF16],
                      "mix_post_ffn2_pre", tr, D, 1)
    h2, ffn2_saved, carried = _ffn_forward(n2, wts["ffn2_wg"], wts["ffn2_wu"], wts.get("ffn2_wd"), "ffn2",
                                           riding(GATHER_IN_FFN2_UP))
    wts.update(zip(GATHER_IN_FFN2_UP, carried))
    dy, loss_part = _loss(x2, h2, small["ffn2_post"], tgt, "loss")

    gw, gs, reduced = {}, {}, {}

    def pair_sums(names, tag):
        if not dist:
            return None, None
        theirs = brought.get(tag)
        if theirs is None:
            theirs = _exchange_in_chip([gw[n] for n in names], [blocks[n].shape[0] for n in names], f"reduce_in_chip_{tag}")
        pairs = [_pair_sum(gw[n], t_, f"pair_{n}") for n, t_ in zip(names, theirs)]
        return pairs, _Rider(_ChipExchange([p.shape for p in pairs]), pairs, EXCHANGE_STEPS)

    def chip_sums(names, pairs, recv):
        for n, p_, r_ in zip(names, pairs or (), recv):
            reduced[n] = _sum_chips(p_, r_, f"sum_{n}")

    brought = {}

    def in_chip_rider(names):
        grads_ = [gw[n] for n in names]
        return _Rider(_InChip([g_.shape for g_ in grads_], [blocks[n].shape[0] for n in names]), grads_, EXCHANGE_STEPS)

    def ffn2_exchange(dwg, dwu, dwd):
        gw.update(ffn2_wg=dwg, ffn2_wu=dwu, ffn2_wd=dwd)
        return in_chip_rider(GROUP_FFN2), None, lambda got_a, got_b: brought.update(ffn2=got_a)

    dh2, gs["ffn2_post"] = _rowwise_bwd(functools.partial(_resid_h_fn, 0.5), [(h2, 0)], [(small["ffn2_post"], None)],
                                        [(dy, 0)], [0], [BF16], "ffn2_post_bwd", tr, D, 1)
    dn2, (gw["ffn2_wg"], gw["ffn2_wu"], gw["ffn2_wd"]), _ = _ffn_backward(
        dh2, ffn2_saved, wts["ffn2_wg"], wts["ffn2_wu"], wts["ffn2_wd"], "ffn2", exchange=ffn2_exchange if dist else None)
    pairs_ffn2, ride_ffn2_grads = pair_sums(GROUP_FFN2, "ffn2")

    dx1, dm, gs["mix_post"], gs["ffn2_pre"] = _rowwise_bwd(
        mix_out, [(x1, 0), (m, 0)], [(small["mix_post"], None), (small["ffn2_pre"], None)], [(dy, 0), (dn2, 0)], [0, 1],
        [F32, BF16], "mix_post_ffn2_pre_bwd", tr, D, 1)
    dmerged = _mm(dm, wts["w_o"], "nt", F32, "d_merged")
    gw["w_o"] = _mm(merged, dm, "tn", BF16, "d_w_o")
    dz0, dz1, dz2, dyh, dyf, dym = _rowwise_bwd(_merge_fn, merge_rows, [], [(dmerged, 0)], [0, 1, 2, 3, 4, 5], [BF16] * 6,
                                                "merge_bwd", tr_merge, 512, zc)
    dz = jnp.concatenate([dz0, dz1, dz2], axis=1)
    gw["w_gate"] = _mm(dz, un, "tn", BF16, "d_w_gate")
    dun = _mm(dz, wts["w_gate"], "nn", F32, "d_un_gate")

    do_h = _mm(dyh, wts["w_hgrn_out"], "nn", F32, "d_o_h")
    gw["w_hgrn_out"] = _mm(dyh, o_h, "tn", BF16, "d_w_hgrn_out")
    do_f = _mm(dyf, wts["w_fox_out"], "nn", F32, "d_o_f")
    gw["w_fox_out"] = _mm(dyf, o_f, "tn", BF16, "d_w_fox_out")
    do_m = _mm(dym, wts["w_mem_out"], "nn", F32, "d_o_m")
    gw["w_mem_out"] = _mm(dym, o_m, "tn", BF16, "d_w_mem_out")

    do_raw, dhog, gs["hgrn_gnorm"] = _rowwise_bwd(_hpost_fn, [(o_raw, 0), (proj, CB_HOG)], [(small["hgrn_gnorm"], 0)],
                                                  [(do_h, 0)], [0, 1], [F32, BF16], "hgrn_post_bwd", tr_head, HD, NH)
    dhq, dhf, dhi, gs["hgrn_lb"], *carried = _hgrn_bwd(proj, small["hgrn_lb"], states, do_raw, ride_ffn2_grads)
    chip_sums(GROUP_FFN2, pairs_ffn2, carried)
    dfq, delta = _fox_bwd_dq(win, proj, ct, cq, lse, do_f)
    dfk, dfv, dc = _fox_bwd_dkv(win, proj, ct, cq, lse, delta, do_f)
    dff, dfb = _fox_cum_bwd(dc, proj, fb_pad)
    gs["fox_fb"] = dfb
    dmq, dmk, dmv = _mem_bwd(proj, mem_kv, do_m)

    dproj = jnp.concatenate([dhq, dhf, dhi, dhog, dfq, dfk, dfv, dff, dmq, jnp.zeros((T, HD), BF16)], axis=1)
    gw["w_in"] = _mm(un, dproj, "tn", BF16, "d_w_in")
    dun = _mm(dproj, wts["w_in"], "nt", F32, "d_un_proj", add=dun)
    dx0, dh1, gs["ffn1_post"], gs["mix_pre"] = _rowwise_bwd(
        ffn1_out, [(x, 0), (h1, 0)], [(small["ffn1_post"], None), (small["mix_pre"], None)], [(dx1, 0), (dun, 0)], [0, 1],
        [F32, BF16], "ffn1_post_mix_pre_bwd", tr, D, 1)

    dmem_kv = jnp.concatenate([dmk, dmv], axis=1)
    gw["w_mem_kv"] = _mm(memn, dmem_kv, "tn", BF16, "d_w_mem_kv")
    dmemn = _mm(dmem_kv, wts["w_mem_kv"], "nt", F32, "d_memn")
    _, gs["mem_norm"] = _rowwise_bwd(_norm_fn, [(mem, 0)], [(small["mem_norm"], None)], [(dmemn, 0)], [0], [BF16],
                                     "mem_norm_bwd", mem.shape[0], D, 1)

    mix = {}

    def mix_after_dwd(got):
        brought.update(mix=got)
        pairs, _ = pair_sums(GROUP_MIX, "mix")
        mix["names"] = GROUP_MIX[1:] + GROUP_MIX[:1]
        mix["pairs"] = pairs[1:] + pairs[:1]
        return tuple(_Rider(_ChipExchange([p.shape for p in part]), part, EXCHANGE_STEPS) for part in (pairs[1:], pairs[:1]))

    def own_exchange(dwg, dwu, dwd):
        gw.update(ffn1_wg=dwg, ffn1_wu=dwu, ffn1_wd=dwd)
        pairs, _ = pair_sums(GROUP_FFN1, "ffn1")
        first, second = pairs[:2], pairs[2:]

        def take(got_a, got_b):
            chip_sums(GROUP_FFN1, pairs, list(got_a) + list(got_b))

        return (_Rider(_ChipExchange([p.shape for p in first]), first, EXCHANGE_STEPS),
                _Rider(_ChipExchange([p.shape for p in second]), second, EXCHANGE_STEPS), take)

    dn1, (gw["ffn1_wg"], gw["ffn1_wu"], gw["ffn1_wd"]), carried = _ffn_backward(
        dh1, ffn1_saved, wts["ffn1_wg"], wts["ffn1_wu"], wts["ffn1_wd"], "ffn1",
        exchange=own_exchange if dist else None, rider_dwd=in_chip_rider(GROUP_MIX) if dist else None,
        after_dwd=mix_after_dwd if dist else None)
    chip_sums(mix.get("names", ()), mix.get("pairs"), carried)
    dx, gs["ffn1_pre"] = _rowwise_bwd(_norm_res_fn, [(x, 0)], [(small["ffn1_pre"], None)], [(dx0, 0), (dn1, 0)], [0], [F32],
                                      "ffn1_pre_bwd", tr, D, 1)
    return loss_part, dx, (reduced if dist else gw), gs


BIG = ("ffn1_wg", "ffn1_wu", "ffn1_wd", "w_in", "w_mem_kv", "w_hgrn_out", "w_fox_out", "w_mem_out", "w_gate", "w_o",
       "ffn2_wg", "ffn2_wu", "ffn2_wd")
TRANSPOSED = ("ffn1_wg", "ffn1_wu", "ffn2_wg", "ffn2_wu", "w_hgrn_out", "w_fox_out", "w_mem_out", "w_gate")
FFN_PAD = {"ffn1_wg": FP - F, "ffn1_wu": FP - F, "ffn1_wd": FP - F, "ffn2_wg": FP - F, "ffn2_wu": FP - F,
           "ffn2_wd": FP - F}
SMALL = GAINS + ("hgrn_lb", "hgrn_gnorm", "fox_fb")
WEIGHTS = ("ffn1_pre", "ffn1_post", "ffn1_wg", "ffn1_wu", "ffn1_wd", "mix_pre", "mix_post", "mem_norm", "w_in", "hgrn_lb",
           "hgrn_gnorm", "fox_fb", "w_mem_kv", "w_hgrn_out", "w_fox_out", "w_mem_out", "w_gate", "w_o", "ffn2_pre",
           "ffn2_post", "ffn2_wg", "ffn2_wu", "ffn2_wd")


def _to_gather_layout(name, w):
    if name in TRANSPOSED:
        w = w.T
    if name == "w_in":
        r = w.shape[0]
        w = jnp.concatenate([w[:, :MQ_COL], jnp.zeros((r, FF_COL + HD - MQ_COL), w.dtype), w[:, MQ_COL:],
                             jnp.zeros((r, P - FF_COL - HD - WM), w.dtype)], axis=1)
    return w.astype(BF16)


def _from_gather_layout(name, g):
    if name == "w_in":
        g = jnp.concatenate([g[:, :MQ_COL], g[:, FF_COL + HD:FF_COL + HD + WM]], axis=1)
    if name in TRANSPOSED:
        g = g.T
    return g


def kernel(x, mem, ffn1_pre, ffn1_post, ffn1_wg, ffn1_wu, ffn1_wd, mix_pre, mix_post, mem_norm, w_in, hgrn_lb, hgrn_gnorm, fox_fb, w_mem_kv, w_hgrn_out, w_fox_out, w_mem_out, w_gate, w_o, ffn2_pre, ffn2_post, ffn2_wg, ffn2_wu, ffn2_wd, loss_target, m_ffn1_pre, m_ffn1_post, m_ffn1_wg, m_ffn1_wu, m_ffn1_wd, m_mix_pre, m_mix_post, m_mem_norm, m_w_in, m_hgrn_lb, m_hgrn_gnorm, m_fox_fb, m_w_mem_kv, m_w_hgrn_out, m_w_fox_out, m_w_mem_out, m_w_gate, m_w_o, m_ffn2_pre, m_ffn2_post, m_ffn2_wg, m_ffn2_wu, m_ffn2_wd, v_ffn1_pre, v_ffn1_post, v_ffn1_wg, v_ffn1_wu, v_ffn1_wd, v_mix_pre, v_mix_post, v_mem_norm, v_w_in, v_hgrn_lb, v_hgrn_gnorm, v_fox_fb, v_w_mem_kv, v_w_hgrn_out, v_w_fox_out, v_w_mem_out, v_w_gate, v_w_o, v_ffn2_pre, v_ffn2_post, v_ffn2_wg, v_ffn2_wu, v_ffn2_wd):
    a = dict(locals())
    small = {n: a[n] for n in SMALL}
    shard = {n: a[n][0] if a[n].ndim == 3 else a[n] for n in BIG}

    blocks = {n: _to_gather_layout(n, shard[n]) for n in BIG}
    loss_part, dx, reduced, gs = _local_step(x[0], mem[0], loss_target[0], small, blocks=blocks)
    loss = lax.psum(0.5 / D * jnp.sum(loss_part), ("x", "y", "c"))

    grads, deltas, new_m, new_v = {}, {}, {}, {}
    for n in BIG:
        g = _from_gather_layout(n, reduced[n])
        d, m2, v2 = _adamw(shard[n], g, a["m_" + n].reshape(g.shape), a["v_" + n].reshape(g.shape), f"adamw_{n}")
        full = a[n].shape
        grads[n], deltas[n], new_m[n], new_v[n] = g.reshape(full), d.reshape(full), m2.reshape(full), v2.reshape(full)

    part = jnp.concatenate([_rows8(gs[n]) for n in GAINS] + [_rows8(gs["hgrn_lb"]), _rows8(gs["hgrn_gnorm"]),
                                                             _rows8(gs["fox_fb"][:, :NH])], axis=0)
    gsum = _all_reduce_small(part)

    def packed(prefix):
        lb = a[prefix + "hgrn_lb"]
        return _pack_small([a[prefix + n] for n in GAINS], lb[0], lb[1], a[prefix + "hgrn_gnorm"], a[prefix + "fox_fb"])

    g_p, d_p, m_p, v_p = _small_update(gsum, packed(""), packed("m_"), packed("v_"))
    for dst, p in ((grads, g_p), (deltas, d_p), (new_m, m_p), (new_v, v_p)):
        dst.update(_unpack_small(p))

    return (loss, dx[None], *[grads[n] for n in WEIGHTS], *[deltas[n] for n in WEIGHTS],
            *[new_m[n] for n in WEIGHTS], *[new_v[n] for n in WEIGHTS])
```
